```python
import jax, jax.numpy as jnp
from jax import lax
import numpy as np

D_MODEL = 1024
BATCH = 16
SEQ = 2048
DEPTH = 4

CHUNK = 64
D_CONV = 512
CONV_HEADS = 8
D_POOL = 512
POOL_WINDOWS = (2, 4, 8, 16)
N_POOL = len(POOL_WINDOWS)
POOL_GROUP = D_POOL // N_POOL
D_MIX = D_CONV + D_POOL
D_IN = 2 * D_CONV + D_POOL
CONV_K = 31
D_FF = 2816
FFN_CONV_K = 3
N_MOD = 6
EPS = 1e-6

kernel_name = "hybrid_conformer_pool_streaming_trunk"


def rms_norm(x, g):
    xf = x.astype(jnp.float32)
    y = xf * lax.rsqrt(jnp.mean(xf * xf, axis=-1, keepdims=True) + EPS)
    return (y * g.astype(jnp.float32)).astype(x.dtype)


def layer_norm(x, g, b):
    xf = x.astype(jnp.float32)
    mu = jnp.mean(xf, axis=-1, keepdims=True)
    xc = xf - mu
    var = jnp.mean(xc * xc, axis=-1, keepdims=True)
    y = xc * lax.rsqrt(var + EPS)
    return (y * g.astype(jnp.float32) + b.astype(jnp.float32)).astype(x.dtype)


def causal_dwconv(x, w, b):
    k = w.shape[0]
    ch = x.shape[-1]
    y = lax.conv_general_dilated(
        x, w[:, None, :].astype(x.dtype), window_strides=(1,), padding=[(k - 1, 0)],
        dimension_numbers=("NWC", "WIO", "NWC"), feature_group_count=ch)
    return y + b.astype(x.dtype)


def conformer_conv_mixer(u_val, u_gate, conv_w, conv_b, ln_g, ln_b):
    a = u_val * jax.nn.sigmoid(u_gate)
    a = causal_dwconv(a, conv_w, conv_b)
    a = layer_norm(a, ln_g, ln_b)
    return jax.nn.silu(a)


def multiscale_pool_mixer(h, pool_w, pool_scale):
    s = h.shape[1]
    hf = h.astype(jnp.float32)
    cs = lax.cumsum(hf, axis=1)
    t = jnp.arange(s)
    outs = []
    for g, w in enumerate(POOL_WINDOWS):
        sl = slice(g * POOL_GROUP, (g + 1) * POOL_GROUP)
        csg = cs[..., sl]
        prev = jnp.pad(csg, ((0, 0), (w, 0), (0, 0)))[:, :s]
        cnt = jnp.minimum(t + 1, w).astype(jnp.float32)[None, :, None]
        d = ((csg - prev) / cnt - hf[..., sl]).astype(h.dtype)
        outs.append(jnp.einsum("bsc,cd->bsd", d, pool_w[g]))
    return jnp.concatenate(outs, axis=-1) * pool_scale


def _fwd_setup_inputs(seed: int = 0) -> dict:
    key = jax.random.key(seed)
    ks = jax.random.split(key, 24)
    f32 = jnp.float32

    def nrm(k, shape, scale):
        return jax.random.normal(k, shape, f32) * scale

    L = DEPTH
    return {
        "x": nrm(ks[0], (BATCH, SEQ, D_MODEL), 1.0),
        "c": nrm(ks[1], (BATCH, D_MODEL), 1.0),
        "ada_w": nrm(ks[2], (L, D_MODEL, N_MOD * D_MODEL), 0.1 * D_MODEL ** -0.5),
        "ada_b": nrm(ks[3], (L, N_MOD * D_MODEL), 0.01),
        "pre_mix_g": 1.0 + nrm(ks[4], (L, D_MODEL), 0.05),
        "post_mix_g": 1.0 + nrm(ks[5], (L, D_MODEL), 0.05),
        "w_in": nrm(ks[6], (L, D_MODEL, D_IN), D_MODEL ** -0.5),
        "conv_w": nrm(ks[7], (L, CONV_K, D_CONV), CONV_K ** -0.5),
        "conv_b": nrm(ks[8], (L, D_CONV), 0.01),
        "conv_ln_g": 1.0 + nrm(ks[9], (L, D_CONV), 0.05),
        "conv_ln_b": nrm(ks[10], (L, D_CONV), 0.01),
        "pool_w": nrm(ks[11], (L, N_POOL, POOL_GROUP, POOL_GROUP), POOL_GROUP ** -0.5),
        "pool_scale": 1.0 + nrm(ks[12], (L, D_POOL), 0.1),
        "w_out": nrm(ks[13], (L, D_MIX, D_MODEL), D_MIX ** -0.5),
        "pre_ffn_g": 1.0 + nrm(ks[14], (L, D_MODEL), 0.05),
        "post_ffn_g": 1.0 + nrm(ks[15], (L, D_MODEL), 0.05),
        "ffn_up": nrm(ks[16], (L, D_MODEL, 2 * D_FF), D_MODEL ** -0.5),
        "ffn_conv_w": nrm(ks[17], (L, FFN_CONV_K, 2 * D_FF), FFN_CONV_K ** -0.5),
        "ffn_conv_b": nrm(ks[18], (L, 2 * D_FF), 0.01),
        "ffn_down": nrm(ks[19], (L, D_FF, D_MODEL), D_FF ** -0.5),
    }


def _fwd_reference(x, c, ada_w, ada_b, pre_mix_g, post_mix_g, w_in, conv_w, conv_b, conv_ln_g,
              conv_ln_b, pool_w, pool_scale, w_out, pre_ffn_g, post_ffn_g, ffn_up,
              ffn_conv_w, ffn_conv_b, ffn_down):
    c_act = jax.nn.silu(c)
    for l in range(DEPTH):
        mod = c_act @ ada_w[l] + ada_b[l]
        sh1, sc1, gt1, sh2, sc2, gt2 = [m[:, None, :] for m in jnp.split(mod, N_MOD, axis=-1)]

        h = rms_norm(x, pre_mix_g[l]) * (1.0 + sc1) + sh1
        u = jnp.einsum("bsd,de->bse", h, w_in[l])
        a = conformer_conv_mixer(u[..., :D_CONV], u[..., D_CONV:2 * D_CONV],
                                 conv_w[l], conv_b[l], conv_ln_g[l], conv_ln_b[l])
        p = multiscale_pool_mixer(u[..., 2 * D_CONV:], pool_w[l], pool_scale[l])
        o = jnp.einsum("bsm,md->bsd", jnp.concatenate([a, p], axis=-1), w_out[l])
        x = x + (1.0 + gt1) * rms_norm(o, post_mix_g[l])

        h = rms_norm(x, pre_ffn_g[l]) * (1.0 + sc2) + sh2
        u = jnp.einsum("bsd,df->bsf", h, ffn_up[l])
        u = causal_dwconv(u, ffn_conv_w[l], ffn_conv_b[l])
        hid = jax.nn.silu(u[..., D_FF:]) * u[..., :D_FF]
        o = jnp.einsum("bsf,fd->bsd", hid, ffn_down[l])
        x = x + (1.0 + gt2) * rms_norm(o, post_ffn_g[l])
    return x


import jax as _jax
import jax.numpy as _jnp

TWIN_FORMAT = 'train_step'
FWD_PARAMS = ['x', 'c', 'ada_w', 'ada_b', 'pre_mix_g', 'post_mix_g', 'w_in', 'conv_w', 'conv_b', 'conv_ln_g', 'conv_ln_b', 'pool_w', 'pool_scale', 'w_out', 'pre_ffn_g', 'post_ffn_g', 'ffn_up', 'ffn_conv_w', 'ffn_conv_b', 'ffn_down']
TWIN_WEIGHTS = ['ada_w', 'ada_b', 'pre_mix_g', 'post_mix_g', 'w_in', 'conv_w', 'conv_b', 'conv_ln_g', 'conv_ln_b', 'pool_w', 'pool_scale', 'w_out', 'pre_ffn_g', 'post_ffn_g', 'ffn_up', 'ffn_conv_w', 'ffn_conv_b', 'ffn_down']
TWIN_DIFF_INPUT = 'x'
TWIN_INPUTS = ['x', 'c', 'ada_w', 'ada_b', 'pre_mix_g', 'post_mix_g', 'w_in', 'conv_w', 'conv_b', 'conv_ln_g', 'conv_ln_b', 'pool_w', 'pool_scale', 'w_out', 'pre_ffn_g', 'post_ffn_g', 'ffn_up', 'ffn_conv_w', 'ffn_conv_b', 'ffn_down', 'loss_target', 'm_ada_w', 'm_ada_b', 'm_pre_mix_g', 'm_post_mix_g', 'm_w_in', 'm_conv_w', 'm_conv_b', 'm_conv_ln_g', 'm_conv_ln_b', 'm_pool_w', 'm_pool_scale', 'm_w_out', 'm_pre_ffn_g', 'm_post_ffn_g', 'm_ffn_up', 'm_ffn_conv_w', 'm_ffn_conv_b', 'm_ffn_down', 'v_ada_w', 'v_ada_b', 'v_pre_mix_g', 'v_post_mix_g', 'v_w_in', 'v_conv_w', 'v_conv_b', 'v_conv_ln_g', 'v_conv_ln_b', 'v_pool_w', 'v_pool_scale', 'v_w_out', 'v_pre_ffn_g', 'v_post_ffn_g', 'v_ffn_up', 'v_ffn_conv_w', 'v_ffn_conv_b', 'v_ffn_down']
TWIN_OUTPUTS = ['loss', 'grad_x', 'grad_ada_w', 'grad_ada_b', 'grad_pre_mix_g', 'grad_post_mix_g', 'grad_w_in', 'grad_conv_w', 'grad_conv_b', 'grad_conv_ln_g', 'grad_conv_ln_b', 'grad_pool_w', 'grad_pool_scale', 'grad_w_out', 'grad_pre_ffn_g', 'grad_post_ffn_g', 'grad_ffn_up', 'grad_ffn_conv_w', 'grad_ffn_conv_b', 'grad_ffn_down', 'delta_ada_w', 'delta_ada_b', 'delta_pre_mix_g', 'delta_post_mix_g', 'delta_w_in', 'delta_conv_w', 'delta_conv_b', 'delta_conv_ln_g', 'delta_conv_ln_b', 'delta_pool_w', 'delta_pool_scale', 'delta_w_out', 'delta_pre_ffn_g', 'delta_post_ffn_g', 'delta_ffn_up', 'delta_ffn_conv_w', 'delta_ffn_conv_b', 'delta_ffn_down', 'new_m_ada_w', 'new_m_ada_b', 'new_m_pre_mix_g', 'new_m_post_mix_g', 'new_m_w_in', 'new_m_conv_w', 'new_m_conv_b', 'new_m_conv_ln_g', 'new_m_conv_ln_b', 'new_m_pool_w', 'new_m_pool_scale', 'new_m_w_out', 'new_m_pre_ffn_g', 'new_m_post_ffn_g', 'new_m_ffn_up', 'new_m_ffn_conv_w', 'new_m_ffn_conv_b', 'new_m_ffn_down', 'new_v_ada_w', 'new_v_ada_b', 'new_v_pre_mix_g', 'new_v_post_mix_g', 'new_v_w_in', 'new_v_conv_w', 'new_v_conv_b', 'new_v_conv_ln_g', 'new_v_conv_ln_b', 'new_v_pool_w', 'new_v_pool_scale', 'new_v_w_out', 'new_v_pre_ffn_g', 'new_v_post_ffn_g', 'new_v_ffn_up', 'new_v_ffn_conv_w', 'new_v_ffn_conv_b', 'new_v_ffn_down']
TWIN_LEAF_KINDS = {'loss': 'loss', 'grad_x': 'grad_x', 'grad_ada_w': 'grad_w', 'grad_ada_b': 'grad_w', 'grad_pre_mix_g': 'grad_w', 'grad_post_mix_g': 'grad_w', 'grad_w_in': 'grad_w', 'grad_conv_w': 'grad_w', 'grad_conv_b': 'grad_w', 'grad_conv_ln_g': 'grad_w', 'grad_conv_ln_b': 'grad_w', 'grad_pool_w': 'grad_w', 'grad_pool_scale': 'grad_w', 'grad_w_out': 'grad_w', 'grad_pre_ffn_g': 'grad_w', 'grad_post_ffn_g': 'grad_w', 'grad_ffn_up': 'grad_w', 'grad_ffn_conv_w': 'grad_w', 'grad_ffn_conv_b': 'grad_w', 'grad_ffn_down': 'grad_w', 'delta_ada_w': 'delta_w', 'delta_ada_b': 'delta_w', 'delta_pre_mix_g': 'delta_w', 'delta_post_mix_g': 'delta_w', 'delta_w_in': 'delta_w', 'delta_conv_w': 'delta_w', 'delta_conv_b': 'delta_w', 'delta_conv_ln_g': 'delta_w', 'delta_conv_ln_b': 'delta_w', 'delta_pool_w': 'delta_w', 'delta_pool_scale': 'delta_w', 'delta_w_out': 'delta_w', 'delta_pre_ffn_g': 'delta_w', 'delta_post_ffn_g': 'delta_w', 'delta_ffn_up': 'delta_w', 'delta_ffn_conv_w': 'delta_w', 'delta_ffn_conv_b': 'delta_w', 'delta_ffn_down': 'delta_w', 'new_m_ada_w': 'new_m', 'new_m_ada_b': 'new_m', 'new_m_pre_mix_g': 'new_m', 'new_m_post_mix_g': 'new_m', 'new_m_w_in': 'new_m', 'new_m_conv_w': 'new_m', 'new_m_conv_b': 'new_m', 'new_m_conv_ln_g': 'new_m', 'new_m_conv_ln_b': 'new_m', 'new_m_pool_w': 'new_m', 'new_m_pool_scale': 'new_m', 'new_m_w_out': 'new_m', 'new_m_pre_ffn_g': 'new_m', 'new_m_post_ffn_g': 'new_m', 'new_m_ffn_up': 'new_m', 'new_m_ffn_conv_w': 'new_m', 'new_m_ffn_conv_b': 'new_m', 'new_m_ffn_down': 'new_m', 'new_v_ada_w': 'new_v', 'new_v_ada_b': 'new_v', 'new_v_pre_mix_g': 'new_v', 'new_v_post_mix_g': 'new_v', 'new_v_w_in': 'new_v', 'new_v_conv_w': 'new_v', 'new_v_conv_b': 'new_v', 'new_v_conv_ln_g': 'new_v', 'new_v_conv_ln_b': 'new_v', 'new_v_pool_w': 'new_v', 'new_v_pool_scale': 'new_v', 'new_v_w_out': 'new_v', 'new_v_pre_ffn_g': 'new_v', 'new_v_post_ffn_g': 'new_v', 'new_v_ffn_up': 'new_v', 'new_v_ffn_conv_w': 'new_v', 'new_v_ffn_conv_b': 'new_v', 'new_v_ffn_down': 'new_v'}


def _forward(args):
    return _fwd_reference(*[args[k] for k in FWD_PARAMS])


def _output_shape():
    out = _jax.eval_shape(lambda: _forward(_fwd_setup_inputs(0)))
    return out.shape, out.dtype

N_MICROBATCH = 1
ADAM_LR = 0.001
ADAM_B1 = 0.9
ADAM_B2 = 0.999
ADAM_EPS = 1e-08
ADAM_WD = 0.01
ADAM_STEP = 10
PER_EXAMPLE_BATCH_AXIS = {'x': 0, 'c': 0, 'loss_target': 0}
SHARED_INPUTS = []
_WEIGHT_DTYPES = {'ada_w': _jnp.float32, 'ada_b': _jnp.float32, 'pre_mix_g': _jnp.float32, 'post_mix_g': _jnp.float32, 'w_in': _jnp.float32, 'conv_w': _jnp.float32, 'conv_b': _jnp.float32, 'conv_ln_g': _jnp.float32, 'conv_ln_b': _jnp.float32, 'pool_w': _jnp.float32, 'pool_scale': _jnp.float32, 'w_out': _jnp.float32, 'pre_ffn_g': _jnp.float32, 'post_ffn_g': _jnp.float32, 'ffn_up': _jnp.float32, 'ffn_conv_w': _jnp.float32, 'ffn_conv_b': _jnp.float32, 'ffn_down': _jnp.float32}
MOMENT_SCALE = {'ada_w': 6.259283e+00, 'ada_b': 1.836057e+01, 'pre_mix_g': 1.796035e+00, 'post_mix_g': 3.217155e+01, 'w_in': 1.438061e+00, 'conv_w': 1.178491e+00, 'conv_b': 1.148543e+01, 'conv_ln_g': 4.539512e+00, 'conv_ln_b': 7.605756e+00, 'pool_w': 2.388269e+00, 'pool_scale': 2.585781e+00, 'w_out': 2.603076e+00, 'pre_ffn_g': 1.332170e+00, 'post_ffn_g': 3.187325e+01, 'ffn_up': 5.969755e-01, 'ffn_conv_w': 6.511376e-01, 'ffn_conv_b': 2.088443e+00, 'ffn_down': 1.162089e+00}


def _to_microbatches(a, axis):
    t = _jnp.moveaxis(a, axis, 0)
    t = t.reshape((N_MICROBATCH, t.shape[0] // N_MICROBATCH) + t.shape[1:])
    return _jnp.moveaxis(t, 1, axis + 1)


def setup_inputs(seed: int = 0) -> dict:
    inp = _fwd_setup_inputs(seed)
    key = _jax.random.fold_in(_jax.random.key(seed), 7919)
    shape, _ = _output_shape()
    out = dict(inp)
    out["loss_target"] = _jax.random.normal(_jax.random.fold_in(key, 0), shape, _jnp.float32)
    for i, name in enumerate(TWIN_WEIGHTS):
        w = inp[name].astype(_jnp.float32)
        if MOMENT_SCALE is None:
            s = _jnp.sqrt(_jnp.mean(_jnp.square(w)) + 1e-30)
        else:
            s = MOMENT_SCALE[name]
        km, kv = _jax.random.split(_jax.random.fold_in(key, i + 1))
        out[name] = w
        out["m_" + name] = s * _jax.random.normal(km, w.shape, _jnp.float32)
        out["v_" + name] = (s * s) * _jax.random.uniform(kv, w.shape, _jnp.float32, 0.5, 1.5)
    if N_MICROBATCH > 1:
        for name, axis in PER_EXAMPLE_BATCH_AXIS.items():
            out[name] = _to_microbatches(out[name], axis)
    return {'x': out['x'], 'c': out['c'], 'ada_w': out['ada_w'], 'ada_b': out['ada_b'], 'pre_mix_g': out['pre_mix_g'], 'post_mix_g': out['post_mix_g'], 'w_in': out['w_in'], 'conv_w': out['conv_w'], 'conv_b': out['conv_b'], 'conv_ln_g': out['conv_ln_g'], 'conv_ln_b': out['conv_ln_b'], 'pool_w': out['pool_w'], 'pool_scale': out['pool_scale'], 'w_out': out['w_out'], 'pre_ffn_g': out['pre_ffn_g'], 'post_ffn_g': out['post_ffn_g'], 'ffn_up': out['ffn_up'], 'ffn_conv_w': out['ffn_conv_w'], 'ffn_conv_b': out['ffn_conv_b'], 'ffn_down': out['ffn_down'], 'loss_target': out['loss_target'], 'm_ada_w': out['m_ada_w'], 'm_ada_b': out['m_ada_b'], 'm_pre_mix_g': out['m_pre_mix_g'], 'm_post_mix_g': out['m_post_mix_g'], 'm_w_in': out['m_w_in'], 'm_conv_w': out['m_conv_w'], 'm_conv_b': out['m_conv_b'], 'm_conv_ln_g': out['m_conv_ln_g'], 'm_conv_ln_b': out['m_conv_ln_b'], 'm_pool_w': out['m_pool_w'], 'm_pool_scale': out['m_pool_scale'], 'm_w_out': out['m_w_out'], 'm_pre_ffn_g': out['m_pre_ffn_g'], 'm_post_ffn_g': out['m_post_ffn_g'], 'm_ffn_up': out['m_ffn_up'], 'm_ffn_conv_w': out['m_ffn_conv_w'], 'm_ffn_conv_b': out['m_ffn_conv_b'], 'm_ffn_down': out['m_ffn_down'], 'v_ada_w': out['v_ada_w'], 'v_ada_b': out['v_ada_b'], 'v_pre_mix_g': out['v_pre_mix_g'], 'v_post_mix_g': out['v_post_mix_g'], 'v_w_in': out['v_w_in'], 'v_conv_w': out['v_conv_w'], 'v_conv_b': out['v_conv_b'], 'v_conv_ln_g': out['v_conv_ln_g'], 'v_conv_ln_b': out['v_conv_ln_b'], 'v_pool_w': out['v_pool_w'], 'v_pool_scale': out['v_pool_scale'], 'v_w_out': out['v_w_out'], 'v_pre_ffn_g': out['v_pre_ffn_g'], 'v_post_ffn_g': out['v_post_ffn_g'], 'v_ffn_up': out['v_ffn_up'], 'v_ffn_conv_w': out['v_ffn_conv_w'], 'v_ffn_conv_b': out['v_ffn_conv_b'], 'v_ffn_down': out['v_ffn_down']}


def _loss(weights, diff, rest, loss_target):
    with _jax.named_scope("forward"):
        args = {**rest, TWIN_DIFF_INPUT: diff, **{k: w.astype(_WEIGHT_DTYPES[k]) for k, w in weights.items()}}
        y = _forward(args)
    with _jax.named_scope("loss_head"):
        err = _jnp.square(y.astype(_jnp.float32) - loss_target)
        return 0.5 * _jnp.sum(_jnp.mean(err, axis=-1)) if err.ndim else 0.5 * err


def _adamw(w, g, m, v):
    m = ADAM_B1 * m + (1.0 - ADAM_B1) * g
    v = ADAM_B2 * v + (1.0 - ADAM_B2) * _jnp.square(g)
    m_hat = m / (1.0 - ADAM_B1 ** ADAM_STEP)
    v_hat = v / (1.0 - ADAM_B2 ** ADAM_STEP)
    delta = -ADAM_LR * (m_hat / (_jnp.sqrt(v_hat) + ADAM_EPS) + ADAM_WD * w)
    return delta, m, v


def reference(x, c, ada_w, ada_b, pre_mix_g, post_mix_g, w_in, conv_w, conv_b, conv_ln_g, conv_ln_b, pool_w, pool_scale, w_out, pre_ffn_g, post_ffn_g, ffn_up, ffn_conv_w, ffn_conv_b, ffn_down, loss_target, m_ada_w, m_ada_b, m_pre_mix_g, m_post_mix_g, m_w_in, m_conv_w, m_conv_b, m_conv_ln_g, m_conv_ln_b, m_pool_w, m_pool_scale, m_w_out, m_pre_ffn_g, m_post_ffn_g, m_ffn_up, m_ffn_conv_w, m_ffn_conv_b, m_ffn_down, v_ada_w, v_ada_b, v_pre_mix_g, v_post_mix_g, v_w_in, v_conv_w, v_conv_b, v_conv_ln_g, v_conv_ln_b, v_pool_w, v_pool_scale, v_w_out, v_pre_ffn_g, v_post_ffn_g, v_ffn_up, v_ffn_conv_w, v_ffn_conv_b, v_ffn_down):
    given = dict(x=x, c=c, ada_w=ada_w, ada_b=ada_b, pre_mix_g=pre_mix_g, post_mix_g=post_mix_g, w_in=w_in, conv_w=conv_w, conv_b=conv_b, conv_ln_g=conv_ln_g, conv_ln_b=conv_ln_b, pool_w=pool_w, pool_scale=pool_scale, w_out=w_out, pre_ffn_g=pre_ffn_g, post_ffn_g=post_ffn_g, ffn_up=ffn_up, ffn_conv_w=ffn_conv_w, ffn_conv_b=ffn_conv_b, ffn_down=ffn_down, loss_target=loss_target, m_ada_w=m_ada_w, m_ada_b=m_ada_b, m_pre_mix_g=m_pre_mix_g, m_post_mix_g=m_post_mix_g, m_w_in=m_w_in, m_conv_w=m_conv_w, m_conv_b=m_conv_b, m_conv_ln_g=m_conv_ln_g, m_conv_ln_b=m_conv_ln_b, m_pool_w=m_pool_w, m_pool_scale=m_pool_scale, m_w_out=m_w_out, m_pre_ffn_g=m_pre_ffn_g, m_post_ffn_g=m_post_ffn_g, m_ffn_up=m_ffn_up, m_ffn_conv_w=m_ffn_conv_w, m_ffn_conv_b=m_ffn_conv_b, m_ffn_down=m_ffn_down, v_ada_w=v_ada_w, v_ada_b=v_ada_b, v_pre_mix_g=v_pre_mix_g, v_post_mix_g=v_post_mix_g, v_w_in=v_w_in, v_conv_w=v_conv_w, v_conv_b=v_conv_b, v_conv_ln_g=v_conv_ln_g, v_conv_ln_b=v_conv_ln_b, v_pool_w=v_pool_w, v_pool_scale=v_pool_scale, v_w_out=v_w_out, v_pre_ffn_g=v_pre_ffn_g, v_post_ffn_g=v_post_ffn_g, v_ffn_up=v_ffn_up, v_ffn_conv_w=v_ffn_conv_w, v_ffn_conv_b=v_ffn_conv_b, v_ffn_down=v_ffn_down)
    weights = {n: given[n] for n in TWIN_WEIGHTS}
    shared = {n: given[n] for n in SHARED_INPUTS}
    per_example = {n: given[n] for n in ['x', 'c']}
    grad_fn = _jax.value_and_grad(_loss, argnums=(0, 1))

    def one_microbatch(ex, loss_target):
        ex = dict(ex)
        diff = ex.pop(TWIN_DIFF_INPUT)
        return grad_fn(weights, diff, {**shared, **ex}, loss_target)

    if N_MICROBATCH == 1:
        loss, (grad_w, grad_x) = one_microbatch(per_example, given["loss_target"])
    else:
        def body(carry, xs):
            loss_sum, grad_sum = carry
            l_k, (gw_k, gx_k) = one_microbatch(xs[0], xs[1])
            with _jax.named_scope("update"):
                return (loss_sum + l_k, _jax.tree.map(_jnp.add, grad_sum, gw_k)), gx_k

        init = (_jnp.zeros((), _jnp.float32), _jax.tree.map(_jnp.zeros_like, weights))
        (loss, grad_w), grad_x = _jax.lax.scan(body, init, (per_example, given["loss_target"]))
    with _jax.named_scope("update"):
        delta_w, new_m, new_v = {}, {}, {}
        for n in TWIN_WEIGHTS:
            delta_w[n], new_m[n], new_v[n] = _adamw(weights[n], grad_w[n], given["m_" + n], given["v_" + n])
    return (loss, grad_x, *[grad_w[n] for n in TWIN_WEIGHTS], *[delta_w[n] for n in TWIN_WEIGHTS],
            *[new_m[n] for n in TWIN_WEIGHTS], *[new_v[n] for n in TWIN_WEIGHTS])
```

```python
import functools

import jax
import jax.numpy as jnp
from jax import lax
from jax.experimental import pallas as pl
from jax.experimental.pallas import tpu as pltpu

F32 = jnp.float32
BF16 = jnp.bfloat16
MESH = pl.DeviceIdType.MESH

EPS = 1e-6
POOL_WINDOWS = (2, 4, 8, 16)
ADAM_LR = 0.001
ADAM_B1 = 0.9
ADAM_B2 = 0.999
ADAM_EPS = 1e-08
ADAM_WD = 0.01
ADAM_STEP = 10

N_CHIPS = 4
N_DEV = 8
LANES = 128
HALO = 32
FHALO = 8
VMEM_LIMIT = 60 * 1024 * 1024


def _pos():
    return lax.axis_index("x"), lax.axis_index("y"), lax.axis_index("c")


def _flip(v, f):
    return 1 - v if f else v


def _full(shape):
    nd = len(shape)
    return pl.BlockSpec(shape, lambda *_: (0,) * nd)


_ANY = pl.BlockSpec(memory_space=pl.ANY)
_VMEM = pl.BlockSpec(memory_space=pltpu.VMEM)


def _sigmoid(v):
    return 1.0 / (1.0 + jnp.exp(-v))


def _colsum(v):
    return jnp.sum(v, axis=0, keepdims=True)


def _gather8(v, name):
    r, ccols = v.shape

    def body(v_ref, out_ref, send_sems, recv_sems, local_sem):
        x, y, c = _pos()
        me = 4 * x + 2 * y + c
        mine = pltpu.make_async_copy(v_ref, out_ref.at[me], local_sem)
        mine.start()
        peers = [(_flip(x, (k >> 2) & 1), _flip(y, (k >> 1) & 1), _flip(c, k & 1)) for k in range(1, N_DEV)]
        sends = []
        for k, peer in enumerate(peers):
            cp = pltpu.make_async_remote_copy(src_ref=v_ref, dst_ref=out_ref.at[me], send_sem=send_sems.at[k],
                                              recv_sem=recv_sems.at[k], device_id=peer, device_id_type=MESH)
            cp.start()
            sends.append(cp)
        for k, peer in enumerate(peers):
            pidx = 4 * peer[0] + 2 * peer[1] + peer[2]
            pltpu.make_async_remote_copy(src_ref=v_ref, dst_ref=out_ref.at[pidx], send_sem=send_sems.at[k],
                                         recv_sem=recv_sems.at[k], device_id=peer, device_id_type=MESH).wait_recv()
        for cp in sends:
            cp.wait_send()
        mine.wait()

    return pl.pallas_call(
        body, name=name, out_shape=jax.ShapeDtypeStruct((N_DEV, r, ccols), v.dtype),
        in_specs=[_VMEM], out_specs=_VMEM,
        scratch_shapes=[pltpu.SemaphoreType.DMA((N_DEV - 1,)), pltpu.SemaphoreType.DMA((N_DEV - 1,)),
                        pltpu.SemaphoreType.DMA(())],
        compiler_params=pltpu.CompilerParams(vmem_limit_bytes=VMEM_LIMIT),
    )(v)


def _allreduce8(v, name):
    r, ccols = v.shape

    def body(v_ref, out_ref, rbuf, send_sems, recv_sems):
        x, y, c = _pos()
        out_ref[...] = v_ref[...]
        peers = [(x, y, 1 - c), (1 - x, y, c), (x, 1 - y, c)]
        for s, peer in enumerate(peers):
            cp = pltpu.make_async_remote_copy(src_ref=out_ref, dst_ref=rbuf.at[s], send_sem=send_sems.at[s],
                                              recv_sem=recv_sems.at[s], device_id=peer, device_id_type=MESH)
            cp.start()
            cp.wait()
            out_ref[...] = out_ref[...] + rbuf[s]

    return pl.pallas_call(
        body, name=name, out_shape=jax.ShapeDtypeStruct((r, ccols), v.dtype),
        in_specs=[_VMEM], out_specs=_VMEM,
        scratch_shapes=[pltpu.VMEM((3, r, ccols), v.dtype), pltpu.SemaphoreType.DMA((3,)), pltpu.SemaphoreType.DMA((3,))],
        compiler_params=pltpu.CompilerParams(vmem_limit_bytes=VMEM_LIMIT),
    )(v)


def _chip_peers(x, y, c):
    peers = [(_flip(x, (k >> 1) & 1), _flip(y, k & 1), c) for k in range(1, N_CHIPS)]
    return peers, [2 * p[0] + p[1] for p in peers]


def _weight_gather(shards, name):
    n = len(shards)
    per = 2 * (N_CHIPS - 1)

    def body(*refs):
        srcs, dsts = refs[:n], refs[n:2 * n]
        send_sems, recv_sems, local_sems = refs[2 * n:]
        x, y, c = _pos()
        jm = 2 * x + y
        sib = (x, y, 1 - c)
        peers, pjs = _chip_peers(x, y, c)
        locs, sends = [], []
        for a in range(n):
            hr = shards[a].shape[0] // 2
            rows = pl.ds(c * hr, hr)
            loc = pltpu.make_async_copy(srcs[a], dsts[a].at[jm], local_sems.at[a])
            loc.start()
            locs.append(loc)
            for k, peer in enumerate(peers):
                cp = pltpu.make_async_remote_copy(src_ref=srcs[a].at[rows], dst_ref=dsts[a].at[jm, rows],
                                                  send_sem=send_sems.at[a * per + k], recv_sem=recv_sems.at[a * per + k],
                                                  device_id=peer, device_id_type=MESH)
                cp.start()
                sends.append(cp)
        for a in range(n):
            hr = shards[a].shape[0] // 2
            rows = pl.ds(c * hr, hr)
            for k, peer in enumerate(peers):
                landed = dsts[a].at[pjs[k], rows]
                pltpu.make_async_remote_copy(src_ref=landed, dst_ref=landed, send_sem=send_sems.at[a * per + k],
                                             recv_sem=recv_sems.at[a * per + k], device_id=peer,
                                             device_id_type=MESH).wait_recv()
                fw = pltpu.make_async_remote_copy(src_ref=landed, dst_ref=landed, send_sem=send_sems.at[a * per + 3 + k],
                                                  recv_sem=recv_sems.at[a * per + 3 + k], device_id=sib,
                                                  device_id_type=MESH)
                fw.start()
                sends.append(fw)
        for a in range(n):
            hr = shards[a].shape[0] // 2
            other = pl.ds((1 - c) * hr, hr)
            for k in range(N_CHIPS - 1):
                landed = dsts[a].at[pjs[k], other]
                pltpu.make_async_remote_copy(src_ref=landed, dst_ref=landed, send_sem=send_sems.at[a * per + 3 + k],
                                             recv_sem=recv_sems.at[a * per + 3 + k], device_id=sib,
                                             device_id_type=MESH).wait_recv()
        for cp in sends:
            cp.wait_send()
        for loc in locs:
            loc.wait()

    return pl.pallas_call(
        body, name=name,
        out_shape=[jax.ShapeDtypeStruct((N_CHIPS,) + s.shape, s.dtype) for s in shards],
        in_specs=[_ANY] * n, out_specs=[_ANY] * n,
        scratch_shapes=[pltpu.SemaphoreType.DMA((n * per,)), pltpu.SemaphoreType.DMA((n * per,)),
                        pltpu.SemaphoreType.DMA((n,))],
    )(*shards)


def _sibling_swap_halves(gs, name):
    n = len(gs)

    def body(*refs):
        srcs, dsts = refs[:n], refs[n:2 * n]
        send_sems, recv_sems = refs[2 * n:]
        x, y, c = _pos()
        sib = (x, y, 1 - c)
        cps = []
        for a in range(n):
            hr = gs[a].shape[1] // 2
            cp = pltpu.make_async_remote_copy(src_ref=srcs[a].at[:, pl.ds((1 - c) * hr, hr), :], dst_ref=dsts[a],
                                              send_sem=send_sems.at[a], recv_sem=recv_sems.at[a],
                                              device_id=sib, device_id_type=MESH)
            cp.start()
            cps.append(cp)
        for cp in cps:
            cp.wait()

    return pl.pallas_call(
        body, name=name,
        out_shape=[jax.ShapeDtypeStruct((g.shape[0], g.shape[1] // 2, g.shape[2]), g.dtype) for g in gs],
        in_specs=[_ANY] * n, out_specs=[_ANY] * n,
        scratch_shapes=[pltpu.SemaphoreType.DMA((n,)), pltpu.SemaphoreType.DMA((n,))],
    )(*gs)


def _chip_exchange(ps, name):
    n = len(ps)
    per = N_CHIPS - 1

    def body(*refs):
        srcs, dsts = refs[:n], refs[n:2 * n]
        send_sems, recv_sems = refs[2 * n:]
        x, y, c = _pos()
        peers, pjs = _chip_peers(x, y, c)
        cps = []
        for a in range(n):
            for k, peer in enumerate(peers):
                cp = pltpu.make_async_remote_copy(src_ref=srcs[a].at[pjs[k]], dst_ref=dsts[a].at[k],
                                                  send_sem=send_sems.at[a * per + k], recv_sem=recv_sems.at[a * per + k],
                                                  device_id=peer, device_id_type=MESH)
                cp.start()
                cps.append(cp)
        for cp in cps:
            cp.wait()

    return pl.pallas_call(
        body, name=name,
        out_shape=[jax.ShapeDtypeStruct((per,) + p.shape[1:], p.dtype) for p in ps],
        in_specs=[_ANY] * n, out_specs=[_ANY] * n,
        scratch_shapes=[pltpu.SemaphoreType.DMA((n * per,)), pltpu.SemaphoreType.DMA((n * per,))],
    )(*ps)


def _sibling_share(fs, name):
    n = len(fs)

    def body(*refs):
        srcs, dsts = refs[:n], refs[n:2 * n]
        send_sems, recv_sems = refs[2 * n:]
        x, y, c = _pos()
        sib = (x, y, 1 - c)
        cps = []
        for a in range(n):
            cp = pltpu.make_async_remote_copy(src_ref=srcs[a].at[c], dst_ref=dsts[a].at[c],
                                              send_sem=send_sems.at[a], recv_sem=recv_sems.at[a],
                                              device_id=sib, device_id_type=MESH)
            cp.start()
            cps.append(cp)
        for a in range(n):
            pltpu.make_async_remote_copy(src_ref=srcs[a].at[1 - c], dst_ref=dsts[a].at[1 - c],
                                         send_sem=send_sems.at[a], recv_sem=recv_sems.at[a],
                                         device_id=sib, device_id_type=MESH).wait_recv()
        for cp in cps:
            cp.wait_send()

    return pl.pallas_call(
        body, name=name,
        out_shape=[jax.ShapeDtypeStruct(f.shape, f.dtype) for f in fs],
        in_specs=[_ANY] * n, out_specs=[_ANY] * n,
        input_output_aliases={a: a for a in range(n)},
        scratch_shapes=[pltpu.SemaphoreType.DMA((n,)), pltpu.SemaphoreType.DMA((n,))],
    )(*fs)


def _row_tile(rows, cols, itemsize, budget=2 * 1024 * 1024):
    best = None
    for t in range(16, rows + 1, 16):
        if rows % t == 0 and t * cols * itemsize <= budget:
            best = t
    return best if best is not None else rows


def _pair_sum(g, r1, cidx, name):
    nj, r, ccols = g.shape
    hr = r // 2
    tr = _row_tile(hr, ccols, 4)
    nt = hr // tr

    def body(c_ref, g_ref, r_ref, o_ref):
        o_ref[...] = (g_ref[...].astype(F32) + r_ref[...].astype(F32)).astype(o_ref.dtype)

    return pl.pallas_call(
        body, name=name, out_shape=jax.ShapeDtypeStruct((nj, hr, ccols), g.dtype),
        grid_spec=pltpu.PrefetchScalarGridSpec(
            num_scalar_prefetch=1, grid=(nj, nt),
            in_specs=[pl.BlockSpec((None, tr, ccols), lambda j, i, c_ref: (j, c_ref[0] * nt + i, 0)),
                      pl.BlockSpec((None, tr, ccols), lambda j, i, c_ref: (j, i, 0))],
            out_specs=pl.BlockSpec((None, tr, ccols), lambda j, i, c_ref: (j, i, 0))),
        compiler_params=pltpu.CompilerParams(dimension_semantics=("arbitrary", "arbitrary")),
    )(cidx, g, r1)


def _chip_sum(p, r2, idx, name):
    nj, hr, ccols = p.shape
    tr = _row_tile(hr, ccols, 4)
    nt = hr // tr

    def body(i_ref, p_ref, r_ref, o_ref):
        s = p_ref[...].astype(F32)
        for k in range(N_CHIPS - 1):
            s = s + r_ref[k].astype(F32)
        o_ref[...] = s

    return pl.pallas_call(
        body, name=name, out_shape=jax.ShapeDtypeStruct((2, hr, ccols), F32),
        grid_spec=pltpu.PrefetchScalarGridSpec(
            num_scalar_prefetch=1, grid=(nt,),
            in_specs=[pl.BlockSpec((None, tr, ccols), lambda i, i_ref: (i_ref[0], i, 0)),
                      pl.BlockSpec((N_CHIPS - 1, tr, ccols), lambda i, i_ref: (0, i, 0))],
            out_specs=pl.BlockSpec((None, tr, ccols), lambda i, i_ref: (i_ref[1], i, 0))),
        compiler_params=pltpu.CompilerParams(dimension_semantics=("arbitrary",)),
    )(idx, p, r2)


def _adam_math(w, g, m, v):
    m2 = ADAM_B1 * m + (1.0 - ADAM_B1) * g
    v2 = ADAM_B2 * v + (1.0 - ADAM_B2) * (g * g)
    m_hat = m2 / (1.0 - ADAM_B1 ** ADAM_STEP)
    v_hat = v2 / (1.0 - ADAM_B2 ** ADAM_STEP)
    delta = -ADAM_LR * (m_hat / (jnp.sqrt(v_hat) + ADAM_EPS) + ADAM_WD * w)
    return delta, m2, v2


def _adamw_layers(w, m, v, gs, name):
    nl, r, ccols = w.shape
    tr = _row_tile(r, ccols, 4, budget=1024 * 1024)
    nt = r // tr

    def body(w_ref, m_ref, v_ref, *rest):
        g_refs, (go_ref, d_ref, mo_ref, vo_ref) = rest[:nl], rest[nl:]
        l = pl.program_id(0)
        g = jnp.zeros((tr, ccols), F32)
        for k in range(nl):
            g = jnp.where(l == k, g_refs[k][...], g)
        delta, m2, v2 = _adam_math(w_ref[...], g, m_ref[...], v_ref[...])
        go_ref[...] = g
        d_ref[...] = delta
        mo_ref[...] = m2
        vo_ref[...] = v2

    big = pl.BlockSpec((None, tr, ccols), lambda l, i: (l, i, 0))

    def gspec(k):
        return pl.BlockSpec((tr, ccols), lambda l, i: (jnp.where(l == k, i, jnp.where(l < k, 0, nt - 1)), 0))

    return pl.pallas_call(
        body, name=name, out_shape=[jax.ShapeDtypeStruct(w.shape, F32)] * 4,
        grid=(nl, nt),
        in_specs=[big, big, big] + [gspec(k) for k in range(nl)],
        out_specs=[big, big, big, big],
        compiler_params=pltpu.CompilerParams(dimension_semantics=("arbitrary", "arbitrary")),
    )(w, m, v, *gs)


def _adamw_flat(w, g, m, v, name):
    r, ccols = w.shape

    def body(w_ref, g_ref, m_ref, v_ref, d_ref, mo_ref, vo_ref):
        delta, m2, v2 = _adam_math(w_ref[...], g_ref[...], m_ref[...], v_ref[...])
        d_ref[...] = delta
        mo_ref[...] = m2
        vo_ref[...] = v2

    return pl.pallas_call(
        body, name=name, out_shape=[jax.ShapeDtypeStruct((r, ccols), F32)] * 3,
        in_specs=[_VMEM] * 4, out_specs=[_VMEM] * 3,
        compiler_params=pltpu.CompilerParams(vmem_limit_bytes=VMEM_LIMIT),
    )(w, g, m, v)


def _ada_forward(c_all, ada_w, ada_b_cols, name):
    nl, d, ncols = ada_w.shape
    bg = c_all.shape[0]
    tn = 512 if ncols % 512 == 0 else ncols

    def body(c_ref, w_ref, b_ref, o_ref):
        cv = c_ref[...]
        ca = (cv * _sigmoid(cv)).astype(BF16)
        o_ref[...] = jnp.dot(ca, w_ref[...].astype(BF16), preferred_element_type=F32) + b_ref[...]

    return pl.pallas_call(
        body, name=name, out_shape=jax.ShapeDtypeStruct((nl, bg, ncols), F32),
        grid=(nl, ncols // tn),
        in_specs=[pl.BlockSpec((bg, d), lambda l, j: (0, 0)),
                  pl.BlockSpec((None, d, tn), lambda l, j: (l, 0, j)),
                  pl.BlockSpec((None, 1, tn), lambda l, j: (l, 0, j))],
        out_specs=pl.BlockSpec((None, bg, tn), lambda l, j: (l, 0, j)),
        compiler_params=pltpu.CompilerParams(dimension_semantics=("arbitrary", "arbitrary")),
    )(c_all, ada_w, ada_b_cols)


def _ada_update(c_all, dmod_cols, w, m, v, name):
    nl, d, ncols = w.shape
    bg = c_all.shape[0]
    tn = 512 if ncols % 512 == 0 else ncols

    def body(c_ref, dm_ref, w_ref, m_ref, v_ref, go_ref, d_ref, mo_ref, vo_ref):
        cv = c_ref[...]
        ca = (cv * _sigmoid(cv)).astype(BF16)
        g = lax.dot_general(ca, dm_ref[...].astype(BF16), (((0,), (0,)), ((), ())), preferred_element_type=F32)
        delta, m2, v2 = _adam_math(w_ref[...], g, m_ref[...], v_ref[...])
        go_ref[...] = g
        d_ref[...] = delta
        mo_ref[...] = m2
        vo_ref[...] = v2

    big = pl.BlockSpec((None, d, tn), lambda l, j: (l, 0, j))
    return pl.pallas_call(
        body, name=name, out_shape=[jax.ShapeDtypeStruct(w.shape, F32)] * 4,
        grid=(nl, ncols // tn),
        in_specs=[pl.BlockSpec((bg, d), lambda l, j: (0, 0)),
                  pl.BlockSpec((None, bg, tn), lambda l, j: (l, 0, j)), big, big, big],
        out_specs=[big, big, big, big],
        compiler_params=pltpu.CompilerParams(dimension_semantics=("arbitrary", "arbitrary"),
                                             vmem_limit_bytes=VMEM_LIMIT),
    )(c_all, dmod_cols, w, m, v)


def _load_weights(first, pairs, sems):
    @pl.when(first)
    def _():
        cps = [pltpu.make_async_copy(src, dst, sems.at[k]) for k, (src, dst) in enumerate(pairs)]
        for cp in cps:
            cp.start()
        for cp in cps:
            cp.wait()


def _ada_norm(xv, g, sc, sh):
    r = lax.rsqrt(jnp.mean(xv * xv, axis=-1, keepdims=True) + EPS)
    xn = xv * r
    return (xn * g) * (1.0 + sc) + sh, xn, r


def _ada_norm_bwd(dh, xn, r, g, sc):
    d_sh = _colsum(dh)
    d_sc = _colsum(dh * (xn * g))
    dxg = dh * (1.0 + sc)
    d_g = _colsum(dxg * xn)
    gd = dxg * g
    dx = r * (gd - xn * jnp.mean(gd * xn, axis=-1, keepdims=True))
    return dx, d_sh, d_sc, d_g


def _gated_residual_bwd(dxo, o, g_post, gt):
    r = lax.rsqrt(jnp.mean(o * o, axis=-1, keepdims=True) + EPS)
    on = o * r
    d_gt = _colsum(dxo * (on * g_post))
    dy = dxo * (1.0 + gt)
    d_gp = _colsum(dy * on)
    gd = dy * g_post
    do = r * (gd - on * jnp.mean(gd * on, axis=-1, keepdims=True))
    return do, d_gt, d_gp


def _seq_positions(i, tm, width):
    return i * tm + lax.broadcasted_iota(jnp.int32, (tm, width), 0)


def _mixer_forward(x, mod, vec_d, vec_c, cw, pw, win_g, wout_g, taps, tm, name):
    nb, s, d = x.shape
    n = s // tm
    nj, _, dcol = win_g.shape
    din = nj * dcol
    dc = vec_c.shape[-1]
    dpool = din - 2 * dc
    dmix = dc + dpool
    ro = wout_g.shape[1]
    ngrp = dpool // LANES

    def body(x_ref, mod_ref, vd_ref, vc_ref, cw_ref, pw_ref, win_hbm, wout_hbm,
             xo_ref, h_ref, u_ref, ac_ref, dp_ref, z_ref, o_ref,
             win_v, wout_v, ext_a, ext_p, sems):
        b, i = pl.program_id(0), pl.program_id(1)
        pairs = [(win_hbm.at[j], win_v.at[:, pl.ds(j * dcol, dcol)]) for j in range(nj)]
        pairs += [(wout_hbm.at[j], wout_v.at[pl.ds(j * ro, ro), :]) for j in range(nj)]
        _load_weights((b == 0) & (i == 0), pairs, sems)

        xv = x_ref[...]
        h, _, _ = _ada_norm(xv, vd_ref[0:1, :], mod_ref[1:2, :], mod_ref[0:1, :])
        hb = h.astype(BF16)
        h_ref[...] = hb
        u = jnp.dot(hb, win_v[...], preferred_element_type=F32)
        u_ref[...] = u.astype(BF16)
        ag = u[:, :dc] * _sigmoid(u[:, dc:2 * dc])
        up = u[:, 2 * dc:]

        @pl.when(i == 0)
        def _():
            ext_a[0:HALO, :] = jnp.zeros((HALO, dc), F32)
            ext_p[0:HALO, :] = jnp.zeros((HALO, dpool), F32)

        @pl.when(i > 0)
        def _():
            ext_a[0:HALO, :] = ext_a[tm:tm + HALO, :]
            ext_p[0:HALO, :] = ext_p[tm:tm + HALO, :]

        ext_a[HALO:HALO + tm, :] = ag
        ext_p[HALO:HALO + tm, :] = up

        acc = jnp.broadcast_to(vc_ref[0:1, :], (tm, dc))
        for k in range(taps):
            acc = acc + cw_ref[k:k + 1, :] * ext_a[pl.ds(HALO - (taps - 1) + k, tm), :]
        ac_ref[...] = acc.astype(BF16)
        mu = jnp.mean(acc, axis=-1, keepdims=True)
        xc = acc - mu
        var = jnp.mean(xc * xc, axis=-1, keepdims=True)
        al = (xc * lax.rsqrt(var + EPS)) * vc_ref[1:2, :] + vc_ref[2:3, :]
        a = al * _sigmoid(al)

        pos = _seq_positions(i, tm, LANES)
        parts = [a.astype(BF16)]
        for g in range(ngrp):
            w = POOL_WINDOWS[g]
            cols = slice(g * LANES, (g + 1) * LANES)
            sw = ext_p[:, cols]
            step = 1
            while step < w:
                sw = sw + pltpu.roll(sw, step, axis=0)
                step *= 2
            cnt = jnp.minimum(pos + 1, w).astype(F32)
            dg = (sw[HALO:, :] / cnt - up[:, cols]).astype(BF16)
            dp_ref[:, cols] = dg
            q = jnp.dot(dg, pw_ref[g], preferred_element_type=F32)
            parts.append((q * vc_ref[3:4, cols]).astype(BF16))
        z = jnp.concatenate(parts, axis=-1)
        z_ref[...] = z
        o = jnp.dot(z, wout_v[...], preferred_element_type=F32)
        o_ref[...] = o
        r2 = lax.rsqrt(jnp.mean(o * o, axis=-1, keepdims=True) + EPS)
        xo_ref[...] = xv + (1.0 + mod_ref[2:3, :]) * ((o * r2) * vd_ref[1:2, :])

    def tile(width):
        return pl.BlockSpec((None, tm, width), lambda b, i: (b, i, 0))

    return pl.pallas_call(
        body, name=name,
        out_shape=[jax.ShapeDtypeStruct((nb, s, d), F32), jax.ShapeDtypeStruct((nb, s, d), BF16),
                   jax.ShapeDtypeStruct((nb, s, din), BF16), jax.ShapeDtypeStruct((nb, s, dc), BF16),
                   jax.ShapeDtypeStruct((nb, s, dpool), BF16), jax.ShapeDtypeStruct((nb, s, dmix), BF16),
                   jax.ShapeDtypeStruct((nb, s, d), F32)],
        grid=(nb, n),
        in_specs=[tile(d), pl.BlockSpec((None, 8, d), lambda b, i: (b, 0, 0)), _full(vec_d.shape), _full(vec_c.shape),
                  _full(cw.shape), _full(pw.shape), _ANY, _ANY],
        out_specs=[tile(d), tile(d), tile(din), tile(dc), tile(dpool), tile(dmix), tile(d)],
        scratch_shapes=[pltpu.VMEM((d, din), BF16), pltpu.VMEM((dmix, d), BF16),
                        pltpu.VMEM((HALO + tm, dc), F32), pltpu.VMEM((HALO + tm, dpool), F32),
                        pltpu.SemaphoreType.DMA((2 * nj,))],
        compiler_params=pltpu.CompilerParams(dimension_semantics=("arbitrary", "arbitrary"),
                                             vmem_limit_bytes=VMEM_LIMIT),
    )(x, mod, vec_d, vec_c, cw, pw, win_g, wout_g)


def _mixer_backward(dxo, x, o, u, ac, dpl, mod, vec_d, vec_c, cw, pw, win_g, wout_g, taps, tm, name):
    nb, s, d = x.shape
    n = s // tm
    nj, _, dcol = win_g.shape
    din = nj * dcol
    dc = vec_c.shape[-1]
    dpool = din - 2 * dc
    dmix = dc + dpool
    ro = wout_g.shape[1]
    ngrp = dpool // LANES
    rext = tm + HALO

    def body(dxo_ref, x_ref, o_ref, u_ref, ac_ref, dp_ref, mod_ref, vd_ref, vc_ref, cw_ref, pw_ref, win_hbm, wout_hbm,
             dx_ref, du_ref, dob_ref, rowd_ref, rowb_ref, rowc_ref, dcw_ref, dpw_ref,
             win_v, wout_v, ext_a, ext_p, sems):
        b, i = pl.program_id(0), pl.program_id(1)
        first = (b == 0) & (i == 0)
        pairs = [(win_hbm.at[j], win_v.at[:, pl.ds(j * dcol, dcol)]) for j in range(nj)]
        pairs += [(wout_hbm.at[j], wout_v.at[pl.ds(j * ro, ro), :]) for j in range(nj)]
        _load_weights(first, pairs, sems)

        @pl.when(first)
        def _():
            rowd_ref[...] = jnp.zeros_like(rowd_ref)
            rowc_ref[...] = jnp.zeros_like(rowc_ref)
            dcw_ref[...] = jnp.zeros_like(dcw_ref)
            dpw_ref[...] = jnp.zeros_like(dpw_ref)

        @pl.when(i == 0)
        def _():
            rowb_ref[...] = jnp.zeros_like(rowb_ref)
            ext_a[tm:rext, :] = jnp.zeros((HALO, dc), F32)
            ext_p[tm:rext, :] = jnp.zeros((HALO, dpool), F32)

        @pl.when(i > 0)
        def _():
            ext_a[tm:rext, :] = ext_a[0:HALO, :]
            ext_p[tm:rext, :] = ext_p[0:HALO, :]

        g_pre, g_post = vd_ref[0:1, :], vd_ref[1:2, :]
        sh, sc, gt = mod_ref[0:1, :], mod_ref[1:2, :], mod_ref[2:3, :]
        do, d_gt, d_gp = _gated_residual_bwd(dxo_ref[...], o_ref[...], g_post, gt)
        dob = do.astype(BF16)
        dob_ref[...] = dob
        dz = lax.dot_general(dob, wout_v[...], (((1,), (1,)), ((), ())), preferred_element_type=F32)

        acv = ac_ref[...].astype(F32)
        mu = jnp.mean(acv, axis=-1, keepdims=True)
        xc = acv - mu
        rstd = lax.rsqrt(jnp.mean(xc * xc, axis=-1, keepdims=True) + EPS)
        an = xc * rstd
        lg = vc_ref[1:2, :]
        al = an * lg + vc_ref[2:3, :]
        sg = _sigmoid(al)
        dal = dz[:, :dc] * (sg * (1.0 + al * (1.0 - sg)))
        d_lg = _colsum(dal * an)
        d_lb = _colsum(dal)
        dan = dal * lg
        dac = rstd * (dan - jnp.mean(dan, axis=-1, keepdims=True) - an * jnp.mean(dan * an, axis=-1, keepdims=True))
        d_cb = _colsum(dac)
        ext_a[0:tm, :] = dac
        uv = u_ref[:, 0:dc].astype(F32)
        sgu = _sigmoid(u_ref[:, dc:2 * dc].astype(F32))
        ag = uv * sgu
        dag = jnp.zeros((tm, dc), F32)
        for k in range(taps):
            sl = ext_a[pl.ds(taps - 1 - k, tm), :]
            dag = dag + cw_ref[k:k + 1, :] * sl
            dcw_ref[k:k + 1, :] += _colsum(ag * sl)
        du_ref[:, 0:dc] = (dag * sgu).astype(BF16)
        du_ref[:, dc:2 * dc] = (dag * uv * (sgu * (1.0 - sgu))).astype(BF16)

        pos = _seq_positions(n - 1 - i, tm, LANES)
        d_ps = []
        for g in range(ngrp):
            w = POOL_WINDOWS[g]
            cols = slice(g * LANES, (g + 1) * LANES)
            gcols = slice(dc + g * LANES, dc + (g + 1) * LANES)
            dgb = dp_ref[:, cols]
            q = jnp.dot(dgb, pw_ref[g], preferred_element_type=F32)
            dpg = dz[:, gcols]
            d_ps.append(_colsum(dpg * q))
            dq = (dpg * vc_ref[3:4, cols]).astype(BF16)
            dpw_ref[g] += lax.dot_general(dgb, dq, (((0,), (0,)), ((), ())), preferred_element_type=F32)
            dd = lax.dot_general(dq, pw_ref[g], (((1,), (1,)), ((), ())), preferred_element_type=F32)
            cnt = jnp.minimum(pos + 1, w).astype(F32)
            ext_p[0:tm, cols] = dd / cnt
            sw = ext_p[:, cols]
            step = 1
            while step < w:
                sw = sw + pltpu.roll(sw, rext - step, axis=0)
                step *= 2
            du_ref[:, 2 * dc + g * LANES:2 * dc + (g + 1) * LANES] = (sw[0:tm, :] - dd).astype(BF16)
        rowc_ref[0:1, :] += d_cb
        rowc_ref[1:2, :] += d_lg
        rowc_ref[2:3, :] += d_lb
        rowc_ref[3:4, :] += jnp.concatenate(d_ps, axis=-1)

        dh = lax.dot_general(du_ref[...], win_v[...], (((1,), (1,)), ((), ())), preferred_element_type=F32)
        _, xn, r1 = _ada_norm(x_ref[...], g_pre, sc, sh)
        dxb, d_sh, d_sc, d_g = _ada_norm_bwd(dh, xn, r1, g_pre, sc)
        dx_ref[...] = dxo_ref[...] + dxb
        rowd_ref[0:1, :] += d_g
        rowd_ref[1:2, :] += d_gp
        rowb_ref[0:1, :] += d_sh
        rowb_ref[1:2, :] += d_sc
        rowb_ref[2:3, :] += d_gt

    def tile(width):
        return pl.BlockSpec((None, tm, width), lambda b, i: (b, n - 1 - i, 0))

    return pl.pallas_call(
        body, name=name,
        out_shape=[jax.ShapeDtypeStruct((nb, s, d), F32), jax.ShapeDtypeStruct((nb, s, din), BF16),
                   jax.ShapeDtypeStruct((nb, s, d), BF16), jax.ShapeDtypeStruct((8, d), F32),
                   jax.ShapeDtypeStruct((nb, 8, d), F32), jax.ShapeDtypeStruct((8, dc), F32),
                   jax.ShapeDtypeStruct(cw.shape, F32), jax.ShapeDtypeStruct(pw.shape, F32)],
        grid=(nb, n),
        in_specs=[tile(d), tile(d), tile(d), tile(din), tile(dc), tile(dpool),
                  pl.BlockSpec((None, 8, d), lambda b, i: (b, 0, 0)), _full(vec_d.shape), _full(vec_c.shape),
                  _full(cw.shape), _full(pw.shape), _ANY, _ANY],
        out_specs=[tile(d), tile(din), tile(d), _full((8, d)), pl.BlockSpec((None, 8, d), lambda b, i: (b, 0, 0)),
                   _full((8, dc)), _full(cw.shape), _full(pw.shape)],
        scratch_shapes=[pltpu.VMEM((d, din), BF16), pltpu.VMEM((dmix, d), BF16),
                        pltpu.VMEM((rext, dc), F32), pltpu.VMEM((rext, dpool), F32),
                        pltpu.SemaphoreType.DMA((2 * nj,))],
        compiler_params=pltpu.CompilerParams(dimension_semantics=("arbitrary", "arbitrary"),
                                             vmem_limit_bytes=VMEM_LIMIT),
    )(dxo, x, o, u, ac, dpl, mod, vec_d, vec_c, cw, pw, win_g, wout_g)


def _ffn_forward(x, mod, vec_d, fw, wup_g, wdn_g, tm, name):
    nb, s, d = x.shape
    n = s // tm
    nj, _, ucol = wup_g.shape
    f2 = nj * ucol
    dff = f2 // 2
    rd = wdn_g.shape[1]
    nq = nj // 2
    cs = dff // nq

    def body(x_ref, mod_ref, vd_ref, fw_ref, wup_hbm, wdn_hbm,
             xo_ref, h_ref, u_ref, hid_ref, o_ref,
             wup_v, wdn_v, ext_u, sems):
        b, i = pl.program_id(0), pl.program_id(1)
        pairs = [(wup_hbm.at[j], wup_v.at[:, pl.ds(j * ucol, ucol)]) for j in range(nj)]
        pairs += [(wdn_hbm.at[j], wdn_v.at[pl.ds(j * rd, rd), :]) for j in range(nj)]
        _load_weights((b == 0) & (i == 0), pairs, sems)

        @pl.when(i == 0)
        def _():
            ext_u[0:FHALO, :] = jnp.zeros((FHALO, f2), F32)

        @pl.when(i > 0)
        def _():
            ext_u[0:FHALO, :] = ext_u[tm:tm + FHALO, :]

        xv = x_ref[...]
        h, _, _ = _ada_norm(xv, vd_ref[2:3, :], mod_ref[4:5, :], mod_ref[3:4, :])
        hb = h.astype(BF16)
        h_ref[...] = hb

        def conv(cols):
            uc = jnp.dot(hb, wup_v[:, cols], preferred_element_type=F32)
            u_ref[:, cols] = uc.astype(BF16)
            ext_u[FHALO:FHALO + tm, cols] = uc
            out = jnp.broadcast_to(fw_ref[3:4, cols], (tm, cs))
            for k in range(3):
                out = out + fw_ref[k:k + 1, cols] * ext_u[pl.ds(FHALO - 2 + k, tm), cols]
            return out

        o = jnp.zeros((tm, d), F32)
        for q in range(nq):
            val = conv(pl.ds(q * cs, cs))
            gate = conv(pl.ds(dff + q * cs, cs))
            hid = ((gate * _sigmoid(gate)) * val).astype(BF16)
            hid_ref[:, pl.ds(q * cs, cs)] = hid
            o = o + jnp.dot(hid, wdn_v[pl.ds(q * cs, cs), :], preferred_element_type=F32)
        o_ref[...] = o
        r2 = lax.rsqrt(jnp.mean(o * o, axis=-1, keepdims=True) + EPS)
        xo_ref[...] = xv + (1.0 + mod_ref[5:6, :]) * ((o * r2) * vd_ref[3:4, :])

    def tile(width):
        return pl.BlockSpec((None, tm, width), lambda b, i: (b, i, 0))

    return pl.pallas_call(
        body, name=name,
        out_shape=[jax.ShapeDtypeStruct((nb, s, d), F32), jax.ShapeDtypeStruct((nb, s, d), BF16),
                   jax.ShapeDtypeStruct((nb, s, f2), BF16), jax.ShapeDtypeStruct((nb, s, dff), BF16),
                   jax.ShapeDtypeStruct((nb, s, d), F32)],
        grid=(nb, n),
        in_specs=[tile(d), pl.BlockSpec((None, 8, d), lambda b, i: (b, 0, 0)), _full(vec_d.shape), _full(fw.shape),
                  _ANY, _ANY],
        out_specs=[tile(d), tile(d), tile(f2), tile(dff), tile(d)],
        scratch_shapes=[pltpu.VMEM((d, f2), BF16), pltpu.VMEM((dff, d), BF16),
                        pltpu.VMEM((FHALO + tm, f2), F32), pltpu.SemaphoreType.DMA((2 * nj,))],
        compiler_params=pltpu.CompilerParams(dimension_semantics=("arbitrary", "arbitrary"),
                                             vmem_limit_bytes=VMEM_LIMIT),
    )(x, mod, vec_d, fw, wup_g, wdn_g)


def _ffn_backward(dxo, x, o, u, mod, vec_d, fw, wup_g, wdn_g, tm, name):
    nb, s, d = x.shape
    n = s // tm
    nj, _, ucol = wup_g.shape
    f2 = nj * ucol
    dff = f2 // 2
    rd = wdn_g.shape[1]
    nq = nj // 2
    cs = dff // nq
    hb_per_tile = tm // FHALO

    def body(dxo_ref, x_ref, o_ref, u_ref, uh_ref, mod_ref, vd_ref, fw_ref, wup_hbm, wdn_hbm,
             dx_ref, du_ref, dob_ref, rowd_ref, rowb_ref, dfw_ref,
             wup_v, wdn_v, ext_u, ext_d, sems):
        b, i = pl.program_id(0), pl.program_id(1)
        first = (b == 0) & (i == 0)
        pairs = [(wup_hbm.at[j], wup_v.at[:, pl.ds(j * ucol, ucol)]) for j in range(nj)]
        pairs += [(wdn_hbm.at[j], wdn_v.at[pl.ds(j * rd, rd), :]) for j in range(nj)]
        _load_weights(first, pairs, sems)

        @pl.when(first)
        def _():
            rowd_ref[...] = jnp.zeros_like(rowd_ref)
            dfw_ref[...] = jnp.zeros_like(dfw_ref)

        @pl.when(i == 0)
        def _():
            rowb_ref[...] = jnp.zeros_like(rowb_ref)
            ext_d[tm:tm + FHALO, :] = jnp.zeros((FHALO, f2), F32)

        @pl.when(i > 0)
        def _():
            ext_d[tm:tm + FHALO, :] = ext_d[0:FHALO, :]

        g_pre, g_post = vd_ref[2:3, :], vd_ref[3:4, :]
        sh, sc, gt = mod_ref[3:4, :], mod_ref[4:5, :], mod_ref[5:6, :]
        do, d_gt, d_gp = _gated_residual_bwd(dxo_ref[...], o_ref[...], g_post, gt)
        dob = do.astype(BF16)
        dob_ref[...] = dob

        keep = jnp.where(i == n - 1, 0.0, 1.0)

        def conv(cols):
            ext_u[0:FHALO, 0:cs] = uh_ref[:, cols].astype(F32) * keep
            ext_u[FHALO:FHALO + tm, 0:cs] = u_ref[:, cols].astype(F32)
            out = jnp.broadcast_to(fw_ref[3:4, cols], (tm, cs))
            for k in range(3):
                out = out + fw_ref[k:k + 1, cols] * ext_u[pl.ds(FHALO - 2 + k, tm), 0:cs]
            return out

        def conv_bwd(cols, duc):
            dfw_ref[3:4, cols] += _colsum(duc)
            ext_d[0:tm, cols] = duc
            uc = u_ref[:, cols].astype(F32)
            out = jnp.zeros((tm, cs), F32)
            for k in range(3):
                sl = ext_d[pl.ds(2 - k, tm), cols]
                out = out + fw_ref[k:k + 1, cols] * sl
                dfw_ref[k:k + 1, cols] += _colsum(uc * sl)
            ob = out.astype(BF16)
            du_ref[:, cols] = ob
            return lax.dot_general(ob, wup_v[:, cols], (((1,), (1,)), ((), ())), preferred_element_type=F32)

        dh = jnp.zeros((tm, d), F32)
        for q in range(nq):
            vcols = pl.ds(q * cs, cs)
            gcols = pl.ds(dff + q * cs, cs)
            dhid = lax.dot_general(dob, wdn_v[vcols, :], (((1,), (1,)), ((), ())), preferred_element_type=F32)
            val = conv(vcols)
            gate = conv(gcols)
            sg = _sigmoid(gate)
            dval = dhid * (gate * sg)
            dgate = dhid * val * (sg * (1.0 + gate * (1.0 - sg)))
            dh = dh + conv_bwd(vcols, dval)
            dh = dh + conv_bwd(gcols, dgate)

        _, xn, r1 = _ada_norm(x_ref[...], g_pre, sc, sh)
        dxb, d_sh, d_sc, d_g = _ada_norm_bwd(dh, xn, r1, g_pre, sc)
        dx_ref[...] = dxo_ref[...] + dxb
        rowd_ref[2:3, :] += d_g
        rowd_ref[3:4, :] += d_gp
        rowb_ref[3:4, :] += d_sh
        rowb_ref[4:5, :] += d_sc
        rowb_ref[5:6, :] += d_gt

    def tile(width):
        return pl.BlockSpec((None, tm, width), lambda b, i: (b, n - 1 - i, 0))

    halo = pl.BlockSpec((None, FHALO, f2), lambda b, i: (b, jnp.maximum((n - 1 - i) * hb_per_tile - 1, 0), 0))
    return pl.pallas_call(
        body, name=name,
        out_shape=[jax.ShapeDtypeStruct((nb, s, d), F32), jax.ShapeDtypeStruct((nb, s, f2), BF16),
                   jax.ShapeDtypeStruct((nb, s, d), BF16), jax.ShapeDtypeStruct((8, d), F32),
                   jax.ShapeDtypeStruct((nb, 8, d), F32), jax.ShapeDtypeStruct(fw.shape, F32)],
        grid=(nb, n),
        in_specs=[tile(d), tile(d), tile(d), tile(f2), halo, pl.BlockSpec((None, 8, d), lambda b, i: (b, 0, 0)),
                  _full(vec_d.shape), _full(fw.shape), _ANY, _ANY],
        out_specs=[tile(d), tile(f2), tile(d), _full((8, d)), pl.BlockSpec((None, 8, d), lambda b, i: (b, 0, 0)),
                   _full(fw.shape)],
        scratch_shapes=[pltpu.VMEM((d, f2), BF16), pltpu.VMEM((dff, d), BF16),
                        pltpu.VMEM((FHALO + tm, cs), F32), pltpu.VMEM((tm + FHALO, f2), F32),
                        pltpu.SemaphoreType.DMA((2 * nj,))],
        compiler_params=pltpu.CompilerParams(dimension_semantics=("arbitrary", "arbitrary"),
                                             vmem_limit_bytes=VMEM_LIMIT),
    )(dxo, x, o, u, u, mod, vec_d, fw, wup_g, wdn_g)


def _weight_grad(a, b, nblk, split, tt, name):
    t, ka = a.shape
    nb_ = b.shape[1]
    if split == "b":
        wa, wb = ka, nb_ // nblk
        a_spec = pl.BlockSpec((tt, wa), lambda j, k: (k, 0))
        b_spec = pl.BlockSpec((tt, wb), lambda j, k: (k, j))
    else:
        wa, wb = ka // nblk, nb_
        a_spec = pl.BlockSpec((tt, wa), lambda j, k: (k, j))
        b_spec = pl.BlockSpec((tt, wb), lambda j, k: (k, 0))
    nk = t // tt

    def body(a_ref, b_ref, o_ref, acc):
        k = pl.program_id(1)

        @pl.when(k == 0)
        def _():
            acc[...] = jnp.zeros_like(acc)

        acc[...] += lax.dot_general(a_ref[...], b_ref[...], (((0,), (0,)), ((), ())), preferred_element_type=F32)

        @pl.when(k == nk - 1)
        def _():
            o_ref[...] = acc[...].astype(o_ref.dtype)

    return pl.pallas_call(
        body, name=name, out_shape=jax.ShapeDtypeStruct((nblk, wa, wb), BF16),
        grid=(nblk, nk), in_specs=[a_spec, b_spec],
        out_specs=pl.BlockSpec((None, wa, wb), lambda j, k: (j, 0, 0)),
        scratch_shapes=[pltpu.VMEM((wa, wb), F32)],
        compiler_params=pltpu.CompilerParams(dimension_semantics=("arbitrary", "arbitrary"),
                                             vmem_limit_bytes=VMEM_LIMIT),
    )(a, b)


def _loss_grad(y, tgt, tm, name):
    nb, s, d = y.shape
    n = s // tm

    def body(y_ref, t_ref, dy_ref, sq_ref):
        @pl.when((pl.program_id(0) == 0) & (pl.program_id(1) == 0))
        def _():
            sq_ref[...] = jnp.zeros_like(sq_ref)

        e = y_ref[...] - t_ref[...]
        dy_ref[...] = e * (1.0 / d)
        sq_ref[0:1, :] += _colsum(e * e)

    tile = pl.BlockSpec((None, tm, d), lambda b, i: (b, i, 0))
    return pl.pallas_call(
        body, name=name, out_shape=[jax.ShapeDtypeStruct((nb, s, d), F32), jax.ShapeDtypeStruct((8, d), F32)],
        grid=(nb, n), in_specs=[tile, tile], out_specs=[tile, _full((8, d))],
        compiler_params=pltpu.CompilerParams(dimension_semantics=("arbitrary", "arbitrary")),
    )(y, tgt)


def _rows128(a):
    return a.reshape(-1, LANES)


def _reduce_scatter_layer(gs, cidx, idx, l):
    r1 = _sibling_swap_halves(gs, name=f"rs_swap_{l}")
    ps = [_pair_sum(g, r, cidx, name=f"rs_pair_{l}_{a}") for a, (g, r) in enumerate(zip(gs, r1))]
    r2 = _chip_exchange(ps, name=f"rs_chips_{l}")
    fh = [_chip_sum(p, r, idx, name=f"rs_sum_{l}_{a}") for a, (p, r) in enumerate(zip(ps, r2))]
    fs = _sibling_share(fh, name=f"rs_share_{l}")
    return [f.reshape(f.shape[0] * f.shape[1], f.shape[2]) for f in fs]


def kernel(x, c, ada_w, ada_b, pre_mix_g, post_mix_g, w_in, conv_w, conv_b, conv_ln_g, conv_ln_b, pool_w, pool_scale, w_out, pre_ffn_g, post_ffn_g, ffn_up, ffn_conv_w, ffn_conv_b, ffn_down, loss_target, m_ada_w, m_ada_b, m_pre_mix_g, m_post_mix_g, m_w_in, m_conv_w, m_conv_b, m_conv_ln_g, m_conv_ln_b, m_pool_w, m_pool_scale, m_w_out, m_pre_ffn_g, m_post_ffn_g, m_ffn_up, m_ffn_conv_w, m_ffn_conv_b, m_ffn_down, v_ada_w, v_ada_b, v_pre_mix_g, v_post_mix_g, v_w_in, v_conv_w, v_conv_b, v_conv_ln_g, v_conv_ln_b, v_pool_w, v_pool_scale, v_w_out, v_pre_ffn_g, v_post_ffn_g, v_ffn_up, v_ffn_conv_w, v_ffn_conv_b, v_ffn_down):
    nb, s, d = x.shape
    nl = w_in.shape[0]
    taps = conv_w.shape[1]
    ccol = conv_w.shape[2]
    dc = conv_b.shape[1]
    fcol = ffn_conv_w.shape[2]
    f2 = ffn_conv_b.shape[1]
    nmod = ada_b.shape[1] // d
    acol = ada_w.shape[2]
    tm = min(256, s)
    tt = min(512, nb * s)

    xi, yi, ci = _pos()
    jm = 2 * xi + yi
    me = 4 * xi + 2 * yi + ci
    cidx = jnp.reshape(ci, (1,)).astype(jnp.int32)
    idx = jnp.stack([jm, ci]).astype(jnp.int32)

    n_cw, n_fw, n_c = nl * taps * ccol, nl * 3 * fcol, nb * d
    packed = jnp.concatenate([conv_w.reshape(-1), ffn_conv_w.reshape(-1), c.reshape(-1)])
    got = _gather8(_rows128(packed), name="gather_small").reshape(N_DEV, -1)
    chips = got[0::2]
    cw_full = chips[:, :n_cw].reshape(N_CHIPS, nl, taps, ccol).transpose(1, 2, 0, 3).reshape(nl, taps, dc)
    fw_full = chips[:, n_cw:n_cw + n_fw].reshape(N_CHIPS, nl, 3, fcol).transpose(1, 2, 0, 3).reshape(nl, 3, f2)
    c_all = got[:, n_cw + n_fw:].reshape(N_DEV * nb, d)

    ada_b_cols = lax.dynamic_slice_in_dim(ada_b, jm * acol, acol, axis=1).reshape(nl, 1, acol)
    mod_cols = _ada_forward(c_all, ada_w, ada_b_cols, name="ada_forward")
    mod_all = _gather8(_rows128(mod_cols), name="gather_mod").reshape(N_DEV, nl, N_DEV * nb, acol)
    mod_all = mod_all[0::2].transpose(1, 2, 0, 3).reshape(nl, N_DEV * nb, N_CHIPS * acol)
    mod_own = lax.dynamic_slice_in_dim(mod_all, me * nb, nb, axis=1).reshape(nl, nb, nmod, d)
    mod_own = jnp.pad(mod_own, ((0, 0), (0, 0), (0, 8 - nmod), (0, 0)))

    vec_d = jnp.stack([pre_mix_g, post_mix_g, pre_ffn_g, post_ffn_g], axis=1)
    vec_c = jnp.stack([conv_b, conv_ln_g, conv_ln_b, pool_scale], axis=1)
    cw_pad = jnp.pad(cw_full, ((0, 0), (0, HALO - taps), (0, 0)))
    fw_rows = jnp.concatenate([fw_full, ffn_conv_b[:, None, :], jnp.zeros((nl, 4, f2), F32)], axis=1)
    pw_b = pool_w.astype(BF16)

    win_b, wout_b, wup_b, wdn_b = (w.astype(BF16) for w in (w_in, w_out, ffn_up, ffn_down))

    saved = []
    xs = x
    for l in range(nl):
        win_g, wout_g, wup_g, wdn_g = _weight_gather([win_b[l], wout_b[l], wup_b[l], wdn_b[l]], name=f"weight_gather_{l}")
        x1, h1, u1, ac1, dp1, z1, o1 = _mixer_forward(xs, mod_own[l], vec_d[l], vec_c[l], cw_pad[l], pw_b[l], win_g, wout_g,
                                                      taps, tm, name=f"mixer_fwd_{l}")
        x2, h2, u2, hid2, o2 = _ffn_forward(x1, mod_own[l], vec_d[l], fw_rows[l], wup_g, wdn_g, tm, name=f"ffn_fwd_{l}")
        saved.append((xs, h1, u1, ac1, dp1, z1, o1, x1, h2, u2, hid2, o2, win_g, wout_g, wup_g, wdn_g))
        xs = x2

    dx, sq = _loss_grad(xs, loss_target, tm, name="loss_grad")
    loss = lax.psum(0.5 * jnp.sum(sq) / d, ("x", "y", "c"))

    flat = lambda a: a.reshape(nb * s, a.shape[-1])
    small = [None] * nl
    big = [None] * nl
    for l in reversed(range(nl)):
        x0, h1, u1, ac1, dp1, z1, o1, x1, h2, u2, hid2, o2, win_g, wout_g, wup_g, wdn_g = saved[l]
        dx, du2, do2, rowd2, rowb2, dfw = _ffn_backward(dx, x1, o2, u2, mod_own[l], vec_d[l], fw_rows[l], wup_g, wdn_g, tm,
                                                        name=f"ffn_bwd_{l}")
        g_up = _weight_grad(flat(h2), flat(du2), N_CHIPS, "b", tt, name=f"grad_ffn_up_{l}")
        g_dn = _weight_grad(flat(hid2), flat(do2), 2, "a", tt, name=f"grad_ffn_down_{l}")
        g_dn = g_dn.reshape(N_CHIPS, g_dn.shape[1] // 2, d)
        dx, du1, do1, rowd1, rowb1, rowc, dcw, dpw = _mixer_backward(dx, x0, o1, u1, ac1, dp1, mod_own[l], vec_d[l], vec_c[l],
                                                                     cw_pad[l], pw_b[l], win_g, wout_g, taps, tm,
                                                                     name=f"mixer_bwd_{l}")
        g_in = _weight_grad(flat(h1), flat(du1), N_CHIPS, "b", tt, name=f"grad_w_in_{l}")
        g_out = _weight_grad(flat(z1), flat(do1), N_CHIPS, "a", tt, name=f"grad_w_out_{l}")
        big[l] = _reduce_scatter_layer([g_in, g_out, g_up, g_dn], cidx, idx, l)
        rowd = rowd1 + rowd2
        rowb = rowb1 + rowb2
        small[l] = dict(rowd=rowd, rowb=rowb, rowc=rowc, dcw=dcw[:taps], dpw=dpw, dfw=dfw)

    dmod_own = jnp.stack([small[l]["rowb"][:, :nmod, :] for l in range(nl)])
    dmod_all = _gather8(_rows128(dmod_own), name="gather_dmod").reshape(N_DEV, nl, nb, nmod * d)
    dmod_all = dmod_all.transpose(1, 0, 2, 3).reshape(nl, N_DEV * nb, nmod * d)
    dmod_cols = lax.dynamic_slice_in_dim(dmod_all, jm * acol, acol, axis=2)
    g_ada_w, d_ada_w, nm_ada_w, nv_ada_w = _ada_update(c_all, dmod_cols, ada_w, m_ada_w, v_ada_w, name="ada_update")

    def st(key, row=None):
        return jnp.stack([small[l][key] if row is None else small[l][key][row] for l in range(nl)])

    local = {
        "ada_b": dmod_own.sum(axis=1).reshape(nl, nmod * d),
        "pre_mix_g": st("rowd", 0), "post_mix_g": st("rowd", 1),
        "conv_b": st("rowc", 0), "conv_ln_g": st("rowc", 1), "conv_ln_b": st("rowc", 2),
        "pool_w": st("dpw"), "pool_scale": st("rowc", 3),
        "pre_ffn_g": st("rowd", 2), "post_ffn_g": st("rowd", 3),
        "ffn_conv_b": st("dfw", 3), "conv_w": st("dcw"), "ffn_conv_w": jnp.stack([small[l]["dfw"][:3] for l in range(nl)]),
    }
    names = list(local)
    sizes = [local[k].size for k in names]
    summed = _allreduce8(_rows128(jnp.concatenate([local[k].reshape(-1) for k in names])), name="allreduce_small").reshape(-1)
    grads, off = {}, 0
    for k, sz in zip(names, sizes):
        grads[k] = summed[off:off + sz].reshape(local[k].shape)
        off += sz
    grads["conv_w"] = lax.dynamic_slice_in_dim(grads["conv_w"], jm * ccol, ccol, axis=2)
    grads["ffn_conv_w"] = lax.dynamic_slice_in_dim(grads["ffn_conv_w"], jm * fcol, fcol, axis=2)

    params = dict(ada_b=(ada_b, m_ada_b, v_ada_b), pre_mix_g=(pre_mix_g, m_pre_mix_g, v_pre_mix_g),
                  post_mix_g=(post_mix_g, m_post_mix_g, v_post_mix_g), conv_b=(conv_b, m_conv_b, v_conv_b),
                  conv_ln_g=(conv_ln_g, m_conv_ln_g, v_conv_ln_g), conv_ln_b=(conv_ln_b, m_conv_ln_b, v_conv_ln_b),
                  pool_w=(pool_w, m_pool_w, v_pool_w), pool_scale=(pool_scale, m_pool_scale, v_pool_scale),
                  pre_ffn_g=(pre_ffn_g, m_pre_ffn_g, v_pre_ffn_g), post_ffn_g=(post_ffn_g, m_post_ffn_g, v_post_ffn_g),
                  ffn_conv_b=(ffn_conv_b, m_ffn_conv_b, v_ffn_conv_b), conv_w=(conv_w, m_conv_w, v_conv_w),
                  ffn_conv_w=(ffn_conv_w, m_ffn_conv_w, v_ffn_conv_w))
    pack = lambda i, g=None: _rows128(jnp.concatenate([(grads[k] if g else params[k][i]).reshape(-1) for k in names]))
    sd, sm, sv = _adamw_flat(pack(0), pack(0, True), pack(1), pack(2), name="adamw_small")
    outs = {}
    off = 0
    for k in names:
        shape, sz = params[k][0].shape, params[k][0].size
        outs[k] = (grads[k],) + tuple(a.reshape(-1)[off:off + sz].reshape(shape) for a in (sd, sm, sv))
        off += sz

    outs["ada_w"] = (g_ada_w, d_ada_w, nm_ada_w, nv_ada_w)
    for a, (k, w, m, v) in enumerate([("w_in", w_in, m_w_in, v_w_in), ("w_out", w_out, m_w_out, v_w_out),
                                      ("ffn_up", ffn_up, m_ffn_up, v_ffn_up), ("ffn_down", ffn_down, m_ffn_down, v_ffn_down)]):
        outs[k] = tuple(_adamw_layers(w, m, v, [big[l][a] for l in range(nl)], name=f"adamw_{k}"))

    order = ["ada_w", "ada_b", "pre_mix_g", "post_mix_g", "w_in", "conv_w", "conv_b", "conv_ln_g", "conv_ln_b", "pool_w",
             "pool_scale", "w_out", "pre_ffn_g", "post_ffn_g", "ffn_up", "ffn_conv_w", "ffn_conv_b", "ffn_down"]
    return (loss, dx) + tuple(outs[k][i] for i in range(4) for k in order)
```

```python
import functools

import jax
import jax.numpy as jnp
from jax import lax
from jax.experimental import pallas as pl
from jax.experimental.pallas import tpu as pltpu

F32 = jnp.float32
BF16 = jnp.bfloat16
MESH = pl.DeviceIdType.MESH

EPS = 1e-6
POOL_WINDOWS = (2, 4, 8, 16)
ADAM_LR = 0.001
ADAM_B1 = 0.9
ADAM_B2 = 0.999
ADAM_EPS = 1e-08
ADAM_WD = 0.01
ADAM_STEP = 10

N_CHIPS = 4
N_DEV = 8
LANES = 128
HALO = 32
FHALO = 8
VMEM_LIMIT = 60 * 1024 * 1024


def _pos():
    return lax.axis_index("x"), lax.axis_index("y"), lax.axis_index("c")


def _flip(v, f):
    return 1 - v if f else v


def _full(shape):
    nd = len(shape)
    return pl.BlockSpec(shape, lambda *_: (0,) * nd)


_ANY = pl.BlockSpec(memory_space=pl.ANY)
_VMEM = pl.BlockSpec(memory_space=pltpu.VMEM)


def _sigmoid(v):
    return 1.0 / (1.0 + jnp.exp(-v))


def _colsum(v):
    return jnp.sum(v, axis=0, keepdims=True)


def _gather8(v, name):
    r, ccols = v.shape

    def body(v_ref, out_ref, send_sems, recv_sems, local_sem):
        x, y, c = _pos()
        me = 4 * x + 2 * y + c
        mine = pltpu.make_async_copy(v_ref, out_ref.at[me], local_sem)
        mine.start()
        peers = [(_flip(x, (k >> 2) & 1), _flip(y, (k >> 1) & 1), _flip(c, k & 1)) for k in range(1, N_DEV)]
        sends = []
        for k, peer in enumerate(peers):
            cp = pltpu.make_async_remote_copy(src_ref=v_ref, dst_ref=out_ref.at[me], send_sem=send_sems.at[k],
                                              recv_sem=recv_sems.at[k], device_id=peer, device_id_type=MESH)
            cp.start()
            sends.append(cp)
        for k, peer in enumerate(peers):
            pidx = 4 * peer[0] + 2 * peer[1] + peer[2]
            pltpu.make_async_remote_copy(src_ref=v_ref, dst_ref=out_ref.at[pidx], send_sem=send_sems.at[k],
                                         recv_sem=recv_sems.at[k], device_id=peer, device_id_type=MESH).wait_recv()
        for cp in sends:
            cp.wait_send()
        mine.wait()

    return pl.pallas_call(
        body, name=name, out_shape=jax.ShapeDtypeStruct((N_DEV, r, ccols), v.dtype),
        in_specs=[_VMEM], out_specs=_VMEM,
        scratch_shapes=[pltpu.SemaphoreType.DMA((N_DEV - 1,)), pltpu.SemaphoreType.DMA((N_DEV - 1,)),
                        pltpu.SemaphoreType.DMA(())],
        compiler_params=pltpu.CompilerParams(vmem_limit_bytes=VMEM_LIMIT),
    )(v)


def _allreduce8(v, name):
    r, ccols = v.shape

    def body(v_ref, out_ref, rbuf, send_sems, recv_sems):
        x, y, c = _pos()
        out_ref[...] = v_ref[...]
        peers = [(x, y, 1 - c), (1 - x, y, c), (x, 1 - y, c)]
        for s, peer in enumerate(peers):
            cp = pltpu.make_async_remote_copy(src_ref=out_ref, dst_ref=rbuf.at[s], send_sem=send_sems.at[s],
                                              recv_sem=recv_sems.at[s], device_id=peer, device_id_type=MESH)
            cp.start()
            cp.wait()
            out_ref[...] = out_ref[...] + rbuf[s]

    return pl.pallas_call(
        body, name=name, out_shape=jax.ShapeDtypeStruct((r, ccols), v.dtype),
        in_specs=[_VMEM], out_specs=_VMEM,
        scratch_shapes=[pltpu.VMEM((3, r, ccols), v.dtype), pltpu.SemaphoreType.DMA((3,)), pltpu.SemaphoreType.DMA((3,))],
        compiler_params=pltpu.CompilerParams(vmem_limit_bytes=VMEM_LIMIT),
    )(v)


def _chip_peers(x, y, c):
    peers = [(_flip(x, (k >> 1) & 1), _flip(y, k & 1), c) for k in range(1, N_CHIPS)]
    return peers, [2 * p[0] + p[1] for p in peers]


class _Exchange:
    def __init__(self, ins, outs, aliases, n_sems, n_local, start, finish):
        self.ins, self.outs, self.aliases = list(ins), list(outs), dict(aliases)
        self.n_sems, self.n_local, self.start, self.finish = n_sems, n_local, start, finish

    def scratch(self):
        return [pltpu.SemaphoreType.DMA((self.n_sems,)), pltpu.SemaphoreType.DMA((self.n_sems,)),
                pltpu.SemaphoreType.DMA((max(self.n_local, 1),))]


def _remote(src, dst, send_sems, recv_sems, k, peer):
    return pltpu.make_async_remote_copy(src_ref=src, dst_ref=dst, send_sem=send_sems.at[k], recv_sem=recv_sems.at[k],
                                        device_id=peer, device_id_type=MESH)


def _gather_chips(shards):
    n = len(shards)
    per = N_CHIPS - 1

    def copies(ins, outs, send_sems, recv_sems):
        x, y, c = _pos()
        jm = 2 * x + y
        peers, pjs = _chip_peers(x, y, c)
        sends, recvs = [], []
        for a in range(n):
            hr = shards[a].shape[0] // 2
            rows = pl.ds(c * hr, hr)
            for k, peer in enumerate(peers):
                sends.append(_remote(ins[a].at[rows], outs[a].at[jm, rows], send_sems, recv_sems, a * per + k, peer))
                landed = outs[a].at[pjs[k], rows]
                recvs.append(_remote(landed, landed, send_sems, recv_sems, a * per + k, peer))
        return sends, recvs

    def local(ins, outs, local_sems):
        x, y, _ = _pos()
        return [pltpu.make_async_copy(ins[a], outs[a].at[2 * x + y], local_sems.at[a]) for a in range(n)]

    def start(ins, outs, send_sems, recv_sems, local_sems):
        for cp in local(ins, outs, local_sems) + copies(ins, outs, send_sems, recv_sems)[0]:
            cp.start()

    def finish(ins, outs, send_sems, recv_sems, local_sems):
        sends, recvs = copies(ins, outs, send_sems, recv_sems)
        for cp in recvs:
            cp.wait_recv()
        for cp in sends:
            cp.wait_send()
        for cp in local(ins, outs, local_sems):
            cp.wait()

    outs = [jax.ShapeDtypeStruct((N_CHIPS,) + s.shape, s.dtype) for s in shards]
    return _Exchange(shards, outs, {}, n * per, n, start, finish)


def _gather_sibling(bufs):
    n = len(bufs)
    per = N_CHIPS - 1

    def copies(outs, send_sems, recv_sems):
        x, y, c = _pos()
        sib = (x, y, 1 - c)
        _, pjs = _chip_peers(x, y, c)
        sends, recvs = [], []
        for a in range(n):
            hr = bufs[a].shape[1] // 2
            for k in range(per):
                mine = outs[a].at[pjs[k], pl.ds(c * hr, hr)]
                theirs = outs[a].at[pjs[k], pl.ds((1 - c) * hr, hr)]
                sends.append(_remote(mine, mine, send_sems, recv_sems, a * per + k, sib))
                recvs.append(_remote(theirs, theirs, send_sems, recv_sems, a * per + k, sib))
        return sends, recvs

    def start(ins, outs, send_sems, recv_sems, local_sems):
        for cp in copies(outs, send_sems, recv_sems)[0]:
            cp.start()

    def finish(ins, outs, send_sems, recv_sems, local_sems):
        sends, recvs = copies(outs, send_sems, recv_sems)
        for cp in recvs:
            cp.wait_recv()
        for cp in sends:
            cp.wait_send()

    outs = [jax.ShapeDtypeStruct(b.shape, b.dtype) for b in bufs]
    return _Exchange(bufs, outs, {a: a for a in range(n)}, n * per, 0, start, finish)


def _swap_halves(gs):
    n = len(gs)

    def copies(ins, outs, send_sems, recv_sems):
        x, y, c = _pos()
        sib = (x, y, 1 - c)
        cps = []
        for a in range(n):
            hr = gs[a].shape[1] // 2
            cps.append(_remote(ins[a].at[:, pl.ds((1 - c) * hr, hr), :], outs[a], send_sems, recv_sems, a, sib))
        return cps

    def start(ins, outs, send_sems, recv_sems, local_sems):
        for cp in copies(ins, outs, send_sems, recv_sems):
            cp.start()

    def finish(ins, outs, send_sems, recv_sems, local_sems):
        for cp in copies(ins, outs, send_sems, recv_sems):
            cp.wait()

    outs = [jax.ShapeDtypeStruct((g.shape[0], g.shape[1] // 2, g.shape[2]), g.dtype) for g in gs]
    return _Exchange(gs, outs, {}, n, 0, start, finish)


def _chip_exchange(ps):
    n = len(ps)
    per = N_CHIPS - 1

    def copies(ins, outs, send_sems, recv_sems):
        x, y, c = _pos()
        peers, pjs = _chip_peers(x, y, c)
        return [_remote(ins[a].at[pjs[k]], outs[a].at[k], send_sems, recv_sems, a * per + k, peer)
                for a in range(n) for k, peer in enumerate(peers)]

    def start(ins, outs, send_sems, recv_sems, local_sems):
        for cp in copies(ins, outs, send_sems, recv_sems):
            cp.start()

    def finish(ins, outs, send_sems, recv_sems, local_sems):
        for cp in copies(ins, outs, send_sems, recv_sems):
            cp.wait()

    outs = [jax.ShapeDtypeStruct((per,) + p.shape[1:], p.dtype) for p in ps]
    return _Exchange(ps, outs, {}, n * per, 0, start, finish)


def _sibling_share(fs):
    n = len(fs)

    def copies(outs, send_sems, recv_sems):
        x, y, c = _pos()
        sib = (x, y, 1 - c)
        sends = [_remote(outs[a].at[c], outs[a].at[c], send_sems, recv_sems, a, sib) for a in range(n)]
        recvs = [_remote(outs[a].at[1 - c], outs[a].at[1 - c], send_sems, recv_sems, a, sib) for a in range(n)]
        return sends, recvs

    def start(ins, outs, send_sems, recv_sems, local_sems):
        for cp in copies(outs, send_sems, recv_sems)[0]:
            cp.start()

    def finish(ins, outs, send_sems, recv_sems, local_sems):
        sends, recvs = copies(outs, send_sems, recv_sems)
        for cp in recvs:
            cp.wait_recv()
        for cp in sends:
            cp.wait_send()

    outs = [jax.ShapeDtypeStruct(f.shape, f.dtype) for f in fs]
    return _Exchange(fs, outs, {a: a for a in range(n)}, n, 0, start, finish)


def _run_exchange(ex, name):
    ni, no = len(ex.ins), len(ex.outs)

    def body(*refs):
        ins, outs, sems = refs[:ni], refs[ni:ni + no], refs[ni + no:]
        ex.start(ins, outs, *sems)
        ex.finish(ins, outs, *sems)

    return pl.pallas_call(
        body, name=name, out_shape=ex.outs, in_specs=[_ANY] * ni, out_specs=[_ANY] * no,
        input_output_aliases=ex.aliases, scratch_shapes=ex.scratch(),
    )(*ex.ins)


def _grid_call(body, name, grid, in_specs, out_specs, out_shape, scratch_shapes, args, exchange=None):
    ni, no = len(in_specs), len(out_specs)
    params = pltpu.CompilerParams(dimension_semantics=("arbitrary",) * len(grid), vmem_limit_bytes=VMEM_LIMIT)
    if exchange is None:
        outs = pl.pallas_call(body, name=name, grid=grid, in_specs=in_specs, out_specs=out_specs, out_shape=out_shape,
                              scratch_shapes=scratch_shapes, compiler_params=params)(*args)
        return list(outs), []
    ex = exchange
    nci, nco = len(ex.ins), len(ex.outs)

    def hosted(*refs):
        cin = refs[ni:ni + nci]
        cout = refs[ni + nci + no:ni + nci + no + nco]
        sems = refs[len(refs) - 3:]
        main = refs[:ni] + refs[ni + nci:ni + nci + no] + refs[ni + nci + no + nco:len(refs) - 3]
        ids = [pl.program_id(a) for a in range(len(grid))]
        first = functools.reduce(lambda p, q: p & q, [i == 0 for i in ids])
        last = functools.reduce(lambda p, q: p & q, [i == g - 1 for i, g in zip(ids, grid)])

        @pl.when(first)
        def _():
            ex.start(cin, cout, *sems)

        body(*main)

        @pl.when(last)
        def _():
            ex.finish(cin, cout, *sems)

    outs = pl.pallas_call(
        hosted, name=name, grid=grid, in_specs=list(in_specs) + [_ANY] * nci, out_specs=list(out_specs) + [_ANY] * nco,
        out_shape=list(out_shape) + ex.outs, scratch_shapes=list(scratch_shapes) + ex.scratch(),
        input_output_aliases={ni + a: no + b for a, b in ex.aliases.items()}, compiler_params=params,
    )(*args, *ex.ins)
    return list(outs[:no]), list(outs[no:])


def _row_tile(rows, cols, itemsize, budget=2 * 1024 * 1024):
    best = None
    for t in range(16, rows + 1, 16):
        if rows % t == 0 and t * cols * itemsize <= budget:
            best = t
    return best if best is not None else rows


def _pair_sum(g, r1, cidx, name):
    nj, r, ccols = g.shape
    hr = r // 2
    tr = _row_tile(hr, ccols, 4)
    nt = hr // tr

    def body(c_ref, g_ref, r_ref, o_ref):
        o_ref[...] = (g_ref[...].astype(F32) + r_ref[...].astype(F32)).astype(o_ref.dtype)

    return pl.pallas_call(
        body, name=name, out_shape=jax.ShapeDtypeStruct((nj, hr, ccols), g.dtype),
        grid_spec=pltpu.PrefetchScalarGridSpec(
            num_scalar_prefetch=1, grid=(nj, nt),
            in_specs=[pl.BlockSpec((None, tr, ccols), lambda j, i, c_ref: (j, c_ref[0] * nt + i, 0)),
                      pl.BlockSpec((None, tr, ccols), lambda j, i, c_ref: (j, i, 0))],
            out_specs=pl.BlockSpec((None, tr, ccols), lambda j, i, c_ref: (j, i, 0))),
        compiler_params=pltpu.CompilerParams(dimension_semantics=("arbitrary", "arbitrary")),
    )(cidx, g, r1)


def _chip_sum(p, r2, idx, name):
    nj, hr, ccols = p.shape
    tr = _row_tile(hr, ccols, 4)
    nt = hr // tr

    def body(i_ref, p_ref, r_ref, o_ref):
        s = p_ref[...].astype(F32)
        for k in range(N_CHIPS - 1):
            s = s + r_ref[k].astype(F32)
        o_ref[...] = s

    return pl.pallas_call(
        body, name=name, out_shape=jax.ShapeDtypeStruct((2, hr, ccols), F32),
        grid_spec=pltpu.PrefetchScalarGridSpec(
            num_scalar_prefetch=1, grid=(nt,),
            in_specs=[pl.BlockSpec((None, tr, ccols), lambda i, i_ref: (i_ref[0], i, 0)),
                      pl.BlockSpec((N_CHIPS - 1, tr, ccols), lambda i, i_ref: (0, i, 0))],
            out_specs=pl.BlockSpec((None, tr, ccols), lambda i, i_ref: (i_ref[1], i, 0))),
        compiler_params=pltpu.CompilerParams(dimension_semantics=("arbitrary",)),
    )(idx, p, r2)


def _adam_math(w, g, m, v):
    m2 = ADAM_B1 * m + (1.0 - ADAM_B1) * g
    v2 = ADAM_B2 * v + (1.0 - ADAM_B2) * (g * g)
    m_hat = m2 / (1.0 - ADAM_B1 ** ADAM_STEP)
    v_hat = v2 / (1.0 - ADAM_B2 ** ADAM_STEP)
    delta = -ADAM_LR * (m_hat / (jnp.sqrt(v_hat) + ADAM_EPS) + ADAM_WD * w)
    return delta, m2, v2


def _adamw_layers(w, m, v, gs, name):
    nl, r, ccols = w.shape
    tr = _row_tile(r, ccols, 4, budget=1024 * 1024)
    nt = r // tr

    def body(w_ref, m_ref, v_ref, *rest):
        g_refs, (go_ref, d_ref, mo_ref, vo_ref) = rest[:nl], rest[nl:]
        l = pl.program_id(0)
        g = jnp.zeros((tr, ccols), F32)
        for k in range(nl):
            g = jnp.where(l == k, g_refs[k][...], g)
        delta, m2, v2 = _adam_math(w_ref[...], g, m_ref[...], v_ref[...])
        go_ref[...] = g
        d_ref[...] = delta
        mo_ref[...] = m2
        vo_ref[...] = v2

    big = pl.BlockSpec((None, tr, ccols), lambda l, i: (l, i, 0))

    def gspec(k):
        return pl.BlockSpec((tr, ccols), lambda l, i: (jnp.where(l == k, i, jnp.where(l < k, 0, nt - 1)), 0))

    return pl.pallas_call(
        body, name=name, out_shape=[jax.ShapeDtypeStruct(w.shape, F32)] * 4,
        grid=(nl, nt),
        in_specs=[big, big, big] + [gspec(k) for k in range(nl)],
        out_specs=[big, big, big, big],
        compiler_params=pltpu.CompilerParams(dimension_semantics=("arbitrary", "arbitrary")),
    )(w, m, v, *gs)


def _adamw_flat(w, g, m, v, name):
    r, ccols = w.shape

    def body(w_ref, g_ref, m_ref, v_ref, d_ref, mo_ref, vo_ref):
        delta, m2, v2 = _adam_math(w_ref[...], g_ref[...], m_ref[...], v_ref[...])
        d_ref[...] = delta
        mo_ref[...] = m2
        vo_ref[...] = v2

    return pl.pallas_call(
        body, name=name, out_shape=[jax.ShapeDtypeStruct((r, ccols), F32)] * 3,
        in_specs=[_VMEM] * 4, out_specs=[_VMEM] * 3,
        compiler_params=pltpu.CompilerParams(vmem_limit_bytes=VMEM_LIMIT),
    )(w, g, m, v)


def _ada_forward(c_all, ada_w, ada_b_cols, name):
    nl, d, ncols = ada_w.shape
    bg = c_all.shape[0]
    tn = 512 if ncols % 512 == 0 else ncols

    def body(c_ref, w_ref, b_ref, o_ref):
        cv = c_ref[...]
        ca = (cv * _sigmoid(cv)).astype(BF16)
        o_ref[...] = jnp.dot(ca, w_ref[...].astype(BF16), preferred_element_type=F32) + b_ref[...]

    return pl.pallas_call(
        body, name=name, out_shape=jax.ShapeDtypeStruct((nl, bg, ncols), F32),
        grid=(nl, ncols // tn),
        in_specs=[pl.BlockSpec((bg, d), lambda l, j: (0, 0)),
                  pl.BlockSpec((None, d, tn), lambda l, j: (l, 0, j)),
                  pl.BlockSpec((None, 1, tn), lambda l, j: (l, 0, j))],
        out_specs=pl.BlockSpec((None, bg, tn), lambda l, j: (l, 0, j)),
        compiler_params=pltpu.CompilerParams(dimension_semantics=("arbitrary", "arbitrary")),
    )(c_all, ada_w, ada_b_cols)


def _ada_update(c_all, dmod_cols, w, m, v, name):
    nl, d, ncols = w.shape
    bg = c_all.shape[0]
    tn = 512 if ncols % 512 == 0 else ncols

    def body(c_ref, dm_ref, w_ref, m_ref, v_ref, go_ref, d_ref, mo_ref, vo_ref):
        cv = c_ref[...]
        ca = (cv * _sigmoid(cv)).astype(BF16)
        g = lax.dot_general(ca, dm_ref[...].astype(BF16), (((0,), (0,)), ((), ())), preferred_element_type=F32)
        delta, m2, v2 = _adam_math(w_ref[...], g, m_ref[...], v_ref[...])
        go_ref[...] = g
        d_ref[...] = delta
        mo_ref[...] = m2
        vo_ref[...] = v2

    big = pl.BlockSpec((None, d, tn), lambda l, j: (l, 0, j))
    return pl.pallas_call(
        body, name=name, out_shape=[jax.ShapeDtypeStruct(w.shape, F32)] * 4,
        grid=(nl, ncols // tn),
        in_specs=[pl.BlockSpec((bg, d), lambda l, j: (0, 0)),
                  pl.BlockSpec((None, bg, tn), lambda l, j: (l, 0, j)), big, big, big],
        out_specs=[big, big, big, big],
        compiler_params=pltpu.CompilerParams(dimension_semantics=("arbitrary", "arbitrary"),
                                             vmem_limit_bytes=VMEM_LIMIT),
    )(c_all, dmod_cols, w, m, v)


def _load_weights(first, pairs, sems):
    @pl.when(first)
    def _():
        cps = [pltpu.make_async_copy(src, dst, sems.at[k]) for k, (src, dst) in enumerate(pairs)]
        for cp in cps:
            cp.start()
        for cp in cps:
            cp.wait()


def _ada_norm(xv, g, sc, sh):
    r = lax.rsqrt(jnp.mean(xv * xv, axis=-1, keepdims=True) + EPS)
    xn = xv * r
    return (xn * g) * (1.0 + sc) + sh, xn, r


def _ada_norm_bwd(dh, xn, r, g, sc):
    d_sh = _colsum(dh)
    d_sc = _colsum(dh * (xn * g))
    dxg = dh * (1.0 + sc)
    d_g = _colsum(dxg * xn)
    gd = dxg * g
    dx = r * (gd - xn * jnp.mean(gd * xn, axis=-1, keepdims=True))
    return dx, d_sh, d_sc, d_g


def _gated_residual_bwd(dxo, o, g_post, gt):
    r = lax.rsqrt(jnp.mean(o * o, axis=-1, keepdims=True) + EPS)
    on = o * r
    d_gt = _colsum(dxo * (on * g_post))
    dy = dxo * (1.0 + gt)
    d_gp = _colsum(dy * on)
    gd = dy * g_post
    do = r * (gd - on * jnp.mean(gd * on, axis=-1, keepdims=True))
    return do, d_gt, d_gp


def _seq_positions(i, tm, width):
    return i * tm + lax.broadcasted_iota(jnp.int32, (tm, width), 0)


def _mixer_forward(x, mod, vec_d, vec_c, cw, pw, win_g, wout_g, taps, tm, name, exchange=None):
    nb, s, d = x.shape
    n = s // tm
    nj, _, dcol = win_g.shape
    din = nj * dcol
    dc = vec_c.shape[-1]
    dpool = din - 2 * dc
    dmix = dc + dpool
    ro = wout_g.shape[1]
    ngrp = dpool // LANES

    def body(x_ref, mod_ref, vd_ref, vc_ref, cw_ref, pw_ref, win_hbm, wout_hbm,
             xo_ref, h_ref, u_ref, ac_ref, dp_ref, z_ref, o_ref,
             win_v, wout_v, ext_a, ext_p, sems):
        b, i = pl.program_id(0), pl.program_id(1)
        pairs = [(win_hbm.at[j], win_v.at[:, pl.ds(j * dcol, dcol)]) for j in range(nj)]
        pairs += [(wout_hbm.at[j], wout_v.at[pl.ds(j * ro, ro), :]) for j in range(nj)]
        _load_weights((b == 0) & (i == 0), pairs, sems)

        xv = x_ref[...]
        h, _, _ = _ada_norm(xv, vd_ref[0:1, :], mod_ref[1:2, :], mod_ref[0:1, :])
        hb = h.astype(BF16)
        h_ref[...] = hb
        u = jnp.dot(hb, win_v[...], preferred_element_type=F32)
        u_ref[...] = u.astype(BF16)
        ag = u[:, :dc] * _sigmoid(u[:, dc:2 * dc])
        up = u[:, 2 * dc:]

        @pl.when(i == 0)
        def _():
            ext_a[0:HALO, :] = jnp.zeros((HALO, dc), F32)
            ext_p[0:HALO, :] = jnp.zeros((HALO, dpool), F32)

        @pl.when(i > 0)
        def _():
            ext_a[0:HALO, :] = ext_a[tm:tm + HALO, :]
            ext_p[0:HALO, :] = ext_p[tm:tm + HALO, :]

        ext_a[HALO:HALO + tm, :] = ag
        ext_p[HALO:HALO + tm, :] = up

        acc = jnp.broadcast_to(vc_ref[0:1, :], (tm, dc))
        for k in range(taps):
            acc = acc + cw_ref[k:k + 1, :] * ext_a[pl.ds(HALO - (taps - 1) + k, tm), :]
        ac_ref[...] = acc.astype(BF16)
        mu = jnp.mean(acc, axis=-1, keepdims=True)
        xc = acc - mu
        var = jnp.mean(xc * xc, axis=-1, keepdims=True)
        al = (xc * lax.rsqrt(var + EPS)) * vc_ref[1:2, :] + vc_ref[2:3, :]
        a = al * _sigmoid(al)

        pos = _seq_positions(i, tm, LANES)
        parts = [a.astype(BF16)]
        for g in range(ngrp):
            w = POOL_WINDOWS[g]
            cols = slice(g * LANES, (g + 1) * LANES)
            sw = ext_p[:, cols]
            step = 1
            while step < w:
                sw = sw + pltpu.roll(sw, step, axis=0)
                step *= 2
            cnt = jnp.minimum(pos + 1, w).astype(F32)
            dg = (sw[HALO:, :] / cnt - up[:, cols]).astype(BF16)
            dp_ref[:, cols] = dg
            q = jnp.dot(dg, pw_ref[g], preferred_element_type=F32)
            parts.append((q * vc_ref[3:4, cols]).astype(BF16))
        z = jnp.concatenate(parts, axis=-1)
        z_ref[...] = z
        o = jnp.dot(z, wout_v[...], preferred_element_type=F32)
        o_ref[...] = o
        r2 = lax.rsqrt(jnp.mean(o * o, axis=-1, keepdims=True) + EPS)
        xo_ref[...] = xv + (1.0 + mod_ref[2:3, :]) * ((o * r2) * vd_ref[1:2, :])

    def tile(width):
        return pl.BlockSpec((None, tm, width), lambda b, i: (b, i, 0))

    return _grid_call(
        body, name, (nb, n),
        in_specs=[tile(d), pl.BlockSpec((None, 8, d), lambda b, i: (b, 0, 0)), _full(vec_d.shape), _full(vec_c.shape),
                  _full(cw.shape), _full(pw.shape), _ANY, _ANY],
        out_specs=[tile(d), tile(d), tile(din), tile(dc), tile(dpool), tile(dmix), tile(d)],
        out_shape=[jax.ShapeDtypeStruct((nb, s, d), F32), jax.ShapeDtypeStruct((nb, s, d), BF16),
                   jax.ShapeDtypeStruct((nb, s, din), BF16), jax.ShapeDtypeStruct((nb, s, dc), BF16),
                   jax.ShapeDtypeStruct((nb, s, dpool), BF16), jax.ShapeDtypeStruct((nb, s, dmix), BF16),
                   jax.ShapeDtypeStruct((nb, s, d), F32)],
        scratch_shapes=[pltpu.VMEM((d, din), BF16), pltpu.VMEM((dmix, d), BF16),
                        pltpu.VMEM((HALO + tm, dc), F32), pltpu.VMEM((HALO + tm, dpool), F32),
                        pltpu.SemaphoreType.DMA((2 * nj,))],
        args=(x, mod, vec_d, vec_c, cw, pw, win_g, wout_g), exchange=exchange)


def _mixer_backward(dxo, x, o, u, ac, dpl, mod, vec_d, vec_c, cw, pw, win_g, wout_g, taps, tm, name, exchange=None):
    nb, s, d = x.shape
    n = s // tm
    nj, _, dcol = win_g.shape
    din = nj * dcol
    dc = vec_c.shape[-1]
    dpool = din - 2 * dc
    dmix = dc + dpool
    ro = wout_g.shape[1]
    ngrp = dpool // LANES
    rext = tm + HALO

    def body(dxo_ref, x_ref, o_ref, u_ref, ac_ref, dp_ref, mod_ref, vd_ref, vc_ref, cw_ref, pw_ref, win_hbm, wout_hbm,
             dx_ref, du_ref, dob_ref, rowd_ref, rowb_ref, rowc_ref, dcw_ref, dpw_ref,
             win_v, wout_v, ext_a, ext_p, sems):
        b, i = pl.program_id(0), pl.program_id(1)
        first = (b == 0) & (i == 0)
        pairs = [(win_hbm.at[j], win_v.at[:, pl.ds(j * dcol, dcol)]) for j in range(nj)]
        pairs += [(wout_hbm.at[j], wout_v.at[pl.ds(j * ro, ro), :]) for j in range(nj)]
        _load_weights(first, pairs, sems)

        @pl.when(first)
        def _():
            rowd_ref[...] = jnp.zeros_like(rowd_ref)
            rowc_ref[...] = jnp.zeros_like(rowc_ref)
            dcw_ref[...] = jnp.zeros_like(dcw_ref)
            dpw_ref[...] = jnp.zeros_like(dpw_ref)

        @pl.when(i == 0)
        def _():
            rowb_ref[...] = jnp.zeros_like(rowb_ref)
            ext_a[tm:rext, :] = jnp.zeros((HALO, dc), F32)
            ext_p[tm:rext, :] = jnp.zeros((HALO, dpool), F32)

        @pl.when(i > 0)
        def _():
            ext_a[tm:rext, :] = ext_a[0:HALO, :]
            ext_p[tm:rext, :] = ext_p[0:HALO, :]

        g_pre, g_post = vd_ref[0:1, :], vd_ref[1:2, :]
        sh, sc, gt = mod_ref[0:1, :], mod_ref[1:2, :], mod_ref[2:3, :]
        do, d_gt, d_gp = _gated_residual_bwd(dxo_ref[...], o_ref[...], g_post, gt)
        dob = do.astype(BF16)
        dob_ref[...] = dob
        dz = lax.dot_general(dob, wout_v[...], (((1,), (1,)), ((), ())), preferred_element_type=F32)

        acv = ac_ref[...].astype(F32)
        mu = jnp.mean(acv, axis=-1, keepdims=True)
        xc = acv - mu
        rstd = lax.rsqrt(jnp.mean(xc * xc, axis=-1, keepdims=True) + EPS)
        an = xc * rstd
        lg = vc_ref[1:2, :]
        al = an * lg + vc_ref[2:3, :]
        sg = _sigmoid(al)
        dal = dz[:, :dc] * (sg * (1.0 + al * (1.0 - sg)))
        d_lg = _colsum(dal * an)
        d_lb = _colsum(dal)
        dan = dal * lg
        dac = rstd * (dan - jnp.mean(dan, axis=-1, keepdims=True) - an * jnp.mean(dan * an, axis=-1, keepdims=True))
        d_cb = _colsum(dac)
        ext_a[0:tm, :] = dac
        uv = u_ref[:, 0:dc].astype(F32)
        sgu = _sigmoid(u_ref[:, dc:2 * dc].astype(F32))
        ag = uv * sgu
        dag = jnp.zeros((tm, dc), F32)
        for k in range(taps):
            sl = ext_a[pl.ds(taps - 1 - k, tm), :]
            dag = dag + cw_ref[k:k + 1, :] * sl
            dcw_ref[k:k + 1, :] += _colsum(ag * sl)
        du_ref[:, 0:dc] = (dag * sgu).astype(BF16)
        du_ref[:, dc:2 * dc] = (dag * uv * (sgu * (1.0 - sgu))).astype(BF16)

        pos = _seq_positions(n - 1 - i, tm, LANES)
        d_ps = []
        for g in range(ngrp):
            w = POOL_WINDOWS[g]
            cols = slice(g * LANES, (g + 1) * LANES)
            gcols = slice(dc + g * LANES, dc + (g + 1) * LANES)
            dgb = dp_ref[:, cols]
            q = jnp.dot(dgb, pw_ref[g], preferred_element_type=F32)
            dpg = dz[:, gcols]
            d_ps.append(_colsum(dpg * q))
            dq = (dpg * vc_ref[3:4, cols]).astype(BF16)
            dpw_ref[g] += lax.dot_general(dgb, dq, (((0,), (0,)), ((), ())), preferred_element_type=F32)
            dd = lax.dot_general(dq, pw_ref[g], (((1,), (1,)), ((), ())), preferred_element_type=F32)
            cnt = jnp.minimum(pos + 1, w).astype(F32)
            ext_p[0:tm, cols] = dd / cnt
            sw = ext_p[:, cols]
            step = 1
            while step < w:
                sw = sw + pltpu.roll(sw, rext - step, axis=0)
                step *= 2
            du_ref[:, 2 * dc + g * LANES:2 * dc + (g + 1) * LANES] = (sw[0:tm, :] - dd).astype(BF16)
        rowc_ref[0:1, :] += d_cb
        rowc_ref[1:2, :] += d_lg
        rowc_ref[2:3, :] += d_lb
        rowc_ref[3:4, :] += jnp.concatenate(d_ps, axis=-1)

        dh = lax.dot_general(du_ref[...], win_v[...], (((1,), (1,)), ((), ())), preferred_element_type=F32)
        _, xn, r1 = _ada_norm(x_ref[...], g_pre, sc, sh)
        dxb, d_sh, d_sc, d_g = _ada_norm_bwd(dh, xn, r1, g_pre, sc)
        dx_ref[...] = dxo_ref[...] + dxb
        rowd_ref[0:1, :] += d_g
        rowd_ref[1:2, :] += d_gp
        rowb_ref[0:1, :] += d_sh
        rowb_ref[1:2, :] += d_sc
        rowb_ref[2:3, :] += d_gt

    def tile(width):
        return pl.BlockSpec((None, tm, width), lambda b, i: (b, n - 1 - i, 0))

    return _grid_call(
        body, name, (nb, n),
        in_specs=[tile(d), tile(d), tile(d), tile(din), tile(dc), tile(dpool),
                  pl.BlockSpec((None, 8, d), lambda b, i: (b, 0, 0)), _full(vec_d.shape), _full(vec_c.shape),
                  _full(cw.shape), _full(pw.shape), _ANY, _ANY],
        out_specs=[tile(d), tile(din), tile(d), _full((8, d)), pl.BlockSpec((None, 8, d), lambda b, i: (b, 0, 0)),
                   _full((8, dc)), _full(cw.shape), _full(pw.shape)],
        out_shape=[jax.ShapeDtypeStruct((nb, s, d), F32), jax.ShapeDtypeStruct((nb, s, din), BF16),
                   jax.ShapeDtypeStruct((nb, s, d), BF16), jax.ShapeDtypeStruct((8, d), F32),
                   jax.ShapeDtypeStruct((nb, 8, d), F32), jax.ShapeDtypeStruct((8, dc), F32),
                   jax.ShapeDtypeStruct(cw.shape, F32), jax.ShapeDtypeStruct(pw.shape, F32)],
        scratch_shapes=[pltpu.VMEM((d, din), BF16), pltpu.VMEM((dmix, d), BF16),
                        pltpu.VMEM((rext, dc), F32), pltpu.VMEM((rext, dpool), F32),
                        pltpu.SemaphoreType.DMA((2 * nj,))],
        args=(dxo, x, o, u, ac, dpl, mod, vec_d, vec_c, cw, pw, win_g, wout_g), exchange=exchange)


def _ffn_forward(x, mod, vec_d, fw, wup_g, wdn_g, tm, name, exchange=None):
    nb, s, d = x.shape
    n = s // tm
    nj, _, ucol = wup_g.shape
    f2 = nj * ucol
    dff = f2 // 2
    rd = wdn_g.shape[1]
    nq = nj // 2
    cs = dff // nq

    def body(x_ref, mod_ref, vd_ref, fw_ref, wup_hbm, wdn_hbm,
             xo_ref, h_ref, u_ref, hid_ref, o_ref,
             wup_v, wdn_v, ext_u, sems):
        b, i = pl.program_id(0), pl.program_id(1)
        pairs = [(wup_hbm.at[j], wup_v.at[:, pl.ds(j * ucol, ucol)]) for j in range(nj)]
        pairs += [(wdn_hbm.at[j], wdn_v.at[pl.ds(j * rd, rd), :]) for j in range(nj)]
        _load_weights((b == 0) & (i == 0), pairs, sems)

        @pl.when(i == 0)
        def _():
            ext_u[0:FHALO, :] = jnp.zeros((FHALO, f2), F32)

        @pl.when(i > 0)
        def _():
            ext_u[0:FHALO, :] = ext_u[tm:tm + FHALO, :]

        xv = x_ref[...]
        h, _, _ = _ada_norm(xv, vd_ref[2:3, :], mod_ref[4:5, :], mod_ref[3:4, :])
        hb = h.astype(BF16)
        h_ref[...] = hb

        def conv(cols):
            uc = jnp.dot(hb, wup_v[:, cols], preferred_element_type=F32)
            u_ref[:, cols] = uc.astype(BF16)
            ext_u[FHALO:FHALO + tm, cols] = uc
            out = jnp.broadcast_to(fw_ref[3:4, cols], (tm, cs))
            for k in range(3):
                out = out + fw_ref[k:k + 1, cols] * ext_u[pl.ds(FHALO - 2 + k, tm), cols]
            return out

        o = jnp.zeros((tm, d), F32)
        for q in range(nq):
            val = conv(pl.ds(q * cs, cs))
            gate = conv(pl.ds(dff + q * cs, cs))
            hid = ((gate * _sigmoid(gate)) * val).astype(BF16)
            hid_ref[:, pl.ds(q * cs, cs)] = hid
            o = o + jnp.dot(hid, wdn_v[pl.ds(q * cs, cs), :], preferred_element_type=F32)
        o_ref[...] = o
        r2 = lax.rsqrt(jnp.mean(o * o, axis=-1, keepdims=True) + EPS)
        xo_ref[...] = xv + (1.0 + mod_ref[5:6, :]) * ((o * r2) * vd_ref[3:4, :])

    def tile(width):
        return pl.BlockSpec((None, tm, width), lambda b, i: (b, i, 0))

    return _grid_call(
        body, name, (nb, n),
        in_specs=[tile(d), pl.BlockSpec((None, 8, d), lambda b, i: (b, 0, 0)), _full(vec_d.shape), _full(fw.shape),
                  _ANY, _ANY],
        out_specs=[tile(d), tile(d), tile(f2), tile(dff), tile(d)],
        out_shape=[jax.ShapeDtypeStruct((nb, s, d), F32), jax.ShapeDtypeStruct((nb, s, d), BF16),
                   jax.ShapeDtypeStruct((nb, s, f2), BF16), jax.ShapeDtypeStruct((nb, s, dff), BF16),
                   jax.ShapeDtypeStruct((nb, s, d), F32)],
        scratch_shapes=[pltpu.VMEM((d, f2), BF16), pltpu.VMEM((dff, d), BF16),
                        pltpu.VMEM((FHALO + tm, f2), F32), pltpu.SemaphoreType.DMA((2 * nj,))],
        args=(x, mod, vec_d, fw, wup_g, wdn_g), exchange=exchange)


def _ffn_backward(dxo, x, o, u, mod, vec_d, fw, wup_g, wdn_g, tm, name, exchange=None):
    nb, s, d = x.shape
    n = s // tm
    nj, _, ucol = wup_g.shape
    f2 = nj * ucol
    dff = f2 // 2
    rd = wdn_g.shape[1]
    nq = nj // 2
    cs = dff // nq
    hb_per_tile = tm // FHALO

    def body(dxo_ref, x_ref, o_ref, u_ref, uh_ref, mod_ref, vd_ref, fw_ref, wup_hbm, wdn_hbm,
             dx_ref, du_ref, dob_ref, rowd_ref, rowb_ref, dfw_ref,
             wup_v, wdn_v, ext_u, ext_d, sems):
        b, i = pl.program_id(0), pl.program_id(1)
        first = (b == 0) & (i == 0)
        pairs = [(wup_hbm.at[j], wup_v.at[:, pl.ds(j * ucol, ucol)]) for j in range(nj)]
        pairs += [(wdn_hbm.at[j], wdn_v.at[pl.ds(j * rd, rd), :]) for j in range(nj)]
        _load_weights(first, pairs, sems)

        @pl.when(first)
        def _():
            rowd_ref[...] = jnp.zeros_like(rowd_ref)
            dfw_ref[...] = jnp.zeros_like(dfw_ref)

        @pl.when(i == 0)
        def _():
            rowb_ref[...] = jnp.zeros_like(rowb_ref)
            ext_d[tm:tm + FHALO, :] = jnp.zeros((FHALO, f2), F32)

        @pl.when(i > 0)
        def _():
            ext_d[tm:tm + FHALO, :] = ext_d[0:FHALO, :]

        g_pre, g_post = vd_ref[2:3, :], vd_ref[3:4, :]
        sh, sc, gt = mod_ref[3:4, :], mod_ref[4:5, :], mod_ref[5:6, :]
        do, d_gt, d_gp = _gated_residual_bwd(dxo_ref[...], o_ref[...], g_post, gt)
        dob = do.astype(BF16)
        dob_ref[...] = dob

        keep = jnp.where(i == n - 1, 0.0, 1.0)

        def conv(cols):
            ext_u[0:FHALO, 0:cs] = uh_ref[:, cols].astype(F32) * keep
            ext_u[FHALO:FHALO + tm, 0:cs] = u_ref[:, cols].astype(F32)
            out = jnp.broadcast_to(fw_ref[3:4, cols], (tm, cs))
            for k in range(3):
                out = out + fw_ref[k:k + 1, cols] * ext_u[pl.ds(FHALO - 2 + k, tm), 0:cs]
            return out

        def conv_bwd(cols, duc):
            dfw_ref[3:4, cols] += _colsum(duc)
            ext_d[0:tm, cols] = duc
            uc = u_ref[:, cols].astype(F32)
            out = jnp.zeros((tm, cs), F32)
            for k in range(3):
                sl = ext_d[pl.ds(2 - k, tm), cols]
                out = out + fw_ref[k:k + 1, cols] * sl
                dfw_ref[k:k + 1, cols] += _colsum(uc * sl)
            ob = out.astype(BF16)
            du_ref[:, cols] = ob
            return lax.dot_general(ob, wup_v[:, cols], (((1,), (1,)), ((), ())), preferred_element_type=F32)

        dh = jnp.zeros((tm, d), F32)
        for q in range(nq):
            vcols = pl.ds(q * cs, cs)
            gcols = pl.ds(dff + q * cs, cs)
            dhid = lax.dot_general(dob, wdn_v[vcols, :], (((1,), (1,)), ((), ())), preferred_element_type=F32)
            val = conv(vcols)
            gate = conv(gcols)
            sg = _sigmoid(gate)
            dval = dhid * (gate * sg)
            dgate = dhid * val * (sg * (1.0 + gate * (1.0 - sg)))
            dh = dh + conv_bwd(vcols, dval)
            dh = dh + conv_bwd(gcols, dgate)

        _, xn, r1 = _ada_norm(x_ref[...], g_pre, sc, sh)
        dxb, d_sh, d_sc, d_g = _ada_norm_bwd(dh, xn, r1, g_pre, sc)
        dx_ref[...] = dxo_ref[...] + dxb
        rowd_ref[2:3, :] += d_g
        rowd_ref[3:4, :] += d_gp
        rowb_ref[3:4, :] += d_sh
        rowb_ref[4:5, :] += d_sc
        rowb_ref[5:6, :] += d_gt

    def tile(width):
        return pl.BlockSpec((None, tm, width), lambda b, i: (b, n - 1 - i, 0))

    halo = pl.BlockSpec((None, FHALO, f2), lambda b, i: (b, jnp.maximum((n - 1 - i) * hb_per_tile - 1, 0), 0))
    return _grid_call(
        body, name, (nb, n),
        in_specs=[tile(d), tile(d), tile(d), tile(f2), halo, pl.BlockSpec((None, 8, d), lambda b, i: (b, 0, 0)),
                  _full(vec_d.shape), _full(fw.shape), _ANY, _ANY],
        out_specs=[tile(d), tile(f2), tile(d), _full((8, d)), pl.BlockSpec((None, 8, d), lambda b, i: (b, 0, 0)),
                   _full(fw.shape)],
        out_shape=[jax.ShapeDtypeStruct((nb, s, d), F32), jax.ShapeDtypeStruct((nb, s, f2), BF16),
                   jax.ShapeDtypeStruct((nb, s, d), BF16), jax.ShapeDtypeStruct((8, d), F32),
                   jax.ShapeDtypeStruct((nb, 8, d), F32), jax.ShapeDtypeStruct(fw.shape, F32)],
        scratch_shapes=[pltpu.VMEM((d, f2), BF16), pltpu.VMEM((dff, d), BF16),
                        pltpu.VMEM((FHALO + tm, cs), F32), pltpu.VMEM((tm + FHALO, f2), F32),
                        pltpu.SemaphoreType.DMA((2 * nj,))],
        args=(dxo, x, o, u, u, mod, vec_d, fw, wup_g, wdn_g), exchange=exchange)


def _weight_grad(a, b, nblk, split, tt, name, exchange=None):
    t, ka = a.shape
    nb_ = b.shape[1]
    nk = t // tt
    if split == "cols":
        wa, wb, grid = ka, nb_ // nblk, (1, nk)
        a_spec = pl.BlockSpec((tt, ka), lambda j, k: (k, 0))
        b_spec = pl.BlockSpec((tt, nb_), lambda j, k: (k, 0))
        o_spec = pl.BlockSpec((nblk, wa, wb), lambda j, k: (0, 0, 0))
        acc_shape = (ka, nb_)
    elif split == "b":
        wa, wb, grid = ka, nb_ // nblk, (nblk, nk)
        a_spec = pl.BlockSpec((tt, wa), lambda j, k: (k, 0))
        b_spec = pl.BlockSpec((tt, wb), lambda j, k: (k, j))
        o_spec = pl.BlockSpec((None, wa, wb), lambda j, k: (j, 0, 0))
        acc_shape = (wa, wb)
    else:
        wa, wb, grid = ka // nblk, nb_, (nblk, nk)
        a_spec = pl.BlockSpec((tt, wa), lambda j, k: (k, j))
        b_spec = pl.BlockSpec((tt, wb), lambda j, k: (k, 0))
        o_spec = pl.BlockSpec((None, wa, wb), lambda j, k: (j, 0, 0))
        acc_shape = (wa, wb)

    def body(a_ref, b_ref, o_ref, acc):
        k = pl.program_id(1)
        prod = lax.dot_general(a_ref[...], b_ref[...], (((0,), (0,)), ((), ())), preferred_element_type=F32)

        @pl.when(k == 0)
        def _():
            acc[...] = prod

        @pl.when(k > 0)
        def _():
            acc[...] += prod

        @pl.when(k == nk - 1)
        def _():
            if split == "cols":
                for j in range(nblk):
                    o_ref[j] = acc[:, j * wb:(j + 1) * wb].astype(o_ref.dtype)
            else:
                o_ref[...] = acc[...].astype(o_ref.dtype)

    outs, exo = _grid_call(body, name, grid, in_specs=[a_spec, b_spec], out_specs=[o_spec],
                           out_shape=[jax.ShapeDtypeStruct((nblk, wa, wb), BF16)],
                           scratch_shapes=[pltpu.VMEM(acc_shape, F32)], args=(a, b), exchange=exchange)
    return outs[0], exo


def _loss_grad(y, tgt, tm, name):
    nb, s, d = y.shape
    n = s // tm

    def body(y_ref, t_ref, dy_ref, sq_ref):
        @pl.when((pl.program_id(0) == 0) & (pl.program_id(1) == 0))
        def _():
            sq_ref[...] = jnp.zeros_like(sq_ref)

        e = y_ref[...] - t_ref[...]
        dy_ref[...] = e * (1.0 / d)
        sq_ref[0:1, :] += _colsum(e * e)

    tile = pl.BlockSpec((None, tm, d), lambda b, i: (b, i, 0))
    return pl.pallas_call(
        body, name=name, out_shape=[jax.ShapeDtypeStruct((nb, s, d), F32), jax.ShapeDtypeStruct((8, d), F32)],
        grid=(nb, n), in_specs=[tile, tile], out_specs=[tile, _full((8, d))],
        compiler_params=pltpu.CompilerParams(dimension_semantics=("arbitrary", "arbitrary")),
    )(y, tgt)


def _rows128(a):
    return a.reshape(-1, LANES)


class _ReduceScatter:
    def __init__(self, gs, cidx, idx, tag):
        self.gs, self.cidx, self.idx, self.tag = gs, cidx, idx, tag

    def swap(self):
        return _swap_halves(self.gs)

    def after_swap(self, r1):
        self.ps = [_pair_sum(g, r, self.cidx, name=f"rs_pair_{self.tag}_{a}") for a, (g, r) in enumerate(zip(self.gs, r1))]

    def chips(self):
        return _chip_exchange(self.ps)

    def after_chips(self, r2):
        self.fh = [_chip_sum(p, r, self.idx, name=f"rs_sum_{self.tag}_{a}") for a, (p, r) in enumerate(zip(self.ps, r2))]

    def share(self):
        return _sibling_share(self.fh)

    @staticmethod
    def result(fs):
        return [f.reshape(f.shape[0] * f.shape[1], f.shape[2]) for f in fs]

    def run_alone(self):
        self.after_swap(_run_exchange(self.swap(), name=f"rs_swap_{self.tag}"))
        self.after_chips(_run_exchange(self.chips(), name=f"rs_chips_{self.tag}"))
        return self.result(_run_exchange(self.share(), name=f"rs_share_{self.tag}"))


def kernel(x, c, ada_w, ada_b, pre_mix_g, post_mix_g, w_in, conv_w, conv_b, conv_ln_g, conv_ln_b, pool_w, pool_scale, w_out, pre_ffn_g, post_ffn_g, ffn_up, ffn_conv_w, ffn_conv_b, ffn_down, loss_target, m_ada_w, m_ada_b, m_pre_mix_g, m_post_mix_g, m_w_in, m_conv_w, m_conv_b, m_conv_ln_g, m_conv_ln_b, m_pool_w, m_pool_scale, m_w_out, m_pre_ffn_g, m_post_ffn_g, m_ffn_up, m_ffn_conv_w, m_ffn_conv_b, m_ffn_down, v_ada_w, v_ada_b, v_pre_mix_g, v_post_mix_g, v_w_in, v_conv_w, v_conv_b, v_conv_ln_g, v_conv_ln_b, v_pool_w, v_pool_scale, v_w_out, v_pre_ffn_g, v_post_ffn_g, v_ffn_up, v_ffn_conv_w, v_ffn_conv_b, v_ffn_down):
    nb, s, d = x.shape
    nl = w_in.shape[0]
    taps = conv_w.shape[1]
    ccol = conv_w.shape[2]
    dc = conv_b.shape[1]
    fcol = ffn_conv_w.shape[2]
    f2 = ffn_conv_b.shape[1]
    nmod = ada_b.shape[1] // d
    acol = ada_w.shape[2]
    tm = min(256, s)
    tt = min(2048, (nb * s) // 2)

    xi, yi, ci = _pos()
    jm = 2 * xi + yi
    me = 4 * xi + 2 * yi + ci
    cidx = jnp.reshape(ci, (1,)).astype(jnp.int32)
    idx = jnp.stack([jm, ci]).astype(jnp.int32)

    n_cw, n_fw, n_c = nl * taps * ccol, nl * 3 * fcol, nb * d
    packed = jnp.concatenate([conv_w.reshape(-1), ffn_conv_w.reshape(-1), c.reshape(-1)])
    got = _gather8(_rows128(packed), name="gather_small").reshape(N_DEV, -1)
    chips = got[0::2]
    cw_full = chips[:, :n_cw].reshape(N_CHIPS, nl, taps, ccol).transpose(1, 2, 0, 3).reshape(nl, taps, dc)
    fw_full = chips[:, n_cw:n_cw + n_fw].reshape(N_CHIPS, nl, 3, fcol).transpose(1, 2, 0, 3).reshape(nl, 3, f2)
    c_all = got[:, n_cw + n_fw:].reshape(N_DEV * nb, d)

    ada_b_cols = lax.dynamic_slice_in_dim(ada_b, jm * acol, acol, axis=1).reshape(nl, 1, acol)
    mod_cols = _ada_forward(c_all, ada_w, ada_b_cols, name="ada_forward")
    mod_all = _gather8(_rows128(mod_cols), name="gather_mod").reshape(N_DEV, nl, N_DEV * nb, acol)
    mod_all = mod_all[0::2].transpose(1, 2, 0, 3).reshape(nl, N_DEV * nb, N_CHIPS * acol)
    mod_own = lax.dynamic_slice_in_dim(mod_all, me * nb, nb, axis=1).reshape(nl, nb, nmod, d)
    mod_own = jnp.pad(mod_own, ((0, 0), (0, 0), (0, 8 - nmod), (0, 0)))

    vec_d = jnp.stack([pre_mix_g, post_mix_g, pre_ffn_g, post_ffn_g], axis=1)
    vec_c = jnp.stack([conv_b, conv_ln_g, conv_ln_b, pool_scale], axis=1)
    cw_pad = jnp.pad(cw_full, ((0, 0), (0, HALO - taps), (0, 0)))
    fw_rows = jnp.concatenate([fw_full, ffn_conv_b[:, None, :], jnp.zeros((nl, 4, f2), F32)], axis=1)
    pw_b = pool_w.astype(BF16)

    win_b, wout_b, wup_b, wdn_b = (w.astype(BF16) for w in (w_in, w_out, ffn_up, ffn_down))

    def shards(l):
        return [win_b[l], wout_b[l], wup_b[l], wdn_b[l]]

    gathered = _run_exchange(_gather_sibling(_run_exchange(_gather_chips(shards(0)), name="gather_chips_0")),
                             name="gather_sibling_0")
    saved = []
    xs = x
    for l in range(nl):
        win_g, wout_g, wup_g, wdn_g = gathered
        more = l + 1 < nl
        (x1, h1, u1, ac1, dp1, z1, o1), part = _mixer_forward(
            xs, mod_own[l], vec_d[l], vec_c[l], cw_pad[l], pw_b[l], win_g, wout_g, taps, tm, name=f"mixer_fwd_{l}",
            exchange=_gather_chips(shards(l + 1)) if more else None)
        (x2, h2, u2, hid2, o2), gathered = _ffn_forward(
            x1, mod_own[l], vec_d[l], fw_rows[l], wup_g, wdn_g, tm, name=f"ffn_fwd_{l}",
            exchange=_gather_sibling(part) if more else None)
        saved.append((xs, h1, u1, ac1, dp1, z1, o1, x1, h2, u2, hid2, o2, win_g, wout_g, wup_g, wdn_g))
        xs = x2

    dx, sq = _loss_grad(xs, loss_target, tm, name="loss_grad")
    loss = lax.psum(0.5 * jnp.sum(sq) / d, ("x", "y", "c"))

    flat = lambda a: a.reshape(nb * s, a.shape[-1])
    small = [None] * nl
    big = [None] * nl
    pend = None
    for l in reversed(range(nl)):
        x0, h1, u1, ac1, dp1, z1, o1, x1, h2, u2, hid2, o2, win_g, wout_g, wup_g, wdn_g = saved[l]
        (dx, du2, do2, rowd2, rowb2, dfw), got = _ffn_backward(
            dx, x1, o2, u2, mod_own[l], vec_d[l], fw_rows[l], wup_g, wdn_g, tm, name=f"ffn_bwd_{l}",
            exchange=pend.swap() if pend else None)
        if pend:
            pend.after_swap(got)
        g_up, _ = _weight_grad(flat(h2), flat(du2), N_CHIPS, "b", tt, name=f"grad_ffn_up_{l}")
        g_dn, _ = _weight_grad(flat(hid2), flat(do2), 2, "a", tt, name=f"grad_ffn_down_{l}")
        (dx, du1, do1, rowd1, rowb1, rowc, dcw, dpw), got = _mixer_backward(
            dx, x0, o1, u1, ac1, dp1, mod_own[l], vec_d[l], vec_c[l], cw_pad[l], pw_b[l], win_g, wout_g, taps, tm,
            name=f"mixer_bwd_{l}", exchange=pend.chips() if pend else None)
        if pend:
            pend.after_chips(got)
        g_in, got = _weight_grad(flat(h1), flat(du1), N_CHIPS, "cols", tt, name=f"grad_w_in_{l}",
                                 exchange=pend.share() if pend else None)
        if pend:
            big[l + 1] = pend.result(got)
        g_out, _ = _weight_grad(flat(z1), flat(do1), 1, "cols", tt, name=f"grad_w_out_{l}")
        pend = _ReduceScatter([g_in, g_out.reshape(N_CHIPS, -1, d), g_up, g_dn.reshape(N_CHIPS, -1, d)], cidx, idx, l)
        small[l] = dict(rowd=rowd1 + rowd2, rowb=rowb1 + rowb2, rowc=rowc, dcw=dcw[:taps], dpw=dpw, dfw=dfw)
    big[0] = pend.run_alone()

    dmod_own = jnp.stack([small[l]["rowb"][:, :nmod, :] for l in range(nl)])
    dmod_all = _gather8(_rows128(dmod_own), name="gather_dmod").reshape(N_DEV, nl, nb, nmod * d)
    dmod_all = dmod_all.transpose(1, 0, 2, 3).reshape(nl, N_DEV * nb, nmod * d)
    dmod_cols = lax.dynamic_slice_in_dim(dmod_all, jm * acol, acol, axis=2)
    g_ada_w, d_ada_w, nm_ada_w, nv_ada_w = _ada_update(c_all, dmod_cols, ada_w, m_ada_w, v_ada_w, name="ada_update")

    def st(key, row=None):
        return jnp.stack([small[l][key] if row is None else small[l][key][row] for l in range(nl)])

    local = {
        "ada_b": dmod_own.sum(axis=1).reshape(nl, nmod * d),
        "pre_mix_g": st("rowd", 0), "post_mix_g": st("rowd", 1),
        "conv_b": st("rowc", 0), "conv_ln_g": st("rowc", 1), "conv_ln_b": st("rowc", 2),
        "pool_w": st("dpw"), "pool_scale": st("rowc", 3),
        "pre_ffn_g": st("rowd", 2), "post_ffn_g": st("rowd", 3),
        "ffn_conv_b": st("dfw", 3), "conv_w": st("dcw"), "ffn_conv_w": jnp.stack([small[l]["dfw"][:3] for l in range(nl)]),
    }
    names = list(local)
    sizes = [local[k].size for k in names]
    summed = _allreduce8(_rows128(jnp.concatenate([local[k].reshape(-1) for k in names])), name="allreduce_small").reshape(-1)
    grads, off = {}, 0
    for k, sz in zip(names, sizes):
        grads[k] = summed[off:off + sz].reshape(local[k].shape)
        off += sz
    grads["conv_w"] = lax.dynamic_slice_in_dim(grads["conv_w"], jm * ccol, ccol, axis=2)
    grads["ffn_conv_w"] = lax.dynamic_slice_in_dim(grads["ffn_conv_w"], jm * fcol, fcol, axis=2)

    params = dict(ada_b=(ada_b, m_ada_b, v_ada_b), pre_mix_g=(pre_mix_g, m_pre_mix_g, v_pre_mix_g),
                  post_mix_g=(post_mix_g, m_post_mix_g, v_post_mix_g), conv_b=(conv_b, m_conv_b, v_conv_b),
                  conv_ln_g=(conv_ln_g, m_conv_ln_g, v_conv_ln_g), conv_ln_b=(conv_ln_b, m_conv_ln_b, v_conv_ln_b),
                  pool_w=(pool_w, m_pool_w, v_pool_w), pool_scale=(pool_scale, m_pool_scale, v_pool_scale),
                  pre_ffn_g=(pre_ffn_g, m_pre_ffn_g, v_pre_ffn_g), post_ffn_g=(post_ffn_g, m_post_ffn_g, v_post_ffn_g),
                  ffn_conv_b=(ffn_conv_b, m_ffn_conv_b, v_ffn_conv_b), conv_w=(conv_w, m_conv_w, v_conv_w),
                  ffn_conv_w=(ffn_conv_w, m_ffn_conv_w, v_ffn_conv_w))
    pack = lambda i, g=None: _rows128(jnp.concatenate([(grads[k] if g else params[k][i]).reshape(-1) for k in names]))
    sd, sm, sv = _adamw_flat(pack(0), pack(0, True), pack(1), pack(2), name="adamw_small")
    outs = {}
    off = 0
    for k in names:
        shape, sz = params[k][0].shape, params[k][0].size
        outs[k] = (grads[k],) + tuple(a.reshape(-1)[off:off + sz].reshape(shape) for a in (sd, sm, sv))
        off += sz

    outs["ada_w"] = (g_ada_w, d_ada_w, nm_ada_w, nv_ada_w)
    for a, (k, w, m, v) in enumerate([("w_in", w_in, m_w_in, v_w_in), ("w_out", w_out, m_w_out, v_w_out),
                                      ("ffn_up", ffn_up, m_ffn_up, v_ffn_up), ("ffn_down", ffn_down, m_ffn_down, v_ffn_down)]):
        outs[k] = tuple(_adamw_layers(w, m, v, [big[l][a] for l in range(nl)], name=f"adamw_{k}"))

    order = ["ada_w", "ada_b", "pre_mix_g", "post_mix_g", "w_in", "conv_w", "conv_b", "conv_ln_g", "conv_ln_b", "pool_w",
             "pool_scale", "w_out", "pre_ffn_g", "post_ffn_g", "ffn_up", "ffn_conv_w", "ffn_conv_b", "ffn_down"]
    return (loss, dx) + tuple(outs[k][i] for i in range(4) for k in order)
```

```python
import functools

import jax
import jax.numpy as jnp
from jax import lax
from jax.experimental import pallas as pl
from jax.experimental.pallas import tpu as pltpu

F32 = jnp.float32
BF16 = jnp.bfloat16
MESH = pl.DeviceIdType.MESH

EPS = 1e-6
POOL_WINDOWS = (2, 4, 8, 16)
ADAM_LR = 0.001
ADAM_B1 = 0.9
ADAM_B2 = 0.999
ADAM_EPS = 1e-08
ADAM_WD = 0.01
ADAM_STEP = 10

N_CHIPS = 4
N_DEV = 8
LANES = 128
SUBLANES = 8
HALO = 32
FHALO = 8
VMEM_LIMIT = 60 * 1024 * 1024


def _pos():
    return lax.axis_index("x"), lax.axis_index("y"), lax.axis_index("c")


def _flip(v, f):
    return 1 - v if f else v


def _full(shape):
    nd = len(shape)
    return pl.BlockSpec(shape, lambda *_: (0,) * nd)


_ANY = pl.BlockSpec(memory_space=pl.ANY)
_VMEM = pl.BlockSpec(memory_space=pltpu.VMEM)


def _sigmoid(v):
    return 1.0 / (1.0 + jnp.exp(-v))


def _colsum(v):
    return jnp.sum(v, axis=0, keepdims=True)


def _gather8(v, name):
    r, ccols = v.shape

    def body(v_ref, out_ref, send_sems, recv_sems, local_sem):
        x, y, c = _pos()
        me = 4 * x + 2 * y + c
        mine = pltpu.make_async_copy(v_ref, out_ref.at[me], local_sem)
        mine.start()
        peers = [(_flip(x, (k >> 2) & 1), _flip(y, (k >> 1) & 1), _flip(c, k & 1)) for k in range(1, N_DEV)]
        sends = []
        for k, peer in enumerate(peers):
            cp = pltpu.make_async_remote_copy(src_ref=v_ref, dst_ref=out_ref.at[me], send_sem=send_sems.at[k],
                                              recv_sem=recv_sems.at[k], device_id=peer, device_id_type=MESH)
            cp.start()
            sends.append(cp)
        for k, peer in enumerate(peers):
            pidx = 4 * peer[0] + 2 * peer[1] + peer[2]
            pltpu.make_async_remote_copy(src_ref=v_ref, dst_ref=out_ref.at[pidx], send_sem=send_sems.at[k],
                                         recv_sem=recv_sems.at[k], device_id=peer, device_id_type=MESH).wait_recv()
        for cp in sends:
            cp.wait_send()
        mine.wait()

    return pl.pallas_call(
        body, name=name, out_shape=jax.ShapeDtypeStruct((N_DEV, r, ccols), v.dtype),
        in_specs=[_VMEM], out_specs=_VMEM,
        scratch_shapes=[pltpu.SemaphoreType.DMA((N_DEV - 1,)), pltpu.SemaphoreType.DMA((N_DEV - 1,)),
                        pltpu.SemaphoreType.DMA(())],
        compiler_params=pltpu.CompilerParams(vmem_limit_bytes=VMEM_LIMIT),
    )(v)


def _allreduce8(v, name):
    r, ccols = v.shape
    h = r // 2

    def body(v_ref, out_ref, whole, half, send_sems, recv_sems):
        x, y, c = _pos()
        sib = (x, y, 1 - c)
        mine = pl.ds(pl.multiple_of(c * h, SUBLANES), h)
        theirs = pl.ds(pl.multiple_of((1 - c) * h, SUBLANES), h)

        def exchange(src, dst, k, peer):
            cp = pltpu.make_async_remote_copy(src_ref=src, dst_ref=dst, send_sem=send_sems.at[k],
                                              recv_sem=recv_sems.at[k], device_id=peer, device_id_type=MESH)
            cp.start()
            cp.wait()

        exchange(v_ref, whole, 0, sib)
        out_ref[...] = v_ref[...] + whole[...]
        for k, peer in ((1, (1 - x, y, c)), (2, (x, 1 - y, c))):
            exchange(out_ref.at[mine], half.at[k - 1], k, peer)
            out_ref[mine, :] = out_ref[mine, :] + half[k - 1]
        exchange(out_ref.at[mine], half.at[2], 3, sib)
        out_ref[theirs, :] = half[2]

    return pl.pallas_call(
        body, name=name, out_shape=jax.ShapeDtypeStruct((r, ccols), v.dtype),
        in_specs=[_VMEM], out_specs=_VMEM,
        scratch_shapes=[pltpu.VMEM((r, ccols), v.dtype), pltpu.VMEM((3, h, ccols), v.dtype),
                        pltpu.SemaphoreType.DMA((4,)), pltpu.SemaphoreType.DMA((4,))],
        compiler_params=pltpu.CompilerParams(vmem_limit_bytes=VMEM_LIMIT),
    )(v)


def _chip_peers(x, y, c):
    peers = [(_flip(x, (k >> 1) & 1), _flip(y, k & 1), c) for k in range(1, N_CHIPS)]
    return peers, [2 * p[0] + p[1] for p in peers]


class _Exchange:
    def __init__(self, ins, outs, aliases, n_sems, n_local, start, finish):
        self.ins, self.outs, self.aliases = list(ins), list(outs), dict(aliases)
        self.n_sems, self.n_local, self.start, self.finish = n_sems, n_local, start, finish

    def scratch(self):
        return [pltpu.SemaphoreType.DMA((self.n_sems,)), pltpu.SemaphoreType.DMA((self.n_sems,)),
                pltpu.SemaphoreType.DMA((max(self.n_local, 1),))]


def _remote(src, dst, send_sems, recv_sems, k, peer):
    return pltpu.make_async_remote_copy(src_ref=src, dst_ref=dst, send_sem=send_sems.at[k], recv_sem=recv_sems.at[k],
                                        device_id=peer, device_id_type=MESH)


def _gather_chips(shards):
    n = len(shards)
    per = N_CHIPS - 1

    def copies(ins, outs, send_sems, recv_sems):
        x, y, c = _pos()
        jm = 2 * x + y
        peers, pjs = _chip_peers(x, y, c)
        sends, recvs = [], []
        for a in range(n):
            hr = shards[a].shape[0] // 2
            rows = pl.ds(c * hr, hr)
            for k, peer in enumerate(peers):
                sends.append(_remote(ins[a].at[rows], outs[a].at[jm, rows], send_sems, recv_sems, a * per + k, peer))
                landed = outs[a].at[pjs[k], rows]
                recvs.append(_remote(landed, landed, send_sems, recv_sems, a * per + k, peer))
        return sends, recvs

    def local(ins, outs, local_sems):
        x, y, _ = _pos()
        return [pltpu.make_async_copy(ins[a], outs[a].at[2 * x + y], local_sems.at[a]) for a in range(n)]

    def start(ins, outs, send_sems, recv_sems, local_sems):
        for cp in local(ins, outs, local_sems) + copies(ins, outs, send_sems, recv_sems)[0]:
            cp.start()

    def finish(ins, outs, send_sems, recv_sems, local_sems):
        sends, recvs = copies(ins, outs, send_sems, recv_sems)
        for cp in recvs:
            cp.wait_recv()
        for cp in sends:
            cp.wait_send()
        for cp in local(ins, outs, local_sems):
            cp.wait()

    outs = [jax.ShapeDtypeStruct((N_CHIPS,) + s.shape, s.dtype) for s in shards]
    return _Exchange(shards, outs, {}, n * per, n, start, finish)


def _gather_sibling(bufs):
    n = len(bufs)
    per = N_CHIPS - 1

    def copies(outs, send_sems, recv_sems):
        x, y, c = _pos()
        sib = (x, y, 1 - c)
        _, pjs = _chip_peers(x, y, c)
        sends, recvs = [], []
        for a in range(n):
            hr = bufs[a].shape[1] // 2
            for k in range(per):
                mine = outs[a].at[pjs[k], pl.ds(c * hr, hr)]
                theirs = outs[a].at[pjs[k], pl.ds((1 - c) * hr, hr)]
                sends.append(_remote(mine, mine, send_sems, recv_sems, a * per + k, sib))
                recvs.append(_remote(theirs, theirs, send_sems, recv_sems, a * per + k, sib))
        return sends, recvs

    def start(ins, outs, send_sems, recv_sems, local_sems):
        for cp in copies(outs, send_sems, recv_sems)[0]:
            cp.start()

    def finish(ins, outs, send_sems, recv_sems, local_sems):
        sends, recvs = copies(outs, send_sems, recv_sems)
        for cp in recvs:
            cp.wait_recv()
        for cp in sends:
            cp.wait_send()

    outs = [jax.ShapeDtypeStruct(b.shape, b.dtype) for b in bufs]
    return _Exchange(bufs, outs, {a: a for a in range(n)}, n * per, 0, start, finish)


def _swap_halves(gs):
    n = len(gs)

    def copies(ins, outs, send_sems, recv_sems):
        x, y, c = _pos()
        sib = (x, y, 1 - c)
        cps = []
        for a in range(n):
            hr = gs[a].shape[1] // 2
            cps.append(_remote(ins[a].at[:, pl.ds((1 - c) * hr, hr), :], outs[a], send_sems, recv_sems, a, sib))
        return cps

    def start(ins, outs, send_sems, recv_sems, local_sems):
        for cp in copies(ins, outs, send_sems, recv_sems):
            cp.start()

    def finish(ins, outs, send_sems, recv_sems, local_sems):
        for cp in copies(ins, outs, send_sems, recv_sems):
            cp.wait()

    outs = [jax.ShapeDtypeStruct((g.shape[0], g.shape[1] // 2, g.shape[2]), g.dtype) for g in gs]
    return _Exchange(gs, outs, {}, n, 0, start, finish)


def _chip_exchange(ps):
    n = len(ps)
    per = N_CHIPS - 1

    def copies(ins, outs, send_sems, recv_sems):
        x, y, c = _pos()
        peers, pjs = _chip_peers(x, y, c)
        return [_remote(ins[a].at[pjs[k]], outs[a].at[k], send_sems, recv_sems, a * per + k, peer)
                for a in range(n) for k, peer in enumerate(peers)]

    def start(ins, outs, send_sems, recv_sems, local_sems):
        for cp in copies(ins, outs, send_sems, recv_sems):
            cp.start()

    def finish(ins, outs, send_sems, recv_sems, local_sems):
        for cp in copies(ins, outs, send_sems, recv_sems):
            cp.wait()

    outs = [jax.ShapeDtypeStruct((per,) + p.shape[1:], p.dtype) for p in ps]
    return _Exchange(ps, outs, {}, n * per, 0, start, finish)


def _sibling_share(fs):
    n = len(fs)

    def copies(outs, send_sems, recv_sems):
        x, y, c = _pos()
        sib = (x, y, 1 - c)
        sends = [_remote(outs[a].at[c], outs[a].at[c], send_sems, recv_sems, a, sib) for a in range(n)]
        recvs = [_remote(outs[a].at[1 - c], outs[a].at[1 - c], send_sems, recv_sems, a, sib) for a in range(n)]
        return sends, recvs

    def start(ins, outs, send_sems, recv_sems, local_sems):
        for cp in copies(outs, send_sems, recv_sems)[0]:
            cp.start()

    def finish(ins, outs, send_sems, recv_sems, local_sems):
        sends, recvs = copies(outs, send_sems, recv_sems)
        for cp in recvs:
            cp.wait_recv()
        for cp in sends:
            cp.wait_send()

    outs = [jax.ShapeDtypeStruct(f.shape, f.dtype) for f in fs]
    return _Exchange(fs, outs, {a: a for a in range(n)}, n, 0, start, finish)


def _run_exchange(ex, name):
    ni, no = len(ex.ins), len(ex.outs)

    def body(*refs):
        ins, outs, sems = refs[:ni], refs[ni:ni + no], refs[ni + no:]
        ex.start(ins, outs, *sems)
        ex.finish(ins, outs, *sems)

    return pl.pallas_call(
        body, name=name, out_shape=ex.outs, in_specs=[_ANY] * ni, out_specs=[_ANY] * no,
        input_output_aliases=ex.aliases, scratch_shapes=ex.scratch(),
    )(*ex.ins)


def _grid_call(body, name, grid, in_specs, out_specs, out_shape, scratch_shapes, args, exchange=None):
    ni, no = len(in_specs), len(out_specs)
    params = pltpu.CompilerParams(dimension_semantics=("arbitrary",) * len(grid), vmem_limit_bytes=VMEM_LIMIT)
    if exchange is None:
        outs = pl.pallas_call(body, name=name, grid=grid, in_specs=in_specs, out_specs=out_specs, out_shape=out_shape,
                              scratch_shapes=scratch_shapes, compiler_params=params)(*args)
        return list(outs), []
    ex = exchange
    nci, nco = len(ex.ins), len(ex.outs)

    def hosted(*refs):
        cin = refs[ni:ni + nci]
        cout = refs[ni + nci + no:ni + nci + no + nco]
        sems = refs[len(refs) - 3:]
        main = refs[:ni] + refs[ni + nci:ni + nci + no] + refs[ni + nci + no + nco:len(refs) - 3]
        ids = [pl.program_id(a) for a in range(len(grid))]
        first = functools.reduce(lambda p, q: p & q, [i == 0 for i in ids])
        last = functools.reduce(lambda p, q: p & q, [i == g - 1 for i, g in zip(ids, grid)])

        @pl.when(first)
        def _():
            ex.start(cin, cout, *sems)

        body(*main)

        @pl.when(last)
        def _():
            ex.finish(cin, cout, *sems)

    outs = pl.pallas_call(
        hosted, name=name, grid=grid, in_specs=list(in_specs) + [_ANY] * nci, out_specs=list(out_specs) + [_ANY] * nco,
        out_shape=list(out_shape) + ex.outs, scratch_shapes=list(scratch_shapes) + ex.scratch(),
        input_output_aliases={ni + a: no + b for a, b in ex.aliases.items()}, compiler_params=params,
    )(*args, *ex.ins)
    return list(outs[:no]), list(outs[no:])


def _row_tile(rows, cols, itemsize, budget=2 * 1024 * 1024):
    best = None
    for t in range(16, rows + 1, 16):
        if rows % t == 0 and t * cols * itemsize <= budget:
            best = t
    return best if best is not None else rows


def _pair_sum(g, r1, cidx, name):
    nj, r, ccols = g.shape
    hr = r // 2
    tr = _row_tile(hr, ccols, 4)
    nt = hr // tr

    def body(c_ref, g_ref, r_ref, o_ref):
        o_ref[...] = (g_ref[...].astype(F32) + r_ref[...].astype(F32)).astype(o_ref.dtype)

    return pl.pallas_call(
        body, name=name, out_shape=jax.ShapeDtypeStruct((nj, hr, ccols), g.dtype),
        grid_spec=pltpu.PrefetchScalarGridSpec(
            num_scalar_prefetch=1, grid=(nj, nt),
            in_specs=[pl.BlockSpec((None, tr, ccols), lambda j, i, c_ref: (j, c_ref[0] * nt + i, 0)),
                      pl.BlockSpec((None, tr, ccols), lambda j, i, c_ref: (j, i, 0))],
            out_specs=pl.BlockSpec((None, tr, ccols), lambda j, i, c_ref: (j, i, 0))),
        compiler_params=pltpu.CompilerParams(dimension_semantics=("arbitrary", "arbitrary")),
    )(cidx, g, r1)


def _chip_sum(p, r2, idx, name):
    nj, hr, ccols = p.shape
    tr = _row_tile(hr, ccols, 4)
    nt = hr // tr

    def body(i_ref, p_ref, r_ref, o_ref):
        s = p_ref[...].astype(F32)
        for k in range(N_CHIPS - 1):
            s = s + r_ref[k].astype(F32)
        o_ref[...] = s

    return pl.pallas_call(
        body, name=name, out_shape=jax.ShapeDtypeStruct((2, hr, ccols), F32),
        grid_spec=pltpu.PrefetchScalarGridSpec(
            num_scalar_prefetch=1, grid=(nt,),
            in_specs=[pl.BlockSpec((None, tr, ccols), lambda i, i_ref: (i_ref[0], i, 0)),
                      pl.BlockSpec((N_CHIPS - 1, tr, ccols), lambda i, i_ref: (0, i, 0))],
            out_specs=pl.BlockSpec((None, tr, ccols), lambda i, i_ref: (i_ref[1], i, 0))),
        compiler_params=pltpu.CompilerParams(dimension_semantics=("arbitrary",)),
    )(idx, p, r2)


def _adam_math(w, g, m, v):
    m2 = ADAM_B1 * m + (1.0 - ADAM_B1) * g
    v2 = ADAM_B2 * v + (1.0 - ADAM_B2) * (g * g)
    m_hat = m2 / (1.0 - ADAM_B1 ** ADAM_STEP)
    v_hat = v2 / (1.0 - ADAM_B2 ** ADAM_STEP)
    delta = -ADAM_LR * (m_hat / (jnp.sqrt(v_hat) + ADAM_EPS) + ADAM_WD * w)
    return delta, m2, v2


def _adamw_layers(w, m, v, gs, first_layer, name, filled=None, exchange=None):
    nl, r, ccols = w.shape
    ng = len(gs)
    tr = _row_tile(r, ccols, 4, budget=1024 * 1024)
    nt = r // tr
    nfill = 0 if filled is None else len(filled)

    def body(w_ref, m_ref, v_ref, *rest):
        g_refs, (go_ref, d_ref, mo_ref, vo_ref) = rest[:ng], rest[ng + nfill:]
        l = pl.program_id(0)
        g = g_refs[0][...]
        for k in range(1, ng):
            g = jnp.where(l == k, g_refs[k][...], g)
        delta, m2, v2 = _adam_math(w_ref[...], g, m_ref[...], v_ref[...])
        go_ref[...] = g
        d_ref[...] = delta
        mo_ref[...] = m2
        vo_ref[...] = v2

    big = pl.BlockSpec((None, tr, ccols), lambda l, i: (l + first_layer, i, 0))

    def gspec(k):
        return pl.BlockSpec((tr, ccols), lambda l, i: (jnp.where(l == k, i, jnp.where(l < k, 0, nt - 1)), 0))

    if exchange is None and filled is not None:
        outs = pl.pallas_call(
            body, name=name, out_shape=[jax.ShapeDtypeStruct(w.shape, F32)] * 4, grid=(ng, nt),
            in_specs=[big, big, big] + [gspec(k) for k in range(ng)] + [_ANY] * nfill, out_specs=[big, big, big, big],
            input_output_aliases={3 + ng + k: k for k in range(nfill)},
            compiler_params=pltpu.CompilerParams(dimension_semantics=("arbitrary", "arbitrary")),
        )(w, m, v, *gs, *filled)
        return list(outs), []
    assert filled is None
    return _grid_call(body, name, (ng, nt), in_specs=[big, big, big] + [gspec(k) for k in range(ng)],
                      out_specs=[big, big, big, big], out_shape=[jax.ShapeDtypeStruct(w.shape, F32)] * 4,
                      scratch_shapes=[], args=(w, m, v, *gs), exchange=exchange)


def _adamw_flat(w, g, m, v, name):
    r, ccols = w.shape

    def body(w_ref, g_ref, m_ref, v_ref, d_ref, mo_ref, vo_ref):
        delta, m2, v2 = _adam_math(w_ref[...], g_ref[...], m_ref[...], v_ref[...])
        d_ref[...] = delta
        mo_ref[...] = m2
        vo_ref[...] = v2

    return pl.pallas_call(
        body, name=name, out_shape=[jax.ShapeDtypeStruct((r, ccols), F32)] * 3,
        in_specs=[_VMEM] * 4, out_specs=[_VMEM] * 3,
        compiler_params=pltpu.CompilerParams(vmem_limit_bytes=VMEM_LIMIT),
    )(w, g, m, v)


def _ada_forward(c_all, ada_w, ada_b_cols, name):
    nl, d, ncols = ada_w.shape
    bg = c_all.shape[0]
    tn = 512 if ncols % 512 == 0 else ncols

    def body(c_ref, w_ref, b_ref, o_ref):
        cv = c_ref[...]
        ca = (cv * _sigmoid(cv)).astype(BF16)
        o_ref[...] = jnp.dot(ca, w_ref[...].astype(BF16), preferred_element_type=F32) + b_ref[...]

    return pl.pallas_call(
        body, name=name, out_shape=jax.ShapeDtypeStruct((nl, bg, ncols), F32),
        grid=(nl, ncols // tn),
        in_specs=[pl.BlockSpec((bg, d), lambda l, j: (0, 0)),
                  pl.BlockSpec((None, d, tn), lambda l, j: (l, 0, j)),
                  pl.BlockSpec((None, 1, tn), lambda l, j: (l, 0, j))],
        out_specs=pl.BlockSpec((None, bg, tn), lambda l, j: (l, 0, j)),
        compiler_params=pltpu.CompilerParams(dimension_semantics=("arbitrary", "arbitrary")),
    )(c_all, ada_w, ada_b_cols)


def _ada_update(c_all, dmod_cols, w, m, v, name, exchange=None):
    nl, d, ncols = w.shape
    bg = c_all.shape[0]
    tn = 512 if ncols % 512 == 0 else ncols

    def body(c_ref, dm_ref, w_ref, m_ref, v_ref, go_ref, d_ref, mo_ref, vo_ref):
        cv = c_ref[...]
        ca = (cv * _sigmoid(cv)).astype(BF16)
        g = lax.dot_general(ca, dm_ref[...].astype(BF16), (((0,), (0,)), ((), ())), preferred_element_type=F32)
        delta, m2, v2 = _adam_math(w_ref[...], g, m_ref[...], v_ref[...])
        go_ref[...] = g
        d_ref[...] = delta
        mo_ref[...] = m2
        vo_ref[...] = v2

    big = pl.BlockSpec((None, d, tn), lambda l, j: (l, 0, j))
    return _grid_call(
        body, name, (nl, ncols // tn),
        in_specs=[pl.BlockSpec((bg, d), lambda l, j: (0, 0)),
                  pl.BlockSpec((None, bg, tn), lambda l, j: (l, 0, j)), big, big, big],
        out_specs=[big, big, big, big], out_shape=[jax.ShapeDtypeStruct(w.shape, F32)] * 4,
        scratch_shapes=[], args=(c_all, dmod_cols, w, m, v), exchange=exchange)


def _load_weights(first, pairs, sems):
    @pl.when(first)
    def _():
        cps = [pltpu.make_async_copy(src, dst, sems.at[k]) for k, (src, dst) in enumerate(pairs)]
        for cp in cps:
            cp.start()
        for cp in cps:
            cp.wait()


def _ada_norm(xv, g, sc, sh):
    r = lax.rsqrt(jnp.mean(xv * xv, axis=-1, keepdims=True) + EPS)
    xn = xv * r
    return (xn * g) * (1.0 + sc) + sh, xn, r


def _ada_norm_bwd(dh, xn, r, g, sc):
    d_sh = _colsum(dh)
    d_sc = _colsum(dh * (xn * g))
    dxg = dh * (1.0 + sc)
    d_g = _colsum(dxg * xn)
    gd = dxg * g
    dx = r * (gd - xn * jnp.mean(gd * xn, axis=-1, keepdims=True))
    return dx, d_sh, d_sc, d_g


def _gated_residual_bwd(dxo, o, g_post, gt):
    r = lax.rsqrt(jnp.mean(o * o, axis=-1, keepdims=True) + EPS)
    on = o * r
    d_gt = _colsum(dxo * (on * g_post))
    dy = dxo * (1.0 + gt)
    d_gp = _colsum(dy * on)
    gd = dy * g_post
    do = r * (gd - on * jnp.mean(gd * on, axis=-1, keepdims=True))
    return do, d_gt, d_gp


def _seq_positions(i, tm, width):
    return i * tm + lax.broadcasted_iota(jnp.int32, (tm, width), 0)


def _fill_phases(ext, phases):
    rows = ext.shape[0]
    ev = ext[...]
    for r in range(1, SUBLANES):
        phases[r - 1] = pltpu.roll(ev, rows - r, axis=0)


def _shifted_rows(ext, phases, offset, n):
    q, r = divmod(offset, SUBLANES)
    if r == 0:
        return ext[pl.ds(q * SUBLANES, n), :]
    return phases[r - 1, pl.ds(q * SUBLANES, n), :]


def _rows_before(halo, cur, shift):
    e = jnp.concatenate([halo, cur], axis=0)
    return pltpu.roll(e, shift, axis=0)[halo.shape[0]:, :]


def _rows_after(cur, halo, shift):
    e = jnp.concatenate([cur, halo], axis=0)
    return pltpu.roll(e, e.shape[0] - shift, axis=0)[:cur.shape[0], :]


def _mixer_forward(x, mod, vec_d, vec_c, cw, pw, win_g, wout_g, taps, tm, name, exchange=None):
    nb, s, d = x.shape
    n = s // tm
    nj, _, dcol = win_g.shape
    din = nj * dcol
    dc = vec_c.shape[-1]
    dpool = din - 2 * dc
    dmix = dc + dpool
    ro = wout_g.shape[1]
    ngrp = dpool // LANES

    def body(x_ref, mod_ref, vd_ref, vc_ref, cw_ref, pw_ref, win_hbm, wout_hbm,
             xo_ref, h_ref, u_ref, ac_ref, dp_ref, z_ref, o_ref,
             win_v, wout_v, ext_a, ext_p, phases, sems):
        b, i = pl.program_id(0), pl.program_id(1)
        pairs = [(win_hbm.at[j], win_v.at[:, pl.ds(j * dcol, dcol)]) for j in range(nj)]
        pairs += [(wout_hbm.at[j], wout_v.at[pl.ds(j * ro, ro), :]) for j in range(nj)]
        _load_weights((b == 0) & (i == 0), pairs, sems)

        xv = x_ref[...]
        h, _, _ = _ada_norm(xv, vd_ref[0:1, :], mod_ref[1:2, :], mod_ref[0:1, :])
        hb = h.astype(BF16)
        h_ref[...] = hb
        u = jnp.dot(hb, win_v[...], preferred_element_type=F32)
        u_ref[...] = u.astype(BF16)
        ag = u[:, :dc] * _sigmoid(u[:, dc:2 * dc])
        up = u[:, 2 * dc:]

        @pl.when(i == 0)
        def _():
            ext_a[0:HALO, :] = jnp.zeros((HALO, dc), F32)
            ext_p[0:HALO, :] = jnp.zeros((HALO, dpool), F32)

        @pl.when(i > 0)
        def _():
            ext_a[0:HALO, :] = ext_a[tm:tm + HALO, :]
            ext_p[0:HALO, :] = ext_p[tm:tm + HALO, :]

        ext_a[HALO:HALO + tm, :] = ag
        ext_p[HALO:HALO + tm, :] = up

        acc = jnp.broadcast_to(vc_ref[0:1, :], (tm, dc))
        _fill_phases(ext_a, phases)
        for k in range(taps):
            acc = acc + cw_ref[k:k + 1, :] * _shifted_rows(ext_a, phases, HALO - (taps - 1) + k, tm)
        ac_ref[...] = acc.astype(BF16)
        mu = jnp.mean(acc, axis=-1, keepdims=True)
        xc = acc - mu
        var = jnp.mean(xc * xc, axis=-1, keepdims=True)
        al = (xc * lax.rsqrt(var + EPS)) * vc_ref[1:2, :] + vc_ref[2:3, :]
        a = al * _sigmoid(al)

        pos = _seq_positions(i, tm, LANES)
        parts = [a.astype(BF16)]
        for g in range(ngrp):
            w = POOL_WINDOWS[g]
            cols = slice(g * LANES, (g + 1) * LANES)
            sw = ext_p[:, cols]
            step = 1
            while step < w:
                sw = sw + pltpu.roll(sw, step, axis=0)
                step *= 2
            cnt = jnp.minimum(pos + 1, w).astype(F32)
            dg = (sw[HALO:, :] / cnt - up[:, cols]).astype(BF16)
            dp_ref[:, cols] = dg
            q = jnp.dot(dg, pw_ref[g], preferred_element_type=F32)
            parts.append((q * vc_ref[3:4, cols]).astype(BF16))
        z = jnp.concatenate(parts, axis=-1)
        z_ref[...] = z
        o = jnp.dot(z, wout_v[...], preferred_element_type=F32)
        o_ref[...] = o
        r2 = lax.rsqrt(jnp.mean(o * o, axis=-1, keepdims=True) + EPS)
        xo_ref[...] = xv + (1.0 + mod_ref[2:3, :]) * ((o * r2) * vd_ref[1:2, :])

    def tile(width):
        return pl.BlockSpec((None, tm, width), lambda b, i: (b, i, 0))

    return _grid_call(
        body, name, (nb, n),
        in_specs=[tile(d), pl.BlockSpec((None, 8, d), lambda b, i: (b, 0, 0)), _full(vec_d.shape), _full(vec_c.shape),
                  _full(cw.shape), _full(pw.shape), _ANY, _ANY],
        out_specs=[tile(d), tile(d), tile(din), tile(dc), tile(dpool), tile(dmix), tile(d)],
        out_shape=[jax.ShapeDtypeStruct((nb, s, d), F32), jax.ShapeDtypeStruct((nb, s, d), BF16),
                   jax.ShapeDtypeStruct((nb, s, din), BF16), jax.ShapeDtypeStruct((nb, s, dc), BF16),
                   jax.ShapeDtypeStruct((nb, s, dpool), BF16), jax.ShapeDtypeStruct((nb, s, dmix), BF16),
                   jax.ShapeDtypeStruct((nb, s, d), F32)],
        scratch_shapes=[pltpu.VMEM((d, din), BF16), pltpu.VMEM((dmix, d), BF16),
                        pltpu.VMEM((HALO + tm, dc), F32), pltpu.VMEM((HALO + tm, dpool), F32),
                        pltpu.VMEM((SUBLANES - 1, HALO + tm, dc), F32), pltpu.SemaphoreType.DMA((2 * nj,))],
        args=(x, mod, vec_d, vec_c, cw, pw, win_g, wout_g), exchange=exchange)


def _mixer_backward(dxo, x, o, u, ac, dpl, mod, vec_d, vec_c, cw, pw, win_g, wout_g, taps, tm, name, exchange=None):
    nb, s, d = x.shape
    n = s // tm
    nj, _, dcol = win_g.shape
    din = nj * dcol
    dc = vec_c.shape[-1]
    dpool = din - 2 * dc
    dmix = dc + dpool
    ro = wout_g.shape[1]
    ngrp = dpool // LANES
    rext = tm + HALO

    def body(dxo_ref, x_ref, o_ref, u_ref, ac_ref, dp_ref, mod_ref, vd_ref, vc_ref, cw_ref, pw_ref, win_hbm, wout_hbm,
             dx_ref, du_ref, dob_ref, rowd_ref, rowb_ref, rowc_ref, dcw_ref, dpw_ref,
             win_v, wout_v, ext_a, ext_p, phases, sems):
        b, i = pl.program_id(0), pl.program_id(1)
        first = (b == 0) & (i == 0)
        pairs = [(win_hbm.at[j], win_v.at[:, pl.ds(j * dcol, dcol)]) for j in range(nj)]
        pairs += [(wout_hbm.at[j], wout_v.at[pl.ds(j * ro, ro), :]) for j in range(nj)]
        _load_weights(first, pairs, sems)

        @pl.when(first)
        def _():
            rowd_ref[...] = jnp.zeros_like(rowd_ref)
            rowc_ref[...] = jnp.zeros_like(rowc_ref)
            dcw_ref[...] = jnp.zeros_like(dcw_ref)
            dpw_ref[...] = jnp.zeros_like(dpw_ref)

        @pl.when(i == 0)
        def _():
            rowb_ref[...] = jnp.zeros_like(rowb_ref)
            ext_a[tm:rext, :] = jnp.zeros((HALO, dc), F32)
            ext_p[tm:rext, :] = jnp.zeros((HALO, dpool), F32)

        @pl.when(i > 0)
        def _():
            ext_a[tm:rext, :] = ext_a[0:HALO, :]
            ext_p[tm:rext, :] = ext_p[0:HALO, :]

        g_pre, g_post = vd_ref[0:1, :], vd_ref[1:2, :]
        sh, sc, gt = mod_ref[0:1, :], mod_ref[1:2, :], mod_ref[2:3, :]
        do, d_gt, d_gp = _gated_residual_bwd(dxo_ref[...], o_ref[...], g_post, gt)
        dob = do.astype(BF16)
        dob_ref[...] = dob
        dz = lax.dot_general(dob, wout_v[...], (((1,), (1,)), ((), ())), preferred_element_type=F32)

        acv = ac_ref[...].astype(F32)
        mu = jnp.mean(acv, axis=-1, keepdims=True)
        xc = acv - mu
        rstd = lax.rsqrt(jnp.mean(xc * xc, axis=-1, keepdims=True) + EPS)
        an = xc * rstd
        lg = vc_ref[1:2, :]
        al = an * lg + vc_ref[2:3, :]
        sg = _sigmoid(al)
        dal = dz[:, :dc] * (sg * (1.0 + al * (1.0 - sg)))
        d_lg = _colsum(dal * an)
        d_lb = _colsum(dal)
        dan = dal * lg
        dac = rstd * (dan - jnp.mean(dan, axis=-1, keepdims=True) - an * jnp.mean(dan * an, axis=-1, keepdims=True))
        d_cb = _colsum(dac)
        ext_a[0:tm, :] = dac
        uv = u_ref[:, 0:dc].astype(F32)
        sgu = _sigmoid(u_ref[:, dc:2 * dc].astype(F32))
        ag = uv * sgu
        dag = jnp.zeros((tm, dc), F32)
        _fill_phases(ext_a, phases)
        for k in range(taps):
            sl = _shifted_rows(ext_a, phases, taps - 1 - k, tm)
            dag = dag + cw_ref[k:k + 1, :] * sl
            dcw_ref[k:k + 1, :] += _colsum(ag * sl)
        du_ref[:, 0:dc] = (dag * sgu).astype(BF16)
        du_ref[:, dc:2 * dc] = (dag * uv * (sgu * (1.0 - sgu))).astype(BF16)

        pos = _seq_positions(n - 1 - i, tm, LANES)
        d_ps = []
        for g in range(ngrp):
            w = POOL_WINDOWS[g]
            cols = slice(g * LANES, (g + 1) * LANES)
            gcols = slice(dc + g * LANES, dc + (g + 1) * LANES)
            dgb = dp_ref[:, cols]
            q = jnp.dot(dgb, pw_ref[g], preferred_element_type=F32)
            dpg = dz[:, gcols]
            d_ps.append(_colsum(dpg * q))
            dq = (dpg * vc_ref[3:4, cols]).astype(BF16)
            dpw_ref[g] += lax.dot_general(dgb, dq, (((0,), (0,)), ((), ())), preferred_element_type=F32)
            dd = lax.dot_general(dq, pw_ref[g], (((1,), (1,)), ((), ())), preferred_element_type=F32)
            cnt = jnp.minimum(pos + 1, w).astype(F32)
            ext_p[0:tm, cols] = dd / cnt
            sw = ext_p[:, cols]
            step = 1
            while step < w:
                sw = sw + pltpu.roll(sw, rext - step, axis=0)
                step *= 2
            du_ref[:, 2 * dc + g * LANES:2 * dc + (g + 1) * LANES] = (sw[0:tm, :] - dd).astype(BF16)
        rowc_ref[0:1, :] += d_cb
        rowc_ref[1:2, :] += d_lg
        rowc_ref[2:3, :] += d_lb
        rowc_ref[3:4, :] += jnp.concatenate(d_ps, axis=-1)

        dh = lax.dot_general(du_ref[...], win_v[...], (((1,), (1,)), ((), ())), preferred_element_type=F32)
        _, xn, r1 = _ada_norm(x_ref[...], g_pre, sc, sh)
        dxb, d_sh, d_sc, d_g = _ada_norm_bwd(dh, xn, r1, g_pre, sc)
        dx_ref[...] = dxo_ref[...] + dxb
        rowd_ref[0:1, :] += d_g
        rowd_ref[1:2, :] += d_gp
        rowb_ref[0:1, :] += d_sh
        rowb_ref[1:2, :] += d_sc
        rowb_ref[2:3, :] += d_gt

    def tile(width):
        return pl.BlockSpec((None, tm, width), lambda b, i: (b, n - 1 - i, 0))

    return _grid_call(
        body, name, (nb, n),
        in_specs=[tile(d), tile(d), tile(d), tile(din), tile(dc), tile(dpool),
                  pl.BlockSpec((None, 8, d), lambda b, i: (b, 0, 0)), _full(vec_d.shape), _full(vec_c.shape),
                  _full(cw.shape), _full(pw.shape), _ANY, _ANY],
        out_specs=[tile(d), tile(din), tile(d), _full((8, d)), pl.BlockSpec((None, 8, d), lambda b, i: (b, 0, 0)),
                   _full((8, dc)), _full(cw.shape), _full(pw.shape)],
        out_shape=[jax.ShapeDtypeStruct((nb, s, d), F32), jax.ShapeDtypeStruct((nb, s, din), BF16),
                   jax.ShapeDtypeStruct((nb, s, d), BF16), jax.ShapeDtypeStruct((8, d), F32),
                   jax.ShapeDtypeStruct((nb, 8, d), F32), jax.ShapeDtypeStruct((8, dc), F32),
                   jax.ShapeDtypeStruct(cw.shape, F32), jax.ShapeDtypeStruct(pw.shape, F32)],
        scratch_shapes=[pltpu.VMEM((d, din), BF16), pltpu.VMEM((dmix, d), BF16),
                        pltpu.VMEM((rext, dc), F32), pltpu.VMEM((rext, dpool), F32),
                        pltpu.VMEM((SUBLANES - 1, rext, dc), F32), pltpu.SemaphoreType.DMA((2 * nj,))],
        args=(dxo, x, o, u, ac, dpl, mod, vec_d, vec_c, cw, pw, win_g, wout_g), exchange=exchange)


def _ffn_forward(x, mod, vec_d, fw, wup_g, wdn_g, tm, name, exchange=None):
    nb, s, d = x.shape
    n = s // tm
    nj, _, ucol = wup_g.shape
    f2 = nj * ucol
    dff = f2 // 2
    rd = wdn_g.shape[1]
    nq = nj // 2
    cs = dff // nq

    def body(x_ref, mod_ref, vd_ref, fw_ref, wup_hbm, wdn_hbm,
             xo_ref, h_ref, u_ref, hid_ref, o_ref,
             wup_v, wdn_v, prev_u, sems):
        b, i = pl.program_id(0), pl.program_id(1)
        pairs = [(wup_hbm.at[j], wup_v.at[:, pl.ds(j * ucol, ucol)]) for j in range(nj)]
        pairs += [(wdn_hbm.at[j], wdn_v.at[pl.ds(j * rd, rd), :]) for j in range(nj)]
        _load_weights((b == 0) & (i == 0), pairs, sems)

        @pl.when(i == 0)
        def _():
            prev_u[...] = jnp.zeros_like(prev_u)

        xv = x_ref[...]
        h, _, _ = _ada_norm(xv, vd_ref[2:3, :], mod_ref[4:5, :], mod_ref[3:4, :])
        hb = h.astype(BF16)
        h_ref[...] = hb

        def conv(cols):
            uc = jnp.dot(hb, wup_v[:, cols], preferred_element_type=F32)
            u_ref[:, cols] = uc.astype(BF16)
            before = prev_u[:, cols]
            prev_u[:, cols] = uc[tm - FHALO:, :]
            return (fw_ref[3:4, cols] + fw_ref[2:3, cols] * uc + fw_ref[1:2, cols] * _rows_before(before, uc, 1)
                    + fw_ref[0:1, cols] * _rows_before(before, uc, 2))

        o = jnp.zeros((tm, d), F32)
        for q in range(nq):
            val = conv(pl.ds(q * cs, cs))
            gate = conv(pl.ds(dff + q * cs, cs))
            hid = ((gate * _sigmoid(gate)) * val).astype(BF16)
            hid_ref[:, pl.ds(q * cs, cs)] = hid
            o = o + jnp.dot(hid, wdn_v[pl.ds(q * cs, cs), :], preferred_element_type=F32)
        o_ref[...] = o
        r2 = lax.rsqrt(jnp.mean(o * o, axis=-1, keepdims=True) + EPS)
        xo_ref[...] = xv + (1.0 + mod_ref[5:6, :]) * ((o * r2) * vd_ref[3:4, :])

    def tile(width):
        return pl.BlockSpec((None, tm, width), lambda b, i: (b, i, 0))

    return _grid_call(
        body, name, (nb, n),
        in_specs=[tile(d), pl.BlockSpec((None, 8, d), lambda b, i: (b, 0, 0)), _full(vec_d.shape), _full(fw.shape),
                  _ANY, _ANY],
        out_specs=[tile(d), tile(d), tile(f2), tile(dff), tile(d)],
        out_shape=[jax.ShapeDtypeStruct((nb, s, d), F32), jax.ShapeDtypeStruct((nb, s, d), BF16),
                   jax.ShapeDtypeStruct((nb, s, f2), BF16), jax.ShapeDtypeStruct((nb, s, dff), BF16),
                   jax.ShapeDtypeStruct((nb, s, d), F32)],
        scratch_shapes=[pltpu.VMEM((d, f2), BF16), pltpu.VMEM((dff, d), BF16),
                        pltpu.VMEM((FHALO, f2), F32), pltpu.SemaphoreType.DMA((2 * nj,))],
        args=(x, mod, vec_d, fw, wup_g, wdn_g), exchange=exchange)


def _ffn_backward(dxo, x, o, u, mod, vec_d, fw, wup_g, wdn_g, tm, name, exchange=None):
    nb, s, d = x.shape
    n = s // tm
    nj, _, ucol = wup_g.shape
    f2 = nj * ucol
    dff = f2 // 2
    rd = wdn_g.shape[1]
    nq = nj // 2
    cs = dff // nq
    hb_per_tile = tm // FHALO

    def body(dxo_ref, x_ref, o_ref, u_ref, uh_ref, mod_ref, vd_ref, fw_ref, wup_hbm, wdn_hbm,
             dx_ref, du_ref, dob_ref, rowd_ref, rowb_ref, dfw_ref,
             wup_v, wdn_v, next_d, sems):
        b, i = pl.program_id(0), pl.program_id(1)
        first = (b == 0) & (i == 0)
        pairs = [(wup_hbm.at[j], wup_v.at[:, pl.ds(j * ucol, ucol)]) for j in range(nj)]
        pairs += [(wdn_hbm.at[j], wdn_v.at[pl.ds(j * rd, rd), :]) for j in range(nj)]
        _load_weights(first, pairs, sems)

        @pl.when(first)
        def _():
            rowd_ref[...] = jnp.zeros_like(rowd_ref)
            dfw_ref[...] = jnp.zeros_like(dfw_ref)

        @pl.when(i == 0)
        def _():
            rowb_ref[...] = jnp.zeros_like(rowb_ref)
            next_d[...] = jnp.zeros_like(next_d)

        g_pre, g_post = vd_ref[2:3, :], vd_ref[3:4, :]
        sh, sc, gt = mod_ref[3:4, :], mod_ref[4:5, :], mod_ref[5:6, :]
        do, d_gt, d_gp = _gated_residual_bwd(dxo_ref[...], o_ref[...], g_post, gt)
        dob = do.astype(BF16)
        dob_ref[...] = dob

        keep = jnp.where(i == n - 1, 0.0, 1.0)

        def conv(cols):
            uc = u_ref[:, cols].astype(F32)
            before = uh_ref[:, cols].astype(F32) * keep
            return (fw_ref[3:4, cols] + fw_ref[2:3, cols] * uc + fw_ref[1:2, cols] * _rows_before(before, uc, 1)
                    + fw_ref[0:1, cols] * _rows_before(before, uc, 2))

        def conv_bwd(cols, duc):
            uc = u_ref[:, cols].astype(F32)
            after = next_d[:, cols]
            next_d[:, cols] = duc[0:FHALO, :]
            d1 = _rows_after(duc, after, 1)
            d2 = _rows_after(duc, after, 2)
            dfw_ref[3:4, cols] += _colsum(duc)
            dfw_ref[2:3, cols] += _colsum(uc * duc)
            dfw_ref[1:2, cols] += _colsum(uc * d1)
            dfw_ref[0:1, cols] += _colsum(uc * d2)
            ob = (fw_ref[2:3, cols] * duc + fw_ref[1:2, cols] * d1 + fw_ref[0:1, cols] * d2).astype(BF16)
            du_ref[:, cols] = ob
            return lax.dot_general(ob, wup_v[:, cols], (((1,), (1,)), ((), ())), preferred_element_type=F32)

        dh = jnp.zeros((tm, d), F32)
        for q in range(nq):
            vcols = pl.ds(q * cs, cs)
            gcols = pl.ds(dff + q * cs, cs)
            dhid = lax.dot_general(dob, wdn_v[vcols, :], (((1,), (1,)), ((), ())), preferred_element_type=F32)
            val = conv(vcols)
            gate = conv(gcols)
            sg = _sigmoid(gate)
            dval = dhid * (gate * sg)
            dgate = dhid * val * (sg * (1.0 + gate * (1.0 - sg)))
            dh = dh + conv_bwd(vcols, dval)
            dh = dh + conv_bwd(gcols, dgate)

        _, xn, r1 = _ada_norm(x_ref[...], g_pre, sc, sh)
        dxb, d_sh, d_sc, d_g = _ada_norm_bwd(dh, xn, r1, g_pre, sc)
        dx_ref[...] = dxo_ref[...] + dxb
        rowd_ref[2:3, :] += d_g
        rowd_ref[3:4, :] += d_gp
        rowb_ref[3:4, :] += d_sh
        rowb_ref[4:5, :] += d_sc
        rowb_ref[5:6, :] += d_gt

    def tile(width):
        return pl.BlockSpec((None, tm, width), lambda b, i: (b, n - 1 - i, 0))

    halo = pl.BlockSpec((None, FHALO, f2), lambda b, i: (b, jnp.maximum((n - 1 - i) * hb_per_tile - 1, 0), 0))
    return _grid_call(
        body, name, (nb, n),
        in_specs=[tile(d), tile(d), tile(d), tile(f2), halo, pl.BlockSpec((None, 8, d), lambda b, i: (b, 0, 0)),
                  _full(vec_d.shape), _full(fw.shape), _ANY, _ANY],
        out_specs=[tile(d), tile(f2), tile(d), _full((8, d)), pl.BlockSpec((None, 8, d), lambda b, i: (b, 0, 0)),
                   _full(fw.shape)],
        out_shape=[jax.ShapeDtypeStruct((nb, s, d), F32), jax.ShapeDtypeStruct((nb, s, f2), BF16),
                   jax.ShapeDtypeStruct((nb, s, d), BF16), jax.ShapeDtypeStruct((8, d), F32),
                   jax.ShapeDtypeStruct((nb, 8, d), F32), jax.ShapeDtypeStruct(fw.shape, F32)],
        scratch_shapes=[pltpu.VMEM((d, f2), BF16), pltpu.VMEM((dff, d), BF16),
                        pltpu.VMEM((FHALO, f2), F32), pltpu.SemaphoreType.DMA((2 * nj,))],
        args=(dxo, x, o, u, u, mod, vec_d, fw, wup_g, wdn_g), exchange=exchange)


def _weight_grad(a, b, nblk, split, tt, name, exchange=None):
    t, ka = a.shape
    nb_ = b.shape[1]
    nk = t // tt
    if split == "cols":
        wa, wb, grid = ka, nb_ // nblk, (1, nk)
        a_spec = pl.BlockSpec((tt, ka), lambda j, k: (k, 0))
        b_spec = pl.BlockSpec((tt, nb_), lambda j, k: (k, 0))
        o_spec = pl.BlockSpec((nblk, wa, wb), lambda j, k: (0, 0, 0))
        acc_shape = (ka, nb_)
    elif split == "b":
        wa, wb, grid = ka, nb_ // nblk, (nblk, nk)
        a_spec = pl.BlockSpec((tt, wa), lambda j, k: (k, 0))
        b_spec = pl.BlockSpec((tt, wb), lambda j, k: (k, j))
        o_spec = pl.BlockSpec((None, wa, wb), lambda j, k: (j, 0, 0))
        acc_shape = (wa, wb)
    else:
        wa, wb, grid = ka // nblk, nb_, (nblk, nk)
        a_spec = pl.BlockSpec((tt, wa), lambda j, k: (k, j))
        b_spec = pl.BlockSpec((tt, wb), lambda j, k: (k, 0))
        o_spec = pl.BlockSpec((None, wa, wb), lambda j, k: (j, 0, 0))
        acc_shape = (wa, wb)

    def body(a_ref, b_ref, o_ref, acc):
        k = pl.program_id(1)
        prod = lax.dot_general(a_ref[...], b_ref[...], (((0,), (0,)), ((), ())), preferred_element_type=F32)

        @pl.when(k == 0)
        def _():
            acc[...] = prod

        @pl.when(k > 0)
        def _():
            acc[...] += prod

        @pl.when(k == nk - 1)
        def _():
            if split == "cols":
                for j in range(nblk):
                    o_ref[j] = acc[:, j * wb:(j + 1) * wb].astype(o_ref.dtype)
            else:
                o_ref[...] = acc[...].astype(o_ref.dtype)

    outs, exo = _grid_call(body, name, grid, in_specs=[a_spec, b_spec], out_specs=[o_spec],
                           out_shape=[jax.ShapeDtypeStruct((nblk, wa, wb), BF16)],
                           scratch_shapes=[pltpu.VMEM(acc_shape, F32)], args=(a, b), exchange=exchange)
    return outs[0], exo


def _loss_grad(y, tgt, tm, name):
    nb, s, d = y.shape
    n = s // tm

    def body(y_ref, t_ref, dy_ref, sq_ref):
        @pl.when((pl.program_id(0) == 0) & (pl.program_id(1) == 0))
        def _():
            sq_ref[...] = jnp.zeros_like(sq_ref)

        e = y_ref[...] - t_ref[...]
        dy_ref[...] = e * (1.0 / d)
        sq_ref[0:1, :] += _colsum(e * e)

    tile = pl.BlockSpec((None, tm, d), lambda b, i: (b, i, 0))
    return pl.pallas_call(
        body, name=name, out_shape=[jax.ShapeDtypeStruct((nb, s, d), F32), jax.ShapeDtypeStruct((8, d), F32)],
        grid=(nb, n), in_specs=[tile, tile], out_specs=[tile, _full((8, d))],
        compiler_params=pltpu.CompilerParams(dimension_semantics=("arbitrary", "arbitrary")),
    )(y, tgt)


def _rows128(a):
    return a.reshape(-1, LANES)


class _ReduceScatter:
    def __init__(self, gs, cidx, idx, tag):
        self.gs, self.cidx, self.idx, self.tag = gs, cidx, idx, tag

    def swap(self):
        return _swap_halves(self.gs)

    def after_swap(self, r1):
        self.ps = [_pair_sum(g, r, self.cidx, name=f"rs_pair_{self.tag}_{a}") for a, (g, r) in enumerate(zip(self.gs, r1))]

    def chips(self, which=None):
        return _chip_exchange(self.ps if which is None else [self.ps[a] for a in which])

    def after_chips(self, r2):
        self.fh = [_chip_sum(p, r, self.idx, name=f"rs_sum_{self.tag}_{a}") for a, (p, r) in enumerate(zip(self.ps, r2))]

    def share(self):
        return _sibling_share(self.fh)

    @staticmethod
    def result(fs):
        return [f.reshape(f.shape[0] * f.shape[1], f.shape[2]) for f in fs]


def kernel(x, c, ada_w, ada_b, pre_mix_g, post_mix_g, w_in, conv_w, conv_b, conv_ln_g, conv_ln_b, pool_w, pool_scale, w_out, pre_ffn_g, post_ffn_g, ffn_up, ffn_conv_w, ffn_conv_b, ffn_down, loss_target, m_ada_w, m_ada_b, m_pre_mix_g, m_post_mix_g, m_w_in, m_conv_w, m_conv_b, m_conv_ln_g, m_conv_ln_b, m_pool_w, m_pool_scale, m_w_out, m_pre_ffn_g, m_post_ffn_g, m_ffn_up, m_ffn_conv_w, m_ffn_conv_b, m_ffn_down, v_ada_w, v_ada_b, v_pre_mix_g, v_post_mix_g, v_w_in, v_conv_w, v_conv_b, v_conv_ln_g, v_conv_ln_b, v_pool_w, v_pool_scale, v_w_out, v_pre_ffn_g, v_post_ffn_g, v_ffn_up, v_ffn_conv_w, v_ffn_conv_b, v_ffn_down):
    nb, s, d = x.shape
    nl = w_in.shape[0]
    taps = conv_w.shape[1]
    ccol = conv_w.shape[2]
    dc = conv_b.shape[1]
    fcol = ffn_conv_w.shape[2]
    f2 = ffn_conv_b.shape[1]
    nmod = ada_b.shape[1] // d
    acol = ada_w.shape[2]
    tm = min(256, s)
    tt = min(2048, (nb * s) // 2)

    xi, yi, ci = _pos()
    jm = 2 * xi + yi
    me = 4 * xi + 2 * yi + ci
    cidx = jnp.reshape(ci, (1,)).astype(jnp.int32)
    idx = jnp.stack([jm, ci]).astype(jnp.int32)

    n_cw, n_fw, n_c = nl * taps * ccol, nl * 3 * fcol, nb * d
    packed = jnp.concatenate([conv_w.reshape(-1), ffn_conv_w.reshape(-1), c.reshape(-1)])
    got = _gather8(_rows128(packed), name="gather_small").reshape(N_DEV, -1)
    chips = got[0::2]
    cw_full = chips[:, :n_cw].reshape(N_CHIPS, nl, taps, ccol).transpose(1, 2, 0, 3).reshape(nl, taps, dc)
    fw_full = chips[:, n_cw:n_cw + n_fw].reshape(N_CHIPS, nl, 3, fcol).transpose(1, 2, 0, 3).reshape(nl, 3, f2)
    c_all = got[:, n_cw + n_fw:].reshape(N_DEV * nb, d)

    ada_b_cols = lax.dynamic_slice_in_dim(ada_b, jm * acol, acol, axis=1).reshape(nl, 1, acol)
    mod_cols = _ada_forward(c_all, ada_w, ada_b_cols, name="ada_forward")
    mod_all = _gather8(_rows128(mod_cols), name="gather_mod").reshape(N_DEV, nl, N_DEV * nb, acol)
    mod_all = mod_all[0::2].transpose(1, 2, 0, 3).reshape(nl, N_DEV * nb, N_CHIPS * acol)
    mod_own = lax.dynamic_slice_in_dim(mod_all, me * nb, nb, axis=1).reshape(nl, nb, nmod, d)
    mod_own = jnp.pad(mod_own, ((0, 0), (0, 0), (0, 8 - nmod), (0, 0)))

    vec_d = jnp.stack([pre_mix_g, post_mix_g, pre_ffn_g, post_ffn_g], axis=1)
    vec_c = jnp.stack([conv_b, conv_ln_g, conv_ln_b, pool_scale], axis=1)
    cw_pad = jnp.pad(cw_full, ((0, 0), (0, HALO - taps), (0, 0)))
    fw_rows = jnp.concatenate([fw_full, ffn_conv_b[:, None, :], jnp.zeros((nl, 4, f2), F32)], axis=1)
    pw_b = pool_w.astype(BF16)

    win_b, wout_b, wup_b, wdn_b = (w.astype(BF16) for w in (w_in, w_out, ffn_up, ffn_down))

    def others(l):
        return [win_b[l], wout_b[l], wdn_b[l]]

    def complete(part, tag):
        return _run_exchange(_gather_sibling(part), name=f"gather_sibling_{tag}")

    win_g, wout_g, wdn_g = complete(_run_exchange(_gather_chips(others(0)), name="gather_chips_0"), "0")
    saved = []
    xs = x
    for l in range(nl):
        (x1, h1, u1, ac1, dp1, z1, o1), part = _mixer_forward(
            xs, mod_own[l], vec_d[l], vec_c[l], cw_pad[l], pw_b[l], win_g, wout_g, taps, tm, name=f"mixer_fwd_{l}",
            exchange=_gather_chips([wup_b[l]]))
        wup_g, = complete(part, f"up_{l}")
        (x2, h2, u2, hid2, o2), part = _ffn_forward(
            x1, mod_own[l], vec_d[l], fw_rows[l], wup_g, wdn_g, tm, name=f"ffn_fwd_{l}",
            exchange=_gather_chips(others(l + 1)) if l + 1 < nl else None)
        saved.append((xs, h1, u1, ac1, dp1, z1, o1, x1, h2, u2, hid2, o2, win_g, wout_g, wup_g, wdn_g))
        if l + 1 < nl:
            win_g, wout_g, wdn_g = complete(part, f"{l + 1}")
        xs = x2

    dx, sq = _loss_grad(xs, loss_target, tm, name="loss_grad")
    loss = lax.psum(0.5 * jnp.sum(sq) / d, ("x", "y", "c"))

    flat = lambda a: a.reshape(nb * s, a.shape[-1])
    small = [None] * nl
    big = [None] * nl
    pend = None
    for l in reversed(range(nl)):
        x0, h1, u1, ac1, dp1, z1, o1, x1, h2, u2, hid2, o2, win_g, wout_g, wup_g, wdn_g = saved[l]
        (dx, du2, do2, rowd2, rowb2, dfw), got = _ffn_backward(
            dx, x1, o2, u2, mod_own[l], vec_d[l], fw_rows[l], wup_g, wdn_g, tm, name=f"ffn_bwd_{l}",
            exchange=pend.swap() if pend else None)
        if pend:
            pend.after_swap(got)
        g_up, _ = _weight_grad(flat(h2), flat(du2), N_CHIPS, "b", tt, name=f"grad_ffn_up_{l}")
        g_dn, _ = _weight_grad(flat(hid2), flat(do2), 2, "a", tt, name=f"grad_ffn_down_{l}")
        (dx, du1, do1, rowd1, rowb1, rowc, dcw, dpw), got = _mixer_backward(
            dx, x0, o1, u1, ac1, dp1, mod_own[l], vec_d[l], vec_c[l], cw_pad[l], pw_b[l], win_g, wout_g, taps, tm,
            name=f"mixer_bwd_{l}", exchange=pend.chips() if pend else None)
        if pend:
            pend.after_chips(got)
        g_in, got = _weight_grad(flat(h1), flat(du1), N_CHIPS, "cols", tt, name=f"grad_w_in_{l}",
                                 exchange=pend.share() if pend else None)
        if pend:
            big[l + 1] = pend.result(got)
        g_out, _ = _weight_grad(flat(z1), flat(do1), 1, "cols", tt, name=f"grad_w_out_{l}")
        pend = _ReduceScatter([g_in, g_out.reshape(N_CHIPS, -1, d), g_up, g_dn.reshape(N_CHIPS, -1, d)], cidx, idx, l)
        small[l] = dict(rowd=rowd1 + rowd2, rowb=rowb1 + rowb2, rowc=rowc, dcw=dcw[:taps], dpw=dpw, dfw=dfw)
    assert nl >= 2
    pend.after_swap(_run_exchange(pend.swap(), name="rs_swap_0"))

    dmod_own = jnp.stack([small[l]["rowb"][:, :nmod, :] for l in range(nl)])
    dmod_all = _gather8(_rows128(dmod_own), name="gather_dmod").reshape(N_DEV, nl, nb, nmod * d)
    dmod_all = dmod_all.transpose(1, 0, 2, 3).reshape(nl, N_DEV * nb, nmod * d)
    dmod_cols = lax.dynamic_slice_in_dim(dmod_all, jm * acol, acol, axis=2)
    (g_ada_w, d_ada_w, nm_ada_w, nv_ada_w), r2_up = _ada_update(c_all, dmod_cols, ada_w, m_ada_w, v_ada_w, name="ada_update",
                                                                exchange=pend.chips([2]))

    def st(key, row=None):
        return jnp.stack([small[l][key] if row is None else small[l][key][row] for l in range(nl)])

    local = {
        "ada_b": dmod_own.sum(axis=1).reshape(nl, nmod * d),
        "pre_mix_g": st("rowd", 0), "post_mix_g": st("rowd", 1),
        "conv_b": st("rowc", 0), "conv_ln_g": st("rowc", 1), "conv_ln_b": st("rowc", 2),
        "pool_w": st("dpw"), "pool_scale": st("rowc", 3),
        "pre_ffn_g": st("rowd", 2), "post_ffn_g": st("rowd", 3),
        "ffn_conv_b": st("dfw", 3), "conv_w": st("dcw"), "ffn_conv_w": jnp.stack([small[l]["dfw"][:3] for l in range(nl)]),
    }
    names = list(local)
    sizes = [local[k].size for k in names]
    pad = -sum(sizes) % (2 * SUBLANES * LANES)
    packed = jnp.concatenate([local[k].reshape(-1) for k in names] + [jnp.zeros((pad,), F32)])
    summed = _allreduce8(_rows128(packed), name="allreduce_small").reshape(-1)
    grads, off = {}, 0
    for k, sz in zip(names, sizes):
        grads[k] = summed[off:off + sz].reshape(local[k].shape)
        off += sz
    grads["conv_w"] = lax.dynamic_slice_in_dim(grads["conv_w"], jm * ccol, ccol, axis=2)
    grads["ffn_conv_w"] = lax.dynamic_slice_in_dim(grads["ffn_conv_w"], jm * fcol, fcol, axis=2)

    params = dict(ada_b=(ada_b, m_ada_b, v_ada_b), pre_mix_g=(pre_mix_g, m_pre_mix_g, v_pre_mix_g),
                  post_mix_g=(post_mix_g, m_post_mix_g, v_post_mix_g), conv_b=(conv_b, m_conv_b, v_conv_b),
                  conv_ln_g=(conv_ln_g, m_conv_ln_g, v_conv_ln_g), conv_ln_b=(conv_ln_b, m_conv_ln_b, v_conv_ln_b),
                  pool_w=(pool_w, m_pool_w, v_pool_w), pool_scale=(pool_scale, m_pool_scale, v_pool_scale),
                  pre_ffn_g=(pre_ffn_g, m_pre_ffn_g, v_pre_ffn_g), post_ffn_g=(post_ffn_g, m_post_ffn_g, v_post_ffn_g),
                  ffn_conv_b=(ffn_conv_b, m_ffn_conv_b, v_ffn_conv_b), conv_w=(conv_w, m_conv_w, v_conv_w),
                  ffn_conv_w=(ffn_conv_w, m_ffn_conv_w, v_ffn_conv_w))
    pack = lambda i, g=None: _rows128(jnp.concatenate([(grads[k] if g else params[k][i]).reshape(-1) for k in names]))
    sd, sm, sv = _adamw_flat(pack(0), pack(0, True), pack(1), pack(2), name="adamw_small")
    outs = {}
    off = 0
    for k in names:
        shape, sz = params[k][0].shape, params[k][0].size
        outs[k] = (grads[k],) + tuple(a.reshape(-1)[off:off + sz].reshape(shape) for a in (sd, sm, sv))
        off += sz

    outs["ada_w"] = (g_ada_w, d_ada_w, nm_ada_w, nv_ada_w)
    large = [("w_in", w_in, m_w_in, v_w_in), ("w_out", w_out, m_w_out, v_w_out),
             ("ffn_up", ffn_up, m_ffn_up, v_ffn_up), ("ffn_down", ffn_down, m_ffn_down, v_ffn_down)]

    def upper(a, exchange=None):
        k, w, m, v = large[a]
        return _adamw_layers(w, m, v, [big[l][a] for l in range(1, nl)], 1, name=f"adamw_{k}_upper", exchange=exchange)

    hi = [None] * len(large)
    hi[2], r2_rest = upper(2, pend.chips([0, 1, 3]))
    pend.after_chips([r2_rest[0], r2_rest[1], r2_up[0], r2_rest[2]])
    hi[3], shared = upper(3, pend.share())
    big[0] = pend.result(shared)
    hi[0], _ = upper(0)
    hi[1], _ = upper(1)
    for a, (k, w, m, v) in enumerate(large):
        outs[k] = tuple(_adamw_layers(w, m, v, [big[0][a]], 0, name=f"adamw_{k}_0", filled=hi[a])[0])

    order = ["ada_w", "ada_b", "pre_mix_g", "post_mix_g", "w_in", "conv_w", "conv_b", "conv_ln_g", "conv_ln_b", "pool_w",
             "pool_scale", "w_out", "pre_ffn_g", "post_ffn_g", "ffn_up", "ffn_conv_w", "ffn_conv_b", "ffn_down"]
    return (loss, dx) + tuple(outs[k][i] for i in range(4) for k in order)
```

```python
import functools

import jax
import jax.numpy as jnp
from jax import lax
from jax.experimental import pallas as pl
from jax.experimental.pallas import tpu as pltpu

F32 = jnp.float32
BF16 = jnp.bfloat16
MESH = pl.DeviceIdType.MESH

EPS = 1e-6
POOL_WINDOWS = (2, 4, 8, 16)
ADAM_LR = 0.001
ADAM_B1 = 0.9
ADAM_B2 = 0.999
ADAM_EPS = 1e-08
ADAM_WD = 0.01
ADAM_STEP = 10

N_CHIPS = 4
N_DEV = 8
LANES = 128
SUBLANES = 8
HALO = 32
FHALO = 8
VMEM_LIMIT = 60 * 1024 * 1024


def _pos():
    return lax.axis_index("x"), lax.axis_index("y"), lax.axis_index("c")


def _flip(v, f):
    return 1 - v if f else v


def _full(shape):
    nd = len(shape)
    return pl.BlockSpec(shape, lambda *_: (0,) * nd)


_ANY = pl.BlockSpec(memory_space=pl.ANY)
_VMEM = pl.BlockSpec(memory_space=pltpu.VMEM)


def _sigmoid(v):
    return 1.0 / (1.0 + jnp.exp(-v))


def _colsum(v):
    return jnp.sum(v, axis=0, keepdims=True)


def _gather8(v, name):
    r, ccols = v.shape

    def body(v_ref, out_ref, send_sems, recv_sems, local_sem):
        x, y, c = _pos()
        me = 4 * x + 2 * y + c
        mine = pltpu.make_async_copy(v_ref, out_ref.at[me], local_sem)
        mine.start()
        peers = [(_flip(x, (k >> 2) & 1), _flip(y, (k >> 1) & 1), _flip(c, k & 1)) for k in range(1, N_DEV)]
        sends = []
        for k, peer in enumerate(peers):
            cp = pltpu.make_async_remote_copy(src_ref=v_ref, dst_ref=out_ref.at[me], send_sem=send_sems.at[k],
                                              recv_sem=recv_sems.at[k], device_id=peer, device_id_type=MESH)
            cp.start()
            sends.append(cp)
        for k, peer in enumerate(peers):
            pidx = 4 * peer[0] + 2 * peer[1] + peer[2]
            pltpu.make_async_remote_copy(src_ref=v_ref, dst_ref=out_ref.at[pidx], send_sem=send_sems.at[k],
                                         recv_sem=recv_sems.at[k], device_id=peer, device_id_type=MESH).wait_recv()
        for cp in sends:
            cp.wait_send()
        mine.wait()

    return pl.pallas_call(
        body, name=name, out_shape=jax.ShapeDtypeStruct((N_DEV, r, ccols), v.dtype),
        in_specs=[_VMEM], out_specs=_VMEM,
        scratch_shapes=[pltpu.SemaphoreType.DMA((N_DEV - 1,)), pltpu.SemaphoreType.DMA((N_DEV - 1,)),
                        pltpu.SemaphoreType.DMA(())],
        compiler_params=pltpu.CompilerParams(vmem_limit_bytes=VMEM_LIMIT),
    )(v)


def _rows_to_owners(v, name):
    _, r, ccols = v.shape

    def body(v_ref, out_ref, send_sems, recv_sems, local_sem):
        x, y, c = _pos()
        jm = 2 * x + y
        mine = pltpu.make_async_copy(v_ref.at[2 * jm + c], out_ref.at[jm], local_sem)
        mine.start()
        peers, pjs = _chip_peers(x, y, c)
        sends = []
        for k, peer in enumerate(peers):
            cp = pltpu.make_async_remote_copy(src_ref=v_ref.at[2 * pjs[k] + c], dst_ref=out_ref.at[jm],
                                              send_sem=send_sems.at[k], recv_sem=recv_sems.at[k],
                                              device_id=peer, device_id_type=MESH)
            cp.start()
            sends.append(cp)
        for k, peer in enumerate(peers):
            pltpu.make_async_remote_copy(src_ref=v_ref.at[0], dst_ref=out_ref.at[pjs[k]], send_sem=send_sems.at[k],
                                         recv_sem=recv_sems.at[k], device_id=peer, device_id_type=MESH).wait_recv()
        for cp in sends:
            cp.wait_send()
        mine.wait()

    return pl.pallas_call(
        body, name=name, out_shape=jax.ShapeDtypeStruct((N_CHIPS, r, ccols), v.dtype),
        in_specs=[_VMEM], out_specs=_VMEM,
        scratch_shapes=[pltpu.SemaphoreType.DMA((N_CHIPS - 1,)), pltpu.SemaphoreType.DMA((N_CHIPS - 1,)),
                        pltpu.SemaphoreType.DMA(())],
        compiler_params=pltpu.CompilerParams(vmem_limit_bytes=VMEM_LIMIT),
    )(v)


def _allreduce8(v, name):
    r, ccols = v.shape
    h = r // 2

    def body(v_ref, out_ref, whole, half, send_sems, recv_sems):
        x, y, c = _pos()
        sib = (x, y, 1 - c)
        mine = pl.ds(pl.multiple_of(c * h, SUBLANES), h)
        theirs = pl.ds(pl.multiple_of((1 - c) * h, SUBLANES), h)

        def exchange(src, dst, k, peer):
            cp = pltpu.make_async_remote_copy(src_ref=src, dst_ref=dst, send_sem=send_sems.at[k],
                                              recv_sem=recv_sems.at[k], device_id=peer, device_id_type=MESH)
            cp.start()
            cp.wait()

        exchange(v_ref, whole, 0, sib)
        out_ref[...] = v_ref[...] + whole[...]
        for k, peer in ((1, (1 - x, y, c)), (2, (x, 1 - y, c))):
            exchange(out_ref.at[mine], half.at[k - 1], k, peer)
            out_ref[mine, :] = out_ref[mine, :] + half[k - 1]
        exchange(out_ref.at[mine], half.at[2], 3, sib)
        out_ref[theirs, :] = half[2]

    return pl.pallas_call(
        body, name=name, out_shape=jax.ShapeDtypeStruct((r, ccols), v.dtype),
        in_specs=[_VMEM], out_specs=_VMEM,
        scratch_shapes=[pltpu.VMEM((r, ccols), v.dtype), pltpu.VMEM((3, h, ccols), v.dtype),
                        pltpu.SemaphoreType.DMA((4,)), pltpu.SemaphoreType.DMA((4,))],
        compiler_params=pltpu.CompilerParams(vmem_limit_bytes=VMEM_LIMIT),
    )(v)


def _chip_peers(x, y, c):
    peers = [(_flip(x, (k >> 1) & 1), _flip(y, k & 1), c) for k in range(1, N_CHIPS)]
    return peers, [2 * p[0] + p[1] for p in peers]


class _Exchange:
    def __init__(self, ins, outs, aliases, n_sems, n_local, start, finish):
        self.ins, self.outs, self.aliases = list(ins), list(outs), dict(aliases)
        self.n_sems, self.n_local, self.start, self.finish = n_sems, n_local, start, finish

    def scratch(self):
        return [pltpu.SemaphoreType.DMA((self.n_sems,)), pltpu.SemaphoreType.DMA((self.n_sems,)),
                pltpu.SemaphoreType.DMA((max(self.n_local, 1),))]


class _Sems:
    def __init__(self, send, recv, local, base=0, lbase=0):
        self.send, self.recv, self.loc, self.base, self.lbase = send, recv, local, base, lbase

    def shifted(self, by, lby):
        return _Sems(self.send, self.recv, self.loc, self.base + by, self.lbase + lby)

    def local(self, k):
        return self.loc.at[self.lbase + k]


def _remote(src, dst, sems, k, peer):
    return pltpu.make_async_remote_copy(src_ref=src, dst_ref=dst, send_sem=sems.send.at[sems.base + k],
                                        recv_sem=sems.recv.at[sems.base + k], device_id=peer, device_id_type=MESH)


def _combine(exs):
    ins = [a for ex in exs for a in ex.ins]
    outs = [o for ex in exs for o in ex.outs]
    aliases, spans, ni, no, ns, nloc = {}, [], 0, 0, 0, 0
    for ex in exs:
        aliases.update({ni + a: no + b for a, b in ex.aliases.items()})
        spans.append((ni, no, ns, nloc))
        ni, no, ns, nloc = ni + len(ex.ins), no + len(ex.outs), ns + ex.n_sems, nloc + ex.n_local

    def each(which):
        def run(ins_, outs_, sems):
            for ex, (i0, o0, s0, l0) in zip(exs, spans):
                getattr(ex, which)(ins_[i0:i0 + len(ex.ins)], outs_[o0:o0 + len(ex.outs)], sems.shifted(s0, l0))
        return run

    return _Exchange(ins, outs, aliases, ns, nloc, each("start"), each("finish"))


def _gather(shards):
    n = len(shards)
    per = N_CHIPS - 1

    def copies(ins, outs, sems):
        x, y, c = _pos()
        jm = 2 * x + y
        sib = (x, y, 1 - c)
        peers, pjs = _chip_peers(x, y, c)
        sends, recvs, passes, passed = [], [], [], []
        for a in range(n):
            hr = shards[a].shape[0] // 2
            rows, other = pl.ds(c * hr, hr), pl.ds((1 - c) * hr, hr)
            for k, peer in enumerate(peers):
                landed, theirs = outs[a].at[pjs[k], rows], outs[a].at[pjs[k], other]
                sends.append(_remote(ins[a].at[rows], outs[a].at[jm, rows], sems, 2 * (a * per + k), peer))
                recvs.append(_remote(landed, landed, sems, 2 * (a * per + k), peer))
                passes.append(_remote(landed, landed, sems, 2 * (a * per + k) + 1, sib))
                passed.append(_remote(theirs, theirs, sems, 2 * (a * per + k) + 1, sib))
        return sends, recvs, passes, passed

    def local(ins, outs, sems):
        x, y, _ = _pos()
        return [pltpu.make_async_copy(ins[a], outs[a].at[2 * x + y], sems.local(a)) for a in range(n)]

    def start(ins, outs, sems):
        for cp in local(ins, outs, sems) + copies(ins, outs, sems)[0]:
            cp.start()

    def finish(ins, outs, sems):
        sends, recvs, passes, passed = copies(ins, outs, sems)
        for got, fwd in zip(recvs, passes):
            got.wait_recv()
            fwd.start()
        for cp in passed:
            cp.wait_recv()
        for cp in sends + passes:
            cp.wait_send()
        for cp in local(ins, outs, sems):
            cp.wait()

    outs = [jax.ShapeDtypeStruct((N_CHIPS,) + s.shape, s.dtype) for s in shards]
    return _Exchange(shards, outs, {}, 2 * n * per, n, start, finish)


def _swap_halves(gs):
    n = len(gs)

    def copies(ins, outs, sems):
        x, y, c = _pos()
        sib = (x, y, 1 - c)
        cps = []
        for a in range(n):
            hr = gs[a].shape[1] // 2
            cps.append(_remote(ins[a].at[:, pl.ds((1 - c) * hr, hr), :], outs[a], sems, a, sib))
        return cps

    def start(ins, outs, sems):
        for cp in copies(ins, outs, sems):
            cp.start()

    def finish(ins, outs, sems):
        for cp in copies(ins, outs, sems):
            cp.wait()

    outs = [jax.ShapeDtypeStruct((g.shape[0], g.shape[1] // 2, g.shape[2]), g.dtype) for g in gs]
    return _Exchange(gs, outs, {}, n, 0, start, finish)


def _chip_exchange(ps):
    n = len(ps)
    per = N_CHIPS - 1

    def copies(ins, outs, sems):
        x, y, c = _pos()
        peers, pjs = _chip_peers(x, y, c)
        return [_remote(ins[a].at[pjs[k]], outs[a].at[k], sems, a * per + k, peer)
                for a in range(n) for k, peer in enumerate(peers)]

    def start(ins, outs, sems):
        for cp in copies(ins, outs, sems):
            cp.start()

    def finish(ins, outs, sems):
        for cp in copies(ins, outs, sems):
            cp.wait()

    outs = [jax.ShapeDtypeStruct((per,) + p.shape[1:], p.dtype) for p in ps]
    return _Exchange(ps, outs, {}, n * per, 0, start, finish)


def _sibling_share(fs):
    n = len(fs)

    def copies(outs, sems):
        x, y, c = _pos()
        sib = (x, y, 1 - c)
        sends = [_remote(outs[a].at[c], outs[a].at[c], sems, a, sib) for a in range(n)]
        recvs = [_remote(outs[a].at[1 - c], outs[a].at[1 - c], sems, a, sib) for a in range(n)]
        return sends, recvs

    def start(ins, outs, sems):
        for cp in copies(outs, sems)[0]:
            cp.start()

    def finish(ins, outs, sems):
        sends, recvs = copies(outs, sems)
        for cp in recvs:
            cp.wait_recv()
        for cp in sends:
            cp.wait_send()

    outs = [jax.ShapeDtypeStruct(f.shape, f.dtype) for f in fs]
    return _Exchange(fs, outs, {a: a for a in range(n)}, n, 0, start, finish)


def _run_exchange(ex, name):
    ni, no = len(ex.ins), len(ex.outs)

    def body(*refs):
        ins, outs, sems = refs[:ni], refs[ni:ni + no], _Sems(*refs[ni + no:])
        ex.start(ins, outs, sems)
        ex.finish(ins, outs, sems)

    return pl.pallas_call(
        body, name=name, out_shape=ex.outs, in_specs=[_ANY] * ni, out_specs=[_ANY] * no,
        input_output_aliases=ex.aliases, scratch_shapes=ex.scratch(),
    )(*ex.ins)


def _grid_call(body, name, grid, in_specs, out_specs, out_shape, scratch_shapes, args, exchange=None):
    ni, no = len(in_specs), len(out_specs)
    params = pltpu.CompilerParams(dimension_semantics=("arbitrary",) * len(grid), vmem_limit_bytes=VMEM_LIMIT)
    if exchange is None:
        outs = pl.pallas_call(body, name=name, grid=grid, in_specs=in_specs, out_specs=out_specs, out_shape=out_shape,
                              scratch_shapes=scratch_shapes, compiler_params=params)(*args)
        return list(outs), []
    ex = exchange
    nci, nco = len(ex.ins), len(ex.outs)

    def hosted(*refs):
        cin = refs[ni:ni + nci]
        cout = refs[ni + nci + no:ni + nci + no + nco]
        sems = _Sems(*refs[len(refs) - 3:])
        main = refs[:ni] + refs[ni + nci:ni + nci + no] + refs[ni + nci + no + nco:len(refs) - 3]
        ids = [pl.program_id(a) for a in range(len(grid))]
        first = functools.reduce(lambda p, q: p & q, [i == 0 for i in ids])
        last = functools.reduce(lambda p, q: p & q, [i == g - 1 for i, g in zip(ids, grid)])

        @pl.when(first)
        def _():
            ex.start(cin, cout, sems)

        body(*main)

        @pl.when(last)
        def _():
            ex.finish(cin, cout, sems)

    outs = pl.pallas_call(
        hosted, name=name, grid=grid, in_specs=list(in_specs) + [_ANY] * nci, out_specs=list(out_specs) + [_ANY] * nco,
        out_shape=list(out_shape) + ex.outs, scratch_shapes=list(scratch_shapes) + ex.scratch(),
        input_output_aliases={ni + a: no + b for a, b in ex.aliases.items()}, compiler_params=params,
    )(*args, *ex.ins)
    return list(outs[:no]), list(outs[no:])


def _row_tile(rows, cols, itemsize, budget=2 * 1024 * 1024):
    best = None
    for t in range(16, rows + 1, 16):
        if rows % t == 0 and t * cols * itemsize <= budget:
            best = t
    return best if best is not None else rows


def _pair_sum(g, r1, cidx, name):
    nj, r, ccols = g.shape
    hr = r // 2
    tr = _row_tile(hr, ccols, 4)
    nt = hr // tr

    def body(c_ref, g_ref, r_ref, o_ref):
        o_ref[...] = (g_ref[...].astype(F32) + r_ref[...].astype(F32)).astype(o_ref.dtype)

    return pl.pallas_call(
        body, name=name, out_shape=jax.ShapeDtypeStruct((nj, hr, ccols), g.dtype),
        grid_spec=pltpu.PrefetchScalarGridSpec(
            num_scalar_prefetch=1, grid=(nj, nt),
            in_specs=[pl.BlockSpec((None, tr, ccols), lambda j, i, c_ref: (j, c_ref[0] * nt + i, 0)),
                      pl.BlockSpec((None, tr, ccols), lambda j, i, c_ref: (j, i, 0))],
            out_specs=pl.BlockSpec((None, tr, ccols), lambda j, i, c_ref: (j, i, 0))),
        compiler_params=pltpu.CompilerParams(dimension_semantics=("arbitrary", "arbitrary")),
    )(cidx, g, r1)


def _chip_sum(p, r2, idx, name):
    nj, hr, ccols = p.shape
    tr = _row_tile(hr, ccols, 4)
    nt = hr // tr

    def body(i_ref, p_ref, r_ref, o_ref):
        s = p_ref[...].astype(F32)
        for k in range(N_CHIPS - 1):
            s = s + r_ref[k].astype(F32)
        o_ref[...] = s

    return pl.pallas_call(
        body, name=name, out_shape=jax.ShapeDtypeStruct((2, hr, ccols), F32),
        grid_spec=pltpu.PrefetchScalarGridSpec(
            num_scalar_prefetch=1, grid=(nt,),
            in_specs=[pl.BlockSpec((None, tr, ccols), lambda i, i_ref: (i_ref[0], i, 0)),
                      pl.BlockSpec((N_CHIPS - 1, tr, ccols), lambda i, i_ref: (0, i, 0))],
            out_specs=pl.BlockSpec((None, tr, ccols), lambda i, i_ref: (i_ref[1], i, 0))),
        compiler_params=pltpu.CompilerParams(dimension_semantics=("arbitrary",)),
    )(idx, p, r2)


def _adam_math(w, g, m, v):
    m2 = ADAM_B1 * m + (1.0 - ADAM_B1) * g
    v2 = ADAM_B2 * v + (1.0 - ADAM_B2) * (g * g)
    m_hat = m2 / (1.0 - ADAM_B1 ** ADAM_STEP)
    v_hat = v2 / (1.0 - ADAM_B2 ** ADAM_STEP)
    delta = -ADAM_LR * (m_hat / (jnp.sqrt(v_hat) + ADAM_EPS) + ADAM_WD * w)
    return delta, m2, v2


def _adamw_layers(w, m, v, gs, name):
    nl, r, ccols = w.shape
    ng = len(gs)
    tr = _row_tile(r, ccols, 4, budget=1024 * 1024)
    nt = r // tr

    def body(w_ref, m_ref, v_ref, *rest):
        g_refs, (go_ref, d_ref, mo_ref, vo_ref) = rest[:ng], rest[ng:]
        l = pl.program_id(0)
        g = g_refs[0][...]
        for k in range(1, ng):
            g = jnp.where(l == k, g_refs[k][...], g)
        delta, m2, v2 = _adam_math(w_ref[...], g, m_ref[...], v_ref[...])
        go_ref[...] = g
        d_ref[...] = delta
        mo_ref[...] = m2
        vo_ref[...] = v2

    big = pl.BlockSpec((None, tr, ccols), lambda l, i: (l, i, 0))

    def gspec(k):
        return pl.BlockSpec((tr, ccols), lambda l, i: (jnp.where(l == k, i, jnp.where(l < k, 0, nt - 1)), 0))

    assert ng == nl
    return _grid_call(body, name, (nl, nt), in_specs=[big, big, big] + [gspec(k) for k in range(ng)],
                      out_specs=[big, big, big, big], out_shape=[jax.ShapeDtypeStruct(w.shape, F32)] * 4,
                      scratch_shapes=[], args=(w, m, v, *gs))[0]


def _adamw_flat(w, g, m, v, name):
    r, ccols = w.shape

    def body(w_ref, g_ref, m_ref, v_ref, d_ref, mo_ref, vo_ref):
        delta, m2, v2 = _adam_math(w_ref[...], g_ref[...], m_ref[...], v_ref[...])
        d_ref[...] = delta
        mo_ref[...] = m2
        vo_ref[...] = v2

    return pl.pallas_call(
        body, name=name, out_shape=[jax.ShapeDtypeStruct((r, ccols), F32)] * 3,
        in_specs=[_VMEM] * 4, out_specs=[_VMEM] * 3,
        compiler_params=pltpu.CompilerParams(vmem_limit_bytes=VMEM_LIMIT),
    )(w, g, m, v)


def _ada_forward(c_all, ada_w, ada_b_cols, name):
    nl, d, ncols = ada_w.shape
    bg = c_all.shape[0]
    tn = 512 if ncols % 512 == 0 else ncols

    def body(c_ref, w_ref, b_ref, o_ref):
        cv = c_ref[...]
        ca = (cv * _sigmoid(cv)).astype(BF16)
        o_ref[...] = jnp.dot(ca, w_ref[...].astype(BF16), preferred_element_type=F32) + b_ref[...]

    return pl.pallas_call(
        body, name=name, out_shape=jax.ShapeDtypeStruct((nl, bg, ncols), F32),
        grid=(nl, ncols // tn),
        in_specs=[pl.BlockSpec((bg, d), lambda l, j: (0, 0)),
                  pl.BlockSpec((None, d, tn), lambda l, j: (l, 0, j)),
                  pl.BlockSpec((None, 1, tn), lambda l, j: (l, 0, j))],
        out_specs=pl.BlockSpec((None, bg, tn), lambda l, j: (l, 0, j)),
        compiler_params=pltpu.CompilerParams(dimension_semantics=("arbitrary", "arbitrary")),
    )(c_all, ada_w, ada_b_cols)


def _ada_update(c_all, dmod_cols, w, m, v, name, exchange=None):
    nl, d, ncols = w.shape
    bg = c_all.shape[0]
    tn = 512 if ncols % 512 == 0 else ncols

    def body(c_ref, dm_ref, w_ref, m_ref, v_ref, go_ref, d_ref, mo_ref, vo_ref):
        cv = c_ref[...]
        ca = (cv * _sigmoid(cv)).astype(BF16)
        g = lax.dot_general(ca, dm_ref[...].astype(BF16), (((0,), (0,)), ((), ())), preferred_element_type=F32)
        delta, m2, v2 = _adam_math(w_ref[...], g, m_ref[...], v_ref[...])
        go_ref[...] = g
        d_ref[...] = delta
        mo_ref[...] = m2
        vo_ref[...] = v2

    big = pl.BlockSpec((None, d, tn), lambda l, j: (l, 0, j))
    return _grid_call(
        body, name, (nl, ncols // tn),
        in_specs=[pl.BlockSpec((bg, d), lambda l, j: (0, 0)),
                  pl.BlockSpec((None, bg, tn), lambda l, j: (l, 0, j)), big, big, big],
        out_specs=[big, big, big, big], out_shape=[jax.ShapeDtypeStruct(w.shape, F32)] * 4,
        scratch_shapes=[], args=(c_all, dmod_cols, w, m, v), exchange=exchange)


def _load_weights(first, pairs, sems):
    @pl.when(first)
    def _():
        cps = [pltpu.make_async_copy(src, dst, sems.at[k]) for k, (src, dst) in enumerate(pairs)]
        for cp in cps:
            cp.start()
        for cp in cps:
            cp.wait()


def _ada_norm(xv, g, sc, sh):
    r = lax.rsqrt(jnp.mean(xv * xv, axis=-1, keepdims=True) + EPS)
    xn = xv * r
    return (xn * g) * (1.0 + sc) + sh, xn, r


def _ada_norm_bwd(dh, xn, r, g, sc):
    d_sh = _colsum(dh)
    d_sc = _colsum(dh * (xn * g))
    dxg = dh * (1.0 + sc)
    d_g = _colsum(dxg * xn)
    gd = dxg * g
    dx = r * (gd - xn * jnp.mean(gd * xn, axis=-1, keepdims=True))
    return dx, d_sh, d_sc, d_g


def _gated_residual_bwd(dxo, o, g_post, gt):
    r = lax.rsqrt(jnp.mean(o * o, axis=-1, keepdims=True) + EPS)
    on = o * r
    d_gt = _colsum(dxo * (on * g_post))
    dy = dxo * (1.0 + gt)
    d_gp = _colsum(dy * on)
    gd = dy * g_post
    do = r * (gd - on * jnp.mean(gd * on, axis=-1, keepdims=True))
    return do, d_gt, d_gp


def _seq_positions(i, tm, width):
    return i * tm + lax.broadcasted_iota(jnp.int32, (tm, width), 0)


def _fill_phases(ext, phases):
    rows = ext.shape[0]
    ev = ext[...]
    for r in range(1, SUBLANES):
        phases[r - 1] = pltpu.roll(ev, rows - r, axis=0)


def _shifted_rows(ext, phases, offset, n):
    q, r = divmod(offset, SUBLANES)
    if r == 0:
        return ext[pl.ds(q * SUBLANES, n), :]
    return phases[r - 1, pl.ds(q * SUBLANES, n), :]


def _rows_before(halo, cur, shift):
    e = jnp.concatenate([halo, cur], axis=0)
    return pltpu.roll(e, shift, axis=0)[halo.shape[0]:, :]


def _rows_after(cur, halo, shift):
    e = jnp.concatenate([cur, halo], axis=0)
    return pltpu.roll(e, e.shape[0] - shift, axis=0)[:cur.shape[0], :]


def _mixer_forward(x, mod, vec_d, vec_c, cw, pw, win_g, wout_g, taps, tm, name, exchange=None):
    nb, s, d = x.shape
    n = s // tm
    nj, _, dcol = win_g.shape
    din = nj * dcol
    dc = vec_c.shape[-1]
    dpool = din - 2 * dc
    dmix = dc + dpool
    ro = wout_g.shape[1]
    ngrp = dpool // LANES

    def body(x_ref, mod_ref, vd_ref, vc_ref, cw_ref, pw_ref, win_hbm, wout_hbm,
             xo_ref, h_ref, u_ref, ac_ref, dp_ref, z_ref, o_ref,
             win_v, wout_v, ext_a, ext_p, phases, sems):
        b, i = pl.program_id(0), pl.program_id(1)
        pairs = [(win_hbm.at[j], win_v.at[:, pl.ds(j * dcol, dcol)]) for j in range(nj)]
        pairs += [(wout_hbm.at[j], wout_v.at[pl.ds(j * ro, ro), :]) for j in range(nj)]
        _load_weights((b == 0) & (i == 0), pairs, sems)

        xv = x_ref[...]
        h, _, _ = _ada_norm(xv, vd_ref[0:1, :], mod_ref[1:2, :], mod_ref[0:1, :])
        hb = h.astype(BF16)
        h_ref[...] = hb
        u = jnp.dot(hb, win_v[...], preferred_element_type=F32)
        u_ref[...] = u.astype(BF16)
        ag = u[:, :dc] * _sigmoid(u[:, dc:2 * dc])
        up = u[:, 2 * dc:]

        @pl.when(i == 0)
        def _():
            ext_a[0:HALO, :] = jnp.zeros((HALO, dc), F32)
            ext_p[0:HALO, :] = jnp.zeros((HALO, dpool), F32)

        @pl.when(i > 0)
        def _():
            ext_a[0:HALO, :] = ext_a[tm:tm + HALO, :]
            ext_p[0:HALO, :] = ext_p[tm:tm + HALO, :]

        ext_a[HALO:HALO + tm, :] = ag
        ext_p[HALO:HALO + tm, :] = up

        acc = jnp.broadcast_to(vc_ref[0:1, :], (tm, dc))
        _fill_phases(ext_a, phases)
        for k in range(taps):
            acc = acc + cw_ref[k:k + 1, :] * _shifted_rows(ext_a, phases, HALO - (taps - 1) + k, tm)
        ac_ref[...] = acc.astype(BF16)
        mu = jnp.mean(acc, axis=-1, keepdims=True)
        xc = acc - mu
        var = jnp.mean(xc * xc, axis=-1, keepdims=True)
        al = (xc * lax.rsqrt(var + EPS)) * vc_ref[1:2, :] + vc_ref[2:3, :]
        a = al * _sigmoid(al)

        pos = _seq_positions(i, tm, LANES)
        parts = [a.astype(BF16)]
        for g in range(ngrp):
            w = POOL_WINDOWS[g]
            cols = slice(g * LANES, (g + 1) * LANES)
            sw = ext_p[:, cols]
            step = 1
            while step < w:
                sw = sw + pltpu.roll(sw, step, axis=0)
                step *= 2
            cnt = jnp.minimum(pos + 1, w).astype(F32)
            dg = (sw[HALO:, :] / cnt - up[:, cols]).astype(BF16)
            dp_ref[:, cols] = dg
            q = jnp.dot(dg, pw_ref[g], preferred_element_type=F32)
            parts.append((q * vc_ref[3:4, cols]).astype(BF16))
        z = jnp.concatenate(parts, axis=-1)
        z_ref[...] = z
        o = jnp.dot(z, wout_v[...], preferred_element_type=F32)
        o_ref[...] = o
        r2 = lax.rsqrt(jnp.mean(o * o, axis=-1, keepdims=True) + EPS)
        xo_ref[...] = xv + (1.0 + mod_ref[2:3, :]) * ((o * r2) * vd_ref[1:2, :])

    def tile(width):
        return pl.BlockSpec((None, tm, width), lambda b, i: (b, i, 0))

    return _grid_call(
        body, name, (nb, n),
        in_specs=[tile(d), pl.BlockSpec((None, 8, d), lambda b, i: (b, 0, 0)), _full(vec_d.shape), _full(vec_c.shape),
                  _full(cw.shape), _full(pw.shape), _ANY, _ANY],
        out_specs=[tile(d), tile(d), tile(din), tile(dc), tile(dpool), tile(dmix), tile(d)],
        out_shape=[jax.ShapeDtypeStruct((nb, s, d), F32), jax.ShapeDtypeStruct((nb, s, d), BF16),
                   jax.ShapeDtypeStruct((nb, s, din), BF16), jax.ShapeDtypeStruct((nb, s, dc), BF16),
                   jax.ShapeDtypeStruct((nb, s, dpool), BF16), jax.ShapeDtypeStruct((nb, s, dmix), BF16),
                   jax.ShapeDtypeStruct((nb, s, d), F32)],
        scratch_shapes=[pltpu.VMEM((d, din), BF16), pltpu.VMEM((dmix, d), BF16),
                        pltpu.VMEM((HALO + tm, dc), F32), pltpu.VMEM((HALO + tm, dpool), F32),
                        pltpu.VMEM((SUBLANES - 1, HALO + tm, dc), F32), pltpu.SemaphoreType.DMA((2 * nj,))],
        args=(x, mod, vec_d, vec_c, cw, pw, win_g, wout_g), exchange=exchange)


def _mixer_backward(dxo, x, o, u, ac, dpl, mod, vec_d, vec_c, cw, pw, win_g, wout_g, taps, tm, name, exchange=None):
    nb, s, d = x.shape
    n = s // tm
    nj, _, dcol = win_g.shape
    din = nj * dcol
    dc = vec_c.shape[-1]
    dpool = din - 2 * dc
    dmix = dc + dpool
    ro = wout_g.shape[1]
    ngrp = dpool // LANES
    rext = tm + HALO

    def body(dxo_ref, x_ref, o_ref, u_ref, ac_ref, dp_ref, mod_ref, vd_ref, vc_ref, cw_ref, pw_ref, win_hbm, wout_hbm,
             dx_ref, du_ref, dob_ref, rowd_ref, rowb_ref, rowc_ref, dcw_ref, dpw_ref,
             win_v, wout_v, ext_a, ext_p, phases, sems):
        b, i = pl.program_id(0), pl.program_id(1)
        first = (b == 0) & (i == 0)
        pairs = [(win_hbm.at[j], win_v.at[:, pl.ds(j * dcol, dcol)]) for j in range(nj)]
        pairs += [(wout_hbm.at[j], wout_v.at[pl.ds(j * ro, ro), :]) for j in range(nj)]
        _load_weights(first, pairs, sems)

        @pl.when(first)
        def _():
            rowd_ref[...] = jnp.zeros_like(rowd_ref)
            rowc_ref[...] = jnp.zeros_like(rowc_ref)
            dcw_ref[...] = jnp.zeros_like(dcw_ref)
            dpw_ref[...] = jnp.zeros_like(dpw_ref)

        @pl.when(i == 0)
        def _():
            rowb_ref[...] = jnp.zeros_like(rowb_ref)
            ext_a[tm:rext, :] = jnp.zeros((HALO, dc), F32)
            ext_p[tm:rext, :] = jnp.zeros((HALO, dpool), F32)

        @pl.when(i > 0)
        def _():
            ext_a[tm:rext, :] = ext_a[0:HALO, :]
            ext_p[tm:rext, :] = ext_p[0:HALO, :]

        g_pre, g_post = vd_ref[0:1, :], vd_ref[1:2, :]
        sh, sc, gt = mod_ref[0:1, :], mod_ref[1:2, :], mod_ref[2:3, :]
        do, d_gt, d_gp = _gated_residual_bwd(dxo_ref[...], o_ref[...], g_post, gt)
        dob = do.astype(BF16)
        dob_ref[...] = dob
        dz = lax.dot_general(dob, wout_v[...], (((1,), (1,)), ((), ())), preferred_element_type=F32)

        acv = ac_ref[...].astype(F32)
        mu = jnp.mean(acv, axis=-1, keepdims=True)
        xc = acv - mu
        rstd = lax.rsqrt(jnp.mean(xc * xc, axis=-1, keepdims=True) + EPS)
        an = xc * rstd
        lg = vc_ref[1:2, :]
        al = an * lg + vc_ref[2:3, :]
        sg = _sigmoid(al)
        dal = dz[:, :dc] * (sg * (1.0 + al * (1.0 - sg)))
        d_lg = _colsum(dal * an)
        d_lb = _colsum(dal)
        dan = dal * lg
        dac = rstd * (dan - jnp.mean(dan, axis=-1, keepdims=True) - an * jnp.mean(dan * an, axis=-1, keepdims=True))
        d_cb = _colsum(dac)
        ext_a[0:tm, :] = dac
        uv = u_ref[:, 0:dc].astype(F32)
        sgu = _sigmoid(u_ref[:, dc:2 * dc].astype(F32))
        ag = uv * sgu
        dag = jnp.zeros((tm, dc), F32)
        _fill_phases(ext_a, phases)
        for k in range(taps):
            sl = _shifted_rows(ext_a, phases, taps - 1 - k, tm)
            dag = dag + cw_ref[k:k + 1, :] * sl
            dcw_ref[k:k + 1, :] += _colsum(ag * sl)
        du_ref[:, 0:dc] = (dag * sgu).astype(BF16)
        du_ref[:, dc:2 * dc] = (dag * uv * (sgu * (1.0 - sgu))).astype(BF16)

        pos = _seq_positions(n - 1 - i, tm, LANES)
        d_ps = []
        for g in range(ngrp):
            w = POOL_WINDOWS[g]
            cols = slice(g * LANES, (g + 1) * LANES)
            gcols = slice(dc + g * LANES, dc + (g + 1) * LANES)
            dgb = dp_ref[:, cols]
            q = jnp.dot(dgb, pw_ref[g], preferred_element_type=F32)
            dpg = dz[:, gcols]
            d_ps.append(_colsum(dpg * q))
            dq = (dpg * vc_ref[3:4, cols]).astype(BF16)
            dpw_ref[g] += lax.dot_general(dgb, dq, (((0,), (0,)), ((), ())), preferred_element_type=F32)
            dd = lax.dot_general(dq, pw_ref[g], (((1,), (1,)), ((), ())), preferred_element_type=F32)
            cnt = jnp.minimum(pos + 1, w).astype(F32)
            ext_p[0:tm, cols] = dd / cnt
            sw = ext_p[:, cols]
            step = 1
            while step < w:
                sw = sw + pltpu.roll(sw, rext - step, axis=0)
                step *= 2
            du_ref[:, 2 * dc + g * LANES:2 * dc + (g + 1) * LANES] = (sw[0:tm, :] - dd).astype(BF16)
        rowc_ref[0:1, :] += d_cb
        rowc_ref[1:2, :] += d_lg
        rowc_ref[2:3, :] += d_lb
        rowc_ref[3:4, :] += jnp.concatenate(d_ps, axis=-1)

        dh = lax.dot_general(du_ref[...], win_v[...], (((1,), (1,)), ((), ())), preferred_element_type=F32)
        _, xn, r1 = _ada_norm(x_ref[...], g_pre, sc, sh)
        dxb, d_sh, d_sc, d_g = _ada_norm_bwd(dh, xn, r1, g_pre, sc)
        dx_ref[...] = dxo_ref[...] + dxb
        rowd_ref[0:1, :] += d_g
        rowd_ref[1:2, :] += d_gp
        rowb_ref[0:1, :] += d_sh
        rowb_ref[1:2, :] += d_sc
        rowb_ref[2:3, :] += d_gt

    def tile(width):
        return pl.BlockSpec((None, tm, width), lambda b, i: (b, n - 1 - i, 0))

    return _grid_call(
        body, name, (nb, n),
        in_specs=[tile(d), tile(d), tile(d), tile(din), tile(dc), tile(dpool),
                  pl.BlockSpec((None, 8, d), lambda b, i: (b, 0, 0)), _full(vec_d.shape), _full(vec_c.shape),
                  _full(cw.shape), _full(pw.shape), _ANY, _ANY],
        out_specs=[tile(d), tile(din), tile(d), _full((8, d)), pl.BlockSpec((None, 8, d), lambda b, i: (b, 0, 0)),
                   _full((8, dc)), _full(cw.shape), _full(pw.shape)],
        out_shape=[jax.ShapeDtypeStruct((nb, s, d), F32), jax.ShapeDtypeStruct((nb, s, din), BF16),
                   jax.ShapeDtypeStruct((nb, s, d), BF16), jax.ShapeDtypeStruct((8, d), F32),
                   jax.ShapeDtypeStruct((nb, 8, d), F32), jax.ShapeDtypeStruct((8, dc), F32),
                   jax.ShapeDtypeStruct(cw.shape, F32), jax.ShapeDtypeStruct(pw.shape, F32)],
        scratch_shapes=[pltpu.VMEM((d, din), BF16), pltpu.VMEM((dmix, d), BF16),
                        pltpu.VMEM((rext, dc), F32), pltpu.VMEM((rext, dpool), F32),
                        pltpu.VMEM((SUBLANES - 1, rext, dc), F32), pltpu.SemaphoreType.DMA((2 * nj,))],
        args=(dxo, x, o, u, ac, dpl, mod, vec_d, vec_c, cw, pw, win_g, wout_g), exchange=exchange)


def _ffn_forward(x, mod, vec_d, fw, wup_g, wdn_g, tm, name, exchange=None):
    nb, s, d = x.shape
    n = s // tm
    nj, _, ucol = wup_g.shape
    f2 = nj * ucol
    dff = f2 // 2
    rd = wdn_g.shape[1]
    nq = nj // 2
    cs = dff // nq

    def body(x_ref, mod_ref, vd_ref, fw_ref, wup_hbm, wdn_hbm,
             xo_ref, h_ref, u_ref, hid_ref, o_ref,
             wup_v, wdn_v, prev_u, sems):
        b, i = pl.program_id(0), pl.program_id(1)
        pairs = [(wup_hbm.at[j], wup_v.at[:, pl.ds(j * ucol, ucol)]) for j in range(nj)]
        pairs += [(wdn_hbm.at[j], wdn_v.at[pl.ds(j * rd, rd), :]) for j in range(nj)]
        _load_weights((b == 0) & (i == 0), pairs, sems)

        @pl.when(i == 0)
        def _():
            prev_u[...] = jnp.zeros_like(prev_u)

        xv = x_ref[...]
        h, _, _ = _ada_norm(xv, vd_ref[2:3, :], mod_ref[4:5, :], mod_ref[3:4, :])
        hb = h.astype(BF16)
        h_ref[...] = hb

        def conv(cols):
            uc = jnp.dot(hb, wup_v[:, cols], preferred_element_type=F32)
            u_ref[:, cols] = uc.astype(BF16)
            before = prev_u[:, cols]
            prev_u[:, cols] = uc[tm - FHALO:, :]
            return (fw_ref[3:4, cols] + fw_ref[2:3, cols] * uc + fw_ref[1:2, cols] * _rows_before(before, uc, 1)
                    + fw_ref[0:1, cols] * _rows_before(before, uc, 2))

        o = jnp.zeros((tm, d), F32)
        for q in range(nq):
            val = conv(pl.ds(q * cs, cs))
            gate = conv(pl.ds(dff + q * cs, cs))
            hid = ((gate * _sigmoid(gate)) * val).astype(BF16)
            hid_ref[:, pl.ds(q * cs, cs)] = hid
            o = o + jnp.dot(hid, wdn_v[pl.ds(q * cs, cs), :], preferred_element_type=F32)
        o_ref[...] = o
        r2 = lax.rsqrt(jnp.mean(o * o, axis=-1, keepdims=True) + EPS)
        xo_ref[...] = xv + (1.0 + mod_ref[5:6, :]) * ((o * r2) * vd_ref[3:4, :])

    def tile(width):
        return pl.BlockSpec((None, tm, width), lambda b, i: (b, i, 0))

    return _grid_call(
        body, name, (nb, n),
        in_specs=[tile(d), pl.BlockSpec((None, 8, d), lambda b, i: (b, 0, 0)), _full(vec_d.shape), _full(fw.shape),
                  _ANY, _ANY],
        out_specs=[tile(d), tile(d), tile(f2), tile(dff), tile(d)],
        out_shape=[jax.ShapeDtypeStruct((nb, s, d), F32), jax.ShapeDtypeStruct((nb, s, d), BF16),
                   jax.ShapeDtypeStruct((nb, s, f2), BF16), jax.ShapeDtypeStruct((nb, s, dff), BF16),
                   jax.ShapeDtypeStruct((nb, s, d), F32)],
        scratch_shapes=[pltpu.VMEM((d, f2), BF16), pltpu.VMEM((dff, d), BF16),
                        pltpu.VMEM((FHALO, f2), F32), pltpu.SemaphoreType.DMA((2 * nj,))],
        args=(x, mod, vec_d, fw, wup_g, wdn_g), exchange=exchange)


def _ffn_backward(dxo, x, o, u, mod, vec_d, fw, wup_g, wdn_g, tm, name, exchange=None):
    nb, s, d = x.shape
    n = s // tm
    nj, _, ucol = wup_g.shape
    f2 = nj * ucol
    dff = f2 // 2
    rd = wdn_g.shape[1]
    nq = nj // 2
    cs = dff // nq
    hb_per_tile = tm // FHALO

    def body(dxo_ref, x_ref, o_ref, u_ref, uh_ref, mod_ref, vd_ref, fw_ref, wup_hbm, wdn_hbm,
             dx_ref, du_ref, dob_ref, rowd_ref, rowb_ref, dfw_ref,
             wup_v, wdn_v, next_d, sems):
        b, i = pl.program_id(0), pl.program_id(1)
        first = (b == 0) & (i == 0)
        pairs = [(wup_hbm.at[j], wup_v.at[:, pl.ds(j * ucol, ucol)]) for j in range(nj)]
        pairs += [(wdn_hbm.at[j], wdn_v.at[pl.ds(j * rd, rd), :]) for j in range(nj)]
        _load_weights(first, pairs, sems)

        @pl.when(first)
        def _():
            rowd_ref[...] = jnp.zeros_like(rowd_ref)
            dfw_ref[...] = jnp.zeros_like(dfw_ref)

        @pl.when(i == 0)
        def _():
            rowb_ref[...] = jnp.zeros_like(rowb_ref)
            next_d[...] = jnp.zeros_like(next_d)

        g_pre, g_post = vd_ref[2:3, :], vd_ref[3:4, :]
        sh, sc, gt = mod_ref[3:4, :], mod_ref[4:5, :], mod_ref[5:6, :]
        do, d_gt, d_gp = _gated_residual_bwd(dxo_ref[...], o_ref[...], g_post, gt)
        dob = do.astype(BF16)
        dob_ref[...] = dob

        keep = jnp.where(i == n - 1, 0.0, 1.0)

        def conv(cols):
            uc = u_ref[:, cols].astype(F32)
            before = uh_ref[:, cols].astype(F32) * keep
            return (fw_ref[3:4, cols] + fw_ref[2:3, cols] * uc + fw_ref[1:2, cols] * _rows_before(before, uc, 1)
                    + fw_ref[0:1, cols] * _rows_before(before, uc, 2))

        def conv_bwd(cols, duc):
            uc = u_ref[:, cols].astype(F32)
            after = next_d[:, cols]
            next_d[:, cols] = duc[0:FHALO, :]
            d1 = _rows_after(duc, after, 1)
            d2 = _rows_after(duc, after, 2)
            dfw_ref[3:4, cols] += _colsum(duc)
            dfw_ref[2:3, cols] += _colsum(uc * duc)
            dfw_ref[1:2, cols] += _colsum(uc * d1)
            dfw_ref[0:1, cols] += _colsum(uc * d2)
            ob = (fw_ref[2:3, cols] * duc + fw_ref[1:2, cols] * d1 + fw_ref[0:1, cols] * d2).astype(BF16)
            du_ref[:, cols] = ob
            return lax.dot_general(ob, wup_v[:, cols], (((1,), (1,)), ((), ())), preferred_element_type=F32)

        dh = jnp.zeros((tm, d), F32)
        for q in range(nq):
            vcols = pl.ds(q * cs, cs)
            gcols = pl.ds(dff + q * cs, cs)
            dhid = lax.dot_general(dob, wdn_v[vcols, :], (((1,), (1,)), ((), ())), preferred_element_type=F32)
            val = conv(vcols)
            gate = conv(gcols)
            sg = _sigmoid(gate)
            dval = dhid * (gate * sg)
            dgate = dhid * val * (sg * (1.0 + gate * (1.0 - sg)))
            dh = dh + conv_bwd(vcols, dval)
            dh = dh + conv_bwd(gcols, dgate)

        _, xn, r1 = _ada_norm(x_ref[...], g_pre, sc, sh)
        dxb, d_sh, d_sc, d_g = _ada_norm_bwd(dh, xn, r1, g_pre, sc)
        dx_ref[...] = dxo_ref[...] + dxb
        rowd_ref[2:3, :] += d_g
        rowd_ref[3:4, :] += d_gp
        rowb_ref[3:4, :] += d_sh
        rowb_ref[4:5, :] += d_sc
        rowb_ref[5:6, :] += d_gt

    def tile(width):
        return pl.BlockSpec((None, tm, width), lambda b, i: (b, n - 1 - i, 0))

    halo = pl.BlockSpec((None, FHALO, f2), lambda b, i: (b, jnp.maximum((n - 1 - i) * hb_per_tile - 1, 0), 0))
    return _grid_call(
        body, name, (nb, n),
        in_specs=[tile(d), tile(d), tile(d), tile(f2), halo, pl.BlockSpec((None, 8, d), lambda b, i: (b, 0, 0)),
                  _full(vec_d.shape), _full(fw.shape), _ANY, _ANY],
        out_specs=[tile(d), tile(f2), tile(d), _full((8, d)), pl.BlockSpec((None, 8, d), lambda b, i: (b, 0, 0)),
                   _full(fw.shape)],
        out_shape=[jax.ShapeDtypeStruct((nb, s, d), F32), jax.ShapeDtypeStruct((nb, s, f2), BF16),
                   jax.ShapeDtypeStruct((nb, s, d), BF16), jax.ShapeDtypeStruct((8, d), F32),
                   jax.ShapeDtypeStruct((nb, 8, d), F32), jax.ShapeDtypeStruct(fw.shape, F32)],
        scratch_shapes=[pltpu.VMEM((d, f2), BF16), pltpu.VMEM((dff, d), BF16),
                        pltpu.VMEM((FHALO, f2), F32), pltpu.SemaphoreType.DMA((2 * nj,))],
        args=(dxo, x, o, u, u, mod, vec_d, fw, wup_g, wdn_g), exchange=exchange)


def _weight_grad(a, b, nblk, split, tt, name, exchange=None):
    t, ka = a.shape
    nb_ = b.shape[1]
    nk = t // tt
    if split == "cols":
        wa, wb, grid = ka, nb_ // nblk, (1, nk)
        a_spec = pl.BlockSpec((tt, ka), lambda j, k: (k, 0))
        b_spec = pl.BlockSpec((tt, nb_), lambda j, k: (k, 0))
        o_spec = pl.BlockSpec((nblk, wa, wb), lambda j, k: (0, 0, 0))
        acc_shape = (ka, nb_)
    elif split == "b":
        wa, wb, grid = ka, nb_ // nblk, (nblk, nk)
        a_spec = pl.BlockSpec((tt, wa), lambda j, k: (k, 0))
        b_spec = pl.BlockSpec((tt, wb), lambda j, k: (k, j))
        o_spec = pl.BlockSpec((None, wa, wb), lambda j, k: (j, 0, 0))
        acc_shape = (wa, wb)
    else:
        wa, wb, grid = ka // nblk, nb_, (nblk, nk)
        a_spec = pl.BlockSpec((tt, wa), lambda j, k: (k, j))
        b_spec = pl.BlockSpec((tt, wb), lambda j, k: (k, 0))
        o_spec = pl.BlockSpec((None, wa, wb), lambda j, k: (j, 0, 0))
        acc_shape = (wa, wb)

    def body(a_ref, b_ref, o_ref, acc):
        k = pl.program_id(1)
        prod = lax.dot_general(a_ref[...], b_ref[...], (((0,), (0,)), ((), ())), preferred_element_type=F32)

        @pl.when(k == 0)
        def _():
            acc[...] = prod

        @pl.when(k > 0)
        def _():
            acc[...] += prod

        @pl.when(k == nk - 1)
        def _():
            if split == "cols":
                for j in range(nblk):
                    o_ref[j] = acc[:, j * wb:(j + 1) * wb].astype(o_ref.dtype)
            else:
                o_ref[...] = acc[...].astype(o_ref.dtype)

    outs, exo = _grid_call(body, name, grid, in_specs=[a_spec, b_spec], out_specs=[o_spec],
                           out_shape=[jax.ShapeDtypeStruct((nblk, wa, wb), BF16)],
                           scratch_shapes=[pltpu.VMEM(acc_shape, F32)], args=(a, b), exchange=exchange)
    return outs[0], exo


def _loss_grad(y, tgt, tm, name):
    nb, s, d = y.shape
    n = s // tm

    def body(y_ref, t_ref, dy_ref, sq_ref):
        @pl.when((pl.program_id(0) == 0) & (pl.program_id(1) == 0))
        def _():
            sq_ref[...] = jnp.zeros_like(sq_ref)

        e = y_ref[...] - t_ref[...]
        dy_ref[...] = e * (1.0 / d)
        sq_ref[0:1, :] += _colsum(e * e)

    tile = pl.BlockSpec((None, tm, d), lambda b, i: (b, i, 0))
    return pl.pallas_call(
        body, name=name, out_shape=[jax.ShapeDtypeStruct((nb, s, d), F32), jax.ShapeDtypeStruct((8, d), F32)],
        grid=(nb, n), in_specs=[tile, tile], out_specs=[tile, _full((8, d))],
        compiler_params=pltpu.CompilerParams(dimension_semantics=("arbitrary", "arbitrary")),
    )(y, tgt)


def _rows128(a):
    return a.reshape(-1, LANES)


class _ReduceScatter:
    def __init__(self, gs, cidx, idx, tag):
        self.gs, self.cidx, self.idx, self.tag = gs, cidx, idx, tag

    def swap(self):
        return _swap_halves(self.gs)

    def after_swap(self, r1):
        self.ps = [_pair_sum(g, r, self.cidx, name=f"rs_pair_{self.tag}_{a}") for a, (g, r) in enumerate(zip(self.gs, r1))]

    def chips(self):
        return _chip_exchange(self.ps)

    def after_chips(self, r2):
        self.fh = [_chip_sum(p, r, self.idx, name=f"rs_sum_{self.tag}_{a}") for a, (p, r) in enumerate(zip(self.ps, r2))]

    def share(self):
        return _sibling_share(self.fh)

    @staticmethod
    def result(fs):
        return [f.reshape(f.shape[0] * f.shape[1], f.shape[2]) for f in fs]


def kernel(x, c, ada_w, ada_b, pre_mix_g, post_mix_g, w_in, conv_w, conv_b, conv_ln_g, conv_ln_b, pool_w, pool_scale, w_out, pre_ffn_g, post_ffn_g, ffn_up, ffn_conv_w, ffn_conv_b, ffn_down, loss_target, m_ada_w, m_ada_b, m_pre_mix_g, m_post_mix_g, m_w_in, m_conv_w, m_conv_b, m_conv_ln_g, m_conv_ln_b, m_pool_w, m_pool_scale, m_w_out, m_pre_ffn_g, m_post_ffn_g, m_ffn_up, m_ffn_conv_w, m_ffn_conv_b, m_ffn_down, v_ada_w, v_ada_b, v_pre_mix_g, v_post_mix_g, v_w_in, v_conv_w, v_conv_b, v_conv_ln_g, v_conv_ln_b, v_pool_w, v_pool_scale, v_w_out, v_pre_ffn_g, v_post_ffn_g, v_ffn_up, v_ffn_conv_w, v_ffn_conv_b, v_ffn_down):
    nb, s, d = x.shape
    nl = w_in.shape[0]
    taps = conv_w.shape[1]
    ccol = conv_w.shape[2]
    dc = conv_b.shape[1]
    fcol = ffn_conv_w.shape[2]
    f2 = ffn_conv_b.shape[1]
    nmod = ada_b.shape[1] // d
    acol = ada_w.shape[2]
    tm = min(256, s)
    tt = min(2048, (nb * s) // 2)

    xi, yi, ci = _pos()
    jm = 2 * xi + yi
    cidx = jnp.reshape(ci, (1,)).astype(jnp.int32)
    idx = jnp.stack([jm, ci]).astype(jnp.int32)

    n_cw, n_fw, n_c = nl * taps * ccol, nl * 3 * fcol, nb * d
    packed = jnp.concatenate([conv_w.reshape(-1), ffn_conv_w.reshape(-1), c.reshape(-1)])
    got = _gather8(_rows128(packed), name="gather_small").reshape(N_DEV, -1)
    chips = got[0::2]
    cw_full = chips[:, :n_cw].reshape(N_CHIPS, nl, taps, ccol).transpose(1, 2, 0, 3).reshape(nl, taps, dc)
    fw_full = chips[:, n_cw:n_cw + n_fw].reshape(N_CHIPS, nl, 3, fcol).transpose(1, 2, 0, 3).reshape(nl, 3, f2)
    c_all = got[:, n_cw + n_fw:].reshape(N_DEV * nb, d)

    ada_b_cols = lax.dynamic_slice_in_dim(ada_b, jm * acol, acol, axis=1).reshape(nl, 1, acol)
    mod_cols = _ada_forward(c_all, ada_w, ada_b_cols, name="ada_forward")
    by_owner = mod_cols.reshape(nl, N_DEV, nb, acol).transpose(1, 0, 2, 3).reshape(N_DEV, -1, LANES)
    mod_own = _rows_to_owners(by_owner, name="mod_to_owners").reshape(N_CHIPS, nl, nb, acol)
    mod_own = mod_own.transpose(1, 2, 0, 3).reshape(nl, nb, nmod, d)
    mod_own = jnp.pad(mod_own, ((0, 0), (0, 0), (0, 8 - nmod), (0, 0)))

    vec_d = jnp.stack([pre_mix_g, post_mix_g, pre_ffn_g, post_ffn_g], axis=1)
    vec_c = jnp.stack([conv_b, conv_ln_g, conv_ln_b, pool_scale], axis=1)
    cw_pad = jnp.pad(cw_full, ((0, 0), (0, HALO - taps), (0, 0)))
    fw_rows = jnp.concatenate([fw_full, ffn_conv_b[:, None, :], jnp.zeros((nl, 4, f2), F32)], axis=1)
    pw_b = pool_w.astype(BF16)

    win_b, wout_b, wup_b, wdn_b = (w.astype(BF16) for w in (w_in, w_out, ffn_up, ffn_down))

    def others(l):
        return [win_b[l], wout_b[l], wdn_b[l]]

    win_g, wout_g, wdn_g = _run_exchange(_gather(others(0)), name="gather_0")
    saved = []
    xs = x
    for l in range(nl):
        (x1, h1, u1, ac1, dp1, z1, o1), (wup_g,) = _mixer_forward(
            xs, mod_own[l], vec_d[l], vec_c[l], cw_pad[l], pw_b[l], win_g, wout_g, taps, tm, name=f"mixer_fwd_{l}",
            exchange=_gather([wup_b[l]]))
        (x2, h2, u2, hid2, o2), nxt = _ffn_forward(
            x1, mod_own[l], vec_d[l], fw_rows[l], wup_g, wdn_g, tm, name=f"ffn_fwd_{l}",
            exchange=_gather(others(l + 1)) if l + 1 < nl else None)
        saved.append((xs, h1, u1, ac1, dp1, z1, o1, x1, h2, u2, hid2, o2, win_g, wout_g, wup_g, wdn_g))
        if l + 1 < nl:
            win_g, wout_g, wdn_g = nxt
        xs = x2

    dx, sq = _loss_grad(xs, loss_target, tm, name="loss_grad")
    loss = lax.psum(0.5 * jnp.sum(sq) / d, ("x", "y", "c"))

    flat = lambda a: a.reshape(nb * s, a.shape[-1])
    small = [None] * nl
    big_mlp, big_mix = [None] * nl, [None] * nl
    mlp = mix = None
    for l in reversed(range(nl)):
        x0, h1, u1, ac1, dp1, z1, o1, x1, h2, u2, hid2, o2, win_g, wout_g, wup_g, wdn_g = saved[l]
        (dx, du2, do2, rowd2, rowb2, dfw), got = _ffn_backward(
            dx, x1, o2, u2, mod_own[l], vec_d[l], fw_rows[l], wup_g, wdn_g, tm, name=f"ffn_bwd_{l}",
            exchange=_combine([mlp.chips(), mix.swap()]) if mlp else None)
        if mlp:
            mlp.after_chips(got[:2])
            mix.after_swap(got[2:])
        g_up, got = _weight_grad(flat(h2), flat(du2), N_CHIPS, "b", tt, name=f"grad_ffn_up_{l}",
                                 exchange=_combine([mlp.share(), mix.chips()]) if mlp else None)
        if mlp:
            big_mlp[l + 1] = mlp.result(got[:2])
            mix.after_chips(got[2:])
        g_dn, got = _weight_grad(flat(hid2), flat(do2), 2, "a", tt, name=f"grad_ffn_down_{l}",
                                 exchange=mix.share() if mix else None)
        if mix:
            big_mix[l + 1] = mix.result(got)
        mlp = _ReduceScatter([g_up, g_dn.reshape(N_CHIPS, -1, d)], cidx, idx, f"mlp_{l}")
        if l == 0:
            mlp.after_swap(_run_exchange(mlp.swap(), name="rs_swap_mlp_0"))
        (dx, du1, do1, rowd1, rowb1, rowc, dcw, dpw), got = _mixer_backward(
            dx, x0, o1, u1, ac1, dp1, mod_own[l], vec_d[l], vec_c[l], cw_pad[l], pw_b[l], win_g, wout_g, taps, tm,
            name=f"mixer_bwd_{l}", exchange=mlp.swap() if l > 0 else mlp.chips())
        if l > 0:
            mlp.after_swap(got)
        else:
            mlp.after_chips(got)
        g_in, got = _weight_grad(flat(h1), flat(du1), N_CHIPS, "cols", tt, name=f"grad_w_in_{l}",
                                 exchange=mlp.share() if l == 0 else None)
        if l == 0:
            big_mlp[0] = mlp.result(got)
        g_out, _ = _weight_grad(flat(z1), flat(do1), 1, "cols", tt, name=f"grad_w_out_{l}")
        mix = _ReduceScatter([g_in, g_out.reshape(N_CHIPS, -1, d)], cidx, idx, f"mix_{l}")
        small[l] = dict(rowd=rowd1 + rowd2, rowb=rowb1 + rowb2, rowc=rowc, dcw=dcw[:taps], dpw=dpw, dfw=dfw)
    mix.after_swap(_run_exchange(mix.swap(), name="rs_swap_mix_0"))
    mix.after_chips(_run_exchange(mix.chips(), name="rs_chips_mix_0"))
    big_mix[0] = mix.result(_run_exchange(mix.share(), name="rs_share_mix_0"))

    dmod_own = jnp.stack([small[l]["rowb"][:, :nmod, :] for l in range(nl)])
    dmod_all = _gather8(_rows128(dmod_own), name="gather_dmod").reshape(N_DEV, nl, nb, nmod * d)
    dmod_all = dmod_all.transpose(1, 0, 2, 3).reshape(nl, N_DEV * nb, nmod * d)
    dmod_cols = lax.dynamic_slice_in_dim(dmod_all, jm * acol, acol, axis=2)
    (g_ada_w, d_ada_w, nm_ada_w, nv_ada_w), _ = _ada_update(c_all, dmod_cols, ada_w, m_ada_w, v_ada_w, name="ada_update")

    def st(key, row=None):
        return jnp.stack([small[l][key] if row is None else small[l][key][row] for l in range(nl)])

    local = {
        "ada_b": dmod_own.sum(axis=1).reshape(nl, nmod * d),
        "pre_mix_g": st("rowd", 0), "post_mix_g": st("rowd", 1),
        "conv_b": st("rowc", 0), "conv_ln_g": st("rowc", 1), "conv_ln_b": st("rowc", 2),
        "pool_w": st("dpw"), "pool_scale": st("rowc", 3),
        "pre_ffn_g": st("rowd", 2), "post_ffn_g": st("rowd", 3),
        "ffn_conv_b": st("dfw", 3), "conv_w": st("dcw"), "ffn_conv_w": jnp.stack([small[l]["dfw"][:3] for l in range(nl)]),
    }
    names = list(local)
    sizes = [local[k].size for k in names]
    pad = -sum(sizes) % (2 * SUBLANES * LANES)
    packed = jnp.concatenate([local[k].reshape(-1) for k in names] + [jnp.zeros((pad,), F32)])
    summed = _allreduce8(_rows128(packed), name="allreduce_small").reshape(-1)
    grads, off = {}, 0
    for k, sz in zip(names, sizes):
        grads[k] = summed[off:off + sz].reshape(local[k].shape)
        off += sz
    grads["conv_w"] = lax.dynamic_slice_in_dim(grads["conv_w"], jm * ccol, ccol, axis=2)
    grads["ffn_conv_w"] = lax.dynamic_slice_in_dim(grads["ffn_conv_w"], jm * fcol, fcol, axis=2)

    params = dict(ada_b=(ada_b, m_ada_b, v_ada_b), pre_mix_g=(pre_mix_g, m_pre_mix_g, v_pre_mix_g),
                  post_mix_g=(post_mix_g, m_post_mix_g, v_post_mix_g), conv_b=(conv_b, m_conv_b, v_conv_b),
                  conv_ln_g=(conv_ln_g, m_conv_ln_g, v_conv_ln_g), conv_ln_b=(conv_ln_b, m_conv_ln_b, v_conv_ln_b),
                  pool_w=(pool_w, m_pool_w, v_pool_w), pool_scale=(pool_scale, m_pool_scale, v_pool_scale),
                  pre_ffn_g=(pre_ffn_g, m_pre_ffn_g, v_pre_ffn_g), post_ffn_g=(post_ffn_g, m_post_ffn_g, v_post_ffn_g),
                  ffn_conv_b=(ffn_conv_b, m_ffn_conv_b, v_ffn_conv_b), conv_w=(conv_w, m_conv_w, v_conv_w),
                  ffn_conv_w=(ffn_conv_w, m_ffn_conv_w, v_ffn_conv_w))
    pack = lambda i, g=None: _rows128(jnp.concatenate([(grads[k] if g else params[k][i]).reshape(-1) for k in names]))
    sd, sm, sv = _adamw_flat(pack(0), pack(0, True), pack(1), pack(2), name="adamw_small")
    outs = {}
    off = 0
    for k in names:
        shape, sz = params[k][0].shape, params[k][0].size
        outs[k] = (grads[k],) + tuple(a.reshape(-1)[off:off + sz].reshape(shape) for a in (sd, sm, sv))
        off += sz

    outs["ada_w"] = (g_ada_w, d_ada_w, nm_ada_w, nv_ada_w)
    for k, w, m, v, gs in [("w_in", w_in, m_w_in, v_w_in, [big_mix[l][0] for l in range(nl)]),
                           ("w_out", w_out, m_w_out, v_w_out, [big_mix[l][1] for l in range(nl)]),
                           ("ffn_up", ffn_up, m_ffn_up, v_ffn_up, [big_mlp[l][0] for l in range(nl)]),
                           ("ffn_down", ffn_down, m_ffn_down, v_ffn_down, [big_mlp[l][1] for l in range(nl)])]:
        outs[k] = tuple(_adamw_layers(w, m, v, gs, name=f"adamw_{k}"))

    order = ["ada_w", "ada_b", "pre_mix_g", "post_mix_g", "w_in", "conv_w", "conv_b", "conv_ln_g", "conv_ln_b", "pool_w",
             "pool_scale", "w_out", "pre_ffn_g", "post_ffn_g", "ffn_up", "ffn_conv_w", "ffn_conv_b", "ffn_down"]
    return (loss, dx) + tuple(outs[k][i] for i in range(4) for k in order)
```

```python
import functools

import jax
import jax.numpy as jnp
from jax import lax
from jax.experimental import pallas as pl
from jax.experimental.pallas import tpu as pltpu

F32 = jnp.float32
BF16 = jnp.bfloat16
MESH = pl.DeviceIdType.MESH

EPS = 1e-6
POOL_WINDOWS = (2, 4, 8, 16)
ADAM_LR = 0.001
ADAM_B1 = 0.9
ADAM_B2 = 0.999
ADAM_EPS = 1e-08
ADAM_WD = 0.01
ADAM_STEP = 10

N_CHIPS = 4
N_DEV = 8
LANES = 128
SUBLANES = 8
HALO = 32
FHALO = 8
VMEM_LIMIT = 60 * 1024 * 1024


def _pos():
    return lax.axis_index("x"), lax.axis_index("y"), lax.axis_index("c")


def _flip(v, f):
    return 1 - v if f else v


def _full(shape):
    nd = len(shape)
    return pl.BlockSpec(shape, lambda *_: (0,) * nd)


_ANY = pl.BlockSpec(memory_space=pl.ANY)
_VMEM = pl.BlockSpec(memory_space=pltpu.VMEM)


def _sigmoid(v):
    return 1.0 / (1.0 + jnp.exp(-v))


def _colsum(v):
    return jnp.sum(v, axis=0, keepdims=True)


def _gather8(v, name):
    r, ccols = v.shape

    def body(v_ref, out_ref, send_sems, recv_sems, local_sem):
        x, y, c = _pos()
        me = 4 * x + 2 * y + c
        mine = pltpu.make_async_copy(v_ref, out_ref.at[me], local_sem)
        mine.start()
        peers = [(_flip(x, (k >> 2) & 1), _flip(y, (k >> 1) & 1), _flip(c, k & 1)) for k in range(1, N_DEV)]
        sends = []
        for k, peer in enumerate(peers):
            cp = pltpu.make_async_remote_copy(src_ref=v_ref, dst_ref=out_ref.at[me], send_sem=send_sems.at[k],
                                              recv_sem=recv_sems.at[k], device_id=peer, device_id_type=MESH)
            cp.start()
            sends.append(cp)
        for k, peer in enumerate(peers):
            pidx = 4 * peer[0] + 2 * peer[1] + peer[2]
            pltpu.make_async_remote_copy(src_ref=v_ref, dst_ref=out_ref.at[pidx], send_sem=send_sems.at[k],
                                         recv_sem=recv_sems.at[k], device_id=peer, device_id_type=MESH).wait_recv()
        for cp in sends:
            cp.wait_send()
        mine.wait()

    return pl.pallas_call(
        body, name=name, out_shape=jax.ShapeDtypeStruct((N_DEV, r, ccols), v.dtype),
        in_specs=[_VMEM], out_specs=_VMEM,
        scratch_shapes=[pltpu.SemaphoreType.DMA((N_DEV - 1,)), pltpu.SemaphoreType.DMA((N_DEV - 1,)),
                        pltpu.SemaphoreType.DMA(())],
        compiler_params=pltpu.CompilerParams(vmem_limit_bytes=VMEM_LIMIT),
    )(v)


def _rows_to_owners(v, name):
    _, r, ccols = v.shape

    def body(v_ref, out_ref, send_sems, recv_sems, local_sem):
        x, y, c = _pos()
        jm = 2 * x + y
        mine = pltpu.make_async_copy(v_ref.at[2 * jm + c], out_ref.at[jm], local_sem)
        mine.start()
        peers, pjs = _chip_peers(x, y, c)
        sends = []
        for k, peer in enumerate(peers):
            cp = pltpu.make_async_remote_copy(src_ref=v_ref.at[2 * pjs[k] + c], dst_ref=out_ref.at[jm],
                                              send_sem=send_sems.at[k], recv_sem=recv_sems.at[k],
                                              device_id=peer, device_id_type=MESH)
            cp.start()
            sends.append(cp)
        for k, peer in enumerate(peers):
            pltpu.make_async_remote_copy(src_ref=v_ref.at[0], dst_ref=out_ref.at[pjs[k]], send_sem=send_sems.at[k],
                                         recv_sem=recv_sems.at[k], device_id=peer, device_id_type=MESH).wait_recv()
        for cp in sends:
            cp.wait_send()
        mine.wait()

    return pl.pallas_call(
        body, name=name, out_shape=jax.ShapeDtypeStruct((N_CHIPS, r, ccols), v.dtype),
        in_specs=[_VMEM], out_specs=_VMEM,
        scratch_shapes=[pltpu.SemaphoreType.DMA((N_CHIPS - 1,)), pltpu.SemaphoreType.DMA((N_CHIPS - 1,)),
                        pltpu.SemaphoreType.DMA(())],
        compiler_params=pltpu.CompilerParams(vmem_limit_bytes=VMEM_LIMIT),
    )(v)


def _allreduce8(v, name):
    r, ccols = v.shape
    h = r // 2

    def body(v_ref, out_ref, whole, half, send_sems, recv_sems):
        x, y, c = _pos()
        sib = (x, y, 1 - c)
        mine = pl.ds(pl.multiple_of(c * h, SUBLANES), h)
        theirs = pl.ds(pl.multiple_of((1 - c) * h, SUBLANES), h)

        def exchange(src, dst, k, peer):
            cp = pltpu.make_async_remote_copy(src_ref=src, dst_ref=dst, send_sem=send_sems.at[k],
                                              recv_sem=recv_sems.at[k], device_id=peer, device_id_type=MESH)
            cp.start()
            cp.wait()

        exchange(v_ref, whole, 0, sib)
        out_ref[...] = v_ref[...] + whole[...]
        for k, peer in ((1, (1 - x, y, c)), (2, (x, 1 - y, c))):
            exchange(out_ref.at[mine], half.at[k - 1], k, peer)
            out_ref[mine, :] = out_ref[mine, :] + half[k - 1]
        exchange(out_ref.at[mine], half.at[2], 3, sib)
        out_ref[theirs, :] = half[2]

    return pl.pallas_call(
        body, name=name, out_shape=jax.ShapeDtypeStruct((r, ccols), v.dtype),
        in_specs=[_VMEM], out_specs=_VMEM,
        scratch_shapes=[pltpu.VMEM((r, ccols), v.dtype), pltpu.VMEM((3, h, ccols), v.dtype),
                        pltpu.SemaphoreType.DMA((4,)), pltpu.SemaphoreType.DMA((4,))],
        compiler_params=pltpu.CompilerParams(vmem_limit_bytes=VMEM_LIMIT),
    )(v)


def _chip_peers(x, y, c):
    peers = [(_flip(x, (k >> 1) & 1), _flip(y, k & 1), c) for k in range(1, N_CHIPS)]
    return peers, [2 * p[0] + p[1] for p in peers]


class _Exchange:
    def __init__(self, ins, outs, aliases, n_sems, n_local, start, finish):
        self.ins, self.outs, self.aliases = list(ins), list(outs), dict(aliases)
        self.n_sems, self.n_local, self.start, self.finish = n_sems, n_local, start, finish

    def scratch(self):
        return [pltpu.SemaphoreType.DMA((self.n_sems,)), pltpu.SemaphoreType.DMA((self.n_sems,)),
                pltpu.SemaphoreType.DMA((max(self.n_local, 1),))]


class _Sems:
    def __init__(self, send, recv, local, base=0, lbase=0):
        self.send, self.recv, self.loc, self.base, self.lbase = send, recv, local, base, lbase

    def shifted(self, by, lby):
        return _Sems(self.send, self.recv, self.loc, self.base + by, self.lbase + lby)

    def local(self, k):
        return self.loc.at[self.lbase + k]


def _remote(src, dst, sems, k, peer):
    return pltpu.make_async_remote_copy(src_ref=src, dst_ref=dst, send_sem=sems.send.at[sems.base + k],
                                        recv_sem=sems.recv.at[sems.base + k], device_id=peer, device_id_type=MESH)


def _combine(exs):
    ins = [a for ex in exs for a in ex.ins]
    outs = [o for ex in exs for o in ex.outs]
    aliases, spans, ni, no, ns, nloc = {}, [], 0, 0, 0, 0
    for ex in exs:
        aliases.update({ni + a: no + b for a, b in ex.aliases.items()})
        spans.append((ni, no, ns, nloc))
        ni, no, ns, nloc = ni + len(ex.ins), no + len(ex.outs), ns + ex.n_sems, nloc + ex.n_local

    def each(which):
        def run(ins_, outs_, sems):
            for ex, (i0, o0, s0, l0) in zip(exs, spans):
                getattr(ex, which)(ins_[i0:i0 + len(ex.ins)], outs_[o0:o0 + len(ex.outs)], sems.shifted(s0, l0))
        return run

    return _Exchange(ins, outs, aliases, ns, nloc, each("start"), each("finish"))


def _gather(shards):
    n = len(shards)
    per = N_CHIPS - 1
    halves = [s.reshape(2, s.shape[0] // 2, s.shape[1]) for s in shards]

    def copies(ins, outs, sems):
        x, y, c = _pos()
        jm = 2 * x + y
        sib = (x, y, 1 - c)
        peers, pjs = _chip_peers(x, y, c)
        sends, recvs, passes, passed = [], [], [], []
        for a in range(n):
            for k, peer in enumerate(peers):
                landed, theirs = outs[a].at[pjs[k], c], outs[a].at[pjs[k], 1 - c]
                sends.append(_remote(ins[a].at[c], outs[a].at[jm, c], sems, 2 * (a * per + k), peer))
                recvs.append(_remote(landed, landed, sems, 2 * (a * per + k), peer))
                passes.append(_remote(landed, landed, sems, 2 * (a * per + k) + 1, sib))
                passed.append(_remote(theirs, theirs, sems, 2 * (a * per + k) + 1, sib))
        return sends, recvs, passes, passed

    def local(ins, outs, sems):
        x, y, _ = _pos()
        return [pltpu.make_async_copy(ins[a], outs[a].at[2 * x + y], sems.local(a)) for a in range(n)]

    def start(ins, outs, sems):
        for cp in local(ins, outs, sems) + copies(ins, outs, sems)[0]:
            cp.start()

    def finish(ins, outs, sems):
        sends, recvs, passes, passed = copies(ins, outs, sems)
        for got, fwd in zip(recvs, passes):
            got.wait_recv()
            fwd.start()
        for cp in passed:
            cp.wait_recv()
        for cp in sends + passes:
            cp.wait_send()
        for cp in local(ins, outs, sems):
            cp.wait()

    outs = [jax.ShapeDtypeStruct((N_CHIPS,) + h.shape, h.dtype) for h in halves]
    return _Exchange(halves, outs, {}, 2 * n * per, n, start, finish)


def _whole(gathered):
    return [g.reshape(g.shape[0], g.shape[1] * g.shape[2], g.shape[3]) for g in gathered]


def _swap_halves(gs):
    n = len(gs)
    halves = [g.reshape(g.shape[0], 2, g.shape[1] // 2, g.shape[2]) for g in gs]

    def copies(ins, outs, sems):
        x, y, c = _pos()
        sib = (x, y, 1 - c)
        return [_remote(ins[a].at[:, 1 - c], outs[a], sems, a, sib) for a in range(n)]

    def start(ins, outs, sems):
        for cp in copies(ins, outs, sems):
            cp.start()

    def finish(ins, outs, sems):
        for cp in copies(ins, outs, sems):
            cp.wait()

    outs = [jax.ShapeDtypeStruct((g.shape[0], g.shape[1] // 2, g.shape[2]), g.dtype) for g in gs]
    return _Exchange(halves, outs, {}, n, 0, start, finish)


def _chip_exchange(ps):
    n = len(ps)
    per = N_CHIPS - 1

    def copies(ins, outs, sems):
        x, y, c = _pos()
        peers, pjs = _chip_peers(x, y, c)
        return [_remote(ins[a].at[pjs[k]], outs[a].at[k], sems, a * per + k, peer)
                for a in range(n) for k, peer in enumerate(peers)]

    def start(ins, outs, sems):
        for cp in copies(ins, outs, sems):
            cp.start()

    def finish(ins, outs, sems):
        for cp in copies(ins, outs, sems):
            cp.wait()

    outs = [jax.ShapeDtypeStruct((per,) + p.shape[1:], p.dtype) for p in ps]
    return _Exchange(ps, outs, {}, n * per, 0, start, finish)


def _sibling_share(fs):
    n = len(fs)

    def copies(outs, sems):
        x, y, c = _pos()
        sib = (x, y, 1 - c)
        sends = [_remote(outs[a].at[c], outs[a].at[c], sems, a, sib) for a in range(n)]
        recvs = [_remote(outs[a].at[1 - c], outs[a].at[1 - c], sems, a, sib) for a in range(n)]
        return sends, recvs

    def start(ins, outs, sems):
        for cp in copies(outs, sems)[0]:
            cp.start()

    def finish(ins, outs, sems):
        sends, recvs = copies(outs, sems)
        for cp in recvs:
            cp.wait_recv()
        for cp in sends:
            cp.wait_send()

    outs = [jax.ShapeDtypeStruct(f.shape, f.dtype) for f in fs]
    return _Exchange(fs, outs, {a: a for a in range(n)}, n, 0, start, finish)


def _run_exchange(ex, name):
    ni, no = len(ex.ins), len(ex.outs)

    def body(*refs):
        ins, outs, sems = refs[:ni], refs[ni:ni + no], _Sems(*refs[ni + no:])
        ex.start(ins, outs, sems)
        ex.finish(ins, outs, sems)

    return pl.pallas_call(
        body, name=name, out_shape=ex.outs, in_specs=[_ANY] * ni, out_specs=[_ANY] * no,
        input_output_aliases=ex.aliases, scratch_shapes=ex.scratch(),
    )(*ex.ins)


def _grid_call(body, name, grid, in_specs, out_specs, out_shape, scratch_shapes, args, exchange=None):
    ni, no = len(in_specs), len(out_specs)
    params = pltpu.CompilerParams(dimension_semantics=("arbitrary",) * len(grid), vmem_limit_bytes=VMEM_LIMIT)
    if exchange is None:
        outs = pl.pallas_call(body, name=name, grid=grid, in_specs=in_specs, out_specs=out_specs, out_shape=out_shape,
                              scratch_shapes=scratch_shapes, compiler_params=params)(*args)
        return list(outs), []
    ex = exchange
    nci, nco = len(ex.ins), len(ex.outs)

    def hosted(*refs):
        cin = refs[ni:ni + nci]
        cout = refs[ni + nci + no:ni + nci + no + nco]
        sems = _Sems(*refs[len(refs) - 3:])
        main = refs[:ni] + refs[ni + nci:ni + nci + no] + refs[ni + nci + no + nco:len(refs) - 3]
        ids = [pl.program_id(a) for a in range(len(grid))]
        first = functools.reduce(lambda p, q: p & q, [i == 0 for i in ids])
        last = functools.reduce(lambda p, q: p & q, [i == g - 1 for i, g in zip(ids, grid)])

        @pl.when(first)
        def _():
            ex.start(cin, cout, sems)

        body(*main)

        @pl.when(last)
        def _():
            ex.finish(cin, cout, sems)

    outs = pl.pallas_call(
        hosted, name=name, grid=grid, in_specs=list(in_specs) + [_ANY] * nci, out_specs=list(out_specs) + [_ANY] * nco,
        out_shape=list(out_shape) + ex.outs, scratch_shapes=list(scratch_shapes) + ex.scratch(),
        input_output_aliases={ni + a: no + b for a, b in ex.aliases.items()}, compiler_params=params,
    )(*args, *ex.ins)
    return list(outs[:no]), list(outs[no:])


def _row_tile(rows, cols, itemsize, budget=2 * 1024 * 1024):
    best = None
    for t in range(16, rows + 1, 16):
        if rows % t == 0 and t * cols * itemsize <= budget:
            best = t
    return best if best is not None else rows


def _pair_sum(g, r1, cidx, name):
    nj, r, ccols = g.shape
    hr = r // 2
    tr = _row_tile(hr, ccols, 4)
    nt = hr // tr

    def body(c_ref, g_ref, r_ref, o_ref):
        o_ref[...] = (g_ref[...].astype(F32) + r_ref[...].astype(F32)).astype(o_ref.dtype)

    return pl.pallas_call(
        body, name=name, out_shape=jax.ShapeDtypeStruct((nj, hr, ccols), g.dtype),
        grid_spec=pltpu.PrefetchScalarGridSpec(
            num_scalar_prefetch=1, grid=(nj, nt),
            in_specs=[pl.BlockSpec((None, tr, ccols), lambda j, i, c_ref: (j, c_ref[0] * nt + i, 0)),
                      pl.BlockSpec((None, tr, ccols), lambda j, i, c_ref: (j, i, 0))],
            out_specs=pl.BlockSpec((None, tr, ccols), lambda j, i, c_ref: (j, i, 0))),
        compiler_params=pltpu.CompilerParams(dimension_semantics=("arbitrary", "arbitrary")),
    )(cidx, g, r1)


def _chip_sum(p, r2, idx, name):
    nj, hr, ccols = p.shape
    tr = _row_tile(hr, ccols, 4)
    nt = hr // tr

    def body(i_ref, p_ref, r_ref, o_ref):
        s = p_ref[...].astype(F32)
        for k in range(N_CHIPS - 1):
            s = s + r_ref[k].astype(F32)
        o_ref[...] = s

    return pl.pallas_call(
        body, name=name, out_shape=jax.ShapeDtypeStruct((2, hr, ccols), F32),
        grid_spec=pltpu.PrefetchScalarGridSpec(
            num_scalar_prefetch=1, grid=(nt,),
            in_specs=[pl.BlockSpec((None, tr, ccols), lambda i, i_ref: (i_ref[0], i, 0)),
                      pl.BlockSpec((N_CHIPS - 1, tr, ccols), lambda i, i_ref: (0, i, 0))],
            out_specs=pl.BlockSpec((None, tr, ccols), lambda i, i_ref: (i_ref[1], i, 0))),
        compiler_params=pltpu.CompilerParams(dimension_semantics=("arbitrary",)),
    )(idx, p, r2)


def _adam_math(w, g, m, v):
    m2 = ADAM_B1 * m + (1.0 - ADAM_B1) * g
    v2 = ADAM_B2 * v + (1.0 - ADAM_B2) * (g * g)
    m_hat = m2 / (1.0 - ADAM_B1 ** ADAM_STEP)
    v_hat = v2 / (1.0 - ADAM_B2 ** ADAM_STEP)
    delta = -ADAM_LR * (m_hat / (jnp.sqrt(v_hat) + ADAM_EPS) + ADAM_WD * w)
    return delta, m2, v2


def _adamw_layers(w, m, v, gs, name):
    nl, r, ccols = w.shape
    ng = len(gs)
    tr = _row_tile(r, ccols, 4, budget=1024 * 1024)
    nt = r // tr

    def body(w_ref, m_ref, v_ref, *rest):
        g_refs, (go_ref, d_ref, mo_ref, vo_ref) = rest[:ng], rest[ng:]
        l = pl.program_id(0)
        g = g_refs[0][...]
        for k in range(1, ng):
            g = jnp.where(l == k, g_refs[k][...], g)
        delta, m2, v2 = _adam_math(w_ref[...], g, m_ref[...], v_ref[...])
        go_ref[...] = g
        d_ref[...] = delta
        mo_ref[...] = m2
        vo_ref[...] = v2

    big = pl.BlockSpec((None, tr, ccols), lambda l, i: (l, i, 0))

    def gspec(k):
        return pl.BlockSpec((tr, ccols), lambda l, i: (jnp.where(l == k, i, jnp.where(l < k, 0, nt - 1)), 0))

    assert ng == nl
    return _grid_call(body, name, (nl, nt), in_specs=[big, big, big] + [gspec(k) for k in range(ng)],
                      out_specs=[big, big, big, big], out_shape=[jax.ShapeDtypeStruct(w.shape, F32)] * 4,
                      scratch_shapes=[], args=(w, m, v, *gs))[0]


def _adamw_flat(w, g, m, v, name):
    r, ccols = w.shape

    def body(w_ref, g_ref, m_ref, v_ref, d_ref, mo_ref, vo_ref):
        delta, m2, v2 = _adam_math(w_ref[...], g_ref[...], m_ref[...], v_ref[...])
        d_ref[...] = delta
        mo_ref[...] = m2
        vo_ref[...] = v2

    return pl.pallas_call(
        body, name=name, out_shape=[jax.ShapeDtypeStruct((r, ccols), F32)] * 3,
        in_specs=[_VMEM] * 4, out_specs=[_VMEM] * 3,
        compiler_params=pltpu.CompilerParams(vmem_limit_bytes=VMEM_LIMIT),
    )(w, g, m, v)


def _ada_forward(c_all, ada_w, ada_b_cols, name):
    nl, d, ncols = ada_w.shape
    bg = c_all.shape[0]
    tn = 512 if ncols % 512 == 0 else ncols

    def body(c_ref, w_ref, b_ref, o_ref):
        cv = c_ref[...]
        ca = (cv * _sigmoid(cv)).astype(BF16)
        o_ref[...] = jnp.dot(ca, w_ref[...].astype(BF16), preferred_element_type=F32) + b_ref[...]

    return pl.pallas_call(
        body, name=name, out_shape=jax.ShapeDtypeStruct((nl, bg, ncols), F32),
        grid=(nl, ncols // tn),
        in_specs=[pl.BlockSpec((bg, d), lambda l, j: (0, 0)),
                  pl.BlockSpec((None, d, tn), lambda l, j: (l, 0, j)),
                  pl.BlockSpec((None, 1, tn), lambda l, j: (l, 0, j))],
        out_specs=pl.BlockSpec((None, bg, tn), lambda l, j: (l, 0, j)),
        compiler_params=pltpu.CompilerParams(dimension_semantics=("arbitrary", "arbitrary")),
    )(c_all, ada_w, ada_b_cols)


def _ada_update(c_all, dmod_cols, w, m, v, name, exchange=None):
    nl, d, ncols = w.shape
    bg = c_all.shape[0]
    tn = 512 if ncols % 512 == 0 else ncols

    def body(c_ref, dm_ref, w_ref, m_ref, v_ref, go_ref, d_ref, mo_ref, vo_ref):
        cv = c_ref[...]
        ca = (cv * _sigmoid(cv)).astype(BF16)
        g = lax.dot_general(ca, dm_ref[...].astype(BF16), (((0,), (0,)), ((), ())), preferred_element_type=F32)
        delta, m2, v2 = _adam_math(w_ref[...], g, m_ref[...], v_ref[...])
        go_ref[...] = g
        d_ref[...] = delta
        mo_ref[...] = m2
        vo_ref[...] = v2

    big = pl.BlockSpec((None, d, tn), lambda l, j: (l, 0, j))
    return _grid_call(
        body, name, (nl, ncols // tn),
        in_specs=[pl.BlockSpec((bg, d), lambda l, j: (0, 0)),
                  pl.BlockSpec((None, bg, tn), lambda l, j: (l, 0, j)), big, big, big],
        out_specs=[big, big, big, big], out_shape=[jax.ShapeDtypeStruct(w.shape, F32)] * 4,
        scratch_shapes=[], args=(c_all, dmod_cols, w, m, v), exchange=exchange)


def _load_weights(first, pairs, sems):
    @pl.when(first)
    def _():
        cps = [pltpu.make_async_copy(src, dst, sems.at[k]) for k, (src, dst) in enumerate(pairs)]
        for cp in cps:
            cp.start()
        for cp in cps:
            cp.wait()


def _ada_norm(xv, g, sc, sh):
    r = lax.rsqrt(jnp.mean(xv * xv, axis=-1, keepdims=True) + EPS)
    xn = xv * r
    return (xn * g) * (1.0 + sc) + sh, xn, r


def _ada_norm_bwd(dh, xn, r, g, sc):
    d_sh = _colsum(dh)
    d_sc = _colsum(dh * (xn * g))
    dxg = dh * (1.0 + sc)
    d_g = _colsum(dxg * xn)
    gd = dxg * g
    dx = r * (gd - xn * jnp.mean(gd * xn, axis=-1, keepdims=True))
    return dx, d_sh, d_sc, d_g


def _gated_residual_bwd(dxo, o, g_post, gt):
    r = lax.rsqrt(jnp.mean(o * o, axis=-1, keepdims=True) + EPS)
    on = o * r
    d_gt = _colsum(dxo * (on * g_post))
    dy = dxo * (1.0 + gt)
    d_gp = _colsum(dy * on)
    gd = dy * g_post
    do = r * (gd - on * jnp.mean(gd * on, axis=-1, keepdims=True))
    return do, d_gt, d_gp


def _seq_positions(i, tm, width):
    return i * tm + lax.broadcasted_iota(jnp.int32, (tm, width), 0)


def _fill_phases(ext, phases):
    rows = ext.shape[0]
    ev = ext[...]
    for r in range(1, SUBLANES):
        phases[r - 1] = pltpu.roll(ev, rows - r, axis=0)


def _shifted_rows(ext, phases, offset, n):
    q, r = divmod(offset, SUBLANES)
    if r == 0:
        return ext[pl.ds(q * SUBLANES, n), :]
    return phases[r - 1, pl.ds(q * SUBLANES, n), :]


def _rows_before(halo, cur, shift):
    e = jnp.concatenate([halo, cur], axis=0)
    return pltpu.roll(e, shift, axis=0)[halo.shape[0]:, :]


def _rows_after(cur, halo, shift):
    e = jnp.concatenate([cur, halo], axis=0)
    return pltpu.roll(e, e.shape[0] - shift, axis=0)[:cur.shape[0], :]


def _mixer_forward(x, mod, vec_d, vec_c, cw, pw, win_g, wout_g, taps, tm, name, exchange=None):
    nb, s, d = x.shape
    n = s // tm
    nj, _, dcol = win_g.shape
    din = nj * dcol
    dc = vec_c.shape[-1]
    dpool = din - 2 * dc
    dmix = dc + dpool
    ro = wout_g.shape[1]
    ngrp = dpool // LANES

    def body(x_ref, mod_ref, vd_ref, vc_ref, cw_ref, pw_ref, win_hbm, wout_hbm,
             xo_ref, h_ref, u_ref, ac_ref, dp_ref, z_ref, o_ref,
             win_v, wout_v, ext_a, ext_p, phases, sems):
        b, i = pl.program_id(0), pl.program_id(1)
        pairs = [(win_hbm.at[j], win_v.at[:, pl.ds(j * dcol, dcol)]) for j in range(nj)]
        pairs += [(wout_hbm.at[j], wout_v.at[pl.ds(j * ro, ro), :]) for j in range(nj)]
        _load_weights((b == 0) & (i == 0), pairs, sems)

        xv = x_ref[...]
        h, _, _ = _ada_norm(xv, vd_ref[0:1, :], mod_ref[1:2, :], mod_ref[0:1, :])
        hb = h.astype(BF16)
        h_ref[...] = hb
        u = jnp.dot(hb, win_v[...], preferred_element_type=F32)
        u_ref[...] = u.astype(BF16)
        ag = u[:, :dc] * _sigmoid(u[:, dc:2 * dc])
        up = u[:, 2 * dc:]

        @pl.when(i == 0)
        def _():
            ext_a[0:HALO, :] = jnp.zeros((HALO, dc), F32)
            ext_p[0:HALO, :] = jnp.zeros((HALO, dpool), F32)

        @pl.when(i > 0)
        def _():
            ext_a[0:HALO, :] = ext_a[tm:tm + HALO, :]
            ext_p[0:HALO, :] = ext_p[tm:tm + HALO, :]

        ext_a[HALO:HALO + tm, :] = ag
        ext_p[HALO:HALO + tm, :] = up

        acc = jnp.broadcast_to(vc_ref[0:1, :], (tm, dc))
        _fill_phases(ext_a, phases)
        for k in range(taps):
            acc = acc + cw_ref[k:k + 1, :] * _shifted_rows(ext_a, phases, HALO - (taps - 1) + k, tm)
        ac_ref[...] = acc.astype(BF16)
        mu = jnp.mean(acc, axis=-1, keepdims=True)
        xc = acc - mu
        var = jnp.mean(xc * xc, axis=-1, keepdims=True)
        al = (xc * lax.rsqrt(var + EPS)) * vc_ref[1:2, :] + vc_ref[2:3, :]
        a = al * _sigmoid(al)

        pos = _seq_positions(i, tm, LANES)
        parts = [a.astype(BF16)]
        for g in range(ngrp):
            w = POOL_WINDOWS[g]
            cols = slice(g * LANES, (g + 1) * LANES)
            sw = ext_p[:, cols]
            step = 1
            while step < w:
                sw = sw + pltpu.roll(sw, step, axis=0)
                step *= 2
            cnt = jnp.minimum(pos + 1, w).astype(F32)
            dg = (sw[HALO:, :] / cnt - up[:, cols]).astype(BF16)
            dp_ref[:, cols] = dg
            q = jnp.dot(dg, pw_ref[g], preferred_element_type=F32)
            parts.append((q * vc_ref[3:4, cols]).astype(BF16))
        z = jnp.concatenate(parts, axis=-1)
        z_ref[...] = z
        o = jnp.dot(z, wout_v[...], preferred_element_type=F32)
        o_ref[...] = o
        r2 = lax.rsqrt(jnp.mean(o * o, axis=-1, keepdims=True) + EPS)
        xo_ref[...] = xv + (1.0 + mod_ref[2:3, :]) * ((o * r2) * vd_ref[1:2, :])

    def tile(width):
        return pl.BlockSpec((None, tm, width), lambda b, i: (b, i, 0))

    return _grid_call(
        body, name, (nb, n),
        in_specs=[tile(d), pl.BlockSpec((None, 8, d), lambda b, i: (b, 0, 0)), _full(vec_d.shape), _full(vec_c.shape),
                  _full(cw.shape), _full(pw.shape), _ANY, _ANY],
        out_specs=[tile(d), tile(d), tile(din), tile(dc), tile(dpool), tile(dmix), tile(d)],
        out_shape=[jax.ShapeDtypeStruct((nb, s, d), F32), jax.ShapeDtypeStruct((nb, s, d), BF16),
                   jax.ShapeDtypeStruct((nb, s, din), BF16), jax.ShapeDtypeStruct((nb, s, dc), BF16),
                   jax.ShapeDtypeStruct((nb, s, dpool), BF16), jax.ShapeDtypeStruct((nb, s, dmix), BF16),
                   jax.ShapeDtypeStruct((nb, s, d), F32)],
        scratch_shapes=[pltpu.VMEM((d, din), BF16), pltpu.VMEM((dmix, d), BF16),
                        pltpu.VMEM((HALO + tm, dc), F32), pltpu.VMEM((HALO + tm, dpool), F32),
                        pltpu.VMEM((SUBLANES - 1, HALO + tm, dc), F32), pltpu.SemaphoreType.DMA((2 * nj,))],
        args=(x, mod, vec_d, vec_c, cw, pw, win_g, wout_g), exchange=exchange)


def _mixer_backward(dxo, x, o, u, ac, dpl, mod, vec_d, vec_c, cw, pw, win_g, wout_g, taps, tm, name, exchange=None):
    nb, s, d = x.shape
    n = s // tm
    nj, _, dcol = win_g.shape
    din = nj * dcol
    dc = vec_c.shape[-1]
    dpool = din - 2 * dc
    dmix = dc + dpool
    ro = wout_g.shape[1]
    ngrp = dpool // LANES
    rext = tm + HALO

    def body(dxo_ref, x_ref, o_ref, u_ref, ac_ref, dp_ref, mod_ref, vd_ref, vc_ref, cw_ref, pw_ref, win_hbm, wout_hbm,
             dx_ref, du_ref, dob_ref, rowd_ref, rowb_ref, rowc_ref, dcw_ref, dpw_ref,
             win_v, wout_v, ext_a, ext_p, phases, sems):
        b, i = pl.program_id(0), pl.program_id(1)
        first = (b == 0) & (i == 0)
        pairs = [(win_hbm.at[j], win_v.at[:, pl.ds(j * dcol, dcol)]) for j in range(nj)]
        pairs += [(wout_hbm.at[j], wout_v.at[pl.ds(j * ro, ro), :]) for j in range(nj)]
        _load_weights(first, pairs, sems)

        @pl.when(first)
        def _():
            rowd_ref[...] = jnp.zeros_like(rowd_ref)
            rowc_ref[...] = jnp.zeros_like(rowc_ref)
            dcw_ref[...] = jnp.zeros_like(dcw_ref)
            dpw_ref[...] = jnp.zeros_like(dpw_ref)

        @pl.when(i == 0)
        def _():
            rowb_ref[...] = jnp.zeros_like(rowb_ref)
            ext_a[tm:rext, :] = jnp.zeros((HALO, dc), F32)
            ext_p[tm:rext, :] = jnp.zeros((HALO, dpool), F32)

        @pl.when(i > 0)
        def _():
            ext_a[tm:rext, :] = ext_a[0:HALO, :]
            ext_p[tm:rext, :] = ext_p[0:HALO, :]

        g_pre, g_post = vd_ref[0:1, :], vd_ref[1:2, :]
        sh, sc, gt = mod_ref[0:1, :], mod_ref[1:2, :], mod_ref[2:3, :]
        do, d_gt, d_gp = _gated_residual_bwd(dxo_ref[...], o_ref[...], g_post, gt)
        dob = do.astype(BF16)
        dob_ref[...] = dob
        dz = lax.dot_general(dob, wout_v[...], (((1,), (1,)), ((), ())), preferred_element_type=F32)

        acv = ac_ref[...].astype(F32)
        mu = jnp.mean(acv, axis=-1, keepdims=True)
        xc = acv - mu
        rstd = lax.rsqrt(jnp.mean(xc * xc, axis=-1, keepdims=True) + EPS)
        an = xc * rstd
        lg = vc_ref[1:2, :]
        al = an * lg + vc_ref[2:3, :]
        sg = _sigmoid(al)
        dal = dz[:, :dc] * (sg * (1.0 + al * (1.0 - sg)))
        d_lg = _colsum(dal * an)
        d_lb = _colsum(dal)
        dan = dal * lg
        dac = rstd * (dan - jnp.mean(dan, axis=-1, keepdims=True) - an * jnp.mean(dan * an, axis=-1, keepdims=True))
        d_cb = _colsum(dac)
        ext_a[0:tm, :] = dac
        uv = u_ref[:, 0:dc].astype(F32)
        sgu = _sigmoid(u_ref[:, dc:2 * dc].astype(F32))
        ag = uv * sgu
        dag = jnp.zeros((tm, dc), F32)
        _fill_phases(ext_a, phases)
        for k in range(taps):
            sl = _shifted_rows(ext_a, phases, taps - 1 - k, tm)
            dag = dag + cw_ref[k:k + 1, :] * sl
            dcw_ref[k:k + 1, :] += _colsum(ag * sl)
        du_ref[:, 0:dc] = (dag * sgu).astype(BF16)
        du_ref[:, dc:2 * dc] = (dag * uv * (sgu * (1.0 - sgu))).astype(BF16)

        pos = _seq_positions(n - 1 - i, tm, LANES)
        d_ps = []
        for g in range(ngrp):
            w = POOL_WINDOWS[g]
            cols = slice(g * LANES, (g + 1) * LANES)
            gcols = slice(dc + g * LANES, dc + (g + 1) * LANES)
            dgb = dp_ref[:, cols]
            q = jnp.dot(dgb, pw_ref[g], preferred_element_type=F32)
            dpg = dz[:, gcols]
            d_ps.append(_colsum(dpg * q))
            dq = (dpg * vc_ref[3:4, cols]).astype(BF16)
            dpw_ref[g] += lax.dot_general(dgb, dq, (((0,), (0,)), ((), ())), preferred_element_type=F32)
            dd = lax.dot_general(dq, pw_ref[g], (((1,), (1,)), ((), ())), preferred_element_type=F32)
            cnt = jnp.minimum(pos + 1, w).astype(F32)
            ext_p[0:tm, cols] = dd / cnt
            sw = ext_p[:, cols]
            step = 1
            while step < w:
                sw = sw + pltpu.roll(sw, rext - step, axis=0)
                step *= 2
            du_ref[:, 2 * dc + g * LANES:2 * dc + (g + 1) * LANES] = (sw[0:tm, :] - dd).astype(BF16)
        rowc_ref[0:1, :] += d_cb
        rowc_ref[1:2, :] += d_lg
        rowc_ref[2:3, :] += d_lb
        rowc_ref[3:4, :] += jnp.concatenate(d_ps, axis=-1)

        dh = lax.dot_general(du_ref[...], win_v[...], (((1,), (1,)), ((), ())), preferred_element_type=F32)
        _, xn, r1 = _ada_norm(x_ref[...], g_pre, sc, sh)
        dxb, d_sh, d_sc, d_g = _ada_norm_bwd(dh, xn, r1, g_pre, sc)
        dx_ref[...] = dxo_ref[...] + dxb
        rowd_ref[0:1, :] += d_g
        rowd_ref[1:2, :] += d_gp
        rowb_ref[0:1, :] += d_sh
        rowb_ref[1:2, :] += d_sc
        rowb_ref[2:3, :] += d_gt

    def tile(width):
        return pl.BlockSpec((None, tm, width), lambda b, i: (b, n - 1 - i, 0))

    return _grid_call(
        body, name, (nb, n),
        in_specs=[tile(d), tile(d), tile(d), tile(din), tile(dc), tile(dpool),
                  pl.BlockSpec((None, 8, d), lambda b, i: (b, 0, 0)), _full(vec_d.shape), _full(vec_c.shape),
                  _full(cw.shape), _full(pw.shape), _ANY, _ANY],
        out_specs=[tile(d), tile(din), tile(d), _full((8, d)), pl.BlockSpec((None, 8, d), lambda b, i: (b, 0, 0)),
                   _full((8, dc)), _full(cw.shape), _full(pw.shape)],
        out_shape=[jax.ShapeDtypeStruct((nb, s, d), F32), jax.ShapeDtypeStruct((nb, s, din), BF16),
                   jax.ShapeDtypeStruct((nb, s, d), BF16), jax.ShapeDtypeStruct((8, d), F32),
                   jax.ShapeDtypeStruct((nb, 8, d), F32), jax.ShapeDtypeStruct((8, dc), F32),
                   jax.ShapeDtypeStruct(cw.shape, F32), jax.ShapeDtypeStruct(pw.shape, F32)],
        scratch_shapes=[pltpu.VMEM((d, din), BF16), pltpu.VMEM((dmix, d), BF16),
                        pltpu.VMEM((rext, dc), F32), pltpu.VMEM((rext, dpool), F32),
                        pltpu.VMEM((SUBLANES - 1, rext, dc), F32), pltpu.SemaphoreType.DMA((2 * nj,))],
        args=(dxo, x, o, u, ac, dpl, mod, vec_d, vec_c, cw, pw, win_g, wout_g), exchange=exchange)


def _ffn_forward(x, mod, vec_d, fw, wup_g, wdn_g, tm, name, exchange=None):
    nb, s, d = x.shape
    n = s // tm
    nj, _, ucol = wup_g.shape
    f2 = nj * ucol
    dff = f2 // 2
    rd = wdn_g.shape[1]
    nq = nj // 2
    cs = dff // nq

    def body(x_ref, mod_ref, vd_ref, fw_ref, wup_hbm, wdn_hbm,
             xo_ref, h_ref, u_ref, hid_ref, o_ref,
             wup_v, wdn_v, prev_u, sems):
        b, i = pl.program_id(0), pl.program_id(1)
        pairs = [(wup_hbm.at[j], wup_v.at[:, pl.ds(j * ucol, ucol)]) for j in range(nj)]
        pairs += [(wdn_hbm.at[j], wdn_v.at[pl.ds(j * rd, rd), :]) for j in range(nj)]
        _load_weights((b == 0) & (i == 0), pairs, sems)

        @pl.when(i == 0)
        def _():
            prev_u[...] = jnp.zeros_like(prev_u)

        xv = x_ref[...]
        h, _, _ = _ada_norm(xv, vd_ref[2:3, :], mod_ref[4:5, :], mod_ref[3:4, :])
        hb = h.astype(BF16)
        h_ref[...] = hb

        def conv(cols):
            uc = jnp.dot(hb, wup_v[:, cols], preferred_element_type=F32)
            u_ref[:, cols] = uc.astype(BF16)
            before = prev_u[:, cols]
            prev_u[:, cols] = uc[tm - FHALO:, :]
            return (fw_ref[3:4, cols] + fw_ref[2:3, cols] * uc + fw_ref[1:2, cols] * _rows_before(before, uc, 1)
                    + fw_ref[0:1, cols] * _rows_before(before, uc, 2))

        o = jnp.zeros((tm, d), F32)
        for q in range(nq):
            val = conv(pl.ds(q * cs, cs))
            gate = conv(pl.ds(dff + q * cs, cs))
            hid = ((gate * _sigmoid(gate)) * val).astype(BF16)
            hid_ref[:, pl.ds(q * cs, cs)] = hid
            o = o + jnp.dot(hid, wdn_v[pl.ds(q * cs, cs), :], preferred_element_type=F32)
        o_ref[...] = o
        r2 = lax.rsqrt(jnp.mean(o * o, axis=-1, keepdims=True) + EPS)
        xo_ref[...] = xv + (1.0 + mod_ref[5:6, :]) * ((o * r2) * vd_ref[3:4, :])

    def tile(width):
        return pl.BlockSpec((None, tm, width), lambda b, i: (b, i, 0))

    return _grid_call(
        body, name, (nb, n),
        in_specs=[tile(d), pl.BlockSpec((None, 8, d), lambda b, i: (b, 0, 0)), _full(vec_d.shape), _full(fw.shape),
                  _ANY, _ANY],
        out_specs=[tile(d), tile(d), tile(f2), tile(dff), tile(d)],
        out_shape=[jax.ShapeDtypeStruct((nb, s, d), F32), jax.ShapeDtypeStruct((nb, s, d), BF16),
                   jax.ShapeDtypeStruct((nb, s, f2), BF16), jax.ShapeDtypeStruct((nb, s, dff), BF16),
                   jax.ShapeDtypeStruct((nb, s, d), F32)],
        scratch_shapes=[pltpu.VMEM((d, f2), BF16), pltpu.VMEM((dff, d), BF16),
                        pltpu.VMEM((FHALO, f2), F32), pltpu.SemaphoreType.DMA((2 * nj,))],
        args=(x, mod, vec_d, fw, wup_g, wdn_g), exchange=exchange)


def _ffn_backward(dxo, x, o, u, mod, vec_d, fw, wup_g, wdn_g, tm, name, exchange=None):
    nb, s, d = x.shape
    n = s // tm
    nj, _, ucol = wup_g.shape
    f2 = nj * ucol
    dff = f2 // 2
    rd = wdn_g.shape[1]
    nq = nj // 2
    cs = dff // nq
    hb_per_tile = tm // FHALO

    def body(dxo_ref, x_ref, o_ref, u_ref, uh_ref, mod_ref, vd_ref, fw_ref, wup_hbm, wdn_hbm,
             dx_ref, du_ref, dob_ref, rowd_ref, rowb_ref, dfw_ref,
             wup_v, wdn_v, next_d, sems):
        b, i = pl.program_id(0), pl.program_id(1)
        first = (b == 0) & (i == 0)
        pairs = [(wup_hbm.at[j], wup_v.at[:, pl.ds(j * ucol, ucol)]) for j in range(nj)]
        pairs += [(wdn_hbm.at[j], wdn_v.at[pl.ds(j * rd, rd), :]) for j in range(nj)]
        _load_weights(first, pairs, sems)

        @pl.when(first)
        def _():
            rowd_ref[...] = jnp.zeros_like(rowd_ref)
            dfw_ref[...] = jnp.zeros_like(dfw_ref)

        @pl.when(i == 0)
        def _():
            rowb_ref[...] = jnp.zeros_like(rowb_ref)
            next_d[...] = jnp.zeros_like(next_d)

        g_pre, g_post = vd_ref[2:3, :], vd_ref[3:4, :]
        sh, sc, gt = mod_ref[3:4, :], mod_ref[4:5, :], mod_ref[5:6, :]
        do, d_gt, d_gp = _gated_residual_bwd(dxo_ref[...], o_ref[...], g_post, gt)
        dob = do.astype(BF16)
        dob_ref[...] = dob

        keep = jnp.where(i == n - 1, 0.0, 1.0)

        def conv(cols):
            uc = u_ref[:, cols].astype(F32)
            before = uh_ref[:, cols].astype(F32) * keep
            return (fw_ref[3:4, cols] + fw_ref[2:3, cols] * uc + fw_ref[1:2, cols] * _rows_before(before, uc, 1)
                    + fw_ref[0:1, cols] * _rows_before(before, uc, 2)), uc

        def conv_bwd(cols, duc, uc):
            after = next_d[:, cols]
            next_d[:, cols] = duc[0:FHALO, :]
            d1 = _rows_after(duc, after, 1)
            d2 = _rows_after(duc, after, 2)
            dfw_ref[3:4, cols] += _colsum(duc)
            dfw_ref[2:3, cols] += _colsum(uc * duc)
            dfw_ref[1:2, cols] += _colsum(uc * d1)
            dfw_ref[0:1, cols] += _colsum(uc * d2)
            ob = (fw_ref[2:3, cols] * duc + fw_ref[1:2, cols] * d1 + fw_ref[0:1, cols] * d2).astype(BF16)
            du_ref[:, cols] = ob
            return lax.dot_general(ob, wup_v[:, cols], (((1,), (1,)), ((), ())), preferred_element_type=F32)

        dh = jnp.zeros((tm, d), F32)
        for q in range(nq):
            vcols = pl.ds(q * cs, cs)
            gcols = pl.ds(dff + q * cs, cs)
            dhid = lax.dot_general(dob, wdn_v[vcols, :], (((1,), (1,)), ((), ())), preferred_element_type=F32)
            val, uval = conv(vcols)
            gate, ugate = conv(gcols)
            sg = _sigmoid(gate)
            act = gate * sg
            dval = dhid * act
            dgate = (dhid * val) * (sg + act * (1.0 - sg))
            dh = dh + conv_bwd(vcols, dval, uval)
            dh = dh + conv_bwd(gcols, dgate, ugate)

        _, xn, r1 = _ada_norm(x_ref[...], g_pre, sc, sh)
        dxb, d_sh, d_sc, d_g = _ada_norm_bwd(dh, xn, r1, g_pre, sc)
        dx_ref[...] = dxo_ref[...] + dxb
        rowd_ref[2:3, :] += d_g
        rowd_ref[3:4, :] += d_gp
        rowb_ref[3:4, :] += d_sh
        rowb_ref[4:5, :] += d_sc
        rowb_ref[5:6, :] += d_gt

    def tile(width):
        return pl.BlockSpec((None, tm, width), lambda b, i: (b, n - 1 - i, 0))

    halo = pl.BlockSpec((None, FHALO, f2), lambda b, i: (b, jnp.maximum((n - 1 - i) * hb_per_tile - 1, 0), 0))
    return _grid_call(
        body, name, (nb, n),
        in_specs=[tile(d), tile(d), tile(d), tile(f2), halo, pl.BlockSpec((None, 8, d), lambda b, i: (b, 0, 0)),
                  _full(vec_d.shape), _full(fw.shape), _ANY, _ANY],
        out_specs=[tile(d), tile(f2), tile(d), _full((8, d)), pl.BlockSpec((None, 8, d), lambda b, i: (b, 0, 0)),
                   _full(fw.shape)],
        out_shape=[jax.ShapeDtypeStruct((nb, s, d), F32), jax.ShapeDtypeStruct((nb, s, f2), BF16),
                   jax.ShapeDtypeStruct((nb, s, d), BF16), jax.ShapeDtypeStruct((8, d), F32),
                   jax.ShapeDtypeStruct((nb, 8, d), F32), jax.ShapeDtypeStruct(fw.shape, F32)],
        scratch_shapes=[pltpu.VMEM((d, f2), BF16), pltpu.VMEM((dff, d), BF16),
                        pltpu.VMEM((FHALO, f2), F32), pltpu.SemaphoreType.DMA((2 * nj,))],
        args=(dxo, x, o, u, u, mod, vec_d, fw, wup_g, wdn_g), exchange=exchange)


def _weight_grad(a, b, nblk, split, tt, name, exchange=None):
    t, ka = a.shape
    nb_ = b.shape[1]
    nk = t // tt
    if split == "cols":
        wa, wb, grid = ka, nb_ // nblk, (1, nk)
        a_spec = pl.BlockSpec((tt, ka), lambda j, k: (k, 0))
        b_spec = pl.BlockSpec((tt, nb_), lambda j, k: (k, 0))
        o_spec = pl.BlockSpec((nblk, wa, wb), lambda j, k: (0, 0, 0))
        acc_shape = (ka, nb_)
    elif split == "b":
        wa, wb, grid = ka, nb_ // nblk, (nblk, nk)
        a_spec = pl.BlockSpec((tt, wa), lambda j, k: (k, 0))
        b_spec = pl.BlockSpec((tt, wb), lambda j, k: (k, j))
        o_spec = pl.BlockSpec((None, wa, wb), lambda j, k: (j, 0, 0))
        acc_shape = (wa, wb)
    else:
        wa, wb, grid = ka // nblk, nb_, (nblk, nk)
        a_spec = pl.BlockSpec((tt, wa), lambda j, k: (k, j))
        b_spec = pl.BlockSpec((tt, wb), lambda j, k: (k, 0))
        o_spec = pl.BlockSpec((None, wa, wb), lambda j, k: (j, 0, 0))
        acc_shape = (wa, wb)

    def body(a_ref, b_ref, o_ref, acc):
        k = pl.program_id(1)
        prod = lax.dot_general(a_ref[...], b_ref[...], (((0,), (0,)), ((), ())), preferred_element_type=F32)

        @pl.when(k == 0)
        def _():
            acc[...] = prod

        @pl.when(k > 0)
        def _():
            acc[...] += prod

        @pl.when(k == nk - 1)
        def _():
            if split == "cols":
                for j in range(nblk):
                    o_ref[j] = acc[:, j * wb:(j + 1) * wb].astype(o_ref.dtype)
            else:
                o_ref[...] = acc[...].astype(o_ref.dtype)

    outs, exo = _grid_call(body, name, grid, in_specs=[a_spec, b_spec], out_specs=[o_spec],
                           out_shape=[jax.ShapeDtypeStruct((nblk, wa, wb), BF16)],
                           scratch_shapes=[pltpu.VMEM(acc_shape, F32)], args=(a, b), exchange=exchange)
    return outs[0], exo


def _loss_grad(y, tgt, tm, name):
    nb, s, d = y.shape
    n = s // tm

    def body(y_ref, t_ref, dy_ref, sq_ref):
        @pl.when((pl.program_id(0) == 0) & (pl.program_id(1) == 0))
        def _():
            sq_ref[...] = jnp.zeros_like(sq_ref)

        e = y_ref[...] - t_ref[...]
        dy_ref[...] = e * (1.0 / d)
        sq_ref[0:1, :] += _colsum(e * e)

    tile = pl.BlockSpec((None, tm, d), lambda b, i: (b, i, 0))
    return pl.pallas_call(
        body, name=name, out_shape=[jax.ShapeDtypeStruct((nb, s, d), F32), jax.ShapeDtypeStruct((8, d), F32)],
        grid=(nb, n), in_specs=[tile, tile], out_specs=[tile, _full((8, d))],
        compiler_params=pltpu.CompilerParams(dimension_semantics=("arbitrary", "arbitrary")),
    )(y, tgt)


def _rows128(a):
    return a.reshape(-1, LANES)


class _ReduceScatter:
    def __init__(self, gs, cidx, idx, tag):
        self.gs, self.cidx, self.idx, self.tag = gs, cidx, idx, tag

    def swap(self):
        return _swap_halves(self.gs)

    def after_swap(self, r1):
        self.ps = [_pair_sum(g, r, self.cidx, name=f"rs_pair_{self.tag}_{a}") for a, (g, r) in enumerate(zip(self.gs, r1))]

    def chips(self):
        return _chip_exchange(self.ps)

    def after_chips(self, r2):
        self.fh = [_chip_sum(p, r, self.idx, name=f"rs_sum_{self.tag}_{a}") for a, (p, r) in enumerate(zip(self.ps, r2))]

    def share(self):
        return _sibling_share(self.fh)

    @staticmethod
    def result(fs):
        return [f.reshape(f.shape[0] * f.shape[1], f.shape[2]) for f in fs]


def kernel(x, c, ada_w, ada_b, pre_mix_g, post_mix_g, w_in, conv_w, conv_b, conv_ln_g, conv_ln_b, pool_w, pool_scale, w_out, pre_ffn_g, post_ffn_g, ffn_up, ffn_conv_w, ffn_conv_b, ffn_down, loss_target, m_ada_w, m_ada_b, m_pre_mix_g, m_post_mix_g, m_w_in, m_conv_w, m_conv_b, m_conv_ln_g, m_conv_ln_b, m_pool_w, m_pool_scale, m_w_out, m_pre_ffn_g, m_post_ffn_g, m_ffn_up, m_ffn_conv_w, m_ffn_conv_b, m_ffn_down, v_ada_w, v_ada_b, v_pre_mix_g, v_post_mix_g, v_w_in, v_conv_w, v_conv_b, v_conv_ln_g, v_conv_ln_b, v_pool_w, v_pool_scale, v_w_out, v_pre_ffn_g, v_post_ffn_g, v_ffn_up, v_ffn_conv_w, v_ffn_conv_b, v_ffn_down):
    nb, s, d = x.shape
    nl = w_in.shape[0]
    taps = conv_w.shape[1]
    ccol = conv_w.shape[2]
    dc = conv_b.shape[1]
    fcol = ffn_conv_w.shape[2]
    f2 = ffn_conv_b.shape[1]
    nmod = ada_b.shape[1] // d
    acol = ada_w.shape[2]
    tm = min(256, s)
    tt = min(2048, (nb * s) // 2)

    xi, yi, ci = _pos()
    jm = 2 * xi + yi
    cidx = jnp.reshape(ci, (1,)).astype(jnp.int32)
    idx = jnp.stack([jm, ci]).astype(jnp.int32)

    n_cw, n_fw, n_c = nl * taps * ccol, nl * 3 * fcol, nb * d
    packed = jnp.concatenate([conv_w.reshape(-1), ffn_conv_w.reshape(-1), c.reshape(-1)])
    got = _gather8(_rows128(packed), name="gather_small").reshape(N_DEV, -1)
    chips = got[0::2]
    cw_full = chips[:, :n_cw].reshape(N_CHIPS, nl, taps, ccol).transpose(1, 2, 0, 3).reshape(nl, taps, dc)
    fw_full = chips[:, n_cw:n_cw + n_fw].reshape(N_CHIPS, nl, 3, fcol).transpose(1, 2, 0, 3).reshape(nl, 3, f2)
    c_all = got[:, n_cw + n_fw:].reshape(N_DEV * nb, d)

    ada_b_cols = lax.dynamic_slice_in_dim(ada_b, jm * acol, acol, axis=1).reshape(nl, 1, acol)
    mod_cols = _ada_forward(c_all, ada_w, ada_b_cols, name="ada_forward")
    by_owner = mod_cols.reshape(nl, N_DEV, nb, acol).transpose(1, 0, 2, 3).reshape(N_DEV, -1, LANES)
    mod_own = _rows_to_owners(by_owner, name="mod_to_owners").reshape(N_CHIPS, nl, nb, acol)
    mod_own = mod_own.transpose(1, 2, 0, 3).reshape(nl, nb, nmod, d)
    mod_own = jnp.pad(mod_own, ((0, 0), (0, 0), (0, 8 - nmod), (0, 0)))

    vec_d = jnp.stack([pre_mix_g, post_mix_g, pre_ffn_g, post_ffn_g], axis=1)
    vec_c = jnp.stack([conv_b, conv_ln_g, conv_ln_b, pool_scale], axis=1)
    cw_pad = jnp.pad(cw_full, ((0, 0), (0, HALO - taps), (0, 0)))
    fw_rows = jnp.concatenate([fw_full, ffn_conv_b[:, None, :], jnp.zeros((nl, 4, f2), F32)], axis=1)
    pw_b = pool_w.astype(BF16)

    win_b, wout_b, wup_b, wdn_b = (w.astype(BF16) for w in (w_in, w_out, ffn_up, ffn_down))

    def others(l):
        return [win_b[l], wout_b[l], wdn_b[l]]

    win_g, wout_g, wdn_g = _whole(_run_exchange(_gather(others(0)), name="gather_0"))
    saved = []
    xs = x
    for l in range(nl):
        (x1, h1, u1, ac1, dp1, z1, o1), got = _mixer_forward(
            xs, mod_own[l], vec_d[l], vec_c[l], cw_pad[l], pw_b[l], win_g, wout_g, taps, tm, name=f"mixer_fwd_{l}",
            exchange=_gather([wup_b[l]]))
        wup_g, = _whole(got)
        (x2, h2, u2, hid2, o2), nxt = _ffn_forward(
            x1, mod_own[l], vec_d[l], fw_rows[l], wup_g, wdn_g, tm, name=f"ffn_fwd_{l}",
            exchange=_gather(others(l + 1)) if l + 1 < nl else None)
        saved.append((xs, h1, u1, ac1, dp1, z1, o1, x1, h2, u2, hid2, o2, win_g, wout_g, wup_g, wdn_g))
        if l + 1 < nl:
            win_g, wout_g, wdn_g = _whole(nxt)
        xs = x2

    dx, sq = _loss_grad(xs, loss_target, tm, name="loss_grad")
    loss = lax.psum(0.5 * jnp.sum(sq) / d, ("x", "y", "c"))

    flat = lambda a: a.reshape(nb * s, a.shape[-1])
    small = [None] * nl
    big_mlp, big_mix = [None] * nl, [None] * nl
    mlp = mix = None
    for l in reversed(range(nl)):
        x0, h1, u1, ac1, dp1, z1, o1, x1, h2, u2, hid2, o2, win_g, wout_g, wup_g, wdn_g = saved[l]
        (dx, du2, do2, rowd2, rowb2, dfw), got = _ffn_backward(
            dx, x1, o2, u2, mod_own[l], vec_d[l], fw_rows[l], wup_g, wdn_g, tm, name=f"ffn_bwd_{l}",
            exchange=_combine([mlp.chips(), mix.swap()]) if mlp else None)
        if mlp:
            mlp.after_chips(got[:2])
            mix.after_swap(got[2:])
        g_up, got = _weight_grad(flat(h2), flat(du2), N_CHIPS, "b", tt, name=f"grad_ffn_up_{l}",
                                 exchange=_combine([mlp.share(), mix.chips()]) if mlp else None)
        if mlp:
            big_mlp[l + 1] = mlp.result(got[:2])
            mix.after_chips(got[2:])
        g_dn, got = _weight_grad(flat(hid2), flat(do2), 2, "a", tt, name=f"grad_ffn_down_{l}",
                                 exchange=mix.share() if mix else None)
        if mix:
            big_mix[l + 1] = mix.result(got)
        mlp = _ReduceScatter([g_up, g_dn.reshape(N_CHIPS, -1, d)], cidx, idx, f"mlp_{l}")
        if l == 0:
            mlp.after_swap(_run_exchange(mlp.swap(), name="rs_swap_mlp_0"))
        (dx, du1, do1, rowd1, rowb1, rowc, dcw, dpw), got = _mixer_backward(
            dx, x0, o1, u1, ac1, dp1, mod_own[l], vec_d[l], vec_c[l], cw_pad[l], pw_b[l], win_g, wout_g, taps, tm,
            name=f"mixer_bwd_{l}", exchange=mlp.swap() if l > 0 else mlp.chips())
        if l > 0:
            mlp.after_swap(got)
        else:
            mlp.after_chips(got)
        g_in, got = _weight_grad(flat(h1), flat(du1), N_CHIPS, "cols", tt, name=f"grad_w_in_{l}",
                                 exchange=mlp.share() if l == 0 else None)
        if l == 0:
            big_mlp[0] = mlp.result(got)
        g_out, _ = _weight_grad(flat(z1), flat(do1), 1, "cols", tt, name=f"grad_w_out_{l}")
        mix = _ReduceScatter([g_in, g_out.reshape(N_CHIPS, -1, d)], cidx, idx, f"mix_{l}")
        small[l] = dict(rowd=rowd1 + rowd2, rowb=rowb1 + rowb2, rowc=rowc, dcw=dcw[:taps], dpw=dpw, dfw=dfw)
    mix.after_swap(_run_exchange(mix.swap(), name="rs_swap_mix_0"))
    mix.after_chips(_run_exchange(mix.chips(), name="rs_chips_mix_0"))
    big_mix[0] = mix.result(_run_exchange(mix.share(), name="rs_share_mix_0"))

    dmod_own = jnp.stack([small[l]["rowb"][:, :nmod, :] for l in range(nl)])
    dmod_all = _gather8(_rows128(dmod_own), name="gather_dmod").reshape(N_DEV, nl, nb, nmod * d)
    dmod_all = dmod_all.transpose(1, 0, 2, 3).reshape(nl, N_DEV * nb, nmod * d)
    dmod_cols = lax.dynamic_slice_in_dim(dmod_all, jm * acol, acol, axis=2)
    (g_ada_w, d_ada_w, nm_ada_w, nv_ada_w), _ = _ada_update(c_all, dmod_cols, ada_w, m_ada_w, v_ada_w, name="ada_update")

    def st(key, row=None):
        return jnp.stack([small[l][key] if row is None else small[l][key][row] for l in range(nl)])

    local = {
        "ada_b": dmod_own.sum(axis=1).reshape(nl, nmod * d),
        "pre_mix_g": st("rowd", 0), "post_mix_g": st("rowd", 1),
        "conv_b": st("rowc", 0), "conv_ln_g": st("rowc", 1), "conv_ln_b": st("rowc", 2),
        "pool_w": st("dpw"), "pool_scale": st("rowc", 3),
        "pre_ffn_g": st("rowd", 2), "post_ffn_g": st("rowd", 3),
        "ffn_conv_b": st("dfw", 3), "conv_w": st("dcw"), "ffn_conv_w": jnp.stack([small[l]["dfw"][:3] for l in range(nl)]),
    }
    names = list(local)
    sizes = [local[k].size for k in names]
    pad = -sum(sizes) % (2 * SUBLANES * LANES)
    packed = jnp.concatenate([local[k].reshape(-1) for k in names] + [jnp.zeros((pad,), F32)])
    summed = _allreduce8(_rows128(packed), name="allreduce_small").reshape(-1)
    grads, off = {}, 0
    for k, sz in zip(names, sizes):
        grads[k] = summed[off:off + sz].reshape(local[k].shape)
        off += sz
    grads["conv_w"] = lax.dynamic_slice_in_dim(grads["conv_w"], jm * ccol, ccol, axis=2)
    grads["ffn_conv_w"] = lax.dynamic_slice_in_dim(grads["ffn_conv_w"], jm * fcol, fcol, axis=2)

    params = dict(ada_b=(ada_b, m_ada_b, v_ada_b), pre_mix_g=(pre_mix_g, m_pre_mix_g, v_pre_mix_g),
                  post_mix_g=(post_mix_g, m_post_mix_g, v_post_mix_g), conv_b=(conv_b, m_conv_b, v_conv_b),
                  conv_ln_g=(conv_ln_g, m_conv_ln_g, v_conv_ln_g), conv_ln_b=(conv_ln_b, m_conv_ln_b, v_conv_ln_b),
                  pool_w=(pool_w, m_pool_w, v_pool_w), pool_scale=(pool_scale, m_pool_scale, v_pool_scale),
                  pre_ffn_g=(pre_ffn_g, m_pre_ffn_g, v_pre_ffn_g), post_ffn_g=(post_ffn_g, m_post_ffn_g, v_post_ffn_g),
                  ffn_conv_b=(ffn_conv_b, m_ffn_conv_b, v_ffn_conv_b), conv_w=(conv_w, m_conv_w, v_conv_w),
                  ffn_conv_w=(ffn_conv_w, m_ffn_conv_w, v_ffn_conv_w))
    pack = lambda i, g=None: _rows128(jnp.concatenate([(grads[k] if g else params[k][i]).reshape(-1) for k in names]))
    sd, sm, sv = _adamw_flat(pack(0), pack(0, True), pack(1), pack(2), name="adamw_small")
    outs = {}
    off = 0
    for k in names:
        shape, sz = params[k][0].shape, params[k][0].size
        outs[k] = (grads[k],) + tuple(a.reshape(-1)[off:off + sz].reshape(shape) for a in (sd, sm, sv))
        off += sz

    outs["ada_w"] = (g_ada_w, d_ada_w, nm_ada_w, nv_ada_w)
    for k, w, m, v, gs in [("w_in", w_in, m_w_in, v_w_in, [big_mix[l][0] for l in range(nl)]),
                           ("w_out", w_out, m_w_out, v_w_out, [big_mix[l][1] for l in range(nl)]),
                           ("ffn_up", ffn_up, m_ffn_up, v_ffn_up, [big_mlp[l][0] for l in range(nl)]),
                           ("ffn_down", ffn_down, m_ffn_down, v_ffn_down, [big_mlp[l][1] for l in range(nl)])]:
        outs[k] = tuple(_adamw_layers(w, m, v, gs, name=f"adamw_{k}"))

    order = ["ada_w", "ada_b", "pre_mix_g", "post_mix_g", "w_in", "conv_w", "conv_b", "conv_ln_g", "conv_ln_b", "pool_w",
             "pool_scale", "w_out", "pre_ffn_g", "post_ffn_g", "ffn_up", "ffn_conv_w", "ffn_conv_b", "ffn_down"]
    return (loss, dx) + tuple(outs[k][i] for i in range(4) for k in order)
```

```python
import functools

import jax
import jax.numpy as jnp
from jax import lax
from jax.experimental import pallas as pl
from jax.experimental.pallas import tpu as pltpu

F32 = jnp.float32
BF16 = jnp.bfloat16
MESH = pl.DeviceIdType.MESH

EPS = 1e-6
POOL_WINDOWS = (2, 4, 8, 16)
ADAM_LR = 0.001
ADAM_B1 = 0.9
ADAM_B2 = 0.999
ADAM_EPS = 1e-08
ADAM_WD = 0.01
ADAM_STEP = 10

N_CHIPS = 4
N_DEV = 8
LANES = 128
SUBLANES = 8
HALO = 32
FHALO = 8
VMEM_LIMIT = 60 * 1024 * 1024


def _pos():
    return lax.axis_index("x"), lax.axis_index("y"), lax.axis_index("c")


def _flip(v, f):
    return 1 - v if f else v


def _full(shape):
    nd = len(shape)
    return pl.BlockSpec(shape, lambda *_: (0,) * nd)


_ANY = pl.BlockSpec(memory_space=pl.ANY)
_VMEM = pl.BlockSpec(memory_space=pltpu.VMEM)


def _sigmoid(v):
    return 1.0 / (1.0 + jnp.exp(-v))


def _colsum(v):
    return jnp.sum(v, axis=0, keepdims=True)


def _gather8(v, name):
    r, ccols = v.shape

    def body(v_ref, out_ref, send_sems, recv_sems, local_sem):
        x, y, c = _pos()
        me = 4 * x + 2 * y + c
        mine = pltpu.make_async_copy(v_ref, out_ref.at[me], local_sem)
        mine.start()
        peers = [(_flip(x, (k >> 2) & 1), _flip(y, (k >> 1) & 1), _flip(c, k & 1)) for k in range(1, N_DEV)]
        sends = []
        for k, peer in enumerate(peers):
            cp = pltpu.make_async_remote_copy(src_ref=v_ref, dst_ref=out_ref.at[me], send_sem=send_sems.at[k],
                                              recv_sem=recv_sems.at[k], device_id=peer, device_id_type=MESH)
            cp.start()
            sends.append(cp)
        for k, peer in enumerate(peers):
            pidx = 4 * peer[0] + 2 * peer[1] + peer[2]
            pltpu.make_async_remote_copy(src_ref=v_ref, dst_ref=out_ref.at[pidx], send_sem=send_sems.at[k],
                                         recv_sem=recv_sems.at[k], device_id=peer, device_id_type=MESH).wait_recv()
        for cp in sends:
            cp.wait_send()
        mine.wait()

    return pl.pallas_call(
        body, name=name, out_shape=jax.ShapeDtypeStruct((N_DEV, r, ccols), v.dtype),
        in_specs=[_VMEM], out_specs=_VMEM,
        scratch_shapes=[pltpu.SemaphoreType.DMA((N_DEV - 1,)), pltpu.SemaphoreType.DMA((N_DEV - 1,)),
                        pltpu.SemaphoreType.DMA(())],
        compiler_params=pltpu.CompilerParams(vmem_limit_bytes=VMEM_LIMIT),
    )(v)


def _rows_to_owners(v, name):
    _, r, ccols = v.shape

    def body(v_ref, out_ref, send_sems, recv_sems, local_sem):
        x, y, c = _pos()
        jm = 2 * x + y
        mine = pltpu.make_async_copy(v_ref.at[2 * jm + c], out_ref.at[jm], local_sem)
        mine.start()
        peers, pjs = _chip_peers(x, y, c)
        sends = []
        for k, peer in enumerate(peers):
            cp = pltpu.make_async_remote_copy(src_ref=v_ref.at[2 * pjs[k] + c], dst_ref=out_ref.at[jm],
                                              send_sem=send_sems.at[k], recv_sem=recv_sems.at[k],
                                              device_id=peer, device_id_type=MESH)
            cp.start()
            sends.append(cp)
        for k, peer in enumerate(peers):
            pltpu.make_async_remote_copy(src_ref=v_ref.at[0], dst_ref=out_ref.at[pjs[k]], send_sem=send_sems.at[k],
                                         recv_sem=recv_sems.at[k], device_id=peer, device_id_type=MESH).wait_recv()
        for cp in sends:
            cp.wait_send()
        mine.wait()

    return pl.pallas_call(
        body, name=name, out_shape=jax.ShapeDtypeStruct((N_CHIPS, r, ccols), v.dtype),
        in_specs=[_VMEM], out_specs=_VMEM,
        scratch_shapes=[pltpu.SemaphoreType.DMA((N_CHIPS - 1,)), pltpu.SemaphoreType.DMA((N_CHIPS - 1,)),
                        pltpu.SemaphoreType.DMA(())],
        compiler_params=pltpu.CompilerParams(vmem_limit_bytes=VMEM_LIMIT),
    )(v)


def _allreduce8(v, name):
    r, ccols = v.shape
    h = r // 2

    def body(v_ref, out_ref, whole, half, send_sems, recv_sems):
        x, y, c = _pos()
        sib = (x, y, 1 - c)
        mine = pl.ds(pl.multiple_of(c * h, SUBLANES), h)
        theirs = pl.ds(pl.multiple_of((1 - c) * h, SUBLANES), h)

        def exchange(src, dst, k, peer):
            cp = pltpu.make_async_remote_copy(src_ref=src, dst_ref=dst, send_sem=send_sems.at[k],
                                              recv_sem=recv_sems.at[k], device_id=peer, device_id_type=MESH)
            cp.start()
            cp.wait()

        exchange(v_ref, whole, 0, sib)
        out_ref[...] = v_ref[...] + whole[...]
        for k, peer in ((1, (1 - x, y, c)), (2, (x, 1 - y, c))):
            exchange(out_ref.at[mine], half.at[k - 1], k, peer)
            out_ref[mine, :] = out_ref[mine, :] + half[k - 1]
        exchange(out_ref.at[mine], half.at[2], 3, sib)
        out_ref[theirs, :] = half[2]

    return pl.pallas_call(
        body, name=name, out_shape=jax.ShapeDtypeStruct((r, ccols), v.dtype),
        in_specs=[_VMEM], out_specs=_VMEM,
        scratch_shapes=[pltpu.VMEM((r, ccols), v.dtype), pltpu.VMEM((3, h, ccols), v.dtype),
                        pltpu.SemaphoreType.DMA((4,)), pltpu.SemaphoreType.DMA((4,))],
        compiler_params=pltpu.CompilerParams(vmem_limit_bytes=VMEM_LIMIT),
    )(v)


def _chip_peers(x, y, c):
    peers = [(_flip(x, (k >> 1) & 1), _flip(y, k & 1), c) for k in range(1, N_CHIPS)]
    return peers, [2 * p[0] + p[1] for p in peers]


class _Exchange:
    def __init__(self, ins, outs, aliases, n_sems, n_local, start, finish, mid=None, mid_at=1.0):
        self.ins, self.outs, self.aliases = list(ins), list(outs), dict(aliases)
        self.n_sems, self.n_local, self.start, self.finish = n_sems, n_local, start, finish
        self.mid, self.mid_at = mid, mid_at

    def scratch(self):
        return [pltpu.SemaphoreType.DMA((self.n_sems,)), pltpu.SemaphoreType.DMA((self.n_sems,)),
                pltpu.SemaphoreType.DMA((max(self.n_local, 1),))]


class _Sems:
    def __init__(self, send, recv, local, base=0, lbase=0):
        self.send, self.recv, self.loc, self.base, self.lbase = send, recv, local, base, lbase

    def shifted(self, by, lby):
        return _Sems(self.send, self.recv, self.loc, self.base + by, self.lbase + lby)

    def local(self, k):
        return self.loc.at[self.lbase + k]


def _remote(src, dst, sems, k, peer):
    return pltpu.make_async_remote_copy(src_ref=src, dst_ref=dst, send_sem=sems.send.at[sems.base + k],
                                        recv_sem=sems.recv.at[sems.base + k], device_id=peer, device_id_type=MESH)


def _combine(exs):
    ins = [a for ex in exs for a in ex.ins]
    outs = [o for ex in exs for o in ex.outs]
    aliases, spans, ni, no, ns, nloc = {}, [], 0, 0, 0, 0
    for ex in exs:
        aliases.update({ni + a: no + b for a, b in ex.aliases.items()})
        spans.append((ni, no, ns, nloc))
        ni, no, ns, nloc = ni + len(ex.ins), no + len(ex.outs), ns + ex.n_sems, nloc + ex.n_local

    def each(which):
        def run(ins_, outs_, sems):
            for ex, (i0, o0, s0, l0) in zip(exs, spans):
                stage = getattr(ex, which)
                if stage is not None:
                    stage(ins_[i0:i0 + len(ex.ins)], outs_[o0:o0 + len(ex.outs)], sems.shifted(s0, l0))
        return run

    mids = [ex.mid_at for ex in exs if ex.mid is not None]
    return _Exchange(ins, outs, aliases, ns, nloc, each("start"), each("finish"),
                     mid=each("mid") if mids else None, mid_at=max(mids) if mids else 1.0)


def _gather(shards, mid_at=1.0):
    n = len(shards)
    per = N_CHIPS - 1
    halves = [s.reshape(2, s.shape[0] // 2, s.shape[1]) for s in shards]

    def copies(ins, outs, sems):
        x, y, c = _pos()
        jm = 2 * x + y
        sib = (x, y, 1 - c)
        peers, pjs = _chip_peers(x, y, c)
        sends, recvs, passes, passed = [], [], [], []
        for a in range(n):
            for k, peer in enumerate(peers):
                landed, theirs = outs[a].at[pjs[k], c], outs[a].at[pjs[k], 1 - c]
                sends.append(_remote(ins[a].at[c], outs[a].at[jm, c], sems, 2 * (a * per + k), peer))
                recvs.append(_remote(landed, landed, sems, 2 * (a * per + k), peer))
                passes.append(_remote(landed, landed, sems, 2 * (a * per + k) + 1, sib))
                passed.append(_remote(theirs, theirs, sems, 2 * (a * per + k) + 1, sib))
        return sends, recvs, passes, passed

    def local(ins, outs, sems):
        x, y, _ = _pos()
        return [pltpu.make_async_copy(ins[a], outs[a].at[2 * x + y], sems.local(a)) for a in range(n)]

    def start(ins, outs, sems):
        for cp in local(ins, outs, sems) + copies(ins, outs, sems)[0]:
            cp.start()

    def mid(ins, outs, sems):
        _, recvs, passes, _ = copies(ins, outs, sems)
        for got, fwd in zip(recvs, passes):
            got.wait_recv()
            fwd.start()

    def finish(ins, outs, sems):
        sends, _, passes, passed = copies(ins, outs, sems)
        for cp in passed:
            cp.wait_recv()
        for cp in sends + passes:
            cp.wait_send()
        for cp in local(ins, outs, sems):
            cp.wait()

    outs = [jax.ShapeDtypeStruct((N_CHIPS,) + h.shape, h.dtype) for h in halves]
    return _Exchange(halves, outs, {}, 2 * n * per, n, start, finish, mid=mid, mid_at=mid_at)


def _whole(gathered):
    return [g.reshape(g.shape[0], g.shape[1] * g.shape[2], g.shape[3]) for g in gathered]


def _swap_halves(gs):
    n = len(gs)
    halves = [g.reshape(g.shape[0], 2, g.shape[1] // 2, g.shape[2]) for g in gs]

    def copies(ins, outs, sems):
        x, y, c = _pos()
        sib = (x, y, 1 - c)
        return [_remote(ins[a].at[:, 1 - c], outs[a], sems, a, sib) for a in range(n)]

    def start(ins, outs, sems):
        for cp in copies(ins, outs, sems):
            cp.start()

    def finish(ins, outs, sems):
        for cp in copies(ins, outs, sems):
            cp.wait()

    outs = [jax.ShapeDtypeStruct((g.shape[0], g.shape[1] // 2, g.shape[2]), g.dtype) for g in gs]
    return _Exchange(halves, outs, {}, n, 0, start, finish)


def _chip_exchange(ps):
    n = len(ps)
    per = N_CHIPS - 1

    def copies(ins, outs, sems):
        x, y, c = _pos()
        peers, pjs = _chip_peers(x, y, c)
        return [_remote(ins[a].at[pjs[k]], outs[a].at[k], sems, a * per + k, peer)
                for a in range(n) for k, peer in enumerate(peers)]

    def start(ins, outs, sems):
        for cp in copies(ins, outs, sems):
            cp.start()

    def finish(ins, outs, sems):
        for cp in copies(ins, outs, sems):
            cp.wait()

    outs = [jax.ShapeDtypeStruct((per,) + p.shape[1:], p.dtype) for p in ps]
    return _Exchange(ps, outs, {}, n * per, 0, start, finish)


def _sibling_share(fs):
    n = len(fs)

    def copies(outs, sems):
        x, y, c = _pos()
        sib = (x, y, 1 - c)
        sends = [_remote(outs[a].at[c], outs[a].at[c], sems, a, sib) for a in range(n)]
        recvs = [_remote(outs[a].at[1 - c], outs[a].at[1 - c], sems, a, sib) for a in range(n)]
        return sends, recvs

    def start(ins, outs, sems):
        for cp in copies(outs, sems)[0]:
            cp.start()

    def finish(ins, outs, sems):
        sends, recvs = copies(outs, sems)
        for cp in recvs:
            cp.wait_recv()
        for cp in sends:
            cp.wait_send()

    outs = [jax.ShapeDtypeStruct(f.shape, f.dtype) for f in fs]
    return _Exchange(fs, outs, {a: a for a in range(n)}, n, 0, start, finish)


def _run_exchange(ex, name):
    ni, no = len(ex.ins), len(ex.outs)

    def body(*refs):
        ins, outs, sems = refs[:ni], refs[ni:ni + no], _Sems(*refs[ni + no:])
        ex.start(ins, outs, sems)
        if ex.mid is not None:
            ex.mid(ins, outs, sems)
        ex.finish(ins, outs, sems)

    return pl.pallas_call(
        body, name=name, out_shape=ex.outs, in_specs=[_ANY] * ni, out_specs=[_ANY] * no,
        input_output_aliases=ex.aliases, scratch_shapes=ex.scratch(),
    )(*ex.ins)


def _grid_call(body, name, grid, in_specs, out_specs, out_shape, scratch_shapes, args, exchange=None):
    ni, no = len(in_specs), len(out_specs)
    params = pltpu.CompilerParams(dimension_semantics=("arbitrary",) * len(grid), vmem_limit_bytes=VMEM_LIMIT)
    if exchange is None:
        outs = pl.pallas_call(body, name=name, grid=grid, in_specs=in_specs, out_specs=out_specs, out_shape=out_shape,
                              scratch_shapes=scratch_shapes, compiler_params=params)(*args)
        return list(outs), []
    ex = exchange
    nci, nco = len(ex.ins), len(ex.outs)

    def hosted(*refs):
        cin = refs[ni:ni + nci]
        cout = refs[ni + nci + no:ni + nci + no + nco]
        sems = _Sems(*refs[len(refs) - 3:])
        main = refs[:ni] + refs[ni + nci:ni + nci + no] + refs[ni + nci + no + nco:len(refs) - 3]
        ids = [pl.program_id(a) for a in range(len(grid))]
        first = functools.reduce(lambda p, q: p & q, [i == 0 for i in ids])
        last = functools.reduce(lambda p, q: p & q, [i == g - 1 for i, g in zip(ids, grid)])

        @pl.when(first)
        def _():
            ex.start(cin, cout, sems)

        if ex.mid is not None:
            steps = functools.reduce(lambda p, q: p * q, grid)
            flat = functools.reduce(lambda p, q: p * q[1] + q[0], zip(ids[1:], grid[1:]), ids[0])

            @pl.when(flat == min(steps - 1, int(ex.mid_at * steps)))
            def _():
                ex.mid(cin, cout, sems)

        body(*main)

        @pl.when(last)
        def _():
            ex.finish(cin, cout, sems)

    outs = pl.pallas_call(
        hosted, name=name, grid=grid, in_specs=list(in_specs) + [_ANY] * nci, out_specs=list(out_specs) + [_ANY] * nco,
        out_shape=list(out_shape) + ex.outs, scratch_shapes=list(scratch_shapes) + ex.scratch(),
        input_output_aliases={ni + a: no + b for a, b in ex.aliases.items()}, compiler_params=params,
    )(*args, *ex.ins)
    return list(outs[:no]), list(outs[no:])


def _row_tile(rows, cols, itemsize, budget=2 * 1024 * 1024):
    best = None
    for t in range(16, rows + 1, 16):
        if rows % t == 0 and t * cols * itemsize <= budget:
            best = t
    return best if best is not None else rows


def _pair_sum(g, r1, cidx, name):
    nj, r, ccols = g.shape
    hr = r // 2
    tr = _row_tile(hr, ccols, 4)
    nt = hr // tr

    def body(c_ref, g_ref, r_ref, o_ref):
        o_ref[...] = (g_ref[...].astype(F32) + r_ref[...].astype(F32)).astype(o_ref.dtype)

    return pl.pallas_call(
        body, name=name, out_shape=jax.ShapeDtypeStruct((nj, hr, ccols), g.dtype),
        grid_spec=pltpu.PrefetchScalarGridSpec(
            num_scalar_prefetch=1, grid=(nj, nt),
            in_specs=[pl.BlockSpec((None, tr, ccols), lambda j, i, c_ref: (j, c_ref[0] * nt + i, 0)),
                      pl.BlockSpec((None, tr, ccols), lambda j, i, c_ref: (j, i, 0))],
            out_specs=pl.BlockSpec((None, tr, ccols), lambda j, i, c_ref: (j, i, 0))),
        compiler_params=pltpu.CompilerParams(dimension_semantics=("arbitrary", "arbitrary")),
    )(cidx, g, r1)


def _chip_sum(p, r2, idx, name):
    nj, hr, ccols = p.shape
    tr = _row_tile(hr, ccols, 4)
    nt = hr // tr

    def body(i_ref, p_ref, r_ref, o_ref):
        s = p_ref[...].astype(F32)
        for k in range(N_CHIPS - 1):
            s = s + r_ref[k].astype(F32)
        o_ref[...] = s

    return pl.pallas_call(
        body, name=name, out_shape=jax.ShapeDtypeStruct((2, hr, ccols), F32),
        grid_spec=pltpu.PrefetchScalarGridSpec(
            num_scalar_prefetch=1, grid=(nt,),
            in_specs=[pl.BlockSpec((None, tr, ccols), lambda i, i_ref: (i_ref[0], i, 0)),
                      pl.BlockSpec((N_CHIPS - 1, tr, ccols), lambda i, i_ref: (0, i, 0))],
            out_specs=pl.BlockSpec((None, tr, ccols), lambda i, i_ref: (i_ref[1], i, 0))),
        compiler_params=pltpu.CompilerParams(dimension_semantics=("arbitrary",)),
    )(idx, p, r2)


def _adam_math(w, g, m, v):
    m2 = ADAM_B1 * m + (1.0 - ADAM_B1) * g
    v2 = ADAM_B2 * v + (1.0 - ADAM_B2) * (g * g)
    m_hat = m2 / (1.0 - ADAM_B1 ** ADAM_STEP)
    v_hat = v2 / (1.0 - ADAM_B2 ** ADAM_STEP)
    delta = -ADAM_LR * (m_hat / (jnp.sqrt(v_hat) + ADAM_EPS) + ADAM_WD * w)
    return delta, m2, v2


def _adamw_layers(w, m, v, gs, name):
    nl, r, ccols = w.shape
    ng = len(gs)
    tr = _row_tile(r, ccols, 4, budget=1024 * 1024)
    nt = r // tr

    def body(w_ref, m_ref, v_ref, *rest):
        g_refs, (go_ref, d_ref, mo_ref, vo_ref) = rest[:ng], rest[ng:]
        l = pl.program_id(0)
        g = g_refs[0][...]
        for k in range(1, ng):
            g = jnp.where(l == k, g_refs[k][...], g)
        delta, m2, v2 = _adam_math(w_ref[...], g, m_ref[...], v_ref[...])
        go_ref[...] = g
        d_ref[...] = delta
        mo_ref[...] = m2
        vo_ref[...] = v2

    big = pl.BlockSpec((None, tr, ccols), lambda l, i: (l, i, 0))

    def gspec(k):
        return pl.BlockSpec((tr, ccols), lambda l, i: (jnp.where(l == k, i, jnp.where(l < k, 0, nt - 1)), 0))

    assert ng == nl
    return _grid_call(body, name, (nl, nt), in_specs=[big, big, big] + [gspec(k) for k in range(ng)],
                      out_specs=[big, big, big, big], out_shape=[jax.ShapeDtypeStruct(w.shape, F32)] * 4,
                      scratch_shapes=[], args=(w, m, v, *gs))[0]


def _adamw_flat(w, g, m, v, name):
    r, ccols = w.shape

    def body(w_ref, g_ref, m_ref, v_ref, d_ref, mo_ref, vo_ref):
        delta, m2, v2 = _adam_math(w_ref[...], g_ref[...], m_ref[...], v_ref[...])
        d_ref[...] = delta
        mo_ref[...] = m2
        vo_ref[...] = v2

    return pl.pallas_call(
        body, name=name, out_shape=[jax.ShapeDtypeStruct((r, ccols), F32)] * 3,
        in_specs=[_VMEM] * 4, out_specs=[_VMEM] * 3,
        compiler_params=pltpu.CompilerParams(vmem_limit_bytes=VMEM_LIMIT),
    )(w, g, m, v)


def _ada_forward(c_all, ada_w, ada_b_cols, name):
    nl, d, ncols = ada_w.shape
    bg = c_all.shape[0]
    tn = 512 if ncols % 512 == 0 else ncols

    def body(c_ref, w_ref, b_ref, o_ref):
        cv = c_ref[...]
        ca = (cv * _sigmoid(cv)).astype(BF16)
        o_ref[...] = jnp.dot(ca, w_ref[...].astype(BF16), preferred_element_type=F32) + b_ref[...]

    return pl.pallas_call(
        body, name=name, out_shape=jax.ShapeDtypeStruct((nl, bg, ncols), F32),
        grid=(nl, ncols // tn),
        in_specs=[pl.BlockSpec((bg, d), lambda l, j: (0, 0)),
                  pl.BlockSpec((None, d, tn), lambda l, j: (l, 0, j)),
                  pl.BlockSpec((None, 1, tn), lambda l, j: (l, 0, j))],
        out_specs=pl.BlockSpec((None, bg, tn), lambda l, j: (l, 0, j)),
        compiler_params=pltpu.CompilerParams(dimension_semantics=("arbitrary", "arbitrary")),
    )(c_all, ada_w, ada_b_cols)


def _ada_update(c_all, dmod_cols, w, m, v, name, exchange=None):
    nl, d, ncols = w.shape
    bg = c_all.shape[0]
    tn = 512 if ncols % 512 == 0 else ncols

    def body(c_ref, dm_ref, w_ref, m_ref, v_ref, go_ref, d_ref, mo_ref, vo_ref):
        cv = c_ref[...]
        ca = (cv * _sigmoid(cv)).astype(BF16)
        g = lax.dot_general(ca, dm_ref[...].astype(BF16), (((0,), (0,)), ((), ())), preferred_element_type=F32)
        delta, m2, v2 = _adam_math(w_ref[...], g, m_ref[...], v_ref[...])
        go_ref[...] = g
        d_ref[...] = delta
        mo_ref[...] = m2
        vo_ref[...] = v2

    big = pl.BlockSpec((None, d, tn), lambda l, j: (l, 0, j))
    return _grid_call(
        body, name, (nl, ncols // tn),
        in_specs=[pl.BlockSpec((bg, d), lambda l, j: (0, 0)),
                  pl.BlockSpec((None, bg, tn), lambda l, j: (l, 0, j)), big, big, big],
        out_specs=[big, big, big, big], out_shape=[jax.ShapeDtypeStruct(w.shape, F32)] * 4,
        scratch_shapes=[], args=(c_all, dmod_cols, w, m, v), exchange=exchange)


def _load_weights(first, pairs, sems):
    @pl.when(first)
    def _():
        cps = [pltpu.make_async_copy(src, dst, sems.at[k]) for k, (src, dst) in enumerate(pairs)]
        for cp in cps:
            cp.start()
        for cp in cps:
            cp.wait()


def _ada_norm(xv, g, sc, sh):
    r = lax.rsqrt(jnp.mean(xv * xv, axis=-1, keepdims=True) + EPS)
    xn = xv * r
    return (xn * g) * (1.0 + sc) + sh, xn, r


def _ada_norm_bwd(dh, xn, r, g, sc):
    d_sh = _colsum(dh)
    d_sc = _colsum(dh * (xn * g))
    dxg = dh * (1.0 + sc)
    d_g = _colsum(dxg * xn)
    gd = dxg * g
    dx = r * (gd - xn * jnp.mean(gd * xn, axis=-1, keepdims=True))
    return dx, d_sh, d_sc, d_g


def _gated_residual_bwd(dxo, o, g_post, gt):
    r = lax.rsqrt(jnp.mean(o * o, axis=-1, keepdims=True) + EPS)
    on = o * r
    d_gt = _colsum(dxo * (on * g_post))
    dy = dxo * (1.0 + gt)
    d_gp = _colsum(dy * on)
    gd = dy * g_post
    do = r * (gd - on * jnp.mean(gd * on, axis=-1, keepdims=True))
    return do, d_gt, d_gp


def _seq_positions(i, tm, width):
    return i * tm + lax.broadcasted_iota(jnp.int32, (tm, width), 0)


def _fill_phases(ext, phases):
    rows = ext.shape[0]
    ev = ext[...]
    for r in range(1, SUBLANES):
        phases[r - 1] = pltpu.roll(ev, rows - r, axis=0)


def _shifted_rows(ext, phases, offset, n):
    q, r = divmod(offset, SUBLANES)
    if r == 0:
        return ext[pl.ds(q * SUBLANES, n), :]
    return phases[r - 1, pl.ds(q * SUBLANES, n), :]


def _rows_before(halo, cur, shift):
    e = jnp.concatenate([halo, cur], axis=0)
    return pltpu.roll(e, shift, axis=0)[halo.shape[0]:, :]


def _rows_after(cur, halo, shift):
    e = jnp.concatenate([cur, halo], axis=0)
    return pltpu.roll(e, e.shape[0] - shift, axis=0)[:cur.shape[0], :]


def _mixer_forward(x, mod, vec_d, vec_c, cw, pw, win_g, wout_g, taps, tm, name, exchange=None):
    nb, s, d = x.shape
    n = s // tm
    nj, _, dcol = win_g.shape
    din = nj * dcol
    dc = vec_c.shape[-1]
    dpool = din - 2 * dc
    dmix = dc + dpool
    ro = wout_g.shape[1]
    ngrp = dpool // LANES

    def body(x_ref, mod_ref, vd_ref, vc_ref, cw_ref, pw_ref, win_hbm, wout_hbm,
             xo_ref, h_ref, u_ref, ac_ref, dp_ref, z_ref, o_ref,
             win_v, wout_v, ext_a, ext_p, phases, sems):
        b, i = pl.program_id(0), pl.program_id(1)
        pairs = [(win_hbm.at[j], win_v.at[:, pl.ds(j * dcol, dcol)]) for j in range(nj)]
        pairs += [(wout_hbm.at[j], wout_v.at[pl.ds(j * ro, ro), :]) for j in range(nj)]
        _load_weights((b == 0) & (i == 0), pairs, sems)

        xv = x_ref[...]
        h, _, _ = _ada_norm(xv, vd_ref[0:1, :], mod_ref[1:2, :], mod_ref[0:1, :])
        hb = h.astype(BF16)
        h_ref[...] = hb
        u = jnp.dot(hb, win_v[...], preferred_element_type=F32)
        u_ref[...] = u.astype(BF16)
        ag = u[:, :dc] * _sigmoid(u[:, dc:2 * dc])
        up = u[:, 2 * dc:]

        @pl.when(i == 0)
        def _():
            ext_a[0:HALO, :] = jnp.zeros((HALO, dc), F32)
            ext_p[0:HALO, :] = jnp.zeros((HALO, dpool), F32)

        @pl.when(i > 0)
        def _():
            ext_a[0:HALO, :] = ext_a[tm:tm + HALO, :]
            ext_p[0:HALO, :] = ext_p[tm:tm + HALO, :]

        ext_a[HALO:HALO + tm, :] = ag
        ext_p[HALO:HALO + tm, :] = up

        acc = jnp.broadcast_to(vc_ref[0:1, :], (tm, dc))
        _fill_phases(ext_a, phases)
        for k in range(taps):
            acc = acc + cw_ref[k:k + 1, :] * _shifted_rows(ext_a, phases, HALO - (taps - 1) + k, tm)
        ac_ref[...] = acc.astype(BF16)
        mu = jnp.mean(acc, axis=-1, keepdims=True)
        xc = acc - mu
        var = jnp.mean(xc * xc, axis=-1, keepdims=True)
        al = (xc * lax.rsqrt(var + EPS)) * vc_ref[1:2, :] + vc_ref[2:3, :]
        a = al * _sigmoid(al)

        pos = _seq_positions(i, tm, LANES)
        parts = [a.astype(BF16)]
        for g in range(ngrp):
            w = POOL_WINDOWS[g]
            cols = slice(g * LANES, (g + 1) * LANES)
            sw = ext_p[:, cols]
            step = 1
            while step < w:
                sw = sw + pltpu.roll(sw, step, axis=0)
                step *= 2
            cnt = jnp.minimum(pos + 1, w).astype(F32)
            dg = (sw[HALO:, :] / cnt - up[:, cols]).astype(BF16)
            dp_ref[:, cols] = dg
            q = jnp.dot(dg, pw_ref[g], preferred_element_type=F32)
            parts.append((q * vc_ref[3:4, cols]).astype(BF16))
        z = jnp.concatenate(parts, axis=-1)
        z_ref[...] = z
        o = jnp.dot(z, wout_v[...], preferred_element_type=F32)
        o_ref[...] = o
        r2 = lax.rsqrt(jnp.mean(o * o, axis=-1, keepdims=True) + EPS)
        xo_ref[...] = xv + (1.0 + mod_ref[2:3, :]) * ((o * r2) * vd_ref[1:2, :])

    def tile(width):
        return pl.BlockSpec((None, tm, width), lambda b, i: (b, i, 0))

    return _grid_call(
        body, name, (nb, n),
        in_specs=[tile(d), pl.BlockSpec((None, 8, d), lambda b, i: (b, 0, 0)), _full(vec_d.shape), _full(vec_c.shape),
                  _full(cw.shape), _full(pw.shape), _ANY, _ANY],
        out_specs=[tile(d), tile(d), tile(din), tile(dc), tile(dpool), tile(dmix), tile(d)],
        out_shape=[jax.ShapeDtypeStruct((nb, s, d), F32), jax.ShapeDtypeStruct((nb, s, d), BF16),
                   jax.ShapeDtypeStruct((nb, s, din), BF16), jax.ShapeDtypeStruct((nb, s, dc), BF16),
                   jax.ShapeDtypeStruct((nb, s, dpool), BF16), jax.ShapeDtypeStruct((nb, s, dmix), BF16),
                   jax.ShapeDtypeStruct((nb, s, d), F32)],
        scratch_shapes=[pltpu.VMEM((d, din), BF16), pltpu.VMEM((dmix, d), BF16),
                        pltpu.VMEM((HALO + tm, dc), F32), pltpu.VMEM((HALO + tm, dpool), F32),
                        pltpu.VMEM((SUBLANES - 1, HALO + tm, dc), F32), pltpu.SemaphoreType.DMA((2 * nj,))],
        args=(x, mod, vec_d, vec_c, cw, pw, win_g, wout_g), exchange=exchange)


def _mixer_backward(dxo, x, o, u, ac, dpl, mod, vec_d, vec_c, cw, pw, win_g, wout_g, taps, tm, name, exchange=None):
    nb, s, d = x.shape
    n = s // tm
    nj, _, dcol = win_g.shape
    din = nj * dcol
    dc = vec_c.shape[-1]
    dpool = din - 2 * dc
    dmix = dc + dpool
    ro = wout_g.shape[1]
    ngrp = dpool // LANES
    rext = tm + HALO

    def body(dxo_ref, x_ref, o_ref, u_ref, ac_ref, dp_ref, mod_ref, vd_ref, vc_ref, cw_ref, pw_ref, win_hbm, wout_hbm,
             dx_ref, du_ref, dob_ref, rowd_ref, rowb_ref, rowc_ref, dcw_ref, dpw_ref,
             win_v, wout_v, ext_a, ext_p, phases, sems):
        b, i = pl.program_id(0), pl.program_id(1)
        first = (b == 0) & (i == 0)
        pairs = [(win_hbm.at[j], win_v.at[:, pl.ds(j * dcol, dcol)]) for j in range(nj)]
        pairs += [(wout_hbm.at[j], wout_v.at[pl.ds(j * ro, ro), :]) for j in range(nj)]
        _load_weights(first, pairs, sems)

        @pl.when(first)
        def _():
            rowd_ref[...] = jnp.zeros_like(rowd_ref)
            rowc_ref[...] = jnp.zeros_like(rowc_ref)
            dcw_ref[...] = jnp.zeros_like(dcw_ref)
            dpw_ref[...] = jnp.zeros_like(dpw_ref)

        @pl.when(i == 0)
        def _():
            rowb_ref[...] = jnp.zeros_like(rowb_ref)
            ext_a[tm:rext, :] = jnp.zeros((HALO, dc), F32)
            ext_p[tm:rext, :] = jnp.zeros((HALO, dpool), F32)

        @pl.when(i > 0)
        def _():
            ext_a[tm:rext, :] = ext_a[0:HALO, :]
            ext_p[tm:rext, :] = ext_p[0:HALO, :]

        g_pre, g_post = vd_ref[0:1, :], vd_ref[1:2, :]
        sh, sc, gt = mod_ref[0:1, :], mod_ref[1:2, :], mod_ref[2:3, :]
        do, d_gt, d_gp = _gated_residual_bwd(dxo_ref[...], o_ref[...], g_post, gt)
        dob = do.astype(BF16)
        dob_ref[...] = dob
        dz = lax.dot_general(dob, wout_v[...], (((1,), (1,)), ((), ())), preferred_element_type=F32)

        acv = ac_ref[...].astype(F32)
        mu = jnp.mean(acv, axis=-1, keepdims=True)
        xc = acv - mu
        rstd = lax.rsqrt(jnp.mean(xc * xc, axis=-1, keepdims=True) + EPS)
        an = xc * rstd
        lg = vc_ref[1:2, :]
        al = an * lg + vc_ref[2:3, :]
        sg = _sigmoid(al)
        dal = dz[:, :dc] * (sg * (1.0 + al * (1.0 - sg)))
        d_lg = _colsum(dal * an)
        d_lb = _colsum(dal)
        dan = dal * lg
        dac = rstd * (dan - jnp.mean(dan, axis=-1, keepdims=True) - an * jnp.mean(dan * an, axis=-1, keepdims=True))
        d_cb = _colsum(dac)
        ext_a[0:tm, :] = dac
        uv = u_ref[:, 0:dc].astype(F32)
        sgu = _sigmoid(u_ref[:, dc:2 * dc].astype(F32))
        ag = uv * sgu
        dag = jnp.zeros((tm, dc), F32)
        _fill_phases(ext_a, phases)
        for k in range(taps):
            sl = _shifted_rows(ext_a, phases, taps - 1 - k, tm)
            dag = dag + cw_ref[k:k + 1, :] * sl
            dcw_ref[k:k + 1, :] += _colsum(ag * sl)
        du_ref[:, 0:dc] = (dag * sgu).astype(BF16)
        du_ref[:, dc:2 * dc] = (dag * uv * (sgu * (1.0 - sgu))).astype(BF16)

        pos = _seq_positions(n - 1 - i, tm, LANES)
        d_ps = []
        for g in range(ngrp):
            w = POOL_WINDOWS[g]
            cols = slice(g * LANES, (g + 1) * LANES)
            gcols = slice(dc + g * LANES, dc + (g + 1) * LANES)
            dgb = dp_ref[:, cols]
            q = jnp.dot(dgb, pw_ref[g], preferred_element_type=F32)
            dpg = dz[:, gcols]
            d_ps.append(_colsum(dpg * q))
            dq = (dpg * vc_ref[3:4, cols]).astype(BF16)
            dpw_ref[g] += lax.dot_general(dgb, dq, (((0,), (0,)), ((), ())), preferred_element_type=F32)
            dd = lax.dot_general(dq, pw_ref[g], (((1,), (1,)), ((), ())), preferred_element_type=F32)
            cnt = jnp.minimum(pos + 1, w).astype(F32)
            ext_p[0:tm, cols] = dd / cnt
            sw = ext_p[:, cols]
            step = 1
            while step < w:
                sw = sw + pltpu.roll(sw, rext - step, axis=0)
                step *= 2
            du_ref[:, 2 * dc + g * LANES:2 * dc + (g + 1) * LANES] = (sw[0:tm, :] - dd).astype(BF16)
        rowc_ref[0:1, :] += d_cb
        rowc_ref[1:2, :] += d_lg
        rowc_ref[2:3, :] += d_lb
        rowc_ref[3:4, :] += jnp.concatenate(d_ps, axis=-1)

        dh = lax.dot_general(du_ref[...], win_v[...], (((1,), (1,)), ((), ())), preferred_element_type=F32)
        _, xn, r1 = _ada_norm(x_ref[...], g_pre, sc, sh)
        dxb, d_sh, d_sc, d_g = _ada_norm_bwd(dh, xn, r1, g_pre, sc)
        dx_ref[...] = dxo_ref[...] + dxb
        rowd_ref[0:1, :] += d_g
        rowd_ref[1:2, :] += d_gp
        rowb_ref[0:1, :] += d_sh
        rowb_ref[1:2, :] += d_sc
        rowb_ref[2:3, :] += d_gt

    def tile(width):
        return pl.BlockSpec((None, tm, width), lambda b, i: (b, n - 1 - i, 0))

    return _grid_call(
        body, name, (nb, n),
        in_specs=[tile(d), tile(d), tile(d), tile(din), tile(dc), tile(dpool),
                  pl.BlockSpec((None, 8, d), lambda b, i: (b, 0, 0)), _full(vec_d.shape), _full(vec_c.shape),
                  _full(cw.shape), _full(pw.shape), _ANY, _ANY],
        out_specs=[tile(d), tile(din), tile(d), _full((8, d)), pl.BlockSpec((None, 8, d), lambda b, i: (b, 0, 0)),
                   _full((8, dc)), _full(cw.shape), _full(pw.shape)],
        out_shape=[jax.ShapeDtypeStruct((nb, s, d), F32), jax.ShapeDtypeStruct((nb, s, din), BF16),
                   jax.ShapeDtypeStruct((nb, s, d), BF16), jax.ShapeDtypeStruct((8, d), F32),
                   jax.ShapeDtypeStruct((nb, 8, d), F32), jax.ShapeDtypeStruct((8, dc), F32),
                   jax.ShapeDtypeStruct(cw.shape, F32), jax.ShapeDtypeStruct(pw.shape, F32)],
        scratch_shapes=[pltpu.VMEM((d, din), BF16), pltpu.VMEM((dmix, d), BF16),
                        pltpu.VMEM((rext, dc), F32), pltpu.VMEM((rext, dpool), F32),
                        pltpu.VMEM((SUBLANES - 1, rext, dc), F32), pltpu.SemaphoreType.DMA((2 * nj,))],
        args=(dxo, x, o, u, ac, dpl, mod, vec_d, vec_c, cw, pw, win_g, wout_g), exchange=exchange)


def _ffn_forward(x, mod, vec_d, fw, wup_g, wdn_g, tm, name, exchange=None):
    nb, s, d = x.shape
    n = s // tm
    nj, _, ucol = wup_g.shape
    f2 = nj * ucol
    dff = f2 // 2
    rd = wdn_g.shape[1]
    nq = nj // 2
    cs = dff // nq

    def body(x_ref, mod_ref, vd_ref, fw_ref, wup_hbm, wdn_hbm,
             xo_ref, h_ref, u_ref, uc_ref, hid_ref, o_ref,
             wup_v, wdn_v, prev_u, sems):
        b, i = pl.program_id(0), pl.program_id(1)
        pairs = [(wup_hbm.at[j], wup_v.at[:, pl.ds(j * ucol, ucol)]) for j in range(nj)]
        pairs += [(wdn_hbm.at[j], wdn_v.at[pl.ds(j * rd, rd), :]) for j in range(nj)]
        _load_weights((b == 0) & (i == 0), pairs, sems)

        @pl.when(i == 0)
        def _():
            prev_u[...] = jnp.zeros_like(prev_u)

        xv = x_ref[...]
        h, _, _ = _ada_norm(xv, vd_ref[2:3, :], mod_ref[4:5, :], mod_ref[3:4, :])
        hb = h.astype(BF16)
        h_ref[...] = hb

        def conv(cols):
            uc = jnp.dot(hb, wup_v[:, cols], preferred_element_type=F32)
            u_ref[:, cols] = uc.astype(BF16)
            before = prev_u[:, cols]
            prev_u[:, cols] = uc[tm - FHALO:, :]
            out = (fw_ref[3:4, cols] + fw_ref[2:3, cols] * uc + fw_ref[1:2, cols] * _rows_before(before, uc, 1)
                   + fw_ref[0:1, cols] * _rows_before(before, uc, 2))
            uc_ref[:, cols] = out.astype(BF16)
            return out

        o = jnp.zeros((tm, d), F32)
        for q in range(nq):
            val = conv(pl.ds(q * cs, cs))
            gate = conv(pl.ds(dff + q * cs, cs))
            hid = ((gate * _sigmoid(gate)) * val).astype(BF16)
            hid_ref[:, pl.ds(q * cs, cs)] = hid
            o = o + jnp.dot(hid, wdn_v[pl.ds(q * cs, cs), :], preferred_element_type=F32)
        o_ref[...] = o
        r2 = lax.rsqrt(jnp.mean(o * o, axis=-1, keepdims=True) + EPS)
        xo_ref[...] = xv + (1.0 + mod_ref[5:6, :]) * ((o * r2) * vd_ref[3:4, :])

    def tile(width):
        return pl.BlockSpec((None, tm, width), lambda b, i: (b, i, 0))

    return _grid_call(
        body, name, (nb, n),
        in_specs=[tile(d), pl.BlockSpec((None, 8, d), lambda b, i: (b, 0, 0)), _full(vec_d.shape), _full(fw.shape),
                  _ANY, _ANY],
        out_specs=[tile(d), tile(d), tile(f2), tile(f2), tile(dff), tile(d)],
        out_shape=[jax.ShapeDtypeStruct((nb, s, d), F32), jax.ShapeDtypeStruct((nb, s, d), BF16),
                   jax.ShapeDtypeStruct((nb, s, f2), BF16), jax.ShapeDtypeStruct((nb, s, f2), BF16),
                   jax.ShapeDtypeStruct((nb, s, dff), BF16), jax.ShapeDtypeStruct((nb, s, d), F32)],
        scratch_shapes=[pltpu.VMEM((d, f2), BF16), pltpu.VMEM((dff, d), BF16),
                        pltpu.VMEM((FHALO, f2), F32), pltpu.SemaphoreType.DMA((2 * nj,))],
        args=(x, mod, vec_d, fw, wup_g, wdn_g), exchange=exchange)


def _ffn_backward(dxo, x, o, u, uc, mod, vec_d, fw, wup_g, wdn_g, tm, name, exchange=None):
    nb, s, d = x.shape
    n = s // tm
    nj, _, ucol = wup_g.shape
    f2 = nj * ucol
    dff = f2 // 2
    rd = wdn_g.shape[1]
    nq = nj // 2
    cs = dff // nq

    def body(dxo_ref, x_ref, o_ref, u_ref, uc_ref, mod_ref, vd_ref, fw_ref, wup_hbm, wdn_hbm,
             dx_ref, du_ref, dob_ref, rowd_ref, rowb_ref, dfw_ref,
             wup_v, wdn_v, next_d, sems):
        b, i = pl.program_id(0), pl.program_id(1)
        first = (b == 0) & (i == 0)
        pairs = [(wup_hbm.at[j], wup_v.at[:, pl.ds(j * ucol, ucol)]) for j in range(nj)]
        pairs += [(wdn_hbm.at[j], wdn_v.at[pl.ds(j * rd, rd), :]) for j in range(nj)]
        _load_weights(first, pairs, sems)

        @pl.when(first)
        def _():
            rowd_ref[...] = jnp.zeros_like(rowd_ref)
            dfw_ref[...] = jnp.zeros_like(dfw_ref)

        @pl.when(i == 0)
        def _():
            rowb_ref[...] = jnp.zeros_like(rowb_ref)
            next_d[...] = jnp.zeros_like(next_d)

        g_pre, g_post = vd_ref[2:3, :], vd_ref[3:4, :]
        sh, sc, gt = mod_ref[3:4, :], mod_ref[4:5, :], mod_ref[5:6, :]
        do, d_gt, d_gp = _gated_residual_bwd(dxo_ref[...], o_ref[...], g_post, gt)
        dob = do.astype(BF16)
        dob_ref[...] = dob

        def conv_bwd(cols, duc):
            uc = u_ref[:, cols].astype(F32)
            after = next_d[:, cols]
            next_d[:, cols] = duc[0:FHALO, :]
            d1 = _rows_after(duc, after, 1)
            d2 = _rows_after(duc, after, 2)
            dfw_ref[3:4, cols] += _colsum(duc)
            dfw_ref[2:3, cols] += _colsum(uc * duc)
            dfw_ref[1:2, cols] += _colsum(uc * d1)
            dfw_ref[0:1, cols] += _colsum(uc * d2)
            ob = (fw_ref[2:3, cols] * duc + fw_ref[1:2, cols] * d1 + fw_ref[0:1, cols] * d2).astype(BF16)
            du_ref[:, cols] = ob
            return lax.dot_general(ob, wup_v[:, cols], (((1,), (1,)), ((), ())), preferred_element_type=F32)

        dh = jnp.zeros((tm, d), F32)
        for q in range(nq):
            vcols = pl.ds(q * cs, cs)
            gcols = pl.ds(dff + q * cs, cs)
            dhid = lax.dot_general(dob, wdn_v[vcols, :], (((1,), (1,)), ((), ())), preferred_element_type=F32)
            val = uc_ref[:, vcols].astype(F32)
            gate = uc_ref[:, gcols].astype(F32)
            sg = _sigmoid(gate)
            act = gate * sg
            dval = dhid * act
            dgate = (dhid * val) * (sg + act * (1.0 - sg))
            dh = dh + conv_bwd(vcols, dval)
            dh = dh + conv_bwd(gcols, dgate)

        _, xn, r1 = _ada_norm(x_ref[...], g_pre, sc, sh)
        dxb, d_sh, d_sc, d_g = _ada_norm_bwd(dh, xn, r1, g_pre, sc)
        dx_ref[...] = dxo_ref[...] + dxb
        rowd_ref[2:3, :] += d_g
        rowd_ref[3:4, :] += d_gp
        rowb_ref[3:4, :] += d_sh
        rowb_ref[4:5, :] += d_sc
        rowb_ref[5:6, :] += d_gt

    def tile(width):
        return pl.BlockSpec((None, tm, width), lambda b, i: (b, n - 1 - i, 0))

    return _grid_call(
        body, name, (nb, n),
        in_specs=[tile(d), tile(d), tile(d), tile(f2), tile(f2), pl.BlockSpec((None, 8, d), lambda b, i: (b, 0, 0)),
                  _full(vec_d.shape), _full(fw.shape), _ANY, _ANY],
        out_specs=[tile(d), tile(f2), tile(d), _full((8, d)), pl.BlockSpec((None, 8, d), lambda b, i: (b, 0, 0)),
                   _full(fw.shape)],
        out_shape=[jax.ShapeDtypeStruct((nb, s, d), F32), jax.ShapeDtypeStruct((nb, s, f2), BF16),
                   jax.ShapeDtypeStruct((nb, s, d), BF16), jax.ShapeDtypeStruct((8, d), F32),
                   jax.ShapeDtypeStruct((nb, 8, d), F32), jax.ShapeDtypeStruct(fw.shape, F32)],
        scratch_shapes=[pltpu.VMEM((d, f2), BF16), pltpu.VMEM((dff, d), BF16),
                        pltpu.VMEM((FHALO, f2), F32), pltpu.SemaphoreType.DMA((2 * nj,))],
        args=(dxo, x, o, u, uc, mod, vec_d, fw, wup_g, wdn_g), exchange=exchange)


def _weight_grad(a, b, nblk, split, tt, name, exchange=None):
    t, ka = a.shape
    nb_ = b.shape[1]
    nk = t // tt
    if split == "cols":
        wa, wb, grid = ka, nb_ // nblk, (1, nk)
        a_spec = pl.BlockSpec((tt, ka), lambda j, k: (k, 0))
        b_spec = pl.BlockSpec((tt, nb_), lambda j, k: (k, 0))
        o_spec = pl.BlockSpec((nblk, wa, wb), lambda j, k: (0, 0, 0))
        acc_shape = (ka, nb_)
    elif split == "b":
        wa, wb, grid = ka, nb_ // nblk, (nblk, nk)
        a_spec = pl.BlockSpec((tt, wa), lambda j, k: (k, 0))
        b_spec = pl.BlockSpec((tt, wb), lambda j, k: (k, j))
        o_spec = pl.BlockSpec((None, wa, wb), lambda j, k: (j, 0, 0))
        acc_shape = (wa, wb)
    else:
        wa, wb, grid = ka // nblk, nb_, (nblk, nk)
        a_spec = pl.BlockSpec((tt, wa), lambda j, k: (k, j))
        b_spec = pl.BlockSpec((tt, wb), lambda j, k: (k, 0))
        o_spec = pl.BlockSpec((None, wa, wb), lambda j, k: (j, 0, 0))
        acc_shape = (wa, wb)

    def body(a_ref, b_ref, o_ref, acc):
        k = pl.program_id(1)
        prod = lax.dot_general(a_ref[...], b_ref[...], (((0,), (0,)), ((), ())), preferred_element_type=F32)

        @pl.when(k == 0)
        def _():
            acc[...] = prod

        @pl.when(k > 0)
        def _():
            acc[...] += prod

        @pl.when(k == nk - 1)
        def _():
            if split == "cols":
                for j in range(nblk):
                    o_ref[j] = acc[:, j * wb:(j + 1) * wb].astype(o_ref.dtype)
            else:
                o_ref[...] = acc[...].astype(o_ref.dtype)

    outs, exo = _grid_call(body, name, grid, in_specs=[a_spec, b_spec], out_specs=[o_spec],
                           out_shape=[jax.ShapeDtypeStruct((nblk, wa, wb), BF16)],
                           scratch_shapes=[pltpu.VMEM(acc_shape, F32)], args=(a, b), exchange=exchange)
    return outs[0], exo


def _loss_grad(y, tgt, tm, name):
    nb, s, d = y.shape
    n = s // tm

    def body(y_ref, t_ref, dy_ref, sq_ref):
        @pl.when((pl.program_id(0) == 0) & (pl.program_id(1) == 0))
        def _():
            sq_ref[...] = jnp.zeros_like(sq_ref)

        e = y_ref[...] - t_ref[...]
        dy_ref[...] = e * (1.0 / d)
        sq_ref[0:1, :] += _colsum(e * e)

    tile = pl.BlockSpec((None, tm, d), lambda b, i: (b, i, 0))
    return pl.pallas_call(
        body, name=name, out_shape=[jax.ShapeDtypeStruct((nb, s, d), F32), jax.ShapeDtypeStruct((8, d), F32)],
        grid=(nb, n), in_specs=[tile, tile], out_specs=[tile, _full((8, d))],
        compiler_params=pltpu.CompilerParams(dimension_semantics=("arbitrary", "arbitrary")),
    )(y, tgt)


def _rows128(a):
    return a.reshape(-1, LANES)


class _ReduceScatter:
    def __init__(self, gs, cidx, idx, tag):
        self.gs, self.cidx, self.idx, self.tag = gs, cidx, idx, tag

    def swap(self):
        return _swap_halves(self.gs)

    def after_swap(self, r1):
        self.ps = [_pair_sum(g, r, self.cidx, name=f"rs_pair_{self.tag}_{a}") for a, (g, r) in enumerate(zip(self.gs, r1))]

    def chips(self):
        return _chip_exchange(self.ps)

    def after_chips(self, r2):
        self.fh = [_chip_sum(p, r, self.idx, name=f"rs_sum_{self.tag}_{a}") for a, (p, r) in enumerate(zip(self.ps, r2))]

    def share(self):
        return _sibling_share(self.fh)

    @staticmethod
    def result(fs):
        return [f.reshape(f.shape[0] * f.shape[1], f.shape[2]) for f in fs]


def kernel(x, c, ada_w, ada_b, pre_mix_g, post_mix_g, w_in, conv_w, conv_b, conv_ln_g, conv_ln_b, pool_w, pool_scale, w_out, pre_ffn_g, post_ffn_g, ffn_up, ffn_conv_w, ffn_conv_b, ffn_down, loss_target, m_ada_w, m_ada_b, m_pre_mix_g, m_post_mix_g, m_w_in, m_conv_w, m_conv_b, m_conv_ln_g, m_conv_ln_b, m_pool_w, m_pool_scale, m_w_out, m_pre_ffn_g, m_post_ffn_g, m_ffn_up, m_ffn_conv_w, m_ffn_conv_b, m_ffn_down, v_ada_w, v_ada_b, v_pre_mix_g, v_post_mix_g, v_w_in, v_conv_w, v_conv_b, v_conv_ln_g, v_conv_ln_b, v_pool_w, v_pool_scale, v_w_out, v_pre_ffn_g, v_post_ffn_g, v_ffn_up, v_ffn_conv_w, v_ffn_conv_b, v_ffn_down):
    nb, s, d = x.shape
    nl = w_in.shape[0]
    taps = conv_w.shape[1]
    ccol = conv_w.shape[2]
    dc = conv_b.shape[1]
    fcol = ffn_conv_w.shape[2]
    f2 = ffn_conv_b.shape[1]
    nmod = ada_b.shape[1] // d
    acol = ada_w.shape[2]
    tm = min(256, s)
    tt = min(2048, (nb * s) // 2)

    xi, yi, ci = _pos()
    jm = 2 * xi + yi
    cidx = jnp.reshape(ci, (1,)).astype(jnp.int32)
    idx = jnp.stack([jm, ci]).astype(jnp.int32)

    n_cw, n_fw, n_c = nl * taps * ccol, nl * 3 * fcol, nb * d
    packed = jnp.concatenate([conv_w.reshape(-1), ffn_conv_w.reshape(-1), c.reshape(-1)])
    got = _gather8(_rows128(packed), name="gather_small").reshape(N_DEV, -1)
    chips = got[0::2]
    cw_full = chips[:, :n_cw].reshape(N_CHIPS, nl, taps, ccol).transpose(1, 2, 0, 3).reshape(nl, taps, dc)
    fw_full = chips[:, n_cw:n_cw + n_fw].reshape(N_CHIPS, nl, 3, fcol).transpose(1, 2, 0, 3).reshape(nl, 3, f2)
    c_all = got[:, n_cw + n_fw:].reshape(N_DEV * nb, d)

    ada_b_cols = lax.dynamic_slice_in_dim(ada_b, jm * acol, acol, axis=1).reshape(nl, 1, acol)
    mod_cols = _ada_forward(c_all, ada_w, ada_b_cols, name="ada_forward")
    by_owner = mod_cols.reshape(nl, N_DEV, nb, acol).transpose(1, 0, 2, 3).reshape(N_DEV, -1, LANES)
    mod_own = _rows_to_owners(by_owner, name="mod_to_owners").reshape(N_CHIPS, nl, nb, acol)
    mod_own = mod_own.transpose(1, 2, 0, 3).reshape(nl, nb, nmod, d)
    mod_own = jnp.pad(mod_own, ((0, 0), (0, 0), (0, 8 - nmod), (0, 0)))

    vec_d = jnp.stack([pre_mix_g, post_mix_g, pre_ffn_g, post_ffn_g], axis=1)
    vec_c = jnp.stack([conv_b, conv_ln_g, conv_ln_b, pool_scale], axis=1)
    cw_pad = jnp.pad(cw_full, ((0, 0), (0, HALO - taps), (0, 0)))
    fw_rows = jnp.concatenate([fw_full, ffn_conv_b[:, None, :], jnp.zeros((nl, 4, f2), F32)], axis=1)
    pw_b = pool_w.astype(BF16)

    win_b, wout_b, wup_b, wdn_b = (w.astype(BF16) for w in (w_in, w_out, ffn_up, ffn_down))

    def others(l):
        return [win_b[l], wout_b[l], wdn_b[l]]

    win_g, wout_g, wdn_g = _whole(_run_exchange(_gather(others(0)), name="gather_0"))
    saved = []
    xs = x
    for l in range(nl):
        (x1, h1, u1, ac1, dp1, z1, o1), got = _mixer_forward(
            xs, mod_own[l], vec_d[l], vec_c[l], cw_pad[l], pw_b[l], win_g, wout_g, taps, tm, name=f"mixer_fwd_{l}",
            exchange=_gather([wup_b[l]], mid_at=0.85))
        wup_g, = _whole(got)
        (x2, h2, u2, uc2, hid2, o2), nxt = _ffn_forward(
            x1, mod_own[l], vec_d[l], fw_rows[l], wup_g, wdn_g, tm, name=f"ffn_fwd_{l}",
            exchange=_gather(others(l + 1), mid_at=0.6) if l + 1 < nl else None)
        saved.append((xs, h1, u1, ac1, dp1, z1, o1, x1, h2, u2, uc2, hid2, o2, win_g, wout_g, wup_g, wdn_g))
        if l + 1 < nl:
            win_g, wout_g, wdn_g = _whole(nxt)
        xs = x2

    dx, sq = _loss_grad(xs, loss_target, tm, name="loss_grad")
    loss = lax.psum(0.5 * jnp.sum(sq) / d, ("x", "y", "c"))

    flat = lambda a: a.reshape(nb * s, a.shape[-1])
    small = [None] * nl
    big_mlp, big_mix = [None] * nl, [None] * nl
    mlp = mix = None
    for l in reversed(range(nl)):
        x0, h1, u1, ac1, dp1, z1, o1, x1, h2, u2, uc2, hid2, o2, win_g, wout_g, wup_g, wdn_g = saved[l]
        (dx, du2, do2, rowd2, rowb2, dfw), got = _ffn_backward(
            dx, x1, o2, u2, uc2, mod_own[l], vec_d[l], fw_rows[l], wup_g, wdn_g, tm, name=f"ffn_bwd_{l}",
            exchange=_combine([mlp.chips(), mix.swap()]) if mlp else None)
        if mlp:
            mlp.after_chips(got[:2])
            mix.after_swap(got[2:])
        g_up, got = _weight_grad(flat(h2), flat(du2), N_CHIPS, "b", tt, name=f"grad_ffn_up_{l}",
                                 exchange=_combine([mlp.share(), mix.chips()]) if mlp else None)
        if mlp:
            big_mlp[l + 1] = mlp.result(got[:2])
            mix.after_chips(got[2:])
        g_dn, got = _weight_grad(flat(hid2), flat(do2), 2, "a", tt, name=f"grad_ffn_down_{l}",
                                 exchange=mix.share() if mix else None)
        if mix:
            big_mix[l + 1] = mix.result(got)
        mlp = _ReduceScatter([g_up, g_dn.reshape(N_CHIPS, -1, d)], cidx, idx, f"mlp_{l}")
        if l == 0:
            mlp.after_swap(_run_exchange(mlp.swap(), name="rs_swap_mlp_0"))
        (dx, du1, do1, rowd1, rowb1, rowc, dcw, dpw), got = _mixer_backward(
            dx, x0, o1, u1, ac1, dp1, mod_own[l], vec_d[l], vec_c[l], cw_pad[l], pw_b[l], win_g, wout_g, taps, tm,
            name=f"mixer_bwd_{l}", exchange=mlp.swap() if l > 0 else mlp.chips())
        if l > 0:
            mlp.after_swap(got)
        else:
            mlp.after_chips(got)
        g_in, got = _weight_grad(flat(h1), flat(du1), N_CHIPS, "cols", tt, name=f"grad_w_in_{l}",
                                 exchange=mlp.share() if l == 0 else None)
        if l == 0:
            big_mlp[0] = mlp.result(got)
        g_out, _ = _weight_grad(flat(z1), flat(do1), 1, "cols", tt, name=f"grad_w_out_{l}")
        mix = _ReduceScatter([g_in, g_out.reshape(N_CHIPS, -1, d)], cidx, idx, f"mix_{l}")
        small[l] = dict(rowd=rowd1 + rowd2, rowb=rowb1 + rowb2, rowc=rowc, dcw=dcw[:taps], dpw=dpw, dfw=dfw)
    mix.after_swap(_run_exchange(mix.swap(), name="rs_swap_mix_0"))
    mix.after_chips(_run_exchange(mix.chips(), name="rs_chips_mix_0"))
    big_mix[0] = mix.result(_run_exchange(mix.share(), name="rs_share_mix_0"))

    dmod_own = jnp.stack([small[l]["rowb"][:, :nmod, :] for l in range(nl)])
    dmod_all = _gather8(_rows128(dmod_own), name="gather_dmod").reshape(N_DEV, nl, nb, nmod * d)
    dmod_all = dmod_all.transpose(1, 0, 2, 3).reshape(nl, N_DEV * nb, nmod * d)
    dmod_cols = lax.dynamic_slice_in_dim(dmod_all, jm * acol, acol, axis=2)
    (g_ada_w, d_ada_w, nm_ada_w, nv_ada_w), _ = _ada_update(c_all, dmod_cols, ada_w, m_ada_w, v_ada_w, name="ada_update")

    def st(key, row=None):
        return jnp.stack([small[l][key] if row is None else small[l][key][row] for l in range(nl)])

    local = {
        "ada_b": dmod_own.sum(axis=1).reshape(nl, nmod * d),
        "pre_mix_g": st("rowd", 0), "post_mix_g": st("rowd", 1),
        "conv_b": st("rowc", 0), "conv_ln_g": st("rowc", 1), "conv_ln_b": st("rowc", 2),
        "pool_w": st("dpw"), "pool_scale": st("rowc", 3),
        "pre_ffn_g": st("rowd", 2), "post_ffn_g": st("rowd", 3),
        "ffn_conv_b": st("dfw", 3), "conv_w": st("dcw"), "ffn_conv_w": jnp.stack([small[l]["dfw"][:3] for l in range(nl)]),
    }
    names = list(local)
    sizes = [local[k].size for k in names]
    pad = -sum(sizes) % (2 * SUBLANES * LANES)
    packed = jnp.concatenate([local[k].reshape(-1) for k in names] + [jnp.zeros((pad,), F32)])
    summed = _allreduce8(_rows128(packed), name="allreduce_small").reshape(-1)
    grads, off = {}, 0
    for k, sz in zip(names, sizes):
        grads[k] = summed[off:off + sz].reshape(local[k].shape)
        off += sz
    grads["conv_w"] = lax.dynamic_slice_in_dim(grads["conv_w"], jm * ccol, ccol, axis=2)
    grads["ffn_conv_w"] = lax.dynamic_slice_in_dim(grads["ffn_conv_w"], jm * fcol, fcol, axis=2)

    params = dict(ada_b=(ada_b, m_ada_b, v_ada_b), pre_mix_g=(pre_mix_g, m_pre_mix_g, v_pre_mix_g),
                  post_mix_g=(post_mix_g, m_post_mix_g, v_post_mix_g), conv_b=(conv_b, m_conv_b, v_conv_b),
                  conv_ln_g=(conv_ln_g, m_conv_ln_g, v_conv_ln_g), conv_ln_b=(conv_ln_b, m_conv_ln_b, v_conv_ln_b),
                  pool_w=(pool_w, m_pool_w, v_pool_w), pool_scale=(pool_scale, m_pool_scale, v_pool_scale),
                  pre_ffn_g=(pre_ffn_g, m_pre_ffn_g, v_pre_ffn_g), post_ffn_g=(post_ffn_g, m_post_ffn_g, v_post_ffn_g),
                  ffn_conv_b=(ffn_conv_b, m_ffn_conv_b, v_ffn_conv_b), conv_w=(conv_w, m_conv_w, v_conv_w),
                  ffn_conv_w=(ffn_conv_w, m_ffn_conv_w, v_ffn_conv_w))
    pack = lambda i, g=None: _rows128(jnp.concatenate([(grads[k] if g else params[k][i]).reshape(-1) for k in names]))
    sd, sm, sv = _adamw_flat(pack(0), pack(0, True), pack(1), pack(2), name="adamw_small")
    outs = {}
    off = 0
    for k in names:
        shape, sz = params[k][0].shape, params[k][0].size
        outs[k] = (grads[k],) + tuple(a.reshape(-1)[off:off + sz].reshape(shape) for a in (sd, sm, sv))
        off += sz

    outs["ada_w"] = (g_ada_w, d_ada_w, nm_ada_w, nv_ada_w)
    for k, w, m, v, gs in [("w_in", w_in, m_w_in, v_w_in, [big_mix[l][0] for l in range(nl)]),
                           ("w_out", w_out, m_w_out, v_w_out, [big_mix[l][1] for l in range(nl)]),
                           ("ffn_up", ffn_up, m_ffn_up, v_ffn_up, [big_mlp[l][0] for l in range(nl)]),
                           ("ffn_down", ffn_down, m_ffn_down, v_ffn_down, [big_mlp[l][1] for l in range(nl)])]:
        outs[k] = tuple(_adamw_layers(w, m, v, gs, name=f"adamw_{k}"))

    order = ["ada_w", "ada_b", "pre_mix_g", "post_mix_g", "w_in", "conv_w", "conv_b", "conv_ln_g", "conv_ln_b", "pool_w",
             "pool_scale", "w_out", "pre_ffn_g", "post_ffn_g", "ffn_up", "ffn_conv_w", "ffn_conv_b", "ffn_down"]
    return (loss, dx) + tuple(outs[k][i] for i in range(4) for k in order)
```

```python
import functools

import jax
import jax.numpy as jnp
from jax import lax
from jax.experimental import pallas as pl
from jax.experimental.pallas import tpu as pltpu

F32 = jnp.float32
BF16 = jnp.bfloat16
MESH = pl.DeviceIdType.MESH

EPS = 1e-6
POOL_WINDOWS = (2, 4, 8, 16)
ADAM_LR = 0.001
ADAM_B1 = 0.9
ADAM_B2 = 0.999
ADAM_EPS = 1e-08
ADAM_WD = 0.01
ADAM_STEP = 10

N_CHIPS = 4
N_DEV = 8
LANES = 128
SUBLANES = 8
HALO = 32
FHALO = 8
VMEM_LIMIT = 60 * 1024 * 1024


def _pos():
    return lax.axis_index("x"), lax.axis_index("y"), lax.axis_index("c")


def _flip(v, f):
    return 1 - v if f else v


def _full(shape):
    nd = len(shape)
    return pl.BlockSpec(shape, lambda *_: (0,) * nd)


_ANY = pl.BlockSpec(memory_space=pl.ANY)
_VMEM = pl.BlockSpec(memory_space=pltpu.VMEM)


def _sigmoid(v):
    return 1.0 / (1.0 + jnp.exp(-v))


def _colsum(v):
    return jnp.sum(v, axis=0, keepdims=True)


def _gather8(v, name):
    r, ccols = v.shape

    def body(v_ref, out_ref, send_sems, recv_sems, local_sem):
        x, y, c = _pos()
        me = 4 * x + 2 * y + c
        mine = pltpu.make_async_copy(v_ref, out_ref.at[me], local_sem)
        mine.start()
        peers = [(_flip(x, (k >> 2) & 1), _flip(y, (k >> 1) & 1), _flip(c, k & 1)) for k in range(1, N_DEV)]
        sends = []
        for k, peer in enumerate(peers):
            cp = pltpu.make_async_remote_copy(src_ref=v_ref, dst_ref=out_ref.at[me], send_sem=send_sems.at[k],
                                              recv_sem=recv_sems.at[k], device_id=peer, device_id_type=MESH)
            cp.start()
            sends.append(cp)
        for k, peer in enumerate(peers):
            pidx = 4 * peer[0] + 2 * peer[1] + peer[2]
            pltpu.make_async_remote_copy(src_ref=v_ref, dst_ref=out_ref.at[pidx], send_sem=send_sems.at[k],
                                         recv_sem=recv_sems.at[k], device_id=peer, device_id_type=MESH).wait_recv()
        for cp in sends:
            cp.wait_send()
        mine.wait()

    return pl.pallas_call(
        body, name=name, out_shape=jax.ShapeDtypeStruct((N_DEV, r, ccols), v.dtype),
        in_specs=[_VMEM], out_specs=_VMEM,
        scratch_shapes=[pltpu.SemaphoreType.DMA((N_DEV - 1,)), pltpu.SemaphoreType.DMA((N_DEV - 1,)),
                        pltpu.SemaphoreType.DMA(())],
        compiler_params=pltpu.CompilerParams(vmem_limit_bytes=VMEM_LIMIT),
    )(v)


def _rows_to_owners(v, name):
    _, r, ccols = v.shape

    def body(v_ref, out_ref, send_sems, recv_sems, local_sem):
        x, y, c = _pos()
        jm = 2 * x + y
        mine = pltpu.make_async_copy(v_ref.at[2 * jm + c], out_ref.at[jm], local_sem)
        mine.start()
        peers, pjs = _chip_peers(x, y, c)
        sends = []
        for k, peer in enumerate(peers):
            cp = pltpu.make_async_remote_copy(src_ref=v_ref.at[2 * pjs[k] + c], dst_ref=out_ref.at[jm],
                                              send_sem=send_sems.at[k], recv_sem=recv_sems.at[k],
                                              device_id=peer, device_id_type=MESH)
            cp.start()
            sends.append(cp)
        for k, peer in enumerate(peers):
            pltpu.make_async_remote_copy(src_ref=v_ref.at[0], dst_ref=out_ref.at[pjs[k]], send_sem=send_sems.at[k],
                                         recv_sem=recv_sems.at[k], device_id=peer, device_id_type=MESH).wait_recv()
        for cp in sends:
            cp.wait_send()
        mine.wait()

    return pl.pallas_call(
        body, name=name, out_shape=jax.ShapeDtypeStruct((N_CHIPS, r, ccols), v.dtype),
        in_specs=[_VMEM], out_specs=_VMEM,
        scratch_shapes=[pltpu.SemaphoreType.DMA((N_CHIPS - 1,)), pltpu.SemaphoreType.DMA((N_CHIPS - 1,)),
                        pltpu.SemaphoreType.DMA(())],
        compiler_params=pltpu.CompilerParams(vmem_limit_bytes=VMEM_LIMIT),
    )(v)


def _allreduce8(v, name):
    r, ccols = v.shape
    h = r // 2

    def body(v_ref, out_ref, whole, half, send_sems, recv_sems):
        x, y, c = _pos()
        sib = (x, y, 1 - c)
        mine = pl.ds(pl.multiple_of(c * h, SUBLANES), h)
        theirs = pl.ds(pl.multiple_of((1 - c) * h, SUBLANES), h)

        def exchange(src, dst, k, peer):
            cp = pltpu.make_async_remote_copy(src_ref=src, dst_ref=dst, send_sem=send_sems.at[k],
                                              recv_sem=recv_sems.at[k], device_id=peer, device_id_type=MESH)
            cp.start()
            cp.wait()

        exchange(v_ref, whole, 0, sib)
        out_ref[...] = v_ref[...] + whole[...]
        for k, peer in ((1, (1 - x, y, c)), (2, (x, 1 - y, c))):
            exchange(out_ref.at[mine], half.at[k - 1], k, peer)
            out_ref[mine, :] = out_ref[mine, :] + half[k - 1]
        exchange(out_ref.at[mine], half.at[2], 3, sib)
        out_ref[theirs, :] = half[2]

    return pl.pallas_call(
        body, name=name, out_shape=jax.ShapeDtypeStruct((r, ccols), v.dtype),
        in_specs=[_VMEM], out_specs=_VMEM,
        scratch_shapes=[pltpu.VMEM((r, ccols), v.dtype), pltpu.VMEM((3, h, ccols), v.dtype),
                        pltpu.SemaphoreType.DMA((4,)), pltpu.SemaphoreType.DMA((4,))],
        compiler_params=pltpu.CompilerParams(vmem_limit_bytes=VMEM_LIMIT),
    )(v)


def _chip_peers(x, y, c):
    peers = [(_flip(x, (k >> 1) & 1), _flip(y, k & 1), c) for k in range(1, N_CHIPS)]
    return peers, [2 * p[0] + p[1] for p in peers]


class _Exchange:
    def __init__(self, ins, outs, aliases, n_sems, n_local, start, finish, mid=None, mid_at=1.0):
        self.ins, self.outs, self.aliases = list(ins), list(outs), dict(aliases)
        self.n_sems, self.n_local, self.start, self.finish = n_sems, n_local, start, finish
        self.mid, self.mid_at = mid, mid_at

    def scratch(self):
        return [pltpu.SemaphoreType.DMA((self.n_sems,)), pltpu.SemaphoreType.DMA((self.n_sems,)),
                pltpu.SemaphoreType.DMA((max(self.n_local, 1),))]


class _Sems:
    def __init__(self, send, recv, local, base=0, lbase=0):
        self.send, self.recv, self.loc, self.base, self.lbase = send, recv, local, base, lbase

    def shifted(self, by, lby):
        return _Sems(self.send, self.recv, self.loc, self.base + by, self.lbase + lby)

    def local(self, k):
        return self.loc.at[self.lbase + k]


def _remote(src, dst, sems, k, peer):
    return pltpu.make_async_remote_copy(src_ref=src, dst_ref=dst, send_sem=sems.send.at[sems.base + k],
                                        recv_sem=sems.recv.at[sems.base + k], device_id=peer, device_id_type=MESH)


def _combine(exs):
    ins = [a for ex in exs for a in ex.ins]
    outs = [o for ex in exs for o in ex.outs]
    aliases, spans, ni, no, ns, nloc = {}, [], 0, 0, 0, 0
    for ex in exs:
        aliases.update({ni + a: no + b for a, b in ex.aliases.items()})
        spans.append((ni, no, ns, nloc))
        ni, no, ns, nloc = ni + len(ex.ins), no + len(ex.outs), ns + ex.n_sems, nloc + ex.n_local

    def each(which):
        def run(ins_, outs_, sems):
            for ex, (i0, o0, s0, l0) in zip(exs, spans):
                stage = getattr(ex, which)
                if stage is not None:
                    stage(ins_[i0:i0 + len(ex.ins)], outs_[o0:o0 + len(ex.outs)], sems.shifted(s0, l0))
        return run

    mids = [ex.mid_at for ex in exs if ex.mid is not None]
    return _Exchange(ins, outs, aliases, ns, nloc, each("start"), each("finish"),
                     mid=each("mid") if mids else None, mid_at=max(mids) if mids else 1.0)


def _gather(shards, mid_at=1.0):
    n = len(shards)
    per = N_CHIPS - 1
    halves = [s.reshape(2, s.shape[0] // 2, s.shape[1]) for s in shards]

    def copies(ins, outs, sems):
        x, y, c = _pos()
        jm = 2 * x + y
        sib = (x, y, 1 - c)
        peers, pjs = _chip_peers(x, y, c)
        sends, recvs, passes, passed = [], [], [], []
        for a in range(n):
            for k, peer in enumerate(peers):
                landed, theirs = outs[a].at[pjs[k], c], outs[a].at[pjs[k], 1 - c]
                sends.append(_remote(ins[a].at[c], outs[a].at[jm, c], sems, 2 * (a * per + k), peer))
                recvs.append(_remote(landed, landed, sems, 2 * (a * per + k), peer))
                passes.append(_remote(landed, landed, sems, 2 * (a * per + k) + 1, sib))
                passed.append(_remote(theirs, theirs, sems, 2 * (a * per + k) + 1, sib))
        return sends, recvs, passes, passed

    def local(ins, outs, sems):
        x, y, _ = _pos()
        return [pltpu.make_async_copy(ins[a], outs[a].at[2 * x + y], sems.local(a)) for a in range(n)]

    def start(ins, outs, sems):
        for cp in local(ins, outs, sems) + copies(ins, outs, sems)[0]:
            cp.start()

    def mid(ins, outs, sems):
        _, recvs, passes, _ = copies(ins, outs, sems)
        for got, fwd in zip(recvs, passes):
            got.wait_recv()
            fwd.start()

    def finish(ins, outs, sems):
        sends, _, passes, passed = copies(ins, outs, sems)
        for cp in passed:
            cp.wait_recv()
        for cp in sends + passes:
            cp.wait_send()
        for cp in local(ins, outs, sems):
            cp.wait()

    outs = [jax.ShapeDtypeStruct((N_CHIPS,) + h.shape, h.dtype) for h in halves]
    return _Exchange(halves, outs, {}, 2 * n * per, n, start, finish, mid=mid, mid_at=mid_at)


def _whole(gathered):
    return [g.reshape(g.shape[0], g.shape[1] * g.shape[2], g.shape[3]) for g in gathered]


def _swap_halves(gs):
    n = len(gs)
    halves = [g.reshape(g.shape[0], 2, g.shape[1] // 2, g.shape[2]) for g in gs]

    def copies(ins, outs, sems):
        x, y, c = _pos()
        sib = (x, y, 1 - c)
        return [_remote(ins[a].at[:, 1 - c], outs[a], sems, a, sib) for a in range(n)]

    def start(ins, outs, sems):
        for cp in copies(ins, outs, sems):
            cp.start()

    def finish(ins, outs, sems):
        for cp in copies(ins, outs, sems):
            cp.wait()

    outs = [jax.ShapeDtypeStruct((g.shape[0], g.shape[1] // 2, g.shape[2]), g.dtype) for g in gs]
    return _Exchange(halves, outs, {}, n, 0, start, finish)


def _chip_exchange(ps):
    n = len(ps)
    per = N_CHIPS - 1

    def copies(ins, outs, sems):
        x, y, c = _pos()
        peers, pjs = _chip_peers(x, y, c)
        return [_remote(ins[a].at[pjs[k]], outs[a].at[k], sems, a * per + k, peer)
                for a in range(n) for k, peer in enumerate(peers)]

    def start(ins, outs, sems):
        for cp in copies(ins, outs, sems):
            cp.start()

    def finish(ins, outs, sems):
        for cp in copies(ins, outs, sems):
            cp.wait()

    outs = [jax.ShapeDtypeStruct((per,) + p.shape[1:], p.dtype) for p in ps]
    return _Exchange(ps, outs, {}, n * per, 0, start, finish)


def _sibling_share(fs):
    n = len(fs)

    def copies(outs, sems):
        x, y, c = _pos()
        sib = (x, y, 1 - c)
        sends = [_remote(outs[a].at[c], outs[a].at[c], sems, a, sib) for a in range(n)]
        recvs = [_remote(outs[a].at[1 - c], outs[a].at[1 - c], sems, a, sib) for a in range(n)]
        return sends, recvs

    def start(ins, outs, sems):
        for cp in copies(outs, sems)[0]:
            cp.start()

    def finish(ins, outs, sems):
        sends, recvs = copies(outs, sems)
        for cp in recvs:
            cp.wait_recv()
        for cp in sends:
            cp.wait_send()

    outs = [jax.ShapeDtypeStruct(f.shape, f.dtype) for f in fs]
    return _Exchange(fs, outs, {a: a for a in range(n)}, n, 0, start, finish)


def _run_exchange(ex, name):
    ni, no = len(ex.ins), len(ex.outs)

    def body(*refs):
        ins, outs, sems = refs[:ni], refs[ni:ni + no], _Sems(*refs[ni + no:])
        ex.start(ins, outs, sems)
        if ex.mid is not None:
            ex.mid(ins, outs, sems)
        ex.finish(ins, outs, sems)

    return pl.pallas_call(
        body, name=name, out_shape=ex.outs, in_specs=[_ANY] * ni, out_specs=[_ANY] * no,
        input_output_aliases=ex.aliases, scratch_shapes=ex.scratch(),
    )(*ex.ins)


def _grid_call(body, name, grid, in_specs, out_specs, out_shape, scratch_shapes, args, exchange=None):
    ni, no = len(in_specs), len(out_specs)
    params = pltpu.CompilerParams(dimension_semantics=("arbitrary",) * len(grid), vmem_limit_bytes=VMEM_LIMIT)
    if exchange is None:
        outs = pl.pallas_call(body, name=name, grid=grid, in_specs=in_specs, out_specs=out_specs, out_shape=out_shape,
                              scratch_shapes=scratch_shapes, compiler_params=params)(*args)
        return list(outs), []
    ex = exchange
    nci, nco = len(ex.ins), len(ex.outs)

    def hosted(*refs):
        cin = refs[ni:ni + nci]
        cout = refs[ni + nci + no:ni + nci + no + nco]
        sems = _Sems(*refs[len(refs) - 3:])
        main = refs[:ni] + refs[ni + nci:ni + nci + no] + refs[ni + nci + no + nco:len(refs) - 3]
        ids = [pl.program_id(a) for a in range(len(grid))]
        first = functools.reduce(lambda p, q: p & q, [i == 0 for i in ids])
        last = functools.reduce(lambda p, q: p & q, [i == g - 1 for i, g in zip(ids, grid)])

        @pl.when(first)
        def _():
            ex.start(cin, cout, sems)

        if ex.mid is not None:
            steps = functools.reduce(lambda p, q: p * q, grid)
            flat = functools.reduce(lambda p, q: p * q[1] + q[0], zip(ids[1:], grid[1:]), ids[0])

            @pl.when(flat == min(steps - 1, int(ex.mid_at * steps)))
            def _():
                ex.mid(cin, cout, sems)

        body(*main)

        @pl.when(last)
        def _():
            ex.finish(cin, cout, sems)

    outs = pl.pallas_call(
        hosted, name=name, grid=grid, in_specs=list(in_specs) + [_ANY] * nci, out_specs=list(out_specs) + [_ANY] * nco,
        out_shape=list(out_shape) + ex.outs, scratch_shapes=list(scratch_shapes) + ex.scratch(),
        input_output_aliases={ni + a: no + b for a, b in ex.aliases.items()}, compiler_params=params,
    )(*args, *ex.ins)
    return list(outs[:no]), list(outs[no:])


def _row_tile(rows, cols, itemsize, budget=2 * 1024 * 1024):
    best = None
    for t in range(16, rows + 1, 16):
        if rows % t == 0 and t * cols * itemsize <= budget:
            best = t
    return best if best is not None else rows


def _pair_sum(g, r1, cidx, name):
    nj, r, ccols = g.shape
    hr = r // 2
    tr = _row_tile(hr, ccols, 4)
    nt = hr // tr

    def body(c_ref, g_ref, r_ref, o_ref):
        o_ref[...] = (g_ref[...].astype(F32) + r_ref[...].astype(F32)).astype(o_ref.dtype)

    return pl.pallas_call(
        body, name=name, out_shape=jax.ShapeDtypeStruct((nj, hr, ccols), g.dtype),
        grid_spec=pltpu.PrefetchScalarGridSpec(
            num_scalar_prefetch=1, grid=(nj, nt),
            in_specs=[pl.BlockSpec((None, tr, ccols), lambda j, i, c_ref: (j, c_ref[0] * nt + i, 0)),
                      pl.BlockSpec((None, tr, ccols), lambda j, i, c_ref: (j, i, 0))],
            out_specs=pl.BlockSpec((None, tr, ccols), lambda j, i, c_ref: (j, i, 0))),
        compiler_params=pltpu.CompilerParams(dimension_semantics=("arbitrary", "arbitrary")),
    )(cidx, g, r1)


def _chip_sum(p, r2, idx, name):
    nj, hr, ccols = p.shape
    tr = _row_tile(hr, ccols, 4)
    nt = hr // tr

    def body(i_ref, p_ref, r_ref, o_ref):
        s = p_ref[...].astype(F32)
        for k in range(N_CHIPS - 1):
            s = s + r_ref[k].astype(F32)
        o_ref[...] = s

    return pl.pallas_call(
        body, name=name, out_shape=jax.ShapeDtypeStruct((2, hr, ccols), F32),
        grid_spec=pltpu.PrefetchScalarGridSpec(
            num_scalar_prefetch=1, grid=(nt,),
            in_specs=[pl.BlockSpec((None, tr, ccols), lambda i, i_ref: (i_ref[0], i, 0)),
                      pl.BlockSpec((N_CHIPS - 1, tr, ccols), lambda i, i_ref: (0, i, 0))],
            out_specs=pl.BlockSpec((None, tr, ccols), lambda i, i_ref: (i_ref[1], i, 0))),
        compiler_params=pltpu.CompilerParams(dimension_semantics=("arbitrary",)),
    )(idx, p, r2)


def _adam_math(w, g, m, v):
    m2 = ADAM_B1 * m + (1.0 - ADAM_B1) * g
    v2 = ADAM_B2 * v + (1.0 - ADAM_B2) * (g * g)
    m_hat = m2 / (1.0 - ADAM_B1 ** ADAM_STEP)
    v_hat = v2 / (1.0 - ADAM_B2 ** ADAM_STEP)
    delta = -ADAM_LR * (m_hat / (jnp.sqrt(v_hat) + ADAM_EPS) + ADAM_WD * w)
    return delta, m2, v2


def _adamw_layers(w, m, v, gs, name):
    nl, r, ccols = w.shape
    ng = len(gs)
    tr = _row_tile(r, ccols, 4, budget=1024 * 1024)
    nt = r // tr

    def body(w_ref, m_ref, v_ref, *rest):
        g_refs, (go_ref, d_ref, mo_ref, vo_ref) = rest[:ng], rest[ng:]
        l = pl.program_id(0)
        g = g_refs[0][...]
        for k in range(1, ng):
            g = jnp.where(l == k, g_refs[k][...], g)
        delta, m2, v2 = _adam_math(w_ref[...], g, m_ref[...], v_ref[...])
        go_ref[...] = g
        d_ref[...] = delta
        mo_ref[...] = m2
        vo_ref[...] = v2

    big = pl.BlockSpec((None, tr, ccols), lambda l, i: (l, i, 0))

    def gspec(k):
        return pl.BlockSpec((tr, ccols), lambda l, i: (jnp.where(l == k, i, jnp.where(l < k, 0, nt - 1)), 0))

    assert ng == nl
    return _grid_call(body, name, (nl, nt), in_specs=[big, big, big] + [gspec(k) for k in range(ng)],
                      out_specs=[big, big, big, big], out_shape=[jax.ShapeDtypeStruct(w.shape, F32)] * 4,
                      scratch_shapes=[], args=(w, m, v, *gs))[0]


def _adamw_flat(w, g, m, v, name):
    r, ccols = w.shape

    def body(w_ref, g_ref, m_ref, v_ref, d_ref, mo_ref, vo_ref):
        delta, m2, v2 = _adam_math(w_ref[...], g_ref[...], m_ref[...], v_ref[...])
        d_ref[...] = delta
        mo_ref[...] = m2
        vo_ref[...] = v2

    return pl.pallas_call(
        body, name=name, out_shape=[jax.ShapeDtypeStruct((r, ccols), F32)] * 3,
        in_specs=[_VMEM] * 4, out_specs=[_VMEM] * 3,
        compiler_params=pltpu.CompilerParams(vmem_limit_bytes=VMEM_LIMIT),
    )(w, g, m, v)


def _ada_forward(c_all, ada_w, ada_b_cols, name):
    nl, d, ncols = ada_w.shape
    bg = c_all.shape[0]
    tn = 512 if ncols % 512 == 0 else ncols

    def body(c_ref, w_ref, b_ref, o_ref):
        cv = c_ref[...]
        ca = (cv * _sigmoid(cv)).astype(BF16)
        o_ref[...] = jnp.dot(ca, w_ref[...].astype(BF16), preferred_element_type=F32) + b_ref[...]

    return pl.pallas_call(
        body, name=name, out_shape=jax.ShapeDtypeStruct((nl, bg, ncols), F32),
        grid=(nl, ncols // tn),
        in_specs=[pl.BlockSpec((bg, d), lambda l, j: (0, 0)),
                  pl.BlockSpec((None, d, tn), lambda l, j: (l, 0, j)),
                  pl.BlockSpec((None, 1, tn), lambda l, j: (l, 0, j))],
        out_specs=pl.BlockSpec((None, bg, tn), lambda l, j: (l, 0, j)),
        compiler_params=pltpu.CompilerParams(dimension_semantics=("arbitrary", "arbitrary")),
    )(c_all, ada_w, ada_b_cols)


def _ada_update(c_all, dmod_cols, w, m, v, name, exchange=None):
    nl, d, ncols = w.shape
    bg = c_all.shape[0]
    tn = 512 if ncols % 512 == 0 else ncols

    def body(c_ref, dm_ref, w_ref, m_ref, v_ref, go_ref, d_ref, mo_ref, vo_ref):
        cv = c_ref[...]
        ca = (cv * _sigmoid(cv)).astype(BF16)
        g = lax.dot_general(ca, dm_ref[...].astype(BF16), (((0,), (0,)), ((), ())), preferred_element_type=F32)
        delta, m2, v2 = _adam_math(w_ref[...], g, m_ref[...], v_ref[...])
        go_ref[...] = g
        d_ref[...] = delta
        mo_ref[...] = m2
        vo_ref[...] = v2

    big = pl.BlockSpec((None, d, tn), lambda l, j: (l, 0, j))
    return _grid_call(
        body, name, (nl, ncols // tn),
        in_specs=[pl.BlockSpec((bg, d), lambda l, j: (0, 0)),
                  pl.BlockSpec((None, bg, tn), lambda l, j: (l, 0, j)), big, big, big],
        out_specs=[big, big, big, big], out_shape=[jax.ShapeDtypeStruct(w.shape, F32)] * 4,
        scratch_shapes=[], args=(c_all, dmod_cols, w, m, v), exchange=exchange)


def _load_weights(first, pairs, sems):
    @pl.when(first)
    def _():
        cps = [pltpu.make_async_copy(src, dst, sems.at[k]) for k, (src, dst) in enumerate(pairs)]
        for cp in cps:
            cp.start()
        for cp in cps:
            cp.wait()


def _ada_norm(xv, g, sc, sh):
    r = lax.rsqrt(jnp.mean(xv * xv, axis=-1, keepdims=True) + EPS)
    xn = xv * r
    return (xn * g) * (1.0 + sc) + sh, xn, r


def _ada_norm_bwd(dh, xn, r, g, sc):
    d_sh = _colsum(dh)
    d_sc = _colsum(dh * (xn * g))
    dxg = dh * (1.0 + sc)
    d_g = _colsum(dxg * xn)
    gd = dxg * g
    dx = r * (gd - xn * jnp.mean(gd * xn, axis=-1, keepdims=True))
    return dx, d_sh, d_sc, d_g


def _gated_residual_bwd(dxo, o, g_post, gt):
    r = lax.rsqrt(jnp.mean(o * o, axis=-1, keepdims=True) + EPS)
    on = o * r
    d_gt = _colsum(dxo * (on * g_post))
    dy = dxo * (1.0 + gt)
    d_gp = _colsum(dy * on)
    gd = dy * g_post
    do = r * (gd - on * jnp.mean(gd * on, axis=-1, keepdims=True))
    return do, d_gt, d_gp


def _seq_positions(i, tm, width):
    return i * tm + lax.broadcasted_iota(jnp.int32, (tm, width), 0)


def _fill_phases(ext, phases):
    rows = ext.shape[0]
    ev = ext[...]
    for r in range(1, SUBLANES):
        phases[r - 1] = pltpu.roll(ev, rows - r, axis=0)


def _shifted_rows(ext, phases, offset, n):
    q, r = divmod(offset, SUBLANES)
    if r == 0:
        return ext[pl.ds(q * SUBLANES, n), :]
    return phases[r - 1, pl.ds(q * SUBLANES, n), :]


def _rows_before(halo, cur, shift):
    e = jnp.concatenate([halo, cur], axis=0)
    return pltpu.roll(e, shift, axis=0)[halo.shape[0]:, :]


def _rows_after(cur, halo, shift):
    e = jnp.concatenate([cur, halo], axis=0)
    return pltpu.roll(e, e.shape[0] - shift, axis=0)[:cur.shape[0], :]


def _mixer_forward(x, mod, vec_d, vec_c, cw, pw, win_g, wout_g, taps, tm, name, exchange=None):
    nb, s, d = x.shape
    n = s // tm
    nj, _, dcol = win_g.shape
    din = nj * dcol
    dc = vec_c.shape[-1]
    dpool = din - 2 * dc
    dmix = dc + dpool
    ro = wout_g.shape[1]
    ngrp = dpool // LANES

    def body(x_ref, mod_ref, vd_ref, vc_ref, cw_ref, pw_ref, win_hbm, wout_hbm,
             xo_ref, h_ref, u_ref, ac_ref, dp_ref, z_ref, o_ref,
             win_v, wout_v, ext_a, ext_p, phases, sems):
        b, i = pl.program_id(0), pl.program_id(1)
        pairs = [(win_hbm.at[j], win_v.at[:, pl.ds(j * dcol, dcol)]) for j in range(nj)]
        pairs += [(wout_hbm.at[j], wout_v.at[pl.ds(j * ro, ro), :]) for j in range(nj)]
        _load_weights((b == 0) & (i == 0), pairs, sems)

        xv = x_ref[...]
        h, _, _ = _ada_norm(xv, vd_ref[0:1, :], mod_ref[1:2, :], mod_ref[0:1, :])
        hb = h.astype(BF16)
        h_ref[...] = hb
        u = jnp.dot(hb, win_v[...], preferred_element_type=F32)
        u_ref[...] = u.astype(BF16)
        ag = u[:, :dc] * _sigmoid(u[:, dc:2 * dc])
        up = u[:, 2 * dc:]

        @pl.when(i == 0)
        def _():
            ext_a[0:HALO, :] = jnp.zeros((HALO, dc), F32)
            ext_p[0:HALO, :] = jnp.zeros((HALO, dpool), F32)

        @pl.when(i > 0)
        def _():
            ext_a[0:HALO, :] = ext_a[tm:tm + HALO, :]
            ext_p[0:HALO, :] = ext_p[tm:tm + HALO, :]

        ext_a[HALO:HALO + tm, :] = ag
        ext_p[HALO:HALO + tm, :] = up

        acc = jnp.broadcast_to(vc_ref[0:1, :], (tm, dc))
        _fill_phases(ext_a, phases)
        for k in range(taps):
            acc = acc + cw_ref[k:k + 1, :] * _shifted_rows(ext_a, phases, HALO - (taps - 1) + k, tm)
        ac_ref[...] = acc.astype(BF16)
        mu = jnp.mean(acc, axis=-1, keepdims=True)
        xc = acc - mu
        var = jnp.mean(xc * xc, axis=-1, keepdims=True)
        al = (xc * lax.rsqrt(var + EPS)) * vc_ref[1:2, :] + vc_ref[2:3, :]
        a = al * _sigmoid(al)

        pos = _seq_positions(i, tm, LANES)
        parts = [a.astype(BF16)]
        for g in range(ngrp):
            w = POOL_WINDOWS[g]
            cols = slice(g * LANES, (g + 1) * LANES)
            sw = ext_p[:, cols]
            step = 1
            while step < w:
                sw = sw + pltpu.roll(sw, step, axis=0)
                step *= 2
            cnt = jnp.minimum(pos + 1, w).astype(F32)
            dg = (sw[HALO:, :] / cnt - up[:, cols]).astype(BF16)
            dp_ref[:, cols] = dg
            q = jnp.dot(dg, pw_ref[g], preferred_element_type=F32)
            parts.append((q * vc_ref[3:4, cols]).astype(BF16))
        z = jnp.concatenate(parts, axis=-1)
        z_ref[...] = z
        o = jnp.dot(z, wout_v[...], preferred_element_type=F32)
        o_ref[...] = o
        r2 = lax.rsqrt(jnp.mean(o * o, axis=-1, keepdims=True) + EPS)
        xo_ref[...] = xv + (1.0 + mod_ref[2:3, :]) * ((o * r2) * vd_ref[1:2, :])

    def tile(width):
        return pl.BlockSpec((None, tm, width), lambda b, i: (b, i, 0))

    return _grid_call(
        body, name, (nb, n),
        in_specs=[tile(d), pl.BlockSpec((None, 8, d), lambda b, i: (b, 0, 0)), _full(vec_d.shape), _full(vec_c.shape),
                  _full(cw.shape), _full(pw.shape), _ANY, _ANY],
        out_specs=[tile(d), tile(d), tile(din), tile(dc), tile(dpool), tile(dmix), tile(d)],
        out_shape=[jax.ShapeDtypeStruct((nb, s, d), F32), jax.ShapeDtypeStruct((nb, s, d), BF16),
                   jax.ShapeDtypeStruct((nb, s, din), BF16), jax.ShapeDtypeStruct((nb, s, dc), BF16),
                   jax.ShapeDtypeStruct((nb, s, dpool), BF16), jax.ShapeDtypeStruct((nb, s, dmix), BF16),
                   jax.ShapeDtypeStruct((nb, s, d), F32)],
        scratch_shapes=[pltpu.VMEM((d, din), BF16), pltpu.VMEM((dmix, d), BF16),
                        pltpu.VMEM((HALO + tm, dc), F32), pltpu.VMEM((HALO + tm, dpool), F32),
                        pltpu.VMEM((SUBLANES - 1, HALO + tm, dc), F32), pltpu.SemaphoreType.DMA((2 * nj,))],
        args=(x, mod, vec_d, vec_c, cw, pw, win_g, wout_g), exchange=exchange)


def _mixer_backward(dxo, x, o, u, ac, dpl, mod, vec_d, vec_c, cw, pw, win_g, wout_g, taps, tm, name, exchange=None):
    nb, s, d = x.shape
    n = s // tm
    nj, _, dcol = win_g.shape
    din = nj * dcol
    dc = vec_c.shape[-1]
    dpool = din - 2 * dc
    dmix = dc + dpool
    ro = wout_g.shape[1]
    ngrp = dpool // LANES
    rext = tm + HALO

    def body(dxo_ref, x_ref, o_ref, u_ref, ac_ref, dp_ref, mod_ref, vd_ref, vc_ref, cw_ref, pw_ref, win_hbm, wout_hbm,
             dx_ref, du_ref, dob_ref, rowd_ref, rowb_ref, rowc_ref, dcw_ref, dpw_ref,
             win_v, wout_v, ext_a, ext_p, phases, sems):
        b, i = pl.program_id(0), pl.program_id(1)
        first = (b == 0) & (i == 0)
        pairs = [(win_hbm.at[j], win_v.at[:, pl.ds(j * dcol, dcol)]) for j in range(nj)]
        pairs += [(wout_hbm.at[j], wout_v.at[pl.ds(j * ro, ro), :]) for j in range(nj)]
        _load_weights(first, pairs, sems)

        @pl.when(first)
        def _():
            rowd_ref[...] = jnp.zeros_like(rowd_ref)
            rowc_ref[...] = jnp.zeros_like(rowc_ref)
            dcw_ref[...] = jnp.zeros_like(dcw_ref)
            dpw_ref[...] = jnp.zeros_like(dpw_ref)

        @pl.when(i == 0)
        def _():
            rowb_ref[...] = jnp.zeros_like(rowb_ref)
            ext_a[tm:rext, :] = jnp.zeros((HALO, dc), F32)
            ext_p[tm:rext, :] = jnp.zeros((HALO, dpool), F32)

        @pl.when(i > 0)
        def _():
            ext_a[tm:rext, :] = ext_a[0:HALO, :]
            ext_p[tm:rext, :] = ext_p[0:HALO, :]

        g_pre, g_post = vd_ref[0:1, :], vd_ref[1:2, :]
        sh, sc, gt = mod_ref[0:1, :], mod_ref[1:2, :], mod_ref[2:3, :]
        do, d_gt, d_gp = _gated_residual_bwd(dxo_ref[...], o_ref[...], g_post, gt)
        dob = do.astype(BF16)
        dob_ref[...] = dob
        dz = lax.dot_general(dob, wout_v[...], (((1,), (1,)), ((), ())), preferred_element_type=F32)

        acv = ac_ref[...].astype(F32)
        mu = jnp.mean(acv, axis=-1, keepdims=True)
        xc = acv - mu
        rstd = lax.rsqrt(jnp.mean(xc * xc, axis=-1, keepdims=True) + EPS)
        an = xc * rstd
        lg = vc_ref[1:2, :]
        al = an * lg + vc_ref[2:3, :]
        sg = _sigmoid(al)
        dal = dz[:, :dc] * (sg * (1.0 + al * (1.0 - sg)))
        d_lg = _colsum(dal * an)
        d_lb = _colsum(dal)
        dan = dal * lg
        dac = rstd * (dan - jnp.mean(dan, axis=-1, keepdims=True) - an * jnp.mean(dan * an, axis=-1, keepdims=True))
        d_cb = _colsum(dac)
        ext_a[0:tm, :] = dac
        uv = u_ref[:, 0:dc].astype(F32)
        sgu = _sigmoid(u_ref[:, dc:2 * dc].astype(F32))
        ag = uv * sgu
        dag = jnp.zeros((tm, dc), F32)
        _fill_phases(ext_a, phases)
        for k in range(taps):
            sl = _shifted_rows(ext_a, phases, taps - 1 - k, tm)
            dag = dag + cw_ref[k:k + 1, :] * sl
            dcw_ref[k:k + 1, :] += _colsum(ag * sl)
        du_ref[:, 0:dc] = (dag * sgu).astype(BF16)
        du_ref[:, dc:2 * dc] = (dag * uv * (sgu * (1.0 - sgu))).astype(BF16)

        pos = _seq_positions(n - 1 - i, tm, LANES)
        d_ps = []
        for g in range(ngrp):
            w = POOL_WINDOWS[g]
            cols = slice(g * LANES, (g + 1) * LANES)
            gcols = slice(dc + g * LANES, dc + (g + 1) * LANES)
            dgb = dp_ref[:, cols]
            q = jnp.dot(dgb, pw_ref[g], preferred_element_type=F32)
            dpg = dz[:, gcols]
            d_ps.append(_colsum(dpg * q))
            dq = (dpg * vc_ref[3:4, cols]).astype(BF16)
            dpw_ref[g] += lax.dot_general(dgb, dq, (((0,), (0,)), ((), ())), preferred_element_type=F32)
            dd = lax.dot_general(dq, pw_ref[g], (((1,), (1,)), ((), ())), preferred_element_type=F32)
            cnt = jnp.minimum(pos + 1, w).astype(F32)
            ext_p[0:tm, cols] = dd / cnt
            sw = ext_p[:, cols]
            step = 1
            while step < w:
                sw = sw + pltpu.roll(sw, rext - step, axis=0)
                step *= 2
            du_ref[:, 2 * dc + g * LANES:2 * dc + (g + 1) * LANES] = (sw[0:tm, :] - dd).astype(BF16)
        rowc_ref[0:1, :] += d_cb
        rowc_ref[1:2, :] += d_lg
        rowc_ref[2:3, :] += d_lb
        rowc_ref[3:4, :] += jnp.concatenate(d_ps, axis=-1)

        dh = lax.dot_general(du_ref[...], win_v[...], (((1,), (1,)), ((), ())), preferred_element_type=F32)
        _, xn, r1 = _ada_norm(x_ref[...], g_pre, sc, sh)
        dxb, d_sh, d_sc, d_g = _ada_norm_bwd(dh, xn, r1, g_pre, sc)
        dx_ref[...] = dxo_ref[...] + dxb
        rowd_ref[0:1, :] += d_g
        rowd_ref[1:2, :] += d_gp
        rowb_ref[0:1, :] += d_sh
        rowb_ref[1:2, :] += d_sc
        rowb_ref[2:3, :] += d_gt

    def tile(width):
        return pl.BlockSpec((None, tm, width), lambda b, i: (b, n - 1 - i, 0))

    return _grid_call(
        body, name, (nb, n),
        in_specs=[tile(d), tile(d), tile(d), tile(din), tile(dc), tile(dpool),
                  pl.BlockSpec((None, 8, d), lambda b, i: (b, 0, 0)), _full(vec_d.shape), _full(vec_c.shape),
                  _full(cw.shape), _full(pw.shape), _ANY, _ANY],
        out_specs=[tile(d), tile(din), tile(d), _full((8, d)), pl.BlockSpec((None, 8, d), lambda b, i: (b, 0, 0)),
                   _full((8, dc)), _full((HALO, dc)), _full(pw.shape)],
        out_shape=[jax.ShapeDtypeStruct((nb, s, d), F32), jax.ShapeDtypeStruct((nb, s, din), BF16),
                   jax.ShapeDtypeStruct((nb, s, d), BF16), jax.ShapeDtypeStruct((8, d), F32),
                   jax.ShapeDtypeStruct((nb, 8, d), F32), jax.ShapeDtypeStruct((8, dc), F32),
                   jax.ShapeDtypeStruct((HALO, dc), F32), jax.ShapeDtypeStruct(pw.shape, F32)],
        scratch_shapes=[pltpu.VMEM((d, din), BF16), pltpu.VMEM((dmix, d), BF16),
                        pltpu.VMEM((rext, dc), F32), pltpu.VMEM((rext, dpool), F32),
                        pltpu.VMEM((SUBLANES - 1, rext, dc), F32), pltpu.SemaphoreType.DMA((2 * nj,))],
        args=(dxo, x, o, u, ac, dpl, mod, vec_d, vec_c, cw, pw, win_g, wout_g), exchange=exchange)


def _ffn_forward(x, mod, vec_d, fw, wup_g, wdn_g, tm, name, exchange=None):
    nb, s, d = x.shape
    n = s // tm
    nj, _, ucol = wup_g.shape
    f2 = nj * ucol
    dff = f2 // 2
    rd = wdn_g.shape[1]
    nq = nj // 2
    cs = dff // nq

    def body(x_ref, mod_ref, vd_ref, fw_ref, wup_hbm, wdn_hbm,
             xo_ref, h_ref, u_ref, uc_ref, hid_ref, o_ref,
             wup_v, wdn_v, prev_u, sems):
        b, i = pl.program_id(0), pl.program_id(1)
        pairs = [(wup_hbm.at[j], wup_v.at[:, pl.ds(j * ucol, ucol)]) for j in range(nj)]
        pairs += [(wdn_hbm.at[j], wdn_v.at[pl.ds(j * rd, rd), :]) for j in range(nj)]
        _load_weights((b == 0) & (i == 0), pairs, sems)

        @pl.when(i == 0)
        def _():
            prev_u[...] = jnp.zeros_like(prev_u)

        xv = x_ref[...]
        h, _, _ = _ada_norm(xv, vd_ref[2:3, :], mod_ref[4:5, :], mod_ref[3:4, :])
        hb = h.astype(BF16)
        h_ref[...] = hb

        def conv(cols):
            uc = jnp.dot(hb, wup_v[:, cols], preferred_element_type=F32)
            u_ref[:, cols] = uc.astype(BF16)
            before = prev_u[:, cols]
            prev_u[:, cols] = uc[tm - FHALO:, :]
            out = (fw_ref[3:4, cols] + fw_ref[2:3, cols] * uc + fw_ref[1:2, cols] * _rows_before(before, uc, 1)
                   + fw_ref[0:1, cols] * _rows_before(before, uc, 2))
            uc_ref[:, cols] = out.astype(BF16)
            return out

        o = jnp.zeros((tm, d), F32)
        for q in range(nq):
            val = conv(pl.ds(q * cs, cs))
            gate = conv(pl.ds(dff + q * cs, cs))
            hid = ((gate * _sigmoid(gate)) * val).astype(BF16)
            hid_ref[:, pl.ds(q * cs, cs)] = hid
            o = o + jnp.dot(hid, wdn_v[pl.ds(q * cs, cs), :], preferred_element_type=F32)
        o_ref[...] = o
        r2 = lax.rsqrt(jnp.mean(o * o, axis=-1, keepdims=True) + EPS)
        xo_ref[...] = xv + (1.0 + mod_ref[5:6, :]) * ((o * r2) * vd_ref[3:4, :])

    def tile(width):
        return pl.BlockSpec((None, tm, width), lambda b, i: (b, i, 0))

    return _grid_call(
        body, name, (nb, n),
        in_specs=[tile(d), pl.BlockSpec((None, 8, d), lambda b, i: (b, 0, 0)), _full(vec_d.shape), _full(fw.shape),
                  _ANY, _ANY],
        out_specs=[tile(d), tile(d), tile(f2), tile(f2), tile(dff), tile(d)],
        out_shape=[jax.ShapeDtypeStruct((nb, s, d), F32), jax.ShapeDtypeStruct((nb, s, d), BF16),
                   jax.ShapeDtypeStruct((nb, s, f2), BF16), jax.ShapeDtypeStruct((nb, s, f2), BF16),
                   jax.ShapeDtypeStruct((nb, s, dff), BF16), jax.ShapeDtypeStruct((nb, s, d), F32)],
        scratch_shapes=[pltpu.VMEM((d, f2), BF16), pltpu.VMEM((dff, d), BF16),
                        pltpu.VMEM((FHALO, f2), F32), pltpu.SemaphoreType.DMA((2 * nj,))],
        args=(x, mod, vec_d, fw, wup_g, wdn_g), exchange=exchange)


def _ffn_backward(dxo, x, o, u, uc, mod, vec_d, fw, wup_g, wdn_g, tm, name, exchange=None):
    nb, s, d = x.shape
    n = s // tm
    nj, _, ucol = wup_g.shape
    f2 = nj * ucol
    dff = f2 // 2
    rd = wdn_g.shape[1]
    nq = nj // 2
    cs = dff // nq

    def body(dxo_ref, x_ref, o_ref, u_ref, uc_ref, mod_ref, vd_ref, fw_ref, wup_hbm, wdn_hbm,
             dx_ref, du_ref, dob_ref, rowd_ref, rowb_ref, dfw_ref,
             wup_v, wdn_v, next_d, sems):
        b, i = pl.program_id(0), pl.program_id(1)
        first = (b == 0) & (i == 0)
        pairs = [(wup_hbm.at[j], wup_v.at[:, pl.ds(j * ucol, ucol)]) for j in range(nj)]
        pairs += [(wdn_hbm.at[j], wdn_v.at[pl.ds(j * rd, rd), :]) for j in range(nj)]
        _load_weights(first, pairs, sems)

        @pl.when(first)
        def _():
            rowd_ref[...] = jnp.zeros_like(rowd_ref)
            dfw_ref[...] = jnp.zeros_like(dfw_ref)

        @pl.when(i == 0)
        def _():
            rowb_ref[...] = jnp.zeros_like(rowb_ref)
            next_d[...] = jnp.zeros_like(next_d)

        g_pre, g_post = vd_ref[2:3, :], vd_ref[3:4, :]
        sh, sc, gt = mod_ref[3:4, :], mod_ref[4:5, :], mod_ref[5:6, :]
        do, d_gt, d_gp = _gated_residual_bwd(dxo_ref[...], o_ref[...], g_post, gt)
        dob = do.astype(BF16)
        dob_ref[...] = dob

        def conv_bwd(cols, duc):
            uc = u_ref[:, cols].astype(F32)
            after = next_d[:, cols]
            next_d[:, cols] = duc[0:FHALO, :]
            d1 = _rows_after(duc, after, 1)
            d2 = _rows_after(duc, after, 2)
            dfw_ref[3:4, cols] += _colsum(duc)
            dfw_ref[2:3, cols] += _colsum(uc * duc)
            dfw_ref[1:2, cols] += _colsum(uc * d1)
            dfw_ref[0:1, cols] += _colsum(uc * d2)
            ob = (fw_ref[2:3, cols] * duc + fw_ref[1:2, cols] * d1 + fw_ref[0:1, cols] * d2).astype(BF16)
            du_ref[:, cols] = ob
            return lax.dot_general(ob, wup_v[:, cols], (((1,), (1,)), ((), ())), preferred_element_type=F32)

        dh = jnp.zeros((tm, d), F32)
        for q in range(nq):
            vcols = pl.ds(q * cs, cs)
            gcols = pl.ds(dff + q * cs, cs)
            dhid = lax.dot_general(dob, wdn_v[vcols, :], (((1,), (1,)), ((), ())), preferred_element_type=F32)
            val = uc_ref[:, vcols].astype(F32)
            gate = uc_ref[:, gcols].astype(F32)
            sg = _sigmoid(gate)
            act = gate * sg
            dval = dhid * act
            dgate = (dhid * val) * (sg + act * (1.0 - sg))
            dh = dh + conv_bwd(vcols, dval)
            dh = dh + conv_bwd(gcols, dgate)

        _, xn, r1 = _ada_norm(x_ref[...], g_pre, sc, sh)
        dxb, d_sh, d_sc, d_g = _ada_norm_bwd(dh, xn, r1, g_pre, sc)
        dx_ref[...] = dxo_ref[...] + dxb
        rowd_ref[2:3, :] += d_g
        rowd_ref[3:4, :] += d_gp
        rowb_ref[3:4, :] += d_sh
        rowb_ref[4:5, :] += d_sc
        rowb_ref[5:6, :] += d_gt

    def tile(width):
        return pl.BlockSpec((None, tm, width), lambda b, i: (b, n - 1 - i, 0))

    return _grid_call(
        body, name, (nb, n),
        in_specs=[tile(d), tile(d), tile(d), tile(f2), tile(f2), pl.BlockSpec((None, 8, d), lambda b, i: (b, 0, 0)),
                  _full(vec_d.shape), _full(fw.shape), _ANY, _ANY],
        out_specs=[tile(d), tile(f2), tile(d), _full((8, d)), pl.BlockSpec((None, 8, d), lambda b, i: (b, 0, 0)),
                   _full(fw.shape)],
        out_shape=[jax.ShapeDtypeStruct((nb, s, d), F32), jax.ShapeDtypeStruct((nb, s, f2), BF16),
                   jax.ShapeDtypeStruct((nb, s, d), BF16), jax.ShapeDtypeStruct((8, d), F32),
                   jax.ShapeDtypeStruct((nb, 8, d), F32), jax.ShapeDtypeStruct(fw.shape, F32)],
        scratch_shapes=[pltpu.VMEM((d, f2), BF16), pltpu.VMEM((dff, d), BF16),
                        pltpu.VMEM((FHALO, f2), F32), pltpu.SemaphoreType.DMA((2 * nj,))],
        args=(dxo, x, o, u, uc, mod, vec_d, fw, wup_g, wdn_g), exchange=exchange)


def _weight_grad(a, b, nblk, split, tt, name, exchange=None):
    t, ka = a.shape
    nb_ = b.shape[1]
    nk = t // tt
    if split == "cols":
        wa, wb, grid = ka, nb_ // nblk, (1, nk)
        a_spec = pl.BlockSpec((tt, ka), lambda j, k: (k, 0))
        b_spec = pl.BlockSpec((tt, nb_), lambda j, k: (k, 0))
        o_spec = pl.BlockSpec((nblk, wa, wb), lambda j, k: (0, 0, 0))
        acc_shape = (ka, nb_)
    elif split == "b":
        wa, wb, grid = ka, nb_ // nblk, (nblk, nk)
        a_spec = pl.BlockSpec((tt, wa), lambda j, k: (k, 0))
        b_spec = pl.BlockSpec((tt, wb), lambda j, k: (k, j))
        o_spec = pl.BlockSpec((None, wa, wb), lambda j, k: (j, 0, 0))
        acc_shape = (wa, wb)
    else:
        wa, wb, grid = ka // nblk, nb_, (nblk, nk)
        a_spec = pl.BlockSpec((tt, wa), lambda j, k: (k, j))
        b_spec = pl.BlockSpec((tt, wb), lambda j, k: (k, 0))
        o_spec = pl.BlockSpec((None, wa, wb), lambda j, k: (j, 0, 0))
        acc_shape = (wa, wb)

    def body(a_ref, b_ref, o_ref, acc):
        k = pl.program_id(1)
        prod = lax.dot_general(a_ref[...], b_ref[...], (((0,), (0,)), ((), ())), preferred_element_type=F32)

        @pl.when(k == 0)
        def _():
            acc[...] = prod

        @pl.when(k > 0)
        def _():
            acc[...] += prod

        @pl.when(k == nk - 1)
        def _():
            if split == "cols":
                for j in range(nblk):
                    o_ref[j] = acc[:, j * wb:(j + 1) * wb].astype(o_ref.dtype)
            else:
                o_ref[...] = acc[...].astype(o_ref.dtype)

    outs, exo = _grid_call(body, name, grid, in_specs=[a_spec, b_spec], out_specs=[o_spec],
                           out_shape=[jax.ShapeDtypeStruct((nblk, wa, wb), BF16)],
                           scratch_shapes=[pltpu.VMEM(acc_shape, F32)], args=(a, b), exchange=exchange)
    return outs[0], exo


def _loss_grad(y, tgt, tm, name):
    nb, s, d = y.shape
    n = s // tm

    def body(y_ref, t_ref, dy_ref, sq_ref):
        @pl.when((pl.program_id(0) == 0) & (pl.program_id(1) == 0))
        def _():
            sq_ref[...] = jnp.zeros_like(sq_ref)

        e = y_ref[...] - t_ref[...]
        dy_ref[...] = e * (1.0 / d)
        sq_ref[0:1, :] += _colsum(e * e)

    tile = pl.BlockSpec((None, tm, d), lambda b, i: (b, i, 0))
    return pl.pallas_call(
        body, name=name, out_shape=[jax.ShapeDtypeStruct((nb, s, d), F32), jax.ShapeDtypeStruct((8, d), F32)],
        grid=(nb, n), in_specs=[tile, tile], out_specs=[tile, _full((8, d))],
        compiler_params=pltpu.CompilerParams(dimension_semantics=("arbitrary", "arbitrary")),
    )(y, tgt)


def _rows128(a):
    return a.reshape(-1, LANES)


class _ReduceScatter:
    def __init__(self, gs, cidx, idx, tag):
        self.gs, self.cidx, self.idx, self.tag = gs, cidx, idx, tag

    def swap(self):
        return _swap_halves(self.gs)

    def after_swap(self, r1):
        self.ps = [_pair_sum(g, r, self.cidx, name=f"rs_pair_{self.tag}_{a}") for a, (g, r) in enumerate(zip(self.gs, r1))]

    def chips(self):
        return _chip_exchange(self.ps)

    def after_chips(self, r2):
        self.fh = [_chip_sum(p, r, self.idx, name=f"rs_sum_{self.tag}_{a}") for a, (p, r) in enumerate(zip(self.ps, r2))]

    def share(self):
        return _sibling_share(self.fh)

    @staticmethod
    def result(fs):
        return [f.reshape(f.shape[0] * f.shape[1], f.shape[2]) for f in fs]


def kernel(x, c, ada_w, ada_b, pre_mix_g, post_mix_g, w_in, conv_w, conv_b, conv_ln_g, conv_ln_b, pool_w, pool_scale, w_out, pre_ffn_g, post_ffn_g, ffn_up, ffn_conv_w, ffn_conv_b, ffn_down, loss_target, m_ada_w, m_ada_b, m_pre_mix_g, m_post_mix_g, m_w_in, m_conv_w, m_conv_b, m_conv_ln_g, m_conv_ln_b, m_pool_w, m_pool_scale, m_w_out, m_pre_ffn_g, m_post_ffn_g, m_ffn_up, m_ffn_conv_w, m_ffn_conv_b, m_ffn_down, v_ada_w, v_ada_b, v_pre_mix_g, v_post_mix_g, v_w_in, v_conv_w, v_conv_b, v_conv_ln_g, v_conv_ln_b, v_pool_w, v_pool_scale, v_w_out, v_pre_ffn_g, v_post_ffn_g, v_ffn_up, v_ffn_conv_w, v_ffn_conv_b, v_ffn_down):
    nb, s, d = x.shape
    nl = w_in.shape[0]
    taps = conv_w.shape[1]
    ccol = conv_w.shape[2]
    dc = conv_b.shape[1]
    fcol = ffn_conv_w.shape[2]
    f2 = ffn_conv_b.shape[1]
    nmod = ada_b.shape[1] // d
    acol = ada_w.shape[2]
    tm = min(256, s)
    tm_mix = min(512, s)
    tt = min(2048, (nb * s) // 2)

    xi, yi, ci = _pos()
    jm = 2 * xi + yi
    cidx = jnp.reshape(ci, (1,)).astype(jnp.int32)
    idx = jnp.stack([jm, ci]).astype(jnp.int32)

    n_cw, n_fw, n_c = nl * taps * ccol, nl * 3 * fcol, nb * d
    packed = jnp.concatenate([conv_w.reshape(-1), ffn_conv_w.reshape(-1), c.reshape(-1)])
    got = _gather8(_rows128(packed), name="gather_small").reshape(N_DEV, -1)
    chips = got[0::2]
    cw_full = chips[:, :n_cw].reshape(N_CHIPS, nl, taps, ccol).transpose(1, 2, 0, 3).reshape(nl, taps, dc)
    fw_full = chips[:, n_cw:n_cw + n_fw].reshape(N_CHIPS, nl, 3, fcol).transpose(1, 2, 0, 3).reshape(nl, 3, f2)
    c_all = got[:, n_cw + n_fw:].reshape(N_DEV * nb, d)

    ada_b_cols = lax.dynamic_slice_in_dim(ada_b, jm * acol, acol, axis=1).reshape(nl, 1, acol)
    mod_cols = _ada_forward(c_all, ada_w, ada_b_cols, name="ada_forward")
    by_owner = mod_cols.reshape(nl, N_DEV, nb, acol).transpose(1, 0, 2, 3).reshape(N_DEV, -1, LANES)
    mod_own = _rows_to_owners(by_owner, name="mod_to_owners").reshape(N_CHIPS, nl, nb, acol)
    mod_own = mod_own.transpose(1, 2, 0, 3).reshape(nl, nb, nmod, d)
    mod_own = jnp.pad(mod_own, ((0, 0), (0, 0), (0, 8 - nmod), (0, 0)))

    vec_d = jnp.stack([pre_mix_g, post_mix_g, pre_ffn_g, post_ffn_g], axis=1)
    vec_c = jnp.stack([conv_b, conv_ln_g, conv_ln_b, pool_scale], axis=1)
    cw_pad = jnp.pad(cw_full, ((0, 0), (0, HALO - taps), (0, 0)))
    fw_rows = jnp.concatenate([fw_full, ffn_conv_b[:, None, :], jnp.zeros((nl, 4, f2), F32)], axis=1)
    pw_b = pool_w.astype(BF16)

    win_b, wout_b, wup_b, wdn_b = (w.astype(BF16) for w in (w_in, w_out, ffn_up, ffn_down))

    def others(l):
        return [win_b[l], wout_b[l], wdn_b[l]]

    win_g, wout_g, wdn_g = _whole(_run_exchange(_gather(others(0)), name="gather_0"))
    saved = []
    xs = x
    for l in range(nl):
        (x1, h1, u1, ac1, dp1, z1, o1), got = _mixer_forward(
            xs, mod_own[l], vec_d[l], vec_c[l], cw_pad[l], pw_b[l], win_g, wout_g, taps, tm_mix, name=f"mixer_fwd_{l}",
            exchange=_gather([wup_b[l]], mid_at=0.85))
        wup_g, = _whole(got)
        (x2, h2, u2, uc2, hid2, o2), nxt = _ffn_forward(
            x1, mod_own[l], vec_d[l], fw_rows[l], wup_g, wdn_g, tm, name=f"ffn_fwd_{l}",
            exchange=_gather(others(l + 1), mid_at=0.6) if l + 1 < nl else None)
        saved.append((xs, h1, u1, ac1, dp1, z1, o1, x1, h2, u2, uc2, hid2, o2, win_g, wout_g, wup_g, wdn_g))
        if l + 1 < nl:
            win_g, wout_g, wdn_g = _whole(nxt)
        xs = x2

    dx, sq = _loss_grad(xs, loss_target, tm, name="loss_grad")
    loss = lax.psum(0.5 * jnp.sum(sq) / d, ("x", "y", "c"))

    flat = lambda a: a.reshape(nb * s, a.shape[-1])
    small = [None] * nl
    big_mlp, big_mix = [None] * nl, [None] * nl
    mlp = mix = None
    for l in reversed(range(nl)):
        x0, h1, u1, ac1, dp1, z1, o1, x1, h2, u2, uc2, hid2, o2, win_g, wout_g, wup_g, wdn_g = saved[l]
        (dx, du2, do2, rowd2, rowb2, dfw), got = _ffn_backward(
            dx, x1, o2, u2, uc2, mod_own[l], vec_d[l], fw_rows[l], wup_g, wdn_g, tm, name=f"ffn_bwd_{l}",
            exchange=_combine([mlp.chips(), mix.swap()]) if mlp else None)
        if mlp:
            mlp.after_chips(got[:2])
            mix.after_swap(got[2:])
        g_up, got = _weight_grad(flat(h2), flat(du2), N_CHIPS, "b", tt, name=f"grad_ffn_up_{l}",
                                 exchange=_combine([mlp.share(), mix.chips()]) if mlp else None)
        if mlp:
            big_mlp[l + 1] = mlp.result(got[:2])
            mix.after_chips(got[2:])
        g_dn, got = _weight_grad(flat(hid2), flat(do2), 2, "a", tt, name=f"grad_ffn_down_{l}",
                                 exchange=mix.share() if mix else None)
        if mix:
            big_mix[l + 1] = mix.result(got)
        mlp = _ReduceScatter([g_up, g_dn.reshape(N_CHIPS, -1, d)], cidx, idx, f"mlp_{l}")
        if l == 0:
            mlp.after_swap(_run_exchange(mlp.swap(), name="rs_swap_mlp_0"))
        (dx, du1, do1, rowd1, rowb1, rowc, dcw, dpw), got = _mixer_backward(
            dx, x0, o1, u1, ac1, dp1, mod_own[l], vec_d[l], vec_c[l], cw_pad[l], pw_b[l], win_g, wout_g, taps, tm_mix,
            name=f"mixer_bwd_{l}", exchange=mlp.swap() if l > 0 else mlp.chips())
        if l > 0:
            mlp.after_swap(got)
        else:
            mlp.after_chips(got)
        g_in, got = _weight_grad(flat(h1), flat(du1), N_CHIPS, "cols", tt, name=f"grad_w_in_{l}",
                                 exchange=mlp.share() if l == 0 else None)
        if l == 0:
            big_mlp[0] = mlp.result(got)
        g_out, _ = _weight_grad(flat(z1), flat(do1), 1, "cols", tt, name=f"grad_w_out_{l}")
        mix = _ReduceScatter([g_in, g_out.reshape(N_CHIPS, -1, d)], cidx, idx, f"mix_{l}")
        small[l] = dict(rowd=rowd1 + rowd2, rowb=rowb1 + rowb2, rowc=rowc, dcw=dcw[:taps], dpw=dpw, dfw=dfw)
    mix.after_swap(_run_exchange(mix.swap(), name="rs_swap_mix_0"))
    mix.after_chips(_run_exchange(mix.chips(), name="rs_chips_mix_0"))
    big_mix[0] = mix.result(_run_exchange(mix.share(), name="rs_share_mix_0"))

    dmod_own = jnp.stack([small[l]["rowb"][:, :nmod, :] for l in range(nl)])
    dmod_all = _gather8(_rows128(dmod_own), name="gather_dmod").reshape(N_DEV, nl, nb, nmod * d)
    dmod_all = dmod_all.transpose(1, 0, 2, 3).reshape(nl, N_DEV * nb, nmod * d)
    dmod_cols = lax.dynamic_slice_in_dim(dmod_all, jm * acol, acol, axis=2)
    (g_ada_w, d_ada_w, nm_ada_w, nv_ada_w), _ = _ada_update(c_all, dmod_cols, ada_w, m_ada_w, v_ada_w, name="ada_update")

    def st(key, row=None):
        return jnp.stack([small[l][key] if row is None else small[l][key][row] for l in range(nl)])

    local = {
        "ada_b": dmod_own.sum(axis=1).reshape(nl, nmod * d),
        "pre_mix_g": st("rowd", 0), "post_mix_g": st("rowd", 1),
        "conv_b": st("rowc", 0), "conv_ln_g": st("rowc", 1), "conv_ln_b": st("rowc", 2),
        "pool_w": st("dpw"), "pool_scale": st("rowc", 3),
        "pre_ffn_g": st("rowd", 2), "post_ffn_g": st("rowd", 3),
        "ffn_conv_b": st("dfw", 3), "conv_w": st("dcw"), "ffn_conv_w": jnp.stack([small[l]["dfw"][:3] for l in range(nl)]),
    }
    names = list(local)
    sizes = [local[k].size for k in names]
    pad = -sum(sizes) % (2 * SUBLANES * LANES)
    packed = jnp.concatenate([local[k].reshape(-1) for k in names] + [jnp.zeros((pad,), F32)])
    summed = _allreduce8(_rows128(packed), name="allreduce_small").reshape(-1)
    grads, off = {}, 0
    for k, sz in zip(names, sizes):
        grads[k] = summed[off:off + sz].reshape(local[k].shape)
        off += sz
    grads["conv_w"] = lax.dynamic_slice_in_dim(grads["conv_w"], jm * ccol, ccol, axis=2)
    grads["ffn_conv_w"] = lax.dynamic_slice_in_dim(grads["ffn_conv_w"], jm * fcol, fcol, axis=2)

    params = dict(ada_b=(ada_b, m_ada_b, v_ada_b), pre_mix_g=(pre_mix_g, m_pre_mix_g, v_pre_mix_g),
                  post_mix_g=(post_mix_g, m_post_mix_g, v_post_mix_g), conv_b=(conv_b, m_conv_b, v_conv_b),
                  conv_ln_g=(conv_ln_g, m_conv_ln_g, v_conv_ln_g), conv_ln_b=(conv_ln_b, m_conv_ln_b, v_conv_ln_b),
                  pool_w=(pool_w, m_pool_w, v_pool_w), pool_scale=(pool_scale, m_pool_scale, v_pool_scale),
                  pre_ffn_g=(pre_ffn_g, m_pre_ffn_g, v_pre_ffn_g), post_ffn_g=(post_ffn_g, m_post_ffn_g, v_post_ffn_g),
                  ffn_conv_b=(ffn_conv_b, m_ffn_conv_b, v_ffn_conv_b), conv_w=(conv_w, m_conv_w, v_conv_w),
                  ffn_conv_w=(ffn_conv_w, m_ffn_conv_w, v_ffn_conv_w))
    pack = lambda i, g=None: _rows128(jnp.concatenate([(grads[k] if g else params[k][i]).reshape(-1) for k in names]))
    sd, sm, sv = _adamw_flat(pack(0), pack(0, True), pack(1), pack(2), name="adamw_small")
    outs = {}
    off = 0
    for k in names:
        shape, sz = params[k][0].shape, params[k][0].size
        outs[k] = (grads[k],) + tuple(a.reshape(-1)[off:off + sz].reshape(shape) for a in (sd, sm, sv))
        off += sz

    outs["ada_w"] = (g_ada_w, d_ada_w, nm_ada_w, nv_ada_w)
    for k, w, m, v, gs in [("w_in", w_in, m_w_in, v_w_in, [big_mix[l][0] for l in range(nl)]),
                           ("w_out", w_out, m_w_out, v_w_out, [big_mix[l][1] for l in range(nl)]),
                           ("ffn_up", ffn_up, m_ffn_up, v_ffn_up, [big_mlp[l][0] for l in range(nl)]),
                           ("ffn_down", ffn_down, m_ffn_down, v_ffn_down, [big_mlp[l][1] for l in range(nl)])]:
        outs[k] = tuple(_adamw_layers(w, m, v, gs, name=f"adamw_{k}"))

    order = ["ada_w", "ada_b", "pre_mix_g", "post_mix_g", "w_in", "conv_w", "conv_b", "conv_ln_g", "conv_ln_b", "pool_w",
             "pool_scale", "w_out", "pre_ffn_g", "post_ffn_g", "ffn_up", "ffn_conv_w", "ffn_conv_b", "ffn_down"]
    return (loss, dx) + tuple(outs[k][i] for i in range(4) for k in order)
```

```python
import functools

import jax
import jax.numpy as jnp
from jax import lax
from jax.experimental import pallas as pl
from jax.experimental.pallas import tpu as pltpu

F32 = jnp.float32
BF16 = jnp.bfloat16
MESH = pl.DeviceIdType.MESH

EPS = 1e-6
POOL_WINDOWS = (2, 4, 8, 16)
ADAM_LR = 0.001
ADAM_B1 = 0.9
ADAM_B2 = 0.999
ADAM_EPS = 1e-08
ADAM_WD = 0.01
ADAM_STEP = 10

N_CHIPS = 4
N_DEV = 8
LANES = 128
SUBLANES = 8
HALO = 32
FHALO = 8
VMEM_LIMIT = 60 * 1024 * 1024


def _pos():
    return lax.axis_index("x"), lax.axis_index("y"), lax.axis_index("c")


def _flip(v, f):
    return 1 - v if f else v


def _full(shape):
    nd = len(shape)
    return pl.BlockSpec(shape, lambda *_: (0,) * nd)


_ANY = pl.BlockSpec(memory_space=pl.ANY)
_VMEM = pl.BlockSpec(memory_space=pltpu.VMEM)


def _sigmoid(v):
    return 1.0 / (1.0 + jnp.exp(-v))


def _colsum(v):
    return jnp.sum(v, axis=0, keepdims=True)


def _gather8(v, name):
    r, ccols = v.shape

    def body(v_ref, out_ref, send_sems, recv_sems, local_sem):
        x, y, c = _pos()
        me = 4 * x + 2 * y + c
        mine = pltpu.make_async_copy(v_ref, out_ref.at[me], local_sem)
        mine.start()
        peers = [(_flip(x, (k >> 2) & 1), _flip(y, (k >> 1) & 1), _flip(c, k & 1)) for k in range(1, N_DEV)]
        sends = []
        for k, peer in enumerate(peers):
            cp = pltpu.make_async_remote_copy(src_ref=v_ref, dst_ref=out_ref.at[me], send_sem=send_sems.at[k],
                                              recv_sem=recv_sems.at[k], device_id=peer, device_id_type=MESH)
            cp.start()
            sends.append(cp)
        for k, peer in enumerate(peers):
            pidx = 4 * peer[0] + 2 * peer[1] + peer[2]
            pltpu.make_async_remote_copy(src_ref=v_ref, dst_ref=out_ref.at[pidx], send_sem=send_sems.at[k],
                                         recv_sem=recv_sems.at[k], device_id=peer, device_id_type=MESH).wait_recv()
        for cp in sends:
            cp.wait_send()
        mine.wait()

    return pl.pallas_call(
        body, name=name, out_shape=jax.ShapeDtypeStruct((N_DEV, r, ccols), v.dtype),
        in_specs=[_VMEM], out_specs=_VMEM,
        scratch_shapes=[pltpu.SemaphoreType.DMA((N_DEV - 1,)), pltpu.SemaphoreType.DMA((N_DEV - 1,)),
                        pltpu.SemaphoreType.DMA(())],
        compiler_params=pltpu.CompilerParams(vmem_limit_bytes=VMEM_LIMIT),
    )(v)


def _rows_to_owners(v, name):
    _, r, ccols = v.shape

    def body(v_ref, out_ref, send_sems, recv_sems, local_sem):
        x, y, c = _pos()
        jm = 2 * x + y
        mine = pltpu.make_async_copy(v_ref.at[2 * jm + c], out_ref.at[jm], local_sem)
        mine.start()
        peers, pjs = _chip_peers(x, y, c)
        sends = []
        for k, peer in enumerate(peers):
            cp = pltpu.make_async_remote_copy(src_ref=v_ref.at[2 * pjs[k] + c], dst_ref=out_ref.at[jm],
                                              send_sem=send_sems.at[k], recv_sem=recv_sems.at[k],
                                              device_id=peer, device_id_type=MESH)
            cp.start()
            sends.append(cp)
        for k, peer in enumerate(peers):
            pltpu.make_async_remote_copy(src_ref=v_ref.at[0], dst_ref=out_ref.at[pjs[k]], send_sem=send_sems.at[k],
                                         recv_sem=recv_sems.at[k], device_id=peer, device_id_type=MESH).wait_recv()
        for cp in sends:
            cp.wait_send()
        mine.wait()

    return pl.pallas_call(
        body, name=name, out_shape=jax.ShapeDtypeStruct((N_CHIPS, r, ccols), v.dtype),
        in_specs=[_VMEM], out_specs=_VMEM,
        scratch_shapes=[pltpu.SemaphoreType.DMA((N_CHIPS - 1,)), pltpu.SemaphoreType.DMA((N_CHIPS - 1,)),
                        pltpu.SemaphoreType.DMA(())],
        compiler_params=pltpu.CompilerParams(vmem_limit_bytes=VMEM_LIMIT),
    )(v)


def _allreduce8(v, name):
    r, ccols = v.shape
    h = r // 2

    def body(v_ref, out_ref, whole, half, send_sems, recv_sems):
        x, y, c = _pos()
        sib = (x, y, 1 - c)
        mine = pl.ds(pl.multiple_of(c * h, SUBLANES), h)
        theirs = pl.ds(pl.multiple_of((1 - c) * h, SUBLANES), h)

        def exchange(src, dst, k, peer):
            cp = pltpu.make_async_remote_copy(src_ref=src, dst_ref=dst, send_sem=send_sems.at[k],
                                              recv_sem=recv_sems.at[k], device_id=peer, device_id_type=MESH)
            cp.start()
            cp.wait()

        exchange(v_ref, whole, 0, sib)
        out_ref[...] = v_ref[...] + whole[...]
        for k, peer in ((1, (1 - x, y, c)), (2, (x, 1 - y, c))):
            exchange(out_ref.at[mine], half.at[k - 1], k, peer)
            out_ref[mine, :] = out_ref[mine, :] + half[k - 1]
        exchange(out_ref.at[mine], half.at[2], 3, sib)
        out_ref[theirs, :] = half[2]

    return pl.pallas_call(
        body, name=name, out_shape=jax.ShapeDtypeStruct((r, ccols), v.dtype),
        in_specs=[_VMEM], out_specs=_VMEM,
        scratch_shapes=[pltpu.VMEM((r, ccols), v.dtype), pltpu.VMEM((3, h, ccols), v.dtype),
                        pltpu.SemaphoreType.DMA((4,)), pltpu.SemaphoreType.DMA((4,))],
        compiler_params=pltpu.CompilerParams(vmem_limit_bytes=VMEM_LIMIT),
    )(v)


def _chip_peers(x, y, c):
    peers = [(_flip(x, (k >> 1) & 1), _flip(y, k & 1), c) for k in range(1, N_CHIPS)]
    return peers, [2 * p[0] + p[1] for p in peers]


class _Exchange:
    def __init__(self, ins, outs, aliases, n_sems, n_local, start, finish, mid=None, mid_at=1.0):
        self.ins, self.outs, self.aliases = list(ins), list(outs), dict(aliases)
        self.n_sems, self.n_local, self.start, self.finish = n_sems, n_local, start, finish
        self.mid, self.mid_at = mid, mid_at

    def scratch(self):
        return [pltpu.SemaphoreType.DMA((self.n_sems,)), pltpu.SemaphoreType.DMA((self.n_sems,)),
                pltpu.SemaphoreType.DMA((max(self.n_local, 1),))]


class _Sems:
    def __init__(self, send, recv, local, base=0, lbase=0):
        self.send, self.recv, self.loc, self.base, self.lbase = send, recv, local, base, lbase

    def shifted(self, by, lby):
        return _Sems(self.send, self.recv, self.loc, self.base + by, self.lbase + lby)

    def local(self, k):
        return self.loc.at[self.lbase + k]


def _remote(src, dst, sems, k, peer):
    return pltpu.make_async_remote_copy(src_ref=src, dst_ref=dst, send_sem=sems.send.at[sems.base + k],
                                        recv_sem=sems.recv.at[sems.base + k], device_id=peer, device_id_type=MESH)


def _combine(exs):
    ins = [a for ex in exs for a in ex.ins]
    outs = [o for ex in exs for o in ex.outs]
    aliases, spans, ni, no, ns, nloc = {}, [], 0, 0, 0, 0
    for ex in exs:
        aliases.update({ni + a: no + b for a, b in ex.aliases.items()})
        spans.append((ni, no, ns, nloc))
        ni, no, ns, nloc = ni + len(ex.ins), no + len(ex.outs), ns + ex.n_sems, nloc + ex.n_local

    def each(which):
        def run(ins_, outs_, sems):
            for ex, (i0, o0, s0, l0) in zip(exs, spans):
                stage = getattr(ex, which)
                if stage is not None:
                    stage(ins_[i0:i0 + len(ex.ins)], outs_[o0:o0 + len(ex.outs)], sems.shifted(s0, l0))
        return run

    mids = [ex.mid_at for ex in exs if ex.mid is not None]
    return _Exchange(ins, outs, aliases, ns, nloc, each("start"), each("finish"),
                     mid=each("mid") if mids else None, mid_at=max(mids) if mids else 1.0)


def _gather(shards, mid_at=1.0):
    n = len(shards)
    per = N_CHIPS - 1
    halves = [s.reshape(2, s.shape[0] // 2, s.shape[1]) for s in shards]

    def copies(ins, outs, sems):
        x, y, c = _pos()
        jm = 2 * x + y
        sib = (x, y, 1 - c)
        peers, pjs = _chip_peers(x, y, c)
        sends, recvs, passes, passed = [], [], [], []
        for a in range(n):
            for k, peer in enumerate(peers):
                landed, theirs = outs[a].at[pjs[k], c], outs[a].at[pjs[k], 1 - c]
                sends.append(_remote(ins[a].at[c], outs[a].at[jm, c], sems, 2 * (a * per + k), peer))
                recvs.append(_remote(landed, landed, sems, 2 * (a * per + k), peer))
                passes.append(_remote(landed, landed, sems, 2 * (a * per + k) + 1, sib))
                passed.append(_remote(theirs, theirs, sems, 2 * (a * per + k) + 1, sib))
        return sends, recvs, passes, passed

    def local(ins, outs, sems):
        x, y, _ = _pos()
        return [pltpu.make_async_copy(ins[a], outs[a].at[2 * x + y], sems.local(a)) for a in range(n)]

    def start(ins, outs, sems):
        for cp in local(ins, outs, sems) + copies(ins, outs, sems)[0]:
            cp.start()

    def mid(ins, outs, sems):
        _, recvs, passes, _ = copies(ins, outs, sems)
        for got, fwd in zip(recvs, passes):
            got.wait_recv()
            fwd.start()

    def finish(ins, outs, sems):
        sends, _, passes, passed = copies(ins, outs, sems)
        for cp in passed:
            cp.wait_recv()
        for cp in sends + passes:
            cp.wait_send()
        for cp in local(ins, outs, sems):
            cp.wait()

    outs = [jax.ShapeDtypeStruct((N_CHIPS,) + h.shape, h.dtype) for h in halves]
    return _Exchange(halves, outs, {}, 2 * n * per, n, start, finish, mid=mid, mid_at=mid_at)


def _whole(gathered):
    return [g.reshape(g.shape[0], g.shape[1] * g.shape[2], g.shape[3]) for g in gathered]


def _swap_halves(gs):
    n = len(gs)
    halves = [g.reshape(g.shape[0], 2, g.shape[1] // 2, g.shape[2]) for g in gs]

    def copies(ins, outs, sems):
        x, y, c = _pos()
        sib = (x, y, 1 - c)
        return [_remote(ins[a].at[:, 1 - c], outs[a], sems, a, sib) for a in range(n)]

    def start(ins, outs, sems):
        for cp in copies(ins, outs, sems):
            cp.start()

    def finish(ins, outs, sems):
        for cp in copies(ins, outs, sems):
            cp.wait()

    outs = [jax.ShapeDtypeStruct((g.shape[0], g.shape[1] // 2, g.shape[2]), g.dtype) for g in gs]
    return _Exchange(halves, outs, {}, n, 0, start, finish)


def _chip_exchange(ps):
    n = len(ps)
    per = N_CHIPS - 1

    def copies(ins, outs, sems):
        x, y, c = _pos()
        peers, pjs = _chip_peers(x, y, c)
        return [_remote(ins[a].at[pjs[k]], outs[a].at[k], sems, a * per + k, peer)
                for a in range(n) for k, peer in enumerate(peers)]

    def start(ins, outs, sems):
        for cp in copies(ins, outs, sems):
            cp.start()

    def finish(ins, outs, sems):
        for cp in copies(ins, outs, sems):
            cp.wait()

    outs = [jax.ShapeDtypeStruct((per,) + p.shape[1:], p.dtype) for p in ps]
    return _Exchange(ps, outs, {}, n * per, 0, start, finish)


def _collect_pieces(ps, r2s, pieces, layer):
    n = len(ps)
    others = pl.ds(1, N_CHIPS - 1)

    def copies(ins, outs, sems):
        x, y, c = _pos()
        jm = 2 * x + y
        sib = (x, y, 1 - c)
        local, sends, recvs = [], [], []
        for a in range(n):
            own, rest, out = ins[a].at[jm], ins[n + a], outs[a].at[layer]
            local += [pltpu.make_async_copy(own, out.at[c, 0], sems.local(2 * a)),
                      pltpu.make_async_copy(rest, out.at[c, others], sems.local(2 * a + 1))]
            sends += [_remote(own, out.at[c, 0], sems, 2 * a, sib), _remote(rest, out.at[c, others], sems, 2 * a + 1, sib)]
            recvs += [_remote(own, out.at[1 - c, 0], sems, 2 * a, sib), _remote(rest, out.at[1 - c, others], sems, 2 * a + 1, sib)]
        return local, sends, recvs

    def start(ins, outs, sems):
        local, sends, _ = copies(ins, outs, sems)
        for cp in local + sends:
            cp.start()

    def finish(ins, outs, sems):
        local, sends, recvs = copies(ins, outs, sems)
        for cp in recvs:
            cp.wait_recv()
        for cp in sends:
            cp.wait_send()
        for cp in local:
            cp.wait()

    outs = [jax.ShapeDtypeStruct(q.shape, q.dtype) for q in pieces]
    return _Exchange(list(ps) + list(r2s) + list(pieces), outs, {2 * n + a: a for a in range(n)}, 2 * n, 2 * n, start, finish)


def _run_exchange(ex, name):
    ni, no = len(ex.ins), len(ex.outs)

    def body(*refs):
        ins, outs, sems = refs[:ni], refs[ni:ni + no], _Sems(*refs[ni + no:])
        ex.start(ins, outs, sems)
        if ex.mid is not None:
            ex.mid(ins, outs, sems)
        ex.finish(ins, outs, sems)

    return pl.pallas_call(
        body, name=name, out_shape=ex.outs, in_specs=[_ANY] * ni, out_specs=[_ANY] * no,
        input_output_aliases=ex.aliases, scratch_shapes=ex.scratch(),
    )(*ex.ins)


def _grid_call(body, name, grid, in_specs, out_specs, out_shape, scratch_shapes, args, exchange=None):
    ni, no = len(in_specs), len(out_specs)
    params = pltpu.CompilerParams(dimension_semantics=("arbitrary",) * len(grid), vmem_limit_bytes=VMEM_LIMIT)
    if exchange is None:
        outs = pl.pallas_call(body, name=name, grid=grid, in_specs=in_specs, out_specs=out_specs, out_shape=out_shape,
                              scratch_shapes=scratch_shapes, compiler_params=params)(*args)
        return list(outs), []
    ex = exchange
    nci, nco = len(ex.ins), len(ex.outs)

    def hosted(*refs):
        cin = refs[ni:ni + nci]
        cout = refs[ni + nci + no:ni + nci + no + nco]
        sems = _Sems(*refs[len(refs) - 3:])
        main = refs[:ni] + refs[ni + nci:ni + nci + no] + refs[ni + nci + no + nco:len(refs) - 3]
        ids = [pl.program_id(a) for a in range(len(grid))]
        first = functools.reduce(lambda p, q: p & q, [i == 0 for i in ids])
        last = functools.reduce(lambda p, q: p & q, [i == g - 1 for i, g in zip(ids, grid)])

        @pl.when(first)
        def _():
            ex.start(cin, cout, sems)

        if ex.mid is not None:
            steps = functools.reduce(lambda p, q: p * q, grid)
            flat = functools.reduce(lambda p, q: p * q[1] + q[0], zip(ids[1:], grid[1:]), ids[0])

            @pl.when(flat == min(steps - 1, int(ex.mid_at * steps)))
            def _():
                ex.mid(cin, cout, sems)

        body(*main)

        @pl.when(last)
        def _():
            ex.finish(cin, cout, sems)

    outs = pl.pallas_call(
        hosted, name=name, grid=grid, in_specs=list(in_specs) + [_ANY] * nci, out_specs=list(out_specs) + [_ANY] * nco,
        out_shape=list(out_shape) + ex.outs, scratch_shapes=list(scratch_shapes) + ex.scratch(),
        input_output_aliases={ni + a: no + b for a, b in ex.aliases.items()}, compiler_params=params,
    )(*args, *ex.ins)
    return list(outs[:no]), list(outs[no:])


def _row_tile(rows, cols, itemsize, budget=2 * 1024 * 1024):
    best = None
    for t in range(16, rows + 1, 16):
        if rows % t == 0 and t * cols * itemsize <= budget:
            best = t
    return best if best is not None else rows


def _pair_sum(g, r1, cidx, name):
    nj, r, ccols = g.shape
    hr = r // 2
    tr = _row_tile(hr, ccols, 4)
    nt = hr // tr

    def body(c_ref, g_ref, r_ref, o_ref):
        o_ref[...] = (g_ref[...].astype(F32) + r_ref[...].astype(F32)).astype(o_ref.dtype)

    return pl.pallas_call(
        body, name=name, out_shape=jax.ShapeDtypeStruct((nj, hr, ccols), g.dtype),
        grid_spec=pltpu.PrefetchScalarGridSpec(
            num_scalar_prefetch=1, grid=(nj, nt),
            in_specs=[pl.BlockSpec((None, tr, ccols), lambda j, i, c_ref: (j, c_ref[0] * nt + i, 0)),
                      pl.BlockSpec((None, tr, ccols), lambda j, i, c_ref: (j, i, 0))],
            out_specs=pl.BlockSpec((None, tr, ccols), lambda j, i, c_ref: (j, i, 0))),
        compiler_params=pltpu.CompilerParams(dimension_semantics=("arbitrary", "arbitrary")),
    )(cidx, g, r1)


def _adam_math(w, g, m, v):
    m2 = ADAM_B1 * m + (1.0 - ADAM_B1) * g
    v2 = ADAM_B2 * v + (1.0 - ADAM_B2) * (g * g)
    m_hat = m2 / (1.0 - ADAM_B1 ** ADAM_STEP)
    v_hat = v2 / (1.0 - ADAM_B2 ** ADAM_STEP)
    delta = -ADAM_LR * (m_hat / (jnp.sqrt(v_hat) + ADAM_EPS) + ADAM_WD * w)
    return delta, m2, v2


def _adamw_layers(w, m, v, pieces, name):
    nl, r, ccols = w.shape
    hr = r // 2
    tr = _row_tile(hr, ccols, 4, budget=1024 * 1024)
    nth = hr // tr

    def body(w_ref, m_ref, v_ref, q_ref, go_ref, d_ref, mo_ref, vo_ref):
        g = q_ref[0].astype(F32)
        for k in range(1, N_CHIPS):
            g = g + q_ref[k].astype(F32)
        delta, m2, v2 = _adam_math(w_ref[...], g, m_ref[...], v_ref[...])
        go_ref[...] = g
        d_ref[...] = delta
        mo_ref[...] = m2
        vo_ref[...] = v2

    big = pl.BlockSpec((None, tr, ccols), lambda l, i: (l, i, 0))
    parts = pl.BlockSpec((None, None, N_CHIPS, tr, ccols), lambda l, i: (l, i // nth, 0, i % nth, 0))
    return _grid_call(body, name, (nl, 2 * nth), in_specs=[big, big, big, parts],
                      out_specs=[big, big, big, big], out_shape=[jax.ShapeDtypeStruct(w.shape, F32)] * 4,
                      scratch_shapes=[], args=(w, m, v, pieces))[0]


def _adamw_flat(w, g, m, v, name):
    r, ccols = w.shape

    def body(w_ref, g_ref, m_ref, v_ref, d_ref, mo_ref, vo_ref):
        delta, m2, v2 = _adam_math(w_ref[...], g_ref[...], m_ref[...], v_ref[...])
        d_ref[...] = delta
        mo_ref[...] = m2
        vo_ref[...] = v2

    return pl.pallas_call(
        body, name=name, out_shape=[jax.ShapeDtypeStruct((r, ccols), F32)] * 3,
        in_specs=[_VMEM] * 4, out_specs=[_VMEM] * 3,
        compiler_params=pltpu.CompilerParams(vmem_limit_bytes=VMEM_LIMIT),
    )(w, g, m, v)


def _ada_forward(c_all, ada_w, ada_b_cols, name):
    nl, d, ncols = ada_w.shape
    bg = c_all.shape[0]
    tn = 512 if ncols % 512 == 0 else ncols

    def body(c_ref, w_ref, b_ref, o_ref):
        cv = c_ref[...]
        ca = (cv * _sigmoid(cv)).astype(BF16)
        o_ref[...] = jnp.dot(ca, w_ref[...].astype(BF16), preferred_element_type=F32) + b_ref[...]

    return pl.pallas_call(
        body, name=name, out_shape=jax.ShapeDtypeStruct((nl, bg, ncols), F32),
        grid=(nl, ncols // tn),
        in_specs=[pl.BlockSpec((bg, d), lambda l, j: (0, 0)),
                  pl.BlockSpec((None, d, tn), lambda l, j: (l, 0, j)),
                  pl.BlockSpec((None, 1, tn), lambda l, j: (l, 0, j))],
        out_specs=pl.BlockSpec((None, bg, tn), lambda l, j: (l, 0, j)),
        compiler_params=pltpu.CompilerParams(dimension_semantics=("arbitrary", "arbitrary")),
    )(c_all, ada_w, ada_b_cols)


def _ada_update(c_all, dmod_cols, w, m, v, name, exchange=None):
    nl, d, ncols = w.shape
    bg = c_all.shape[0]
    tn = 512 if ncols % 512 == 0 else ncols

    def body(c_ref, dm_ref, w_ref, m_ref, v_ref, go_ref, d_ref, mo_ref, vo_ref):
        cv = c_ref[...]
        ca = (cv * _sigmoid(cv)).astype(BF16)
        g = lax.dot_general(ca, dm_ref[...].astype(BF16), (((0,), (0,)), ((), ())), preferred_element_type=F32)
        delta, m2, v2 = _adam_math(w_ref[...], g, m_ref[...], v_ref[...])
        go_ref[...] = g
        d_ref[...] = delta
        mo_ref[...] = m2
        vo_ref[...] = v2

    big = pl.BlockSpec((None, d, tn), lambda l, j: (l, 0, j))
    return _grid_call(
        body, name, (nl, ncols // tn),
        in_specs=[pl.BlockSpec((bg, d), lambda l, j: (0, 0)),
                  pl.BlockSpec((None, bg, tn), lambda l, j: (l, 0, j)), big, big, big],
        out_specs=[big, big, big, big], out_shape=[jax.ShapeDtypeStruct(w.shape, F32)] * 4,
        scratch_shapes=[], args=(c_all, dmod_cols, w, m, v), exchange=exchange)


def _load_weights(first, pairs, sems):
    @pl.when(first)
    def _():
        cps = [pltpu.make_async_copy(src, dst, sems.at[k]) for k, (src, dst) in enumerate(pairs)]
        for cp in cps:
            cp.start()
        for cp in cps:
            cp.wait()


def _ada_norm(xv, g, sc, sh):
    r = lax.rsqrt(jnp.mean(xv * xv, axis=-1, keepdims=True) + EPS)
    xn = xv * r
    return (xn * g) * (1.0 + sc) + sh, xn, r


def _ada_norm_bwd(dh, xn, r, g, sc):
    d_sh = _colsum(dh)
    d_sc = _colsum(dh * (xn * g))
    dxg = dh * (1.0 + sc)
    d_g = _colsum(dxg * xn)
    gd = dxg * g
    dx = r * (gd - xn * jnp.mean(gd * xn, axis=-1, keepdims=True))
    return dx, d_sh, d_sc, d_g


def _gated_residual_bwd(dxo, o, g_post, gt):
    r = lax.rsqrt(jnp.mean(o * o, axis=-1, keepdims=True) + EPS)
    on = o * r
    d_gt = _colsum(dxo * (on * g_post))
    dy = dxo * (1.0 + gt)
    d_gp = _colsum(dy * on)
    gd = dy * g_post
    do = r * (gd - on * jnp.mean(gd * on, axis=-1, keepdims=True))
    return do, d_gt, d_gp


def _seq_positions(i, tm, width):
    return i * tm + lax.broadcasted_iota(jnp.int32, (tm, width), 0)


def _fill_phases(ext, phases):
    rows = ext.shape[0]
    ev = ext[...]
    for r in range(1, SUBLANES):
        phases[r - 1] = pltpu.roll(ev, rows - r, axis=0)


def _shifted_rows(ext, phases, offset, n):
    q, r = divmod(offset, SUBLANES)
    if r == 0:
        return ext[pl.ds(q * SUBLANES, n), :]
    return phases[r - 1, pl.ds(q * SUBLANES, n), :]


def _rows_before(halo, cur, shift):
    e = jnp.concatenate([halo, cur], axis=0)
    return pltpu.roll(e, shift, axis=0)[halo.shape[0]:, :]


def _rows_after(cur, halo, shift):
    e = jnp.concatenate([cur, halo], axis=0)
    return pltpu.roll(e, e.shape[0] - shift, axis=0)[:cur.shape[0], :]


def _mixer_forward(x, mod, vec_d, vec_c, cw, pw, win_g, wout_g, taps, tm, name, exchange=None):
    nb, s, d = x.shape
    n = s // tm
    nj, _, dcol = win_g.shape
    din = nj * dcol
    dc = vec_c.shape[-1]
    dpool = din - 2 * dc
    dmix = dc + dpool
    ro = wout_g.shape[1]
    ngrp = dpool // LANES

    def body(x_ref, mod_ref, vd_ref, vc_ref, cw_ref, pw_ref, win_hbm, wout_hbm,
             xo_ref, h_ref, u_ref, ac_ref, dp_ref, z_ref, o_ref,
             win_v, wout_v, ext_a, ext_p, phases, sems):
        b, i = pl.program_id(0), pl.program_id(1)
        pairs = [(win_hbm.at[j], win_v.at[:, pl.ds(j * dcol, dcol)]) for j in range(nj)]
        pairs += [(wout_hbm.at[j], wout_v.at[pl.ds(j * ro, ro), :]) for j in range(nj)]
        _load_weights((b == 0) & (i == 0), pairs, sems)

        xv = x_ref[...]
        h, _, _ = _ada_norm(xv, vd_ref[0:1, :], mod_ref[1:2, :], mod_ref[0:1, :])
        hb = h.astype(BF16)
        h_ref[...] = hb
        u = jnp.dot(hb, win_v[...], preferred_element_type=F32)
        u_ref[...] = u.astype(BF16)
        ag = u[:, :dc] * _sigmoid(u[:, dc:2 * dc])
        up = u[:, 2 * dc:]

        @pl.when(i == 0)
        def _():
            ext_a[0:HALO, :] = jnp.zeros((HALO, dc), F32)
            ext_p[0:HALO, :] = jnp.zeros((HALO, dpool), F32)

        @pl.when(i > 0)
        def _():
            ext_a[0:HALO, :] = ext_a[tm:tm + HALO, :]
            ext_p[0:HALO, :] = ext_p[tm:tm + HALO, :]

        ext_a[HALO:HALO + tm, :] = ag
        ext_p[HALO:HALO + tm, :] = up

        acc = jnp.broadcast_to(vc_ref[0:1, :], (tm, dc))
        _fill_phases(ext_a, phases)
        for k in range(taps):
            acc = acc + cw_ref[k:k + 1, :] * _shifted_rows(ext_a, phases, HALO - (taps - 1) + k, tm)
        ac_ref[...] = acc.astype(BF16)
        mu = jnp.mean(acc, axis=-1, keepdims=True)
        xc = acc - mu
        var = jnp.mean(xc * xc, axis=-1, keepdims=True)
        al = (xc * lax.rsqrt(var + EPS)) * vc_ref[1:2, :] + vc_ref[2:3, :]
        a = al * _sigmoid(al)

        pos = _seq_positions(i, tm, LANES)
        parts = [a.astype(BF16)]
        for g in range(ngrp):
            w = POOL_WINDOWS[g]
            cols = slice(g * LANES, (g + 1) * LANES)
            sw = ext_p[:, cols]
            step = 1
            while step < w:
                sw = sw + pltpu.roll(sw, step, axis=0)
                step *= 2
            cnt = jnp.minimum(pos + 1, w).astype(F32)
            dg = (sw[HALO:, :] / cnt - up[:, cols]).astype(BF16)
            dp_ref[:, cols] = dg
            q = jnp.dot(dg, pw_ref[g], preferred_element_type=F32)
            parts.append((q * vc_ref[3:4, cols]).astype(BF16))
        z = jnp.concatenate(parts, axis=-1)
        z_ref[...] = z
        o = jnp.dot(z, wout_v[...], preferred_element_type=F32)
        o_ref[...] = o
        r2 = lax.rsqrt(jnp.mean(o * o, axis=-1, keepdims=True) + EPS)
        xo_ref[...] = xv + (1.0 + mod_ref[2:3, :]) * ((o * r2) * vd_ref[1:2, :])

    def tile(width):
        return pl.BlockSpec((None, tm, width), lambda b, i: (b, i, 0))

    return _grid_call(
        body, name, (nb, n),
        in_specs=[tile(d), pl.BlockSpec((None, 8, d), lambda b, i: (b, 0, 0)), _full(vec_d.shape), _full(vec_c.shape),
                  _full(cw.shape), _full(pw.shape), _ANY, _ANY],
        out_specs=[tile(d), tile(d), tile(din), tile(dc), tile(dpool), tile(dmix), tile(d)],
        out_shape=[jax.ShapeDtypeStruct((nb, s, d), F32), jax.ShapeDtypeStruct((nb, s, d), BF16),
                   jax.ShapeDtypeStruct((nb, s, din), BF16), jax.ShapeDtypeStruct((nb, s, dc), BF16),
                   jax.ShapeDtypeStruct((nb, s, dpool), BF16), jax.ShapeDtypeStruct((nb, s, dmix), BF16),
                   jax.ShapeDtypeStruct((nb, s, d), F32)],
        scratch_shapes=[pltpu.VMEM((d, din), BF16), pltpu.VMEM((dmix, d), BF16),
                        pltpu.VMEM((HALO + tm, dc), F32), pltpu.VMEM((HALO + tm, dpool), F32),
                        pltpu.VMEM((SUBLANES - 1, HALO + tm, dc), F32), pltpu.SemaphoreType.DMA((2 * nj,))],
        args=(x, mod, vec_d, vec_c, cw, pw, win_g, wout_g), exchange=exchange)


def _mixer_backward(dxo, x, o, u, ac, dpl, mod, vec_d, vec_c, cw, pw, win_g, wout_g, taps, tm, name, exchange=None):
    nb, s, d = x.shape
    n = s // tm
    nj, _, dcol = win_g.shape
    din = nj * dcol
    dc = vec_c.shape[-1]
    dpool = din - 2 * dc
    dmix = dc + dpool
    ro = wout_g.shape[1]
    ngrp = dpool // LANES
    rext = tm + HALO

    def body(dxo_ref, x_ref, o_ref, u_ref, ac_ref, dp_ref, mod_ref, vd_ref, vc_ref, cw_ref, pw_ref, win_hbm, wout_hbm,
             dx_ref, du_ref, dob_ref, rowd_ref, rowb_ref, rowc_ref, dcw_ref, dpw_ref,
             win_v, wout_v, ext_a, ext_p, phases, sems):
        b, i = pl.program_id(0), pl.program_id(1)
        first = (b == 0) & (i == 0)
        pairs = [(win_hbm.at[j], win_v.at[:, pl.ds(j * dcol, dcol)]) for j in range(nj)]
        pairs += [(wout_hbm.at[j], wout_v.at[pl.ds(j * ro, ro), :]) for j in range(nj)]
        _load_weights(first, pairs, sems)

        @pl.when(first)
        def _():
            rowd_ref[...] = jnp.zeros_like(rowd_ref)
            rowc_ref[...] = jnp.zeros_like(rowc_ref)
            dcw_ref[...] = jnp.zeros_like(dcw_ref)
            dpw_ref[...] = jnp.zeros_like(dpw_ref)

        @pl.when(i == 0)
        def _():
            rowb_ref[...] = jnp.zeros_like(rowb_ref)
            ext_a[tm:rext, :] = jnp.zeros((HALO, dc), F32)
            ext_p[tm:rext, :] = jnp.zeros((HALO, dpool), F32)

        @pl.when(i > 0)
        def _():
            ext_a[tm:rext, :] = ext_a[0:HALO, :]
            ext_p[tm:rext, :] = ext_p[0:HALO, :]

        g_pre, g_post = vd_ref[0:1, :], vd_ref[1:2, :]
        sh, sc, gt = mod_ref[0:1, :], mod_ref[1:2, :], mod_ref[2:3, :]
        do, d_gt, d_gp = _gated_residual_bwd(dxo_ref[...], o_ref[...], g_post, gt)
        dob = do.astype(BF16)
        dob_ref[...] = dob
        dz = lax.dot_general(dob, wout_v[...], (((1,), (1,)), ((), ())), preferred_element_type=F32)

        acv = ac_ref[...].astype(F32)
        mu = jnp.mean(acv, axis=-1, keepdims=True)
        xc = acv - mu
        rstd = lax.rsqrt(jnp.mean(xc * xc, axis=-1, keepdims=True) + EPS)
        an = xc * rstd
        lg = vc_ref[1:2, :]
        al = an * lg + vc_ref[2:3, :]
        sg = _sigmoid(al)
        dal = dz[:, :dc] * (sg * (1.0 + al * (1.0 - sg)))
        d_lg = _colsum(dal * an)
        d_lb = _colsum(dal)
        dan = dal * lg
        dac = rstd * (dan - jnp.mean(dan, axis=-1, keepdims=True) - an * jnp.mean(dan * an, axis=-1, keepdims=True))
        d_cb = _colsum(dac)
        ext_a[0:tm, :] = dac
        uv = u_ref[:, 0:dc].astype(F32)
        sgu = _sigmoid(u_ref[:, dc:2 * dc].astype(F32))
        ag = uv * sgu
        dag = jnp.zeros((tm, dc), F32)
        _fill_phases(ext_a, phases)
        for k in range(taps):
            sl = _shifted_rows(ext_a, phases, taps - 1 - k, tm)
            dag = dag + cw_ref[k:k + 1, :] * sl
            dcw_ref[k:k + 1, :] += _colsum(ag * sl)
        du_ref[:, 0:dc] = (dag * sgu).astype(BF16)
        du_ref[:, dc:2 * dc] = (dag * uv * (sgu * (1.0 - sgu))).astype(BF16)

        pos = _seq_positions(n - 1 - i, tm, LANES)
        d_ps = []
        for g in range(ngrp):
            w = POOL_WINDOWS[g]
            cols = slice(g * LANES, (g + 1) * LANES)
            gcols = slice(dc + g * LANES, dc + (g + 1) * LANES)
            dgb = dp_ref[:, cols]
            q = jnp.dot(dgb, pw_ref[g], preferred_element_type=F32)
            dpg = dz[:, gcols]
            d_ps.append(_colsum(dpg * q))
            dq = (dpg * vc_ref[3:4, cols]).astype(BF16)
            dpw_ref[g] += lax.dot_general(dgb, dq, (((0,), (0,)), ((), ())), preferred_element_type=F32)
            dd = lax.dot_general(dq, pw_ref[g], (((1,), (1,)), ((), ())), preferred_element_type=F32)
            cnt = jnp.minimum(pos + 1, w).astype(F32)
            ext_p[0:tm, cols] = dd / cnt
            sw = ext_p[:, cols]
            step = 1
            while step < w:
                sw = sw + pltpu.roll(sw, rext - step, axis=0)
                step *= 2
            du_ref[:, 2 * dc + g * LANES:2 * dc + (g + 1) * LANES] = (sw[0:tm, :] - dd).astype(BF16)
        rowc_ref[0:1, :] += d_cb
        rowc_ref[1:2, :] += d_lg
        rowc_ref[2:3, :] += d_lb
        rowc_ref[3:4, :] += jnp.concatenate(d_ps, axis=-1)

        dh = lax.dot_general(du_ref[...], win_v[...], (((1,), (1,)), ((), ())), preferred_element_type=F32)
        _, xn, r1 = _ada_norm(x_ref[...], g_pre, sc, sh)
        dxb, d_sh, d_sc, d_g = _ada_norm_bwd(dh, xn, r1, g_pre, sc)
        dx_ref[...] = dxo_ref[...] + dxb
        rowd_ref[0:1, :] += d_g
        rowd_ref[1:2, :] += d_gp
        rowb_ref[0:1, :] += d_sh
        rowb_ref[1:2, :] += d_sc
        rowb_ref[2:3, :] += d_gt

    def tile(width):
        return pl.BlockSpec((None, tm, width), lambda b, i: (b, n - 1 - i, 0))

    return _grid_call(
        body, name, (nb, n),
        in_specs=[tile(d), tile(d), tile(d), tile(din), tile(dc), tile(dpool),
                  pl.BlockSpec((None, 8, d), lambda b, i: (b, 0, 0)), _full(vec_d.shape), _full(vec_c.shape),
                  _full(cw.shape), _full(pw.shape), _ANY, _ANY],
        out_specs=[tile(d), tile(din), tile(d), _full((8, d)), pl.BlockSpec((None, 8, d), lambda b, i: (b, 0, 0)),
                   _full((8, dc)), _full((HALO, dc)), _full(pw.shape)],
        out_shape=[jax.ShapeDtypeStruct((nb, s, d), F32), jax.ShapeDtypeStruct((nb, s, din), BF16),
                   jax.ShapeDtypeStruct((nb, s, d), BF16), jax.ShapeDtypeStruct((8, d), F32),
                   jax.ShapeDtypeStruct((nb, 8, d), F32), jax.ShapeDtypeStruct((8, dc), F32),
                   jax.ShapeDtypeStruct((HALO, dc), F32), jax.ShapeDtypeStruct(pw.shape, F32)],
        scratch_shapes=[pltpu.VMEM((d, din), BF16), pltpu.VMEM((dmix, d), BF16),
                        pltpu.VMEM((rext, dc), F32), pltpu.VMEM((rext, dpool), F32),
                        pltpu.VMEM((SUBLANES - 1, rext, dc), F32), pltpu.SemaphoreType.DMA((2 * nj,))],
        args=(dxo, x, o, u, ac, dpl, mod, vec_d, vec_c, cw, pw, win_g, wout_g), exchange=exchange)


def _ffn_forward(x, mod, vec_d, fw, wup_g, wdn_g, tm, name, exchange=None):
    nb, s, d = x.shape
    n = s // tm
    nj, _, ucol = wup_g.shape
    f2 = nj * ucol
    dff = f2 // 2
    rd = wdn_g.shape[1]
    nq = nj // 2
    cs = dff // nq

    def body(x_ref, mod_ref, vd_ref, fw_ref, wup_hbm, wdn_hbm,
             xo_ref, h_ref, u_ref, uc_ref, hid_ref, o_ref,
             wup_v, wdn_v, prev_u, sems):
        b, i = pl.program_id(0), pl.program_id(1)
        pairs = [(wup_hbm.at[j], wup_v.at[:, pl.ds(j * ucol, ucol)]) for j in range(nj)]
        pairs += [(wdn_hbm.at[j], wdn_v.at[pl.ds(j * rd, rd), :]) for j in range(nj)]
        _load_weights((b == 0) & (i == 0), pairs, sems)

        @pl.when(i == 0)
        def _():
            prev_u[...] = jnp.zeros_like(prev_u)

        xv = x_ref[...]
        h, _, _ = _ada_norm(xv, vd_ref[2:3, :], mod_ref[4:5, :], mod_ref[3:4, :])
        hb = h.astype(BF16)
        h_ref[...] = hb

        def conv(cols):
            uc = jnp.dot(hb, wup_v[:, cols], preferred_element_type=F32)
            u_ref[:, cols] = uc.astype(BF16)
            before = prev_u[:, cols]
            prev_u[:, cols] = uc[tm - FHALO:, :]
            out = (fw_ref[3:4, cols] + fw_ref[2:3, cols] * uc + fw_ref[1:2, cols] * _rows_before(before, uc, 1)
                   + fw_ref[0:1, cols] * _rows_before(before, uc, 2))
            uc_ref[:, cols] = out.astype(BF16)
            return out

        o = jnp.zeros((tm, d), F32)
        for q in range(nq):
            val = conv(pl.ds(q * cs, cs))
            gate = conv(pl.ds(dff + q * cs, cs))
            hid = ((gate * _sigmoid(gate)) * val).astype(BF16)
            hid_ref[:, pl.ds(q * cs, cs)] = hid
            o = o + jnp.dot(hid, wdn_v[pl.ds(q * cs, cs), :], preferred_element_type=F32)
        o_ref[...] = o
        r2 = lax.rsqrt(jnp.mean(o * o, axis=-1, keepdims=True) + EPS)
        xo_ref[...] = xv + (1.0 + mod_ref[5:6, :]) * ((o * r2) * vd_ref[3:4, :])

    def tile(width):
        return pl.BlockSpec((None, tm, width), lambda b, i: (b, i, 0))

    return _grid_call(
        body, name, (nb, n),
        in_specs=[tile(d), pl.BlockSpec((None, 8, d), lambda b, i: (b, 0, 0)), _full(vec_d.shape), _full(fw.shape),
                  _ANY, _ANY],
        out_specs=[tile(d), tile(d), tile(f2), tile(f2), tile(dff), tile(d)],
        out_shape=[jax.ShapeDtypeStruct((nb, s, d), F32), jax.ShapeDtypeStruct((nb, s, d), BF16),
                   jax.ShapeDtypeStruct((nb, s, f2), BF16), jax.ShapeDtypeStruct((nb, s, f2), BF16),
                   jax.ShapeDtypeStruct((nb, s, dff), BF16), jax.ShapeDtypeStruct((nb, s, d), F32)],
        scratch_shapes=[pltpu.VMEM((d, f2), BF16), pltpu.VMEM((dff, d), BF16),
                        pltpu.VMEM((FHALO, f2), F32), pltpu.SemaphoreType.DMA((2 * nj,))],
        args=(x, mod, vec_d, fw, wup_g, wdn_g), exchange=exchange)


def _ffn_backward(dxo, x, o, u, uc, mod, vec_d, fw, wup_g, wdn_g, tm, name, exchange=None):
    nb, s, d = x.shape
    n = s // tm
    nj, _, ucol = wup_g.shape
    f2 = nj * ucol
    dff = f2 // 2
    rd = wdn_g.shape[1]
    nq = nj // 2
    cs = dff // nq

    def body(dxo_ref, x_ref, o_ref, u_ref, uc_ref, mod_ref, vd_ref, fw_ref, wup_hbm, wdn_hbm,
             dx_ref, du_ref, dob_ref, rowd_ref, rowb_ref, dfw_ref,
             wup_v, wdn_v, next_d, sems):
        b, i = pl.program_id(0), pl.program_id(1)
        first = (b == 0) & (i == 0)
        pairs = [(wup_hbm.at[j], wup_v.at[:, pl.ds(j * ucol, ucol)]) for j in range(nj)]
        pairs += [(wdn_hbm.at[j], wdn_v.at[pl.ds(j * rd, rd), :]) for j in range(nj)]
        _load_weights(first, pairs, sems)

        @pl.when(first)
        def _():
            rowd_ref[...] = jnp.zeros_like(rowd_ref)
            dfw_ref[...] = jnp.zeros_like(dfw_ref)

        @pl.when(i == 0)
        def _():
            rowb_ref[...] = jnp.zeros_like(rowb_ref)
            next_d[...] = jnp.zeros_like(next_d)

        g_pre, g_post = vd_ref[2:3, :], vd_ref[3:4, :]
        sh, sc, gt = mod_ref[3:4, :], mod_ref[4:5, :], mod_ref[5:6, :]
        do, d_gt, d_gp = _gated_residual_bwd(dxo_ref[...], o_ref[...], g_post, gt)
        dob = do.astype(BF16)
        dob_ref[...] = dob

        def conv_bwd(cols, duc):
            uc = u_ref[:, cols].astype(F32)
            after = next_d[:, cols]
            next_d[:, cols] = duc[0:FHALO, :]
            d1 = _rows_after(duc, after, 1)
            d2 = _rows_after(duc, after, 2)
            dfw_ref[3:4, cols] += _colsum(duc)
            dfw_ref[2:3, cols] += _colsum(uc * duc)
            dfw_ref[1:2, cols] += _colsum(uc * d1)
            dfw_ref[0:1, cols] += _colsum(uc * d2)
            ob = (fw_ref[2:3, cols] * duc + fw_ref[1:2, cols] * d1 + fw_ref[0:1, cols] * d2).astype(BF16)
            du_ref[:, cols] = ob
            return lax.dot_general(ob, wup_v[:, cols], (((1,), (1,)), ((), ())), preferred_element_type=F32)

        dh = jnp.zeros((tm, d), F32)
        for q in range(nq):
            vcols = pl.ds(q * cs, cs)
            gcols = pl.ds(dff + q * cs, cs)
            dhid = lax.dot_general(dob, wdn_v[vcols, :], (((1,), (1,)), ((), ())), preferred_element_type=F32)
            val = uc_ref[:, vcols].astype(F32)
            gate = uc_ref[:, gcols].astype(F32)
            sg = _sigmoid(gate)
            act = gate * sg
            dval = dhid * act
            dgate = (dhid * val) * (sg + act * (1.0 - sg))
            dh = dh + conv_bwd(vcols, dval)
            dh = dh + conv_bwd(gcols, dgate)

        _, xn, r1 = _ada_norm(x_ref[...], g_pre, sc, sh)
        dxb, d_sh, d_sc, d_g = _ada_norm_bwd(dh, xn, r1, g_pre, sc)
        dx_ref[...] = dxo_ref[...] + dxb
        rowd_ref[2:3, :] += d_g
        rowd_ref[3:4, :] += d_gp
        rowb_ref[3:4, :] += d_sh
        rowb_ref[4:5, :] += d_sc
        rowb_ref[5:6, :] += d_gt

    def tile(width):
        return pl.BlockSpec((None, tm, width), lambda b, i: (b, n - 1 - i, 0))

    return _grid_call(
        body, name, (nb, n),
        in_specs=[tile(d), tile(d), tile(d), tile(f2), tile(f2), pl.BlockSpec((None, 8, d), lambda b, i: (b, 0, 0)),
                  _full(vec_d.shape), _full(fw.shape), _ANY, _ANY],
        out_specs=[tile(d), tile(f2), tile(d), _full((8, d)), pl.BlockSpec((None, 8, d), lambda b, i: (b, 0, 0)),
                   _full(fw.shape)],
        out_shape=[jax.ShapeDtypeStruct((nb, s, d), F32), jax.ShapeDtypeStruct((nb, s, f2), BF16),
                   jax.ShapeDtypeStruct((nb, s, d), BF16), jax.ShapeDtypeStruct((8, d), F32),
                   jax.ShapeDtypeStruct((nb, 8, d), F32), jax.ShapeDtypeStruct(fw.shape, F32)],
        scratch_shapes=[pltpu.VMEM((d, f2), BF16), pltpu.VMEM((dff, d), BF16),
                        pltpu.VMEM((FHALO, f2), F32), pltpu.SemaphoreType.DMA((2 * nj,))],
        args=(dxo, x, o, u, uc, mod, vec_d, fw, wup_g, wdn_g), exchange=exchange)


def _weight_grad(a, b, nblk, split, tt, name, exchange=None):
    t, ka = a.shape
    nb_ = b.shape[1]
    nk = t // tt
    if split == "cols":
        wa, wb, grid = ka, nb_ // nblk, (1, nk)
        a_spec = pl.BlockSpec((tt, ka), lambda j, k: (k, 0))
        b_spec = pl.BlockSpec((tt, nb_), lambda j, k: (k, 0))
        o_spec = pl.BlockSpec((nblk, wa, wb), lambda j, k: (0, 0, 0))
        acc_shape = (ka, nb_)
    elif split == "b":
        wa, wb, grid = ka, nb_ // nblk, (nblk, nk)
        a_spec = pl.BlockSpec((tt, wa), lambda j, k: (k, 0))
        b_spec = pl.BlockSpec((tt, wb), lambda j, k: (k, j))
        o_spec = pl.BlockSpec((None, wa, wb), lambda j, k: (j, 0, 0))
        acc_shape = (wa, wb)
    else:
        wa, wb, grid = ka // nblk, nb_, (nblk, nk)
        a_spec = pl.BlockSpec((tt, wa), lambda j, k: (k, j))
        b_spec = pl.BlockSpec((tt, wb), lambda j, k: (k, 0))
        o_spec = pl.BlockSpec((None, wa, wb), lambda j, k: (j, 0, 0))
        acc_shape = (wa, wb)

    def body(a_ref, b_ref, o_ref, acc):
        k = pl.program_id(1)
        prod = lax.dot_general(a_ref[...], b_ref[...], (((0,), (0,)), ((), ())), preferred_element_type=F32)

        @pl.when(k == 0)
        def _():
            acc[...] = prod

        @pl.when(k > 0)
        def _():
            acc[...] += prod

        @pl.when(k == nk - 1)
        def _():
            if split == "cols":
                for j in range(nblk):
                    o_ref[j] = acc[:, j * wb:(j + 1) * wb].astype(o_ref.dtype)
            else:
                o_ref[...] = acc[...].astype(o_ref.dtype)

    outs, exo = _grid_call(body, name, grid, in_specs=[a_spec, b_spec], out_specs=[o_spec],
                           out_shape=[jax.ShapeDtypeStruct((nblk, wa, wb), BF16)],
                           scratch_shapes=[pltpu.VMEM(acc_shape, F32)], args=(a, b), exchange=exchange)
    return outs[0], exo


def _loss_grad(y, tgt, tm, name):
    nb, s, d = y.shape
    n = s // tm

    def body(y_ref, t_ref, dy_ref, sq_ref):
        @pl.when((pl.program_id(0) == 0) & (pl.program_id(1) == 0))
        def _():
            sq_ref[...] = jnp.zeros_like(sq_ref)

        e = y_ref[...] - t_ref[...]
        dy_ref[...] = e * (1.0 / d)
        sq_ref[0:1, :] += _colsum(e * e)

    tile = pl.BlockSpec((None, tm, d), lambda b, i: (b, i, 0))
    return pl.pallas_call(
        body, name=name, out_shape=[jax.ShapeDtypeStruct((nb, s, d), F32), jax.ShapeDtypeStruct((8, d), F32)],
        grid=(nb, n), in_specs=[tile, tile], out_specs=[tile, _full((8, d))],
        compiler_params=pltpu.CompilerParams(dimension_semantics=("arbitrary", "arbitrary")),
    )(y, tgt)


def _rows128(a):
    return a.reshape(-1, LANES)


class _ReduceScatter:
    def __init__(self, gs, cidx, tag):
        self.gs, self.cidx, self.tag = gs, cidx, tag

    def swap(self):
        return _swap_halves(self.gs)

    def after_swap(self, r1):
        self.ps = [_pair_sum(g, r, self.cidx, name=f"rs_pair_{self.tag}_{a}") for a, (g, r) in enumerate(zip(self.gs, r1))]

    def chips(self):
        return _chip_exchange(self.ps)

    def after_chips(self, r2):
        self.r2 = r2

    def collect(self, pieces, layer):
        return _collect_pieces(self.ps, self.r2, pieces, layer)


def kernel(x, c, ada_w, ada_b, pre_mix_g, post_mix_g, w_in, conv_w, conv_b, conv_ln_g, conv_ln_b, pool_w, pool_scale, w_out, pre_ffn_g, post_ffn_g, ffn_up, ffn_conv_w, ffn_conv_b, ffn_down, loss_target, m_ada_w, m_ada_b, m_pre_mix_g, m_post_mix_g, m_w_in, m_conv_w, m_conv_b, m_conv_ln_g, m_conv_ln_b, m_pool_w, m_pool_scale, m_w_out, m_pre_ffn_g, m_post_ffn_g, m_ffn_up, m_ffn_conv_w, m_ffn_conv_b, m_ffn_down, v_ada_w, v_ada_b, v_pre_mix_g, v_post_mix_g, v_w_in, v_conv_w, v_conv_b, v_conv_ln_g, v_conv_ln_b, v_pool_w, v_pool_scale, v_w_out, v_pre_ffn_g, v_post_ffn_g, v_ffn_up, v_ffn_conv_w, v_ffn_conv_b, v_ffn_down):
    nb, s, d = x.shape
    nl = w_in.shape[0]
    taps = conv_w.shape[1]
    ccol = conv_w.shape[2]
    dc = conv_b.shape[1]
    fcol = ffn_conv_w.shape[2]
    f2 = ffn_conv_b.shape[1]
    nmod = ada_b.shape[1] // d
    acol = ada_w.shape[2]
    tm = min(256, s)
    tm_mix = min(512, s)
    tt = min(2048, (nb * s) // 2)

    xi, yi, ci = _pos()
    jm = 2 * xi + yi
    cidx = jnp.reshape(ci, (1,)).astype(jnp.int32)

    n_cw, n_fw, n_c = nl * taps * ccol, nl * 3 * fcol, nb * d
    packed = jnp.concatenate([conv_w.reshape(-1), ffn_conv_w.reshape(-1), c.reshape(-1)])
    got = _gather8(_rows128(packed), name="gather_small").reshape(N_DEV, -1)
    chips = got[0::2]
    cw_full = chips[:, :n_cw].reshape(N_CHIPS, nl, taps, ccol).transpose(1, 2, 0, 3).reshape(nl, taps, dc)
    fw_full = chips[:, n_cw:n_cw + n_fw].reshape(N_CHIPS, nl, 3, fcol).transpose(1, 2, 0, 3).reshape(nl, 3, f2)
    c_all = got[:, n_cw + n_fw:].reshape(N_DEV * nb, d)

    ada_b_cols = lax.dynamic_slice_in_dim(ada_b, jm * acol, acol, axis=1).reshape(nl, 1, acol)
    mod_cols = _ada_forward(c_all, ada_w, ada_b_cols, name="ada_forward")
    by_owner = mod_cols.reshape(nl, N_DEV, nb, acol).transpose(1, 0, 2, 3).reshape(N_DEV, -1, LANES)
    mod_own = _rows_to_owners(by_owner, name="mod_to_owners").reshape(N_CHIPS, nl, nb, acol)
    mod_own = mod_own.transpose(1, 2, 0, 3).reshape(nl, nb, nmod, d)
    mod_own = jnp.pad(mod_own, ((0, 0), (0, 0), (0, 8 - nmod), (0, 0)))

    vec_d = jnp.stack([pre_mix_g, post_mix_g, pre_ffn_g, post_ffn_g], axis=1)
    vec_c = jnp.stack([conv_b, conv_ln_g, conv_ln_b, pool_scale], axis=1)
    cw_pad = jnp.pad(cw_full, ((0, 0), (0, HALO - taps), (0, 0)))
    fw_rows = jnp.concatenate([fw_full, ffn_conv_b[:, None, :], jnp.zeros((nl, 4, f2), F32)], axis=1)
    pw_b = pool_w.astype(BF16)

    win_b, wout_b, wup_b, wdn_b = (w.astype(BF16) for w in (w_in, w_out, ffn_up, ffn_down))

    def others(l):
        return [win_b[l], wout_b[l], wdn_b[l]]

    win_g, wout_g, wdn_g = _whole(_run_exchange(_gather(others(0)), name="gather_0"))
    saved = []
    xs = x
    for l in range(nl):
        (x1, h1, u1, ac1, dp1, z1, o1), got = _mixer_forward(
            xs, mod_own[l], vec_d[l], vec_c[l], cw_pad[l], pw_b[l], win_g, wout_g, taps, tm_mix, name=f"mixer_fwd_{l}",
            exchange=_gather([wup_b[l]], mid_at=0.9))
        wup_g, = _whole(got)
        (x2, h2, u2, uc2, hid2, o2), nxt = _ffn_forward(
            x1, mod_own[l], vec_d[l], fw_rows[l], wup_g, wdn_g, tm, name=f"ffn_fwd_{l}",
            exchange=_gather(others(l + 1), mid_at=0.6) if l + 1 < nl else None)
        saved.append((xs, h1, u1, ac1, dp1, z1, o1, x1, h2, u2, uc2, hid2, o2, win_g, wout_g, wup_g, wdn_g))
        if l + 1 < nl:
            win_g, wout_g, wdn_g = _whole(nxt)
        xs = x2

    dx, sq = _loss_grad(xs, loss_target, tm, name="loss_grad")
    loss = lax.psum(0.5 * jnp.sum(sq) / d, ("x", "y", "c"))

    flat = lambda a: a.reshape(nb * s, a.shape[-1])
    small = [None] * nl

    def pieces_like(w):
        return lax.empty((nl, 2, N_CHIPS, w.shape[1] // 2, w.shape[2]), BF16)

    q_mlp = [pieces_like(ffn_up), pieces_like(ffn_down)]
    q_mix = [pieces_like(w_in), pieces_like(w_out)]
    mlp = mix = None
    for l in reversed(range(nl)):
        x0, h1, u1, ac1, dp1, z1, o1, x1, h2, u2, uc2, hid2, o2, win_g, wout_g, wup_g, wdn_g = saved[l]
        (dx, du2, do2, rowd2, rowb2, dfw), got = _ffn_backward(
            dx, x1, o2, u2, uc2, mod_own[l], vec_d[l], fw_rows[l], wup_g, wdn_g, tm, name=f"ffn_bwd_{l}",
            exchange=_combine([mlp.chips(), mix.swap()]) if mlp else None)
        if mlp:
            mlp.after_chips(got[:2])
            mix.after_swap(got[2:])
        g_up, got = _weight_grad(flat(h2), flat(du2), N_CHIPS, "b", tt, name=f"grad_ffn_up_{l}",
                                 exchange=_combine([mlp.collect(q_mlp, l + 1), mix.chips()]) if mlp else None)
        if mlp:
            q_mlp = got[:2]
            mix.after_chips(got[2:])
        g_dn, _ = _weight_grad(flat(hid2), flat(do2), 2, "a", tt, name=f"grad_ffn_down_{l}")
        mlp_above, mlp = mlp, _ReduceScatter([g_up, g_dn.reshape(N_CHIPS, -1, d)], cidx, f"mlp_{l}")
        if l == 0:
            mlp.after_swap(_run_exchange(mlp.swap(), name="rs_swap_mlp_0"))
        first = mlp.swap() if l > 0 else mlp.chips()
        (dx, du1, do1, rowd1, rowb1, rowc, dcw, dpw), got = _mixer_backward(
            dx, x0, o1, u1, ac1, dp1, mod_own[l], vec_d[l], vec_c[l], cw_pad[l], pw_b[l], win_g, wout_g, taps, tm_mix,
            name=f"mixer_bwd_{l}", exchange=_combine([first, mix.collect(q_mix, l + 1)]) if mlp_above else first)
        if l > 0:
            mlp.after_swap(got[:2])
        else:
            mlp.after_chips(got[:2])
        if mlp_above:
            q_mix = got[2:]
        g_in, got = _weight_grad(flat(h1), flat(du1), N_CHIPS, "cols", tt, name=f"grad_w_in_{l}",
                                 exchange=mlp.collect(q_mlp, 0) if l == 0 else None)
        if l == 0:
            q_mlp = got
        g_out, _ = _weight_grad(flat(z1), flat(do1), 1, "cols", tt, name=f"grad_w_out_{l}")
        mix = _ReduceScatter([g_in, g_out.reshape(N_CHIPS, -1, d)], cidx, f"mix_{l}")
        small[l] = dict(rowd=rowd1 + rowd2, rowb=rowb1 + rowb2, rowc=rowc, dcw=dcw[:taps], dpw=dpw, dfw=dfw)
    mix.after_swap(_run_exchange(mix.swap(), name="rs_swap_mix_0"))
    mix.after_chips(_run_exchange(mix.chips(), name="rs_chips_mix_0"))
    q_mix = _run_exchange(mix.collect(q_mix, 0), name="rs_collect_mix_0")

    dmod_own = jnp.stack([small[l]["rowb"][:, :nmod, :] for l in range(nl)])
    dmod_all = _gather8(_rows128(dmod_own), name="gather_dmod").reshape(N_DEV, nl, nb, nmod * d)
    dmod_all = dmod_all.transpose(1, 0, 2, 3).reshape(nl, N_DEV * nb, nmod * d)
    dmod_cols = lax.dynamic_slice_in_dim(dmod_all, jm * acol, acol, axis=2)
    (g_ada_w, d_ada_w, nm_ada_w, nv_ada_w), _ = _ada_update(c_all, dmod_cols, ada_w, m_ada_w, v_ada_w, name="ada_update")

    def st(key, row=None):
        return jnp.stack([small[l][key] if row is None else small[l][key][row] for l in range(nl)])

    local = {
        "ada_b": dmod_own.sum(axis=1).reshape(nl, nmod * d),
        "pre_mix_g": st("rowd", 0), "post_mix_g": st("rowd", 1),
        "conv_b": st("rowc", 0), "conv_ln_g": st("rowc", 1), "conv_ln_b": st("rowc", 2),
        "pool_w": st("dpw"), "pool_scale": st("rowc", 3),
        "pre_ffn_g": st("rowd", 2), "post_ffn_g": st("rowd", 3),
        "ffn_conv_b": st("dfw", 3), "conv_w": st("dcw"), "ffn_conv_w": jnp.stack([small[l]["dfw"][:3] for l in range(nl)]),
    }
    names = list(local)
    sizes = [local[k].size for k in names]
    pad = -sum(sizes) % (2 * SUBLANES * LANES)
    packed = jnp.concatenate([local[k].reshape(-1) for k in names] + [jnp.zeros((pad,), F32)])
    summed = _allreduce8(_rows128(packed), name="allreduce_small").reshape(-1)
    grads, off = {}, 0
    for k, sz in zip(names, sizes):
        grads[k] = summed[off:off + sz].reshape(local[k].shape)
        off += sz
    grads["conv_w"] = lax.dynamic_slice_in_dim(grads["conv_w"], jm * ccol, ccol, axis=2)
    grads["ffn_conv_w"] = lax.dynamic_slice_in_dim(grads["ffn_conv_w"], jm * fcol, fcol, axis=2)

    params = dict(ada_b=(ada_b, m_ada_b, v_ada_b), pre_mix_g=(pre_mix_g, m_pre_mix_g, v_pre_mix_g),
                  post_mix_g=(post_mix_g, m_post_mix_g, v_post_mix_g), conv_b=(conv_b, m_conv_b, v_conv_b),
                  conv_ln_g=(conv_ln_g, m_conv_ln_g, v_conv_ln_g), conv_ln_b=(conv_ln_b, m_conv_ln_b, v_conv_ln_b),
                  pool_w=(pool_w, m_pool_w, v_pool_w), pool_scale=(pool_scale, m_pool_scale, v_pool_scale),
                  pre_ffn_g=(pre_ffn_g, m_pre_ffn_g, v_pre_ffn_g), post_ffn_g=(post_ffn_g, m_post_ffn_g, v_post_ffn_g),
                  ffn_conv_b=(ffn_conv_b, m_ffn_conv_b, v_ffn_conv_b), conv_w=(conv_w, m_conv_w, v_conv_w),
                  ffn_conv_w=(ffn_conv_w, m_ffn_conv_w, v_ffn_conv_w))
    pack = lambda i, g=None: _rows128(jnp.concatenate([(grads[k] if g else params[k][i]).reshape(-1) for k in names]))
    sd, sm, sv = _adamw_flat(pack(0), pack(0, True), pack(1), pack(2), name="adamw_small")
    outs = {}
    off = 0
    for k in names:
        shape, sz = params[k][0].shape, params[k][0].size
        outs[k] = (grads[k],) + tuple(a.reshape(-1)[off:off + sz].reshape(shape) for a in (sd, sm, sv))
        off += sz

    outs["ada_w"] = (g_ada_w, d_ada_w, nm_ada_w, nv_ada_w)
    for k, w, m, v, q in [("w_in", w_in, m_w_in, v_w_in, q_mix[0]), ("w_out", w_out, m_w_out, v_w_out, q_mix[1]),
                          ("ffn_up", ffn_up, m_ffn_up, v_ffn_up, q_mlp[0]), ("ffn_down", ffn_down, m_ffn_down, v_ffn_down, q_mlp[1])]:
        outs[k] = tuple(_adamw_layers(w, m, v, q, name=f"adamw_{k}"))

    order = ["ada_w", "ada_b", "pre_mix_g", "post_mix_g", "w_in", "conv_w", "conv_b", "conv_ln_g", "conv_ln_b", "pool_w",
             "pool_scale", "w_out", "pre_ffn_g", "post_ffn_g", "ffn_up", "ffn_conv_w", "ffn_conv_b", "ffn_down"]
    return (loss, dx) + tuple(outs[k][i] for i in range(4) for k in order)
```

```python
import functools

import jax
import jax.numpy as jnp
from jax import lax
from jax.experimental import pallas as pl
from jax.experimental.pallas import tpu as pltpu

F32 = jnp.float32
BF16 = jnp.bfloat16
MESH = pl.DeviceIdType.MESH

EPS = 1e-6
POOL_WINDOWS = (2, 4, 8, 16)
ADAM_LR = 0.001
ADAM_B1 = 0.9
ADAM_B2 = 0.999
ADAM_EPS = 1e-08
ADAM_WD = 0.01
ADAM_STEP = 10

N_CHIPS = 4
N_DEV = 8
LANES = 128
SUBLANES = 8
HALO = 32
FHALO = 8
VMEM_LIMIT = 60 * 1024 * 1024


def _pos():
    return lax.axis_index("x"), lax.axis_index("y"), lax.axis_index("c")


def _flip(v, f):
    return 1 - v if f else v


def _full(shape):
    nd = len(shape)
    return pl.BlockSpec(shape, lambda *_: (0,) * nd)


_ANY = pl.BlockSpec(memory_space=pl.ANY)
_VMEM = pl.BlockSpec(memory_space=pltpu.VMEM)


def _sigmoid(v):
    return 1.0 / (1.0 + jnp.exp(-v))


def _colsum(v):
    return jnp.sum(v, axis=0, keepdims=True)


def _gather8(v, name):
    r, ccols = v.shape

    def body(v_ref, out_ref, send_sems, recv_sems, local_sem):
        x, y, c = _pos()
        me = 4 * x + 2 * y + c
        mine = pltpu.make_async_copy(v_ref, out_ref.at[me], local_sem)
        mine.start()
        peers = [(_flip(x, (k >> 2) & 1), _flip(y, (k >> 1) & 1), _flip(c, k & 1)) for k in range(1, N_DEV)]
        sends = []
        for k, peer in enumerate(peers):
            cp = pltpu.make_async_remote_copy(src_ref=v_ref, dst_ref=out_ref.at[me], send_sem=send_sems.at[k],
                                              recv_sem=recv_sems.at[k], device_id=peer, device_id_type=MESH)
            cp.start()
            sends.append(cp)
        for k, peer in enumerate(peers):
            pidx = 4 * peer[0] + 2 * peer[1] + peer[2]
            pltpu.make_async_remote_copy(src_ref=v_ref, dst_ref=out_ref.at[pidx], send_sem=send_sems.at[k],
                                         recv_sem=recv_sems.at[k], device_id=peer, device_id_type=MESH).wait_recv()
        for cp in sends:
            cp.wait_send()
        mine.wait()

    return pl.pallas_call(
        body, name=name, out_shape=jax.ShapeDtypeStruct((N_DEV, r, ccols), v.dtype),
        in_specs=[_VMEM], out_specs=_VMEM,
        scratch_shapes=[pltpu.SemaphoreType.DMA((N_DEV - 1,)), pltpu.SemaphoreType.DMA((N_DEV - 1,)),
                        pltpu.SemaphoreType.DMA(())],
        compiler_params=pltpu.CompilerParams(vmem_limit_bytes=VMEM_LIMIT),
    )(v)


def _rows_to_owners(v, name):
    _, r, ccols = v.shape

    def body(v_ref, out_ref, send_sems, recv_sems, local_sem):
        x, y, c = _pos()
        jm = 2 * x + y
        mine = pltpu.make_async_copy(v_ref.at[2 * jm + c], out_ref.at[jm], local_sem)
        mine.start()
        peers, pjs = _chip_peers(x, y, c)
        sends = []
        for k, peer in enumerate(peers):
            cp = pltpu.make_async_remote_copy(src_ref=v_ref.at[2 * pjs[k] + c], dst_ref=out_ref.at[jm],
                                              send_sem=send_sems.at[k], recv_sem=recv_sems.at[k],
                                              device_id=peer, device_id_type=MESH)
            cp.start()
            sends.append(cp)
        for k, peer in enumerate(peers):
            pltpu.make_async_remote_copy(src_ref=v_ref.at[0], dst_ref=out_ref.at[pjs[k]], send_sem=send_sems.at[k],
                                         recv_sem=recv_sems.at[k], device_id=peer, device_id_type=MESH).wait_recv()
        for cp in sends:
            cp.wait_send()
        mine.wait()

    return pl.pallas_call(
        body, name=name, out_shape=jax.ShapeDtypeStruct((N_CHIPS, r, ccols), v.dtype),
        in_specs=[_VMEM], out_specs=_VMEM,
        scratch_shapes=[pltpu.SemaphoreType.DMA((N_CHIPS - 1,)), pltpu.SemaphoreType.DMA((N_CHIPS - 1,)),
                        pltpu.SemaphoreType.DMA(())],
        compiler_params=pltpu.CompilerParams(vmem_limit_bytes=VMEM_LIMIT),
    )(v)


def _allreduce8(v, name):
    r, ccols = v.shape
    h = r // 2

    def body(v_ref, out_ref, whole, half, send_sems, recv_sems):
        x, y, c = _pos()
        sib = (x, y, 1 - c)
        mine = pl.ds(pl.multiple_of(c * h, SUBLANES), h)
        theirs = pl.ds(pl.multiple_of((1 - c) * h, SUBLANES), h)

        def exchange(src, dst, k, peer):
            cp = pltpu.make_async_remote_copy(src_ref=src, dst_ref=dst, send_sem=send_sems.at[k],
                                              recv_sem=recv_sems.at[k], device_id=peer, device_id_type=MESH)
            cp.start()
            cp.wait()

        exchange(v_ref, whole, 0, sib)
        out_ref[...] = v_ref[...] + whole[...]
        for k, peer in ((1, (1 - x, y, c)), (2, (x, 1 - y, c))):
            exchange(out_ref.at[mine], half.at[k - 1], k, peer)
            out_ref[mine, :] = out_ref[mine, :] + half[k - 1]
        exchange(out_ref.at[mine], half.at[2], 3, sib)
        out_ref[theirs, :] = half[2]

    return pl.pallas_call(
        body, name=name, out_shape=jax.ShapeDtypeStruct((r, ccols), v.dtype),
        in_specs=[_VMEM], out_specs=_VMEM,
        scratch_shapes=[pltpu.VMEM((r, ccols), v.dtype), pltpu.VMEM((3, h, ccols), v.dtype),
                        pltpu.SemaphoreType.DMA((4,)), pltpu.SemaphoreType.DMA((4,))],
        compiler_params=pltpu.CompilerParams(vmem_limit_bytes=VMEM_LIMIT),
    )(v)


def _chip_peers(x, y, c):
    peers = [(_flip(x, (k >> 1) & 1), _flip(y, k & 1), c) for k in range(1, N_CHIPS)]
    return peers, [2 * p[0] + p[1] for p in peers]


class _Exchange:
    def __init__(self, ins, outs, aliases, n_sems, n_local, start, finish, mid=None, mid_at=1.0):
        self.ins, self.outs, self.aliases = list(ins), list(outs), dict(aliases)
        self.n_sems, self.n_local, self.start, self.finish = n_sems, n_local, start, finish
        self.mid, self.mid_at = mid, mid_at

    def scratch(self):
        return [pltpu.SemaphoreType.DMA((self.n_sems,)), pltpu.SemaphoreType.DMA((self.n_sems,)),
                pltpu.SemaphoreType.DMA((max(self.n_local, 1),))]


class _Sems:
    def __init__(self, send, recv, local, base=0, lbase=0):
        self.send, self.recv, self.loc, self.base, self.lbase = send, recv, local, base, lbase

    def shifted(self, by, lby):
        return _Sems(self.send, self.recv, self.loc, self.base + by, self.lbase + lby)

    def local(self, k):
        return self.loc.at[self.lbase + k]


def _remote(src, dst, sems, k, peer):
    return pltpu.make_async_remote_copy(src_ref=src, dst_ref=dst, send_sem=sems.send.at[sems.base + k],
                                        recv_sem=sems.recv.at[sems.base + k], device_id=peer, device_id_type=MESH)


def _combine(exs):
    ins = [a for ex in exs for a in ex.ins]
    outs = [o for ex in exs for o in ex.outs]
    aliases, spans, ni, no, ns, nloc = {}, [], 0, 0, 0, 0
    for ex in exs:
        aliases.update({ni + a: no + b for a, b in ex.aliases.items()})
        spans.append((ni, no, ns, nloc))
        ni, no, ns, nloc = ni + len(ex.ins), no + len(ex.outs), ns + ex.n_sems, nloc + ex.n_local

    def each(which):
        def run(ins_, outs_, sems):
            for ex, (i0, o0, s0, l0) in zip(exs, spans):
                stage = getattr(ex, which)
                if stage is not None:
                    stage(ins_[i0:i0 + len(ex.ins)], outs_[o0:o0 + len(ex.outs)], sems.shifted(s0, l0))
        return run

    mids = [ex.mid_at for ex in exs if ex.mid is not None]
    return _Exchange(ins, outs, aliases, ns, nloc, each("start"), each("finish"),
                     mid=each("mid") if mids else None, mid_at=max(mids) if mids else 1.0)


def _gather(shards, mid_at=1.0):
    n = len(shards)
    per = N_CHIPS - 1
    halves = [s.reshape(2, s.shape[0] // 2, s.shape[1]) for s in shards]

    def copies(ins, outs, sems):
        x, y, c = _pos()
        jm = 2 * x + y
        sib = (x, y, 1 - c)
        peers, pjs = _chip_peers(x, y, c)
        sends, recvs, passes, passed = [], [], [], []
        for a in range(n):
            own = _remote(ins[a], outs[a].at[jm], sems, 2 * n * per + a, sib)
            sends.append(own)
            passed.append(own)
            for k, peer in enumerate(peers):
                landed, theirs = outs[a].at[pjs[k], c], outs[a].at[pjs[k], 1 - c]
                sends.append(_remote(ins[a].at[c], outs[a].at[jm, c], sems, 2 * (a * per + k), peer))
                recvs.append(_remote(landed, landed, sems, 2 * (a * per + k), peer))
                passes.append(_remote(landed, landed, sems, 2 * (a * per + k) + 1, sib))
                passed.append(_remote(theirs, theirs, sems, 2 * (a * per + k) + 1, sib))
        return sends, recvs, passes, passed

    def start(ins, outs, sems):
        for cp in copies(ins, outs, sems)[0]:
            cp.start()

    def mid(ins, outs, sems):
        _, recvs, passes, _ = copies(ins, outs, sems)
        for got, fwd in zip(recvs, passes):
            got.wait_recv()
            fwd.start()

    def finish(ins, outs, sems):
        sends, _, passes, passed = copies(ins, outs, sems)
        for cp in passed:
            cp.wait_recv()
        for cp in sends + passes:
            cp.wait_send()

    outs = [jax.ShapeDtypeStruct((N_CHIPS,) + h.shape, h.dtype) for h in halves]
    return _Exchange(halves, outs, {}, 2 * n * per + n, 0, start, finish, mid=mid, mid_at=mid_at)


def _whole(gathered):
    return [g.reshape(g.shape[0], g.shape[1] * g.shape[2], g.shape[3]) for g in gathered]


def _swap_halves(gs):
    n = len(gs)
    halves = [g.reshape(g.shape[0], 2, g.shape[1] // 2, g.shape[2]) for g in gs]

    def copies(ins, outs, sems):
        x, y, c = _pos()
        sib = (x, y, 1 - c)
        return [_remote(ins[a].at[:, 1 - c], outs[a], sems, a, sib) for a in range(n)]

    def start(ins, outs, sems):
        for cp in copies(ins, outs, sems):
            cp.start()

    def finish(ins, outs, sems):
        for cp in copies(ins, outs, sems):
            cp.wait()

    outs = [jax.ShapeDtypeStruct((g.shape[0], g.shape[1] // 2, g.shape[2]), g.dtype) for g in gs]
    return _Exchange(halves, outs, {}, n, 0, start, finish)


def _chip_exchange(ps):
    n = len(ps)
    per = N_CHIPS - 1

    def copies(ins, outs, sems):
        x, y, c = _pos()
        peers, pjs = _chip_peers(x, y, c)
        return [_remote(ins[a].at[pjs[k]], outs[a].at[k], sems, a * per + k, peer)
                for a in range(n) for k, peer in enumerate(peers)]

    def start(ins, outs, sems):
        for cp in copies(ins, outs, sems):
            cp.start()

    def finish(ins, outs, sems):
        for cp in copies(ins, outs, sems):
            cp.wait()

    outs = [jax.ShapeDtypeStruct((per,) + p.shape[1:], p.dtype) for p in ps]
    return _Exchange(ps, outs, {}, n * per, 0, start, finish)


def _sibling_share(fs):
    n = len(fs)

    def copies(outs, sems):
        x, y, c = _pos()
        sib = (x, y, 1 - c)
        sends = [_remote(outs[a].at[c], outs[a].at[c], sems, a, sib) for a in range(n)]
        recvs = [_remote(outs[a].at[1 - c], outs[a].at[1 - c], sems, a, sib) for a in range(n)]
        return sends, recvs

    def start(ins, outs, sems):
        for cp in copies(outs, sems)[0]:
            cp.start()

    def finish(ins, outs, sems):
        sends, recvs = copies(outs, sems)
        for cp in recvs:
            cp.wait_recv()
        for cp in sends:
            cp.wait_send()

    outs = [jax.ShapeDtypeStruct(f.shape, f.dtype) for f in fs]
    return _Exchange(fs, outs, {a: a for a in range(n)}, n, 0, start, finish)


def _run_exchange(ex, name):
    ni, no = len(ex.ins), len(ex.outs)

    def body(*refs):
        ins, outs, sems = refs[:ni], refs[ni:ni + no], _Sems(*refs[ni + no:])
        ex.start(ins, outs, sems)
        if ex.mid is not None:
            ex.mid(ins, outs, sems)
        ex.finish(ins, outs, sems)

    return pl.pallas_call(
        body, name=name, out_shape=ex.outs, in_specs=[_ANY] * ni, out_specs=[_ANY] * no,
        input_output_aliases=ex.aliases, scratch_shapes=ex.scratch(),
    )(*ex.ins)


def _grid_call(body, name, grid, in_specs, out_specs, out_shape, scratch_shapes, args, exchange=None):
    ni, no = len(in_specs), len(out_specs)
    params = pltpu.CompilerParams(dimension_semantics=("arbitrary",) * len(grid), vmem_limit_bytes=VMEM_LIMIT)
    if exchange is None:
        outs = pl.pallas_call(body, name=name, grid=grid, in_specs=in_specs, out_specs=out_specs, out_shape=out_shape,
                              scratch_shapes=scratch_shapes, compiler_params=params)(*args)
        return list(outs), []
    ex = exchange
    nci, nco = len(ex.ins), len(ex.outs)

    def hosted(*refs):
        cin = refs[ni:ni + nci]
        cout = refs[ni + nci + no:ni + nci + no + nco]
        sems = _Sems(*refs[len(refs) - 3:])
        main = refs[:ni] + refs[ni + nci:ni + nci + no] + refs[ni + nci + no + nco:len(refs) - 3]
        ids = [pl.program_id(a) for a in range(len(grid))]
        first = functools.reduce(lambda p, q: p & q, [i == 0 for i in ids])
        last = functools.reduce(lambda p, q: p & q, [i == g - 1 for i, g in zip(ids, grid)])

        @pl.when(first)
        def _():
            ex.start(cin, cout, sems)

        if ex.mid is not None:
            steps = functools.reduce(lambda p, q: p * q, grid)
            flat = functools.reduce(lambda p, q: p * q[1] + q[0], zip(ids[1:], grid[1:]), ids[0])

            @pl.when(flat == min(steps - 1, int(ex.mid_at * steps)))
            def _():
                ex.mid(cin, cout, sems)

        body(*main)

        @pl.when(last)
        def _():
            ex.finish(cin, cout, sems)

    outs = pl.pallas_call(
        hosted, name=name, grid=grid, in_specs=list(in_specs) + [_ANY] * nci, out_specs=list(out_specs) + [_ANY] * nco,
        out_shape=list(out_shape) + ex.outs, scratch_shapes=list(scratch_shapes) + ex.scratch(),
        input_output_aliases={ni + a: no + b for a, b in ex.aliases.items()}, compiler_params=params,
    )(*args, *ex.ins)
    return list(outs[:no]), list(outs[no:])


def _row_tile(rows, cols, itemsize, budget=2 * 1024 * 1024):
    best = None
    for t in range(16, rows + 1, 16):
        if rows % t == 0 and t * cols * itemsize <= budget:
            best = t
    return best if best is not None else rows


def _pair_sum(g, r1, cidx, name):
    nj, r, ccols = g.shape
    hr = r // 2
    tr = _row_tile(hr, ccols, 4)
    nt = hr // tr

    def body(c_ref, g_ref, r_ref, o_ref):
        o_ref[...] = (g_ref[...].astype(F32) + r_ref[...].astype(F32)).astype(o_ref.dtype)

    return pl.pallas_call(
        body, name=name, out_shape=jax.ShapeDtypeStruct((nj, hr, ccols), g.dtype),
        grid_spec=pltpu.PrefetchScalarGridSpec(
            num_scalar_prefetch=1, grid=(nj, nt),
            in_specs=[pl.BlockSpec((None, tr, ccols), lambda j, i, c_ref: (j, c_ref[0] * nt + i, 0)),
                      pl.BlockSpec((None, tr, ccols), lambda j, i, c_ref: (j, i, 0))],
            out_specs=pl.BlockSpec((None, tr, ccols), lambda j, i, c_ref: (j, i, 0))),
        compiler_params=pltpu.CompilerParams(dimension_semantics=("arbitrary", "arbitrary")),
    )(cidx, g, r1)


def _chip_sum(p, r2, idx, name):
    nj, hr, ccols = p.shape
    tr = _row_tile(hr, ccols, 4)
    nt = hr // tr

    def body(i_ref, p_ref, r_ref, o_ref):
        s = p_ref[...].astype(F32)
        for k in range(N_CHIPS - 1):
            s = s + r_ref[k].astype(F32)
        o_ref[...] = s

    return pl.pallas_call(
        body, name=name, out_shape=jax.ShapeDtypeStruct((2, hr, ccols), F32),
        grid_spec=pltpu.PrefetchScalarGridSpec(
            num_scalar_prefetch=1, grid=(nt,),
            in_specs=[pl.BlockSpec((None, tr, ccols), lambda i, i_ref: (i_ref[0], i, 0)),
                      pl.BlockSpec((N_CHIPS - 1, tr, ccols), lambda i, i_ref: (0, i, 0))],
            out_specs=pl.BlockSpec((None, tr, ccols), lambda i, i_ref: (i_ref[1], i, 0))),
        compiler_params=pltpu.CompilerParams(dimension_semantics=("arbitrary",)),
    )(idx, p, r2)


def _adam_math(w, g, m, v):
    m2 = ADAM_B1 * m + (1.0 - ADAM_B1) * g
    v2 = ADAM_B2 * v + (1.0 - ADAM_B2) * (g * g)
    m_hat = m2 / (1.0 - ADAM_B1 ** ADAM_STEP)
    v_hat = v2 / (1.0 - ADAM_B2 ** ADAM_STEP)
    delta = -ADAM_LR * (m_hat / (jnp.sqrt(v_hat) + ADAM_EPS) + ADAM_WD * w)
    return delta, m2, v2


def _adamw_layers(w, m, v, gs, name):
    nl, r, ccols = w.shape
    ng = len(gs)
    tr = _row_tile(r, ccols, 4, budget=1024 * 1024)
    nt = r // tr

    def body(w_ref, m_ref, v_ref, *rest):
        g_refs, (go_ref, d_ref, mo_ref, vo_ref) = rest[:ng], rest[ng:]
        l = pl.program_id(0)
        g = g_refs[0][...]
        for k in range(1, ng):
            g = jnp.where(l == k, g_refs[k][...], g)
        delta, m2, v2 = _adam_math(w_ref[...], g, m_ref[...], v_ref[...])
        go_ref[...] = g
        d_ref[...] = delta
        mo_ref[...] = m2
        vo_ref[...] = v2

    big = pl.BlockSpec((None, tr, ccols), lambda l, i: (l, i, 0))

    def gspec(k):
        return pl.BlockSpec((tr, ccols), lambda l, i: (jnp.where(l == k, i, jnp.where(l < k, 0, nt - 1)), 0))

    assert ng == nl
    return _grid_call(body, name, (nl, nt), in_specs=[big, big, big] + [gspec(k) for k in range(ng)],
                      out_specs=[big, big, big, big], out_shape=[jax.ShapeDtypeStruct(w.shape, F32)] * 4,
                      scratch_shapes=[], args=(w, m, v, *gs))[0]


def _adamw_flat(w, g, m, v, name):
    r, ccols = w.shape

    def body(w_ref, g_ref, m_ref, v_ref, d_ref, mo_ref, vo_ref):
        delta, m2, v2 = _adam_math(w_ref[...], g_ref[...], m_ref[...], v_ref[...])
        d_ref[...] = delta
        mo_ref[...] = m2
        vo_ref[...] = v2

    return pl.pallas_call(
        body, name=name, out_shape=[jax.ShapeDtypeStruct((r, ccols), F32)] * 3,
        in_specs=[_VMEM] * 4, out_specs=[_VMEM] * 3,
        compiler_params=pltpu.CompilerParams(vmem_limit_bytes=VMEM_LIMIT),
    )(w, g, m, v)


def _ada_forward(c_all, ada_w, ada_b_cols, name):
    nl, d, ncols = ada_w.shape
    bg = c_all.shape[0]
    tn = 512 if ncols % 512 == 0 else ncols

    def body(c_ref, w_ref, b_ref, o_ref):
        cv = c_ref[...]
        ca = (cv * _sigmoid(cv)).astype(BF16)
        o_ref[...] = jnp.dot(ca, w_ref[...].astype(BF16), preferred_element_type=F32) + b_ref[...]

    return pl.pallas_call(
        body, name=name, out_shape=jax.ShapeDtypeStruct((nl, bg, ncols), F32),
        grid=(nl, ncols // tn),
        in_specs=[pl.BlockSpec((bg, d), lambda l, j: (0, 0)),
                  pl.BlockSpec((None, d, tn), lambda l, j: (l, 0, j)),
                  pl.BlockSpec((None, 1, tn), lambda l, j: (l, 0, j))],
        out_specs=pl.BlockSpec((None, bg, tn), lambda l, j: (l, 0, j)),
        compiler_params=pltpu.CompilerParams(dimension_semantics=("arbitrary", "arbitrary")),
    )(c_all, ada_w, ada_b_cols)


def _ada_update(c_all, dmod_cols, w, m, v, name, exchange=None):
    nl, d, ncols = w.shape
    bg = c_all.shape[0]
    tn = 512 if ncols % 512 == 0 else ncols

    def body(c_ref, dm_ref, w_ref, m_ref, v_ref, go_ref, d_ref, mo_ref, vo_ref):
        cv = c_ref[...]
        ca = (cv * _sigmoid(cv)).astype(BF16)
        g = lax.dot_general(ca, dm_ref[...].astype(BF16), (((0,), (0,)), ((), ())), preferred_element_type=F32)
        delta, m2, v2 = _adam_math(w_ref[...], g, m_ref[...], v_ref[...])
        go_ref[...] = g
        d_ref[...] = delta
        mo_ref[...] = m2
        vo_ref[...] = v2

    big = pl.BlockSpec((None, d, tn), lambda l, j: (l, 0, j))
    return _grid_call(
        body, name, (nl, ncols // tn),
        in_specs=[pl.BlockSpec((bg, d), lambda l, j: (0, 0)),
                  pl.BlockSpec((None, bg, tn), lambda l, j: (l, 0, j)), big, big, big],
        out_specs=[big, big, big, big], out_shape=[jax.ShapeDtypeStruct(w.shape, F32)] * 4,
        scratch_shapes=[], args=(c_all, dmod_cols, w, m, v), exchange=exchange)


def _load_weights(first, pairs, sems):
    @pl.when(first)
    def _():
        cps = [pltpu.make_async_copy(src, dst, sems.at[k]) for k, (src, dst) in enumerate(pairs)]
        for cp in cps:
            cp.start()
        for cp in cps:
            cp.wait()


def _ada_norm(xv, g, sc, sh):
    r = lax.rsqrt(jnp.mean(xv * xv, axis=-1, keepdims=True) + EPS)
    xn = xv * r
    return (xn * g) * (1.0 + sc) + sh, xn, r


def _ada_norm_bwd(dh, xn, r, g, sc):
    d_sh = _colsum(dh)
    d_sc = _colsum(dh * (xn * g))
    dxg = dh * (1.0 + sc)
    d_g = _colsum(dxg * xn)
    gd = dxg * g
    dx = r * (gd - xn * jnp.mean(gd * xn, axis=-1, keepdims=True))
    return dx, d_sh, d_sc, d_g


def _gated_residual_bwd(dxo, o, g_post, gt):
    r = lax.rsqrt(jnp.mean(o * o, axis=-1, keepdims=True) + EPS)
    on = o * r
    d_gt = _colsum(dxo * (on * g_post))
    dy = dxo * (1.0 + gt)
    d_gp = _colsum(dy * on)
    gd = dy * g_post
    do = r * (gd - on * jnp.mean(gd * on, axis=-1, keepdims=True))
    return do, d_gt, d_gp


def _seq_positions(i, tm, width):
    return i * tm + lax.broadcasted_iota(jnp.int32, (tm, width), 0)


def _fill_phases(ext, phases):
    rows = ext.shape[0]
    ev = ext[...]
    for r in range(1, SUBLANES):
        phases[r - 1] = pltpu.roll(ev, rows - r, axis=0)


def _shifted_rows(ext, phases, offset, n):
    q, r = divmod(offset, SUBLANES)
    if r == 0:
        return ext[pl.ds(q * SUBLANES, n), :]
    return phases[r - 1, pl.ds(q * SUBLANES, n), :]


def _rows_before(halo, cur, shift):
    e = jnp.concatenate([halo, cur], axis=0)
    return pltpu.roll(e, shift, axis=0)[halo.shape[0]:, :]


def _rows_after(cur, halo, shift):
    e = jnp.concatenate([cur, halo], axis=0)
    return pltpu.roll(e, e.shape[0] - shift, axis=0)[:cur.shape[0], :]


def _mixer_forward(x, mod, vec_d, vec_c, cw, pw, win_g, wout_g, taps, tm, name, exchange=None):
    nb, s, d = x.shape
    n = s // tm
    nj, _, dcol = win_g.shape
    din = nj * dcol
    dc = vec_c.shape[-1]
    dpool = din - 2 * dc
    dmix = dc + dpool
    ro = wout_g.shape[1]
    ngrp = dpool // LANES

    def body(x_ref, mod_ref, vd_ref, vc_ref, cw_ref, pw_ref, win_hbm, wout_hbm,
             xo_ref, h_ref, u_ref, ac_ref, dp_ref, z_ref, o_ref,
             win_v, wout_v, ext_a, ext_p, phases, sems):
        b, i = pl.program_id(0), pl.program_id(1)
        pairs = [(win_hbm.at[j], win_v.at[:, pl.ds(j * dcol, dcol)]) for j in range(nj)]
        pairs += [(wout_hbm.at[j], wout_v.at[pl.ds(j * ro, ro), :]) for j in range(nj)]
        _load_weights((b == 0) & (i == 0), pairs, sems)

        xv = x_ref[...]
        h, _, _ = _ada_norm(xv, vd_ref[0:1, :], mod_ref[1:2, :], mod_ref[0:1, :])
        hb = h.astype(BF16)
        h_ref[...] = hb
        u = jnp.dot(hb, win_v[...], preferred_element_type=F32)
        u_ref[...] = u.astype(BF16)
        ag = u[:, :dc] * _sigmoid(u[:, dc:2 * dc])
        up = u[:, 2 * dc:]

        @pl.when(i == 0)
        def _():
            ext_a[0:HALO, :] = jnp.zeros((HALO, dc), F32)
            ext_p[0:HALO, :] = jnp.zeros((HALO, dpool), F32)

        @pl.when(i > 0)
        def _():
            ext_a[0:HALO, :] = ext_a[tm:tm + HALO, :]
            ext_p[0:HALO, :] = ext_p[tm:tm + HALO, :]

        ext_a[HALO:HALO + tm, :] = ag
        ext_p[HALO:HALO + tm, :] = up

        acc = jnp.broadcast_to(vc_ref[0:1, :], (tm, dc))
        _fill_phases(ext_a, phases)
        for k in range(taps):
            acc = acc + cw_ref[k:k + 1, :] * _shifted_rows(ext_a, phases, HALO - (taps - 1) + k, tm)
        ac_ref[...] = acc.astype(BF16)
        mu = jnp.mean(acc, axis=-1, keepdims=True)
        xc = acc - mu
        var = jnp.mean(xc * xc, axis=-1, keepdims=True)
        al = (xc * lax.rsqrt(var + EPS)) * vc_ref[1:2, :] + vc_ref[2:3, :]
        a = al * _sigmoid(al)

        pos = _seq_positions(i, tm, LANES)
        parts = [a.astype(BF16)]
        for g in range(ngrp):
            w = POOL_WINDOWS[g]
            cols = slice(g * LANES, (g + 1) * LANES)
            sw = ext_p[:, cols]
            step = 1
            while step < w:
                sw = sw + pltpu.roll(sw, step, axis=0)
                step *= 2
            cnt = jnp.minimum(pos + 1, w).astype(F32)
            dg = (sw[HALO:, :] / cnt - up[:, cols]).astype(BF16)
            dp_ref[:, cols] = dg
            q = jnp.dot(dg, pw_ref[g], preferred_element_type=F32)
            parts.append((q * vc_ref[3:4, cols]).astype(BF16))
        z = jnp.concatenate(parts, axis=-1)
        z_ref[...] = z
        o = jnp.dot(z, wout_v[...], preferred_element_type=F32)
        o_ref[...] = o
        r2 = lax.rsqrt(jnp.mean(o * o, axis=-1, keepdims=True) + EPS)
        xo_ref[...] = xv + (1.0 + mod_ref[2:3, :]) * ((o * r2) * vd_ref[1:2, :])

    def tile(width):
        return pl.BlockSpec((None, tm, width), lambda b, i: (b, i, 0))

    return _grid_call(
        body, name, (nb, n),
        in_specs=[tile(d), pl.BlockSpec((None, 8, d), lambda b, i: (b, 0, 0)), _full(vec_d.shape), _full(vec_c.shape),
                  _full(cw.shape), _full(pw.shape), _ANY, _ANY],
        out_specs=[tile(d), tile(d), tile(din), tile(dc), tile(dpool), tile(dmix), tile(d)],
        out_shape=[jax.ShapeDtypeStruct((nb, s, d), F32), jax.ShapeDtypeStruct((nb, s, d), BF16),
                   jax.ShapeDtypeStruct((nb, s, din), BF16), jax.ShapeDtypeStruct((nb, s, dc), BF16),
                   jax.ShapeDtypeStruct((nb, s, dpool), BF16), jax.ShapeDtypeStruct((nb, s, dmix), BF16),
                   jax.ShapeDtypeStruct((nb, s, d), F32)],
        scratch_shapes=[pltpu.VMEM((d, din), BF16), pltpu.VMEM((dmix, d), BF16),
                        pltpu.VMEM((HALO + tm, dc), F32), pltpu.VMEM((HALO + tm, dpool), F32),
                        pltpu.VMEM((SUBLANES - 1, HALO + tm, dc), F32), pltpu.SemaphoreType.DMA((2 * nj,))],
        args=(x, mod, vec_d, vec_c, cw, pw, win_g, wout_g), exchange=exchange)


def _mixer_backward(dxo, x, o, u, ac, dpl, mod, vec_d, vec_c, cw, pw, win_g, wout_g, taps, tm, name, exchange=None):
    nb, s, d = x.shape
    n = s // tm
    nj, _, dcol = win_g.shape
    din = nj * dcol
    dc = vec_c.shape[-1]
    dpool = din - 2 * dc
    dmix = dc + dpool
    ro = wout_g.shape[1]
    ngrp = dpool // LANES
    rext = tm + HALO

    def body(dxo_ref, x_ref, o_ref, u_ref, ac_ref, dp_ref, mod_ref, vd_ref, vc_ref, cw_ref, pw_ref, win_hbm, wout_hbm,
             dx_ref, du_ref, dob_ref, rowd_ref, rowb_ref, rowc_ref, dcw_ref, dpw_ref,
             win_v, wout_v, ext_a, ext_p, phases, sems):
        b, i = pl.program_id(0), pl.program_id(1)
        first = (b == 0) & (i == 0)
        pairs = [(win_hbm.at[j], win_v.at[:, pl.ds(j * dcol, dcol)]) for j in range(nj)]
        pairs += [(wout_hbm.at[j], wout_v.at[pl.ds(j * ro, ro), :]) for j in range(nj)]
        _load_weights(first, pairs, sems)

        @pl.when(first)
        def _():
            rowd_ref[...] = jnp.zeros_like(rowd_ref)
            rowc_ref[...] = jnp.zeros_like(rowc_ref)
            dcw_ref[...] = jnp.zeros_like(dcw_ref)
            dpw_ref[...] = jnp.zeros_like(dpw_ref)

        @pl.when(i == 0)
        def _():
            rowb_ref[...] = jnp.zeros_like(rowb_ref)
            ext_a[tm:rext, :] = jnp.zeros((HALO, dc), F32)
            ext_p[tm:rext, :] = jnp.zeros((HALO, dpool), F32)

        @pl.when(i > 0)
        def _():
            ext_a[tm:rext, :] = ext_a[0:HALO, :]
            ext_p[tm:rext, :] = ext_p[0:HALO, :]

        g_pre, g_post = vd_ref[0:1, :], vd_ref[1:2, :]
        sh, sc, gt = mod_ref[0:1, :], mod_ref[1:2, :], mod_ref[2:3, :]
        do, d_gt, d_gp = _gated_residual_bwd(dxo_ref[...], o_ref[...], g_post, gt)
        dob = do.astype(BF16)
        dob_ref[...] = dob
        dz = lax.dot_general(dob, wout_v[...], (((1,), (1,)), ((), ())), preferred_element_type=F32)

        acv = ac_ref[...].astype(F32)
        mu = jnp.mean(acv, axis=-1, keepdims=True)
        xc = acv - mu
        rstd = lax.rsqrt(jnp.mean(xc * xc, axis=-1, keepdims=True) + EPS)
        an = xc * rstd
        lg = vc_ref[1:2, :]
        al = an * lg + vc_ref[2:3, :]
        sg = _sigmoid(al)
        dal = dz[:, :dc] * (sg * (1.0 + al * (1.0 - sg)))
        d_lg = _colsum(dal * an)
        d_lb = _colsum(dal)
        dan = dal * lg
        dac = rstd * (dan - jnp.mean(dan, axis=-1, keepdims=True) - an * jnp.mean(dan * an, axis=-1, keepdims=True))
        d_cb = _colsum(dac)
        ext_a[0:tm, :] = dac
        uv = u_ref[:, 0:dc].astype(F32)
        sgu = _sigmoid(u_ref[:, dc:2 * dc].astype(F32))
        ag = uv * sgu
        dag = jnp.zeros((tm, dc), F32)
        _fill_phases(ext_a, phases)
        for k in range(taps):
            sl = _shifted_rows(ext_a, phases, taps - 1 - k, tm)
            dag = dag + cw_ref[k:k + 1, :] * sl
            dcw_ref[k:k + 1, :] += _colsum(ag * sl)
        du_ref[:, 0:dc] = (dag * sgu).astype(BF16)
        du_ref[:, dc:2 * dc] = (dag * uv * (sgu * (1.0 - sgu))).astype(BF16)

        pos = _seq_positions(n - 1 - i, tm, LANES)
        d_ps = []
        for g in range(ngrp):
            w = POOL_WINDOWS[g]
            cols = slice(g * LANES, (g + 1) * LANES)
            gcols = slice(dc + g * LANES, dc + (g + 1) * LANES)
            dgb = dp_ref[:, cols]
            q = jnp.dot(dgb, pw_ref[g], preferred_element_type=F32)
            dpg = dz[:, gcols]
            d_ps.append(_colsum(dpg * q))
            dq = (dpg * vc_ref[3:4, cols]).astype(BF16)
            dpw_ref[g] += lax.dot_general(dgb, dq, (((0,), (0,)), ((), ())), preferred_element_type=F32)
            dd = lax.dot_general(dq, pw_ref[g], (((1,), (1,)), ((), ())), preferred_element_type=F32)
            cnt = jnp.minimum(pos + 1, w).astype(F32)
            ext_p[0:tm, cols] = dd / cnt
            sw = ext_p[:, cols]
            step = 1
            while step < w:
                sw = sw + pltpu.roll(sw, rext - step, axis=0)
                step *= 2
            du_ref[:, 2 * dc + g * LANES:2 * dc + (g + 1) * LANES] = (sw[0:tm, :] - dd).astype(BF16)
        rowc_ref[0:1, :] += d_cb
        rowc_ref[1:2, :] += d_lg
        rowc_ref[2:3, :] += d_lb
        rowc_ref[3:4, :] += jnp.concatenate(d_ps, axis=-1)

        dh = lax.dot_general(du_ref[...], win_v[...], (((1,), (1,)), ((), ())), preferred_element_type=F32)
        _, xn, r1 = _ada_norm(x_ref[...], g_pre, sc, sh)
        dxb, d_sh, d_sc, d_g = _ada_norm_bwd(dh, xn, r1, g_pre, sc)
        dx_ref[...] = dxo_ref[...] + dxb
        rowd_ref[0:1, :] += d_g
        rowd_ref[1:2, :] += d_gp
        rowb_ref[0:1, :] += d_sh
        rowb_ref[1:2, :] += d_sc
        rowb_ref[2:3, :] += d_gt

    def tile(width):
        return pl.BlockSpec((None, tm, width), lambda b, i: (b, n - 1 - i, 0))

    return _grid_call(
        body, name, (nb, n),
        in_specs=[tile(d), tile(d), tile(d), tile(din), tile(dc), tile(dpool),
                  pl.BlockSpec((None, 8, d), lambda b, i: (b, 0, 0)), _full(vec_d.shape), _full(vec_c.shape),
                  _full(cw.shape), _full(pw.shape), _ANY, _ANY],
        out_specs=[tile(d), tile(din), tile(d), _full((8, d)), pl.BlockSpec((None, 8, d), lambda b, i: (b, 0, 0)),
                   _full((8, dc)), _full((HALO, dc)), _full(pw.shape)],
        out_shape=[jax.ShapeDtypeStruct((nb, s, d), F32), jax.ShapeDtypeStruct((nb, s, din), BF16),
                   jax.ShapeDtypeStruct((nb, s, d), BF16), jax.ShapeDtypeStruct((8, d), F32),
                   jax.ShapeDtypeStruct((nb, 8, d), F32), jax.ShapeDtypeStruct((8, dc), F32),
                   jax.ShapeDtypeStruct((HALO, dc), F32), jax.ShapeDtypeStruct(pw.shape, F32)],
        scratch_shapes=[pltpu.VMEM((d, din), BF16), pltpu.VMEM((dmix, d), BF16),
                        pltpu.VMEM((rext, dc), F32), pltpu.VMEM((rext, dpool), F32),
                        pltpu.VMEM((SUBLANES - 1, rext, dc), F32), pltpu.SemaphoreType.DMA((2 * nj,))],
        args=(dxo, x, o, u, ac, dpl, mod, vec_d, vec_c, cw, pw, win_g, wout_g), exchange=exchange)


def _ffn_forward(x, mod, vec_d, fw, wup_g, wdn_g, tm, name, exchange=None):
    nb, s, d = x.shape
    n = s // tm
    nj, _, ucol = wup_g.shape
    f2 = nj * ucol
    dff = f2 // 2
    rd = wdn_g.shape[1]
    nq = nj // 2
    cs = dff // nq

    def body(x_ref, mod_ref, vd_ref, fw_ref, wup_hbm, wdn_hbm,
             xo_ref, h_ref, u_ref, uc_ref, hid_ref, o_ref,
             wup_v, wdn_v, prev_u, sems):
        b, i = pl.program_id(0), pl.program_id(1)
        pairs = [(wup_hbm.at[j], wup_v.at[:, pl.ds(j * ucol, ucol)]) for j in range(nj)]
        pairs += [(wdn_hbm.at[j], wdn_v.at[pl.ds(j * rd, rd), :]) for j in range(nj)]
        _load_weights((b == 0) & (i == 0), pairs, sems)

        @pl.when(i == 0)
        def _():
            prev_u[...] = jnp.zeros_like(prev_u)

        xv = x_ref[...]
        h, _, _ = _ada_norm(xv, vd_ref[2:3, :], mod_ref[4:5, :], mod_ref[3:4, :])
        hb = h.astype(BF16)
        h_ref[...] = hb

        def conv(cols):
            uc = jnp.dot(hb, wup_v[:, cols], preferred_element_type=F32)
            u_ref[:, cols] = uc.astype(BF16)
            before = prev_u[:, cols]
            prev_u[:, cols] = uc[tm - FHALO:, :]
            out = (fw_ref[3:4, cols] + fw_ref[2:3, cols] * uc + fw_ref[1:2, cols] * _rows_before(before, uc, 1)
                   + fw_ref[0:1, cols] * _rows_before(before, uc, 2))
            uc_ref[:, cols] = out.astype(BF16)
            return out

        o = jnp.zeros((tm, d), F32)
        for q in range(nq):
            val = conv(pl.ds(q * cs, cs))
            gate = conv(pl.ds(dff + q * cs, cs))
            hid = ((gate * _sigmoid(gate)) * val).astype(BF16)
            hid_ref[:, pl.ds(q * cs, cs)] = hid
            o = o + jnp.dot(hid, wdn_v[pl.ds(q * cs, cs), :], preferred_element_type=F32)
        o_ref[...] = o
        r2 = lax.rsqrt(jnp.mean(o * o, axis=-1, keepdims=True) + EPS)
        xo_ref[...] = xv + (1.0 + mod_ref[5:6, :]) * ((o * r2) * vd_ref[3:4, :])

    def tile(width):
        return pl.BlockSpec((None, tm, width), lambda b, i: (b, i, 0))

    return _grid_call(
        body, name, (nb, n),
        in_specs=[tile(d), pl.BlockSpec((None, 8, d), lambda b, i: (b, 0, 0)), _full(vec_d.shape), _full(fw.shape),
                  _ANY, _ANY],
        out_specs=[tile(d), tile(d), tile(f2), tile(f2), tile(dff), tile(d)],
        out_shape=[jax.ShapeDtypeStruct((nb, s, d), F32), jax.ShapeDtypeStruct((nb, s, d), BF16),
                   jax.ShapeDtypeStruct((nb, s, f2), BF16), jax.ShapeDtypeStruct((nb, s, f2), BF16),
                   jax.ShapeDtypeStruct((nb, s, dff), BF16), jax.ShapeDtypeStruct((nb, s, d), F32)],
        scratch_shapes=[pltpu.VMEM((d, f2), BF16), pltpu.VMEM((dff, d), BF16),
                        pltpu.VMEM((FHALO, f2), F32), pltpu.SemaphoreType.DMA((2 * nj,))],
        args=(x, mod, vec_d, fw, wup_g, wdn_g), exchange=exchange)


def _ffn_backward(dxo, x, o, u, uc, mod, vec_d, fw, wup_g, wdn_g, tm, name, exchange=None):
    nb, s, d = x.shape
    n = s // tm
    nj, _, ucol = wup_g.shape
    f2 = nj * ucol
    dff = f2 // 2
    rd = wdn_g.shape[1]
    nq = nj // 2
    cs = dff // nq

    def body(dxo_ref, x_ref, o_ref, u_ref, uc_ref, mod_ref, vd_ref, fw_ref, wup_hbm, wdn_hbm,
             dx_ref, du_ref, dob_ref, rowd_ref, rowb_ref, dfw_ref,
             wup_v, wdn_v, next_d, sems):
        b, i = pl.program_id(0), pl.program_id(1)
        first = (b == 0) & (i == 0)
        pairs = [(wup_hbm.at[j], wup_v.at[:, pl.ds(j * ucol, ucol)]) for j in range(nj)]
        pairs += [(wdn_hbm.at[j], wdn_v.at[pl.ds(j * rd, rd), :]) for j in range(nj)]
        _load_weights(first, pairs, sems)

        @pl.when(first)
        def _():
            rowd_ref[...] = jnp.zeros_like(rowd_ref)
            dfw_ref[...] = jnp.zeros_like(dfw_ref)

        @pl.when(i == 0)
        def _():
            rowb_ref[...] = jnp.zeros_like(rowb_ref)
            next_d[...] = jnp.zeros_like(next_d)

        g_pre, g_post = vd_ref[2:3, :], vd_ref[3:4, :]
        sh, sc, gt = mod_ref[3:4, :], mod_ref[4:5, :], mod_ref[5:6, :]
        do, d_gt, d_gp = _gated_residual_bwd(dxo_ref[...], o_ref[...], g_post, gt)
        dob = do.astype(BF16)
        dob_ref[...] = dob

        def conv_bwd(cols, duc):
            uc = u_ref[:, cols].astype(F32)
            after = next_d[:, cols]
            next_d[:, cols] = duc[0:FHALO, :]
            d1 = _rows_after(duc, after, 1)
            d2 = _rows_after(duc, after, 2)
            dfw_ref[3:4, cols] += _colsum(duc)
            dfw_ref[2:3, cols] += _colsum(uc * duc)
            dfw_ref[1:2, cols] += _colsum(uc * d1)
            dfw_ref[0:1, cols] += _colsum(uc * d2)
            ob = (fw_ref[2:3, cols] * duc + fw_ref[1:2, cols] * d1 + fw_ref[0:1, cols] * d2).astype(BF16)
            du_ref[:, cols] = ob
            return lax.dot_general(ob, wup_v[:, cols], (((1,), (1,)), ((), ())), preferred_element_type=F32)

        dh = jnp.zeros((tm, d), F32)
        for q in range(nq):
            vcols = pl.ds(q * cs, cs)
            gcols = pl.ds(dff + q * cs, cs)
            dhid = lax.dot_general(dob, wdn_v[vcols, :], (((1,), (1,)), ((), ())), preferred_element_type=F32)
            val = uc_ref[:, vcols].astype(F32)
            gate = uc_ref[:, gcols].astype(F32)
            sg = _sigmoid(gate)
            act = gate * sg
            dval = dhid * act
            dgate = (dhid * val) * (sg + act * (1.0 - sg))
            dh = dh + conv_bwd(vcols, dval)
            dh = dh + conv_bwd(gcols, dgate)

        _, xn, r1 = _ada_norm(x_ref[...], g_pre, sc, sh)
        dxb, d_sh, d_sc, d_g = _ada_norm_bwd(dh, xn, r1, g_pre, sc)
        dx_ref[...] = dxo_ref[...] + dxb
        rowd_ref[2:3, :] += d_g
        rowd_ref[3:4, :] += d_gp
        rowb_ref[3:4, :] += d_sh
        rowb_ref[4:5, :] += d_sc
        rowb_ref[5:6, :] += d_gt

    def tile(width):
        return pl.BlockSpec((None, tm, width), lambda b, i: (b, n - 1 - i, 0))

    return _grid_call(
        body, name, (nb, n),
        in_specs=[tile(d), tile(d), tile(d), tile(f2), tile(f2), pl.BlockSpec((None, 8, d), lambda b, i: (b, 0, 0)),
                  _full(vec_d.shape), _full(fw.shape), _ANY, _ANY],
        out_specs=[tile(d), tile(f2), tile(d), _full((8, d)), pl.BlockSpec((None, 8, d), lambda b, i: (b, 0, 0)),
                   _full(fw.shape)],
        out_shape=[jax.ShapeDtypeStruct((nb, s, d), F32), jax.ShapeDtypeStruct((nb, s, f2), BF16),
                   jax.ShapeDtypeStruct((nb, s, d), BF16), jax.ShapeDtypeStruct((8, d), F32),
                   jax.ShapeDtypeStruct((nb, 8, d), F32), jax.ShapeDtypeStruct(fw.shape, F32)],
        scratch_shapes=[pltpu.VMEM((d, f2), BF16), pltpu.VMEM((dff, d), BF16),
                        pltpu.VMEM((FHALO, f2), F32), pltpu.SemaphoreType.DMA((2 * nj,))],
        args=(dxo, x, o, u, uc, mod, vec_d, fw, wup_g, wdn_g), exchange=exchange)


def _weight_grad(a, b, nblk, split, tt, name, exchange=None):
    t, ka = a.shape
    nb_ = b.shape[1]
    nk = t // tt
    if split == "cols":
        wa, wb, grid = ka, nb_ // nblk, (1, nk)
        a_spec = pl.BlockSpec((tt, ka), lambda j, k: (k, 0))
        b_spec = pl.BlockSpec((tt, nb_), lambda j, k: (k, 0))
        o_spec = pl.BlockSpec((nblk, wa, wb), lambda j, k: (0, 0, 0))
        acc_shape = (ka, nb_)
    elif split == "b":
        wa, wb, grid = ka, nb_ // nblk, (nblk, nk)
        a_spec = pl.BlockSpec((tt, wa), lambda j, k: (k, 0))
        b_spec = pl.BlockSpec((tt, wb), lambda j, k: (k, j))
        o_spec = pl.BlockSpec((None, wa, wb), lambda j, k: (j, 0, 0))
        acc_shape = (wa, wb)
    else:
        wa, wb, grid = ka // nblk, nb_, (nblk, nk)
        a_spec = pl.BlockSpec((tt, wa), lambda j, k: (k, j))
        b_spec = pl.BlockSpec((tt, wb), lambda j, k: (k, 0))
        o_spec = pl.BlockSpec((None, wa, wb), lambda j, k: (j, 0, 0))
        acc_shape = (wa, wb)

    def body(a_ref, b_ref, o_ref, acc):
        k = pl.program_id(1)
        prod = lax.dot_general(a_ref[...], b_ref[...], (((0,), (0,)), ((), ())), preferred_element_type=F32)

        @pl.when(k == 0)
        def _():
            acc[...] = prod

        @pl.when(k > 0)
        def _():
            acc[...] += prod

        @pl.when(k == nk - 1)
        def _():
            if split == "cols":
                for j in range(nblk):
                    o_ref[j] = acc[:, j * wb:(j + 1) * wb].astype(o_ref.dtype)
            else:
                o_ref[...] = acc[...].astype(o_ref.dtype)

    outs, exo = _grid_call(body, name, grid, in_specs=[a_spec, b_spec], out_specs=[o_spec],
                           out_shape=[jax.ShapeDtypeStruct((nblk, wa, wb), BF16)],
                           scratch_shapes=[pltpu.VMEM(acc_shape, F32)], args=(a, b), exchange=exchange)
    return outs[0], exo


def _loss_grad(y, tgt, tm, name):
    nb, s, d = y.shape
    n = s // tm

    def body(y_ref, t_ref, dy_ref, sq_ref):
        @pl.when((pl.program_id(0) == 0) & (pl.program_id(1) == 0))
        def _():
            sq_ref[...] = jnp.zeros_like(sq_ref)

        e = y_ref[...] - t_ref[...]
        dy_ref[...] = e * (1.0 / d)
        sq_ref[0:1, :] += _colsum(e * e)

    tile = pl.BlockSpec((None, tm, d), lambda b, i: (b, i, 0))
    return pl.pallas_call(
        body, name=name, out_shape=[jax.ShapeDtypeStruct((nb, s, d), F32), jax.ShapeDtypeStruct((8, d), F32)],
        grid=(nb, n), in_specs=[tile, tile], out_specs=[tile, _full((8, d))],
        compiler_params=pltpu.CompilerParams(dimension_semantics=("arbitrary", "arbitrary")),
    )(y, tgt)


def _rows128(a):
    return a.reshape(-1, LANES)


class _ReduceScatter:
    def __init__(self, gs, cidx, idx, tag):
        self.gs, self.cidx, self.idx, self.tag = gs, cidx, idx, tag

    def swap(self):
        return _swap_halves(self.gs)

    def after_swap(self, r1):
        self.ps = [_pair_sum(g, r, self.cidx, name=f"rs_pair_{self.tag}_{a}") for a, (g, r) in enumerate(zip(self.gs, r1))]

    def chips(self):
        return _chip_exchange(self.ps)

    def after_chips(self, r2):
        self.fh = [_chip_sum(p, r, self.idx, name=f"rs_sum_{self.tag}_{a}") for a, (p, r) in enumerate(zip(self.ps, r2))]

    def share(self):
        return _sibling_share(self.fh)

    @staticmethod
    def result(fs):
        return [f.reshape(f.shape[0] * f.shape[1], f.shape[2]) for f in fs]


def kernel(x, c, ada_w, ada_b, pre_mix_g, post_mix_g, w_in, conv_w, conv_b, conv_ln_g, conv_ln_b, pool_w, pool_scale, w_out, pre_ffn_g, post_ffn_g, ffn_up, ffn_conv_w, ffn_conv_b, ffn_down, loss_target, m_ada_w, m_ada_b, m_pre_mix_g, m_post_mix_g, m_w_in, m_conv_w, m_conv_b, m_conv_ln_g, m_conv_ln_b, m_pool_w, m_pool_scale, m_w_out, m_pre_ffn_g, m_post_ffn_g, m_ffn_up, m_ffn_conv_w, m_ffn_conv_b, m_ffn_down, v_ada_w, v_ada_b, v_pre_mix_g, v_post_mix_g, v_w_in, v_conv_w, v_conv_b, v_conv_ln_g, v_conv_ln_b, v_pool_w, v_pool_scale, v_w_out, v_pre_ffn_g, v_post_ffn_g, v_ffn_up, v_ffn_conv_w, v_ffn_conv_b, v_ffn_down):
    nb, s, d = x.shape
    nl = w_in.shape[0]
    taps = conv_w.shape[1]
    ccol = conv_w.shape[2]
    dc = conv_b.shape[1]
    fcol = ffn_conv_w.shape[2]
    f2 = ffn_conv_b.shape[1]
    nmod = ada_b.shape[1] // d
    acol = ada_w.shape[2]
    tm = min(256, s)
    tm_mix = min(512, s)
    tt = min(2048, (nb * s) // 2)

    xi, yi, ci = _pos()
    jm = 2 * xi + yi
    cidx = jnp.reshape(ci, (1,)).astype(jnp.int32)
    idx = jnp.stack([jm, ci]).astype(jnp.int32)

    n_cw, n_fw, n_c = nl * taps * ccol, nl * 3 * fcol, nb * d
    packed = jnp.concatenate([conv_w.reshape(-1), ffn_conv_w.reshape(-1), c.reshape(-1)])
    got = _gather8(_rows128(packed), name="gather_small").reshape(N_DEV, -1)
    chips = got[0::2]
    cw_full = chips[:, :n_cw].reshape(N_CHIPS, nl, taps, ccol).transpose(1, 2, 0, 3).reshape(nl, taps, dc)
    fw_full = chips[:, n_cw:n_cw + n_fw].reshape(N_CHIPS, nl, 3, fcol).transpose(1, 2, 0, 3).reshape(nl, 3, f2)
    c_all = got[:, n_cw + n_fw:].reshape(N_DEV * nb, d)

    ada_b_cols = lax.dynamic_slice_in_dim(ada_b, jm * acol, acol, axis=1).reshape(nl, 1, acol)
    mod_cols = _ada_forward(c_all, ada_w, ada_b_cols, name="ada_forward")
    by_owner = mod_cols.reshape(nl, N_DEV, nb, acol).transpose(1, 0, 2, 3).reshape(N_DEV, -1, LANES)
    mod_own = _rows_to_owners(by_owner, name="mod_to_owners").reshape(N_CHIPS, nl, nb, acol)
    mod_own = mod_own.transpose(1, 2, 0, 3).reshape(nl, nb, nmod, d)
    mod_own = jnp.pad(mod_own, ((0, 0), (0, 0), (0, 8 - nmod), (0, 0)))

    vec_d = jnp.stack([pre_mix_g, post_mix_g, pre_ffn_g, post_ffn_g], axis=1)
    vec_c = jnp.stack([conv_b, conv_ln_g, conv_ln_b, pool_scale], axis=1)
    cw_pad = jnp.pad(cw_full, ((0, 0), (0, HALO - taps), (0, 0)))
    fw_rows = jnp.concatenate([fw_full, ffn_conv_b[:, None, :], jnp.zeros((nl, 4, f2), F32)], axis=1)
    pw_b = pool_w.astype(BF16)

    win_b, wout_b, wup_b, wdn_b = (w.astype(BF16) for w in (w_in, w_out, ffn_up, ffn_down))

    def others(l):
        return [win_b[l], wout_b[l], wdn_b[l]]

    win_g, wout_g, wdn_g = _whole(_run_exchange(_gather(others(0)), name="gather_0"))
    saved = []
    xs = x
    for l in range(nl):
        (x1, h1, u1, ac1, dp1, z1, o1), got = _mixer_forward(
            xs, mod_own[l], vec_d[l], vec_c[l], cw_pad[l], pw_b[l], win_g, wout_g, taps, tm_mix, name=f"mixer_fwd_{l}",
            exchange=_gather([wup_b[l]], mid_at=0.9))
        wup_g, = _whole(got)
        (x2, h2, u2, uc2, hid2, o2), nxt = _ffn_forward(
            x1, mod_own[l], vec_d[l], fw_rows[l], wup_g, wdn_g, tm, name=f"ffn_fwd_{l}",
            exchange=_gather(others(l + 1), mid_at=0.6) if l + 1 < nl else None)
        saved.append((xs, h1, u1, ac1, dp1, z1, o1, x1, h2, u2, uc2, hid2, o2, win_g, wout_g, wup_g, wdn_g))
        if l + 1 < nl:
            win_g, wout_g, wdn_g = _whole(nxt)
        xs = x2

    dx, sq = _loss_grad(xs, loss_target, tm, name="loss_grad")
    loss = lax.psum(0.5 * jnp.sum(sq) / d, ("x", "y", "c"))

    flat = lambda a: a.reshape(nb * s, a.shape[-1])
    small = [None] * nl
    big_mlp, big_mix = [None] * nl, [None] * nl
    mlp = mix = None
    for l in reversed(range(nl)):
        x0, h1, u1, ac1, dp1, z1, o1, x1, h2, u2, uc2, hid2, o2, win_g, wout_g, wup_g, wdn_g = saved[l]
        (dx, du2, do2, rowd2, rowb2, dfw), got = _ffn_backward(
            dx, x1, o2, u2, uc2, mod_own[l], vec_d[l], fw_rows[l], wup_g, wdn_g, tm, name=f"ffn_bwd_{l}",
            exchange=_combine([mlp.chips(), mix.swap()]) if mlp else None)
        if mlp:
            mlp.after_chips(got[:2])
            mix.after_swap(got[2:])
        g_up, got = _weight_grad(flat(h2), flat(du2), N_CHIPS, "b", tt, name=f"grad_ffn_up_{l}",
                                 exchange=_combine([mlp.share(), mix.chips()]) if mlp else None)
        if mlp:
            big_mlp[l + 1] = mlp.result(got[:2])
            mix.after_chips(got[2:])
        g_dn, _ = _weight_grad(flat(hid2), flat(do2), 2, "a", tt, name=f"grad_ffn_down_{l}")
        mlp_above, mlp = mlp, _ReduceScatter([g_up, g_dn.reshape(N_CHIPS, -1, d)], cidx, idx, f"mlp_{l}")
        if l == 0:
            mlp.after_swap(_run_exchange(mlp.swap(), name="rs_swap_mlp_0"))
        first = mlp.swap() if l > 0 else mlp.chips()
        (dx, du1, do1, rowd1, rowb1, rowc, dcw, dpw), got = _mixer_backward(
            dx, x0, o1, u1, ac1, dp1, mod_own[l], vec_d[l], vec_c[l], cw_pad[l], pw_b[l], win_g, wout_g, taps, tm_mix,
            name=f"mixer_bwd_{l}", exchange=_combine([first, mix.share()]) if mlp_above else first)
        if l > 0:
            mlp.after_swap(got[:2])
        else:
            mlp.after_chips(got[:2])
        if mlp_above:
            big_mix[l + 1] = mix.result(got[2:])
        g_in, got = _weight_grad(flat(h1), flat(du1), N_CHIPS, "cols", tt, name=f"grad_w_in_{l}",
                                 exchange=mlp.share() if l == 0 else None)
        if l == 0:
            big_mlp[0] = mlp.result(got)
        g_out, _ = _weight_grad(flat(z1), flat(do1), 1, "cols", tt, name=f"grad_w_out_{l}")
        mix = _ReduceScatter([g_in, g_out.reshape(N_CHIPS, -1, d)], cidx, idx, f"mix_{l}")
        small[l] = dict(rowd=rowd1 + rowd2, rowb=rowb1 + rowb2, rowc=rowc, dcw=dcw[:taps], dpw=dpw, dfw=dfw)
    mix.after_swap(_run_exchange(mix.swap(), name="rs_swap_mix_0"))
    mix.after_chips(_run_exchange(mix.chips(), name="rs_chips_mix_0"))
    big_mix[0] = mix.result(_run_exchange(mix.share(), name="rs_share_mix_0"))

    dmod_own = jnp.stack([small[l]["rowb"][:, :nmod, :] for l in range(nl)])
    dmod_all = _gather8(_rows128(dmod_own), name="gather_dmod").reshape(N_DEV, nl, nb, nmod * d)
    dmod_all = dmod_all.transpose(1, 0, 2, 3).reshape(nl, N_DEV * nb, nmod * d)
    dmod_cols = lax.dynamic_slice_in_dim(dmod_all, jm * acol, acol, axis=2)
    (g_ada_w, d_ada_w, nm_ada_w, nv_ada_w), _ = _ada_update(c_all, dmod_cols, ada_w, m_ada_w, v_ada_w, name="ada_update")

    def st(key, row=None):
        return jnp.stack([small[l][key] if row is None else small[l][key][row] for l in range(nl)])

    local = {
        "ada_b": dmod_own.sum(axis=1).reshape(nl, nmod * d),
        "pre_mix_g": st("rowd", 0), "post_mix_g": st("rowd", 1),
        "conv_b": st("rowc", 0), "conv_ln_g": st("rowc", 1), "conv_ln_b": st("rowc", 2),
        "pool_w": st("dpw"), "pool_scale": st("rowc", 3),
        "pre_ffn_g": st("rowd", 2), "post_ffn_g": st("rowd", 3),
        "ffn_conv_b": st("dfw", 3), "conv_w": st("dcw"), "ffn_conv_w": jnp.stack([small[l]["dfw"][:3] for l in range(nl)]),
    }
    names = list(local)
    sizes = [local[k].size for k in names]
    pad = -sum(sizes) % (2 * SUBLANES * LANES)
    packed = jnp.concatenate([local[k].reshape(-1) for k in names] + [jnp.zeros((pad,), F32)])
    summed = _allreduce8(_rows128(packed), name="allreduce_small").reshape(-1)
    grads, off = {}, 0
    for k, sz in zip(names, sizes):
        grads[k] = summed[off:off + sz].reshape(local[k].shape)
        off += sz
    grads["conv_w"] = lax.dynamic_slice_in_dim(grads["conv_w"], jm * ccol, ccol, axis=2)
    grads["ffn_conv_w"] = lax.dynamic_slice_in_dim(grads["ffn_conv_w"], jm * fcol, fcol, axis=2)

    params = dict(ada_b=(ada_b, m_ada_b, v_ada_b), pre_mix_g=(pre_mix_g, m_pre_mix_g, v_pre_mix_g),
                  post_mix_g=(post_mix_g, m_post_mix_g, v_post_mix_g), conv_b=(conv_b, m_conv_b, v_conv_b),
                  conv_ln_g=(conv_ln_g, m_conv_ln_g, v_conv_ln_g), conv_ln_b=(conv_ln_b, m_conv_ln_b, v_conv_ln_b),
                  pool_w=(pool_w, m_pool_w, v_pool_w), pool_scale=(pool_scale, m_pool_scale, v_pool_scale),
                  pre_ffn_g=(pre_ffn_g, m_pre_ffn_g, v_pre_ffn_g), post_ffn_g=(post_ffn_g, m_post_ffn_g, v_post_ffn_g),
                  ffn_conv_b=(ffn_conv_b, m_ffn_conv_b, v_ffn_conv_b), conv_w=(conv_w, m_conv_w, v_conv_w),
                  ffn_conv_w=(ffn_conv_w, m_ffn_conv_w, v_ffn_conv_w))
    pack = lambda i, g=None: _rows128(jnp.concatenate([(grads[k] if g else params[k][i]).reshape(-1) for k in names]))
    sd, sm, sv = _adamw_flat(pack(0), pack(0, True), pack(1), pack(2), name="adamw_small")
    outs = {}
    off = 0
    for k in names:
        shape, sz = params[k][0].shape, params[k][0].size
        outs[k] = (grads[k],) + tuple(a.reshape(-1)[off:off + sz].reshape(shape) for a in (sd, sm, sv))
        off += sz

    outs["ada_w"] = (g_ada_w, d_ada_w, nm_ada_w, nv_ada_w)
    for k, w, m, v, gs in [("w_in", w_in, m_w_in, v_w_in, [big_mix[l][0] for l in range(nl)]),
                           ("w_out", w_out, m_w_out, v_w_out, [big_mix[l][1] for l in range(nl)]),
                           ("ffn_up", ffn_up, m_ffn_up, v_ffn_up, [big_mlp[l][0] for l in range(nl)]),
                           ("ffn_down", ffn_down, m_ffn_down, v_ffn_down, [big_mlp[l][1] for l in range(nl)])]:
        outs[k] = tuple(_adamw_layers(w, m, v, gs, name=f"adamw_{k}"))

    order = ["ada_w", "ada_b", "pre_mix_g", "post_mix_g", "w_in", "conv_w", "conv_b", "conv_ln_g", "conv_ln_b", "pool_w",
             "pool_scale", "w_out", "pre_ffn_g", "post_ffn_g", "ffn_up", "ffn_conv_w", "ffn_conv_b", "ffn_down"]
    return (loss, dx) + tuple(outs[k][i] for i in range(4) for k in order)
```

```python
import functools

import jax
import jax.numpy as jnp
from jax import lax
from jax.experimental import pallas as pl
from jax.experimental.pallas import tpu as pltpu

F32 = jnp.float32
BF16 = jnp.bfloat16
MESH = pl.DeviceIdType.MESH

EPS = 1e-6
POOL_WINDOWS = (2, 4, 8, 16)
ADAM_LR = 0.001
ADAM_B1 = 0.9
ADAM_B2 = 0.999
ADAM_EPS = 1e-08
ADAM_WD = 0.01
ADAM_STEP = 10

N_CHIPS = 4
N_DEV = 8
LANES = 128
SUBLANES = 8
HALO = 32
FHALO = 8
VMEM_LIMIT = 60 * 1024 * 1024


def _pos():
    return lax.axis_index("x"), lax.axis_index("y"), lax.axis_index("c")


def _flip(v, f):
    return 1 - v if f else v


def _full(shape):
    nd = len(shape)
    return pl.BlockSpec(shape, lambda *_: (0,) * nd)


_ANY = pl.BlockSpec(memory_space=pl.ANY)
_VMEM = pl.BlockSpec(memory_space=pltpu.VMEM)


def _sigmoid(v):
    return 1.0 / (1.0 + jnp.exp(-v))


def _colsum(v):
    return jnp.sum(v, axis=0, keepdims=True)


def _gather8(v, name, exchange=None):
    r, ccols = v.shape
    ex = exchange
    nci, nco = (len(ex.ins), len(ex.outs)) if ex else (0, 0)

    def body(*refs):
        v_ref, cin, out_ref, cout = refs[0], refs[1:1 + nci], refs[1 + nci], refs[2 + nci:2 + nci + nco]
        send_sems, recv_sems, local_sem = refs[2 + nci + nco:5 + nci + nco]
        if ex:
            sems = _Sems(*refs[5 + nci + nco:])
            ex.start(cin, cout, sems)
        x, y, c = _pos()
        me = 4 * x + 2 * y + c
        mine = pltpu.make_async_copy(v_ref, out_ref.at[me], local_sem)
        mine.start()
        peers = [(_flip(x, (k >> 2) & 1), _flip(y, (k >> 1) & 1), _flip(c, k & 1)) for k in range(1, N_DEV)]
        sends = []
        for k, peer in enumerate(peers):
            cp = pltpu.make_async_remote_copy(src_ref=v_ref, dst_ref=out_ref.at[me], send_sem=send_sems.at[k],
                                              recv_sem=recv_sems.at[k], device_id=peer, device_id_type=MESH)
            cp.start()
            sends.append(cp)
        for k, peer in enumerate(peers):
            pidx = 4 * peer[0] + 2 * peer[1] + peer[2]
            pltpu.make_async_remote_copy(src_ref=v_ref, dst_ref=out_ref.at[pidx], send_sem=send_sems.at[k],
                                         recv_sem=recv_sems.at[k], device_id=peer, device_id_type=MESH).wait_recv()
        for cp in sends:
            cp.wait_send()
        mine.wait()
        if ex:
            if ex.mid is not None:
                ex.mid(cin, cout, sems)
            ex.finish(cin, cout, sems)

    outs = pl.pallas_call(
        body, name=name, out_shape=[jax.ShapeDtypeStruct((N_DEV, r, ccols), v.dtype)] + (ex.outs if ex else []),
        in_specs=[_VMEM] + [_ANY] * nci, out_specs=[_VMEM] + [_ANY] * nco,
        scratch_shapes=[pltpu.SemaphoreType.DMA((N_DEV - 1,)), pltpu.SemaphoreType.DMA((N_DEV - 1,)),
                        pltpu.SemaphoreType.DMA(())] + (ex.scratch() if ex else []),
        input_output_aliases={1 + a: 1 + b for a, b in ex.aliases.items()} if ex else {},
        compiler_params=pltpu.CompilerParams(vmem_limit_bytes=VMEM_LIMIT),
    )(v, *(ex.ins if ex else []))
    return (outs[0], list(outs[1:])) if ex else outs[0]


def _rows_to_owners(v, name):
    _, r, ccols = v.shape

    def body(v_ref, out_ref, send_sems, recv_sems, local_sem):
        x, y, c = _pos()
        jm = 2 * x + y
        mine = pltpu.make_async_copy(v_ref.at[2 * jm + c], out_ref.at[jm], local_sem)
        mine.start()
        peers, pjs = _chip_peers(x, y, c)
        sends = []
        for k, peer in enumerate(peers):
            cp = pltpu.make_async_remote_copy(src_ref=v_ref.at[2 * pjs[k] + c], dst_ref=out_ref.at[jm],
                                              send_sem=send_sems.at[k], recv_sem=recv_sems.at[k],
                                              device_id=peer, device_id_type=MESH)
            cp.start()
            sends.append(cp)
        for k, peer in enumerate(peers):
            pltpu.make_async_remote_copy(src_ref=v_ref.at[0], dst_ref=out_ref.at[pjs[k]], send_sem=send_sems.at[k],
                                         recv_sem=recv_sems.at[k], device_id=peer, device_id_type=MESH).wait_recv()
        for cp in sends:
            cp.wait_send()
        mine.wait()

    return pl.pallas_call(
        body, name=name, out_shape=jax.ShapeDtypeStruct((N_CHIPS, r, ccols), v.dtype),
        in_specs=[_VMEM], out_specs=_VMEM,
        scratch_shapes=[pltpu.SemaphoreType.DMA((N_CHIPS - 1,)), pltpu.SemaphoreType.DMA((N_CHIPS - 1,)),
                        pltpu.SemaphoreType.DMA(())],
        compiler_params=pltpu.CompilerParams(vmem_limit_bytes=VMEM_LIMIT),
    )(v)


def _allreduce8(v, name):
    r, ccols = v.shape
    h = r // 2
    q = h // 2

    def body(v_ref, out_ref, whole, part, done, send_sems, recv_sems):
        x, y, c = _pos()
        sib = (x, y, 1 - c)
        mine = pl.ds(pl.multiple_of(c * h, SUBLANES), h)
        theirs = pl.ds(pl.multiple_of((1 - c) * h, SUBLANES), h)
        quarters = [pl.ds(pl.multiple_of(c * h + k * q, SUBLANES), q) for k in range(2)]
        along_x, along_y = (1 - x, y, c), (x, 1 - y, c)

        def exchange(pairs):
            cps = [pltpu.make_async_remote_copy(src_ref=src, dst_ref=dst, send_sem=send_sems.at[k], recv_sem=recv_sems.at[k],
                                                device_id=peer, device_id_type=MESH) for src, dst, k, peer in pairs]
            for cp in cps:
                cp.start()
            for cp in cps:
                cp.wait()

        exchange([(v_ref, whole, 0, sib)])
        out_ref[...] = v_ref[...] + whole[...]
        for stage, peers in enumerate(((along_x, along_y), (along_y, along_x))):
            exchange([(out_ref.at[quarters[k]], part.at[2 * stage + k], 1 + 2 * stage + k, peers[k]) for k in range(2)])
            for k in range(2):
                out_ref[quarters[k], :] = out_ref[quarters[k], :] + part[2 * stage + k]
        exchange([(out_ref.at[mine], done, 5, sib)])
        out_ref[theirs, :] = done[...]

    return pl.pallas_call(
        body, name=name, out_shape=jax.ShapeDtypeStruct((r, ccols), v.dtype),
        in_specs=[_VMEM], out_specs=_VMEM,
        scratch_shapes=[pltpu.VMEM((r, ccols), v.dtype), pltpu.VMEM((4, q, ccols), v.dtype), pltpu.VMEM((h, ccols), v.dtype),
                        pltpu.SemaphoreType.DMA((6,)), pltpu.SemaphoreType.DMA((6,))],
        compiler_params=pltpu.CompilerParams(vmem_limit_bytes=VMEM_LIMIT),
    )(v)


def _chip_peers(x, y, c):
    peers = [(_flip(x, (k >> 1) & 1), _flip(y, k & 1), c) for k in range(1, N_CHIPS)]
    return peers, [2 * p[0] + p[1] for p in peers]


class _Exchange:
    def __init__(self, ins, outs, aliases, n_sems, n_local, start, finish, mid=None, mid_at=1.0):
        self.ins, self.outs, self.aliases = list(ins), list(outs), dict(aliases)
        self.n_sems, self.n_local, self.start, self.finish = n_sems, n_local, start, finish
        self.mid, self.mid_at = mid, mid_at

    def scratch(self):
        return [pltpu.SemaphoreType.DMA((self.n_sems,)), pltpu.SemaphoreType.DMA((self.n_sems,)),
                pltpu.SemaphoreType.DMA((max(self.n_local, 1),))]


class _Sems:
    def __init__(self, send, recv, local, base=0, lbase=0):
        self.send, self.recv, self.loc, self.base, self.lbase = send, recv, local, base, lbase

    def shifted(self, by, lby):
        return _Sems(self.send, self.recv, self.loc, self.base + by, self.lbase + lby)

    def local(self, k):
        return self.loc.at[self.lbase + k]


def _remote(src, dst, sems, k, peer):
    return pltpu.make_async_remote_copy(src_ref=src, dst_ref=dst, send_sem=sems.send.at[sems.base + k],
                                        recv_sem=sems.recv.at[sems.base + k], device_id=peer, device_id_type=MESH)


def _combine(exs):
    ins = [a for ex in exs for a in ex.ins]
    outs = [o for ex in exs for o in ex.outs]
    aliases, spans, ni, no, ns, nloc = {}, [], 0, 0, 0, 0
    for ex in exs:
        aliases.update({ni + a: no + b for a, b in ex.aliases.items()})
        spans.append((ni, no, ns, nloc))
        ni, no, ns, nloc = ni + len(ex.ins), no + len(ex.outs), ns + ex.n_sems, nloc + ex.n_local

    def each(which):
        def run(ins_, outs_, sems):
            for ex, (i0, o0, s0, l0) in zip(exs, spans):
                stage = getattr(ex, which)
                if stage is not None:
                    stage(ins_[i0:i0 + len(ex.ins)], outs_[o0:o0 + len(ex.outs)], sems.shifted(s0, l0))
        return run

    mids = [ex.mid_at for ex in exs if ex.mid is not None]
    return _Exchange(ins, outs, aliases, ns, nloc, each("start"), each("finish"),
                     mid=each("mid") if mids else None, mid_at=max(mids) if mids else 1.0)


def _gather(shards, mid_at=1.0):
    n = len(shards)
    per = N_CHIPS - 1
    halves = [s.reshape(2, s.shape[0] // 2, s.shape[1]) for s in shards]

    def copies(ins, outs, sems):
        x, y, c = _pos()
        jm = 2 * x + y
        sib = (x, y, 1 - c)
        peers, pjs = _chip_peers(x, y, c)
        sends, recvs, passes, passed = [], [], [], []
        for a in range(n):
            own = _remote(ins[a], outs[a].at[jm], sems, 2 * n * per + a, sib)
            sends.append(own)
            passed.append(own)
            for k, peer in enumerate(peers):
                landed, theirs = outs[a].at[pjs[k], c], outs[a].at[pjs[k], 1 - c]
                sends.append(_remote(ins[a].at[c], outs[a].at[jm, c], sems, 2 * (a * per + k), peer))
                recvs.append(_remote(landed, landed, sems, 2 * (a * per + k), peer))
                passes.append(_remote(landed, landed, sems, 2 * (a * per + k) + 1, sib))
                passed.append(_remote(theirs, theirs, sems, 2 * (a * per + k) + 1, sib))
        return sends, recvs, passes, passed

    def start(ins, outs, sems):
        for cp in copies(ins, outs, sems)[0]:
            cp.start()

    def mid(ins, outs, sems):
        _, recvs, passes, _ = copies(ins, outs, sems)
        for got, fwd in zip(recvs, passes):
            got.wait_recv()
            fwd.start()

    def finish(ins, outs, sems):
        sends, _, passes, passed = copies(ins, outs, sems)
        for cp in passed:
            cp.wait_recv()
        for cp in sends + passes:
            cp.wait_send()

    outs = [jax.ShapeDtypeStruct((N_CHIPS,) + h.shape, h.dtype) for h in halves]
    return _Exchange(halves, outs, {}, 2 * n * per + n, 0, start, finish, mid=mid, mid_at=mid_at)


def _whole(gathered):
    return [g.reshape(g.shape[0], g.shape[1] * g.shape[2], g.shape[3]) for g in gathered]


def _swap_halves(gs):
    n = len(gs)
    halves = [g.reshape(g.shape[0], 2, g.shape[1] // 2, g.shape[2]) for g in gs]

    def copies(ins, outs, sems):
        x, y, c = _pos()
        sib = (x, y, 1 - c)
        return [_remote(ins[a].at[:, 1 - c], outs[a], sems, a, sib) for a in range(n)]

    def start(ins, outs, sems):
        for cp in copies(ins, outs, sems):
            cp.start()

    def finish(ins, outs, sems):
        for cp in copies(ins, outs, sems):
            cp.wait()

    outs = [jax.ShapeDtypeStruct((g.shape[0], g.shape[1] // 2, g.shape[2]), g.dtype) for g in gs]
    return _Exchange(halves, outs, {}, n, 0, start, finish)


def _chip_exchange(ps):
    n = len(ps)
    per = N_CHIPS - 1

    def copies(ins, outs, sems):
        x, y, c = _pos()
        peers, pjs = _chip_peers(x, y, c)
        return [_remote(ins[a].at[pjs[k]], outs[a].at[k], sems, a * per + k, peer)
                for a in range(n) for k, peer in enumerate(peers)]

    def start(ins, outs, sems):
        for cp in copies(ins, outs, sems):
            cp.start()

    def finish(ins, outs, sems):
        for cp in copies(ins, outs, sems):
            cp.wait()

    outs = [jax.ShapeDtypeStruct((per,) + p.shape[1:], p.dtype) for p in ps]
    return _Exchange(ps, outs, {}, n * per, 0, start, finish)


def _sibling_share(fs):
    n = len(fs)

    def copies(outs, sems):
        x, y, c = _pos()
        sib = (x, y, 1 - c)
        sends = [_remote(outs[a].at[c], outs[a].at[c], sems, a, sib) for a in range(n)]
        recvs = [_remote(outs[a].at[1 - c], outs[a].at[1 - c], sems, a, sib) for a in range(n)]
        return sends, recvs

    def start(ins, outs, sems):
        for cp in copies(outs, sems)[0]:
            cp.start()

    def finish(ins, outs, sems):
        sends, recvs = copies(outs, sems)
        for cp in recvs:
            cp.wait_recv()
        for cp in sends:
            cp.wait_send()

    outs = [jax.ShapeDtypeStruct(f.shape, f.dtype) for f in fs]
    return _Exchange(fs, outs, {a: a for a in range(n)}, n, 0, start, finish)


def _run_exchange(ex, name):
    ni, no = len(ex.ins), len(ex.outs)

    def body(*refs):
        ins, outs, sems = refs[:ni], refs[ni:ni + no], _Sems(*refs[ni + no:])
        ex.start(ins, outs, sems)
        if ex.mid is not None:
            ex.mid(ins, outs, sems)
        ex.finish(ins, outs, sems)

    return pl.pallas_call(
        body, name=name, out_shape=ex.outs, in_specs=[_ANY] * ni, out_specs=[_ANY] * no,
        input_output_aliases=ex.aliases, scratch_shapes=ex.scratch(),
    )(*ex.ins)


def _grid_call(body, name, grid, in_specs, out_specs, out_shape, scratch_shapes, args, exchange=None):
    ni, no = len(in_specs), len(out_specs)
    params = pltpu.CompilerParams(dimension_semantics=("arbitrary",) * len(grid), vmem_limit_bytes=VMEM_LIMIT)
    if exchange is None:
        outs = pl.pallas_call(body, name=name, grid=grid, in_specs=in_specs, out_specs=out_specs, out_shape=out_shape,
                              scratch_shapes=scratch_shapes, compiler_params=params)(*args)
        return list(outs), []
    ex = exchange
    nci, nco = len(ex.ins), len(ex.outs)

    def hosted(*refs):
        cin = refs[ni:ni + nci]
        cout = refs[ni + nci + no:ni + nci + no + nco]
        sems = _Sems(*refs[len(refs) - 3:])
        main = refs[:ni] + refs[ni + nci:ni + nci + no] + refs[ni + nci + no + nco:len(refs) - 3]
        ids = [pl.program_id(a) for a in range(len(grid))]
        first = functools.reduce(lambda p, q: p & q, [i == 0 for i in ids])
        last = functools.reduce(lambda p, q: p & q, [i == g - 1 for i, g in zip(ids, grid)])

        @pl.when(first)
        def _():
            ex.start(cin, cout, sems)

        if ex.mid is not None:
            steps = functools.reduce(lambda p, q: p * q, grid)
            flat = functools.reduce(lambda p, q: p * q[1] + q[0], zip(ids[1:], grid[1:]), ids[0])

            @pl.when(flat == min(steps - 1, int(ex.mid_at * steps)))
            def _():
                ex.mid(cin, cout, sems)

        body(*main)

        @pl.when(last)
        def _():
            ex.finish(cin, cout, sems)

    outs = pl.pallas_call(
        hosted, name=name, grid=grid, in_specs=list(in_specs) + [_ANY] * nci, out_specs=list(out_specs) + [_ANY] * nco,
        out_shape=list(out_shape) + ex.outs, scratch_shapes=list(scratch_shapes) + ex.scratch(),
        input_output_aliases={ni + a: no + b for a, b in ex.aliases.items()}, compiler_params=params,
    )(*args, *ex.ins)
    return list(outs[:no]), list(outs[no:])


def _row_tile(rows, cols, itemsize, budget=2 * 1024 * 1024):
    best = None
    for t in range(16, rows + 1, 16):
        if rows % t == 0 and t * cols * itemsize <= budget:
            best = t
    return best if best is not None else rows


def _pair_sum(g, r1, cidx, name):
    nj, r, ccols = g.shape
    hr = r // 2
    tr = _row_tile(hr, ccols, 4)
    nt = hr // tr

    def body(c_ref, g_ref, r_ref, o_ref):
        o_ref[...] = (g_ref[...].astype(F32) + r_ref[...].astype(F32)).astype(o_ref.dtype)

    return pl.pallas_call(
        body, name=name, out_shape=jax.ShapeDtypeStruct((nj, hr, ccols), g.dtype),
        grid_spec=pltpu.PrefetchScalarGridSpec(
            num_scalar_prefetch=1, grid=(nj, nt),
            in_specs=[pl.BlockSpec((None, tr, ccols), lambda j, i, c_ref: (j, c_ref[0] * nt + i, 0)),
                      pl.BlockSpec((None, tr, ccols), lambda j, i, c_ref: (j, i, 0))],
            out_specs=pl.BlockSpec((None, tr, ccols), lambda j, i, c_ref: (j, i, 0))),
        compiler_params=pltpu.CompilerParams(dimension_semantics=("arbitrary", "arbitrary")),
    )(cidx, g, r1)


def _chip_sum(p, r2, idx, name):
    nj, hr, ccols = p.shape
    tr = _row_tile(hr, ccols, 4)
    nt = hr // tr

    def body(i_ref, p_ref, r_ref, o_ref):
        s = p_ref[...].astype(F32)
        for k in range(N_CHIPS - 1):
            s = s + r_ref[k].astype(F32)
        o_ref[...] = s

    return pl.pallas_call(
        body, name=name, out_shape=jax.ShapeDtypeStruct((2, hr, ccols), F32),
        grid_spec=pltpu.PrefetchScalarGridSpec(
            num_scalar_prefetch=1, grid=(nt,),
            in_specs=[pl.BlockSpec((None, tr, ccols), lambda i, i_ref: (i_ref[0], i, 0)),
                      pl.BlockSpec((N_CHIPS - 1, tr, ccols), lambda i, i_ref: (0, i, 0))],
            out_specs=pl.BlockSpec((None, tr, ccols), lambda i, i_ref: (i_ref[1], i, 0))),
        compiler_params=pltpu.CompilerParams(dimension_semantics=("arbitrary",)),
    )(idx, p, r2)


def _adam_math(w, g, m, v):
    m2 = ADAM_B1 * m + (1.0 - ADAM_B1) * g
    v2 = ADAM_B2 * v + (1.0 - ADAM_B2) * (g * g)
    m_hat = m2 / (1.0 - ADAM_B1 ** ADAM_STEP)
    v_hat = v2 / (1.0 - ADAM_B2 ** ADAM_STEP)
    delta = -ADAM_LR * (m_hat / (jnp.sqrt(v_hat) + ADAM_EPS) + ADAM_WD * w)
    return delta, m2, v2


def _adamw_layers(w, m, v, gs, name):
    nl, r, ccols = w.shape
    ng = len(gs)
    tr = _row_tile(r, ccols, 4, budget=1024 * 1024)
    nt = r // tr

    def body(w_ref, m_ref, v_ref, *rest):
        g_refs, (go_ref, d_ref, mo_ref, vo_ref) = rest[:ng], rest[ng:]
        l = pl.program_id(0)
        g = g_refs[0][...]
        for k in range(1, ng):
            g = jnp.where(l == k, g_refs[k][...], g)
        delta, m2, v2 = _adam_math(w_ref[...], g, m_ref[...], v_ref[...])
        go_ref[...] = g
        d_ref[...] = delta
        mo_ref[...] = m2
        vo_ref[...] = v2

    big = pl.BlockSpec((None, tr, ccols), lambda l, i: (l, i, 0))

    def gspec(k):
        return pl.BlockSpec((tr, ccols), lambda l, i: (jnp.where(l == k, i, jnp.where(l < k, 0, nt - 1)), 0))

    assert ng == nl
    return _grid_call(body, name, (nl, nt), in_specs=[big, big, big] + [gspec(k) for k in range(ng)],
                      out_specs=[big, big, big, big], out_shape=[jax.ShapeDtypeStruct(w.shape, F32)] * 4,
                      scratch_shapes=[], args=(w, m, v, *gs))[0]


def _adamw_flat(w, g, m, v, name):
    r, ccols = w.shape

    def body(w_ref, g_ref, m_ref, v_ref, d_ref, mo_ref, vo_ref):
        delta, m2, v2 = _adam_math(w_ref[...], g_ref[...], m_ref[...], v_ref[...])
        d_ref[...] = delta
        mo_ref[...] = m2
        vo_ref[...] = v2

    return pl.pallas_call(
        body, name=name, out_shape=[jax.ShapeDtypeStruct((r, ccols), F32)] * 3,
        in_specs=[_VMEM] * 4, out_specs=[_VMEM] * 3,
        compiler_params=pltpu.CompilerParams(vmem_limit_bytes=VMEM_LIMIT),
    )(w, g, m, v)


def _ada_forward(c_all, ada_w, ada_b_cols, name):
    nl, d, ncols = ada_w.shape
    bg = c_all.shape[0]
    tn = 512 if ncols % 512 == 0 else ncols

    def body(c_ref, w_ref, b_ref, o_ref):
        cv = c_ref[...]
        ca = (cv * _sigmoid(cv)).astype(BF16)
        o_ref[...] = jnp.dot(ca, w_ref[...].astype(BF16), preferred_element_type=F32) + b_ref[...]

    return pl.pallas_call(
        body, name=name, out_shape=jax.ShapeDtypeStruct((nl, bg, ncols), F32),
        grid=(nl, ncols // tn),
        in_specs=[pl.BlockSpec((bg, d), lambda l, j: (0, 0)),
                  pl.BlockSpec((None, d, tn), lambda l, j: (l, 0, j)),
                  pl.BlockSpec((None, 1, tn), lambda l, j: (l, 0, j))],
        out_specs=pl.BlockSpec((None, bg, tn), lambda l, j: (l, 0, j)),
        compiler_params=pltpu.CompilerParams(dimension_semantics=("arbitrary", "arbitrary")),
    )(c_all, ada_w, ada_b_cols)


def _ada_update(c_all, dmod_cols, w, m, v, name, exchange=None):
    nl, d, ncols = w.shape
    bg = c_all.shape[0]
    tn = 512 if ncols % 512 == 0 else ncols

    def body(c_ref, dm_ref, w_ref, m_ref, v_ref, go_ref, d_ref, mo_ref, vo_ref):
        cv = c_ref[...]
        ca = (cv * _sigmoid(cv)).astype(BF16)
        g = lax.dot_general(ca, dm_ref[...].astype(BF16), (((0,), (0,)), ((), ())), preferred_element_type=F32)
        delta, m2, v2 = _adam_math(w_ref[...], g, m_ref[...], v_ref[...])
        go_ref[...] = g
        d_ref[...] = delta
        mo_ref[...] = m2
        vo_ref[...] = v2

    big = pl.BlockSpec((None, d, tn), lambda l, j: (l, 0, j))
    return _grid_call(
        body, name, (nl, ncols // tn),
        in_specs=[pl.BlockSpec((bg, d), lambda l, j: (0, 0)),
                  pl.BlockSpec((None, bg, tn), lambda l, j: (l, 0, j)), big, big, big],
        out_specs=[big, big, big, big], out_shape=[jax.ShapeDtypeStruct(w.shape, F32)] * 4,
        scratch_shapes=[], args=(c_all, dmod_cols, w, m, v), exchange=exchange)


def _load_weights(first, pairs, sems):
    @pl.when(first)
    def _():
        cps = [pltpu.make_async_copy(src, dst, sems.at[k]) for k, (src, dst) in enumerate(pairs)]
        for cp in cps:
            cp.start()
        for cp in cps:
            cp.wait()


def _ada_norm(xv, g, sc, sh):
    r = lax.rsqrt(jnp.mean(xv * xv, axis=-1, keepdims=True) + EPS)
    xn = xv * r
    return (xn * g) * (1.0 + sc) + sh, xn, r


def _ada_norm_bwd(dh, xn, r, g, sc):
    d_sh = _colsum(dh)
    d_sc = _colsum(dh * (xn * g))
    dxg = dh * (1.0 + sc)
    d_g = _colsum(dxg * xn)
    gd = dxg * g
    dx = r * (gd - xn * jnp.mean(gd * xn, axis=-1, keepdims=True))
    return dx, d_sh, d_sc, d_g


def _gated_residual_bwd(dxo, o, g_post, gt):
    r = lax.rsqrt(jnp.mean(o * o, axis=-1, keepdims=True) + EPS)
    on = o * r
    d_gt = _colsum(dxo * (on * g_post))
    dy = dxo * (1.0 + gt)
    d_gp = _colsum(dy * on)
    gd = dy * g_post
    do = r * (gd - on * jnp.mean(gd * on, axis=-1, keepdims=True))
    return do, d_gt, d_gp


def _seq_positions(i, tm, width):
    return i * tm + lax.broadcasted_iota(jnp.int32, (tm, width), 0)


def _fill_phases(ext, phases):
    rows = ext.shape[0]
    ev = ext[...]
    for r in range(1, SUBLANES):
        phases[r - 1] = pltpu.roll(ev, rows - r, axis=0)


def _shifted_rows(ext, phases, offset, n):
    q, r = divmod(offset, SUBLANES)
    if r == 0:
        return ext[pl.ds(q * SUBLANES, n), :]
    return phases[r - 1, pl.ds(q * SUBLANES, n), :]


def _rows_before(halo, cur, shift):
    e = jnp.concatenate([halo, cur], axis=0)
    return pltpu.roll(e, shift, axis=0)[halo.shape[0]:, :]


def _rows_after(cur, halo, shift):
    e = jnp.concatenate([cur, halo], axis=0)
    return pltpu.roll(e, e.shape[0] - shift, axis=0)[:cur.shape[0], :]


def _mixer_forward(x, mod, vec_d, vec_c, cw, pw, win_g, wout_g, taps, tm, name, exchange=None):
    nb, s, d = x.shape
    n = s // tm
    nj, _, dcol = win_g.shape
    din = nj * dcol
    dc = vec_c.shape[-1]
    dpool = din - 2 * dc
    dmix = dc + dpool
    ro = wout_g.shape[1]
    ngrp = dpool // LANES

    def body(x_ref, mod_ref, vd_ref, vc_ref, cw_ref, pw_ref, win_hbm, wout_hbm,
             xo_ref, h_ref, u_ref, ac_ref, dp_ref, z_ref, o_ref,
             win_v, wout_v, ext_a, ext_p, phases, sems):
        b, i = pl.program_id(0), pl.program_id(1)
        pairs = [(win_hbm.at[j], win_v.at[:, pl.ds(j * dcol, dcol)]) for j in range(nj)]
        pairs += [(wout_hbm.at[j], wout_v.at[pl.ds(j * ro, ro), :]) for j in range(nj)]
        _load_weights((b == 0) & (i == 0), pairs, sems)

        xv = x_ref[...]
        h, _, _ = _ada_norm(xv, vd_ref[0:1, :], mod_ref[1:2, :], mod_ref[0:1, :])
        hb = h.astype(BF16)
        h_ref[...] = hb
        u = jnp.dot(hb, win_v[...], preferred_element_type=F32)
        u_ref[...] = u.astype(BF16)
        ag = u[:, :dc] * _sigmoid(u[:, dc:2 * dc])
        up = u[:, 2 * dc:]

        @pl.when(i == 0)
        def _():
            ext_a[0:HALO, :] = jnp.zeros((HALO, dc), F32)
            ext_p[0:HALO, :] = jnp.zeros((HALO, dpool), F32)

        @pl.when(i > 0)
        def _():
            ext_a[0:HALO, :] = ext_a[tm:tm + HALO, :]
            ext_p[0:HALO, :] = ext_p[tm:tm + HALO, :]

        ext_a[HALO:HALO + tm, :] = ag
        ext_p[HALO:HALO + tm, :] = up

        acc = jnp.broadcast_to(vc_ref[0:1, :], (tm, dc))
        _fill_phases(ext_a, phases)
        for k in range(taps):
            acc = acc + cw_ref[k:k + 1, :] * _shifted_rows(ext_a, phases, HALO - (taps - 1) + k, tm)
        ac_ref[...] = acc.astype(BF16)
        mu = jnp.mean(acc, axis=-1, keepdims=True)
        xc = acc - mu
        var = jnp.mean(xc * xc, axis=-1, keepdims=True)
        al = (xc * lax.rsqrt(var + EPS)) * vc_ref[1:2, :] + vc_ref[2:3, :]
        a = al * _sigmoid(al)

        pos = _seq_positions(i, tm, LANES)
        parts = [a.astype(BF16)]
        for g in range(ngrp):
            w = POOL_WINDOWS[g]
            cols = slice(g * LANES, (g + 1) * LANES)
            sw = ext_p[:, cols]
            step = 1
            while step < w:
                sw = sw + pltpu.roll(sw, step, axis=0)
                step *= 2
            cnt = jnp.minimum(pos + 1, w).astype(F32)
            dg = (sw[HALO:, :] / cnt - up[:, cols]).astype(BF16)
            dp_ref[:, cols] = dg
            q = jnp.dot(dg, pw_ref[g], preferred_element_type=F32)
            parts.append((q * vc_ref[3:4, cols]).astype(BF16))
        z = jnp.concatenate(parts, axis=-1)
        z_ref[...] = z
        o = jnp.dot(z, wout_v[...], preferred_element_type=F32)
        o_ref[...] = o
        r2 = lax.rsqrt(jnp.mean(o * o, axis=-1, keepdims=True) + EPS)
        xo_ref[...] = xv + (1.0 + mod_ref[2:3, :]) * ((o * r2) * vd_ref[1:2, :])

    def tile(width):
        return pl.BlockSpec((None, tm, width), lambda b, i: (b, i, 0))

    return _grid_call(
        body, name, (nb, n),
        in_specs=[tile(d), pl.BlockSpec((None, 8, d), lambda b, i: (b, 0, 0)), _full(vec_d.shape), _full(vec_c.shape),
                  _full(cw.shape), _full(pw.shape), _ANY, _ANY],
        out_specs=[tile(d), tile(d), tile(din), tile(dc), tile(dpool), tile(dmix), tile(d)],
        out_shape=[jax.ShapeDtypeStruct((nb, s, d), F32), jax.ShapeDtypeStruct((nb, s, d), BF16),
                   jax.ShapeDtypeStruct((nb, s, din), BF16), jax.ShapeDtypeStruct((nb, s, dc), BF16),
                   jax.ShapeDtypeStruct((nb, s, dpool), BF16), jax.ShapeDtypeStruct((nb, s, dmix), BF16),
                   jax.ShapeDtypeStruct((nb, s, d), F32)],
        scratch_shapes=[pltpu.VMEM((d, din), BF16), pltpu.VMEM((dmix, d), BF16),
                        pltpu.VMEM((HALO + tm, dc), F32), pltpu.VMEM((HALO + tm, dpool), F32),
                        pltpu.VMEM((SUBLANES - 1, HALO + tm, dc), F32), pltpu.SemaphoreType.DMA((2 * nj,))],
        args=(x, mod, vec_d, vec_c, cw, pw, win_g, wout_g), exchange=exchange)


def _mixer_backward(dxo, x, o, u, ac, dpl, mod, vec_d, vec_c, cw, pw, win_g, wout_g, taps, tm, name, exchange=None):
    nb, s, d = x.shape
    n = s // tm
    nj, _, dcol = win_g.shape
    din = nj * dcol
    dc = vec_c.shape[-1]
    dpool = din - 2 * dc
    dmix = dc + dpool
    ro = wout_g.shape[1]
    ngrp = dpool // LANES
    rext = tm + HALO

    def body(dxo_ref, x_ref, o_ref, u_ref, ac_ref, dp_ref, mod_ref, vd_ref, vc_ref, cw_ref, pw_ref, win_hbm, wout_hbm,
             dx_ref, du_ref, dob_ref, rowd_ref, rowb_ref, rowc_ref, dcw_ref, dpw_ref,
             win_v, wout_v, ext_a, ext_p, phases, sems):
        b, i = pl.program_id(0), pl.program_id(1)
        first = (b == 0) & (i == 0)
        pairs = [(win_hbm.at[j], win_v.at[:, pl.ds(j * dcol, dcol)]) for j in range(nj)]
        pairs += [(wout_hbm.at[j], wout_v.at[pl.ds(j * ro, ro), :]) for j in range(nj)]
        _load_weights(first, pairs, sems)

        @pl.when(first)
        def _():
            rowd_ref[...] = jnp.zeros_like(rowd_ref)
            rowc_ref[...] = jnp.zeros_like(rowc_ref)
            dcw_ref[...] = jnp.zeros_like(dcw_ref)
            dpw_ref[...] = jnp.zeros_like(dpw_ref)

        @pl.when(i == 0)
        def _():
            rowb_ref[...] = jnp.zeros_like(rowb_ref)
            ext_a[tm:rext, :] = jnp.zeros((HALO, dc), F32)
            ext_p[tm:rext, :] = jnp.zeros((HALO, dpool), F32)

        @pl.when(i > 0)
        def _():
            ext_a[tm:rext, :] = ext_a[0:HALO, :]
            ext_p[tm:rext, :] = ext_p[0:HALO, :]

        g_pre, g_post = vd_ref[0:1, :], vd_ref[1:2, :]
        sh, sc, gt = mod_ref[0:1, :], mod_ref[1:2, :], mod_ref[2:3, :]
        do, d_gt, d_gp = _gated_residual_bwd(dxo_ref[...], o_ref[...], g_post, gt)
        dob = do.astype(BF16)
        dob_ref[...] = dob
        dz = lax.dot_general(dob, wout_v[...], (((1,), (1,)), ((), ())), preferred_element_type=F32)

        acv = ac_ref[...].astype(F32)
        mu = jnp.mean(acv, axis=-1, keepdims=True)
        xc = acv - mu
        rstd = lax.rsqrt(jnp.mean(xc * xc, axis=-1, keepdims=True) + EPS)
        an = xc * rstd
        lg = vc_ref[1:2, :]
        al = an * lg + vc_ref[2:3, :]
        sg = _sigmoid(al)
        dal = dz[:, :dc] * (sg * (1.0 + al * (1.0 - sg)))
        d_lg = _colsum(dal * an)
        d_lb = _colsum(dal)
        dan = dal * lg
        dac = rstd * (dan - jnp.mean(dan, axis=-1, keepdims=True) - an * jnp.mean(dan * an, axis=-1, keepdims=True))
        d_cb = _colsum(dac)
        ext_a[0:tm, :] = dac
        uv = u_ref[:, 0:dc].astype(F32)
        sgu = _sigmoid(u_ref[:, dc:2 * dc].astype(F32))
        ag = uv * sgu
        dag = jnp.zeros((tm, dc), F32)
        _fill_phases(ext_a, phases)
        for k in range(taps):
            sl = _shifted_rows(ext_a, phases, taps - 1 - k, tm)
            dag = dag + cw_ref[k:k + 1, :] * sl
            dcw_ref[k:k + 1, :] += _colsum(ag * sl)
        du_ref[:, 0:dc] = (dag * sgu).astype(BF16)
        du_ref[:, dc:2 * dc] = (dag * uv * (sgu * (1.0 - sgu))).astype(BF16)

        pos = _seq_positions(n - 1 - i, tm, LANES)
        d_ps = []
        for g in range(ngrp):
            w = POOL_WINDOWS[g]
            cols = slice(g * LANES, (g + 1) * LANES)
            gcols = slice(dc + g * LANES, dc + (g + 1) * LANES)
            dgb = dp_ref[:, cols]
            q = jnp.dot(dgb, pw_ref[g], preferred_element_type=F32)
            dpg = dz[:, gcols]
            d_ps.append(_colsum(dpg * q))
            dq = (dpg * vc_ref[3:4, cols]).astype(BF16)
            dpw_ref[g] += lax.dot_general(dgb, dq, (((0,), (0,)), ((), ())), preferred_element_type=F32)
            dd = lax.dot_general(dq, pw_ref[g], (((1,), (1,)), ((), ())), preferred_element_type=F32)
            cnt = jnp.minimum(pos + 1, w).astype(F32)
            ext_p[0:tm, cols] = dd / cnt
            sw = ext_p[:, cols]
            step = 1
            while step < w:
                sw = sw + pltpu.roll(sw, rext - step, axis=0)
                step *= 2
            du_ref[:, 2 * dc + g * LANES:2 * dc + (g + 1) * LANES] = (sw[0:tm, :] - dd).astype(BF16)
        rowc_ref[0:1, :] += d_cb
        rowc_ref[1:2, :] += d_lg
        rowc_ref[2:3, :] += d_lb
        rowc_ref[3:4, :] += jnp.concatenate(d_ps, axis=-1)

        dh = lax.dot_general(du_ref[...], win_v[...], (((1,), (1,)), ((), ())), preferred_element_type=F32)
        _, xn, r1 = _ada_norm(x_ref[...], g_pre, sc, sh)
        dxb, d_sh, d_sc, d_g = _ada_norm_bwd(dh, xn, r1, g_pre, sc)
        dx_ref[...] = dxo_ref[...] + dxb
        rowd_ref[0:1, :] += d_g
        rowd_ref[1:2, :] += d_gp
        rowb_ref[0:1, :] += d_sh
        rowb_ref[1:2, :] += d_sc
        rowb_ref[2:3, :] += d_gt

    def tile(width):
        return pl.BlockSpec((None, tm, width), lambda b, i: (b, n - 1 - i, 0))

    return _grid_call(
        body, name, (nb, n),
        in_specs=[tile(d), tile(d), tile(d), tile(din), tile(dc), tile(dpool),
                  pl.BlockSpec((None, 8, d), lambda b, i: (b, 0, 0)), _full(vec_d.shape), _full(vec_c.shape),
                  _full(cw.shape), _full(pw.shape), _ANY, _ANY],
        out_specs=[tile(d), tile(din), tile(d), _full((8, d)), pl.BlockSpec((None, 8, d), lambda b, i: (b, 0, 0)),
                   _full((8, dc)), _full((HALO, dc)), _full(pw.shape)],
        out_shape=[jax.ShapeDtypeStruct((nb, s, d), F32), jax.ShapeDtypeStruct((nb, s, din), BF16),
                   jax.ShapeDtypeStruct((nb, s, d), BF16), jax.ShapeDtypeStruct((8, d), F32),
                   jax.ShapeDtypeStruct((nb, 8, d), F32), jax.ShapeDtypeStruct((8, dc), F32),
                   jax.ShapeDtypeStruct((HALO, dc), F32), jax.ShapeDtypeStruct(pw.shape, F32)],
        scratch_shapes=[pltpu.VMEM((d, din), BF16), pltpu.VMEM((dmix, d), BF16),
                        pltpu.VMEM((rext, dc), F32), pltpu.VMEM((rext, dpool), F32),
                        pltpu.VMEM((SUBLANES - 1, rext, dc), F32), pltpu.SemaphoreType.DMA((2 * nj,))],
        args=(dxo, x, o, u, ac, dpl, mod, vec_d, vec_c, cw, pw, win_g, wout_g), exchange=exchange)


def _ffn_forward(x, mod, vec_d, fw, wup_g, wdn_g, tm, name, exchange=None):
    nb, s, d = x.shape
    n = s // tm
    nj, _, ucol = wup_g.shape
    f2 = nj * ucol
    dff = f2 // 2
    rd = wdn_g.shape[1]
    nq = nj // 2
    cs = dff // nq

    def body(x_ref, mod_ref, vd_ref, fw_ref, wup_hbm, wdn_hbm,
             xo_ref, h_ref, u_ref, uc_ref, hid_ref, o_ref,
             wup_v, wdn_v, prev_u, sems):
        b, i = pl.program_id(0), pl.program_id(1)
        pairs = [(wup_hbm.at[j], wup_v.at[:, pl.ds(j * ucol, ucol)]) for j in range(nj)]
        pairs += [(wdn_hbm.at[j], wdn_v.at[pl.ds(j * rd, rd), :]) for j in range(nj)]
        _load_weights((b == 0) & (i == 0), pairs, sems)

        @pl.when(i == 0)
        def _():
            prev_u[...] = jnp.zeros_like(prev_u)

        xv = x_ref[...]
        h, _, _ = _ada_norm(xv, vd_ref[2:3, :], mod_ref[4:5, :], mod_ref[3:4, :])
        hb = h.astype(BF16)
        h_ref[...] = hb

        def conv(cols):
            uc = jnp.dot(hb, wup_v[:, cols], preferred_element_type=F32)
            u_ref[:, cols] = uc.astype(BF16)
            before = prev_u[:, cols]
            prev_u[:, cols] = uc[tm - FHALO:, :]
            out = (fw_ref[3:4, cols] + fw_ref[2:3, cols] * uc + fw_ref[1:2, cols] * _rows_before(before, uc, 1)
                   + fw_ref[0:1, cols] * _rows_before(before, uc, 2))
            uc_ref[:, cols] = out.astype(BF16)
            return out

        o = jnp.zeros((tm, d), F32)
        for q in range(nq):
            val = conv(pl.ds(q * cs, cs))
            gate = conv(pl.ds(dff + q * cs, cs))
            hid = ((gate * _sigmoid(gate)) * val).astype(BF16)
            hid_ref[:, pl.ds(q * cs, cs)] = hid
            o = o + jnp.dot(hid, wdn_v[pl.ds(q * cs, cs), :], preferred_element_type=F32)
        o_ref[...] = o
        r2 = lax.rsqrt(jnp.mean(o * o, axis=-1, keepdims=True) + EPS)
        xo_ref[...] = xv + (1.0 + mod_ref[5:6, :]) * ((o * r2) * vd_ref[3:4, :])

    def tile(width):
        return pl.BlockSpec((None, tm, width), lambda b, i: (b, i, 0))

    return _grid_call(
        body, name, (nb, n),
        in_specs=[tile(d), pl.BlockSpec((None, 8, d), lambda b, i: (b, 0, 0)), _full(vec_d.shape), _full(fw.shape),
                  _ANY, _ANY],
        out_specs=[tile(d), tile(d), tile(f2), tile(f2), tile(dff), tile(d)],
        out_shape=[jax.ShapeDtypeStruct((nb, s, d), F32), jax.ShapeDtypeStruct((nb, s, d), BF16),
                   jax.ShapeDtypeStruct((nb, s, f2), BF16), jax.ShapeDtypeStruct((nb, s, f2), BF16),
                   jax.ShapeDtypeStruct((nb, s, dff), BF16), jax.ShapeDtypeStruct((nb, s, d), F32)],
        scratch_shapes=[pltpu.VMEM((d, f2), BF16), pltpu.VMEM((dff, d), BF16),
                        pltpu.VMEM((FHALO, f2), F32), pltpu.SemaphoreType.DMA((2 * nj,))],
        args=(x, mod, vec_d, fw, wup_g, wdn_g), exchange=exchange)


def _ffn_backward(dxo, x, o, u, uc, mod, vec_d, fw, wup_g, wdn_g, tm, name, exchange=None):
    nb, s, d = x.shape
    n = s // tm
    nj, _, ucol = wup_g.shape
    f2 = nj * ucol
    dff = f2 // 2
    rd = wdn_g.shape[1]
    nq = nj // 2
    cs = dff // nq

    def body(dxo_ref, x_ref, o_ref, u_ref, uc_ref, mod_ref, vd_ref, fw_ref, wup_hbm, wdn_hbm,
             dx_ref, du_ref, dob_ref, rowd_ref, rowb_ref, dfw_ref,
             wup_v, wdn_v, next_d, sems):
        b, i = pl.program_id(0), pl.program_id(1)
        first = (b == 0) & (i == 0)
        pairs = [(wup_hbm.at[j], wup_v.at[:, pl.ds(j * ucol, ucol)]) for j in range(nj)]
        pairs += [(wdn_hbm.at[j], wdn_v.at[pl.ds(j * rd, rd), :]) for j in range(nj)]
        _load_weights(first, pairs, sems)

        @pl.when(first)
        def _():
            rowd_ref[...] = jnp.zeros_like(rowd_ref)
            dfw_ref[...] = jnp.zeros_like(dfw_ref)

        @pl.when(i == 0)
        def _():
            rowb_ref[...] = jnp.zeros_like(rowb_ref)
            next_d[...] = jnp.zeros_like(next_d)

        g_pre, g_post = vd_ref[2:3, :], vd_ref[3:4, :]
        sh, sc, gt = mod_ref[3:4, :], mod_ref[4:5, :], mod_ref[5:6, :]
        do, d_gt, d_gp = _gated_residual_bwd(dxo_ref[...], o_ref[...], g_post, gt)
        dob = do.astype(BF16)
        dob_ref[...] = dob

        def conv_bwd(cols, duc):
            uc = u_ref[:, cols].astype(F32)
            after = next_d[:, cols]
            next_d[:, cols] = duc[0:FHALO, :]
            d1 = _rows_after(duc, after, 1)
            d2 = _rows_after(duc, after, 2)
            dfw_ref[3:4, cols] += _colsum(duc)
            dfw_ref[2:3, cols] += _colsum(uc * duc)
            dfw_ref[1:2, cols] += _colsum(uc * d1)
            dfw_ref[0:1, cols] += _colsum(uc * d2)
            ob = (fw_ref[2:3, cols] * duc + fw_ref[1:2, cols] * d1 + fw_ref[0:1, cols] * d2).astype(BF16)
            du_ref[:, cols] = ob
            return lax.dot_general(ob, wup_v[:, cols], (((1,), (1,)), ((), ())), preferred_element_type=F32)

        dh = jnp.zeros((tm, d), F32)
        for q in range(nq):
            vcols = pl.ds(q * cs, cs)
            gcols = pl.ds(dff + q * cs, cs)
            dhid = lax.dot_general(dob, wdn_v[vcols, :], (((1,), (1,)), ((), ())), preferred_element_type=F32)
            val = uc_ref[:, vcols].astype(F32)
            gate = uc_ref[:, gcols].astype(F32)
            sg = _sigmoid(gate)
            act = gate * sg
            dval = dhid * act
            dgate = (dhid * val) * (sg + act * (1.0 - sg))
            dh = dh + conv_bwd(vcols, dval)
            dh = dh + conv_bwd(gcols, dgate)

        _, xn, r1 = _ada_norm(x_ref[...], g_pre, sc, sh)
        dxb, d_sh, d_sc, d_g = _ada_norm_bwd(dh, xn, r1, g_pre, sc)
        dx_ref[...] = dxo_ref[...] + dxb
        rowd_ref[2:3, :] += d_g
        rowd_ref[3:4, :] += d_gp
        rowb_ref[3:4, :] += d_sh
        rowb_ref[4:5, :] += d_sc
        rowb_ref[5:6, :] += d_gt

    def tile(width):
        return pl.BlockSpec((None, tm, width), lambda b, i: (b, n - 1 - i, 0))

    return _grid_call(
        body, name, (nb, n),
        in_specs=[tile(d), tile(d), tile(d), tile(f2), tile(f2), pl.BlockSpec((None, 8, d), lambda b, i: (b, 0, 0)),
                  _full(vec_d.shape), _full(fw.shape), _ANY, _ANY],
        out_specs=[tile(d), tile(f2), tile(d), _full((8, d)), pl.BlockSpec((None, 8, d), lambda b, i: (b, 0, 0)),
                   _full(fw.shape)],
        out_shape=[jax.ShapeDtypeStruct((nb, s, d), F32), jax.ShapeDtypeStruct((nb, s, f2), BF16),
                   jax.ShapeDtypeStruct((nb, s, d), BF16), jax.ShapeDtypeStruct((8, d), F32),
                   jax.ShapeDtypeStruct((nb, 8, d), F32), jax.ShapeDtypeStruct(fw.shape, F32)],
        scratch_shapes=[pltpu.VMEM((d, f2), BF16), pltpu.VMEM((dff, d), BF16),
                        pltpu.VMEM((FHALO, f2), F32), pltpu.SemaphoreType.DMA((2 * nj,))],
        args=(dxo, x, o, u, uc, mod, vec_d, fw, wup_g, wdn_g), exchange=exchange)


def _weight_grad(a, b, nblk, split, tt, name, exchange=None):
    t, ka = a.shape
    nb_ = b.shape[1]
    nk = t // tt
    if split == "cols":
        wa, wb, grid = ka, nb_ // nblk, (1, nk)
        a_spec = pl.BlockSpec((tt, ka), lambda j, k: (k, 0))
        b_spec = pl.BlockSpec((tt, nb_), lambda j, k: (k, 0))
        o_spec = pl.BlockSpec((nblk, wa, wb), lambda j, k: (0, 0, 0))
        acc_shape = (ka, nb_)
    elif split == "b":
        wa, wb, grid = ka, nb_ // nblk, (nblk, nk)
        a_spec = pl.BlockSpec((tt, wa), lambda j, k: (k, 0))
        b_spec = pl.BlockSpec((tt, wb), lambda j, k: (k, j))
        o_spec = pl.BlockSpec((None, wa, wb), lambda j, k: (j, 0, 0))
        acc_shape = (wa, wb)
    else:
        wa, wb, grid = ka // nblk, nb_, (nblk, nk)
        a_spec = pl.BlockSpec((tt, wa), lambda j, k: (k, j))
        b_spec = pl.BlockSpec((tt, wb), lambda j, k: (k, 0))
        o_spec = pl.BlockSpec((None, wa, wb), lambda j, k: (j, 0, 0))
        acc_shape = (wa, wb)

    def body(a_ref, b_ref, o_ref, acc):
        k = pl.program_id(1)
        prod = lax.dot_general(a_ref[...], b_ref[...], (((0,), (0,)), ((), ())), preferred_element_type=F32)

        @pl.when(k == 0)
        def _():
            acc[...] = prod

        @pl.when(k > 0)
        def _():
            acc[...] += prod

        @pl.when(k == nk - 1)
        def _():
            if split == "cols":
                for j in range(nblk):
                    o_ref[j] = acc[:, j * wb:(j + 1) * wb].astype(o_ref.dtype)
            else:
                o_ref[...] = acc[...].astype(o_ref.dtype)

    outs, exo = _grid_call(body, name, grid, in_specs=[a_spec, b_spec], out_specs=[o_spec],
                           out_shape=[jax.ShapeDtypeStruct((nblk, wa, wb), BF16)],
                           scratch_shapes=[pltpu.VMEM(acc_shape, F32)], args=(a, b), exchange=exchange)
    return outs[0], exo


def _loss_grad(y, tgt, tm, name):
    nb, s, d = y.shape
    n = s // tm

    def body(y_ref, t_ref, dy_ref, sq_ref):
        @pl.when((pl.program_id(0) == 0) & (pl.program_id(1) == 0))
        def _():
            sq_ref[...] = jnp.zeros_like(sq_ref)

        e = y_ref[...] - t_ref[...]
        dy_ref[...] = e * (1.0 / d)
        sq_ref[0:1, :] += _colsum(e * e)

    tile = pl.BlockSpec((None, tm, d), lambda b, i: (b, i, 0))
    return pl.pallas_call(
        body, name=name, out_shape=[jax.ShapeDtypeStruct((nb, s, d), F32), jax.ShapeDtypeStruct((8, d), F32)],
        grid=(nb, n), in_specs=[tile, tile], out_specs=[tile, _full((8, d))],
        compiler_params=pltpu.CompilerParams(dimension_semantics=("arbitrary", "arbitrary")),
    )(y, tgt)


def _rows128(a):
    return a.reshape(-1, LANES)


class _ReduceScatter:
    def __init__(self, gs, cidx, idx, tag):
        self.gs, self.cidx, self.idx, self.tag = gs, cidx, idx, tag

    def swap(self):
        return _swap_halves(self.gs)

    def after_swap(self, r1):
        self.ps = [_pair_sum(g, r, self.cidx, name=f"rs_pair_{self.tag}_{a}") for a, (g, r) in enumerate(zip(self.gs, r1))]

    def chips(self):
        return _chip_exchange(self.ps)

    def after_chips(self, r2):
        self.fh = [_chip_sum(p, r, self.idx, name=f"rs_sum_{self.tag}_{a}") for a, (p, r) in enumerate(zip(self.ps, r2))]

    def share(self):
        return _sibling_share(self.fh)

    @staticmethod
    def result(fs):
        return [f.reshape(f.shape[0] * f.shape[1], f.shape[2]) for f in fs]


def kernel(x, c, ada_w, ada_b, pre_mix_g, post_mix_g, w_in, conv_w, conv_b, conv_ln_g, conv_ln_b, pool_w, pool_scale, w_out, pre_ffn_g, post_ffn_g, ffn_up, ffn_conv_w, ffn_conv_b, ffn_down, loss_target, m_ada_w, m_ada_b, m_pre_mix_g, m_post_mix_g, m_w_in, m_conv_w, m_conv_b, m_conv_ln_g, m_conv_ln_b, m_pool_w, m_pool_scale, m_w_out, m_pre_ffn_g, m_post_ffn_g, m_ffn_up, m_ffn_conv_w, m_ffn_conv_b, m_ffn_down, v_ada_w, v_ada_b, v_pre_mix_g, v_post_mix_g, v_w_in, v_conv_w, v_conv_b, v_conv_ln_g, v_conv_ln_b, v_pool_w, v_pool_scale, v_w_out, v_pre_ffn_g, v_post_ffn_g, v_ffn_up, v_ffn_conv_w, v_ffn_conv_b, v_ffn_down):
    nb, s, d = x.shape
    nl = w_in.shape[0]
    taps = conv_w.shape[1]
    ccol = conv_w.shape[2]
    dc = conv_b.shape[1]
    fcol = ffn_conv_w.shape[2]
    f2 = ffn_conv_b.shape[1]
    nmod = ada_b.shape[1] // d
    acol = ada_w.shape[2]
    tm = min(256, s)
    tm_mix = min(512, s)
    tt = min(2048, (nb * s) // 2)

    xi, yi, ci = _pos()
    jm = 2 * xi + yi
    cidx = jnp.reshape(ci, (1,)).astype(jnp.int32)
    idx = jnp.stack([jm, ci]).astype(jnp.int32)

    win_b, wout_b, wup_b, wdn_b = (w.astype(BF16) for w in (w_in, w_out, ffn_up, ffn_down))

    def others(l):
        return [win_b[l], wout_b[l], wdn_b[l]]

    n_cw, n_fw, n_c = nl * taps * ccol, nl * 3 * fcol, nb * d
    packed = jnp.concatenate([conv_w.reshape(-1), ffn_conv_w.reshape(-1), c.reshape(-1)])
    got, first_weights = _gather8(_rows128(packed), name="gather_small", exchange=_gather(others(0)))
    got = got.reshape(N_DEV, -1)
    chips = got[0::2]
    cw_full = chips[:, :n_cw].reshape(N_CHIPS, nl, taps, ccol).transpose(1, 2, 0, 3).reshape(nl, taps, dc)
    fw_full = chips[:, n_cw:n_cw + n_fw].reshape(N_CHIPS, nl, 3, fcol).transpose(1, 2, 0, 3).reshape(nl, 3, f2)
    c_all = got[:, n_cw + n_fw:].reshape(N_DEV * nb, d)

    ada_b_cols = lax.dynamic_slice_in_dim(ada_b, jm * acol, acol, axis=1).reshape(nl, 1, acol)
    mod_cols = _ada_forward(c_all, ada_w, ada_b_cols, name="ada_forward")
    by_owner = mod_cols.reshape(nl, N_DEV, nb, acol).transpose(1, 0, 2, 3).reshape(N_DEV, -1, LANES)
    mod_own = _rows_to_owners(by_owner, name="mod_to_owners").reshape(N_CHIPS, nl, nb, acol)
    mod_own = mod_own.transpose(1, 2, 0, 3).reshape(nl, nb, nmod, d)
    mod_own = jnp.pad(mod_own, ((0, 0), (0, 0), (0, 8 - nmod), (0, 0)))

    vec_d = jnp.stack([pre_mix_g, post_mix_g, pre_ffn_g, post_ffn_g], axis=1)
    vec_c = jnp.stack([conv_b, conv_ln_g, conv_ln_b, pool_scale], axis=1)
    cw_pad = jnp.pad(cw_full, ((0, 0), (0, HALO - taps), (0, 0)))
    fw_rows = jnp.concatenate([fw_full, ffn_conv_b[:, None, :], jnp.zeros((nl, 4, f2), F32)], axis=1)
    pw_b = pool_w.astype(BF16)

    win_g, wout_g, wdn_g = _whole(first_weights)
    saved = []
    xs = x
    for l in range(nl):
        (x1, h1, u1, ac1, dp1, z1, o1), got = _mixer_forward(
            xs, mod_own[l], vec_d[l], vec_c[l], cw_pad[l], pw_b[l], win_g, wout_g, taps, tm_mix, name=f"mixer_fwd_{l}",
            exchange=_gather([wup_b[l]], mid_at=0.9))
        wup_g, = _whole(got)
        (x2, h2, u2, uc2, hid2, o2), nxt = _ffn_forward(
            x1, mod_own[l], vec_d[l], fw_rows[l], wup_g, wdn_g, tm, name=f"ffn_fwd_{l}",
            exchange=_gather(others(l + 1), mid_at=0.6) if l + 1 < nl else None)
        saved.append((xs, h1, u1, ac1, dp1, z1, o1, x1, h2, u2, uc2, hid2, o2, win_g, wout_g, wup_g, wdn_g))
        if l + 1 < nl:
            win_g, wout_g, wdn_g = _whole(nxt)
        xs = x2

    dx, sq = _loss_grad(xs, loss_target, tm, name="loss_grad")
    loss = lax.psum(0.5 * jnp.sum(sq) / d, ("x", "y", "c"))

    flat = lambda a: a.reshape(nb * s, a.shape[-1])
    small = [None] * nl
    big_mlp, big_mix = [None] * nl, [None] * nl
    mlp = mix = None
    for l in reversed(range(nl)):
        x0, h1, u1, ac1, dp1, z1, o1, x1, h2, u2, uc2, hid2, o2, win_g, wout_g, wup_g, wdn_g = saved[l]
        (dx, du2, do2, rowd2, rowb2, dfw), got = _ffn_backward(
            dx, x1, o2, u2, uc2, mod_own[l], vec_d[l], fw_rows[l], wup_g, wdn_g, tm, name=f"ffn_bwd_{l}",
            exchange=_combine([mlp.chips(), mix.swap()]) if mlp else None)
        if mlp:
            mlp.after_chips(got[:2])
            mix.after_swap(got[2:])
        g_up, got = _weight_grad(flat(h2), flat(du2), N_CHIPS, "b", tt, name=f"grad_ffn_up_{l}",
                                 exchange=_combine([mlp.share(), mix.chips()]) if mlp else None)
        if mlp:
            big_mlp[l + 1] = mlp.result(got[:2])
            mix.after_chips(got[2:])
        g_dn, _ = _weight_grad(flat(hid2), flat(do2), 2, "a", tt, name=f"grad_ffn_down_{l}")
        mlp_above, mlp = mlp, _ReduceScatter([g_up, g_dn.reshape(N_CHIPS, -1, d)], cidx, idx, f"mlp_{l}")
        if l == 0:
            mlp.after_swap(_run_exchange(mlp.swap(), name="rs_swap_mlp_0"))
        first = mlp.swap() if l > 0 else mlp.chips()
        (dx, du1, do1, rowd1, rowb1, rowc, dcw, dpw), got = _mixer_backward(
            dx, x0, o1, u1, ac1, dp1, mod_own[l], vec_d[l], vec_c[l], cw_pad[l], pw_b[l], win_g, wout_g, taps, tm_mix,
            name=f"mixer_bwd_{l}", exchange=_combine([first, mix.share()]) if mlp_above else first)
        if l > 0:
            mlp.after_swap(got[:2])
        else:
            mlp.after_chips(got[:2])
        if mlp_above:
            big_mix[l + 1] = mix.result(got[2:])
        g_in, got = _weight_grad(flat(h1), flat(du1), N_CHIPS, "cols", tt, name=f"grad_w_in_{l}",
                                 exchange=mlp.share() if l == 0 else None)
        if l == 0:
            big_mlp[0] = mlp.result(got)
        g_out, _ = _weight_grad(flat(z1), flat(do1), 1, "cols", tt, name=f"grad_w_out_{l}")
        mix = _ReduceScatter([g_in, g_out.reshape(N_CHIPS, -1, d)], cidx, idx, f"mix_{l}")
        small[l] = dict(rowd=rowd1 + rowd2, rowb=rowb1 + rowb2, rowc=rowc, dcw=dcw[:taps], dpw=dpw, dfw=dfw)
    mix.after_swap(_run_exchange(mix.swap(), name="rs_swap_mix_0"))
    mix.after_chips(_run_exchange(mix.chips(), name="rs_chips_mix_0"))
    big_mix[0] = mix.result(_run_exchange(mix.share(), name="rs_share_mix_0"))

    dmod_own = jnp.stack([small[l]["rowb"][:, :nmod, :] for l in range(nl)])
    dmod_all = _gather8(_rows128(dmod_own), name="gather_dmod").reshape(N_DEV, nl, nb, nmod * d)
    dmod_all = dmod_all.transpose(1, 0, 2, 3).reshape(nl, N_DEV * nb, nmod * d)
    dmod_cols = lax.dynamic_slice_in_dim(dmod_all, jm * acol, acol, axis=2)
    (g_ada_w, d_ada_w, nm_ada_w, nv_ada_w), _ = _ada_update(c_all, dmod_cols, ada_w, m_ada_w, v_ada_w, name="ada_update")

    def st(key, row=None):
        return jnp.stack([small[l][key] if row is None else small[l][key][row] for l in range(nl)])

    local = {
        "ada_b": dmod_own.sum(axis=1).reshape(nl, nmod * d),
        "pre_mix_g": st("rowd", 0), "post_mix_g": st("rowd", 1),
        "conv_b": st("rowc", 0), "conv_ln_g": st("rowc", 1), "conv_ln_b": st("rowc", 2),
        "pool_w": st("dpw"), "pool_scale": st("rowc", 3),
        "pre_ffn_g": st("rowd", 2), "post_ffn_g": st("rowd", 3),
        "ffn_conv_b": st("dfw", 3), "conv_w": st("dcw"), "ffn_conv_w": jnp.stack([small[l]["dfw"][:3] for l in range(nl)]),
    }
    names = list(local)
    sizes = [local[k].size for k in names]
    pad = -sum(sizes) % (4 * SUBLANES * LANES)
    packed = jnp.concatenate([local[k].reshape(-1) for k in names] + [jnp.zeros((pad,), F32)])
    summed = _allreduce8(_rows128(packed), name="allreduce_small").reshape(-1)
    grads, off = {}, 0
    for k, sz in zip(names, sizes):
        grads[k] = summed[off:off + sz].reshape(local[k].shape)
        off += sz
    grads["conv_w"] = lax.dynamic_slice_in_dim(grads["conv_w"], jm * ccol, ccol, axis=2)
    grads["ffn_conv_w"] = lax.dynamic_slice_in_dim(grads["ffn_conv_w"], jm * fcol, fcol, axis=2)

    params = dict(ada_b=(ada_b, m_ada_b, v_ada_b), pre_mix_g=(pre_mix_g, m_pre_mix_g, v_pre_mix_g),
                  post_mix_g=(post_mix_g, m_post_mix_g, v_post_mix_g), conv_b=(conv_b, m_conv_b, v_conv_b),
                  conv_ln_g=(conv_ln_g, m_conv_ln_g, v_conv_ln_g), conv_ln_b=(conv_ln_b, m_conv_ln_b, v_conv_ln_b),
                  pool_w=(pool_w, m_pool_w, v_pool_w), pool_scale=(pool_scale, m_pool_scale, v_pool_scale),
                  pre_ffn_g=(pre_ffn_g, m_pre_ffn_g, v_pre_ffn_g), post_ffn_g=(post_ffn_g, m_post_ffn_g, v_post_ffn_g),
                  ffn_conv_b=(ffn_conv_b, m_ffn_conv_b, v_ffn_conv_b), conv_w=(conv_w, m_conv_w, v_conv_w),
                  ffn_conv_w=(ffn_conv_w, m_ffn_conv_w, v_ffn_conv_w))
    pack = lambda i, g=None: _rows128(jnp.concatenate([(grads[k] if g else params[k][i]).reshape(-1) for k in names]))
    sd, sm, sv = _adamw_flat(pack(0), pack(0, True), pack(1), pack(2), name="adamw_small")
    outs = {}
    off = 0
    for k in names:
        shape, sz = params[k][0].shape, params[k][0].size
        outs[k] = (grads[k],) + tuple(a.reshape(-1)[off:off + sz].reshape(shape) for a in (sd, sm, sv))
        off += sz

    outs["ada_w"] = (g_ada_w, d_ada_w, nm_ada_w, nv_ada_w)
    for k, w, m, v, gs in [("w_in", w_in, m_w_in, v_w_in, [big_mix[l][0] for l in range(nl)]),
                           ("w_out", w_out, m_w_out, v_w_out, [big_mix[l][1] for l in range(nl)]),
                           ("ffn_up", ffn_up, m_ffn_up, v_ffn_up, [big_mlp[l][0] for l in range(nl)]),
                           ("ffn_down", ffn_down, m_ffn_down, v_ffn_down, [big_mlp[l][1] for l in range(nl)])]:
        outs[k] = tuple(_adamw_layers(w, m, v, gs, name=f"adamw_{k}"))

    order = ["ada_w", "ada_b", "pre_mix_g", "post_mix_g", "w_in", "conv_w", "conv_b", "conv_ln_g", "conv_ln_b", "pool_w",
             "pool_scale", "w_out", "pre_ffn_g", "post_ffn_g", "ffn_up", "ffn_conv_w", "ffn_conv_b", "ffn_down"]
    return (loss, dx) + tuple(outs[k][i] for i in range(4) for k in order)
```

```python
import functools

import jax
import jax.numpy as jnp
from jax import lax
from jax.experimental import pallas as pl
from jax.experimental.pallas import tpu as pltpu

F32 = jnp.float32
BF16 = jnp.bfloat16
MESH = pl.DeviceIdType.MESH

EPS = 1e-6
POOL_WINDOWS = (2, 4, 8, 16)
ADAM_LR = 0.001
ADAM_B1 = 0.9
ADAM_B2 = 0.999
ADAM_EPS = 1e-08
ADAM_WD = 0.01
ADAM_STEP = 10

N_CHIPS = 4
N_DEV = 8
LANES = 128
SUBLANES = 8
HALO = 32
FHALO = 8
VMEM_LIMIT = 60 * 1024 * 1024


def _pos():
    return lax.axis_index("x"), lax.axis_index("y"), lax.axis_index("c")


def _flip(v, f):
    return 1 - v if f else v


def _full(shape):
    nd = len(shape)
    return pl.BlockSpec(shape, lambda *_: (0,) * nd)


_ANY = pl.BlockSpec(memory_space=pl.ANY)
_VMEM = pl.BlockSpec(memory_space=pltpu.VMEM)


def _sigmoid(v):
    return 1.0 / (1.0 + jnp.exp(-v))


def _colsum(v):
    return jnp.sum(v, axis=0, keepdims=True)


def _gather8(v, name, exchange=None):
    r, ccols = v.shape
    ex = exchange
    nci, nco = (len(ex.ins), len(ex.outs)) if ex else (0, 0)

    def body(*refs):
        v_ref, cin, out_ref, cout = refs[0], refs[1:1 + nci], refs[1 + nci], refs[2 + nci:2 + nci + nco]
        send_sems, recv_sems, local_sem = refs[2 + nci + nco:5 + nci + nco]
        if ex:
            sems = _Sems(*refs[5 + nci + nco:])
            ex.start(cin, cout, sems)
        x, y, c = _pos()
        me = 4 * x + 2 * y + c
        mine = pltpu.make_async_copy(v_ref, out_ref.at[me], local_sem)
        mine.start()
        peers = [(_flip(x, (k >> 2) & 1), _flip(y, (k >> 1) & 1), _flip(c, k & 1)) for k in range(1, N_DEV)]
        sends = []
        for k, peer in enumerate(peers):
            cp = pltpu.make_async_remote_copy(src_ref=v_ref, dst_ref=out_ref.at[me], send_sem=send_sems.at[k],
                                              recv_sem=recv_sems.at[k], device_id=peer, device_id_type=MESH)
            cp.start()
            sends.append(cp)
        for k, peer in enumerate(peers):
            pidx = 4 * peer[0] + 2 * peer[1] + peer[2]
            pltpu.make_async_remote_copy(src_ref=v_ref, dst_ref=out_ref.at[pidx], send_sem=send_sems.at[k],
                                         recv_sem=recv_sems.at[k], device_id=peer, device_id_type=MESH).wait_recv()
        for cp in sends:
            cp.wait_send()
        mine.wait()
        if ex:
            if ex.mid is not None:
                ex.mid(cin, cout, sems)
            ex.finish(cin, cout, sems)

    outs = pl.pallas_call(
        body, name=name, out_shape=[jax.ShapeDtypeStruct((N_DEV, r, ccols), v.dtype)] + (ex.outs if ex else []),
        in_specs=[_VMEM] + [_ANY] * nci, out_specs=[_VMEM] + [_ANY] * nco,
        scratch_shapes=[pltpu.SemaphoreType.DMA((N_DEV - 1,)), pltpu.SemaphoreType.DMA((N_DEV - 1,)),
                        pltpu.SemaphoreType.DMA(())] + (ex.scratch() if ex else []),
        input_output_aliases={1 + a: 1 + b for a, b in ex.aliases.items()} if ex else {},
        compiler_params=pltpu.CompilerParams(vmem_limit_bytes=VMEM_LIMIT),
    )(v, *(ex.ins if ex else []))
    return (outs[0], list(outs[1:])) if ex else outs[0]


def _rows_to_owners(v, name):
    _, r, ccols = v.shape

    def body(v_ref, out_ref, send_sems, recv_sems, local_sem):
        x, y, c = _pos()
        jm = 2 * x + y
        mine = pltpu.make_async_copy(v_ref.at[2 * jm + c], out_ref.at[jm], local_sem)
        mine.start()
        peers, pjs = _chip_peers(x, y, c)
        sends = []
        for k, peer in enumerate(peers):
            cp = pltpu.make_async_remote_copy(src_ref=v_ref.at[2 * pjs[k] + c], dst_ref=out_ref.at[jm],
                                              send_sem=send_sems.at[k], recv_sem=recv_sems.at[k],
                                              device_id=peer, device_id_type=MESH)
            cp.start()
            sends.append(cp)
        for k, peer in enumerate(peers):
            pltpu.make_async_remote_copy(src_ref=v_ref.at[0], dst_ref=out_ref.at[pjs[k]], send_sem=send_sems.at[k],
                                         recv_sem=recv_sems.at[k], device_id=peer, device_id_type=MESH).wait_recv()
        for cp in sends:
            cp.wait_send()
        mine.wait()

    return pl.pallas_call(
        body, name=name, out_shape=jax.ShapeDtypeStruct((N_CHIPS, r, ccols), v.dtype),
        in_specs=[_VMEM], out_specs=_VMEM,
        scratch_shapes=[pltpu.SemaphoreType.DMA((N_CHIPS - 1,)), pltpu.SemaphoreType.DMA((N_CHIPS - 1,)),
                        pltpu.SemaphoreType.DMA(())],
        compiler_params=pltpu.CompilerParams(vmem_limit_bytes=VMEM_LIMIT),
    )(v)


def _allreduce8(v, name):
    r, ccols = v.shape
    h = r // 2
    q = h // 2

    def body(v_ref, out_ref, whole, part, done, send_sems, recv_sems):
        x, y, c = _pos()
        sib = (x, y, 1 - c)
        mine = pl.ds(pl.multiple_of(c * h, SUBLANES), h)
        theirs = pl.ds(pl.multiple_of((1 - c) * h, SUBLANES), h)
        quarters = [pl.ds(pl.multiple_of(c * h + k * q, SUBLANES), q) for k in range(2)]
        along_x, along_y = (1 - x, y, c), (x, 1 - y, c)

        def exchange(pairs):
            cps = [pltpu.make_async_remote_copy(src_ref=src, dst_ref=dst, send_sem=send_sems.at[k], recv_sem=recv_sems.at[k],
                                                device_id=peer, device_id_type=MESH) for src, dst, k, peer in pairs]
            for cp in cps:
                cp.start()
            for cp in cps:
                cp.wait()

        exchange([(v_ref, whole, 0, sib)])
        out_ref[...] = v_ref[...] + whole[...]
        for stage, peers in enumerate(((along_x, along_y), (along_y, along_x))):
            exchange([(out_ref.at[quarters[k]], part.at[2 * stage + k], 1 + 2 * stage + k, peers[k]) for k in range(2)])
            for k in range(2):
                out_ref[quarters[k], :] = out_ref[quarters[k], :] + part[2 * stage + k]
        exchange([(out_ref.at[mine], done, 5, sib)])
        out_ref[theirs, :] = done[...]

    return pl.pallas_call(
        body, name=name, out_shape=jax.ShapeDtypeStruct((r, ccols), v.dtype),
        in_specs=[_VMEM], out_specs=_VMEM,
        scratch_shapes=[pltpu.VMEM((r, ccols), v.dtype), pltpu.VMEM((4, q, ccols), v.dtype), pltpu.VMEM((h, ccols), v.dtype),
                        pltpu.SemaphoreType.DMA((6,)), pltpu.SemaphoreType.DMA((6,))],
        compiler_params=pltpu.CompilerParams(vmem_limit_bytes=VMEM_LIMIT),
    )(v)


def _chip_peers(x, y, c):
    peers = [(_flip(x, (k >> 1) & 1), _flip(y, k & 1), c) for k in range(1, N_CHIPS)]
    return peers, [2 * p[0] + p[1] for p in peers]


class _Exchange:
    def __init__(self, ins, outs, aliases, n_sems, n_local, start, finish, mid=None, mid_at=1.0):
        self.ins, self.outs, self.aliases = list(ins), list(outs), dict(aliases)
        self.n_sems, self.n_local, self.start, self.finish = n_sems, n_local, start, finish
        self.mid, self.mid_at = mid, mid_at

    def scratch(self):
        return [pltpu.SemaphoreType.DMA((self.n_sems,)), pltpu.SemaphoreType.DMA((self.n_sems,)),
                pltpu.SemaphoreType.DMA((max(self.n_local, 1),))]


class _Sems:
    def __init__(self, send, recv, local, base=0, lbase=0):
        self.send, self.recv, self.loc, self.base, self.lbase = send, recv, local, base, lbase

    def shifted(self, by, lby):
        return _Sems(self.send, self.recv, self.loc, self.base + by, self.lbase + lby)

    def local(self, k):
        return self.loc.at[self.lbase + k]


def _remote(src, dst, sems, k, peer):
    return pltpu.make_async_remote_copy(src_ref=src, dst_ref=dst, send_sem=sems.send.at[sems.base + k],
                                        recv_sem=sems.recv.at[sems.base + k], device_id=peer, device_id_type=MESH)


def _combine(exs):
    ins = [a for ex in exs for a in ex.ins]
    outs = [o for ex in exs for o in ex.outs]
    aliases, spans, ni, no, ns, nloc = {}, [], 0, 0, 0, 0
    for ex in exs:
        aliases.update({ni + a: no + b for a, b in ex.aliases.items()})
        spans.append((ni, no, ns, nloc))
        ni, no, ns, nloc = ni + len(ex.ins), no + len(ex.outs), ns + ex.n_sems, nloc + ex.n_local

    def each(which):
        def run(ins_, outs_, sems):
            for ex, (i0, o0, s0, l0) in zip(exs, spans):
                stage = getattr(ex, which)
                if stage is not None:
                    stage(ins_[i0:i0 + len(ex.ins)], outs_[o0:o0 + len(ex.outs)], sems.shifted(s0, l0))
        return run

    mids = [ex.mid_at for ex in exs if ex.mid is not None]
    return _Exchange(ins, outs, aliases, ns, nloc, each("start"), each("finish"),
                     mid=each("mid") if mids else None, mid_at=max(mids) if mids else 1.0)


def _gather(shards, mid_at=1.0):
    n = len(shards)
    per = N_CHIPS - 1
    halves = [s.reshape(2, s.shape[0] // 2, s.shape[1]) for s in shards]

    def copies(ins, outs, sems):
        x, y, c = _pos()
        jm = 2 * x + y
        sib = (x, y, 1 - c)
        peers, pjs = _chip_peers(x, y, c)
        sends, recvs, passes, passed = [], [], [], []
        for a in range(n):
            own = _remote(ins[a], outs[a].at[jm], sems, 2 * n * per + a, sib)
            sends.append(own)
            passed.append(own)
            for k, peer in enumerate(peers):
                landed, theirs = outs[a].at[pjs[k], c], outs[a].at[pjs[k], 1 - c]
                sends.append(_remote(ins[a].at[c], outs[a].at[jm, c], sems, 2 * (a * per + k), peer))
                recvs.append(_remote(landed, landed, sems, 2 * (a * per + k), peer))
                passes.append(_remote(landed, landed, sems, 2 * (a * per + k) + 1, sib))
                passed.append(_remote(theirs, theirs, sems, 2 * (a * per + k) + 1, sib))
        return sends, recvs, passes, passed

    def start(ins, outs, sems):
        for cp in copies(ins, outs, sems)[0]:
            cp.start()

    def mid(ins, outs, sems):
        _, recvs, passes, _ = copies(ins, outs, sems)
        for got, fwd in zip(recvs, passes):
            got.wait_recv()
            fwd.start()

    def finish(ins, outs, sems):
        sends, _, passes, passed = copies(ins, outs, sems)
        for cp in passed:
            cp.wait_recv()
        for cp in sends + passes:
            cp.wait_send()

    outs = [jax.ShapeDtypeStruct((N_CHIPS,) + h.shape, h.dtype) for h in halves]
    return _Exchange(halves, outs, {}, 2 * n * per + n, 0, start, finish, mid=mid, mid_at=mid_at)


def _whole(gathered):
    return [g.reshape(g.shape[0], g.shape[1] * g.shape[2], g.shape[3]) for g in gathered]


def _swap_halves(gs):
    n = len(gs)
    halves = [g.reshape(g.shape[0], 2, g.shape[1] // 2, g.shape[2]) for g in gs]

    def copies(ins, outs, sems):
        x, y, c = _pos()
        sib = (x, y, 1 - c)
        return [_remote(ins[a].at[:, 1 - c], outs[a], sems, a, sib) for a in range(n)]

    def start(ins, outs, sems):
        for cp in copies(ins, outs, sems):
            cp.start()

    def finish(ins, outs, sems):
        for cp in copies(ins, outs, sems):
            cp.wait()

    outs = [jax.ShapeDtypeStruct((g.shape[0], g.shape[1] // 2, g.shape[2]), g.dtype) for g in gs]
    return _Exchange(halves, outs, {}, n, 0, start, finish)


def _chip_exchange(ps):
    n = len(ps)
    per = N_CHIPS - 1

    def copies(ins, outs, sems):
        x, y, c = _pos()
        peers, pjs = _chip_peers(x, y, c)
        return [_remote(ins[a].at[pjs[k]], outs[a].at[k], sems, a * per + k, peer)
                for a in range(n) for k, peer in enumerate(peers)]

    def start(ins, outs, sems):
        for cp in copies(ins, outs, sems):
            cp.start()

    def finish(ins, outs, sems):
        for cp in copies(ins, outs, sems):
            cp.wait()

    outs = [jax.ShapeDtypeStruct((per,) + p.shape[1:], p.dtype) for p in ps]
    return _Exchange(ps, outs, {}, n * per, 0, start, finish)


def _sibling_share(fs):
    n = len(fs)

    def copies(outs, sems):
        x, y, c = _pos()
        sib = (x, y, 1 - c)
        sends = [_remote(outs[a].at[c], outs[a].at[c], sems, a, sib) for a in range(n)]
        recvs = [_remote(outs[a].at[1 - c], outs[a].at[1 - c], sems, a, sib) for a in range(n)]
        return sends, recvs

    def start(ins, outs, sems):
        for cp in copies(outs, sems)[0]:
            cp.start()

    def finish(ins, outs, sems):
        sends, recvs = copies(outs, sems)
        for cp in recvs:
            cp.wait_recv()
        for cp in sends:
            cp.wait_send()

    outs = [jax.ShapeDtypeStruct(f.shape, f.dtype) for f in fs]
    return _Exchange(fs, outs, {a: a for a in range(n)}, n, 0, start, finish)


def _run_exchange(ex, name):
    ni, no = len(ex.ins), len(ex.outs)

    def body(*refs):
        ins, outs, sems = refs[:ni], refs[ni:ni + no], _Sems(*refs[ni + no:])
        ex.start(ins, outs, sems)
        if ex.mid is not None:
            ex.mid(ins, outs, sems)
        ex.finish(ins, outs, sems)

    return pl.pallas_call(
        body, name=name, out_shape=ex.outs, in_specs=[_ANY] * ni, out_specs=[_ANY] * no,
        input_output_aliases=ex.aliases, scratch_shapes=ex.scratch(),
    )(*ex.ins)


def _grid_call(body, name, grid, in_specs, out_specs, out_shape, scratch_shapes, args, exchange=None):
    ni, no = len(in_specs), len(out_specs)
    params = pltpu.CompilerParams(dimension_semantics=("arbitrary",) * len(grid), vmem_limit_bytes=VMEM_LIMIT)
    if exchange is None:
        outs = pl.pallas_call(body, name=name, grid=grid, in_specs=in_specs, out_specs=out_specs, out_shape=out_shape,
                              scratch_shapes=scratch_shapes, compiler_params=params)(*args)
        return list(outs), []
    ex = exchange
    nci, nco = len(ex.ins), len(ex.outs)

    def hosted(*refs):
        cin = refs[ni:ni + nci]
        cout = refs[ni + nci + no:ni + nci + no + nco]
        sems = _Sems(*refs[len(refs) - 3:])
        main = refs[:ni] + refs[ni + nci:ni + nci + no] + refs[ni + nci + no + nco:len(refs) - 3]
        ids = [pl.program_id(a) for a in range(len(grid))]
        first = functools.reduce(lambda p, q: p & q, [i == 0 for i in ids])
        last = functools.reduce(lambda p, q: p & q, [i == g - 1 for i, g in zip(ids, grid)])

        @pl.when(first)
        def _():
            ex.start(cin, cout, sems)

        if ex.mid is not None:
            steps = functools.reduce(lambda p, q: p * q, grid)
            flat = functools.reduce(lambda p, q: p * q[1] + q[0], zip(ids[1:], grid[1:]), ids[0])

            @pl.when(flat == min(steps - 1, int(ex.mid_at * steps)))
            def _():
                ex.mid(cin, cout, sems)

        body(*main)

        @pl.when(last)
        def _():
            ex.finish(cin, cout, sems)

    outs = pl.pallas_call(
        hosted, name=name, grid=grid, in_specs=list(in_specs) + [_ANY] * nci, out_specs=list(out_specs) + [_ANY] * nco,
        out_shape=list(out_shape) + ex.outs, scratch_shapes=list(scratch_shapes) + ex.scratch(),
        input_output_aliases={ni + a: no + b for a, b in ex.aliases.items()}, compiler_params=params,
    )(*args, *ex.ins)
    return list(outs[:no]), list(outs[no:])


def _row_tile(rows, cols, itemsize, budget=2 * 1024 * 1024):
    best = None
    for t in range(16, rows + 1, 16):
        if rows % t == 0 and t * cols * itemsize <= budget:
            best = t
    return best if best is not None else rows


def _pair_sum(g, r1, cidx, name):
    nj, r, ccols = g.shape
    hr = r // 2
    tr = _row_tile(hr, ccols, 4)
    nt = hr // tr

    def body(c_ref, g_ref, r_ref, o_ref):
        o_ref[...] = (g_ref[...].astype(F32) + r_ref[...].astype(F32)).astype(o_ref.dtype)

    return pl.pallas_call(
        body, name=name, out_shape=jax.ShapeDtypeStruct((nj, hr, ccols), g.dtype),
        grid_spec=pltpu.PrefetchScalarGridSpec(
            num_scalar_prefetch=1, grid=(nj, nt),
            in_specs=[pl.BlockSpec((None, tr, ccols), lambda j, i, c_ref: (j, c_ref[0] * nt + i, 0)),
                      pl.BlockSpec((None, tr, ccols), lambda j, i, c_ref: (j, i, 0))],
            out_specs=pl.BlockSpec((None, tr, ccols), lambda j, i, c_ref: (j, i, 0))),
        compiler_params=pltpu.CompilerParams(dimension_semantics=("arbitrary", "arbitrary")),
    )(cidx, g, r1)


def _chip_sum(p, r2, idx, name):
    nj, hr, ccols = p.shape
    tr = _row_tile(hr, ccols, 4)
    nt = hr // tr

    def body(i_ref, p_ref, r_ref, o_ref):
        s = p_ref[...].astype(F32)
        for k in range(N_CHIPS - 1):
            s = s + r_ref[k].astype(F32)
        o_ref[...] = s

    return pl.pallas_call(
        body, name=name, out_shape=jax.ShapeDtypeStruct((2, hr, ccols), F32),
        grid_spec=pltpu.PrefetchScalarGridSpec(
            num_scalar_prefetch=1, grid=(nt,),
            in_specs=[pl.BlockSpec((None, tr, ccols), lambda i, i_ref: (i_ref[0], i, 0)),
                      pl.BlockSpec((N_CHIPS - 1, tr, ccols), lambda i, i_ref: (0, i, 0))],
            out_specs=pl.BlockSpec((None, tr, ccols), lambda i, i_ref: (i_ref[1], i, 0))),
        compiler_params=pltpu.CompilerParams(dimension_semantics=("arbitrary",)),
    )(idx, p, r2)


def _adam_math(w, g, m, v):
    m2 = ADAM_B1 * m + (1.0 - ADAM_B1) * g
    v2 = ADAM_B2 * v + (1.0 - ADAM_B2) * (g * g)
    m_hat = m2 / (1.0 - ADAM_B1 ** ADAM_STEP)
    v_hat = v2 / (1.0 - ADAM_B2 ** ADAM_STEP)
    delta = -ADAM_LR * (m_hat / (jnp.sqrt(v_hat) + ADAM_EPS) + ADAM_WD * w)
    return delta, m2, v2


def _adamw_layers(w, m, v, gs, name):
    nl, r, ccols = w.shape
    ng = len(gs)
    tr = _row_tile(r, ccols, 4, budget=1024 * 1024)
    nt = r // tr

    def body(w_ref, m_ref, v_ref, *rest):
        g_refs, (go_ref, d_ref, mo_ref, vo_ref) = rest[:ng], rest[ng:]
        l = pl.program_id(0)
        g = g_refs[0][...]
        for k in range(1, ng):
            g = jnp.where(l == k, g_refs[k][...], g)
        delta, m2, v2 = _adam_math(w_ref[...], g, m_ref[...], v_ref[...])
        go_ref[...] = g
        d_ref[...] = delta
        mo_ref[...] = m2
        vo_ref[...] = v2

    big = pl.BlockSpec((None, tr, ccols), lambda l, i: (l, i, 0))

    def gspec(k):
        return pl.BlockSpec((tr, ccols), lambda l, i: (jnp.where(l == k, i, jnp.where(l < k, 0, nt - 1)), 0))

    assert ng == nl
    return _grid_call(body, name, (nl, nt), in_specs=[big, big, big] + [gspec(k) for k in range(ng)],
                      out_specs=[big, big, big, big], out_shape=[jax.ShapeDtypeStruct(w.shape, F32)] * 4,
                      scratch_shapes=[], args=(w, m, v, *gs))[0]


def _adamw_flat(w, g, m, v, name):
    r, ccols = w.shape

    def body(w_ref, g_ref, m_ref, v_ref, d_ref, mo_ref, vo_ref):
        delta, m2, v2 = _adam_math(w_ref[...], g_ref[...], m_ref[...], v_ref[...])
        d_ref[...] = delta
        mo_ref[...] = m2
        vo_ref[...] = v2

    return pl.pallas_call(
        body, name=name, out_shape=[jax.ShapeDtypeStruct((r, ccols), F32)] * 3,
        in_specs=[_VMEM] * 4, out_specs=[_VMEM] * 3,
        compiler_params=pltpu.CompilerParams(vmem_limit_bytes=VMEM_LIMIT),
    )(w, g, m, v)


def _ada_forward(c_all, ada_w, ada_b_cols, name):
    nl, d, ncols = ada_w.shape
    bg = c_all.shape[0]
    tn = 512 if ncols % 512 == 0 else ncols

    def body(c_ref, w_ref, b_ref, o_ref):
        cv = c_ref[...]
        ca = (cv * _sigmoid(cv)).astype(BF16)
        o_ref[...] = jnp.dot(ca, w_ref[...].astype(BF16), preferred_element_type=F32) + b_ref[...]

    return pl.pallas_call(
        body, name=name, out_shape=jax.ShapeDtypeStruct((nl, bg, ncols), F32),
        grid=(nl, ncols // tn),
        in_specs=[pl.BlockSpec((bg, d), lambda l, j: (0, 0)),
                  pl.BlockSpec((None, d, tn), lambda l, j: (l, 0, j)),
                  pl.BlockSpec((None, 1, tn), lambda l, j: (l, 0, j))],
        out_specs=pl.BlockSpec((None, bg, tn), lambda l, j: (l, 0, j)),
        compiler_params=pltpu.CompilerParams(dimension_semantics=("arbitrary", "arbitrary")),
    )(c_all, ada_w, ada_b_cols)


def _ada_update(c_all, dmod_cols, w, m, v, name, exchange=None):
    nl, d, ncols = w.shape
    bg = c_all.shape[0]
    tn = 512 if ncols % 512 == 0 else ncols

    def body(c_ref, dm_ref, w_ref, m_ref, v_ref, go_ref, d_ref, mo_ref, vo_ref):
        cv = c_ref[...]
        ca = (cv * _sigmoid(cv)).astype(BF16)
        g = lax.dot_general(ca, dm_ref[...].astype(BF16), (((0,), (0,)), ((), ())), preferred_element_type=F32)
        delta, m2, v2 = _adam_math(w_ref[...], g, m_ref[...], v_ref[...])
        go_ref[...] = g
        d_ref[...] = delta
        mo_ref[...] = m2
        vo_ref[...] = v2

    big = pl.BlockSpec((None, d, tn), lambda l, j: (l, 0, j))
    return _grid_call(
        body, name, (nl, ncols // tn),
        in_specs=[pl.BlockSpec((bg, d), lambda l, j: (0, 0)),
                  pl.BlockSpec((None, bg, tn), lambda l, j: (l, 0, j)), big, big, big],
        out_specs=[big, big, big, big], out_shape=[jax.ShapeDtypeStruct(w.shape, F32)] * 4,
        scratch_shapes=[], args=(c_all, dmod_cols, w, m, v), exchange=exchange)


def _load_weights(first, pairs, sems):
    @pl.when(first)
    def _():
        cps = [pltpu.make_async_copy(src, dst, sems.at[k]) for k, (src, dst) in enumerate(pairs)]
        for cp in cps:
            cp.start()
        for cp in cps:
            cp.wait()


def _ada_norm(xv, g, sc, sh):
    r = lax.rsqrt(jnp.mean(xv * xv, axis=-1, keepdims=True) + EPS)
    xn = xv * r
    return (xn * g) * (1.0 + sc) + sh, xn, r


def _ada_norm_bwd(dh, xn, r, g, sc):
    d_sh = _colsum(dh)
    d_sc = _colsum(dh * (xn * g))
    dxg = dh * (1.0 + sc)
    d_g = _colsum(dxg * xn)
    gd = dxg * g
    dx = r * (gd - xn * jnp.mean(gd * xn, axis=-1, keepdims=True))
    return dx, d_sh, d_sc, d_g


def _gated_residual_bwd(dxo, o, g_post, gt):
    r = lax.rsqrt(jnp.mean(o * o, axis=-1, keepdims=True) + EPS)
    on = o * r
    d_gt = _colsum(dxo * (on * g_post))
    dy = dxo * (1.0 + gt)
    d_gp = _colsum(dy * on)
    gd = dy * g_post
    do = r * (gd - on * jnp.mean(gd * on, axis=-1, keepdims=True))
    return do, d_gt, d_gp


def _seq_positions(i, tm, width):
    return i * tm + lax.broadcasted_iota(jnp.int32, (tm, width), 0)


def _fill_phases(ext, phases):
    rows = ext.shape[0]
    ev = ext[...]
    for r in range(1, SUBLANES):
        phases[r - 1] = pltpu.roll(ev, rows - r, axis=0)


def _shifted_rows(ext, phases, offset, n):
    q, r = divmod(offset, SUBLANES)
    if r == 0:
        return ext[pl.ds(q * SUBLANES, n), :]
    return phases[r - 1, pl.ds(q * SUBLANES, n), :]


def _rows_before(halo, cur, shift):
    e = jnp.concatenate([halo, cur], axis=0)
    return pltpu.roll(e, shift, axis=0)[halo.shape[0]:, :]


def _rows_after(cur, halo, shift):
    e = jnp.concatenate([cur, halo], axis=0)
    return pltpu.roll(e, e.shape[0] - shift, axis=0)[:cur.shape[0], :]


def _mixer_forward(x, mod, vec_d, vec_c, cw, pw, win_g, wout_g, taps, tm, name, exchange=None):
    nb, s, d = x.shape
    n = s // tm
    nj, _, dcol = win_g.shape
    din = nj * dcol
    dc = vec_c.shape[-1]
    dpool = din - 2 * dc
    dmix = dc + dpool
    ro = wout_g.shape[1]
    ngrp = dpool // LANES

    def body(x_ref, mod_ref, vd_ref, vc_ref, cw_ref, pw_ref, win_hbm, wout_hbm,
             xo_ref, h_ref, u_ref, ac_ref, dp_ref, z_ref, o_ref,
             win_v, wout_v, ext_a, ext_p, phases, sems):
        b, i = pl.program_id(0), pl.program_id(1)
        pairs = [(win_hbm.at[j], win_v.at[:, pl.ds(j * dcol, dcol)]) for j in range(nj)]
        pairs += [(wout_hbm.at[j], wout_v.at[pl.ds(j * ro, ro), :]) for j in range(nj)]
        _load_weights((b == 0) & (i == 0), pairs, sems)

        xv = x_ref[...]
        h, _, _ = _ada_norm(xv, vd_ref[0:1, :], mod_ref[1:2, :], mod_ref[0:1, :])
        hb = h.astype(BF16)
        h_ref[...] = hb
        u = jnp.dot(hb, win_v[...], preferred_element_type=F32)
        u_ref[...] = u.astype(BF16)
        ag = u[:, :dc] * _sigmoid(u[:, dc:2 * dc])
        up = u[:, 2 * dc:]

        @pl.when(i == 0)
        def _():
            ext_a[0:HALO, :] = jnp.zeros((HALO, dc), F32)
            ext_p[0:HALO, :] = jnp.zeros((HALO, dpool), F32)

        @pl.when(i > 0)
        def _():
            ext_a[0:HALO, :] = ext_a[tm:tm + HALO, :]
            ext_p[0:HALO, :] = ext_p[tm:tm + HALO, :]

        ext_a[HALO:HALO + tm, :] = ag
        ext_p[HALO:HALO + tm, :] = up

        acc = jnp.broadcast_to(vc_ref[0:1, :], (tm, dc))
        _fill_phases(ext_a, phases)
        for k in range(taps):
            acc = acc + cw_ref[k:k + 1, :] * _shifted_rows(ext_a, phases, HALO - (taps - 1) + k, tm)
        ac_ref[...] = acc.astype(BF16)
        mu = jnp.mean(acc, axis=-1, keepdims=True)
        xc = acc - mu
        var = jnp.mean(xc * xc, axis=-1, keepdims=True)
        al = (xc * lax.rsqrt(var + EPS)) * vc_ref[1:2, :] + vc_ref[2:3, :]
        a = al * _sigmoid(al)

        pos = _seq_positions(i, tm, LANES)
        parts = [a.astype(BF16)]
        for g in range(ngrp):
            w = POOL_WINDOWS[g]
            cols = slice(g * LANES, (g + 1) * LANES)
            sw = ext_p[:, cols]
            step = 1
            while step < w:
                sw = sw + pltpu.roll(sw, step, axis=0)
                step *= 2
            cnt = jnp.minimum(pos + 1, w).astype(F32)
            dg = (sw[HALO:, :] / cnt - up[:, cols]).astype(BF16)
            dp_ref[:, cols] = dg
            q = jnp.dot(dg, pw_ref[g], preferred_element_type=F32)
            parts.append((q * vc_ref[3:4, cols]).astype(BF16))
        z = jnp.concatenate(parts, axis=-1)
        z_ref[...] = z
        o = jnp.dot(z, wout_v[...], preferred_element_type=F32)
        o_ref[...] = o
        r2 = lax.rsqrt(jnp.mean(o * o, axis=-1, keepdims=True) + EPS)
        xo_ref[...] = xv + (1.0 + mod_ref[2:3, :]) * ((o * r2) * vd_ref[1:2, :])

    def tile(width):
        return pl.BlockSpec((None, tm, width), lambda b, i: (b, i, 0))

    return _grid_call(
        body, name, (nb, n),
        in_specs=[tile(d), pl.BlockSpec((None, 8, d), lambda b, i: (b, 0, 0)), _full(vec_d.shape), _full(vec_c.shape),
                  _full(cw.shape), _full(pw.shape), _ANY, _ANY],
        out_specs=[tile(d), tile(d), tile(din), tile(dc), tile(dpool), tile(dmix), tile(d)],
        out_shape=[jax.ShapeDtypeStruct((nb, s, d), F32), jax.ShapeDtypeStruct((nb, s, d), BF16),
                   jax.ShapeDtypeStruct((nb, s, din), BF16), jax.ShapeDtypeStruct((nb, s, dc), BF16),
                   jax.ShapeDtypeStruct((nb, s, dpool), BF16), jax.ShapeDtypeStruct((nb, s, dmix), BF16),
                   jax.ShapeDtypeStruct((nb, s, d), F32)],
        scratch_shapes=[pltpu.VMEM((d, din), BF16), pltpu.VMEM((dmix, d), BF16),
                        pltpu.VMEM((HALO + tm, dc), F32), pltpu.VMEM((HALO + tm, dpool), F32),
                        pltpu.VMEM((SUBLANES - 1, HALO + tm, dc), F32), pltpu.SemaphoreType.DMA((2 * nj,))],
        args=(x, mod, vec_d, vec_c, cw, pw, win_g, wout_g), exchange=exchange)


def _mixer_backward(dxo, x, o, u, ac, dpl, mod, vec_d, vec_c, cw, pw, win_g, wout_g, taps, tm, name, exchange=None):
    nb, s, d = x.shape
    n = s // tm
    nj, _, dcol = win_g.shape
    din = nj * dcol
    dc = vec_c.shape[-1]
    dpool = din - 2 * dc
    dmix = dc + dpool
    ro = wout_g.shape[1]
    ngrp = dpool // LANES
    rext = tm + HALO

    def body(dxo_ref, x_ref, o_ref, u_ref, ac_ref, dp_ref, mod_ref, vd_ref, vc_ref, cw_ref, pw_ref, win_hbm, wout_hbm,
             dx_ref, du_ref, dob_ref, rowd_ref, rowb_ref, rowc_ref, dcw_ref, dpw_ref,
             win_v, wout_v, ext_a, ext_p, phases, sems):
        b, i = pl.program_id(0), pl.program_id(1)
        first = (b == 0) & (i == 0)
        pairs = [(win_hbm.at[j], win_v.at[:, pl.ds(j * dcol, dcol)]) for j in range(nj)]
        pairs += [(wout_hbm.at[j], wout_v.at[pl.ds(j * ro, ro), :]) for j in range(nj)]
        _load_weights(first, pairs, sems)

        @pl.when(first)
        def _():
            rowd_ref[...] = jnp.zeros_like(rowd_ref)
            rowc_ref[...] = jnp.zeros_like(rowc_ref)
            dcw_ref[...] = jnp.zeros_like(dcw_ref)
            dpw_ref[...] = jnp.zeros_like(dpw_ref)

        @pl.when(i == 0)
        def _():
            rowb_ref[...] = jnp.zeros_like(rowb_ref)
            ext_a[tm:rext, :] = jnp.zeros((HALO, dc), F32)
            ext_p[tm:rext, :] = jnp.zeros((HALO, dpool), F32)

        @pl.when(i > 0)
        def _():
            ext_a[tm:rext, :] = ext_a[0:HALO, :]
            ext_p[tm:rext, :] = ext_p[0:HALO, :]

        g_pre, g_post = vd_ref[0:1, :], vd_ref[1:2, :]
        sh, sc, gt = mod_ref[0:1, :], mod_ref[1:2, :], mod_ref[2:3, :]
        do, d_gt, d_gp = _gated_residual_bwd(dxo_ref[...], o_ref[...], g_post, gt)
        dob = do.astype(BF16)
        dob_ref[...] = dob
        dz = lax.dot_general(dob, wout_v[...], (((1,), (1,)), ((), ())), preferred_element_type=F32)

        acv = ac_ref[...].astype(F32)
        mu = jnp.mean(acv, axis=-1, keepdims=True)
        xc = acv - mu
        rstd = lax.rsqrt(jnp.mean(xc * xc, axis=-1, keepdims=True) + EPS)
        an = xc * rstd
        lg = vc_ref[1:2, :]
        al = an * lg + vc_ref[2:3, :]
        sg = _sigmoid(al)
        dal = dz[:, :dc] * (sg * (1.0 + al * (1.0 - sg)))
        d_lg = _colsum(dal * an)
        d_lb = _colsum(dal)
        dan = dal * lg
        dac = rstd * (dan - jnp.mean(dan, axis=-1, keepdims=True) - an * jnp.mean(dan * an, axis=-1, keepdims=True))
        d_cb = _colsum(dac)
        ext_a[0:tm, :] = dac
        uv = u_ref[:, 0:dc].astype(F32)
        sgu = _sigmoid(u_ref[:, dc:2 * dc].astype(F32))
        ag = uv * sgu
        dag = jnp.zeros((tm, dc), F32)
        _fill_phases(ext_a, phases)
        for k in range(taps):
            sl = _shifted_rows(ext_a, phases, taps - 1 - k, tm)
            dag = dag + cw_ref[k:k + 1, :] * sl
            dcw_ref[k:k + 1, :] += _colsum(ag * sl)
        du_ref[:, 0:dc] = (dag * sgu).astype(BF16)
        du_ref[:, dc:2 * dc] = (dag * uv * (sgu * (1.0 - sgu))).astype(BF16)

        pos = _seq_positions(n - 1 - i, tm, LANES)
        d_ps = []
        for g in range(ngrp):
            w = POOL_WINDOWS[g]
            cols = slice(g * LANES, (g + 1) * LANES)
            gcols = slice(dc + g * LANES, dc + (g + 1) * LANES)
            dgb = dp_ref[:, cols]
            q = jnp.dot(dgb, pw_ref[g], preferred_element_type=F32)
            dpg = dz[:, gcols]
            d_ps.append(_colsum(dpg * q))
            dq = (dpg * vc_ref[3:4, cols]).astype(BF16)
            dpw_ref[g] += lax.dot_general(dgb, dq, (((0,), (0,)), ((), ())), preferred_element_type=F32)
            dd = lax.dot_general(dq, pw_ref[g], (((1,), (1,)), ((), ())), preferred_element_type=F32)
            cnt = jnp.minimum(pos + 1, w).astype(F32)
            ext_p[0:tm, cols] = dd / cnt
            sw = ext_p[:, cols]
            step = 1
            while step < w:
                sw = sw + pltpu.roll(sw, rext - step, axis=0)
                step *= 2
            du_ref[:, 2 * dc + g * LANES:2 * dc + (g + 1) * LANES] = (sw[0:tm, :] - dd).astype(BF16)
        rowc_ref[0:1, :] += d_cb
        rowc_ref[1:2, :] += d_lg
        rowc_ref[2:3, :] += d_lb
        rowc_ref[3:4, :] += jnp.concatenate(d_ps, axis=-1)

        dh = lax.dot_general(du_ref[...], win_v[...], (((1,), (1,)), ((), ())), preferred_element_type=F32)
        _, xn, r1 = _ada_norm(x_ref[...], g_pre, sc, sh)
        dxb, d_sh, d_sc, d_g = _ada_norm_bwd(dh, xn, r1, g_pre, sc)
        dx_ref[...] = dxo_ref[...] + dxb
        rowd_ref[0:1, :] += d_g
        rowd_ref[1:2, :] += d_gp
        rowb_ref[0:1, :] += d_sh
        rowb_ref[1:2, :] += d_sc
        rowb_ref[2:3, :] += d_gt

    def tile(width):
        return pl.BlockSpec((None, tm, width), lambda b, i: (b, n - 1 - i, 0))

    return _grid_call(
        body, name, (nb, n),
        in_specs=[tile(d), tile(d), tile(d), tile(din), tile(dc), tile(dpool),
                  pl.BlockSpec((None, 8, d), lambda b, i: (b, 0, 0)), _full(vec_d.shape), _full(vec_c.shape),
                  _full(cw.shape), _full(pw.shape), _ANY, _ANY],
        out_specs=[tile(d), tile(din), tile(d), _full((8, d)), pl.BlockSpec((None, 8, d), lambda b, i: (b, 0, 0)),
                   _full((8, dc)), _full((HALO, dc)), _full(pw.shape)],
        out_shape=[jax.ShapeDtypeStruct((nb, s, d), F32), jax.ShapeDtypeStruct((nb, s, din), BF16),
                   jax.ShapeDtypeStruct((nb, s, d), BF16), jax.ShapeDtypeStruct((8, d), F32),
                   jax.ShapeDtypeStruct((nb, 8, d), F32), jax.ShapeDtypeStruct((8, dc), F32),
                   jax.ShapeDtypeStruct((HALO, dc), F32), jax.ShapeDtypeStruct(pw.shape, F32)],
        scratch_shapes=[pltpu.VMEM((d, din), BF16), pltpu.VMEM((dmix, d), BF16),
                        pltpu.VMEM((rext, dc), F32), pltpu.VMEM((rext, dpool), F32),
                        pltpu.VMEM((SUBLANES - 1, rext, dc), F32), pltpu.SemaphoreType.DMA((2 * nj,))],
        args=(dxo, x, o, u, ac, dpl, mod, vec_d, vec_c, cw, pw, win_g, wout_g), exchange=exchange)


def _ffn_forward(x, mod, vec_d, fw, wup_g, wdn_g, tm, name, exchange=None):
    nb, s, d = x.shape
    n = s // tm
    nj, _, ucol = wup_g.shape
    f2 = nj * ucol
    dff = f2 // 2
    rd = wdn_g.shape[1]
    nq = nj // 2
    cs = dff // nq

    def body(x_ref, mod_ref, vd_ref, fw_ref, wup_hbm, wdn_hbm,
             xo_ref, h_ref, u_ref, uc_ref, hid_ref, o_ref,
             wup_v, wdn_v, prev_u, sems):
        b, i = pl.program_id(0), pl.program_id(1)
        pairs = [(wup_hbm.at[j], wup_v.at[:, pl.ds(j * ucol, ucol)]) for j in range(nj)]
        pairs += [(wdn_hbm.at[j], wdn_v.at[pl.ds(j * rd, rd), :]) for j in range(nj)]
        _load_weights((b == 0) & (i == 0), pairs, sems)

        @pl.when(i == 0)
        def _():
            prev_u[...] = jnp.zeros_like(prev_u)

        xv = x_ref[...]
        h, _, _ = _ada_norm(xv, vd_ref[2:3, :], mod_ref[4:5, :], mod_ref[3:4, :])
        hb = h.astype(BF16)
        h_ref[...] = hb

        def conv(cols):
            uc = jnp.dot(hb, wup_v[:, cols], preferred_element_type=F32)
            u_ref[:, cols] = uc.astype(BF16)
            before = prev_u[:, cols]
            prev_u[:, cols] = uc[tm - FHALO:, :]
            out = (fw_ref[3:4, cols] + fw_ref[2:3, cols] * uc + fw_ref[1:2, cols] * _rows_before(before, uc, 1)
                   + fw_ref[0:1, cols] * _rows_before(before, uc, 2))
            uc_ref[:, cols] = out.astype(BF16)
            return out

        o = jnp.zeros((tm, d), F32)
        for q in range(nq):
            val = conv(pl.ds(q * cs, cs))
            gate = conv(pl.ds(dff + q * cs, cs))
            hid = ((gate * _sigmoid(gate)) * val).astype(BF16)
            hid_ref[:, pl.ds(q * cs, cs)] = hid
            o = o + jnp.dot(hid, wdn_v[pl.ds(q * cs, cs), :], preferred_element_type=F32)
        o_ref[...] = o
        r2 = lax.rsqrt(jnp.mean(o * o, axis=-1, keepdims=True) + EPS)
        xo_ref[...] = xv + (1.0 + mod_ref[5:6, :]) * ((o * r2) * vd_ref[3:4, :])

    def tile(width):
        return pl.BlockSpec((None, tm, width), lambda b, i: (b, i, 0))

    return _grid_call(
        body, name, (nb, n),
        in_specs=[tile(d), pl.BlockSpec((None, 8, d), lambda b, i: (b, 0, 0)), _full(vec_d.shape), _full(fw.shape),
                  _ANY, _ANY],
        out_specs=[tile(d), tile(d), tile(f2), tile(f2), tile(dff), tile(d)],
        out_shape=[jax.ShapeDtypeStruct((nb, s, d), F32), jax.ShapeDtypeStruct((nb, s, d), BF16),
                   jax.ShapeDtypeStruct((nb, s, f2), BF16), jax.ShapeDtypeStruct((nb, s, f2), BF16),
                   jax.ShapeDtypeStruct((nb, s, dff), BF16), jax.ShapeDtypeStruct((nb, s, d), F32)],
        scratch_shapes=[pltpu.VMEM((d, f2), BF16), pltpu.VMEM((dff, d), BF16),
                        pltpu.VMEM((FHALO, f2), F32), pltpu.SemaphoreType.DMA((2 * nj,))],
        args=(x, mod, vec_d, fw, wup_g, wdn_g), exchange=exchange)


def _ffn_backward(dxo, x, o, u, uc, mod, vec_d, fw, wup_g, wdn_g, tm, name, exchange=None):
    nb, s, d = x.shape
    n = s // tm
    nj, _, ucol = wup_g.shape
    f2 = nj * ucol
    dff = f2 // 2
    rd = wdn_g.shape[1]
    nq = nj // 2
    cs = dff // nq

    def body(dxo_ref, x_ref, o_ref, u_ref, uc_ref, mod_ref, vd_ref, fw_ref, wup_hbm, wdn_hbm,
             dx_ref, du_ref, dob_ref, rowd_ref, rowb_ref, dfw_ref,
             wup_v, wdn_v, next_d, sems):
        b, i = pl.program_id(0), pl.program_id(1)
        first = (b == 0) & (i == 0)
        pairs = [(wup_hbm.at[j], wup_v.at[:, pl.ds(j * ucol, ucol)]) for j in range(nj)]
        pairs += [(wdn_hbm.at[j], wdn_v.at[pl.ds(j * rd, rd), :]) for j in range(nj)]
        _load_weights(first, pairs, sems)

        @pl.when(first)
        def _():
            rowd_ref[...] = jnp.zeros_like(rowd_ref)
            dfw_ref[...] = jnp.zeros_like(dfw_ref)

        @pl.when(i == 0)
        def _():
            rowb_ref[...] = jnp.zeros_like(rowb_ref)
            next_d[...] = jnp.zeros_like(next_d)

        g_pre, g_post = vd_ref[2:3, :], vd_ref[3:4, :]
        sh, sc, gt = mod_ref[3:4, :], mod_ref[4:5, :], mod_ref[5:6, :]
        do, d_gt, d_gp = _gated_residual_bwd(dxo_ref[...], o_ref[...], g_post, gt)
        dob = do.astype(BF16)
        dob_ref[...] = dob

        def conv_bwd(cols, duc):
            uc = u_ref[:, cols].astype(F32)
            after = next_d[:, cols]
            next_d[:, cols] = duc[0:FHALO, :]
            d1 = _rows_after(duc, after, 1)
            d2 = _rows_after(duc, after, 2)
            dfw_ref[3:4, cols] += _colsum(duc)
            dfw_ref[2:3, cols] += _colsum(uc * duc)
            dfw_ref[1:2, cols] += _colsum(uc * d1)
            dfw_ref[0:1, cols] += _colsum(uc * d2)
            ob = (fw_ref[2:3, cols] * duc + fw_ref[1:2, cols] * d1 + fw_ref[0:1, cols] * d2).astype(BF16)
            du_ref[:, cols] = ob
            return lax.dot_general(ob, wup_v[:, cols], (((1,), (1,)), ((), ())), preferred_element_type=F32)

        dh = jnp.zeros((tm, d), F32)
        for q in range(nq):
            vcols = pl.ds(q * cs, cs)
            gcols = pl.ds(dff + q * cs, cs)
            dhid = lax.dot_general(dob, wdn_v[vcols, :], (((1,), (1,)), ((), ())), preferred_element_type=F32)
            val = uc_ref[:, vcols].astype(F32)
            gate = uc_ref[:, gcols].astype(F32)
            sg = _sigmoid(gate)
            act = gate * sg
            dval = dhid * act
            dgate = (dhid * val) * (sg + act * (1.0 - sg))
            dh = dh + conv_bwd(vcols, dval)
            dh = dh + conv_bwd(gcols, dgate)

        _, xn, r1 = _ada_norm(x_ref[...], g_pre, sc, sh)
        dxb, d_sh, d_sc, d_g = _ada_norm_bwd(dh, xn, r1, g_pre, sc)
        dx_ref[...] = dxo_ref[...] + dxb
        rowd_ref[2:3, :] += d_g
        rowd_ref[3:4, :] += d_gp
        rowb_ref[3:4, :] += d_sh
        rowb_ref[4:5, :] += d_sc
        rowb_ref[5:6, :] += d_gt

    def tile(width):
        return pl.BlockSpec((None, tm, width), lambda b, i: (b, n - 1 - i, 0))

    return _grid_call(
        body, name, (nb, n),
        in_specs=[tile(d), tile(d), tile(d), tile(f2), tile(f2), pl.BlockSpec((None, 8, d), lambda b, i: (b, 0, 0)),
                  _full(vec_d.shape), _full(fw.shape), _ANY, _ANY],
        out_specs=[tile(d), tile(f2), tile(d), _full((8, d)), pl.BlockSpec((None, 8, d), lambda b, i: (b, 0, 0)),
                   _full(fw.shape)],
        out_shape=[jax.ShapeDtypeStruct((nb, s, d), F32), jax.ShapeDtypeStruct((nb, s, f2), BF16),
                   jax.ShapeDtypeStruct((nb, s, d), BF16), jax.ShapeDtypeStruct((8, d), F32),
                   jax.ShapeDtypeStruct((nb, 8, d), F32), jax.ShapeDtypeStruct(fw.shape, F32)],
        scratch_shapes=[pltpu.VMEM((d, f2), BF16), pltpu.VMEM((dff, d), BF16),
                        pltpu.VMEM((FHALO, f2), F32), pltpu.SemaphoreType.DMA((2 * nj,))],
        args=(dxo, x, o, u, uc, mod, vec_d, fw, wup_g, wdn_g), exchange=exchange)


def _weight_grad(a, b, nblk, split, tt, name, exchange=None):
    t, ka = a.shape
    nb_ = b.shape[1]
    nk = t // tt
    if split == "cols":
        wa, wb, grid = ka, nb_ // nblk, (1, nk)
        a_spec = pl.BlockSpec((tt, ka), lambda j, k: (k, 0))
        b_spec = pl.BlockSpec((tt, nb_), lambda j, k: (k, 0))
        o_spec = pl.BlockSpec((nblk, wa, wb), lambda j, k: (0, 0, 0))
        acc_shape = (ka, nb_)
    elif split == "b":
        wa, wb, grid = ka, nb_ // nblk, (nblk, nk)
        a_spec = pl.BlockSpec((tt, wa), lambda j, k: (k, 0))
        b_spec = pl.BlockSpec((tt, wb), lambda j, k: (k, j))
        o_spec = pl.BlockSpec((None, wa, wb), lambda j, k: (j, 0, 0))
        acc_shape = (wa, wb)
    else:
        wa, wb, grid = ka // nblk, nb_, (nblk, nk)
        a_spec = pl.BlockSpec((tt, wa), lambda j, k: (k, j))
        b_spec = pl.BlockSpec((tt, wb), lambda j, k: (k, 0))
        o_spec = pl.BlockSpec((None, wa, wb), lambda j, k: (j, 0, 0))
        acc_shape = (wa, wb)

    def body(a_ref, b_ref, o_ref, acc):
        k = pl.program_id(1)
        prod = lax.dot_general(a_ref[...], b_ref[...], (((0,), (0,)), ((), ())), preferred_element_type=F32)

        @pl.when(k == 0)
        def _():
            acc[...] = prod

        @pl.when(k > 0)
        def _():
            acc[...] += prod

        @pl.when(k == nk - 1)
        def _():
            if split == "cols":
                for j in range(nblk):
                    o_ref[j] = acc[:, j * wb:(j + 1) * wb].astype(o_ref.dtype)
            else:
                o_ref[...] = acc[...].astype(o_ref.dtype)

    outs, exo = _grid_call(body, name, grid, in_specs=[a_spec, b_spec], out_specs=[o_spec],
                           out_shape=[jax.ShapeDtypeStruct((nblk, wa, wb), BF16)],
                           scratch_shapes=[pltpu.VMEM(acc_shape, F32)], args=(a, b), exchange=exchange)
    return outs[0], exo


def _loss_grad(y, tgt, tm, name):
    nb, s, d = y.shape
    n = s // tm

    def body(y_ref, t_ref, dy_ref, sq_ref):
        @pl.when((pl.program_id(0) == 0) & (pl.program_id(1) == 0))
        def _():
            sq_ref[...] = jnp.zeros_like(sq_ref)

        e = y_ref[...] - t_ref[...]
        dy_ref[...] = e * (1.0 / d)
        sq_ref[0:1, :] += _colsum(e * e)

    tile = pl.BlockSpec((None, tm, d), lambda b, i: (b, i, 0))
    return pl.pallas_call(
        body, name=name, out_shape=[jax.ShapeDtypeStruct((nb, s, d), F32), jax.ShapeDtypeStruct((8, d), F32)],
        grid=(nb, n), in_specs=[tile, tile], out_specs=[tile, _full((8, d))],
        compiler_params=pltpu.CompilerParams(dimension_semantics=("arbitrary", "arbitrary")),
    )(y, tgt)


def _rows128(a):
    return a.reshape(-1, LANES)


class _ReduceScatter:
    def __init__(self, gs, cidx, idx, tag):
        self.gs, self.cidx, self.idx, self.tag = gs, cidx, idx, tag

    def swap(self):
        return _swap_halves(self.gs)

    def after_swap(self, r1):
        self.ps = [_pair_sum(g, r, self.cidx, name=f"rs_pair_{self.tag}_{a}") for a, (g, r) in enumerate(zip(self.gs, r1))]

    def chips(self):
        return _chip_exchange(self.ps)

    def after_chips(self, r2):
        self.fh = [_chip_sum(p, r, self.idx, name=f"rs_sum_{self.tag}_{a}") for a, (p, r) in enumerate(zip(self.ps, r2))]

    def share(self):
        return _sibling_share(self.fh)

    @staticmethod
    def result(fs):
        return [f.reshape(f.shape[0] * f.shape[1], f.shape[2]) for f in fs]


def kernel(x, c, ada_w, ada_b, pre_mix_g, post_mix_g, w_in, conv_w, conv_b, conv_ln_g, conv_ln_b, pool_w, pool_scale, w_out, pre_ffn_g, post_ffn_g, ffn_up, ffn_conv_w, ffn_conv_b, ffn_down, loss_target, m_ada_w, m_ada_b, m_pre_mix_g, m_post_mix_g, m_w_in, m_conv_w, m_conv_b, m_conv_ln_g, m_conv_ln_b, m_pool_w, m_pool_scale, m_w_out, m_pre_ffn_g, m_post_ffn_g, m_ffn_up, m_ffn_conv_w, m_ffn_conv_b, m_ffn_down, v_ada_w, v_ada_b, v_pre_mix_g, v_post_mix_g, v_w_in, v_conv_w, v_conv_b, v_conv_ln_g, v_conv_ln_b, v_pool_w, v_pool_scale, v_w_out, v_pre_ffn_g, v_post_ffn_g, v_ffn_up, v_ffn_conv_w, v_ffn_conv_b, v_ffn_down):
    nb, s, d = x.shape
    nl = w_in.shape[0]
    taps = conv_w.shape[1]
    ccol = conv_w.shape[2]
    dc = conv_b.shape[1]
    fcol = ffn_conv_w.shape[2]
    f2 = ffn_conv_b.shape[1]
    nmod = ada_b.shape[1] // d
    acol = ada_w.shape[2]
    tm = min(256, s)
    tm_mix = min(512, s)
    tt = min(2048, (nb * s) // 2)

    xi, yi, ci = _pos()
    jm = 2 * xi + yi
    cidx = jnp.reshape(ci, (1,)).astype(jnp.int32)
    idx = jnp.stack([jm, ci]).astype(jnp.int32)

    win_b, wout_b, wup_b, wdn_b = (w.astype(BF16) for w in (w_in, w_out, ffn_up, ffn_down))

    def others(l):
        return [win_b[l], wout_b[l], wdn_b[l]]

    n_cw, n_fw, n_c = nl * taps * ccol, nl * 3 * fcol, nb * d
    packed = jnp.concatenate([conv_w.reshape(-1), ffn_conv_w.reshape(-1), c.reshape(-1)])
    got, first_weights = _gather8(_rows128(packed), name="gather_small", exchange=_gather(others(0)))
    got = got.reshape(N_DEV, -1)
    chips = got[0::2]
    cw_full = chips[:, :n_cw].reshape(N_CHIPS, nl, taps, ccol).transpose(1, 2, 0, 3).reshape(nl, taps, dc)
    fw_full = chips[:, n_cw:n_cw + n_fw].reshape(N_CHIPS, nl, 3, fcol).transpose(1, 2, 0, 3).reshape(nl, 3, f2)
    c_all = got[:, n_cw + n_fw:].reshape(N_DEV * nb, d)

    ada_b_cols = lax.dynamic_slice_in_dim(ada_b, jm * acol, acol, axis=1).reshape(nl, 1, acol)
    mod_cols = _ada_forward(c_all, ada_w, ada_b_cols, name="ada_forward")
    by_owner = mod_cols.reshape(nl, N_DEV, nb, acol).transpose(1, 0, 2, 3).reshape(N_DEV, -1, LANES)
    mod_own = _rows_to_owners(by_owner, name="mod_to_owners").reshape(N_CHIPS, nl, nb, acol)
    mod_own = mod_own.transpose(1, 2, 0, 3).reshape(nl, nb, nmod, d)
    mod_own = jnp.pad(mod_own, ((0, 0), (0, 0), (0, 8 - nmod), (0, 0)))

    vec_d = jnp.stack([pre_mix_g, post_mix_g, pre_ffn_g, post_ffn_g], axis=1)
    vec_c = jnp.stack([conv_b, conv_ln_g, conv_ln_b, pool_scale], axis=1)
    cw_pad = jnp.pad(cw_full, ((0, 0), (0, HALO - taps), (0, 0)))
    fw_rows = jnp.concatenate([fw_full, ffn_conv_b[:, None, :], jnp.zeros((nl, 4, f2), F32)], axis=1)
    pw_b = pool_w.astype(BF16)

    win_g, wout_g, wdn_g = _whole(first_weights)
    saved = []
    xs = x
    for l in range(nl):
        (x1, h1, u1, ac1, dp1, z1, o1), got = _mixer_forward(
            xs, mod_own[l], vec_d[l], vec_c[l], cw_pad[l], pw_b[l], win_g, wout_g, taps, tm_mix, name=f"mixer_fwd_{l}",
            exchange=_gather([wup_b[l]], mid_at=0.9))
        wup_g, = _whole(got)
        (x2, h2, u2, uc2, hid2, o2), nxt = _ffn_forward(
            x1, mod_own[l], vec_d[l], fw_rows[l], wup_g, wdn_g, tm, name=f"ffn_fwd_{l}",
            exchange=_gather(others(l + 1), mid_at=0.6) if l + 1 < nl else None)
        saved.append((xs, h1, u1, ac1, dp1, z1, o1, x1, h2, u2, uc2, hid2, o2, win_g, wout_g, wup_g, wdn_g))
        if l + 1 < nl:
            win_g, wout_g, wdn_g = _whole(nxt)
        xs = x2

    dx, sq = _loss_grad(xs, loss_target, tm, name="loss_grad")
    loss = lax.psum(0.5 * jnp.sum(sq) / d, ("x", "y", "c"))

    flat = lambda a: a.reshape(nb * s, a.shape[-1])
    small = [None] * nl
    big_mlp, big_mix = [None] * nl, [None] * nl
    mlp = mix = None
    for l in reversed(range(nl)):
        x0, h1, u1, ac1, dp1, z1, o1, x1, h2, u2, uc2, hid2, o2, win_g, wout_g, wup_g, wdn_g = saved[l]
        (dx, du2, do2, rowd2, rowb2, dfw), got = _ffn_backward(
            dx, x1, o2, u2, uc2, mod_own[l], vec_d[l], fw_rows[l], wup_g, wdn_g, tm, name=f"ffn_bwd_{l}",
            exchange=_combine([mlp.chips(), mix.swap()]) if mlp else None)
        if mlp:
            mlp.after_chips(got[:2])
            mix.after_swap(got[2:])
        g_up, got = _weight_grad(flat(h2), flat(du2), N_CHIPS, "b", tt, name=f"grad_ffn_up_{l}",
                                 exchange=_combine([mlp.share(), mix.chips()]) if mlp else None)
        if mlp:
            big_mlp[l + 1] = mlp.result(got[:2])
            mix.after_chips(got[2:])
        g_dn, _ = _weight_grad(flat(hid2), flat(do2), 2, "a", tt, name=f"grad_ffn_down_{l}")
        mlp_above, mlp = mlp, _ReduceScatter([g_up, g_dn.reshape(N_CHIPS, -1, d)], cidx, idx, f"mlp_{l}")
        if l == 0:
            mlp.after_swap(_run_exchange(mlp.swap(), name="rs_swap_mlp_0"))
        first = mlp.swap() if l > 0 else mlp.chips()
        (dx, du1, do1, rowd1, rowb1, rowc, dcw, dpw), got = _mixer_backward(
            dx, x0, o1, u1, ac1, dp1, mod_own[l], vec_d[l], vec_c[l], cw_pad[l], pw_b[l], win_g, wout_g, taps, tm_mix,
            name=f"mixer_bwd_{l}", exchange=_combine([first, mix.share()]) if mlp_above else first)
        if l > 0:
            mlp.after_swap(got[:2])
        else:
            mlp.after_chips(got[:2])
        if mlp_above:
            big_mix[l + 1] = mix.result(got[2:])
        g_in, got = _weight_grad(flat(h1), flat(du1), N_CHIPS, "cols", tt, name=f"grad_w_in_{l}",
                                 exchange=mlp.share() if l == 0 else None)
        if l == 0:
            big_mlp[0] = mlp.result(got)
        g_out, _ = _weight_grad(flat(z1), flat(do1), 1, "cols", tt, name=f"grad_w_out_{l}")
        mix = _ReduceScatter([g_in, g_out.reshape(N_CHIPS, -1, d)], cidx, idx, f"mix_{l}")
        small[l] = dict(rowd=rowd1 + rowd2, rowb=rowb1 + rowb2, rowc=rowc, dcw=dcw[:taps], dpw=dpw, dfw=dfw)
    mix.after_swap(_run_exchange(mix.swap(), name="rs_swap_mix_0"))

    dmod_own = jnp.stack([small[l]["rowb"][:, :nmod, :] for l in range(nl)])
    dmod_all, got = _gather8(_rows128(dmod_own), name="gather_dmod", exchange=mix.chips())
    mix.after_chips(got)
    big_mix[0] = mix.result(_run_exchange(mix.share(), name="rs_share_mix_0"))
    dmod_all = dmod_all.reshape(N_DEV, nl, nb, nmod * d)
    dmod_all = dmod_all.transpose(1, 0, 2, 3).reshape(nl, N_DEV * nb, nmod * d)
    dmod_cols = lax.dynamic_slice_in_dim(dmod_all, jm * acol, acol, axis=2)
    (g_ada_w, d_ada_w, nm_ada_w, nv_ada_w), _ = _ada_update(c_all, dmod_cols, ada_w, m_ada_w, v_ada_w, name="ada_update")

    def st(key, row=None):
        return jnp.stack([small[l][key] if row is None else small[l][key][row] for l in range(nl)])

    local = {
        "ada_b": dmod_own.sum(axis=1).reshape(nl, nmod * d),
        "pre_mix_g": st("rowd", 0), "post_mix_g": st("rowd", 1),
        "conv_b": st("rowc", 0), "conv_ln_g": st("rowc", 1), "conv_ln_b": st("rowc", 2),
        "pool_w": st("dpw"), "pool_scale": st("rowc", 3),
        "pre_ffn_g": st("rowd", 2), "post_ffn_g": st("rowd", 3),
        "ffn_conv_b": st("dfw", 3), "conv_w": st("dcw"), "ffn_conv_w": jnp.stack([small[l]["dfw"][:3] for l in range(nl)]),
    }
    names = list(local)
    sizes = [local[k].size for k in names]
    pad = -sum(sizes) % (4 * SUBLANES * LANES)
    packed = jnp.concatenate([local[k].reshape(-1) for k in names] + [jnp.zeros((pad,), F32)])
    summed = _allreduce8(_rows128(packed), name="allreduce_small").reshape(-1)
    grads, off = {}, 0
    for k, sz in zip(names, sizes):
        grads[k] = summed[off:off + sz].reshape(local[k].shape)
        off += sz
    grads["conv_w"] = lax.dynamic_slice_in_dim(grads["conv_w"], jm * ccol, ccol, axis=2)
    grads["ffn_conv_w"] = lax.dynamic_slice_in_dim(grads["ffn_conv_w"], jm * fcol, fcol, axis=2)

    params = dict(ada_b=(ada_b, m_ada_b, v_ada_b), pre_mix_g=(pre_mix_g, m_pre_mix_g, v_pre_mix_g),
                  post_mix_g=(post_mix_g, m_post_mix_g, v_post_mix_g), conv_b=(conv_b, m_conv_b, v_conv_b),
                  conv_ln_g=(conv_ln_g, m_conv_ln_g, v_conv_ln_g), conv_ln_b=(conv_ln_b, m_conv_ln_b, v_conv_ln_b),
                  pool_w=(pool_w, m_pool_w, v_pool_w), pool_scale=(pool_scale, m_pool_scale, v_pool_scale),
                  pre_ffn_g=(pre_ffn_g, m_pre_ffn_g, v_pre_ffn_g), post_ffn_g=(post_ffn_g, m_post_ffn_g, v_post_ffn_g),
                  ffn_conv_b=(ffn_conv_b, m_ffn_conv_b, v_ffn_conv_b), conv_w=(conv_w, m_conv_w, v_conv_w),
                  ffn_conv_w=(ffn_conv_w, m_ffn_conv_w, v_ffn_conv_w))
    pack = lambda i, g=None: _rows128(jnp.concatenate([(grads[k] if g else params[k][i]).reshape(-1) for k in names]))
    sd, sm, sv = _adamw_flat(pack(0), pack(0, True), pack(1), pack(2), name="adamw_small")
    outs = {}
    off = 0
    for k in names:
        shape, sz = params[k][0].shape, params[k][0].size
        outs[k] = (grads[k],) + tuple(a.reshape(-1)[off:off + sz].reshape(shape) for a in (sd, sm, sv))
        off += sz

    outs["ada_w"] = (g_ada_w, d_ada_w, nm_ada_w, nv_ada_w)
    for k, w, m, v, gs in [("w_in", w_in, m_w_in, v_w_in, [big_mix[l][0] for l in range(nl)]),
                           ("w_out", w_out, m_w_out, v_w_out, [big_mix[l][1] for l in range(nl)]),
                           ("ffn_up", ffn_up, m_ffn_up, v_ffn_up, [big_mlp[l][0] for l in range(nl)]),
                           ("ffn_down", ffn_down, m_ffn_down, v_ffn_down, [big_mlp[l][1] for l in range(nl)])]:
        outs[k] = tuple(_adamw_layers(w, m, v, gs, name=f"adamw_{k}"))

    order = ["ada_w", "ada_b", "pre_mix_g", "post_mix_g", "w_in", "conv_w", "conv_b", "conv_ln_g", "conv_ln_b", "pool_w",
             "pool_scale", "w_out", "pre_ffn_g", "post_ffn_g", "ffn_up", "ffn_conv_w", "ffn_conv_b", "ffn_down"]
    return (loss, dx) + tuple(outs[k][i] for i in range(4) for k in order)
```

```python
import functools

import jax
import jax.numpy as jnp
from jax import lax
from jax.experimental import pallas as pl
from jax.experimental.pallas import tpu as pltpu

F32 = jnp.float32
BF16 = jnp.bfloat16
MESH = pl.DeviceIdType.MESH

EPS = 1e-6
POOL_WINDOWS = (2, 4, 8, 16)
ADAM_LR = 0.001
ADAM_B1 = 0.9
ADAM_B2 = 0.999
ADAM_EPS = 1e-08
ADAM_WD = 0.01
ADAM_STEP = 10

N_CHIPS = 4
N_DEV = 8
LANES = 128
SUBLANES = 8
HALO = 32
FHALO = 8
VMEM_LIMIT = 60 * 1024 * 1024
MLP_TILE_ROWS = 256
MIXER_TILE_ROWS = 512
GRAD_CHUNK_ROWS = 2048


def _pos():
    return lax.axis_index("x"), lax.axis_index("y"), lax.axis_index("c")


def _flip(v, f):
    return 1 - v if f else v


def _full(shape):
    nd = len(shape)
    return pl.BlockSpec(shape, lambda *_: (0,) * nd)


_ANY = pl.BlockSpec(memory_space=pl.ANY)
_VMEM = pl.BlockSpec(memory_space=pltpu.VMEM)


def _sigmoid(v):
    return 1.0 / (1.0 + jnp.exp(-v))


def _colsum(v):
    return jnp.sum(v, axis=0, keepdims=True)


def _gather8(v, name, exchange=None):
    r, ccols = v.shape
    ex = exchange
    nci, nco = (len(ex.ins), len(ex.outs)) if ex else (0, 0)

    def body(*refs):
        v_ref, cin, out_ref, cout = refs[0], refs[1:1 + nci], refs[1 + nci], refs[2 + nci:2 + nci + nco]
        send_sems, recv_sems, local_sem = refs[2 + nci + nco:5 + nci + nco]
        if ex:
            sems = _Sems(*refs[5 + nci + nco:])
            ex.start(cin, cout, sems)
        x, y, c = _pos()
        me = 4 * x + 2 * y + c
        mine = pltpu.make_async_copy(v_ref, out_ref.at[me], local_sem)
        mine.start()
        peers = [(_flip(x, (k >> 2) & 1), _flip(y, (k >> 1) & 1), _flip(c, k & 1)) for k in range(1, N_DEV)]
        sends = []
        for k, peer in enumerate(peers):
            cp = pltpu.make_async_remote_copy(src_ref=v_ref, dst_ref=out_ref.at[me], send_sem=send_sems.at[k],
                                              recv_sem=recv_sems.at[k], device_id=peer, device_id_type=MESH)
            cp.start()
            sends.append(cp)
        for k, peer in enumerate(peers):
            pidx = 4 * peer[0] + 2 * peer[1] + peer[2]
            pltpu.make_async_remote_copy(src_ref=v_ref, dst_ref=out_ref.at[pidx], send_sem=send_sems.at[k],
                                         recv_sem=recv_sems.at[k], device_id=peer, device_id_type=MESH).wait_recv()
        for cp in sends:
            cp.wait_send()
        mine.wait()
        if ex:
            if ex.mid is not None:
                ex.mid(cin, cout, sems)
            ex.finish(cin, cout, sems)

    outs = pl.pallas_call(
        body, name=name, out_shape=[jax.ShapeDtypeStruct((N_DEV, r, ccols), v.dtype)] + (ex.outs if ex else []),
        in_specs=[_VMEM] + [_ANY] * nci, out_specs=[_VMEM] + [_ANY] * nco,
        scratch_shapes=[pltpu.SemaphoreType.DMA((N_DEV - 1,)), pltpu.SemaphoreType.DMA((N_DEV - 1,)),
                        pltpu.SemaphoreType.DMA(())] + (ex.scratch() if ex else []),
        input_output_aliases={1 + a: 1 + b for a, b in ex.aliases.items()} if ex else {},
        compiler_params=pltpu.CompilerParams(vmem_limit_bytes=VMEM_LIMIT),
    )(v, *(ex.ins if ex else []))
    return (outs[0], list(outs[1:])) if ex else outs[0]


def _rows_to_owners(v, name):
    _, r, ccols = v.shape

    def body(v_ref, out_ref, send_sems, recv_sems, local_sem):
        x, y, c = _pos()
        jm = 2 * x + y
        mine = pltpu.make_async_copy(v_ref.at[2 * jm + c], out_ref.at[jm], local_sem)
        mine.start()
        peers, pjs = _chip_peers(x, y, c)
        sends = []
        for k, peer in enumerate(peers):
            cp = pltpu.make_async_remote_copy(src_ref=v_ref.at[2 * pjs[k] + c], dst_ref=out_ref.at[jm],
                                              send_sem=send_sems.at[k], recv_sem=recv_sems.at[k],
                                              device_id=peer, device_id_type=MESH)
            cp.start()
            sends.append(cp)
        for k, peer in enumerate(peers):
            pltpu.make_async_remote_copy(src_ref=v_ref.at[0], dst_ref=out_ref.at[pjs[k]], send_sem=send_sems.at[k],
                                         recv_sem=recv_sems.at[k], device_id=peer, device_id_type=MESH).wait_recv()
        for cp in sends:
            cp.wait_send()
        mine.wait()

    return pl.pallas_call(
        body, name=name, out_shape=jax.ShapeDtypeStruct((N_CHIPS, r, ccols), v.dtype),
        in_specs=[_VMEM], out_specs=_VMEM,
        scratch_shapes=[pltpu.SemaphoreType.DMA((N_CHIPS - 1,)), pltpu.SemaphoreType.DMA((N_CHIPS - 1,)),
                        pltpu.SemaphoreType.DMA(())],
        compiler_params=pltpu.CompilerParams(vmem_limit_bytes=VMEM_LIMIT),
    )(v)


def _allreduce8(v, name):
    r, ccols = v.shape
    h = r // 2
    q = h // 2

    def body(v_ref, out_ref, whole, part, done, send_sems, recv_sems):
        x, y, c = _pos()
        sib = (x, y, 1 - c)
        mine = pl.ds(pl.multiple_of(c * h, SUBLANES), h)
        theirs = pl.ds(pl.multiple_of((1 - c) * h, SUBLANES), h)
        quarters = [pl.ds(pl.multiple_of(c * h + k * q, SUBLANES), q) for k in range(2)]
        along_x, along_y = (1 - x, y, c), (x, 1 - y, c)

        def exchange(pairs):
            cps = [pltpu.make_async_remote_copy(src_ref=src, dst_ref=dst, send_sem=send_sems.at[k], recv_sem=recv_sems.at[k],
                                                device_id=peer, device_id_type=MESH) for src, dst, k, peer in pairs]
            for cp in cps:
                cp.start()
            for cp in cps:
                cp.wait()

        exchange([(v_ref, whole, 0, sib)])
        out_ref[...] = v_ref[...] + whole[...]
        for stage, peers in enumerate(((along_x, along_y), (along_y, along_x))):
            exchange([(out_ref.at[quarters[k]], part.at[2 * stage + k], 1 + 2 * stage + k, peers[k]) for k in range(2)])
            for k in range(2):
                out_ref[quarters[k], :] = out_ref[quarters[k], :] + part[2 * stage + k]
        exchange([(out_ref.at[mine], done, 5, sib)])
        out_ref[theirs, :] = done[...]

    return pl.pallas_call(
        body, name=name, out_shape=jax.ShapeDtypeStruct((r, ccols), v.dtype),
        in_specs=[_VMEM], out_specs=_VMEM,
        scratch_shapes=[pltpu.VMEM((r, ccols), v.dtype), pltpu.VMEM((4, q, ccols), v.dtype), pltpu.VMEM((h, ccols), v.dtype),
                        pltpu.SemaphoreType.DMA((6,)), pltpu.SemaphoreType.DMA((6,))],
        compiler_params=pltpu.CompilerParams(vmem_limit_bytes=VMEM_LIMIT),
    )(v)


def _chip_peers(x, y, c):
    peers = [(_flip(x, (k >> 1) & 1), _flip(y, k & 1), c) for k in range(1, N_CHIPS)]
    return peers, [2 * p[0] + p[1] for p in peers]


class _Exchange:
    def __init__(self, ins, outs, aliases, n_sems, n_local, start, finish, mid=None, mid_at=1.0):
        self.ins, self.outs, self.aliases = list(ins), list(outs), dict(aliases)
        self.n_sems, self.n_local, self.start, self.finish = n_sems, n_local, start, finish
        self.mid, self.mid_at = mid, mid_at

    def scratch(self):
        return [pltpu.SemaphoreType.DMA((self.n_sems,)), pltpu.SemaphoreType.DMA((self.n_sems,)),
                pltpu.SemaphoreType.DMA((max(self.n_local, 1),))]


class _Sems:
    def __init__(self, send, recv, local, base=0, lbase=0):
        self.send, self.recv, self.loc, self.base, self.lbase = send, recv, local, base, lbase

    def shifted(self, by, lby):
        return _Sems(self.send, self.recv, self.loc, self.base + by, self.lbase + lby)

    def local(self, k):
        return self.loc.at[self.lbase + k]


def _remote(src, dst, sems, k, peer):
    return pltpu.make_async_remote_copy(src_ref=src, dst_ref=dst, send_sem=sems.send.at[sems.base + k],
                                        recv_sem=sems.recv.at[sems.base + k], device_id=peer, device_id_type=MESH)


def _combine(exs):
    ins = [a for ex in exs for a in ex.ins]
    outs = [o for ex in exs for o in ex.outs]
    aliases, spans, ni, no, ns, nloc = {}, [], 0, 0, 0, 0
    for ex in exs:
        aliases.update({ni + a: no + b for a, b in ex.aliases.items()})
        spans.append((ni, no, ns, nloc))
        ni, no, ns, nloc = ni + len(ex.ins), no + len(ex.outs), ns + ex.n_sems, nloc + ex.n_local

    def each(which):
        def run(ins_, outs_, sems):
            for ex, (i0, o0, s0, l0) in zip(exs, spans):
                stage = getattr(ex, which)
                if stage is not None:
                    stage(ins_[i0:i0 + len(ex.ins)], outs_[o0:o0 + len(ex.outs)], sems.shifted(s0, l0))
        return run

    mids = [ex.mid_at for ex in exs if ex.mid is not None]
    return _Exchange(ins, outs, aliases, ns, nloc, each("start"), each("finish"),
                     mid=each("mid") if mids else None, mid_at=max(mids) if mids else 1.0)


def _gather(shards, mid_at=1.0):
    n = len(shards)
    per = N_CHIPS - 1
    halves = [s.reshape(2, s.shape[0] // 2, s.shape[1]) for s in shards]

    def copies(ins, outs, sems):
        x, y, c = _pos()
        jm = 2 * x + y
        sib = (x, y, 1 - c)
        peers, pjs = _chip_peers(x, y, c)
        sends, recvs, passes, passed = [], [], [], []
        for a in range(n):
            own = _remote(ins[a], outs[a].at[jm], sems, 2 * n * per + a, sib)
            sends.append(own)
            passed.append(own)
            for k, peer in enumerate(peers):
                landed, theirs = outs[a].at[pjs[k], c], outs[a].at[pjs[k], 1 - c]
                sends.append(_remote(ins[a].at[c], outs[a].at[jm, c], sems, 2 * (a * per + k), peer))
                recvs.append(_remote(landed, landed, sems, 2 * (a * per + k), peer))
                passes.append(_remote(landed, landed, sems, 2 * (a * per + k) + 1, sib))
                passed.append(_remote(theirs, theirs, sems, 2 * (a * per + k) + 1, sib))
        return sends, recvs, passes, passed

    def start(ins, outs, sems):
        for cp in copies(ins, outs, sems)[0]:
            cp.start()

    def mid(ins, outs, sems):
        _, recvs, passes, _ = copies(ins, outs, sems)
        for got, fwd in zip(recvs, passes):
            got.wait_recv()
            fwd.start()

    def finish(ins, outs, sems):
        sends, _, passes, passed = copies(ins, outs, sems)
        for cp in passed:
            cp.wait_recv()
        for cp in sends + passes:
            cp.wait_send()

    outs = [jax.ShapeDtypeStruct((N_CHIPS,) + h.shape, h.dtype) for h in halves]
    return _Exchange(halves, outs, {}, 2 * n * per + n, 0, start, finish, mid=mid, mid_at=mid_at)


def _whole(gathered):
    return [g.reshape(g.shape[0], g.shape[1] * g.shape[2], g.shape[3]) for g in gathered]


def _swap_halves(gs):
    n = len(gs)
    halves = [g.reshape(g.shape[0], 2, g.shape[1] // 2, g.shape[2]) for g in gs]

    def copies(ins, outs, sems):
        x, y, c = _pos()
        sib = (x, y, 1 - c)
        return [_remote(ins[a].at[:, 1 - c], outs[a], sems, a, sib) for a in range(n)]

    def start(ins, outs, sems):
        for cp in copies(ins, outs, sems):
            cp.start()

    def finish(ins, outs, sems):
        for cp in copies(ins, outs, sems):
            cp.wait()

    outs = [jax.ShapeDtypeStruct((g.shape[0], g.shape[1] // 2, g.shape[2]), g.dtype) for g in gs]
    return _Exchange(halves, outs, {}, n, 0, start, finish)


def _chip_exchange(ps):
    n = len(ps)
    per = N_CHIPS - 1

    def copies(ins, outs, sems):
        x, y, c = _pos()
        peers, pjs = _chip_peers(x, y, c)
        return [_remote(ins[a].at[pjs[k]], outs[a].at[k], sems, a * per + k, peer)
                for a in range(n) for k, peer in enumerate(peers)]

    def start(ins, outs, sems):
        for cp in copies(ins, outs, sems):
            cp.start()

    def finish(ins, outs, sems):
        for cp in copies(ins, outs, sems):
            cp.wait()

    outs = [jax.ShapeDtypeStruct((per,) + p.shape[1:], p.dtype) for p in ps]
    return _Exchange(ps, outs, {}, n * per, 0, start, finish)


def _sibling_share(fs):
    n = len(fs)

    def copies(outs, sems):
        x, y, c = _pos()
        sib = (x, y, 1 - c)
        sends = [_remote(outs[a].at[c], outs[a].at[c], sems, a, sib) for a in range(n)]
        recvs = [_remote(outs[a].at[1 - c], outs[a].at[1 - c], sems, a, sib) for a in range(n)]
        return sends, recvs

    def start(ins, outs, sems):
        for cp in copies(outs, sems)[0]:
            cp.start()

    def finish(ins, outs, sems):
        sends, recvs = copies(outs, sems)
        for cp in recvs:
            cp.wait_recv()
        for cp in sends:
            cp.wait_send()

    outs = [jax.ShapeDtypeStruct(f.shape, f.dtype) for f in fs]
    return _Exchange(fs, outs, {a: a for a in range(n)}, n, 0, start, finish)


def _run_exchange(ex, name):
    ni, no = len(ex.ins), len(ex.outs)

    def body(*refs):
        ins, outs, sems = refs[:ni], refs[ni:ni + no], _Sems(*refs[ni + no:])
        ex.start(ins, outs, sems)
        if ex.mid is not None:
            ex.mid(ins, outs, sems)
        ex.finish(ins, outs, sems)

    return pl.pallas_call(
        body, name=name, out_shape=ex.outs, in_specs=[_ANY] * ni, out_specs=[_ANY] * no,
        input_output_aliases=ex.aliases, scratch_shapes=ex.scratch(),
    )(*ex.ins)


def _grid_call(body, name, grid, in_specs, out_specs, out_shape, scratch_shapes, args, exchange=None):
    ni, no = len(in_specs), len(out_specs)
    params = pltpu.CompilerParams(dimension_semantics=("arbitrary",) * len(grid), vmem_limit_bytes=VMEM_LIMIT)
    if exchange is None:
        outs = pl.pallas_call(body, name=name, grid=grid, in_specs=in_specs, out_specs=out_specs, out_shape=out_shape,
                              scratch_shapes=scratch_shapes, compiler_params=params)(*args)
        return list(outs), []
    ex = exchange
    nci, nco = len(ex.ins), len(ex.outs)

    def hosted(*refs):
        cin = refs[ni:ni + nci]
        cout = refs[ni + nci + no:ni + nci + no + nco]
        sems = _Sems(*refs[len(refs) - 3:])
        main = refs[:ni] + refs[ni + nci:ni + nci + no] + refs[ni + nci + no + nco:len(refs) - 3]
        ids = [pl.program_id(a) for a in range(len(grid))]
        first = functools.reduce(lambda p, q: p & q, [i == 0 for i in ids])
        last = functools.reduce(lambda p, q: p & q, [i == g - 1 for i, g in zip(ids, grid)])

        @pl.when(first)
        def _():
            ex.start(cin, cout, sems)

        if ex.mid is not None:
            steps = functools.reduce(lambda p, q: p * q, grid)
            flat = functools.reduce(lambda p, q: p * q[1] + q[0], zip(ids[1:], grid[1:]), ids[0])

            @pl.when(flat == min(steps - 1, int(ex.mid_at * steps)))
            def _():
                ex.mid(cin, cout, sems)

        body(*main)

        @pl.when(last)
        def _():
            ex.finish(cin, cout, sems)

    outs = pl.pallas_call(
        hosted, name=name, grid=grid, in_specs=list(in_specs) + [_ANY] * nci, out_specs=list(out_specs) + [_ANY] * nco,
        out_shape=list(out_shape) + ex.outs, scratch_shapes=list(scratch_shapes) + ex.scratch(),
        input_output_aliases={ni + a: no + b for a, b in ex.aliases.items()}, compiler_params=params,
    )(*args, *ex.ins)
    return list(outs[:no]), list(outs[no:])


SUM_BLOCK_BYTES = 4 * 1024 * 1024
ADAM_BLOCK_BYTES = 2 * 1024 * 1024


def _row_tile(rows, cols, itemsize, budget):
    best = None
    for t in range(16, rows + 1, 16):
        if rows % t == 0 and t * cols * itemsize <= budget:
            best = t
    return best if best is not None else rows


def _pair_sum(g, r1, cidx, name):
    nj, r, ccols = g.shape
    hr = r // 2
    tr = _row_tile(hr, ccols, 4, SUM_BLOCK_BYTES)
    nt = hr // tr

    def body(c_ref, g_ref, r_ref, o_ref):
        o_ref[...] = (g_ref[...].astype(F32) + r_ref[...].astype(F32)).astype(o_ref.dtype)

    return pl.pallas_call(
        body, name=name, out_shape=jax.ShapeDtypeStruct((nj, hr, ccols), g.dtype),
        grid_spec=pltpu.PrefetchScalarGridSpec(
            num_scalar_prefetch=1, grid=(nj, nt),
            in_specs=[pl.BlockSpec((None, tr, ccols), lambda j, i, c_ref: (j, c_ref[0] * nt + i, 0)),
                      pl.BlockSpec((None, tr, ccols), lambda j, i, c_ref: (j, i, 0))],
            out_specs=pl.BlockSpec((None, tr, ccols), lambda j, i, c_ref: (j, i, 0))),
        compiler_params=pltpu.CompilerParams(dimension_semantics=("arbitrary", "arbitrary"), vmem_limit_bytes=VMEM_LIMIT),
    )(cidx, g, r1)


def _chip_sum(p, r2, idx, name):
    nj, hr, ccols = p.shape
    tr = _row_tile(hr, ccols, 4, SUM_BLOCK_BYTES)
    nt = hr // tr

    def body(i_ref, p_ref, r_ref, o_ref):
        s = p_ref[...].astype(F32)
        for k in range(N_CHIPS - 1):
            s = s + r_ref[k].astype(F32)
        o_ref[...] = s

    return pl.pallas_call(
        body, name=name, out_shape=jax.ShapeDtypeStruct((2, hr, ccols), F32),
        grid_spec=pltpu.PrefetchScalarGridSpec(
            num_scalar_prefetch=1, grid=(nt,),
            in_specs=[pl.BlockSpec((None, tr, ccols), lambda i, i_ref: (i_ref[0], i, 0)),
                      pl.BlockSpec((N_CHIPS - 1, tr, ccols), lambda i, i_ref: (0, i, 0))],
            out_specs=pl.BlockSpec((None, tr, ccols), lambda i, i_ref: (i_ref[1], i, 0))),
        compiler_params=pltpu.CompilerParams(dimension_semantics=("arbitrary",), vmem_limit_bytes=VMEM_LIMIT),
    )(idx, p, r2)


def _adam_math(w, g, m, v):
    m2 = ADAM_B1 * m + (1.0 - ADAM_B1) * g
    v2 = ADAM_B2 * v + (1.0 - ADAM_B2) * (g * g)
    m_hat = m2 / (1.0 - ADAM_B1 ** ADAM_STEP)
    v_hat = v2 / (1.0 - ADAM_B2 ** ADAM_STEP)
    delta = -ADAM_LR * (m_hat / (jnp.sqrt(v_hat) + ADAM_EPS) + ADAM_WD * w)
    return delta, m2, v2


def _adamw_layers(w, m, v, gs, name):
    nl, r, ccols = w.shape
    ng = len(gs)
    tr = _row_tile(r, ccols, 4, ADAM_BLOCK_BYTES)
    nt = r // tr

    def body(w_ref, m_ref, v_ref, *rest):
        g_refs, (go_ref, d_ref, mo_ref, vo_ref) = rest[:ng], rest[ng:]
        l = pl.program_id(0)
        g = g_refs[0][...]
        for k in range(1, ng):
            g = jnp.where(l == k, g_refs[k][...], g)
        delta, m2, v2 = _adam_math(w_ref[...], g, m_ref[...], v_ref[...])
        go_ref[...] = g
        d_ref[...] = delta
        mo_ref[...] = m2
        vo_ref[...] = v2

    big = pl.BlockSpec((None, tr, ccols), lambda l, i: (l, i, 0))

    def gspec(k):
        return pl.BlockSpec((tr, ccols), lambda l, i: (jnp.where(l == k, i, jnp.where(l < k, 0, nt - 1)), 0))

    assert ng == nl
    return _grid_call(body, name, (nl, nt), in_specs=[big, big, big] + [gspec(k) for k in range(ng)],
                      out_specs=[big, big, big, big], out_shape=[jax.ShapeDtypeStruct(w.shape, F32)] * 4,
                      scratch_shapes=[], args=(w, m, v, *gs))[0]


def _adamw_flat(w, g, m, v, name):
    r, ccols = w.shape

    def body(w_ref, g_ref, m_ref, v_ref, d_ref, mo_ref, vo_ref):
        delta, m2, v2 = _adam_math(w_ref[...], g_ref[...], m_ref[...], v_ref[...])
        d_ref[...] = delta
        mo_ref[...] = m2
        vo_ref[...] = v2

    return pl.pallas_call(
        body, name=name, out_shape=[jax.ShapeDtypeStruct((r, ccols), F32)] * 3,
        in_specs=[_VMEM] * 4, out_specs=[_VMEM] * 3,
        compiler_params=pltpu.CompilerParams(vmem_limit_bytes=VMEM_LIMIT),
    )(w, g, m, v)


def _ada_forward(c_all, ada_w, ada_b_cols, name):
    nl, d, ncols = ada_w.shape
    bg = c_all.shape[0]
    tn = 512 if ncols % 512 == 0 else ncols

    def body(c_ref, w_ref, b_ref, o_ref):
        cv = c_ref[...]
        ca = (cv * _sigmoid(cv)).astype(BF16)
        o_ref[...] = jnp.dot(ca, w_ref[...].astype(BF16), preferred_element_type=F32) + b_ref[...]

    return pl.pallas_call(
        body, name=name, out_shape=jax.ShapeDtypeStruct((nl, bg, ncols), F32),
        grid=(nl, ncols // tn),
        in_specs=[pl.BlockSpec((bg, d), lambda l, j: (0, 0)),
                  pl.BlockSpec((None, d, tn), lambda l, j: (l, 0, j)),
                  pl.BlockSpec((None, 1, tn), lambda l, j: (l, 0, j))],
        out_specs=pl.BlockSpec((None, bg, tn), lambda l, j: (l, 0, j)),
        compiler_params=pltpu.CompilerParams(dimension_semantics=("arbitrary", "arbitrary")),
    )(c_all, ada_w, ada_b_cols)


def _ada_update(c_all, dmod_cols, w, m, v, name, exchange=None):
    nl, d, ncols = w.shape
    bg = c_all.shape[0]
    tn = 512 if ncols % 512 == 0 else ncols

    def body(c_ref, dm_ref, w_ref, m_ref, v_ref, go_ref, d_ref, mo_ref, vo_ref):
        cv = c_ref[...]
        ca = (cv * _sigmoid(cv)).astype(BF16)
        g = lax.dot_general(ca, dm_ref[...].astype(BF16), (((0,), (0,)), ((), ())), preferred_element_type=F32)
        delta, m2, v2 = _adam_math(w_ref[...], g, m_ref[...], v_ref[...])
        go_ref[...] = g
        d_ref[...] = delta
        mo_ref[...] = m2
        vo_ref[...] = v2

    big = pl.BlockSpec((None, d, tn), lambda l, j: (l, 0, j))
    return _grid_call(
        body, name, (nl, ncols // tn),
        in_specs=[pl.BlockSpec((bg, d), lambda l, j: (0, 0)),
                  pl.BlockSpec((None, bg, tn), lambda l, j: (l, 0, j)), big, big, big],
        out_specs=[big, big, big, big], out_shape=[jax.ShapeDtypeStruct(w.shape, F32)] * 4,
        scratch_shapes=[], args=(c_all, dmod_cols, w, m, v), exchange=exchange)


def _load_weights(first, pairs, sems):
    @pl.when(first)
    def _():
        cps = [pltpu.make_async_copy(src, dst, sems.at[k]) for k, (src, dst) in enumerate(pairs)]
        for cp in cps:
            cp.start()
        for cp in cps:
            cp.wait()


def _ada_norm(xv, g, sc, sh):
    r = lax.rsqrt(jnp.mean(xv * xv, axis=-1, keepdims=True) + EPS)
    xn = xv * r
    return (xn * g) * (1.0 + sc) + sh, xn, r


def _ada_norm_bwd(dh, xn, r, g, sc):
    d_sh = _colsum(dh)
    d_sc = _colsum(dh * (xn * g))
    dxg = dh * (1.0 + sc)
    d_g = _colsum(dxg * xn)
    gd = dxg * g
    dx = r * (gd - xn * jnp.mean(gd * xn, axis=-1, keepdims=True))
    return dx, d_sh, d_sc, d_g


def _gated_residual_bwd(dxo, o, g_post, gt):
    r = lax.rsqrt(jnp.mean(o * o, axis=-1, keepdims=True) + EPS)
    on = o * r
    d_gt = _colsum(dxo * (on * g_post))
    dy = dxo * (1.0 + gt)
    d_gp = _colsum(dy * on)
    gd = dy * g_post
    do = r * (gd - on * jnp.mean(gd * on, axis=-1, keepdims=True))
    return do, d_gt, d_gp


def _seq_positions(i, tm, width):
    return i * tm + lax.broadcasted_iota(jnp.int32, (tm, width), 0)


def _fill_phases(ext, phases):
    rows = ext.shape[0]
    ev = ext[...]
    for r in range(1, SUBLANES):
        phases[r - 1] = pltpu.roll(ev, rows - r, axis=0)


def _shifted_rows(ext, phases, offset, n):
    q, r = divmod(offset, SUBLANES)
    if r == 0:
        return ext[pl.ds(q * SUBLANES, n), :]
    return phases[r - 1, pl.ds(q * SUBLANES, n), :]


def _rows_before(halo, cur, shift):
    e = jnp.concatenate([halo, cur], axis=0)
    return pltpu.roll(e, shift, axis=0)[halo.shape[0]:, :]


def _rows_after(cur, halo, shift):
    e = jnp.concatenate([cur, halo], axis=0)
    return pltpu.roll(e, e.shape[0] - shift, axis=0)[:cur.shape[0], :]


def _mixer_forward(x, mod, vec_d, vec_c, cw, pw, win_g, wout_g, taps, tm, name, exchange=None):
    nb, s, d = x.shape
    n = s // tm
    nj, _, dcol = win_g.shape
    din = nj * dcol
    dc = vec_c.shape[-1]
    dpool = din - 2 * dc
    dmix = dc + dpool
    ro = wout_g.shape[1]
    ngrp = dpool // LANES

    def body(x_ref, mod_ref, vd_ref, vc_ref, cw_ref, pw_ref, win_hbm, wout_hbm,
             xo_ref, h_ref, u_ref, ac_ref, dp_ref, z_ref, o_ref,
             win_v, wout_v, ext_a, ext_p, phases, sems):
        b, i = pl.program_id(0), pl.program_id(1)
        pairs = [(win_hbm.at[j], win_v.at[:, pl.ds(j * dcol, dcol)]) for j in range(nj)]
        pairs += [(wout_hbm.at[j], wout_v.at[pl.ds(j * ro, ro), :]) for j in range(nj)]
        _load_weights((b == 0) & (i == 0), pairs, sems)

        xv = x_ref[...]
        h, _, _ = _ada_norm(xv, vd_ref[0:1, :], mod_ref[1:2, :], mod_ref[0:1, :])
        hb = h.astype(BF16)
        h_ref[...] = hb
        u = jnp.dot(hb, win_v[...], preferred_element_type=F32)
        u_ref[...] = u.astype(BF16)
        ag = u[:, :dc] * _sigmoid(u[:, dc:2 * dc])
        up = u[:, 2 * dc:]

        @pl.when(i == 0)
        def _():
            ext_a[0:HALO, :] = jnp.zeros((HALO, dc), F32)
            ext_p[0:HALO, :] = jnp.zeros((HALO, dpool), F32)

        @pl.when(i > 0)
        def _():
            ext_a[0:HALO, :] = ext_a[tm:tm + HALO, :]
            ext_p[0:HALO, :] = ext_p[tm:tm + HALO, :]

        ext_a[HALO:HALO + tm, :] = ag
        ext_p[HALO:HALO + tm, :] = up

        acc = jnp.broadcast_to(vc_ref[0:1, :], (tm, dc))
        _fill_phases(ext_a, phases)
        for k in range(taps):
            acc = acc + cw_ref[k:k + 1, :] * _shifted_rows(ext_a, phases, HALO - (taps - 1) + k, tm)
        ac_ref[...] = acc.astype(BF16)
        mu = jnp.mean(acc, axis=-1, keepdims=True)
        xc = acc - mu
        var = jnp.mean(xc * xc, axis=-1, keepdims=True)
        al = (xc * lax.rsqrt(var + EPS)) * vc_ref[1:2, :] + vc_ref[2:3, :]
        a = al * _sigmoid(al)

        pos = _seq_positions(i, tm, LANES)
        parts = [a.astype(BF16)]
        for g in range(ngrp):
            w = POOL_WINDOWS[g]
            cols = slice(g * LANES, (g + 1) * LANES)
            sw = ext_p[:, cols]
            step = 1
            while step < w:
                sw = sw + pltpu.roll(sw, step, axis=0)
                step *= 2
            cnt = jnp.minimum(pos + 1, w).astype(F32)
            dg = (sw[HALO:, :] / cnt - up[:, cols]).astype(BF16)
            dp_ref[:, cols] = dg
            q = jnp.dot(dg, pw_ref[g], preferred_element_type=F32)
            parts.append((q * vc_ref[3:4, cols]).astype(BF16))
        z = jnp.concatenate(parts, axis=-1)
        z_ref[...] = z
        o = jnp.dot(z, wout_v[...], preferred_element_type=F32)
        o_ref[...] = o
        r2 = lax.rsqrt(jnp.mean(o * o, axis=-1, keepdims=True) + EPS)
        xo_ref[...] = xv + (1.0 + mod_ref[2:3, :]) * ((o * r2) * vd_ref[1:2, :])

    def tile(width):
        return pl.BlockSpec((None, tm, width), lambda b, i: (b, i, 0))

    return _grid_call(
        body, name, (nb, n),
        in_specs=[tile(d), pl.BlockSpec((None, 8, d), lambda b, i: (b, 0, 0)), _full(vec_d.shape), _full(vec_c.shape),
                  _full(cw.shape), _full(pw.shape), _ANY, _ANY],
        out_specs=[tile(d), tile(d), tile(din), tile(dc), tile(dpool), tile(dmix), tile(d)],
        out_shape=[jax.ShapeDtypeStruct((nb, s, d), F32), jax.ShapeDtypeStruct((nb, s, d), BF16),
                   jax.ShapeDtypeStruct((nb, s, din), BF16), jax.ShapeDtypeStruct((nb, s, dc), BF16),
                   jax.ShapeDtypeStruct((nb, s, dpool), BF16), jax.ShapeDtypeStruct((nb, s, dmix), BF16),
                   jax.ShapeDtypeStruct((nb, s, d), F32)],
        scratch_shapes=[pltpu.VMEM((d, din), BF16), pltpu.VMEM((dmix, d), BF16),
                        pltpu.VMEM((HALO + tm, dc), F32), pltpu.VMEM((HALO + tm, dpool), F32),
                        pltpu.VMEM((SUBLANES - 1, HALO + tm, dc), F32), pltpu.SemaphoreType.DMA((2 * nj,))],
        args=(x, mod, vec_d, vec_c, cw, pw, win_g, wout_g), exchange=exchange)


def _mixer_backward(dxo, x, o, u, ac, dpl, mod, vec_d, vec_c, cw, pw, win_g, wout_g, taps, tm, name, exchange=None):
    nb, s, d = x.shape
    n = s // tm
    nj, _, dcol = win_g.shape
    din = nj * dcol
    dc = vec_c.shape[-1]
    dpool = din - 2 * dc
    dmix = dc + dpool
    ro = wout_g.shape[1]
    ngrp = dpool // LANES
    rext = tm + HALO

    def body(dxo_ref, x_ref, o_ref, u_ref, ac_ref, dp_ref, mod_ref, vd_ref, vc_ref, cw_ref, pw_ref, win_hbm, wout_hbm,
             dx_ref, du_ref, dob_ref, rowd_ref, rowb_ref, rowc_ref, dcw_ref, dpw_ref,
             win_v, wout_v, ext_a, ext_p, phases, sems):
        b, i = pl.program_id(0), pl.program_id(1)
        first = (b == 0) & (i == 0)
        pairs = [(win_hbm.at[j], win_v.at[:, pl.ds(j * dcol, dcol)]) for j in range(nj)]
        pairs += [(wout_hbm.at[j], wout_v.at[pl.ds(j * ro, ro), :]) for j in range(nj)]
        _load_weights(first, pairs, sems)

        @pl.when(first)
        def _():
            rowd_ref[...] = jnp.zeros_like(rowd_ref)
            rowc_ref[...] = jnp.zeros_like(rowc_ref)
            dcw_ref[...] = jnp.zeros_like(dcw_ref)
            dpw_ref[...] = jnp.zeros_like(dpw_ref)

        @pl.when(i == 0)
        def _():
            rowb_ref[...] = jnp.zeros_like(rowb_ref)
            ext_a[tm:rext, :] = jnp.zeros((HALO, dc), F32)
            ext_p[tm:rext, :] = jnp.zeros((HALO, dpool), F32)

        @pl.when(i > 0)
        def _():
            ext_a[tm:rext, :] = ext_a[0:HALO, :]
            ext_p[tm:rext, :] = ext_p[0:HALO, :]

        g_pre, g_post = vd_ref[0:1, :], vd_ref[1:2, :]
        sh, sc, gt = mod_ref[0:1, :], mod_ref[1:2, :], mod_ref[2:3, :]
        do, d_gt, d_gp = _gated_residual_bwd(dxo_ref[...], o_ref[...], g_post, gt)
        dob = do.astype(BF16)
        dob_ref[...] = dob
        dz = lax.dot_general(dob, wout_v[...], (((1,), (1,)), ((), ())), preferred_element_type=F32)

        acv = ac_ref[...].astype(F32)
        mu = jnp.mean(acv, axis=-1, keepdims=True)
        xc = acv - mu
        rstd = lax.rsqrt(jnp.mean(xc * xc, axis=-1, keepdims=True) + EPS)
        an = xc * rstd
        lg = vc_ref[1:2, :]
        al = an * lg + vc_ref[2:3, :]
        sg = _sigmoid(al)
        dal = dz[:, :dc] * (sg * (1.0 + al * (1.0 - sg)))
        d_lg = _colsum(dal * an)
        d_lb = _colsum(dal)
        dan = dal * lg
        dac = rstd * (dan - jnp.mean(dan, axis=-1, keepdims=True) - an * jnp.mean(dan * an, axis=-1, keepdims=True))
        d_cb = _colsum(dac)
        ext_a[0:tm, :] = dac
        uv = u_ref[:, 0:dc].astype(F32)
        sgu = _sigmoid(u_ref[:, dc:2 * dc].astype(F32))
        ag = uv * sgu
        dag = jnp.zeros((tm, dc), F32)
        _fill_phases(ext_a, phases)
        for k in range(taps):
            sl = _shifted_rows(ext_a, phases, taps - 1 - k, tm)
            dag = dag + cw_ref[k:k + 1, :] * sl
            dcw_ref[k:k + 1, :] += _colsum(ag * sl)
        du_ref[:, 0:dc] = (dag * sgu).astype(BF16)
        du_ref[:, dc:2 * dc] = (dag * uv * (sgu * (1.0 - sgu))).astype(BF16)

        pos = _seq_positions(n - 1 - i, tm, LANES)
        d_ps = []
        for g in range(ngrp):
            w = POOL_WINDOWS[g]
            cols = slice(g * LANES, (g + 1) * LANES)
            gcols = slice(dc + g * LANES, dc + (g + 1) * LANES)
            dgb = dp_ref[:, cols]
            q = jnp.dot(dgb, pw_ref[g], preferred_element_type=F32)
            dpg = dz[:, gcols]
            d_ps.append(_colsum(dpg * q))
            dq = (dpg * vc_ref[3:4, cols]).astype(BF16)
            dpw_ref[g] += lax.dot_general(dgb, dq, (((0,), (0,)), ((), ())), preferred_element_type=F32)
            dd = lax.dot_general(dq, pw_ref[g], (((1,), (1,)), ((), ())), preferred_element_type=F32)
            cnt = jnp.minimum(pos + 1, w).astype(F32)
            ext_p[0:tm, cols] = dd / cnt
            sw = ext_p[:, cols]
            step = 1
            while step < w:
                sw = sw + pltpu.roll(sw, rext - step, axis=0)
                step *= 2
            du_ref[:, 2 * dc + g * LANES:2 * dc + (g + 1) * LANES] = (sw[0:tm, :] - dd).astype(BF16)
        rowc_ref[0:1, :] += d_cb
        rowc_ref[1:2, :] += d_lg
        rowc_ref[2:3, :] += d_lb
        rowc_ref[3:4, :] += jnp.concatenate(d_ps, axis=-1)

        dh = lax.dot_general(du_ref[...], win_v[...], (((1,), (1,)), ((), ())), preferred_element_type=F32)
        _, xn, r1 = _ada_norm(x_ref[...], g_pre, sc, sh)
        dxb, d_sh, d_sc, d_g = _ada_norm_bwd(dh, xn, r1, g_pre, sc)
        dx_ref[...] = dxo_ref[...] + dxb
        rowd_ref[0:1, :] += d_g
        rowd_ref[1:2, :] += d_gp
        rowb_ref[0:1, :] += d_sh
        rowb_ref[1:2, :] += d_sc
        rowb_ref[2:3, :] += d_gt

    def tile(width):
        return pl.BlockSpec((None, tm, width), lambda b, i: (b, n - 1 - i, 0))

    return _grid_call(
        body, name, (nb, n),
        in_specs=[tile(d), tile(d), tile(d), tile(din), tile(dc), tile(dpool),
                  pl.BlockSpec((None, 8, d), lambda b, i: (b, 0, 0)), _full(vec_d.shape), _full(vec_c.shape),
                  _full(cw.shape), _full(pw.shape), _ANY, _ANY],
        out_specs=[tile(d), tile(din), tile(d), _full((8, d)), pl.BlockSpec((None, 8, d), lambda b, i: (b, 0, 0)),
                   _full((8, dc)), _full((HALO, dc)), _full(pw.shape)],
        out_shape=[jax.ShapeDtypeStruct((nb, s, d), F32), jax.ShapeDtypeStruct((nb, s, din), BF16),
                   jax.ShapeDtypeStruct((nb, s, d), BF16), jax.ShapeDtypeStruct((8, d), F32),
                   jax.ShapeDtypeStruct((nb, 8, d), F32), jax.ShapeDtypeStruct((8, dc), F32),
                   jax.ShapeDtypeStruct((HALO, dc), F32), jax.ShapeDtypeStruct(pw.shape, F32)],
        scratch_shapes=[pltpu.VMEM((d, din), BF16), pltpu.VMEM((dmix, d), BF16),
                        pltpu.VMEM((rext, dc), F32), pltpu.VMEM((rext, dpool), F32),
                        pltpu.VMEM((SUBLANES - 1, rext, dc), F32), pltpu.SemaphoreType.DMA((2 * nj,))],
        args=(dxo, x, o, u, ac, dpl, mod, vec_d, vec_c, cw, pw, win_g, wout_g), exchange=exchange)


def _ffn_forward(x, mod, vec_d, fw, wup_g, wdn_g, tm, name, exchange=None):
    nb, s, d = x.shape
    n = s // tm
    nj, _, ucol = wup_g.shape
    f2 = nj * ucol
    dff = f2 // 2
    rd = wdn_g.shape[1]
    nq = nj // 2
    cs = dff // nq

    def body(x_ref, mod_ref, vd_ref, fw_ref, wup_hbm, wdn_hbm,
             xo_ref, h_ref, u_ref, uc_ref, hid_ref, o_ref,
             wup_v, wdn_v, prev_u, sems):
        b, i = pl.program_id(0), pl.program_id(1)
        pairs = [(wup_hbm.at[j], wup_v.at[:, pl.ds(j * ucol, ucol)]) for j in range(nj)]
        pairs += [(wdn_hbm.at[j], wdn_v.at[pl.ds(j * rd, rd), :]) for j in range(nj)]
        _load_weights((b == 0) & (i == 0), pairs, sems)

        @pl.when(i == 0)
        def _():
            prev_u[...] = jnp.zeros_like(prev_u)

        xv = x_ref[...]
        h, _, _ = _ada_norm(xv, vd_ref[2:3, :], mod_ref[4:5, :], mod_ref[3:4, :])
        hb = h.astype(BF16)
        h_ref[...] = hb

        def conv(cols):
            uc = jnp.dot(hb, wup_v[:, cols], preferred_element_type=F32)
            u_ref[:, cols] = uc.astype(BF16)
            before = prev_u[:, cols]
            prev_u[:, cols] = uc[tm - FHALO:, :]
            out = (fw_ref[3:4, cols] + fw_ref[2:3, cols] * uc + fw_ref[1:2, cols] * _rows_before(before, uc, 1)
                   + fw_ref[0:1, cols] * _rows_before(before, uc, 2))
            uc_ref[:, cols] = out.astype(BF16)
            return out

        o = jnp.zeros((tm, d), F32)
        for q in range(nq):
            val = conv(pl.ds(q * cs, cs))
            gate = conv(pl.ds(dff + q * cs, cs))
            hid = ((gate * _sigmoid(gate)) * val).astype(BF16)
            hid_ref[:, pl.ds(q * cs, cs)] = hid
            o = o + jnp.dot(hid, wdn_v[pl.ds(q * cs, cs), :], preferred_element_type=F32)
        o_ref[...] = o
        r2 = lax.rsqrt(jnp.mean(o * o, axis=-1, keepdims=True) + EPS)
        xo_ref[...] = xv + (1.0 + mod_ref[5:6, :]) * ((o * r2) * vd_ref[3:4, :])

    def tile(width):
        return pl.BlockSpec((None, tm, width), lambda b, i: (b, i, 0))

    return _grid_call(
        body, name, (nb, n),
        in_specs=[tile(d), pl.BlockSpec((None, 8, d), lambda b, i: (b, 0, 0)), _full(vec_d.shape), _full(fw.shape),
                  _ANY, _ANY],
        out_specs=[tile(d), tile(d), tile(f2), tile(f2), tile(dff), tile(d)],
        out_shape=[jax.ShapeDtypeStruct((nb, s, d), F32), jax.ShapeDtypeStruct((nb, s, d), BF16),
                   jax.ShapeDtypeStruct((nb, s, f2), BF16), jax.ShapeDtypeStruct((nb, s, f2), BF16),
                   jax.ShapeDtypeStruct((nb, s, dff), BF16), jax.ShapeDtypeStruct((nb, s, d), F32)],
        scratch_shapes=[pltpu.VMEM((d, f2), BF16), pltpu.VMEM((dff, d), BF16),
                        pltpu.VMEM((FHALO, f2), F32), pltpu.SemaphoreType.DMA((2 * nj,))],
        args=(x, mod, vec_d, fw, wup_g, wdn_g), exchange=exchange)


def _ffn_backward(dxo, x, o, u, uc, mod, vec_d, fw, wup_g, wdn_g, tm, name, exchange=None):
    nb, s, d = x.shape
    n = s // tm
    nj, _, ucol = wup_g.shape
    f2 = nj * ucol
    dff = f2 // 2
    rd = wdn_g.shape[1]
    nq = nj // 2
    cs = dff // nq

    def body(dxo_ref, x_ref, o_ref, u_ref, uc_ref, mod_ref, vd_ref, fw_ref, wup_hbm, wdn_hbm,
             dx_ref, du_ref, dob_ref, rowd_ref, rowb_ref, dfw_ref,
             wup_v, wdn_v, next_d, sems):
        b, i = pl.program_id(0), pl.program_id(1)
        first = (b == 0) & (i == 0)
        pairs = [(wup_hbm.at[j], wup_v.at[:, pl.ds(j * ucol, ucol)]) for j in range(nj)]
        pairs += [(wdn_hbm.at[j], wdn_v.at[pl.ds(j * rd, rd), :]) for j in range(nj)]
        _load_weights(first, pairs, sems)

        @pl.when(first)
        def _():
            rowd_ref[...] = jnp.zeros_like(rowd_ref)
            dfw_ref[...] = jnp.zeros_like(dfw_ref)

        @pl.when(i == 0)
        def _():
            rowb_ref[...] = jnp.zeros_like(rowb_ref)
            next_d[...] = jnp.zeros_like(next_d)

        g_pre, g_post = vd_ref[2:3, :], vd_ref[3:4, :]
        sh, sc, gt = mod_ref[3:4, :], mod_ref[4:5, :], mod_ref[5:6, :]
        do, d_gt, d_gp = _gated_residual_bwd(dxo_ref[...], o_ref[...], g_post, gt)
        dob = do.astype(BF16)
        dob_ref[...] = dob

        def conv_bwd(cols, duc):
            uc = u_ref[:, cols].astype(F32)
            after = next_d[:, cols]
            next_d[:, cols] = duc[0:FHALO, :]
            d1 = _rows_after(duc, after, 1)
            d2 = _rows_after(duc, after, 2)
            dfw_ref[3:4, cols] += _colsum(duc)
            dfw_ref[2:3, cols] += _colsum(uc * duc)
            dfw_ref[1:2, cols] += _colsum(uc * d1)
            dfw_ref[0:1, cols] += _colsum(uc * d2)
            ob = (fw_ref[2:3, cols] * duc + fw_ref[1:2, cols] * d1 + fw_ref[0:1, cols] * d2).astype(BF16)
            du_ref[:, cols] = ob
            return lax.dot_general(ob, wup_v[:, cols], (((1,), (1,)), ((), ())), preferred_element_type=F32)

        dh = jnp.zeros((tm, d), F32)
        for q in range(nq):
            vcols = pl.ds(q * cs, cs)
            gcols = pl.ds(dff + q * cs, cs)
            dhid = lax.dot_general(dob, wdn_v[vcols, :], (((1,), (1,)), ((), ())), preferred_element_type=F32)
            val = uc_ref[:, vcols].astype(F32)
            gate = uc_ref[:, gcols].astype(F32)
            sg = _sigmoid(gate)
            act = gate * sg
            dval = dhid * act
            dgate = (dhid * val) * (sg + act * (1.0 - sg))
            dh = dh + conv_bwd(vcols, dval)
            dh = dh + conv_bwd(gcols, dgate)

        _, xn, r1 = _ada_norm(x_ref[...], g_pre, sc, sh)
        dxb, d_sh, d_sc, d_g = _ada_norm_bwd(dh, xn, r1, g_pre, sc)
        dx_ref[...] = dxo_ref[...] + dxb
        rowd_ref[2:3, :] += d_g
        rowd_ref[3:4, :] += d_gp
        rowb_ref[3:4, :] += d_sh
        rowb_ref[4:5, :] += d_sc
        rowb_ref[5:6, :] += d_gt

    def tile(width):
        return pl.BlockSpec((None, tm, width), lambda b, i: (b, n - 1 - i, 0))

    return _grid_call(
        body, name, (nb, n),
        in_specs=[tile(d), tile(d), tile(d), tile(f2), tile(f2), pl.BlockSpec((None, 8, d), lambda b, i: (b, 0, 0)),
                  _full(vec_d.shape), _full(fw.shape), _ANY, _ANY],
        out_specs=[tile(d), tile(f2), tile(d), _full((8, d)), pl.BlockSpec((None, 8, d), lambda b, i: (b, 0, 0)),
                   _full(fw.shape)],
        out_shape=[jax.ShapeDtypeStruct((nb, s, d), F32), jax.ShapeDtypeStruct((nb, s, f2), BF16),
                   jax.ShapeDtypeStruct((nb, s, d), BF16), jax.ShapeDtypeStruct((8, d), F32),
                   jax.ShapeDtypeStruct((nb, 8, d), F32), jax.ShapeDtypeStruct(fw.shape, F32)],
        scratch_shapes=[pltpu.VMEM((d, f2), BF16), pltpu.VMEM((dff, d), BF16),
                        pltpu.VMEM((FHALO, f2), F32), pltpu.SemaphoreType.DMA((2 * nj,))],
        args=(dxo, x, o, u, uc, mod, vec_d, fw, wup_g, wdn_g), exchange=exchange)


def _weight_grad(a, b, nblk, split, tt, name, exchange=None):
    t, ka = a.shape
    nb_ = b.shape[1]
    nk = t // tt
    if split == "cols":
        wa, wb, grid = ka, nb_ // nblk, (1, nk)
        a_spec = pl.BlockSpec((tt, ka), lambda j, k: (k, 0))
        b_spec = pl.BlockSpec((tt, nb_), lambda j, k: (k, 0))
        o_spec = pl.BlockSpec((nblk, wa, wb), lambda j, k: (0, 0, 0))
        acc_shape = (ka, nb_)
    elif split == "b":
        wa, wb, grid = ka, nb_ // nblk, (nblk, nk)
        a_spec = pl.BlockSpec((tt, wa), lambda j, k: (k, 0))
        b_spec = pl.BlockSpec((tt, wb), lambda j, k: (k, j))
        o_spec = pl.BlockSpec((None, wa, wb), lambda j, k: (j, 0, 0))
        acc_shape = (wa, wb)
    else:
        wa, wb, grid = ka // nblk, nb_, (nblk, nk)
        a_spec = pl.BlockSpec((tt, wa), lambda j, k: (k, j))
        b_spec = pl.BlockSpec((tt, wb), lambda j, k: (k, 0))
        o_spec = pl.BlockSpec((None, wa, wb), lambda j, k: (j, 0, 0))
        acc_shape = (wa, wb)

    def body(a_ref, b_ref, o_ref, acc):
        k = pl.program_id(1)
        prod = lax.dot_general(a_ref[...], b_ref[...], (((0,), (0,)), ((), ())), preferred_element_type=F32)

        @pl.when(k == 0)
        def _():
            acc[...] = prod

        @pl.when(k > 0)
        def _():
            acc[...] += prod

        @pl.when(k == nk - 1)
        def _():
            if split == "cols":
                for j in range(nblk):
                    o_ref[j] = acc[:, j * wb:(j + 1) * wb].astype(o_ref.dtype)
            else:
                o_ref[...] = acc[...].astype(o_ref.dtype)

    outs, exo = _grid_call(body, name, grid, in_specs=[a_spec, b_spec], out_specs=[o_spec],
                           out_shape=[jax.ShapeDtypeStruct((nblk, wa, wb), BF16)],
                           scratch_shapes=[pltpu.VMEM(acc_shape, F32)], args=(a, b), exchange=exchange)
    return outs[0], exo


def _loss_grad(y, tgt, tm, name):
    nb, s, d = y.shape
    n = s // tm

    def body(y_ref, t_ref, dy_ref, sq_ref):
        @pl.when((pl.program_id(0) == 0) & (pl.program_id(1) == 0))
        def _():
            sq_ref[...] = jnp.zeros_like(sq_ref)

        e = y_ref[...] - t_ref[...]
        dy_ref[...] = e * (1.0 / d)
        sq_ref[0:1, :] += _colsum(e * e)

    tile = pl.BlockSpec((None, tm, d), lambda b, i: (b, i, 0))
    return pl.pallas_call(
        body, name=name, out_shape=[jax.ShapeDtypeStruct((nb, s, d), F32), jax.ShapeDtypeStruct((8, d), F32)],
        grid=(nb, n), in_specs=[tile, tile], out_specs=[tile, _full((8, d))],
        compiler_params=pltpu.CompilerParams(dimension_semantics=("arbitrary", "arbitrary")),
    )(y, tgt)


def _rows128(a):
    return a.reshape(-1, LANES)


class _ReduceScatter:
    def __init__(self, gs, cidx, idx, tag):
        self.gs, self.cidx, self.idx, self.tag = gs, cidx, idx, tag

    def swap(self):
        return _swap_halves(self.gs)

    def after_swap(self, r1):
        self.ps = [_pair_sum(g, r, self.cidx, name=f"rs_pair_{self.tag}_{a}") for a, (g, r) in enumerate(zip(self.gs, r1))]

    def chips(self):
        return _chip_exchange(self.ps)

    def after_chips(self, r2):
        self.fh = [_chip_sum(p, r, self.idx, name=f"rs_sum_{self.tag}_{a}") for a, (p, r) in enumerate(zip(self.ps, r2))]

    def share(self):
        return _sibling_share(self.fh)

    @staticmethod
    def result(fs):
        return [f.reshape(f.shape[0] * f.shape[1], f.shape[2]) for f in fs]


def kernel(x, c, ada_w, ada_b, pre_mix_g, post_mix_g, w_in, conv_w, conv_b, conv_ln_g, conv_ln_b, pool_w, pool_scale, w_out, pre_ffn_g, post_ffn_g, ffn_up, ffn_conv_w, ffn_conv_b, ffn_down, loss_target, m_ada_w, m_ada_b, m_pre_mix_g, m_post_mix_g, m_w_in, m_conv_w, m_conv_b, m_conv_ln_g, m_conv_ln_b, m_pool_w, m_pool_scale, m_w_out, m_pre_ffn_g, m_post_ffn_g, m_ffn_up, m_ffn_conv_w, m_ffn_conv_b, m_ffn_down, v_ada_w, v_ada_b, v_pre_mix_g, v_post_mix_g, v_w_in, v_conv_w, v_conv_b, v_conv_ln_g, v_conv_ln_b, v_pool_w, v_pool_scale, v_w_out, v_pre_ffn_g, v_post_ffn_g, v_ffn_up, v_ffn_conv_w, v_ffn_conv_b, v_ffn_down):
    nb, s, d = x.shape
    nl = w_in.shape[0]
    taps = conv_w.shape[1]
    ccol = conv_w.shape[2]
    dc = conv_b.shape[1]
    fcol = ffn_conv_w.shape[2]
    f2 = ffn_conv_b.shape[1]
    nmod = ada_b.shape[1] // d
    acol = ada_w.shape[2]
    tm = min(MLP_TILE_ROWS, s)
    tm_mix = min(MIXER_TILE_ROWS, s)
    tt = min(GRAD_CHUNK_ROWS, (nb * s) // 2)

    xi, yi, ci = _pos()
    jm = 2 * xi + yi
    cidx = jnp.reshape(ci, (1,)).astype(jnp.int32)
    idx = jnp.stack([jm, ci]).astype(jnp.int32)

    win_b, wout_b, wup_b, wdn_b = (w.astype(BF16) for w in (w_in, w_out, ffn_up, ffn_down))

    def others(l):
        return [win_b[l], wout_b[l], wdn_b[l]]

    n_cw, n_fw, n_c = nl * taps * ccol, nl * 3 * fcol, nb * d
    packed = jnp.concatenate([conv_w.reshape(-1), ffn_conv_w.reshape(-1), c.reshape(-1)])
    got, first_weights = _gather8(_rows128(packed), name="gather_small", exchange=_gather(others(0)))
    got = got.reshape(N_DEV, -1)
    chips = got[0::2]
    cw_full = chips[:, :n_cw].reshape(N_CHIPS, nl, taps, ccol).transpose(1, 2, 0, 3).reshape(nl, taps, dc)
    fw_full = chips[:, n_cw:n_cw + n_fw].reshape(N_CHIPS, nl, 3, fcol).transpose(1, 2, 0, 3).reshape(nl, 3, f2)
    c_all = got[:, n_cw + n_fw:].reshape(N_DEV * nb, d)

    ada_b_cols = lax.dynamic_slice_in_dim(ada_b, jm * acol, acol, axis=1).reshape(nl, 1, acol)
    mod_cols = _ada_forward(c_all, ada_w, ada_b_cols, name="ada_forward")
    by_owner = mod_cols.reshape(nl, N_DEV, nb, acol).transpose(1, 0, 2, 3).reshape(N_DEV, -1, LANES)
    mod_own = _rows_to_owners(by_owner, name="mod_to_owners").reshape(N_CHIPS, nl, nb, acol)
    mod_own = mod_own.transpose(1, 2, 0, 3).reshape(nl, nb, nmod, d)
    mod_own = jnp.pad(mod_own, ((0, 0), (0, 0), (0, 8 - nmod), (0, 0)))

    vec_d = jnp.stack([pre_mix_g, post_mix_g, pre_ffn_g, post_ffn_g], axis=1)
    vec_c = jnp.stack([conv_b, conv_ln_g, conv_ln_b, pool_scale], axis=1)
    cw_pad = jnp.pad(cw_full, ((0, 0), (0, HALO - taps), (0, 0)))
    fw_rows = jnp.concatenate([fw_full, ffn_conv_b[:, None, :], jnp.zeros((nl, 4, f2), F32)], axis=1)
    pw_b = pool_w.astype(BF16)

    win_g, wout_g, wdn_g = _whole(first_weights)
    saved = []
    xs = x
    for l in range(nl):
        (x1, h1, u1, ac1, dp1, z1, o1), got = _mixer_forward(
            xs, mod_own[l], vec_d[l], vec_c[l], cw_pad[l], pw_b[l], win_g, wout_g, taps, tm_mix, name=f"mixer_fwd_{l}",
            exchange=_gather([wup_b[l]], mid_at=0.9))
        wup_g, = _whole(got)
        (x2, h2, u2, uc2, hid2, o2), nxt = _ffn_forward(
            x1, mod_own[l], vec_d[l], fw_rows[l], wup_g, wdn_g, tm, name=f"ffn_fwd_{l}",
            exchange=_gather(others(l + 1), mid_at=0.6) if l + 1 < nl else None)
        saved.append((xs, h1, u1, ac1, dp1, z1, o1, x1, h2, u2, uc2, hid2, o2, win_g, wout_g, wup_g, wdn_g))
        if l + 1 < nl:
            win_g, wout_g, wdn_g = _whole(nxt)
        xs = x2

    dx, sq = _loss_grad(xs, loss_target, tm_mix, name="loss_grad")
    loss = lax.psum(0.5 * jnp.sum(sq) / d, ("x", "y", "c"))

    flat = lambda a: a.reshape(nb * s, a.shape[-1])
    small = [None] * nl
    big_mlp, big_mix = [None] * nl, [None] * nl
    mlp = mix = None
    for l in reversed(range(nl)):
        x0, h1, u1, ac1, dp1, z1, o1, x1, h2, u2, uc2, hid2, o2, win_g, wout_g, wup_g, wdn_g = saved[l]
        (dx, du2, do2, rowd2, rowb2, dfw), got = _ffn_backward(
            dx, x1, o2, u2, uc2, mod_own[l], vec_d[l], fw_rows[l], wup_g, wdn_g, tm, name=f"ffn_bwd_{l}",
            exchange=_combine([mlp.chips(), mix.swap()]) if mlp else None)
        if mlp:
            mlp.after_chips(got[:2])
            mix.after_swap(got[2:])
        g_up, got = _weight_grad(flat(h2), flat(du2), N_CHIPS, "b", tt, name=f"grad_ffn_up_{l}",
                                 exchange=_combine([mlp.share(), mix.chips()]) if mlp else None)
        if mlp:
            big_mlp[l + 1] = mlp.result(got[:2])
            mix.after_chips(got[2:])
        g_dn, _ = _weight_grad(flat(hid2), flat(do2), 2, "a", tt, name=f"grad_ffn_down_{l}")
        mlp_above, mlp = mlp, _ReduceScatter([g_up, g_dn.reshape(N_CHIPS, -1, d)], cidx, idx, f"mlp_{l}")
        if l == 0:
            mlp.after_swap(_run_exchange(mlp.swap(), name="rs_swap_mlp_0"))
        first = mlp.swap() if l > 0 else mlp.chips()
        (dx, du1, do1, rowd1, rowb1, rowc, dcw, dpw), got = _mixer_backward(
            dx, x0, o1, u1, ac1, dp1, mod_own[l], vec_d[l], vec_c[l], cw_pad[l], pw_b[l], win_g, wout_g, taps, tm_mix,
            name=f"mixer_bwd_{l}", exchange=_combine([first, mix.share()]) if mlp_above else first)
        if l > 0:
            mlp.after_swap(got[:2])
        else:
            mlp.after_chips(got[:2])
        if mlp_above:
            big_mix[l + 1] = mix.result(got[2:])
        g_in, got = _weight_grad(flat(h1), flat(du1), N_CHIPS, "cols", tt, name=f"grad_w_in_{l}",
                                 exchange=mlp.share() if l == 0 else None)
        if l == 0:
            big_mlp[0] = mlp.result(got)
        g_out, _ = _weight_grad(flat(z1), flat(do1), 1, "cols", tt, name=f"grad_w_out_{l}")
        mix = _ReduceScatter([g_in, g_out.reshape(N_CHIPS, -1, d)], cidx, idx, f"mix_{l}")
        small[l] = dict(rowd=rowd1 + rowd2, rowb=rowb1 + rowb2, rowc=rowc, dcw=dcw[:taps], dpw=dpw, dfw=dfw)
    mix.after_swap(_run_exchange(mix.swap(), name="rs_swap_mix_0"))

    dmod_own = jnp.stack([small[l]["rowb"][:, :nmod, :] for l in range(nl)])
    dmod_all, got = _gather8(_rows128(dmod_own), name="gather_dmod", exchange=mix.chips())
    mix.after_chips(got)
    big_mix[0] = mix.result(_run_exchange(mix.share(), name="rs_share_mix_0"))
    dmod_all = dmod_all.reshape(N_DEV, nl, nb, nmod * d)
    dmod_all = dmod_all.transpose(1, 0, 2, 3).reshape(nl, N_DEV * nb, nmod * d)
    dmod_cols = lax.dynamic_slice_in_dim(dmod_all, jm * acol, acol, axis=2)
    (g_ada_w, d_ada_w, nm_ada_w, nv_ada_w), _ = _ada_update(c_all, dmod_cols, ada_w, m_ada_w, v_ada_w, name="ada_update")

    def st(key, row=None):
        return jnp.stack([small[l][key] if row is None else small[l][key][row] for l in range(nl)])

    local = {
        "ada_b": dmod_own.sum(axis=1).reshape(nl, nmod * d),
        "pre_mix_g": st("rowd", 0), "post_mix_g": st("rowd", 1),
        "conv_b": st("rowc", 0), "conv_ln_g": st("rowc", 1), "conv_ln_b": st("rowc", 2),
        "pool_w": st("dpw"), "pool_scale": st("rowc", 3),
        "pre_ffn_g": st("rowd", 2), "post_ffn_g": st("rowd", 3),
        "ffn_conv_b": st("dfw", 3), "conv_w": st("dcw"), "ffn_conv_w": jnp.stack([small[l]["dfw"][:3] for l in range(nl)]),
    }
    names = list(local)
    sizes = [local[k].size for k in names]
    pad = -sum(sizes) % (4 * SUBLANES * LANES)
    packed = jnp.concatenate([local[k].reshape(-1) for k in names] + [jnp.zeros((pad,), F32)])
    summed = _allreduce8(_rows128(packed), name="allreduce_small").reshape(-1)
    grads, off = {}, 0
    for k, sz in zip(names, sizes):
        grads[k] = summed[off:off + sz].reshape(local[k].shape)
        off += sz
    grads["conv_w"] = lax.dynamic_slice_in_dim(grads["conv_w"], jm * ccol, ccol, axis=2)
    grads["ffn_conv_w"] = lax.dynamic_slice_in_dim(grads["ffn_conv_w"], jm * fcol, fcol, axis=2)

    params = dict(ada_b=(ada_b, m_ada_b, v_ada_b), pre_mix_g=(pre_mix_g, m_pre_mix_g, v_pre_mix_g),
                  post_mix_g=(post_mix_g, m_post_mix_g, v_post_mix_g), conv_b=(conv_b, m_conv_b, v_conv_b),
                  conv_ln_g=(conv_ln_g, m_conv_ln_g, v_conv_ln_g), conv_ln_b=(conv_ln_b, m_conv_ln_b, v_conv_ln_b),
                  pool_w=(pool_w, m_pool_w, v_pool_w), pool_scale=(pool_scale, m_pool_scale, v_pool_scale),
                  pre_ffn_g=(pre_ffn_g, m_pre_ffn_g, v_pre_ffn_g), post_ffn_g=(post_ffn_g, m_post_ffn_g, v_post_ffn_g),
                  ffn_conv_b=(ffn_conv_b, m_ffn_conv_b, v_ffn_conv_b), conv_w=(conv_w, m_conv_w, v_conv_w),
                  ffn_conv_w=(ffn_conv_w, m_ffn_conv_w, v_ffn_conv_w))
    pack = lambda i, g=None: _rows128(jnp.concatenate([(grads[k] if g else params[k][i]).reshape(-1) for k in names]))
    sd, sm, sv = _adamw_flat(pack(0), pack(0, True), pack(1), pack(2), name="adamw_small")
    outs = {}
    off = 0
    for k in names:
        shape, sz = params[k][0].shape, params[k][0].size
        outs[k] = (grads[k],) + tuple(a.reshape(-1)[off:off + sz].reshape(shape) for a in (sd, sm, sv))
        off += sz

    outs["ada_w"] = (g_ada_w, d_ada_w, nm_ada_w, nv_ada_w)
    for k, w, m, v, gs in [("w_in", w_in, m_w_in, v_w_in, [big_mix[l][0] for l in range(nl)]),
                           ("w_out", w_out, m_w_out, v_w_out, [big_mix[l][1] for l in range(nl)]),
                           ("ffn_up", ffn_up, m_ffn_up, v_ffn_up, [big_mlp[l][0] for l in range(nl)]),
                           ("ffn_down", ffn_down, m_ffn_down, v_ffn_down, [big_mlp[l][1] for l in range(nl)])]:
        outs[k] = tuple(_adamw_layers(w, m, v, gs, name=f"adamw_{k}"))

    order = ["ada_w", "ada_b", "pre_mix_g", "post_mix_g", "w_in", "conv_w", "conv_b", "conv_ln_g", "conv_ln_b", "pool_w",
             "pool_scale", "w_out", "pre_ffn_g", "post_ffn_g", "ffn_up", "ffn_conv_w", "ffn_conv_b", "ffn_down"]
    return (loss, dx) + tuple(outs[k][i] for i in range(4) for k in order)
```

```python
import functools

import jax
import jax.numpy as jnp
from jax import lax
from jax.experimental import pallas as pl
from jax.experimental.pallas import tpu as pltpu

F32 = jnp.float32
BF16 = jnp.bfloat16
MESH = pl.DeviceIdType.MESH

EPS = 1e-6
POOL_WINDOWS = (2, 4, 8, 16)
ADAM_LR = 0.001
ADAM_B1 = 0.9
ADAM_B2 = 0.999
ADAM_EPS = 1e-08
ADAM_WD = 0.01
ADAM_STEP = 10

N_CHIPS = 4
N_DEV = 8
LANES = 128
SUBLANES = 8
HALO = 32
FHALO = 8
VMEM_LIMIT = 60 * 1024 * 1024
MLP_TILE_ROWS = 256
MIXER_TILE_ROWS = 512
GRAD_CHUNK_ROWS = 2048


def _pos():
    return lax.axis_index("x"), lax.axis_index("y"), lax.axis_index("c")


def _flip(v, f):
    return 1 - v if f else v


def _full(shape):
    nd = len(shape)
    return pl.BlockSpec(shape, lambda *_: (0,) * nd)


_ANY = pl.BlockSpec(memory_space=pl.ANY)
_VMEM = pl.BlockSpec(memory_space=pltpu.VMEM)


def _sigmoid(v):
    return 1.0 / (1.0 + jnp.exp(-v))


def _colsum(v):
    return jnp.sum(v, axis=0, keepdims=True)


def _gather8(v, name, exchange=None):
    r, ccols = v.shape
    ex = exchange
    nci, nco = (len(ex.ins), len(ex.outs)) if ex else (0, 0)

    def body(*refs):
        v_ref, cin, out_ref, cout = refs[0], refs[1:1 + nci], refs[1 + nci], refs[2 + nci:2 + nci + nco]
        send_sems, recv_sems, local_sem = refs[2 + nci + nco:5 + nci + nco]
        if ex:
            sems = _Sems(*refs[5 + nci + nco:])
            ex.start(cin, cout, sems)
        x, y, c = _pos()
        me = 4 * x + 2 * y + c
        mine = pltpu.make_async_copy(v_ref, out_ref.at[me], local_sem)
        mine.start()
        peers = [(_flip(x, (k >> 2) & 1), _flip(y, (k >> 1) & 1), _flip(c, k & 1)) for k in range(1, N_DEV)]
        sends = []
        for k, peer in enumerate(peers):
            cp = pltpu.make_async_remote_copy(src_ref=v_ref, dst_ref=out_ref.at[me], send_sem=send_sems.at[k],
                                              recv_sem=recv_sems.at[k], device_id=peer, device_id_type=MESH)
            cp.start()
            sends.append(cp)
        for k, peer in enumerate(peers):
            pidx = 4 * peer[0] + 2 * peer[1] + peer[2]
            pltpu.make_async_remote_copy(src_ref=v_ref, dst_ref=out_ref.at[pidx], send_sem=send_sems.at[k],
                                         recv_sem=recv_sems.at[k], device_id=peer, device_id_type=MESH).wait_recv()
        for cp in sends:
            cp.wait_send()
        mine.wait()
        if ex:
            if ex.mid is not None:
                ex.mid(cin, cout, sems)
            ex.finish(cin, cout, sems)

    outs = pl.pallas_call(
        body, name=name, out_shape=[jax.ShapeDtypeStruct((N_DEV, r, ccols), v.dtype)] + (ex.outs if ex else []),
        in_specs=[_VMEM] + [_ANY] * nci, out_specs=[_VMEM] + [_ANY] * nco,
        scratch_shapes=[pltpu.SemaphoreType.DMA((N_DEV - 1,)), pltpu.SemaphoreType.DMA((N_DEV - 1,)),
                        pltpu.SemaphoreType.DMA(())] + (ex.scratch() if ex else []),
        input_output_aliases={1 + a: 1 + b for a, b in ex.aliases.items()} if ex else {},
        compiler_params=pltpu.CompilerParams(vmem_limit_bytes=VMEM_LIMIT),
    )(v, *(ex.ins if ex else []))
    return (outs[0], list(outs[1:])) if ex else outs[0]


def _rows_to_owners(v, name):
    _, r, ccols = v.shape

    def body(v_ref, out_ref, send_sems, recv_sems, local_sem):
        x, y, c = _pos()
        jm = 2 * x + y
        mine = pltpu.make_async_copy(v_ref.at[2 * jm + c], out_ref.at[jm], local_sem)
        mine.start()
        peers, pjs = _chip_peers(x, y, c)
        sends = []
        for k, peer in enumerate(peers):
            cp = pltpu.make_async_remote_copy(src_ref=v_ref.at[2 * pjs[k] + c], dst_ref=out_ref.at[jm],
                                              send_sem=send_sems.at[k], recv_sem=recv_sems.at[k],
                                              device_id=peer, device_id_type=MESH)
            cp.start()
            sends.append(cp)
        for k, peer in enumerate(peers):
            pltpu.make_async_remote_copy(src_ref=v_ref.at[0], dst_ref=out_ref.at[pjs[k]], send_sem=send_sems.at[k],
                                         recv_sem=recv_sems.at[k], device_id=peer, device_id_type=MESH).wait_recv()
        for cp in sends:
            cp.wait_send()
        mine.wait()

    return pl.pallas_call(
        body, name=name, out_shape=jax.ShapeDtypeStruct((N_CHIPS, r, ccols), v.dtype),
        in_specs=[_VMEM], out_specs=_VMEM,
        scratch_shapes=[pltpu.SemaphoreType.DMA((N_CHIPS - 1,)), pltpu.SemaphoreType.DMA((N_CHIPS - 1,)),
                        pltpu.SemaphoreType.DMA(())],
        compiler_params=pltpu.CompilerParams(vmem_limit_bytes=VMEM_LIMIT),
    )(v)


def _allreduce8(v, name):
    r, ccols = v.shape
    h = r // 2
    q = h // 2

    def body(v_ref, out_ref, whole, part, done, send_sems, recv_sems):
        x, y, c = _pos()
        sib = (x, y, 1 - c)
        mine = pl.ds(pl.multiple_of(c * h, SUBLANES), h)
        theirs = pl.ds(pl.multiple_of((1 - c) * h, SUBLANES), h)
        quarters = [pl.ds(pl.multiple_of(c * h + k * q, SUBLANES), q) for k in range(2)]
        along_x, along_y = (1 - x, y, c), (x, 1 - y, c)

        def exchange(pairs):
            cps = [pltpu.make_async_remote_copy(src_ref=src, dst_ref=dst, send_sem=send_sems.at[k], recv_sem=recv_sems.at[k],
                                                device_id=peer, device_id_type=MESH) for src, dst, k, peer in pairs]
            for cp in cps:
                cp.start()
            for cp in cps:
                cp.wait()

        exchange([(v_ref, whole, 0, sib)])
        out_ref[...] = v_ref[...] + whole[...]
        for stage, peers in enumerate(((along_x, along_y), (along_y, along_x))):
            exchange([(out_ref.at[quarters[k]], part.at[2 * stage + k], 1 + 2 * stage + k, peers[k]) for k in range(2)])
            for k in range(2):
                out_ref[quarters[k], :] = out_ref[quarters[k], :] + part[2 * stage + k]
        exchange([(out_ref.at[mine], done, 5, sib)])
        out_ref[theirs, :] = done[...]

    return pl.pallas_call(
        body, name=name, out_shape=jax.ShapeDtypeStruct((r, ccols), v.dtype),
        in_specs=[_VMEM], out_specs=_VMEM,
        scratch_shapes=[pltpu.VMEM((r, ccols), v.dtype), pltpu.VMEM((4, q, ccols), v.dtype), pltpu.VMEM((h, ccols), v.dtype),
                        pltpu.SemaphoreType.DMA((6,)), pltpu.SemaphoreType.DMA((6,))],
        compiler_params=pltpu.CompilerParams(vmem_limit_bytes=VMEM_LIMIT),
    )(v)


def _chip_peers(x, y, c):
    peers = [(_flip(x, (k >> 1) & 1), _flip(y, k & 1), c) for k in range(1, N_CHIPS)]
    return peers, [2 * p[0] + p[1] for p in peers]


class _Exchange:
    def __init__(self, ins, outs, aliases, n_sems, n_local, start, finish, mid=None, mid_at=1.0, sibling=False, chips=False):
        self.ins, self.outs, self.aliases = list(ins), list(outs), dict(aliases)
        self.n_sems, self.n_local, self.start, self.finish = n_sems, n_local, start, finish
        self.mid, self.mid_at = mid, mid_at
        self.sibling, self.chips = sibling, chips

    def collective_id(self):
        return {(True, False): 1, (False, True): 2, (True, True): 3}[(self.sibling, self.chips)]

    def handshake(self):
        x, y, c = _pos()
        peers = ([(x, y, 1 - c)] if self.sibling else []) + (_chip_peers(x, y, c)[0] if self.chips else [])
        barrier = pltpu.get_barrier_semaphore()
        for peer in peers:
            pl.semaphore_signal(barrier, inc=1, device_id=peer, device_id_type=MESH)
        pl.semaphore_wait(barrier, len(peers))

    def scratch(self):
        return [pltpu.SemaphoreType.DMA((self.n_sems,)), pltpu.SemaphoreType.DMA((self.n_sems,)),
                pltpu.SemaphoreType.DMA((max(self.n_local, 1),))]


class _Sems:
    def __init__(self, send, recv, local, base=0, lbase=0):
        self.send, self.recv, self.loc, self.base, self.lbase = send, recv, local, base, lbase

    def shifted(self, by, lby):
        return _Sems(self.send, self.recv, self.loc, self.base + by, self.lbase + lby)

    def local(self, k):
        return self.loc.at[self.lbase + k]


def _remote(src, dst, sems, k, peer):
    return pltpu.make_async_remote_copy(src_ref=src, dst_ref=dst, send_sem=sems.send.at[sems.base + k],
                                        recv_sem=sems.recv.at[sems.base + k], device_id=peer, device_id_type=MESH)


def _combine(exs):
    ins = [a for ex in exs for a in ex.ins]
    outs = [o for ex in exs for o in ex.outs]
    aliases, spans, ni, no, ns, nloc = {}, [], 0, 0, 0, 0
    for ex in exs:
        aliases.update({ni + a: no + b for a, b in ex.aliases.items()})
        spans.append((ni, no, ns, nloc))
        ni, no, ns, nloc = ni + len(ex.ins), no + len(ex.outs), ns + ex.n_sems, nloc + ex.n_local

    def each(which):
        def run(ins_, outs_, sems):
            for ex, (i0, o0, s0, l0) in zip(exs, spans):
                stage = getattr(ex, which)
                if stage is not None:
                    stage(ins_[i0:i0 + len(ex.ins)], outs_[o0:o0 + len(ex.outs)], sems.shifted(s0, l0))
        return run

    mids = [ex.mid_at for ex in exs if ex.mid is not None]
    return _Exchange(ins, outs, aliases, ns, nloc, each("start"), each("finish"),
                     mid=each("mid") if mids else None, mid_at=max(mids) if mids else 1.0,
                     sibling=any(ex.sibling for ex in exs), chips=any(ex.chips for ex in exs))


def _gather(shards, mid_at=1.0):
    n = len(shards)
    per = N_CHIPS - 1
    halves = [s.reshape(2, s.shape[0] // 2, s.shape[1]) for s in shards]

    def copies(ins, outs, sems):
        x, y, c = _pos()
        jm = 2 * x + y
        sib = (x, y, 1 - c)
        peers, pjs = _chip_peers(x, y, c)
        sends, recvs, passes, passed = [], [], [], []
        for a in range(n):
            own = _remote(ins[a], outs[a].at[jm], sems, 2 * n * per + a, sib)
            sends.append(own)
            passed.append(own)
            for k, peer in enumerate(peers):
                landed, theirs = outs[a].at[pjs[k], c], outs[a].at[pjs[k], 1 - c]
                sends.append(_remote(ins[a].at[c], outs[a].at[jm, c], sems, 2 * (a * per + k), peer))
                recvs.append(_remote(landed, landed, sems, 2 * (a * per + k), peer))
                passes.append(_remote(landed, landed, sems, 2 * (a * per + k) + 1, sib))
                passed.append(_remote(theirs, theirs, sems, 2 * (a * per + k) + 1, sib))
        return sends, recvs, passes, passed

    def start(ins, outs, sems):
        for cp in copies(ins, outs, sems)[0]:
            cp.start()

    def mid(ins, outs, sems):
        _, recvs, passes, _ = copies(ins, outs, sems)
        for got, fwd in zip(recvs, passes):
            got.wait_recv()
            fwd.start()

    def finish(ins, outs, sems):
        sends, _, passes, passed = copies(ins, outs, sems)
        for cp in passed:
            cp.wait_recv()
        for cp in sends + passes:
            cp.wait_send()

    outs = [jax.ShapeDtypeStruct((N_CHIPS,) + h.shape, h.dtype) for h in halves]
    return _Exchange(halves, outs, {}, 2 * n * per + n, 0, start, finish, mid=mid, mid_at=mid_at, sibling=True, chips=True)


def _whole(gathered):
    return [g.reshape(g.shape[0], g.shape[1] * g.shape[2], g.shape[3]) for g in gathered]


def _swap_halves(gs):
    n = len(gs)
    halves = [g.reshape(g.shape[0], 2, g.shape[1] // 2, g.shape[2]) for g in gs]

    def copies(ins, outs, sems):
        x, y, c = _pos()
        sib = (x, y, 1 - c)
        return [_remote(ins[a].at[:, 1 - c], outs[a], sems, a, sib) for a in range(n)]

    def start(ins, outs, sems):
        for cp in copies(ins, outs, sems):
            cp.start()

    def finish(ins, outs, sems):
        for cp in copies(ins, outs, sems):
            cp.wait()

    outs = [jax.ShapeDtypeStruct((g.shape[0], g.shape[1] // 2, g.shape[2]), g.dtype) for g in gs]
    return _Exchange(halves, outs, {}, n, 0, start, finish, sibling=True)


def _chip_exchange(ps):
    n = len(ps)
    per = N_CHIPS - 1

    def copies(ins, outs, sems):
        x, y, c = _pos()
        peers, pjs = _chip_peers(x, y, c)
        return [_remote(ins[a].at[pjs[k]], outs[a].at[k], sems, a * per + k, peer)
                for a in range(n) for k, peer in enumerate(peers)]

    def start(ins, outs, sems):
        for cp in copies(ins, outs, sems):
            cp.start()

    def finish(ins, outs, sems):
        for cp in copies(ins, outs, sems):
            cp.wait()

    outs = [jax.ShapeDtypeStruct((per,) + p.shape[1:], p.dtype) for p in ps]
    return _Exchange(ps, outs, {}, n * per, 0, start, finish, chips=True)


def _sibling_share(fs):
    n = len(fs)

    def copies(outs, sems):
        x, y, c = _pos()
        sib = (x, y, 1 - c)
        sends = [_remote(outs[a].at[c], outs[a].at[c], sems, a, sib) for a in range(n)]
        recvs = [_remote(outs[a].at[1 - c], outs[a].at[1 - c], sems, a, sib) for a in range(n)]
        return sends, recvs

    def start(ins, outs, sems):
        for cp in copies(outs, sems)[0]:
            cp.start()

    def finish(ins, outs, sems):
        sends, recvs = copies(outs, sems)
        for cp in recvs:
            cp.wait_recv()
        for cp in sends:
            cp.wait_send()

    outs = [jax.ShapeDtypeStruct(f.shape, f.dtype) for f in fs]
    return _Exchange(fs, outs, {a: a for a in range(n)}, n, 0, start, finish, sibling=True)


def _run_exchange(ex, name):
    ni, no = len(ex.ins), len(ex.outs)

    def body(*refs):
        ins, outs, sems = refs[:ni], refs[ni:ni + no], _Sems(*refs[ni + no:])
        ex.handshake()
        ex.start(ins, outs, sems)
        if ex.mid is not None:
            ex.mid(ins, outs, sems)
        ex.finish(ins, outs, sems)

    return pl.pallas_call(
        body, name=name, out_shape=ex.outs, in_specs=[_ANY] * ni, out_specs=[_ANY] * no,
        input_output_aliases=ex.aliases, scratch_shapes=ex.scratch(),
        compiler_params=pltpu.CompilerParams(collective_id=ex.collective_id()),
    )(*ex.ins)


def _grid_call(body, name, grid, in_specs, out_specs, out_shape, scratch_shapes, args, exchange=None):
    ni, no = len(in_specs), len(out_specs)
    params = pltpu.CompilerParams(dimension_semantics=("arbitrary",) * len(grid), vmem_limit_bytes=VMEM_LIMIT)
    if exchange is None:
        outs = pl.pallas_call(body, name=name, grid=grid, in_specs=in_specs, out_specs=out_specs, out_shape=out_shape,
                              scratch_shapes=scratch_shapes, compiler_params=params)(*args)
        return list(outs), []
    ex = exchange
    nci, nco = len(ex.ins), len(ex.outs)

    def hosted(*refs):
        cin = refs[ni:ni + nci]
        cout = refs[ni + nci + no:ni + nci + no + nco]
        sems = _Sems(*refs[len(refs) - 3:])
        main = refs[:ni] + refs[ni + nci:ni + nci + no] + refs[ni + nci + no + nco:len(refs) - 3]
        ids = [pl.program_id(a) for a in range(len(grid))]
        first = functools.reduce(lambda p, q: p & q, [i == 0 for i in ids])
        last = functools.reduce(lambda p, q: p & q, [i == g - 1 for i, g in zip(ids, grid)])

        @pl.when(first)
        def _():
            ex.handshake()
            ex.start(cin, cout, sems)

        if ex.mid is not None:
            steps = functools.reduce(lambda p, q: p * q, grid)
            flat = functools.reduce(lambda p, q: p * q[1] + q[0], zip(ids[1:], grid[1:]), ids[0])

            @pl.when(flat == min(steps - 1, int(ex.mid_at * steps)))
            def _():
                ex.mid(cin, cout, sems)

        body(*main)

        @pl.when(last)
        def _():
            ex.finish(cin, cout, sems)

    outs = pl.pallas_call(
        hosted, name=name, grid=grid, in_specs=list(in_specs) + [_ANY] * nci, out_specs=list(out_specs) + [_ANY] * nco,
        out_shape=list(out_shape) + ex.outs, scratch_shapes=list(scratch_shapes) + ex.scratch(),
        input_output_aliases={ni + a: no + b for a, b in ex.aliases.items()},
        compiler_params=pltpu.CompilerParams(dimension_semantics=("arbitrary",) * len(grid), vmem_limit_bytes=VMEM_LIMIT,
                                             collective_id=ex.collective_id()),
    )(*args, *ex.ins)
    return list(outs[:no]), list(outs[no:])


SUM_BLOCK_BYTES = 4 * 1024 * 1024
ADAM_BLOCK_BYTES = 2 * 1024 * 1024


def _row_tile(rows, cols, itemsize, budget):
    best = None
    for t in range(16, rows + 1, 16):
        if rows % t == 0 and t * cols * itemsize <= budget:
            best = t
    return best if best is not None else rows


def _pair_sum(g, r1, cidx, name):
    nj, r, ccols = g.shape
    hr = r // 2
    tr = _row_tile(hr, ccols, 4, SUM_BLOCK_BYTES)
    nt = hr // tr

    def body(c_ref, g_ref, r_ref, o_ref):
        o_ref[...] = (g_ref[...].astype(F32) + r_ref[...].astype(F32)).astype(o_ref.dtype)

    return pl.pallas_call(
        body, name=name, out_shape=jax.ShapeDtypeStruct((nj, hr, ccols), g.dtype),
        grid_spec=pltpu.PrefetchScalarGridSpec(
            num_scalar_prefetch=1, grid=(nj, nt),
            in_specs=[pl.BlockSpec((None, tr, ccols), lambda j, i, c_ref: (j, c_ref[0] * nt + i, 0)),
                      pl.BlockSpec((None, tr, ccols), lambda j, i, c_ref: (j, i, 0))],
            out_specs=pl.BlockSpec((None, tr, ccols), lambda j, i, c_ref: (j, i, 0))),
        compiler_params=pltpu.CompilerParams(dimension_semantics=("arbitrary", "arbitrary"), vmem_limit_bytes=VMEM_LIMIT),
    )(cidx, g, r1)


def _chip_sum(p, r2, idx, name):
    nj, hr, ccols = p.shape
    tr = _row_tile(hr, ccols, 4, SUM_BLOCK_BYTES)
    nt = hr // tr

    def body(i_ref, p_ref, r_ref, o_ref):
        s = p_ref[...].astype(F32)
        for k in range(N_CHIPS - 1):
            s = s + r_ref[k].astype(F32)
        o_ref[...] = s

    return pl.pallas_call(
        body, name=name, out_shape=jax.ShapeDtypeStruct((2, hr, ccols), F32),
        grid_spec=pltpu.PrefetchScalarGridSpec(
            num_scalar_prefetch=1, grid=(nt,),
            in_specs=[pl.BlockSpec((None, tr, ccols), lambda i, i_ref: (i_ref[0], i, 0)),
                      pl.BlockSpec((N_CHIPS - 1, tr, ccols), lambda i, i_ref: (0, i, 0))],
            out_specs=pl.BlockSpec((None, tr, ccols), lambda i, i_ref: (i_ref[1], i, 0))),
        compiler_params=pltpu.CompilerParams(dimension_semantics=("arbitrary",), vmem_limit_bytes=VMEM_LIMIT),
    )(idx, p, r2)


def _adam_math(w, g, m, v):
    m2 = ADAM_B1 * m + (1.0 - ADAM_B1) * g
    v2 = ADAM_B2 * v + (1.0 - ADAM_B2) * (g * g)
    m_hat = m2 / (1.0 - ADAM_B1 ** ADAM_STEP)
    v_hat = v2 / (1.0 - ADAM_B2 ** ADAM_STEP)
    delta = -ADAM_LR * (m_hat / (jnp.sqrt(v_hat) + ADAM_EPS) + ADAM_WD * w)
    return delta, m2, v2


def _adamw_layers(w, m, v, gs, name):
    nl, r, ccols = w.shape
    ng = len(gs)
    tr = _row_tile(r, ccols, 4, ADAM_BLOCK_BYTES)
    nt = r // tr

    def body(w_ref, m_ref, v_ref, *rest):
        g_refs, (go_ref, d_ref, mo_ref, vo_ref) = rest[:ng], rest[ng:]
        l = pl.program_id(0)
        g = g_refs[0][...]
        for k in range(1, ng):
            g = jnp.where(l == k, g_refs[k][...], g)
        delta, m2, v2 = _adam_math(w_ref[...], g, m_ref[...], v_ref[...])
        go_ref[...] = g
        d_ref[...] = delta
        mo_ref[...] = m2
        vo_ref[...] = v2

    big = pl.BlockSpec((None, tr, ccols), lambda l, i: (l, i, 0))

    def gspec(k):
        return pl.BlockSpec((tr, ccols), lambda l, i: (jnp.where(l == k, i, jnp.where(l < k, 0, nt - 1)), 0))

    assert ng == nl
    return _grid_call(body, name, (nl, nt), in_specs=[big, big, big] + [gspec(k) for k in range(ng)],
                      out_specs=[big, big, big, big], out_shape=[jax.ShapeDtypeStruct(w.shape, F32)] * 4,
                      scratch_shapes=[], args=(w, m, v, *gs))[0]


def _adamw_flat(w, g, m, v, name):
    r, ccols = w.shape

    def body(w_ref, g_ref, m_ref, v_ref, d_ref, mo_ref, vo_ref):
        delta, m2, v2 = _adam_math(w_ref[...], g_ref[...], m_ref[...], v_ref[...])
        d_ref[...] = delta
        mo_ref[...] = m2
        vo_ref[...] = v2

    return pl.pallas_call(
        body, name=name, out_shape=[jax.ShapeDtypeStruct((r, ccols), F32)] * 3,
        in_specs=[_VMEM] * 4, out_specs=[_VMEM] * 3,
        compiler_params=pltpu.CompilerParams(vmem_limit_bytes=VMEM_LIMIT),
    )(w, g, m, v)


def _ada_forward(c_all, ada_w, ada_b_cols, name):
    nl, d, ncols = ada_w.shape
    bg = c_all.shape[0]
    tn = 512 if ncols % 512 == 0 else ncols

    def body(c_ref, w_ref, b_ref, o_ref):
        cv = c_ref[...]
        ca = (cv * _sigmoid(cv)).astype(BF16)
        o_ref[...] = jnp.dot(ca, w_ref[...].astype(BF16), preferred_element_type=F32) + b_ref[...]

    return pl.pallas_call(
        body, name=name, out_shape=jax.ShapeDtypeStruct((nl, bg, ncols), F32),
        grid=(nl, ncols // tn),
        in_specs=[pl.BlockSpec((bg, d), lambda l, j: (0, 0)),
                  pl.BlockSpec((None, d, tn), lambda l, j: (l, 0, j)),
                  pl.BlockSpec((None, 1, tn), lambda l, j: (l, 0, j))],
        out_specs=pl.BlockSpec((None, bg, tn), lambda l, j: (l, 0, j)),
        compiler_params=pltpu.CompilerParams(dimension_semantics=("arbitrary", "arbitrary")),
    )(c_all, ada_w, ada_b_cols)


def _ada_update(c_all, dmod_cols, w, m, v, name, exchange=None):
    nl, d, ncols = w.shape
    bg = c_all.shape[0]
    tn = 512 if ncols % 512 == 0 else ncols

    def body(c_ref, dm_ref, w_ref, m_ref, v_ref, go_ref, d_ref, mo_ref, vo_ref):
        cv = c_ref[...]
        ca = (cv * _sigmoid(cv)).astype(BF16)
        g = lax.dot_general(ca, dm_ref[...].astype(BF16), (((0,), (0,)), ((), ())), preferred_element_type=F32)
        delta, m2, v2 = _adam_math(w_ref[...], g, m_ref[...], v_ref[...])
        go_ref[...] = g
        d_ref[...] = delta
        mo_ref[...] = m2
        vo_ref[...] = v2

    big = pl.BlockSpec((None, d, tn), lambda l, j: (l, 0, j))
    return _grid_call(
        body, name, (nl, ncols // tn),
        in_specs=[pl.BlockSpec((bg, d), lambda l, j: (0, 0)),
                  pl.BlockSpec((None, bg, tn), lambda l, j: (l, 0, j)), big, big, big],
        out_specs=[big, big, big, big], out_shape=[jax.ShapeDtypeStruct(w.shape, F32)] * 4,
        scratch_shapes=[], args=(c_all, dmod_cols, w, m, v), exchange=exchange)


def _load_weights(first, pairs, sems):
    @pl.when(first)
    def _():
        cps = [pltpu.make_async_copy(src, dst, sems.at[k]) for k, (src, dst) in enumerate(pairs)]
        for cp in cps:
            cp.start()
        for cp in cps:
            cp.wait()


def _ada_norm(xv, g, sc, sh):
    r = lax.rsqrt(jnp.mean(xv * xv, axis=-1, keepdims=True) + EPS)
    xn = xv * r
    return (xn * g) * (1.0 + sc) + sh, xn, r


def _ada_norm_bwd(dh, xn, r, g, sc):
    d_sh = _colsum(dh)
    d_sc = _colsum(dh * (xn * g))
    dxg = dh * (1.0 + sc)
    d_g = _colsum(dxg * xn)
    gd = dxg * g
    dx = r * (gd - xn * jnp.mean(gd * xn, axis=-1, keepdims=True))
    return dx, d_sh, d_sc, d_g


def _gated_residual_bwd(dxo, o, g_post, gt):
    r = lax.rsqrt(jnp.mean(o * o, axis=-1, keepdims=True) + EPS)
    on = o * r
    d_gt = _colsum(dxo * (on * g_post))
    dy = dxo * (1.0 + gt)
    d_gp = _colsum(dy * on)
    gd = dy * g_post
    do = r * (gd - on * jnp.mean(gd * on, axis=-1, keepdims=True))
    return do, d_gt, d_gp


def _seq_positions(i, tm, width):
    return i * tm + lax.broadcasted_iota(jnp.int32, (tm, width), 0)


def _fill_phases(ext, phases):
    rows = ext.shape[0]
    ev = ext[...]
    for r in range(1, SUBLANES):
        phases[r - 1] = pltpu.roll(ev, rows - r, axis=0)


def _shifted_rows(ext, phases, offset, n):
    q, r = divmod(offset, SUBLANES)
    if r == 0:
        return ext[pl.ds(q * SUBLANES, n), :]
    return phases[r - 1, pl.ds(q * SUBLANES, n), :]


def _rows_before(halo, cur, shift):
    e = jnp.concatenate([halo, cur], axis=0)
    return pltpu.roll(e, shift, axis=0)[halo.shape[0]:, :]


def _rows_after(cur, halo, shift):
    e = jnp.concatenate([cur, halo], axis=0)
    return pltpu.roll(e, e.shape[0] - shift, axis=0)[:cur.shape[0], :]


def _mixer_forward(x, mod, vec_d, vec_c, cw, pw, win_g, wout_g, taps, tm, name, exchange=None):
    nb, s, d = x.shape
    n = s // tm
    nj, _, dcol = win_g.shape
    din = nj * dcol
    dc = vec_c.shape[-1]
    dpool = din - 2 * dc
    dmix = dc + dpool
    ro = wout_g.shape[1]
    ngrp = dpool // LANES

    def body(x_ref, mod_ref, vd_ref, vc_ref, cw_ref, pw_ref, win_hbm, wout_hbm,
             xo_ref, h_ref, u_ref, ac_ref, dp_ref, z_ref, o_ref,
             win_v, wout_v, ext_a, ext_p, phases, sems):
        b, i = pl.program_id(0), pl.program_id(1)
        pairs = [(win_hbm.at[j], win_v.at[:, pl.ds(j * dcol, dcol)]) for j in range(nj)]
        pairs += [(wout_hbm.at[j], wout_v.at[pl.ds(j * ro, ro), :]) for j in range(nj)]
        _load_weights((b == 0) & (i == 0), pairs, sems)

        xv = x_ref[...]
        h, _, _ = _ada_norm(xv, vd_ref[0:1, :], mod_ref[1:2, :], mod_ref[0:1, :])
        hb = h.astype(BF16)
        h_ref[...] = hb
        u = jnp.dot(hb, win_v[...], preferred_element_type=F32)
        u_ref[...] = u.astype(BF16)
        ag = u[:, :dc] * _sigmoid(u[:, dc:2 * dc])
        up = u[:, 2 * dc:]

        @pl.when(i == 0)
        def _():
            ext_a[0:HALO, :] = jnp.zeros((HALO, dc), F32)
            ext_p[0:HALO, :] = jnp.zeros((HALO, dpool), F32)

        @pl.when(i > 0)
        def _():
            ext_a[0:HALO, :] = ext_a[tm:tm + HALO, :]
            ext_p[0:HALO, :] = ext_p[tm:tm + HALO, :]

        ext_a[HALO:HALO + tm, :] = ag
        ext_p[HALO:HALO + tm, :] = up

        acc = jnp.broadcast_to(vc_ref[0:1, :], (tm, dc))
        _fill_phases(ext_a, phases)
        for k in range(taps):
            acc = acc + cw_ref[k:k + 1, :] * _shifted_rows(ext_a, phases, HALO - (taps - 1) + k, tm)
        ac_ref[...] = acc.astype(BF16)
        mu = jnp.mean(acc, axis=-1, keepdims=True)
        xc = acc - mu
        var = jnp.mean(xc * xc, axis=-1, keepdims=True)
        al = (xc * lax.rsqrt(var + EPS)) * vc_ref[1:2, :] + vc_ref[2:3, :]
        a = al * _sigmoid(al)

        pos = _seq_positions(i, tm, LANES)
        parts = [a.astype(BF16)]
        for g in range(ngrp):
            w = POOL_WINDOWS[g]
            cols = slice(g * LANES, (g + 1) * LANES)
            sw = ext_p[:, cols]
            step = 1
            while step < w:
                sw = sw + pltpu.roll(sw, step, axis=0)
                step *= 2
            cnt = jnp.minimum(pos + 1, w).astype(F32)
            dg = (sw[HALO:, :] / cnt - up[:, cols]).astype(BF16)
            dp_ref[:, cols] = dg
            q = jnp.dot(dg, pw_ref[g], preferred_element_type=F32)
            parts.append((q * vc_ref[3:4, cols]).astype(BF16))
        z = jnp.concatenate(parts, axis=-1)
        z_ref[...] = z
        o = jnp.dot(z, wout_v[...], preferred_element_type=F32)
        o_ref[...] = o
        r2 = lax.rsqrt(jnp.mean(o * o, axis=-1, keepdims=True) + EPS)
        xo_ref[...] = xv + (1.0 + mod_ref[2:3, :]) * ((o * r2) * vd_ref[1:2, :])

    def tile(width):
        return pl.BlockSpec((None, tm, width), lambda b, i: (b, i, 0))

    return _grid_call(
        body, name, (nb, n),
        in_specs=[tile(d), pl.BlockSpec((None, 8, d), lambda b, i: (b, 0, 0)), _full(vec_d.shape), _full(vec_c.shape),
                  _full(cw.shape), _full(pw.shape), _ANY, _ANY],
        out_specs=[tile(d), tile(d), tile(din), tile(dc), tile(dpool), tile(dmix), tile(d)],
        out_shape=[jax.ShapeDtypeStruct((nb, s, d), F32), jax.ShapeDtypeStruct((nb, s, d), BF16),
                   jax.ShapeDtypeStruct((nb, s, din), BF16), jax.ShapeDtypeStruct((nb, s, dc), BF16),
                   jax.ShapeDtypeStruct((nb, s, dpool), BF16), jax.ShapeDtypeStruct((nb, s, dmix), BF16),
                   jax.ShapeDtypeStruct((nb, s, d), F32)],
        scratch_shapes=[pltpu.VMEM((d, din), BF16), pltpu.VMEM((dmix, d), BF16),
                        pltpu.VMEM((HALO + tm, dc), F32), pltpu.VMEM((HALO + tm, dpool), F32),
                        pltpu.VMEM((SUBLANES - 1, HALO + tm, dc), F32), pltpu.SemaphoreType.DMA((2 * nj,))],
        args=(x, mod, vec_d, vec_c, cw, pw, win_g, wout_g), exchange=exchange)


def _mixer_backward(dxo, x, o, u, ac, dpl, mod, vec_d, vec_c, cw, pw, win_g, wout_g, taps, tm, name, exchange=None):
    nb, s, d = x.shape
    n = s // tm
    nj, _, dcol = win_g.shape
    din = nj * dcol
    dc = vec_c.shape[-1]
    dpool = din - 2 * dc
    dmix = dc + dpool
    ro = wout_g.shape[1]
    ngrp = dpool // LANES
    rext = tm + HALO

    def body(dxo_ref, x_ref, o_ref, u_ref, ac_ref, dp_ref, mod_ref, vd_ref, vc_ref, cw_ref, pw_ref, win_hbm, wout_hbm,
             dx_ref, du_ref, dob_ref, rowd_ref, rowb_ref, rowc_ref, dcw_ref, dpw_ref,
             win_v, wout_v, ext_a, ext_p, phases, sems):
        b, i = pl.program_id(0), pl.program_id(1)
        first = (b == 0) & (i == 0)
        pairs = [(win_hbm.at[j], win_v.at[:, pl.ds(j * dcol, dcol)]) for j in range(nj)]
        pairs += [(wout_hbm.at[j], wout_v.at[pl.ds(j * ro, ro), :]) for j in range(nj)]
        _load_weights(first, pairs, sems)

        @pl.when(first)
        def _():
            rowd_ref[...] = jnp.zeros_like(rowd_ref)
            rowc_ref[...] = jnp.zeros_like(rowc_ref)
            dcw_ref[...] = jnp.zeros_like(dcw_ref)
            dpw_ref[...] = jnp.zeros_like(dpw_ref)

        @pl.when(i == 0)
        def _():
            rowb_ref[...] = jnp.zeros_like(rowb_ref)
            ext_a[tm:rext, :] = jnp.zeros((HALO, dc), F32)
            ext_p[tm:rext, :] = jnp.zeros((HALO, dpool), F32)

        @pl.when(i > 0)
        def _():
            ext_a[tm:rext, :] = ext_a[0:HALO, :]
            ext_p[tm:rext, :] = ext_p[0:HALO, :]

        g_pre, g_post = vd_ref[0:1, :], vd_ref[1:2, :]
        sh, sc, gt = mod_ref[0:1, :], mod_ref[1:2, :], mod_ref[2:3, :]
        do, d_gt, d_gp = _gated_residual_bwd(dxo_ref[...], o_ref[...], g_post, gt)
        dob = do.astype(BF16)
        dob_ref[...] = dob
        dz = lax.dot_general(dob, wout_v[...], (((1,), (1,)), ((), ())), preferred_element_type=F32)

        acv = ac_ref[...].astype(F32)
        mu = jnp.mean(acv, axis=-1, keepdims=True)
        xc = acv - mu
        rstd = lax.rsqrt(jnp.mean(xc * xc, axis=-1, keepdims=True) + EPS)
        an = xc * rstd
        lg = vc_ref[1:2, :]
        al = an * lg + vc_ref[2:3, :]
        sg = _sigmoid(al)
        dal = dz[:, :dc] * (sg * (1.0 + al * (1.0 - sg)))
        d_lg = _colsum(dal * an)
        d_lb = _colsum(dal)
        dan = dal * lg
        dac = rstd * (dan - jnp.mean(dan, axis=-1, keepdims=True) - an * jnp.mean(dan * an, axis=-1, keepdims=True))
        d_cb = _colsum(dac)
        ext_a[0:tm, :] = dac
        uv = u_ref[:, 0:dc].astype(F32)
        sgu = _sigmoid(u_ref[:, dc:2 * dc].astype(F32))
        ag = uv * sgu
        dag = jnp.zeros((tm, dc), F32)
        _fill_phases(ext_a, phases)
        for k in range(taps):
            sl = _shifted_rows(ext_a, phases, taps - 1 - k, tm)
            dag = dag + cw_ref[k:k + 1, :] * sl
            dcw_ref[k:k + 1, :] += _colsum(ag * sl)
        du_ref[:, 0:dc] = (dag * sgu).astype(BF16)
        du_ref[:, dc:2 * dc] = (dag * uv * (sgu * (1.0 - sgu))).astype(BF16)

        pos = _seq_positions(n - 1 - i, tm, LANES)
        d_ps = []
        for g in range(ngrp):
            w = POOL_WINDOWS[g]
            cols = slice(g * LANES, (g + 1) * LANES)
            gcols = slice(dc + g * LANES, dc + (g + 1) * LANES)
            dgb = dp_ref[:, cols]
            q = jnp.dot(dgb, pw_ref[g], preferred_element_type=F32)
            dpg = dz[:, gcols]
            d_ps.append(_colsum(dpg * q))
            dq = (dpg * vc_ref[3:4, cols]).astype(BF16)
            dpw_ref[g] += lax.dot_general(dgb, dq, (((0,), (0,)), ((), ())), preferred_element_type=F32)
            dd = lax.dot_general(dq, pw_ref[g], (((1,), (1,)), ((), ())), preferred_element_type=F32)
            cnt = jnp.minimum(pos + 1, w).astype(F32)
            ext_p[0:tm, cols] = dd / cnt
            sw = ext_p[:, cols]
            step = 1
            while step < w:
                sw = sw + pltpu.roll(sw, rext - step, axis=0)
                step *= 2
            du_ref[:, 2 * dc + g * LANES:2 * dc + (g + 1) * LANES] = (sw[0:tm, :] - dd).astype(BF16)
        rowc_ref[0:1, :] += d_cb
        rowc_ref[1:2, :] += d_lg
        rowc_ref[2:3, :] += d_lb
        rowc_ref[3:4, :] += jnp.concatenate(d_ps, axis=-1)

        dh = lax.dot_general(du_ref[...], win_v[...], (((1,), (1,)), ((), ())), preferred_element_type=F32)
        _, xn, r1 = _ada_norm(x_ref[...], g_pre, sc, sh)
        dxb, d_sh, d_sc, d_g = _ada_norm_bwd(dh, xn, r1, g_pre, sc)
        dx_ref[...] = dxo_ref[...] + dxb
        rowd_ref[0:1, :] += d_g
        rowd_ref[1:2, :] += d_gp
        rowb_ref[0:1, :] += d_sh
        rowb_ref[1:2, :] += d_sc
        rowb_ref[2:3, :] += d_gt

    def tile(width):
        return pl.BlockSpec((None, tm, width), lambda b, i: (b, n - 1 - i, 0))

    return _grid_call(
        body, name, (nb, n),
        in_specs=[tile(d), tile(d), tile(d), tile(din), tile(dc), tile(dpool),
                  pl.BlockSpec((None, 8, d), lambda b, i: (b, 0, 0)), _full(vec_d.shape), _full(vec_c.shape),
                  _full(cw.shape), _full(pw.shape), _ANY, _ANY],
        out_specs=[tile(d), tile(din), tile(d), _full((8, d)), pl.BlockSpec((None, 8, d), lambda b, i: (b, 0, 0)),
                   _full((8, dc)), _full((HALO, dc)), _full(pw.shape)],
        out_shape=[jax.ShapeDtypeStruct((nb, s, d), F32), jax.ShapeDtypeStruct((nb, s, din), BF16),
                   jax.ShapeDtypeStruct((nb, s, d), BF16), jax.ShapeDtypeStruct((8, d), F32),
                   jax.ShapeDtypeStruct((nb, 8, d), F32), jax.ShapeDtypeStruct((8, dc), F32),
                   jax.ShapeDtypeStruct((HALO, dc), F32), jax.ShapeDtypeStruct(pw.shape, F32)],
        scratch_shapes=[pltpu.VMEM((d, din), BF16), pltpu.VMEM((dmix, d), BF16),
                        pltpu.VMEM((rext, dc), F32), pltpu.VMEM((rext, dpool), F32),
                        pltpu.VMEM((SUBLANES - 1, rext, dc), F32), pltpu.SemaphoreType.DMA((2 * nj,))],
        args=(dxo, x, o, u, ac, dpl, mod, vec_d, vec_c, cw, pw, win_g, wout_g), exchange=exchange)


def _ffn_forward(x, mod, vec_d, fw, wup_g, wdn_g, tm, name, exchange=None):
    nb, s, d = x.shape
    n = s // tm
    nj, _, ucol = wup_g.shape
    f2 = nj * ucol
    dff = f2 // 2
    rd = wdn_g.shape[1]
    nq = nj // 2
    cs = dff // nq

    def body(x_ref, mod_ref, vd_ref, fw_ref, wup_hbm, wdn_hbm,
             xo_ref, h_ref, u_ref, uc_ref, hid_ref, o_ref,
             wup_v, wdn_v, prev_u, sems):
        b, i = pl.program_id(0), pl.program_id(1)
        pairs = [(wup_hbm.at[j], wup_v.at[:, pl.ds(j * ucol, ucol)]) for j in range(nj)]
        pairs += [(wdn_hbm.at[j], wdn_v.at[pl.ds(j * rd, rd), :]) for j in range(nj)]
        _load_weights((b == 0) & (i == 0), pairs, sems)

        @pl.when(i == 0)
        def _():
            prev_u[...] = jnp.zeros_like(prev_u)

        xv = x_ref[...]
        h, _, _ = _ada_norm(xv, vd_ref[2:3, :], mod_ref[4:5, :], mod_ref[3:4, :])
        hb = h.astype(BF16)
        h_ref[...] = hb

        def conv(cols):
            uc = jnp.dot(hb, wup_v[:, cols], preferred_element_type=F32)
            u_ref[:, cols] = uc.astype(BF16)
            before = prev_u[:, cols]
            prev_u[:, cols] = uc[tm - FHALO:, :]
            out = (fw_ref[3:4, cols] + fw_ref[2:3, cols] * uc + fw_ref[1:2, cols] * _rows_before(before, uc, 1)
                   + fw_ref[0:1, cols] * _rows_before(before, uc, 2))
            uc_ref[:, cols] = out.astype(BF16)
            return out

        o = jnp.zeros((tm, d), F32)
        for q in range(nq):
            val = conv(pl.ds(q * cs, cs))
            gate = conv(pl.ds(dff + q * cs, cs))
            hid = ((gate * _sigmoid(gate)) * val).astype(BF16)
            hid_ref[:, pl.ds(q * cs, cs)] = hid
            o = o + jnp.dot(hid, wdn_v[pl.ds(q * cs, cs), :], preferred_element_type=F32)
        o_ref[...] = o
        r2 = lax.rsqrt(jnp.mean(o * o, axis=-1, keepdims=True) + EPS)
        xo_ref[...] = xv + (1.0 + mod_ref[5:6, :]) * ((o * r2) * vd_ref[3:4, :])

    def tile(width):
        return pl.BlockSpec((None, tm, width), lambda b, i: (b, i, 0))

    return _grid_call(
        body, name, (nb, n),
        in_specs=[tile(d), pl.BlockSpec((None, 8, d), lambda b, i: (b, 0, 0)), _full(vec_d.shape), _full(fw.shape),
                  _ANY, _ANY],
        out_specs=[tile(d), tile(d), tile(f2), tile(f2), tile(dff), tile(d)],
        out_shape=[jax.ShapeDtypeStruct((nb, s, d), F32), jax.ShapeDtypeStruct((nb, s, d), BF16),
                   jax.ShapeDtypeStruct((nb, s, f2), BF16), jax.ShapeDtypeStruct((nb, s, f2), BF16),
                   jax.ShapeDtypeStruct((nb, s, dff), BF16), jax.ShapeDtypeStruct((nb, s, d), F32)],
        scratch_shapes=[pltpu.VMEM((d, f2), BF16), pltpu.VMEM((dff, d), BF16),
                        pltpu.VMEM((FHALO, f2), F32), pltpu.SemaphoreType.DMA((2 * nj,))],
        args=(x, mod, vec_d, fw, wup_g, wdn_g), exchange=exchange)


def _ffn_backward(dxo, x, o, u, uc, mod, vec_d, fw, wup_g, wdn_g, tm, name, exchange=None):
    nb, s, d = x.shape
    n = s // tm
    nj, _, ucol = wup_g.shape
    f2 = nj * ucol
    dff = f2 // 2
    rd = wdn_g.shape[1]
    nq = nj // 2
    cs = dff // nq

    def body(dxo_ref, x_ref, o_ref, u_ref, uc_ref, mod_ref, vd_ref, fw_ref, wup_hbm, wdn_hbm,
             dx_ref, du_ref, dob_ref, rowd_ref, rowb_ref, dfw_ref,
             wup_v, wdn_v, next_d, sems):
        b, i = pl.program_id(0), pl.program_id(1)
        first = (b == 0) & (i == 0)
        pairs = [(wup_hbm.at[j], wup_v.at[:, pl.ds(j * ucol, ucol)]) for j in range(nj)]
        pairs += [(wdn_hbm.at[j], wdn_v.at[pl.ds(j * rd, rd), :]) for j in range(nj)]
        _load_weights(first, pairs, sems)

        @pl.when(first)
        def _():
            rowd_ref[...] = jnp.zeros_like(rowd_ref)
            dfw_ref[...] = jnp.zeros_like(dfw_ref)

        @pl.when(i == 0)
        def _():
            rowb_ref[...] = jnp.zeros_like(rowb_ref)
            next_d[...] = jnp.zeros_like(next_d)

        g_pre, g_post = vd_ref[2:3, :], vd_ref[3:4, :]
        sh, sc, gt = mod_ref[3:4, :], mod_ref[4:5, :], mod_ref[5:6, :]
        do, d_gt, d_gp = _gated_residual_bwd(dxo_ref[...], o_ref[...], g_post, gt)
        dob = do.astype(BF16)
        dob_ref[...] = dob

        def conv_bwd(cols, duc):
            uc = u_ref[:, cols].astype(F32)
            after = next_d[:, cols]
            next_d[:, cols] = duc[0:FHALO, :]
            d1 = _rows_after(duc, after, 1)
            d2 = _rows_after(duc, after, 2)
            dfw_ref[3:4, cols] += _colsum(duc)
            dfw_ref[2:3, cols] += _colsum(uc * duc)
            dfw_ref[1:2, cols] += _colsum(uc * d1)
            dfw_ref[0:1, cols] += _colsum(uc * d2)
            ob = (fw_ref[2:3, cols] * duc + fw_ref[1:2, cols] * d1 + fw_ref[0:1, cols] * d2).astype(BF16)
            du_ref[:, cols] = ob
            return lax.dot_general(ob, wup_v[:, cols], (((1,), (1,)), ((), ())), preferred_element_type=F32)

        dh = jnp.zeros((tm, d), F32)
        for q in range(nq):
            vcols = pl.ds(q * cs, cs)
            gcols = pl.ds(dff + q * cs, cs)
            dhid = lax.dot_general(dob, wdn_v[vcols, :], (((1,), (1,)), ((), ())), preferred_element_type=F32)
            val = uc_ref[:, vcols].astype(F32)
            gate = uc_ref[:, gcols].astype(F32)
            sg = _sigmoid(gate)
            act = gate * sg
            dval = dhid * act
            dgate = (dhid * val) * (sg + act * (1.0 - sg))
            dh = dh + conv_bwd(vcols, dval)
            dh = dh + conv_bwd(gcols, dgate)

        _, xn, r1 = _ada_norm(x_ref[...], g_pre, sc, sh)
        dxb, d_sh, d_sc, d_g = _ada_norm_bwd(dh, xn, r1, g_pre, sc)
        dx_ref[...] = dxo_ref[...] + dxb
        rowd_ref[2:3, :] += d_g
        rowd_ref[3:4, :] += d_gp
        rowb_ref[3:4, :] += d_sh
        rowb_ref[4:5, :] += d_sc
        rowb_ref[5:6, :] += d_gt

    def tile(width):
        return pl.BlockSpec((None, tm, width), lambda b, i: (b, n - 1 - i, 0))

    return _grid_call(
        body, name, (nb, n),
        in_specs=[tile(d), tile(d), tile(d), tile(f2), tile(f2), pl.BlockSpec((None, 8, d), lambda b, i: (b, 0, 0)),
                  _full(vec_d.shape), _full(fw.shape), _ANY, _ANY],
        out_specs=[tile(d), tile(f2), tile(d), _full((8, d)), pl.BlockSpec((None, 8, d), lambda b, i: (b, 0, 0)),
                   _full(fw.shape)],
        out_shape=[jax.ShapeDtypeStruct((nb, s, d), F32), jax.ShapeDtypeStruct((nb, s, f2), BF16),
                   jax.ShapeDtypeStruct((nb, s, d), BF16), jax.ShapeDtypeStruct((8, d), F32),
                   jax.ShapeDtypeStruct((nb, 8, d), F32), jax.ShapeDtypeStruct(fw.shape, F32)],
        scratch_shapes=[pltpu.VMEM((d, f2), BF16), pltpu.VMEM((dff, d), BF16),
                        pltpu.VMEM((FHALO, f2), F32), pltpu.SemaphoreType.DMA((2 * nj,))],
        args=(dxo, x, o, u, uc, mod, vec_d, fw, wup_g, wdn_g), exchange=exchange)


def _weight_grad(a, b, nblk, split, tt, name, exchange=None):
    t, ka = a.shape
    nb_ = b.shape[1]
    nk = t // tt
    if split == "cols":
        wa, wb, grid = ka, nb_ // nblk, (1, nk)
        a_spec = pl.BlockSpec((tt, ka), lambda j, k: (k, 0))
        b_spec = pl.BlockSpec((tt, nb_), lambda j, k: (k, 0))
        o_spec = pl.BlockSpec((nblk, wa, wb), lambda j, k: (0, 0, 0))
        acc_shape = (ka, nb_)
    elif split == "b":
        wa, wb, grid = ka, nb_ // nblk, (nblk, nk)
        a_spec = pl.BlockSpec((tt, wa), lambda j, k: (k, 0))
        b_spec = pl.BlockSpec((tt, wb), lambda j, k: (k, j))
        o_spec = pl.BlockSpec((None, wa, wb), lambda j, k: (j, 0, 0))
        acc_shape = (wa, wb)
    else:
        wa, wb, grid = ka // nblk, nb_, (nblk, nk)
        a_spec = pl.BlockSpec((tt, wa), lambda j, k: (k, j))
        b_spec = pl.BlockSpec((tt, wb), lambda j, k: (k, 0))
        o_spec = pl.BlockSpec((None, wa, wb), lambda j, k: (j, 0, 0))
        acc_shape = (wa, wb)

    def body(a_ref, b_ref, o_ref, acc):
        k = pl.program_id(1)
        prod = lax.dot_general(a_ref[...], b_ref[...], (((0,), (0,)), ((), ())), preferred_element_type=F32)

        @pl.when(k == 0)
        def _():
            acc[...] = prod

        @pl.when(k > 0)
        def _():
            acc[...] += prod

        @pl.when(k == nk - 1)
        def _():
            if split == "cols":
                for j in range(nblk):
                    o_ref[j] = acc[:, j * wb:(j + 1) * wb].astype(o_ref.dtype)
            else:
                o_ref[...] = acc[...].astype(o_ref.dtype)

    outs, exo = _grid_call(body, name, grid, in_specs=[a_spec, b_spec], out_specs=[o_spec],
                           out_shape=[jax.ShapeDtypeStruct((nblk, wa, wb), BF16)],
                           scratch_shapes=[pltpu.VMEM(acc_shape, F32)], args=(a, b), exchange=exchange)
    return outs[0], exo


def _loss_grad(y, tgt, tm, name):
    nb, s, d = y.shape
    n = s // tm

    def body(y_ref, t_ref, dy_ref, sq_ref):
        @pl.when((pl.program_id(0) == 0) & (pl.program_id(1) == 0))
        def _():
            sq_ref[...] = jnp.zeros_like(sq_ref)

        e = y_ref[...] - t_ref[...]
        dy_ref[...] = e * (1.0 / d)
        sq_ref[0:1, :] += _colsum(e * e)

    tile = pl.BlockSpec((None, tm, d), lambda b, i: (b, i, 0))
    return pl.pallas_call(
        body, name=name, out_shape=[jax.ShapeDtypeStruct((nb, s, d), F32), jax.ShapeDtypeStruct((8, d), F32)],
        grid=(nb, n), in_specs=[tile, tile], out_specs=[tile, _full((8, d))],
        compiler_params=pltpu.CompilerParams(dimension_semantics=("arbitrary", "arbitrary")),
    )(y, tgt)


def _rows128(a):
    return a.reshape(-1, LANES)


class _ReduceScatter:
    def __init__(self, gs, cidx, idx, tag):
        self.gs, self.cidx, self.idx, self.tag = gs, cidx, idx, tag

    def swap(self):
        return _swap_halves(self.gs)

    def after_swap(self, r1):
        self.ps = [_pair_sum(g, r, self.cidx, name=f"rs_pair_{self.tag}_{a}") for a, (g, r) in enumerate(zip(self.gs, r1))]

    def chips(self):
        return _chip_exchange(self.ps)

    def after_chips(self, r2):
        self.fh = [_chip_sum(p, r, self.idx, name=f"rs_sum_{self.tag}_{a}") for a, (p, r) in enumerate(zip(self.ps, r2))]

    def share(self):
        return _sibling_share(self.fh)

    @staticmethod
    def result(fs):
        return [f.reshape(f.shape[0] * f.shape[1], f.shape[2]) for f in fs]


def kernel(x, c, ada_w, ada_b, pre_mix_g, post_mix_g, w_in, conv_w, conv_b, conv_ln_g, conv_ln_b, pool_w, pool_scale, w_out, pre_ffn_g, post_ffn_g, ffn_up, ffn_conv_w, ffn_conv_b, ffn_down, loss_target, m_ada_w, m_ada_b, m_pre_mix_g, m_post_mix_g, m_w_in, m_conv_w, m_conv_b, m_conv_ln_g, m_conv_ln_b, m_pool_w, m_pool_scale, m_w_out, m_pre_ffn_g, m_post_ffn_g, m_ffn_up, m_ffn_conv_w, m_ffn_conv_b, m_ffn_down, v_ada_w, v_ada_b, v_pre_mix_g, v_post_mix_g, v_w_in, v_conv_w, v_conv_b, v_conv_ln_g, v_conv_ln_b, v_pool_w, v_pool_scale, v_w_out, v_pre_ffn_g, v_post_ffn_g, v_ffn_up, v_ffn_conv_w, v_ffn_conv_b, v_ffn_down):
    nb, s, d = x.shape
    nl = w_in.shape[0]
    taps = conv_w.shape[1]
    ccol = conv_w.shape[2]
    dc = conv_b.shape[1]
    fcol = ffn_conv_w.shape[2]
    f2 = ffn_conv_b.shape[1]
    nmod = ada_b.shape[1] // d
    acol = ada_w.shape[2]
    tm = min(MLP_TILE_ROWS, s)
    tm_mix = min(MIXER_TILE_ROWS, s)
    tt = min(GRAD_CHUNK_ROWS, (nb * s) // 2)

    xi, yi, ci = _pos()
    jm = 2 * xi + yi
    cidx = jnp.reshape(ci, (1,)).astype(jnp.int32)
    idx = jnp.stack([jm, ci]).astype(jnp.int32)

    win_b, wout_b, wup_b, wdn_b = (w.astype(BF16) for w in (w_in, w_out, ffn_up, ffn_down))

    def others(l):
        return [win_b[l], wout_b[l], wdn_b[l]]

    n_cw, n_fw, n_c = nl * taps * ccol, nl * 3 * fcol, nb * d
    packed = jnp.concatenate([conv_w.reshape(-1), ffn_conv_w.reshape(-1), c.reshape(-1)])
    got, first_weights = _gather8(_rows128(packed), name="gather_small", exchange=_gather(others(0)))
    got = got.reshape(N_DEV, -1)
    chips = got[0::2]
    cw_full = chips[:, :n_cw].reshape(N_CHIPS, nl, taps, ccol).transpose(1, 2, 0, 3).reshape(nl, taps, dc)
    fw_full = chips[:, n_cw:n_cw + n_fw].reshape(N_CHIPS, nl, 3, fcol).transpose(1, 2, 0, 3).reshape(nl, 3, f2)
    c_all = got[:, n_cw + n_fw:].reshape(N_DEV * nb, d)

    ada_b_cols = lax.dynamic_slice_in_dim(ada_b, jm * acol, acol, axis=1).reshape(nl, 1, acol)
    mod_cols = _ada_forward(c_all, ada_w, ada_b_cols, name="ada_forward")
    by_owner = mod_cols.reshape(nl, N_DEV, nb, acol).transpose(1, 0, 2, 3).reshape(N_DEV, -1, LANES)
    mod_own = _rows_to_owners(by_owner, name="mod_to_owners").reshape(N_CHIPS, nl, nb, acol)
    mod_own = mod_own.transpose(1, 2, 0, 3).reshape(nl, nb, nmod, d)
    mod_own = jnp.pad(mod_own, ((0, 0), (0, 0), (0, 8 - nmod), (0, 0)))

    vec_d = jnp.stack([pre_mix_g, post_mix_g, pre_ffn_g, post_ffn_g], axis=1)
    vec_c = jnp.stack([conv_b, conv_ln_g, conv_ln_b, pool_scale], axis=1)
    cw_pad = jnp.pad(cw_full, ((0, 0), (0, HALO - taps), (0, 0)))
    fw_rows = jnp.concatenate([fw_full, ffn_conv_b[:, None, :], jnp.zeros((nl, 4, f2), F32)], axis=1)
    pw_b = pool_w.astype(BF16)

    win_g, wout_g, wdn_g = _whole(first_weights)
    saved = []
    xs = x
    for l in range(nl):
        (x1, h1, u1, ac1, dp1, z1, o1), got = _mixer_forward(
            xs, mod_own[l], vec_d[l], vec_c[l], cw_pad[l], pw_b[l], win_g, wout_g, taps, tm_mix, name=f"mixer_fwd_{l}",
            exchange=_gather([wup_b[l]], mid_at=0.9))
        wup_g, = _whole(got)
        (x2, h2, u2, uc2, hid2, o2), nxt = _ffn_forward(
            x1, mod_own[l], vec_d[l], fw_rows[l], wup_g, wdn_g, tm, name=f"ffn_fwd_{l}",
            exchange=_gather(others(l + 1), mid_at=0.6) if l + 1 < nl else None)
        saved.append((xs, h1, u1, ac1, dp1, z1, o1, x1, h2, u2, uc2, hid2, o2, win_g, wout_g, wup_g, wdn_g))
        if l + 1 < nl:
            win_g, wout_g, wdn_g = _whole(nxt)
        xs = x2

    dx, sq = _loss_grad(xs, loss_target, tm_mix, name="loss_grad")
    loss = lax.psum(0.5 * jnp.sum(sq) / d, ("x", "y", "c"))

    flat = lambda a: a.reshape(nb * s, a.shape[-1])
    small = [None] * nl
    big_mlp, big_mix = [None] * nl, [None] * nl
    mlp = mix = None
    for l in reversed(range(nl)):
        x0, h1, u1, ac1, dp1, z1, o1, x1, h2, u2, uc2, hid2, o2, win_g, wout_g, wup_g, wdn_g = saved[l]
        (dx, du2, do2, rowd2, rowb2, dfw), got = _ffn_backward(
            dx, x1, o2, u2, uc2, mod_own[l], vec_d[l], fw_rows[l], wup_g, wdn_g, tm, name=f"ffn_bwd_{l}",
            exchange=_combine([mlp.chips(), mix.swap()]) if mlp else None)
        if mlp:
            mlp.after_chips(got[:2])
            mix.after_swap(got[2:])
        g_up, got = _weight_grad(flat(h2), flat(du2), N_CHIPS, "b", tt, name=f"grad_ffn_up_{l}",
                                 exchange=_combine([mlp.share(), mix.chips()]) if mlp else None)
        if mlp:
            big_mlp[l + 1] = mlp.result(got[:2])
            mix.after_chips(got[2:])
        g_dn, _ = _weight_grad(flat(hid2), flat(do2), 2, "a", tt, name=f"grad_ffn_down_{l}")
        mlp_above, mlp = mlp, _ReduceScatter([g_up, g_dn.reshape(N_CHIPS, -1, d)], cidx, idx, f"mlp_{l}")
        if l == 0:
            mlp.after_swap(_run_exchange(mlp.swap(), name="rs_swap_mlp_0"))
        first = mlp.swap() if l > 0 else mlp.chips()
        (dx, du1, do1, rowd1, rowb1, rowc, dcw, dpw), got = _mixer_backward(
            dx, x0, o1, u1, ac1, dp1, mod_own[l], vec_d[l], vec_c[l], cw_pad[l], pw_b[l], win_g, wout_g, taps, tm_mix,
            name=f"mixer_bwd_{l}", exchange=_combine([first, mix.share()]) if mlp_above else first)
        if l > 0:
            mlp.after_swap(got[:2])
        else:
            mlp.after_chips(got[:2])
        if mlp_above:
            big_mix[l + 1] = mix.result(got[2:])
        g_in, got = _weight_grad(flat(h1), flat(du1), N_CHIPS, "cols", tt, name=f"grad_w_in_{l}",
                                 exchange=mlp.share() if l == 0 else None)
        if l == 0:
            big_mlp[0] = mlp.result(got)
        g_out, _ = _weight_grad(flat(z1), flat(do1), 1, "cols", tt, name=f"grad_w_out_{l}")
        mix = _ReduceScatter([g_in, g_out.reshape(N_CHIPS, -1, d)], cidx, idx, f"mix_{l}")
        small[l] = dict(rowd=rowd1 + rowd2, rowb=rowb1 + rowb2, rowc=rowc, dcw=dcw[:taps], dpw=dpw, dfw=dfw)
    mix.after_swap(_run_exchange(mix.swap(), name="rs_swap_mix_0"))

    dmod_own = jnp.stack([small[l]["rowb"][:, :nmod, :] for l in range(nl)])
    dmod_all, got = _gather8(_rows128(dmod_own), name="gather_dmod", exchange=mix.chips())
    mix.after_chips(got)
    big_mix[0] = mix.result(_run_exchange(mix.share(), name="rs_share_mix_0"))
    dmod_all = dmod_all.reshape(N_DEV, nl, nb, nmod * d)
    dmod_all = dmod_all.transpose(1, 0, 2, 3).reshape(nl, N_DEV * nb, nmod * d)
    dmod_cols = lax.dynamic_slice_in_dim(dmod_all, jm * acol, acol, axis=2)
    (g_ada_w, d_ada_w, nm_ada_w, nv_ada_w), _ = _ada_update(c_all, dmod_cols, ada_w, m_ada_w, v_ada_w, name="ada_update")

    def st(key, row=None):
        return jnp.stack([small[l][key] if row is None else small[l][key][row] for l in range(nl)])

    local = {
        "ada_b": dmod_own.sum(axis=1).reshape(nl, nmod * d),
        "pre_mix_g": st("rowd", 0), "post_mix_g": st("rowd", 1),
        "conv_b": st("rowc", 0), "conv_ln_g": st("rowc", 1), "conv_ln_b": st("rowc", 2),
        "pool_w": st("dpw"), "pool_scale": st("rowc", 3),
        "pre_ffn_g": st("rowd", 2), "post_ffn_g": st("rowd", 3),
        "ffn_conv_b": st("dfw", 3), "conv_w": st("dcw"), "ffn_conv_w": jnp.stack([small[l]["dfw"][:3] for l in range(nl)]),
    }
    names = list(local)
    sizes = [local[k].size for k in names]
    pad = -sum(sizes) % (4 * SUBLANES * LANES)
    packed = jnp.concatenate([local[k].reshape(-1) for k in names] + [jnp.zeros((pad,), F32)])
    summed = _allreduce8(_rows128(packed), name="allreduce_small").reshape(-1)
    grads, off = {}, 0
    for k, sz in zip(names, sizes):
        grads[k] = summed[off:off + sz].reshape(local[k].shape)
        off += sz
    grads["conv_w"] = lax.dynamic_slice_in_dim(grads["conv_w"], jm * ccol, ccol, axis=2)
    grads["ffn_conv_w"] = lax.dynamic_slice_in_dim(grads["ffn_conv_w"], jm * fcol, fcol, axis=2)

    params = dict(ada_b=(ada_b, m_ada_b, v_ada_b), pre_mix_g=(pre_mix_g, m_pre_mix_g, v_pre_mix_g),
                  post_mix_g=(post_mix_g, m_post_mix_g, v_post_mix_g), conv_b=(conv_b, m_conv_b, v_conv_b),
                  conv_ln_g=(conv_ln_g, m_conv_ln_g, v_conv_ln_g), conv_ln_b=(conv_ln_b, m_conv_ln_b, v_conv_ln_b),
                  pool_w=(pool_w, m_pool_w, v_pool_w), pool_scale=(pool_scale, m_pool_scale, v_pool_scale),
                  pre_ffn_g=(pre_ffn_g, m_pre_ffn_g, v_pre_ffn_g), post_ffn_g=(post_ffn_g, m_post_ffn_g, v_post_ffn_g),
                  ffn_conv_b=(ffn_conv_b, m_ffn_conv_b, v_ffn_conv_b), conv_w=(conv_w, m_conv_w, v_conv_w),
                  ffn_conv_w=(ffn_conv_w, m_ffn_conv_w, v_ffn_conv_w))
    pack = lambda i, g=None: _rows128(jnp.concatenate([(grads[k] if g else params[k][i]).reshape(-1) for k in names]))
    sd, sm, sv = _adamw_flat(pack(0), pack(0, True), pack(1), pack(2), name="adamw_small")
    outs = {}
    off = 0
    for k in names:
        shape, sz = params[k][0].shape, params[k][0].size
        outs[k] = (grads[k],) + tuple(a.reshape(-1)[off:off + sz].reshape(shape) for a in (sd, sm, sv))
        off += sz

    outs["ada_w"] = (g_ada_w, d_ada_w, nm_ada_w, nv_ada_w)
    for k, w, m, v, gs in [("w_in", w_in, m_w_in, v_w_in, [big_mix[l][0] for l in range(nl)]),
                           ("w_out", w_out, m_w_out, v_w_out, [big_mix[l][1] for l in range(nl)]),
                           ("ffn_up", ffn_up, m_ffn_up, v_ffn_up, [big_mlp[l][0] for l in range(nl)]),
                           ("ffn_down", ffn_down, m_ffn_down, v_ffn_down, [big_mlp[l][1] for l in range(nl)])]:
        outs[k] = tuple(_adamw_layers(w, m, v, gs, name=f"adamw_{k}"))

    order = ["ada_w", "ada_b", "pre_mix_g", "post_mix_g", "w_in", "conv_w", "conv_b", "conv_ln_g", "conv_ln_b", "pool_w",
             "pool_scale", "w_out", "pre_ffn_g", "post_ffn_g", "ffn_up", "ffn_conv_w", "ffn_conv_b", "ffn_down"]
    return (loss, dx) + tuple(outs[k][i] for i in range(4) for k in order)
```

```python
import functools

import jax
import jax.numpy as jnp
from jax import lax
from jax.experimental import pallas as pl
from jax.experimental.pallas import tpu as pltpu

F32 = jnp.float32
BF16 = jnp.bfloat16
MESH = pl.DeviceIdType.MESH

EPS = 1e-6
POOL_WINDOWS = (2, 4, 8, 16)
ADAM_LR = 0.001
ADAM_B1 = 0.9
ADAM_B2 = 0.999
ADAM_EPS = 1e-08
ADAM_WD = 0.01
ADAM_STEP = 10

N_CHIPS = 4
N_DEV = 8
LANES = 128
SUBLANES = 8
HALO = 32
FHALO = 8
VMEM_LIMIT = 60 * 1024 * 1024
MLP_TILE_ROWS = 256
MIXER_TILE_ROWS = 512
GRAD_CHUNK_ROWS = 2048


def _pos():
    return lax.axis_index("x"), lax.axis_index("y"), lax.axis_index("c")


def _flip(v, f):
    return 1 - v if f else v


def _full(shape):
    nd = len(shape)
    return pl.BlockSpec(shape, lambda *_: (0,) * nd)


_ANY = pl.BlockSpec(memory_space=pl.ANY)
_VMEM = pl.BlockSpec(memory_space=pltpu.VMEM)


def _sigmoid(v):
    return 1.0 / (1.0 + jnp.exp(-v))


def _colsum(v):
    return jnp.sum(v, axis=0, keepdims=True)


def _gather8(v, name, exchange=None):
    r, ccols = v.shape
    ex = exchange
    nci, nco = (len(ex.ins), len(ex.outs)) if ex else (0, 0)

    def body(*refs):
        v_ref, cin, out_ref, cout = refs[0], refs[1:1 + nci], refs[1 + nci], refs[2 + nci:2 + nci + nco]
        send_sems, recv_sems, local_sem = refs[2 + nci + nco:5 + nci + nco]
        if ex:
            sems = _Sems(*refs[5 + nci + nco:])
            ex.start(cin, cout, sems)
        x, y, c = _pos()
        me = 4 * x + 2 * y + c
        mine = pltpu.make_async_copy(v_ref, out_ref.at[me], local_sem)
        mine.start()
        peers = [(_flip(x, (k >> 2) & 1), _flip(y, (k >> 1) & 1), _flip(c, k & 1)) for k in range(1, N_DEV)]
        sends = []
        for k, peer in enumerate(peers):
            cp = pltpu.make_async_remote_copy(src_ref=v_ref, dst_ref=out_ref.at[me], send_sem=send_sems.at[k],
                                              recv_sem=recv_sems.at[k], device_id=peer, device_id_type=MESH)
            cp.start()
            sends.append(cp)
        for k, peer in enumerate(peers):
            pidx = 4 * peer[0] + 2 * peer[1] + peer[2]
            pltpu.make_async_remote_copy(src_ref=v_ref, dst_ref=out_ref.at[pidx], send_sem=send_sems.at[k],
                                         recv_sem=recv_sems.at[k], device_id=peer, device_id_type=MESH).wait_recv()
        for cp in sends:
            cp.wait_send()
        mine.wait()
        if ex:
            if ex.mid is not None:
                ex.mid(cin, cout, sems)
            ex.finish(cin, cout, sems)

    outs = pl.pallas_call(
        body, name=name, out_shape=[jax.ShapeDtypeStruct((N_DEV, r, ccols), v.dtype)] + (ex.outs if ex else []),
        in_specs=[_VMEM] + [_ANY] * nci, out_specs=[_VMEM] + [_ANY] * nco,
        scratch_shapes=[pltpu.SemaphoreType.DMA((N_DEV - 1,)), pltpu.SemaphoreType.DMA((N_DEV - 1,)),
                        pltpu.SemaphoreType.DMA(())] + (ex.scratch() if ex else []),
        input_output_aliases={1 + a: 1 + b for a, b in ex.aliases.items()} if ex else {},
        compiler_params=pltpu.CompilerParams(vmem_limit_bytes=VMEM_LIMIT),
    )(v, *(ex.ins if ex else []))
    return (outs[0], list(outs[1:])) if ex else outs[0]


def _rows_to_owners(v, name):
    _, r, ccols = v.shape

    def body(v_ref, out_ref, send_sems, recv_sems, local_sem):
        x, y, c = _pos()
        jm = 2 * x + y
        mine = pltpu.make_async_copy(v_ref.at[2 * jm + c], out_ref.at[jm], local_sem)
        mine.start()
        peers, pjs = _chip_peers(x, y, c)
        sends = []
        for k, peer in enumerate(peers):
            cp = pltpu.make_async_remote_copy(src_ref=v_ref.at[2 * pjs[k] + c], dst_ref=out_ref.at[jm],
                                              send_sem=send_sems.at[k], recv_sem=recv_sems.at[k],
                                              device_id=peer, device_id_type=MESH)
            cp.start()
            sends.append(cp)
        for k, peer in enumerate(peers):
            pltpu.make_async_remote_copy(src_ref=v_ref.at[0], dst_ref=out_ref.at[pjs[k]], send_sem=send_sems.at[k],
                                         recv_sem=recv_sems.at[k], device_id=peer, device_id_type=MESH).wait_recv()
        for cp in sends:
            cp.wait_send()
        mine.wait()

    return pl.pallas_call(
        body, name=name, out_shape=jax.ShapeDtypeStruct((N_CHIPS, r, ccols), v.dtype),
        in_specs=[_VMEM], out_specs=_VMEM,
        scratch_shapes=[pltpu.SemaphoreType.DMA((N_CHIPS - 1,)), pltpu.SemaphoreType.DMA((N_CHIPS - 1,)),
                        pltpu.SemaphoreType.DMA(())],
        compiler_params=pltpu.CompilerParams(vmem_limit_bytes=VMEM_LIMIT),
    )(v)


def _allreduce8(v, name):
    r, ccols = v.shape
    h = r // 2
    q = h // 2

    def body(v_ref, out_ref, whole, part, done, send_sems, recv_sems):
        x, y, c = _pos()
        sib = (x, y, 1 - c)
        mine = pl.ds(pl.multiple_of(c * h, SUBLANES), h)
        theirs = pl.ds(pl.multiple_of((1 - c) * h, SUBLANES), h)
        quarters = [pl.ds(pl.multiple_of(c * h + k * q, SUBLANES), q) for k in range(2)]
        along_x, along_y = (1 - x, y, c), (x, 1 - y, c)

        def exchange(pairs):
            cps = [pltpu.make_async_remote_copy(src_ref=src, dst_ref=dst, send_sem=send_sems.at[k], recv_sem=recv_sems.at[k],
                                                device_id=peer, device_id_type=MESH) for src, dst, k, peer in pairs]
            for cp in cps:
                cp.start()
            for cp in cps:
                cp.wait()

        exchange([(v_ref, whole, 0, sib)])
        out_ref[...] = v_ref[...] + whole[...]
        for stage, peers in enumerate(((along_x, along_y), (along_y, along_x))):
            exchange([(out_ref.at[quarters[k]], part.at[2 * stage + k], 1 + 2 * stage + k, peers[k]) for k in range(2)])
            for k in range(2):
                out_ref[quarters[k], :] = out_ref[quarters[k], :] + part[2 * stage + k]
        exchange([(out_ref.at[mine], done, 5, sib)])
        out_ref[theirs, :] = done[...]

    return pl.pallas_call(
        body, name=name, out_shape=jax.ShapeDtypeStruct((r, ccols), v.dtype),
        in_specs=[_VMEM], out_specs=_VMEM,
        scratch_shapes=[pltpu.VMEM((r, ccols), v.dtype), pltpu.VMEM((4, q, ccols), v.dtype), pltpu.VMEM((h, ccols), v.dtype),
                        pltpu.SemaphoreType.DMA((6,)), pltpu.SemaphoreType.DMA((6,))],
        compiler_params=pltpu.CompilerParams(vmem_limit_bytes=VMEM_LIMIT),
    )(v)


def _chip_peers(x, y, c):
    peers = [(_flip(x, (k >> 1) & 1), _flip(y, k & 1), c) for k in range(1, N_CHIPS)]
    return peers, [2 * p[0] + p[1] for p in peers]


class _Exchange:
    def __init__(self, ins, outs, aliases, n_sems, n_local, start, finish, mid=None, mid_at=1.0, sibling=False, chips=False):
        self.ins, self.outs, self.aliases = list(ins), list(outs), dict(aliases)
        self.n_sems, self.n_local, self.start, self.finish = n_sems, n_local, start, finish
        self.mid, self.mid_at = mid, mid_at
        self.sibling, self.chips = sibling, chips

    def collective_id(self):
        return {(True, False): 1, (False, True): 2, (True, True): 3}[(self.sibling, self.chips)]

    def handshake(self):
        x, y, c = _pos()
        peers = ([(x, y, 1 - c)] if self.sibling else []) + (_chip_peers(x, y, c)[0] if self.chips else [])
        barrier = pltpu.get_barrier_semaphore()
        for peer in peers:
            pl.semaphore_signal(barrier, inc=1, device_id=peer, device_id_type=MESH)
        pl.semaphore_wait(barrier, len(peers))

    def scratch(self):
        return [pltpu.SemaphoreType.DMA((self.n_sems,)), pltpu.SemaphoreType.DMA((self.n_sems,)),
                pltpu.SemaphoreType.DMA((max(self.n_local, 1),))]


class _Sems:
    def __init__(self, send, recv, local, base=0, lbase=0):
        self.send, self.recv, self.loc, self.base, self.lbase = send, recv, local, base, lbase

    def shifted(self, by, lby):
        return _Sems(self.send, self.recv, self.loc, self.base + by, self.lbase + lby)

    def local(self, k):
        return self.loc.at[self.lbase + k]


def _remote(src, dst, sems, k, peer):
    return pltpu.make_async_remote_copy(src_ref=src, dst_ref=dst, send_sem=sems.send.at[sems.base + k],
                                        recv_sem=sems.recv.at[sems.base + k], device_id=peer, device_id_type=MESH)


def _combine(exs):
    ins = [a for ex in exs for a in ex.ins]
    outs = [o for ex in exs for o in ex.outs]
    aliases, spans, ni, no, ns, nloc = {}, [], 0, 0, 0, 0
    for ex in exs:
        aliases.update({ni + a: no + b for a, b in ex.aliases.items()})
        spans.append((ni, no, ns, nloc))
        ni, no, ns, nloc = ni + len(ex.ins), no + len(ex.outs), ns + ex.n_sems, nloc + ex.n_local

    def each(which):
        def run(ins_, outs_, sems):
            for ex, (i0, o0, s0, l0) in zip(exs, spans):
                stage = getattr(ex, which)
                if stage is not None:
                    stage(ins_[i0:i0 + len(ex.ins)], outs_[o0:o0 + len(ex.outs)], sems.shifted(s0, l0))
        return run

    mids = [ex.mid_at for ex in exs if ex.mid is not None]
    return _Exchange(ins, outs, aliases, ns, nloc, each("start"), each("finish"),
                     mid=each("mid") if mids else None, mid_at=max(mids) if mids else 1.0,
                     sibling=any(ex.sibling for ex in exs), chips=any(ex.chips for ex in exs))


def _gather(shards, mid_at=1.0):
    n = len(shards)
    per = N_CHIPS - 1
    halves = [s.reshape(2, s.shape[0] // 2, s.shape[1]) for s in shards]

    def copies(ins, outs, sems):
        x, y, c = _pos()
        jm = 2 * x + y
        sib = (x, y, 1 - c)
        peers, pjs = _chip_peers(x, y, c)
        sends, recvs, passes, passed = [], [], [], []
        for a in range(n):
            own = _remote(ins[a], outs[a].at[jm], sems, 2 * n * per + a, sib)
            sends.append(own)
            passed.append(own)
            for k, peer in enumerate(peers):
                landed, theirs = outs[a].at[pjs[k], c], outs[a].at[pjs[k], 1 - c]
                sends.append(_remote(ins[a].at[c], outs[a].at[jm, c], sems, 2 * (a * per + k), peer))
                recvs.append(_remote(landed, landed, sems, 2 * (a * per + k), peer))
                passes.append(_remote(landed, landed, sems, 2 * (a * per + k) + 1, sib))
                passed.append(_remote(theirs, theirs, sems, 2 * (a * per + k) + 1, sib))
        return sends, recvs, passes, passed

    def start(ins, outs, sems):
        for cp in copies(ins, outs, sems)[0]:
            cp.start()

    def mid(ins, outs, sems):
        _, recvs, passes, _ = copies(ins, outs, sems)
        for got, fwd in zip(recvs, passes):
            got.wait_recv()
            fwd.start()

    def finish(ins, outs, sems):
        sends, _, passes, passed = copies(ins, outs, sems)
        for cp in passed:
            cp.wait_recv()
        for cp in sends + passes:
            cp.wait_send()

    outs = [jax.ShapeDtypeStruct((N_CHIPS,) + h.shape, h.dtype) for h in halves]
    return _Exchange(halves, outs, {}, 2 * n * per + n, 0, start, finish, mid=mid, mid_at=mid_at, sibling=True, chips=True)


def _whole(gathered):
    return [g.reshape(g.shape[0], g.shape[1] * g.shape[2], g.shape[3]) for g in gathered]


def _swap_halves(gs):
    n = len(gs)
    halves = [g.reshape(g.shape[0], 2, g.shape[1] // 2, g.shape[2]) for g in gs]

    def copies(ins, outs, sems):
        x, y, c = _pos()
        sib = (x, y, 1 - c)
        return [_remote(ins[a].at[:, 1 - c], outs[a], sems, a, sib) for a in range(n)]

    def start(ins, outs, sems):
        for cp in copies(ins, outs, sems):
            cp.start()

    def finish(ins, outs, sems):
        for cp in copies(ins, outs, sems):
            cp.wait()

    outs = [jax.ShapeDtypeStruct((g.shape[0], g.shape[1] // 2, g.shape[2]), g.dtype) for g in gs]
    return _Exchange(halves, outs, {}, n, 0, start, finish, sibling=True)


def _chip_exchange(ps):
    n = len(ps)
    per = N_CHIPS - 1

    def copies(ins, outs, sems):
        x, y, c = _pos()
        peers, pjs = _chip_peers(x, y, c)
        return [_remote(ins[a].at[pjs[k]], outs[a].at[k], sems, a * per + k, peer)
                for a in range(n) for k, peer in enumerate(peers)]

    def start(ins, outs, sems):
        for cp in copies(ins, outs, sems):
            cp.start()

    def finish(ins, outs, sems):
        for cp in copies(ins, outs, sems):
            cp.wait()

    outs = [jax.ShapeDtypeStruct((per,) + p.shape[1:], p.dtype) for p in ps]
    return _Exchange(ps, outs, {}, n * per, 0, start, finish, chips=True)


def _sibling_share(fs):
    n = len(fs)

    def copies(outs, sems):
        x, y, c = _pos()
        sib = (x, y, 1 - c)
        sends = [_remote(outs[a].at[c], outs[a].at[c], sems, a, sib) for a in range(n)]
        recvs = [_remote(outs[a].at[1 - c], outs[a].at[1 - c], sems, a, sib) for a in range(n)]
        return sends, recvs

    def start(ins, outs, sems):
        for cp in copies(outs, sems)[0]:
            cp.start()

    def finish(ins, outs, sems):
        sends, recvs = copies(outs, sems)
        for cp in recvs:
            cp.wait_recv()
        for cp in sends:
            cp.wait_send()

    outs = [jax.ShapeDtypeStruct(f.shape, f.dtype) for f in fs]
    return _Exchange(fs, outs, {a: a for a in range(n)}, n, 0, start, finish, sibling=True)


def _run_exchange(ex, name):
    ni, no = len(ex.ins), len(ex.outs)

    def body(*refs):
        ins, outs, sems = refs[:ni], refs[ni:ni + no], _Sems(*refs[ni + no:])
        ex.handshake()
        ex.start(ins, outs, sems)
        if ex.mid is not None:
            ex.mid(ins, outs, sems)
        ex.finish(ins, outs, sems)

    return pl.pallas_call(
        body, name=name, out_shape=ex.outs, in_specs=[_ANY] * ni, out_specs=[_ANY] * no,
        input_output_aliases=ex.aliases, scratch_shapes=ex.scratch(),
        compiler_params=pltpu.CompilerParams(collective_id=ex.collective_id()),
    )(*ex.ins)


def _grid_call(body, name, grid, in_specs, out_specs, out_shape, scratch_shapes, args, exchange=None):
    ni, no = len(in_specs), len(out_specs)
    params = pltpu.CompilerParams(dimension_semantics=("arbitrary",) * len(grid), vmem_limit_bytes=VMEM_LIMIT)
    if exchange is None:
        outs = pl.pallas_call(body, name=name, grid=grid, in_specs=in_specs, out_specs=out_specs, out_shape=out_shape,
                              scratch_shapes=scratch_shapes, compiler_params=params)(*args)
        return list(outs), []
    ex = exchange
    nci, nco = len(ex.ins), len(ex.outs)

    def hosted(*refs):
        cin = refs[ni:ni + nci]
        cout = refs[ni + nci + no:ni + nci + no + nco]
        sems = _Sems(*refs[len(refs) - 3:])
        main = refs[:ni] + refs[ni + nci:ni + nci + no] + refs[ni + nci + no + nco:len(refs) - 3]
        ids = [pl.program_id(a) for a in range(len(grid))]
        first = functools.reduce(lambda p, q: p & q, [i == 0 for i in ids])
        last = functools.reduce(lambda p, q: p & q, [i == g - 1 for i, g in zip(ids, grid)])

        @pl.when(first)
        def _():
            ex.handshake()
            ex.start(cin, cout, sems)

        if ex.mid is not None:
            steps = functools.reduce(lambda p, q: p * q, grid)
            flat = functools.reduce(lambda p, q: p * q[1] + q[0], zip(ids[1:], grid[1:]), ids[0])

            @pl.when(flat == min(steps - 1, int(ex.mid_at * steps)))
            def _():
                ex.mid(cin, cout, sems)

        body(*main)

        @pl.when(last)
        def _():
            ex.finish(cin, cout, sems)

    outs = pl.pallas_call(
        hosted, name=name, grid=grid, in_specs=list(in_specs) + [_ANY] * nci, out_specs=list(out_specs) + [_ANY] * nco,
        out_shape=list(out_shape) + ex.outs, scratch_shapes=list(scratch_shapes) + ex.scratch(),
        input_output_aliases={ni + a: no + b for a, b in ex.aliases.items()},
        compiler_params=pltpu.CompilerParams(dimension_semantics=("arbitrary",) * len(grid), vmem_limit_bytes=VMEM_LIMIT,
                                             collective_id=ex.collective_id()),
    )(*args, *ex.ins)
    return list(outs[:no]), list(outs[no:])


SUM_BLOCK_BYTES = 4 * 1024 * 1024
ADAM_BLOCK_BYTES = 2 * 1024 * 1024


def _row_tile(rows, cols, itemsize, budget):
    best = None
    for t in range(16, rows + 1, 16):
        if rows % t == 0 and t * cols * itemsize <= budget:
            best = t
    return best if best is not None else rows


def _pair_sum(g, r1, cidx, name):
    nj, r, ccols = g.shape
    hr = r // 2
    tr = _row_tile(hr, ccols, 4, SUM_BLOCK_BYTES)
    nt = hr // tr

    def body(c_ref, g_ref, r_ref, o_ref):
        o_ref[...] = (g_ref[...].astype(F32) + r_ref[...].astype(F32)).astype(o_ref.dtype)

    return pl.pallas_call(
        body, name=name, out_shape=jax.ShapeDtypeStruct((nj, hr, ccols), g.dtype),
        grid_spec=pltpu.PrefetchScalarGridSpec(
            num_scalar_prefetch=1, grid=(nj, nt),
            in_specs=[pl.BlockSpec((None, tr, ccols), lambda j, i, c_ref: (j, c_ref[0] * nt + i, 0)),
                      pl.BlockSpec((None, tr, ccols), lambda j, i, c_ref: (j, i, 0))],
            out_specs=pl.BlockSpec((None, tr, ccols), lambda j, i, c_ref: (j, i, 0))),
        compiler_params=pltpu.CompilerParams(dimension_semantics=("arbitrary", "arbitrary"), vmem_limit_bytes=VMEM_LIMIT),
    )(cidx, g, r1)


def _chip_sum(p, r2, idx, name):
    nj, hr, ccols = p.shape
    tr = _row_tile(hr, ccols, 4, SUM_BLOCK_BYTES)
    nt = hr // tr

    def body(i_ref, p_ref, r_ref, o_ref):
        s = p_ref[...].astype(F32)
        for k in range(N_CHIPS - 1):
            s = s + r_ref[k].astype(F32)
        o_ref[...] = s

    return pl.pallas_call(
        body, name=name, out_shape=jax.ShapeDtypeStruct((2, hr, ccols), F32),
        grid_spec=pltpu.PrefetchScalarGridSpec(
            num_scalar_prefetch=1, grid=(nt,),
            in_specs=[pl.BlockSpec((None, tr, ccols), lambda i, i_ref: (i_ref[0], i, 0)),
                      pl.BlockSpec((N_CHIPS - 1, tr, ccols), lambda i, i_ref: (0, i, 0))],
            out_specs=pl.BlockSpec((None, tr, ccols), lambda i, i_ref: (i_ref[1], i, 0))),
        compiler_params=pltpu.CompilerParams(dimension_semantics=("arbitrary",), vmem_limit_bytes=VMEM_LIMIT),
    )(idx, p, r2)


def _adam_math(w, g, m, v):
    m2 = ADAM_B1 * m + (1.0 - ADAM_B1) * g
    v2 = ADAM_B2 * v + (1.0 - ADAM_B2) * (g * g)
    m_hat = m2 / (1.0 - ADAM_B1 ** ADAM_STEP)
    v_hat = v2 / (1.0 - ADAM_B2 ** ADAM_STEP)
    delta = -ADAM_LR * (m_hat / (jnp.sqrt(v_hat) + ADAM_EPS) + ADAM_WD * w)
    return delta, m2, v2


def _adamw_layers(w, m, v, gs, name):
    nl, r, ccols = w.shape
    ng = len(gs)
    tr = _row_tile(r, ccols, 4, ADAM_BLOCK_BYTES)
    nt = r // tr

    def body(w_ref, m_ref, v_ref, *rest):
        g_refs, (go_ref, d_ref, mo_ref, vo_ref) = rest[:ng], rest[ng:]
        l = pl.program_id(0)
        g = g_refs[0][...]
        for k in range(1, ng):
            g = jnp.where(l == k, g_refs[k][...], g)
        delta, m2, v2 = _adam_math(w_ref[...], g, m_ref[...], v_ref[...])
        go_ref[...] = g
        d_ref[...] = delta
        mo_ref[...] = m2
        vo_ref[...] = v2

    big = pl.BlockSpec((None, tr, ccols), lambda l, i: (l, i, 0))

    def gspec(k):
        return pl.BlockSpec((tr, ccols), lambda l, i: (jnp.where(l == k, i, jnp.where(l < k, 0, nt - 1)), 0))

    assert ng == nl
    return _grid_call(body, name, (nl, nt), in_specs=[big, big, big] + [gspec(k) for k in range(ng)],
                      out_specs=[big, big, big, big], out_shape=[jax.ShapeDtypeStruct(w.shape, F32)] * 4,
                      scratch_shapes=[], args=(w, m, v, *gs))[0]


def _adamw_flat(w, g, m, v, name):
    r, ccols = w.shape

    def body(w_ref, g_ref, m_ref, v_ref, d_ref, mo_ref, vo_ref):
        delta, m2, v2 = _adam_math(w_ref[...], g_ref[...], m_ref[...], v_ref[...])
        d_ref[...] = delta
        mo_ref[...] = m2
        vo_ref[...] = v2

    return pl.pallas_call(
        body, name=name, out_shape=[jax.ShapeDtypeStruct((r, ccols), F32)] * 3,
        in_specs=[_VMEM] * 4, out_specs=[_VMEM] * 3,
        compiler_params=pltpu.CompilerParams(vmem_limit_bytes=VMEM_LIMIT),
    )(w, g, m, v)


def _ada_forward(c_all, ada_w, ada_b_cols, name, exchange=None):
    nl, d, ncols = ada_w.shape
    bg = c_all.shape[0]
    tn = 512 if ncols % 512 == 0 else ncols

    def body(c_ref, w_ref, b_ref, o_ref):
        cv = c_ref[...]
        ca = (cv * _sigmoid(cv)).astype(BF16)
        o_ref[...] = jnp.dot(ca, w_ref[...].astype(BF16), preferred_element_type=F32) + b_ref[...]

    outs, got = _grid_call(
        body, name, (nl, ncols // tn),
        in_specs=[pl.BlockSpec((bg, d), lambda l, j: (0, 0)),
                  pl.BlockSpec((None, d, tn), lambda l, j: (l, 0, j)),
                  pl.BlockSpec((None, 1, tn), lambda l, j: (l, 0, j))],
        out_specs=[pl.BlockSpec((None, bg, tn), lambda l, j: (l, 0, j))],
        out_shape=[jax.ShapeDtypeStruct((nl, bg, ncols), F32)], scratch_shapes=[], args=(c_all, ada_w, ada_b_cols),
        exchange=exchange)
    return outs[0], got


def _ada_update(c_all, dmod_cols, w, m, v, name, exchange=None):
    nl, d, ncols = w.shape
    bg = c_all.shape[0]
    tn = 512 if ncols % 512 == 0 else ncols

    def body(c_ref, dm_ref, w_ref, m_ref, v_ref, go_ref, d_ref, mo_ref, vo_ref):
        cv = c_ref[...]
        ca = (cv * _sigmoid(cv)).astype(BF16)
        g = lax.dot_general(ca, dm_ref[...].astype(BF16), (((0,), (0,)), ((), ())), preferred_element_type=F32)
        delta, m2, v2 = _adam_math(w_ref[...], g, m_ref[...], v_ref[...])
        go_ref[...] = g
        d_ref[...] = delta
        mo_ref[...] = m2
        vo_ref[...] = v2

    big = pl.BlockSpec((None, d, tn), lambda l, j: (l, 0, j))
    return _grid_call(
        body, name, (nl, ncols // tn),
        in_specs=[pl.BlockSpec((bg, d), lambda l, j: (0, 0)),
                  pl.BlockSpec((None, bg, tn), lambda l, j: (l, 0, j)), big, big, big],
        out_specs=[big, big, big, big], out_shape=[jax.ShapeDtypeStruct(w.shape, F32)] * 4,
        scratch_shapes=[], args=(c_all, dmod_cols, w, m, v), exchange=exchange)


def _load_weights(first, pairs, sems):
    @pl.when(first)
    def _():
        cps = [pltpu.make_async_copy(src, dst, sems.at[k]) for k, (src, dst) in enumerate(pairs)]
        for cp in cps:
            cp.start()
        for cp in cps:
            cp.wait()


def _ada_norm(xv, g, sc, sh):
    r = lax.rsqrt(jnp.mean(xv * xv, axis=-1, keepdims=True) + EPS)
    xn = xv * r
    return (xn * g) * (1.0 + sc) + sh, xn, r


def _ada_norm_bwd(dh, xn, r, g, sc):
    d_sh = _colsum(dh)
    d_sc = _colsum(dh * (xn * g))
    dxg = dh * (1.0 + sc)
    d_g = _colsum(dxg * xn)
    gd = dxg * g
    dx = r * (gd - xn * jnp.mean(gd * xn, axis=-1, keepdims=True))
    return dx, d_sh, d_sc, d_g


def _gated_residual_bwd(dxo, o, g_post, gt):
    r = lax.rsqrt(jnp.mean(o * o, axis=-1, keepdims=True) + EPS)
    on = o * r
    d_gt = _colsum(dxo * (on * g_post))
    dy = dxo * (1.0 + gt)
    d_gp = _colsum(dy * on)
    gd = dy * g_post
    do = r * (gd - on * jnp.mean(gd * on, axis=-1, keepdims=True))
    return do, d_gt, d_gp


def _seq_positions(i, tm, width):
    return i * tm + lax.broadcasted_iota(jnp.int32, (tm, width), 0)


def _fill_phases(ext, phases):
    rows = ext.shape[0]
    ev = ext[...]
    for r in range(1, SUBLANES):
        phases[r - 1] = pltpu.roll(ev, rows - r, axis=0)


def _shifted_rows(ext, phases, offset, n):
    q, r = divmod(offset, SUBLANES)
    if r == 0:
        return ext[pl.ds(q * SUBLANES, n), :]
    return phases[r - 1, pl.ds(q * SUBLANES, n), :]


def _rows_before(halo, cur, shift):
    e = jnp.concatenate([halo, cur], axis=0)
    return pltpu.roll(e, shift, axis=0)[halo.shape[0]:, :]


def _rows_after(cur, halo, shift):
    e = jnp.concatenate([cur, halo], axis=0)
    return pltpu.roll(e, e.shape[0] - shift, axis=0)[:cur.shape[0], :]


def _mixer_forward(x, mod, vec_d, vec_c, cw, pw, win_g, wout_g, taps, tm, name, exchange=None):
    nb, s, d = x.shape
    n = s // tm
    nj, _, dcol = win_g.shape
    din = nj * dcol
    dc = vec_c.shape[-1]
    dpool = din - 2 * dc
    dmix = dc + dpool
    ro = wout_g.shape[1]
    ngrp = dpool // LANES

    def body(x_ref, mod_ref, vd_ref, vc_ref, cw_ref, pw_ref, win_hbm, wout_hbm,
             xo_ref, h_ref, u_ref, ac_ref, dp_ref, z_ref, o_ref,
             win_v, wout_v, ext_a, ext_p, phases, sems):
        b, i = pl.program_id(0), pl.program_id(1)
        pairs = [(win_hbm.at[j], win_v.at[:, pl.ds(j * dcol, dcol)]) for j in range(nj)]
        pairs += [(wout_hbm.at[j], wout_v.at[pl.ds(j * ro, ro), :]) for j in range(nj)]
        _load_weights((b == 0) & (i == 0), pairs, sems)

        xv = x_ref[...]
        h, _, _ = _ada_norm(xv, vd_ref[0:1, :], mod_ref[1:2, :], mod_ref[0:1, :])
        hb = h.astype(BF16)
        h_ref[...] = hb
        u = jnp.dot(hb, win_v[...], preferred_element_type=F32)
        u_ref[...] = u.astype(BF16)
        ag = u[:, :dc] * _sigmoid(u[:, dc:2 * dc])
        up = u[:, 2 * dc:]

        @pl.when(i == 0)
        def _():
            ext_a[0:HALO, :] = jnp.zeros((HALO, dc), F32)
            ext_p[0:HALO, :] = jnp.zeros((HALO, dpool), F32)

        @pl.when(i > 0)
        def _():
            ext_a[0:HALO, :] = ext_a[tm:tm + HALO, :]
            ext_p[0:HALO, :] = ext_p[tm:tm + HALO, :]

        ext_a[HALO:HALO + tm, :] = ag
        ext_p[HALO:HALO + tm, :] = up

        acc = jnp.broadcast_to(vc_ref[0:1, :], (tm, dc))
        _fill_phases(ext_a, phases)
        for k in range(taps):
            acc = acc + cw_ref[k:k + 1, :] * _shifted_rows(ext_a, phases, HALO - (taps - 1) + k, tm)
        ac_ref[...] = acc.astype(BF16)
        mu = jnp.mean(acc, axis=-1, keepdims=True)
        xc = acc - mu
        var = jnp.mean(xc * xc, axis=-1, keepdims=True)
        al = (xc * lax.rsqrt(var + EPS)) * vc_ref[1:2, :] + vc_ref[2:3, :]
        a = al * _sigmoid(al)

        pos = _seq_positions(i, tm, LANES)
        parts = [a.astype(BF16)]
        for g in range(ngrp):
            w = POOL_WINDOWS[g]
            cols = slice(g * LANES, (g + 1) * LANES)
            sw = ext_p[:, cols]
            step = 1
            while step < w:
                sw = sw + pltpu.roll(sw, step, axis=0)
                step *= 2
            cnt = jnp.minimum(pos + 1, w).astype(F32)
            dg = (sw[HALO:, :] / cnt - up[:, cols]).astype(BF16)
            dp_ref[:, cols] = dg
            q = jnp.dot(dg, pw_ref[g], preferred_element_type=F32)
            parts.append((q * vc_ref[3:4, cols]).astype(BF16))
        z = jnp.concatenate(parts, axis=-1)
        z_ref[...] = z
        o = jnp.dot(z, wout_v[...], preferred_element_type=F32)
        o_ref[...] = o
        r2 = lax.rsqrt(jnp.mean(o * o, axis=-1, keepdims=True) + EPS)
        xo_ref[...] = xv + (1.0 + mod_ref[2:3, :]) * ((o * r2) * vd_ref[1:2, :])

    def tile(width):
        return pl.BlockSpec((None, tm, width), lambda b, i: (b, i, 0))

    return _grid_call(
        body, name, (nb, n),
        in_specs=[tile(d), pl.BlockSpec((None, 8, d), lambda b, i: (b, 0, 0)), _full(vec_d.shape), _full(vec_c.shape),
                  _full(cw.shape), _full(pw.shape), _ANY, _ANY],
        out_specs=[tile(d), tile(d), tile(din), tile(dc), tile(dpool), tile(dmix), tile(d)],
        out_shape=[jax.ShapeDtypeStruct((nb, s, d), F32), jax.ShapeDtypeStruct((nb, s, d), BF16),
                   jax.ShapeDtypeStruct((nb, s, din), BF16), jax.ShapeDtypeStruct((nb, s, dc), BF16),
                   jax.ShapeDtypeStruct((nb, s, dpool), BF16), jax.ShapeDtypeStruct((nb, s, dmix), BF16),
                   jax.ShapeDtypeStruct((nb, s, d), F32)],
        scratch_shapes=[pltpu.VMEM((d, din), BF16), pltpu.VMEM((dmix, d), BF16),
                        pltpu.VMEM((HALO + tm, dc), F32), pltpu.VMEM((HALO + tm, dpool), F32),
                        pltpu.VMEM((SUBLANES - 1, HALO + tm, dc), F32), pltpu.SemaphoreType.DMA((2 * nj,))],
        args=(x, mod, vec_d, vec_c, cw, pw, win_g, wout_g), exchange=exchange)


def _mixer_backward(dxo, x, o, u, ac, dpl, mod, vec_d, vec_c, cw, pw, win_g, wout_g, taps, tm, name, exchange=None):
    nb, s, d = x.shape
    n = s // tm
    nj, _, dcol = win_g.shape
    din = nj * dcol
    dc = vec_c.shape[-1]
    dpool = din - 2 * dc
    dmix = dc + dpool
    ro = wout_g.shape[1]
    ngrp = dpool // LANES
    rext = tm + HALO

    def body(dxo_ref, x_ref, o_ref, u_ref, ac_ref, dp_ref, mod_ref, vd_ref, vc_ref, cw_ref, pw_ref, win_hbm, wout_hbm,
             dx_ref, du_ref, dob_ref, rowd_ref, rowb_ref, rowc_ref, dcw_ref, dpw_ref,
             win_v, wout_v, ext_a, ext_p, phases, sems):
        b, i = pl.program_id(0), pl.program_id(1)
        first = (b == 0) & (i == 0)
        pairs = [(win_hbm.at[j], win_v.at[:, pl.ds(j * dcol, dcol)]) for j in range(nj)]
        pairs += [(wout_hbm.at[j], wout_v.at[pl.ds(j * ro, ro), :]) for j in range(nj)]
        _load_weights(first, pairs, sems)

        @pl.when(first)
        def _():
            rowd_ref[...] = jnp.zeros_like(rowd_ref)
            rowc_ref[...] = jnp.zeros_like(rowc_ref)
            dcw_ref[...] = jnp.zeros_like(dcw_ref)
            dpw_ref[...] = jnp.zeros_like(dpw_ref)

        @pl.when(i == 0)
        def _():
            rowb_ref[...] = jnp.zeros_like(rowb_ref)
            ext_a[tm:rext, :] = jnp.zeros((HALO, dc), F32)
            ext_p[tm:rext, :] = jnp.zeros((HALO, dpool), F32)

        @pl.when(i > 0)
        def _():
            ext_a[tm:rext, :] = ext_a[0:HALO, :]
            ext_p[tm:rext, :] = ext_p[0:HALO, :]

        g_pre, g_post = vd_ref[0:1, :], vd_ref[1:2, :]
        sh, sc, gt = mod_ref[0:1, :], mod_ref[1:2, :], mod_ref[2:3, :]
        do, d_gt, d_gp = _gated_residual_bwd(dxo_ref[...], o_ref[...], g_post, gt)
        dob = do.astype(BF16)
        dob_ref[...] = dob
        dz = lax.dot_general(dob, wout_v[...], (((1,), (1,)), ((), ())), preferred_element_type=F32)

        acv = ac_ref[...].astype(F32)
        mu = jnp.mean(acv, axis=-1, keepdims=True)
        xc = acv - mu
        rstd = lax.rsqrt(jnp.mean(xc * xc, axis=-1, keepdims=True) + EPS)
        an = xc * rstd
        lg = vc_ref[1:2, :]
        al = an * lg + vc_ref[2:3, :]
        sg = _sigmoid(al)
        dal = dz[:, :dc] * (sg * (1.0 + al * (1.0 - sg)))
        d_lg = _colsum(dal * an)
        d_lb = _colsum(dal)
        dan = dal * lg
        dac = rstd * (dan - jnp.mean(dan, axis=-1, keepdims=True) - an * jnp.mean(dan * an, axis=-1, keepdims=True))
        d_cb = _colsum(dac)
        ext_a[0:tm, :] = dac
        uv = u_ref[:, 0:dc].astype(F32)
        sgu = _sigmoid(u_ref[:, dc:2 * dc].astype(F32))
        ag = uv * sgu
        dag = jnp.zeros((tm, dc), F32)
        _fill_phases(ext_a, phases)
        for k in range(taps):
            sl = _shifted_rows(ext_a, phases, taps - 1 - k, tm)
            dag = dag + cw_ref[k:k + 1, :] * sl
            dcw_ref[k:k + 1, :] += _colsum(ag * sl)
        du_ref[:, 0:dc] = (dag * sgu).astype(BF16)
        du_ref[:, dc:2 * dc] = (dag * uv * (sgu * (1.0 - sgu))).astype(BF16)

        pos = _seq_positions(n - 1 - i, tm, LANES)
        d_ps = []
        for g in range(ngrp):
            w = POOL_WINDOWS[g]
            cols = slice(g * LANES, (g + 1) * LANES)
            gcols = slice(dc + g * LANES, dc + (g + 1) * LANES)
            dgb = dp_ref[:, cols]
            q = jnp.dot(dgb, pw_ref[g], preferred_element_type=F32)
            dpg = dz[:, gcols]
            d_ps.append(_colsum(dpg * q))
            dq = (dpg * vc_ref[3:4, cols]).astype(BF16)
            dpw_ref[g] += lax.dot_general(dgb, dq, (((0,), (0,)), ((), ())), preferred_element_type=F32)
            dd = lax.dot_general(dq, pw_ref[g], (((1,), (1,)), ((), ())), preferred_element_type=F32)
            cnt = jnp.minimum(pos + 1, w).astype(F32)
            ext_p[0:tm, cols] = dd / cnt
            sw = ext_p[:, cols]
            step = 1
            while step < w:
                sw = sw + pltpu.roll(sw, rext - step, axis=0)
                step *= 2
            du_ref[:, 2 * dc + g * LANES:2 * dc + (g + 1) * LANES] = (sw[0:tm, :] - dd).astype(BF16)
        rowc_ref[0:1, :] += d_cb
        rowc_ref[1:2, :] += d_lg
        rowc_ref[2:3, :] += d_lb
        rowc_ref[3:4, :] += jnp.concatenate(d_ps, axis=-1)

        dh = lax.dot_general(du_ref[...], win_v[...], (((1,), (1,)), ((), ())), preferred_element_type=F32)
        _, xn, r1 = _ada_norm(x_ref[...], g_pre, sc, sh)
        dxb, d_sh, d_sc, d_g = _ada_norm_bwd(dh, xn, r1, g_pre, sc)
        dx_ref[...] = dxo_ref[...] + dxb
        rowd_ref[0:1, :] += d_g
        rowd_ref[1:2, :] += d_gp
        rowb_ref[0:1, :] += d_sh
        rowb_ref[1:2, :] += d_sc
        rowb_ref[2:3, :] += d_gt

    def tile(width):
        return pl.BlockSpec((None, tm, width), lambda b, i: (b, n - 1 - i, 0))

    return _grid_call(
        body, name, (nb, n),
        in_specs=[tile(d), tile(d), tile(d), tile(din), tile(dc), tile(dpool),
                  pl.BlockSpec((None, 8, d), lambda b, i: (b, 0, 0)), _full(vec_d.shape), _full(vec_c.shape),
                  _full(cw.shape), _full(pw.shape), _ANY, _ANY],
        out_specs=[tile(d), tile(din), tile(d), _full((8, d)), pl.BlockSpec((None, 8, d), lambda b, i: (b, 0, 0)),
                   _full((8, dc)), _full((HALO, dc)), _full(pw.shape)],
        out_shape=[jax.ShapeDtypeStruct((nb, s, d), F32), jax.ShapeDtypeStruct((nb, s, din), BF16),
                   jax.ShapeDtypeStruct((nb, s, d), BF16), jax.ShapeDtypeStruct((8, d), F32),
                   jax.ShapeDtypeStruct((nb, 8, d), F32), jax.ShapeDtypeStruct((8, dc), F32),
                   jax.ShapeDtypeStruct((HALO, dc), F32), jax.ShapeDtypeStruct(pw.shape, F32)],
        scratch_shapes=[pltpu.VMEM((d, din), BF16), pltpu.VMEM((dmix, d), BF16),
                        pltpu.VMEM((rext, dc), F32), pltpu.VMEM((rext, dpool), F32),
                        pltpu.VMEM((SUBLANES - 1, rext, dc), F32), pltpu.SemaphoreType.DMA((2 * nj,))],
        args=(dxo, x, o, u, ac, dpl, mod, vec_d, vec_c, cw, pw, win_g, wout_g), exchange=exchange)


def _ffn_forward(x, mod, vec_d, fw, wup_g, wdn_g, tm, name, exchange=None):
    nb, s, d = x.shape
    n = s // tm
    nj, _, ucol = wup_g.shape
    f2 = nj * ucol
    dff = f2 // 2
    rd = wdn_g.shape[1]
    nq = nj // 2
    cs = dff // nq

    def body(x_ref, mod_ref, vd_ref, fw_ref, wup_hbm, wdn_hbm,
             xo_ref, h_ref, u_ref, uc_ref, hid_ref, o_ref,
             wup_v, wdn_v, prev_u, sems):
        b, i = pl.program_id(0), pl.program_id(1)
        pairs = [(wup_hbm.at[j], wup_v.at[:, pl.ds(j * ucol, ucol)]) for j in range(nj)]
        pairs += [(wdn_hbm.at[j], wdn_v.at[pl.ds(j * rd, rd), :]) for j in range(nj)]
        _load_weights((b == 0) & (i == 0), pairs, sems)

        @pl.when(i == 0)
        def _():
            prev_u[...] = jnp.zeros_like(prev_u)

        xv = x_ref[...]
        h, _, _ = _ada_norm(xv, vd_ref[2:3, :], mod_ref[4:5, :], mod_ref[3:4, :])
        hb = h.astype(BF16)
        h_ref[...] = hb

        def conv(cols):
            uc = jnp.dot(hb, wup_v[:, cols], preferred_element_type=F32)
            u_ref[:, cols] = uc.astype(BF16)
            before = prev_u[:, cols]
            prev_u[:, cols] = uc[tm - FHALO:, :]
            out = (fw_ref[3:4, cols] + fw_ref[2:3, cols] * uc + fw_ref[1:2, cols] * _rows_before(before, uc, 1)
                   + fw_ref[0:1, cols] * _rows_before(before, uc, 2))
            uc_ref[:, cols] = out.astype(BF16)
            return out

        o = jnp.zeros((tm, d), F32)
        for q in range(nq):
            val = conv(pl.ds(q * cs, cs))
            gate = conv(pl.ds(dff + q * cs, cs))
            hid = ((gate * _sigmoid(gate)) * val).astype(BF16)
            hid_ref[:, pl.ds(q * cs, cs)] = hid
            o = o + jnp.dot(hid, wdn_v[pl.ds(q * cs, cs), :], preferred_element_type=F32)
        o_ref[...] = o
        r2 = lax.rsqrt(jnp.mean(o * o, axis=-1, keepdims=True) + EPS)
        xo_ref[...] = xv + (1.0 + mod_ref[5:6, :]) * ((o * r2) * vd_ref[3:4, :])

    def tile(width):
        return pl.BlockSpec((None, tm, width), lambda b, i: (b, i, 0))

    return _grid_call(
        body, name, (nb, n),
        in_specs=[tile(d), pl.BlockSpec((None, 8, d), lambda b, i: (b, 0, 0)), _full(vec_d.shape), _full(fw.shape),
                  _ANY, _ANY],
        out_specs=[tile(d), tile(d), tile(f2), tile(f2), tile(dff), tile(d)],
        out_shape=[jax.ShapeDtypeStruct((nb, s, d), F32), jax.ShapeDtypeStruct((nb, s, d), BF16),
                   jax.ShapeDtypeStruct((nb, s, f2), BF16), jax.ShapeDtypeStruct((nb, s, f2), BF16),
                   jax.ShapeDtypeStruct((nb, s, dff), BF16), jax.ShapeDtypeStruct((nb, s, d), F32)],
        scratch_shapes=[pltpu.VMEM((d, f2), BF16), pltpu.VMEM((dff, d), BF16),
                        pltpu.VMEM((FHALO, f2), F32), pltpu.SemaphoreType.DMA((2 * nj,))],
        args=(x, mod, vec_d, fw, wup_g, wdn_g), exchange=exchange)


def _ffn_backward(dxo, x, o, u, uc, mod, vec_d, fw, wup_g, wdn_g, tm, name, exchange=None):
    nb, s, d = x.shape
    n = s // tm
    nj, _, ucol = wup_g.shape
    f2 = nj * ucol
    dff = f2 // 2
    rd = wdn_g.shape[1]
    nq = nj // 2
    cs = dff // nq

    def body(dxo_ref, x_ref, o_ref, u_ref, uc_ref, mod_ref, vd_ref, fw_ref, wup_hbm, wdn_hbm,
             dx_ref, du_ref, dob_ref, rowd_ref, rowb_ref, dfw_ref,
             wup_v, wdn_v, next_d, sems):
        b, i = pl.program_id(0), pl.program_id(1)
        first = (b == 0) & (i == 0)
        pairs = [(wup_hbm.at[j], wup_v.at[:, pl.ds(j * ucol, ucol)]) for j in range(nj)]
        pairs += [(wdn_hbm.at[j], wdn_v.at[pl.ds(j * rd, rd), :]) for j in range(nj)]
        _load_weights(first, pairs, sems)

        @pl.when(first)
        def _():
            rowd_ref[...] = jnp.zeros_like(rowd_ref)
            dfw_ref[...] = jnp.zeros_like(dfw_ref)

        @pl.when(i == 0)
        def _():
            rowb_ref[...] = jnp.zeros_like(rowb_ref)
            next_d[...] = jnp.zeros_like(next_d)

        g_pre, g_post = vd_ref[2:3, :], vd_ref[3:4, :]
        sh, sc, gt = mod_ref[3:4, :], mod_ref[4:5, :], mod_ref[5:6, :]
        do, d_gt, d_gp = _gated_residual_bwd(dxo_ref[...], o_ref[...], g_post, gt)
        dob = do.astype(BF16)
        dob_ref[...] = dob

        def conv_bwd(cols, duc):
            uc = u_ref[:, cols].astype(F32)
            after = next_d[:, cols]
            next_d[:, cols] = duc[0:FHALO, :]
            d1 = _rows_after(duc, after, 1)
            d2 = _rows_after(duc, after, 2)
            dfw_ref[3:4, cols] += _colsum(duc)
            dfw_ref[2:3, cols] += _colsum(uc * duc)
            dfw_ref[1:2, cols] += _colsum(uc * d1)
            dfw_ref[0:1, cols] += _colsum(uc * d2)
            ob = (fw_ref[2:3, cols] * duc + fw_ref[1:2, cols] * d1 + fw_ref[0:1, cols] * d2).astype(BF16)
            du_ref[:, cols] = ob
            return lax.dot_general(ob, wup_v[:, cols], (((1,), (1,)), ((), ())), preferred_element_type=F32)

        dh = jnp.zeros((tm, d), F32)
        for q in range(nq):
            vcols = pl.ds(q * cs, cs)
            gcols = pl.ds(dff + q * cs, cs)
            dhid = lax.dot_general(dob, wdn_v[vcols, :], (((1,), (1,)), ((), ())), preferred_element_type=F32)
            val = uc_ref[:, vcols].astype(F32)
            gate = uc_ref[:, gcols].astype(F32)
            sg = _sigmoid(gate)
            act = gate * sg
            dval = dhid * act
            dgate = (dhid * val) * (sg + act * (1.0 - sg))
            dh = dh + conv_bwd(vcols, dval)
            dh = dh + conv_bwd(gcols, dgate)

        _, xn, r1 = _ada_norm(x_ref[...], g_pre, sc, sh)
        dxb, d_sh, d_sc, d_g = _ada_norm_bwd(dh, xn, r1, g_pre, sc)
        dx_ref[...] = dxo_ref[...] + dxb
        rowd_ref[2:3, :] += d_g
        rowd_ref[3:4, :] += d_gp
        rowb_ref[3:4, :] += d_sh
        rowb_ref[4:5, :] += d_sc
        rowb_ref[5:6, :] += d_gt

    def tile(width):
        return pl.BlockSpec((None, tm, width), lambda b, i: (b, n - 1 - i, 0))

    return _grid_call(
        body, name, (nb, n),
        in_specs=[tile(d), tile(d), tile(d), tile(f2), tile(f2), pl.BlockSpec((None, 8, d), lambda b, i: (b, 0, 0)),
                  _full(vec_d.shape), _full(fw.shape), _ANY, _ANY],
        out_specs=[tile(d), tile(f2), tile(d), _full((8, d)), pl.BlockSpec((None, 8, d), lambda b, i: (b, 0, 0)),
                   _full(fw.shape)],
        out_shape=[jax.ShapeDtypeStruct((nb, s, d), F32), jax.ShapeDtypeStruct((nb, s, f2), BF16),
                   jax.ShapeDtypeStruct((nb, s, d), BF16), jax.ShapeDtypeStruct((8, d), F32),
                   jax.ShapeDtypeStruct((nb, 8, d), F32), jax.ShapeDtypeStruct(fw.shape, F32)],
        scratch_shapes=[pltpu.VMEM((d, f2), BF16), pltpu.VMEM((dff, d), BF16),
                        pltpu.VMEM((FHALO, f2), F32), pltpu.SemaphoreType.DMA((2 * nj,))],
        args=(dxo, x, o, u, uc, mod, vec_d, fw, wup_g, wdn_g), exchange=exchange)


def _weight_grad(a, b, nblk, split, tt, name, exchange=None):
    t, ka = a.shape
    nb_ = b.shape[1]
    nk = t // tt
    if split == "cols":
        wa, wb, grid = ka, nb_ // nblk, (1, nk)
        a_spec = pl.BlockSpec((tt, ka), lambda j, k: (k, 0))
        b_spec = pl.BlockSpec((tt, nb_), lambda j, k: (k, 0))
        o_spec = pl.BlockSpec((nblk, wa, wb), lambda j, k: (0, 0, 0))
        acc_shape = (ka, nb_)
    elif split == "b":
        wa, wb, grid = ka, nb_ // nblk, (nblk, nk)
        a_spec = pl.BlockSpec((tt, wa), lambda j, k: (k, 0))
        b_spec = pl.BlockSpec((tt, wb), lambda j, k: (k, j))
        o_spec = pl.BlockSpec((None, wa, wb), lambda j, k: (j, 0, 0))
        acc_shape = (wa, wb)
    else:
        wa, wb, grid = ka // nblk, nb_, (nblk, nk)
        a_spec = pl.BlockSpec((tt, wa), lambda j, k: (k, j))
        b_spec = pl.BlockSpec((tt, wb), lambda j, k: (k, 0))
        o_spec = pl.BlockSpec((None, wa, wb), lambda j, k: (j, 0, 0))
        acc_shape = (wa, wb)

    def body(a_ref, b_ref, o_ref, acc):
        k = pl.program_id(1)
        prod = lax.dot_general(a_ref[...], b_ref[...], (((0,), (0,)), ((), ())), preferred_element_type=F32)

        @pl.when(k == 0)
        def _():
            acc[...] = prod

        @pl.when(k > 0)
        def _():
            acc[...] += prod

        @pl.when(k == nk - 1)
        def _():
            if split == "cols":
                for j in range(nblk):
                    o_ref[j] = acc[:, j * wb:(j + 1) * wb].astype(o_ref.dtype)
            else:
                o_ref[...] = acc[...].astype(o_ref.dtype)

    outs, exo = _grid_call(body, name, grid, in_specs=[a_spec, b_spec], out_specs=[o_spec],
                           out_shape=[jax.ShapeDtypeStruct((nblk, wa, wb), BF16)],
                           scratch_shapes=[pltpu.VMEM(acc_shape, F32)], args=(a, b), exchange=exchange)
    return outs[0], exo


def _loss_grad(y, tgt, tm, name):
    nb, s, d = y.shape
    n = s // tm

    def body(y_ref, t_ref, dy_ref, sq_ref):
        @pl.when((pl.program_id(0) == 0) & (pl.program_id(1) == 0))
        def _():
            sq_ref[...] = jnp.zeros_like(sq_ref)

        e = y_ref[...] - t_ref[...]
        dy_ref[...] = e * (1.0 / d)
        sq_ref[0:1, :] += _colsum(e * e)

    tile = pl.BlockSpec((None, tm, d), lambda b, i: (b, i, 0))
    return pl.pallas_call(
        body, name=name, out_shape=[jax.ShapeDtypeStruct((nb, s, d), F32), jax.ShapeDtypeStruct((8, d), F32)],
        grid=(nb, n), in_specs=[tile, tile], out_specs=[tile, _full((8, d))],
        compiler_params=pltpu.CompilerParams(dimension_semantics=("arbitrary", "arbitrary")),
    )(y, tgt)


def _rows128(a):
    return a.reshape(-1, LANES)


class _ReduceScatter:
    def __init__(self, gs, cidx, idx, tag):
        self.gs, self.cidx, self.idx, self.tag = gs, cidx, idx, tag

    def swap(self):
        return _swap_halves(self.gs)

    def after_swap(self, r1):
        self.ps = [_pair_sum(g, r, self.cidx, name=f"rs_pair_{self.tag}_{a}") for a, (g, r) in enumerate(zip(self.gs, r1))]

    def chips(self):
        return _chip_exchange(self.ps)

    def after_chips(self, r2):
        self.fh = [_chip_sum(p, r, self.idx, name=f"rs_sum_{self.tag}_{a}") for a, (p, r) in enumerate(zip(self.ps, r2))]

    def share(self):
        return _sibling_share(self.fh)

    @staticmethod
    def result(fs):
        return [f.reshape(f.shape[0] * f.shape[1], f.shape[2]) for f in fs]


def kernel(x, c, ada_w, ada_b, pre_mix_g, post_mix_g, w_in, conv_w, conv_b, conv_ln_g, conv_ln_b, pool_w, pool_scale, w_out, pre_ffn_g, post_ffn_g, ffn_up, ffn_conv_w, ffn_conv_b, ffn_down, loss_target, m_ada_w, m_ada_b, m_pre_mix_g, m_post_mix_g, m_w_in, m_conv_w, m_conv_b, m_conv_ln_g, m_conv_ln_b, m_pool_w, m_pool_scale, m_w_out, m_pre_ffn_g, m_post_ffn_g, m_ffn_up, m_ffn_conv_w, m_ffn_conv_b, m_ffn_down, v_ada_w, v_ada_b, v_pre_mix_g, v_post_mix_g, v_w_in, v_conv_w, v_conv_b, v_conv_ln_g, v_conv_ln_b, v_pool_w, v_pool_scale, v_w_out, v_pre_ffn_g, v_post_ffn_g, v_ffn_up, v_ffn_conv_w, v_ffn_conv_b, v_ffn_down):
    nb, s, d = x.shape
    nl = w_in.shape[0]
    taps = conv_w.shape[1]
    ccol = conv_w.shape[2]
    dc = conv_b.shape[1]
    fcol = ffn_conv_w.shape[2]
    f2 = ffn_conv_b.shape[1]
    nmod = ada_b.shape[1] // d
    acol = ada_w.shape[2]
    tm = min(MLP_TILE_ROWS, s)
    tm_mix = min(MIXER_TILE_ROWS, s)
    tt = min(GRAD_CHUNK_ROWS, (nb * s) // 2)

    xi, yi, ci = _pos()
    jm = 2 * xi + yi
    cidx = jnp.reshape(ci, (1,)).astype(jnp.int32)
    idx = jnp.stack([jm, ci]).astype(jnp.int32)

    win_b, wout_b, wup_b, wdn_b = (w.astype(BF16) for w in (w_in, w_out, ffn_up, ffn_down))

    def others(l):
        return [win_b[l], wout_b[l], wdn_b[l]]

    n_cw, n_fw, n_c = nl * taps * ccol, nl * 3 * fcol, nb * d
    packed = jnp.concatenate([conv_w.reshape(-1), ffn_conv_w.reshape(-1), c.reshape(-1)])
    got = _gather8(_rows128(packed), name="gather_small").reshape(N_DEV, -1)
    chips = got[0::2]
    cw_full = chips[:, :n_cw].reshape(N_CHIPS, nl, taps, ccol).transpose(1, 2, 0, 3).reshape(nl, taps, dc)
    fw_full = chips[:, n_cw:n_cw + n_fw].reshape(N_CHIPS, nl, 3, fcol).transpose(1, 2, 0, 3).reshape(nl, 3, f2)
    c_all = got[:, n_cw + n_fw:].reshape(N_DEV * nb, d)

    ada_b_cols = lax.dynamic_slice_in_dim(ada_b, jm * acol, acol, axis=1).reshape(nl, 1, acol)
    mod_cols, first_weights = _ada_forward(c_all, ada_w, ada_b_cols, name="ada_forward",
                                           exchange=_gather(others(0), mid_at=0.9))
    by_owner = mod_cols.reshape(nl, N_DEV, nb, acol).transpose(1, 0, 2, 3).reshape(N_DEV, -1, LANES)
    mod_own = _rows_to_owners(by_owner, name="mod_to_owners").reshape(N_CHIPS, nl, nb, acol)
    mod_own = mod_own.transpose(1, 2, 0, 3).reshape(nl, nb, nmod, d)
    mod_own = jnp.pad(mod_own, ((0, 0), (0, 0), (0, 8 - nmod), (0, 0)))

    vec_d = jnp.stack([pre_mix_g, post_mix_g, pre_ffn_g, post_ffn_g], axis=1)
    vec_c = jnp.stack([conv_b, conv_ln_g, conv_ln_b, pool_scale], axis=1)
    cw_pad = jnp.pad(cw_full, ((0, 0), (0, HALO - taps), (0, 0)))
    fw_rows = jnp.concatenate([fw_full, ffn_conv_b[:, None, :], jnp.zeros((nl, 4, f2), F32)], axis=1)
    pw_b = pool_w.astype(BF16)

    win_g, wout_g, wdn_g = _whole(first_weights)
    saved = []
    xs = x
    for l in range(nl):
        (x1, h1, u1, ac1, dp1, z1, o1), got = _mixer_forward(
            xs, mod_own[l], vec_d[l], vec_c[l], cw_pad[l], pw_b[l], win_g, wout_g, taps, tm_mix, name=f"mixer_fwd_{l}",
            exchange=_gather([wup_b[l]], mid_at=0.9))
        wup_g, = _whole(got)
        (x2, h2, u2, uc2, hid2, o2), nxt = _ffn_forward(
            x1, mod_own[l], vec_d[l], fw_rows[l], wup_g, wdn_g, tm, name=f"ffn_fwd_{l}",
            exchange=_gather(others(l + 1), mid_at=0.6) if l + 1 < nl else None)
        saved.append((xs, h1, u1, ac1, dp1, z1, o1, x1, h2, u2, uc2, hid2, o2, win_g, wout_g, wup_g, wdn_g))
        if l + 1 < nl:
            win_g, wout_g, wdn_g = _whole(nxt)
        xs = x2

    dx, sq = _loss_grad(xs, loss_target, tm_mix, name="loss_grad")
    loss = lax.psum(0.5 * jnp.sum(sq) / d, ("x", "y", "c"))

    flat = lambda a: a.reshape(nb * s, a.shape[-1])
    small = [None] * nl
    big_mlp, big_mix = [None] * nl, [None] * nl
    mlp = mix = None
    for l in reversed(range(nl)):
        x0, h1, u1, ac1, dp1, z1, o1, x1, h2, u2, uc2, hid2, o2, win_g, wout_g, wup_g, wdn_g = saved[l]
        (dx, du2, do2, rowd2, rowb2, dfw), got = _ffn_backward(
            dx, x1, o2, u2, uc2, mod_own[l], vec_d[l], fw_rows[l], wup_g, wdn_g, tm, name=f"ffn_bwd_{l}",
            exchange=_combine([mlp.chips(), mix.swap()]) if mlp else None)
        if mlp:
            mlp.after_chips(got[:2])
            mix.after_swap(got[2:])
        g_up, got = _weight_grad(flat(h2), flat(du2), N_CHIPS, "b", tt, name=f"grad_ffn_up_{l}",
                                 exchange=_combine([mlp.share(), mix.chips()]) if mlp else None)
        if mlp:
            big_mlp[l + 1] = mlp.result(got[:2])
            mix.after_chips(got[2:])
        g_dn, _ = _weight_grad(flat(hid2), flat(do2), 2, "a", tt, name=f"grad_ffn_down_{l}")
        mlp_above, mlp = mlp, _ReduceScatter([g_up, g_dn.reshape(N_CHIPS, -1, d)], cidx, idx, f"mlp_{l}")
        if l == 0:
            mlp.after_swap(_run_exchange(mlp.swap(), name="rs_swap_mlp_0"))
        first = mlp.swap() if l > 0 else mlp.chips()
        (dx, du1, do1, rowd1, rowb1, rowc, dcw, dpw), got = _mixer_backward(
            dx, x0, o1, u1, ac1, dp1, mod_own[l], vec_d[l], vec_c[l], cw_pad[l], pw_b[l], win_g, wout_g, taps, tm_mix,
            name=f"mixer_bwd_{l}", exchange=_combine([first, mix.share()]) if mlp_above else first)
        if l > 0:
            mlp.after_swap(got[:2])
        else:
            mlp.after_chips(got[:2])
        if mlp_above:
            big_mix[l + 1] = mix.result(got[2:])
        g_in, got = _weight_grad(flat(h1), flat(du1), N_CHIPS, "cols", tt, name=f"grad_w_in_{l}",
                                 exchange=mlp.share() if l == 0 else None)
        if l == 0:
            big_mlp[0] = mlp.result(got)
        g_out, _ = _weight_grad(flat(z1), flat(do1), 1, "cols", tt, name=f"grad_w_out_{l}")
        mix = _ReduceScatter([g_in, g_out.reshape(N_CHIPS, -1, d)], cidx, idx, f"mix_{l}")
        small[l] = dict(rowd=rowd1 + rowd2, rowb=rowb1 + rowb2, rowc=rowc, dcw=dcw[:taps], dpw=dpw, dfw=dfw)
    mix.after_swap(_run_exchange(mix.swap(), name="rs_swap_mix_0"))

    dmod_own = jnp.stack([small[l]["rowb"][:, :nmod, :] for l in range(nl)])
    dmod_all, got = _gather8(_rows128(dmod_own), name="gather_dmod", exchange=mix.chips())
    mix.after_chips(got)
    big_mix[0] = mix.result(_run_exchange(mix.share(), name="rs_share_mix_0"))
    dmod_all = dmod_all.reshape(N_DEV, nl, nb, nmod * d)
    dmod_all = dmod_all.transpose(1, 0, 2, 3).reshape(nl, N_DEV * nb, nmod * d)
    dmod_cols = lax.dynamic_slice_in_dim(dmod_all, jm * acol, acol, axis=2)
    (g_ada_w, d_ada_w, nm_ada_w, nv_ada_w), _ = _ada_update(c_all, dmod_cols, ada_w, m_ada_w, v_ada_w, name="ada_update")

    def st(key, row=None):
        return jnp.stack([small[l][key] if row is None else small[l][key][row] for l in range(nl)])

    local = {
        "ada_b": dmod_own.sum(axis=1).reshape(nl, nmod * d),
        "pre_mix_g": st("rowd", 0), "post_mix_g": st("rowd", 1),
        "conv_b": st("rowc", 0), "conv_ln_g": st("rowc", 1), "conv_ln_b": st("rowc", 2),
        "pool_w": st("dpw"), "pool_scale": st("rowc", 3),
        "pre_ffn_g": st("rowd", 2), "post_ffn_g": st("rowd", 3),
        "ffn_conv_b": st("dfw", 3), "conv_w": st("dcw"), "ffn_conv_w": jnp.stack([small[l]["dfw"][:3] for l in range(nl)]),
    }
    names = list(local)
    sizes = [local[k].size for k in names]
    pad = -sum(sizes) % (4 * SUBLANES * LANES)
    packed = jnp.concatenate([local[k].reshape(-1) for k in names] + [jnp.zeros((pad,), F32)])
    summed = _allreduce8(_rows128(packed), name="allreduce_small").reshape(-1)
    grads, off = {}, 0
    for k, sz in zip(names, sizes):
        grads[k] = summed[off:off + sz].reshape(local[k].shape)
        off += sz
    grads["conv_w"] = lax.dynamic_slice_in_dim(grads["conv_w"], jm * ccol, ccol, axis=2)
    grads["ffn_conv_w"] = lax.dynamic_slice_in_dim(grads["ffn_conv_w"], jm * fcol, fcol, axis=2)

    params = dict(ada_b=(ada_b, m_ada_b, v_ada_b), pre_mix_g=(pre_mix_g, m_pre_mix_g, v_pre_mix_g),
                  post_mix_g=(post_mix_g, m_post_mix_g, v_post_mix_g), conv_b=(conv_b, m_conv_b, v_conv_b),
                  conv_ln_g=(conv_ln_g, m_conv_ln_g, v_conv_ln_g), conv_ln_b=(conv_ln_b, m_conv_ln_b, v_conv_ln_b),
                  pool_w=(pool_w, m_pool_w, v_pool_w), pool_scale=(pool_scale, m_pool_scale, v_pool_scale),
                  pre_ffn_g=(pre_ffn_g, m_pre_ffn_g, v_pre_ffn_g), post_ffn_g=(post_ffn_g, m_post_ffn_g, v_post_ffn_g),
                  ffn_conv_b=(ffn_conv_b, m_ffn_conv_b, v_ffn_conv_b), conv_w=(conv_w, m_conv_w, v_conv_w),
                  ffn_conv_w=(ffn_conv_w, m_ffn_conv_w, v_ffn_conv_w))
    pack = lambda i, g=None: _rows128(jnp.concatenate([(grads[k] if g else params[k][i]).reshape(-1) for k in names]))
    sd, sm, sv = _adamw_flat(pack(0), pack(0, True), pack(1), pack(2), name="adamw_small")
    outs = {}
    off = 0
    for k in names:
        shape, sz = params[k][0].shape, params[k][0].size
        outs[k] = (grads[k],) + tuple(a.reshape(-1)[off:off + sz].reshape(shape) for a in (sd, sm, sv))
        off += sz

    outs["ada_w"] = (g_ada_w, d_ada_w, nm_ada_w, nv_ada_w)
    for k, w, m, v, gs in [("w_in", w_in, m_w_in, v_w_in, [big_mix[l][0] for l in range(nl)]),
                           ("w_out", w_out, m_w_out, v_w_out, [big_mix[l][1] for l in range(nl)]),
                           ("ffn_up", ffn_up, m_ffn_up, v_ffn_up, [big_mlp[l][0] for l in range(nl)]),
                           ("ffn_down", ffn_down, m_ffn_down, v_ffn_down, [big_mlp[l][1] for l in range(nl)])]:
        outs[k] = tuple(_adamw_layers(w, m, v, gs, name=f"adamw_{k}"))

    order = ["ada_w", "ada_b", "pre_mix_g", "post_mix_g", "w_in", "conv_w", "conv_b", "conv_ln_g", "conv_ln_b", "pool_w",
             "pool_scale", "w_out", "pre_ffn_g", "post_ffn_g", "ffn_up", "ffn_conv_w", "ffn_conv_b", "ffn_down"]
    return (loss, dx) + tuple(outs[k][i] for i in range(4) for k in order)
```

```python
import functools

import jax
import jax.numpy as jnp
from jax import lax
from jax.experimental import pallas as pl
from jax.experimental.pallas import tpu as pltpu

F32 = jnp.float32
BF16 = jnp.bfloat16
MESH = pl.DeviceIdType.MESH

EPS = 1e-6
POOL_WINDOWS = (2, 4, 8, 16)
ADAM_LR = 0.001
ADAM_B1 = 0.9
ADAM_B2 = 0.999
ADAM_EPS = 1e-08
ADAM_WD = 0.01
ADAM_STEP = 10

N_CHIPS = 4
N_DEV = 8
LANES = 128
SUBLANES = 8
HALO = 32
FHALO = 8
VMEM_LIMIT = 60 * 1024 * 1024
MLP_TILE_ROWS = 256
MIXER_TILE_ROWS = 512
GRAD_CHUNK_ROWS = 2048


def _pos():
    return lax.axis_index("x"), lax.axis_index("y"), lax.axis_index("c")


def _flip(v, f):
    return 1 - v if f else v


def _full(shape):
    nd = len(shape)
    return pl.BlockSpec(shape, lambda *_: (0,) * nd)


_ANY = pl.BlockSpec(memory_space=pl.ANY)
_VMEM = pl.BlockSpec(memory_space=pltpu.VMEM)


def _sigmoid(v):
    return 1.0 / (1.0 + jnp.exp(-v))


def _colsum(v):
    return jnp.sum(v, axis=0, keepdims=True)


def _gather8(v, name, exchange=None):
    r, ccols = v.shape
    ex = exchange
    nci, nco = (len(ex.ins), len(ex.outs)) if ex else (0, 0)

    def body(*refs):
        v_ref, cin, out_ref, cout = refs[0], refs[1:1 + nci], refs[1 + nci], refs[2 + nci:2 + nci + nco]
        send_sems, recv_sems, local_sem = refs[2 + nci + nco:5 + nci + nco]
        if ex:
            sems = _Sems(*refs[5 + nci + nco:])
            ex.start(cin, cout, sems)
        x, y, c = _pos()
        me = 4 * x + 2 * y + c
        mine = pltpu.make_async_copy(v_ref, out_ref.at[me], local_sem)
        mine.start()
        peers = [(_flip(x, (k >> 2) & 1), _flip(y, (k >> 1) & 1), _flip(c, k & 1)) for k in range(1, N_DEV)]
        sends = []
        for k, peer in enumerate(peers):
            cp = pltpu.make_async_remote_copy(src_ref=v_ref, dst_ref=out_ref.at[me], send_sem=send_sems.at[k],
                                              recv_sem=recv_sems.at[k], device_id=peer, device_id_type=MESH)
            cp.start()
            sends.append(cp)
        for k, peer in enumerate(peers):
            pidx = 4 * peer[0] + 2 * peer[1] + peer[2]
            pltpu.make_async_remote_copy(src_ref=v_ref, dst_ref=out_ref.at[pidx], send_sem=send_sems.at[k],
                                         recv_sem=recv_sems.at[k], device_id=peer, device_id_type=MESH).wait_recv()
        for cp in sends:
            cp.wait_send()
        mine.wait()
        if ex:
            if ex.mid is not None:
                ex.mid(cin, cout, sems)
            ex.finish(cin, cout, sems)

    outs = pl.pallas_call(
        body, name=name, out_shape=[jax.ShapeDtypeStruct((N_DEV, r, ccols), v.dtype)] + (ex.outs if ex else []),
        in_specs=[_VMEM] + [_ANY] * nci, out_specs=[_VMEM] + [_ANY] * nco,
        scratch_shapes=[pltpu.SemaphoreType.DMA((N_DEV - 1,)), pltpu.SemaphoreType.DMA((N_DEV - 1,)),
                        pltpu.SemaphoreType.DMA(())] + (ex.scratch() if ex else []),
        input_output_aliases={1 + a: 1 + b for a, b in ex.aliases.items()} if ex else {},
        compiler_params=pltpu.CompilerParams(vmem_limit_bytes=VMEM_LIMIT),
    )(v, *(ex.ins if ex else []))
    return (outs[0], list(outs[1:])) if ex else outs[0]


def _rows_to_owners(v, name):
    _, r, ccols = v.shape

    def body(v_ref, out_ref, send_sems, recv_sems, local_sem):
        x, y, c = _pos()
        jm = 2 * x + y
        mine = pltpu.make_async_copy(v_ref.at[2 * jm + c], out_ref.at[jm], local_sem)
        mine.start()
        peers, pjs = _chip_peers(x, y, c)
        sends = []
        for k, peer in enumerate(peers):
            cp = pltpu.make_async_remote_copy(src_ref=v_ref.at[2 * pjs[k] + c], dst_ref=out_ref.at[jm],
                                              send_sem=send_sems.at[k], recv_sem=recv_sems.at[k],
                                              device_id=peer, device_id_type=MESH)
            cp.start()
            sends.append(cp)
        for k, peer in enumerate(peers):
            pltpu.make_async_remote_copy(src_ref=v_ref.at[0], dst_ref=out_ref.at[pjs[k]], send_sem=send_sems.at[k],
                                         recv_sem=recv_sems.at[k], device_id=peer, device_id_type=MESH).wait_recv()
        for cp in sends:
            cp.wait_send()
        mine.wait()

    return pl.pallas_call(
        body, name=name, out_shape=jax.ShapeDtypeStruct((N_CHIPS, r, ccols), v.dtype),
        in_specs=[_VMEM], out_specs=_VMEM,
        scratch_shapes=[pltpu.SemaphoreType.DMA((N_CHIPS - 1,)), pltpu.SemaphoreType.DMA((N_CHIPS - 1,)),
                        pltpu.SemaphoreType.DMA(())],
        compiler_params=pltpu.CompilerParams(vmem_limit_bytes=VMEM_LIMIT),
    )(v)


def _allreduce8(v, name):
    r, ccols = v.shape
    h = r // 2
    q = h // 2

    def body(v_ref, out_ref, whole, part, done, send_sems, recv_sems):
        x, y, c = _pos()
        sib = (x, y, 1 - c)
        mine = pl.ds(pl.multiple_of(c * h, SUBLANES), h)
        theirs = pl.ds(pl.multiple_of((1 - c) * h, SUBLANES), h)
        quarters = [pl.ds(pl.multiple_of(c * h + k * q, SUBLANES), q) for k in range(2)]
        along_x, along_y = (1 - x, y, c), (x, 1 - y, c)

        def exchange(pairs):
            cps = [pltpu.make_async_remote_copy(src_ref=src, dst_ref=dst, send_sem=send_sems.at[k], recv_sem=recv_sems.at[k],
                                                device_id=peer, device_id_type=MESH) for src, dst, k, peer in pairs]
            for cp in cps:
                cp.start()
            for cp in cps:
                cp.wait()

        exchange([(v_ref, whole, 0, sib)])
        out_ref[...] = v_ref[...] + whole[...]
        for stage, peers in enumerate(((along_x, along_y), (along_y, along_x))):
            exchange([(out_ref.at[quarters[k]], part.at[2 * stage + k], 1 + 2 * stage + k, peers[k]) for k in range(2)])
            for k in range(2):
                out_ref[quarters[k], :] = out_ref[quarters[k], :] + part[2 * stage + k]
        exchange([(out_ref.at[mine], done, 5, sib)])
        out_ref[theirs, :] = done[...]

    return pl.pallas_call(
        body, name=name, out_shape=jax.ShapeDtypeStruct((r, ccols), v.dtype),
        in_specs=[_VMEM], out_specs=_VMEM,
        scratch_shapes=[pltpu.VMEM((r, ccols), v.dtype), pltpu.VMEM((4, q, ccols), v.dtype), pltpu.VMEM((h, ccols), v.dtype),
                        pltpu.SemaphoreType.DMA((6,)), pltpu.SemaphoreType.DMA((6,))],
        compiler_params=pltpu.CompilerParams(vmem_limit_bytes=VMEM_LIMIT),
    )(v)


def _chip_peers(x, y, c):
    peers = [(_flip(x, (k >> 1) & 1), _flip(y, k & 1), c) for k in range(1, N_CHIPS)]
    return peers, [2 * p[0] + p[1] for p in peers]


class _Exchange:
    def __init__(self, ins, outs, aliases, n_sems, n_local, start, finish, mid=None, mid_at=1.0, sibling=False, chips=False):
        self.ins, self.outs, self.aliases = list(ins), list(outs), dict(aliases)
        self.n_sems, self.n_local, self.start, self.finish = n_sems, n_local, start, finish
        self.mid, self.mid_at = mid, mid_at
        self.sibling, self.chips = sibling, chips

    def collective_id(self):
        return {(True, False): 1, (False, True): 2, (True, True): 3}[(self.sibling, self.chips)]

    def handshake(self):
        x, y, c = _pos()
        peers = ([(x, y, 1 - c)] if self.sibling else []) + (_chip_peers(x, y, c)[0] if self.chips else [])
        barrier = pltpu.get_barrier_semaphore()
        for peer in peers:
            pl.semaphore_signal(barrier, inc=1, device_id=peer, device_id_type=MESH)
        pl.semaphore_wait(barrier, len(peers))

    def scratch(self):
        return [pltpu.SemaphoreType.DMA((self.n_sems,)), pltpu.SemaphoreType.DMA((self.n_sems,)),
                pltpu.SemaphoreType.DMA((max(self.n_local, 1),))]


class _Sems:
    def __init__(self, send, recv, local, base=0, lbase=0):
        self.send, self.recv, self.loc, self.base, self.lbase = send, recv, local, base, lbase

    def shifted(self, by, lby):
        return _Sems(self.send, self.recv, self.loc, self.base + by, self.lbase + lby)

    def local(self, k):
        return self.loc.at[self.lbase + k]


def _remote(src, dst, sems, k, peer):
    return pltpu.make_async_remote_copy(src_ref=src, dst_ref=dst, send_sem=sems.send.at[sems.base + k],
                                        recv_sem=sems.recv.at[sems.base + k], device_id=peer, device_id_type=MESH)


def _combine(exs):
    ins = [a for ex in exs for a in ex.ins]
    outs = [o for ex in exs for o in ex.outs]
    aliases, spans, ni, no, ns, nloc = {}, [], 0, 0, 0, 0
    for ex in exs:
        aliases.update({ni + a: no + b for a, b in ex.aliases.items()})
        spans.append((ni, no, ns, nloc))
        ni, no, ns, nloc = ni + len(ex.ins), no + len(ex.outs), ns + ex.n_sems, nloc + ex.n_local

    def each(which):
        def run(ins_, outs_, sems):
            for ex, (i0, o0, s0, l0) in zip(exs, spans):
                stage = getattr(ex, which)
                if stage is not None:
                    stage(ins_[i0:i0 + len(ex.ins)], outs_[o0:o0 + len(ex.outs)], sems.shifted(s0, l0))
        return run

    mids = [ex.mid_at for ex in exs if ex.mid is not None]
    return _Exchange(ins, outs, aliases, ns, nloc, each("start"), each("finish"),
                     mid=each("mid") if mids else None, mid_at=max(mids) if mids else 1.0,
                     sibling=any(ex.sibling for ex in exs), chips=any(ex.chips for ex in exs))


def _gather(shards, mid_at=1.0):
    n = len(shards)
    per = N_CHIPS - 1
    halves = [s.reshape(2, s.shape[0] // 2, s.shape[1]) for s in shards]

    def copies(ins, outs, sems):
        x, y, c = _pos()
        jm = 2 * x + y
        sib = (x, y, 1 - c)
        peers, pjs = _chip_peers(x, y, c)
        sends, recvs, passes, passed = [], [], [], []
        for a in range(n):
            own = _remote(ins[a], outs[a].at[jm], sems, 2 * n * per + a, sib)
            sends.append(own)
            passed.append(own)
            for k, peer in enumerate(peers):
                landed, theirs = outs[a].at[pjs[k], c], outs[a].at[pjs[k], 1 - c]
                sends.append(_remote(ins[a].at[c], outs[a].at[jm, c], sems, 2 * (a * per + k), peer))
                recvs.append(_remote(landed, landed, sems, 2 * (a * per + k), peer))
                passes.append(_remote(landed, landed, sems, 2 * (a * per + k) + 1, sib))
                passed.append(_remote(theirs, theirs, sems, 2 * (a * per + k) + 1, sib))
        return sends, recvs, passes, passed

    def start(ins, outs, sems):
        for cp in copies(ins, outs, sems)[0]:
            cp.start()

    def mid(ins, outs, sems):
        _, recvs, passes, _ = copies(ins, outs, sems)
        for got, fwd in zip(recvs, passes):
            got.wait_recv()
            fwd.start()

    def finish(ins, outs, sems):
        sends, _, passes, passed = copies(ins, outs, sems)
        for cp in passed:
            cp.wait_recv()
        for cp in sends + passes:
            cp.wait_send()

    outs = [jax.ShapeDtypeStruct((N_CHIPS,) + h.shape, h.dtype) for h in halves]
    return _Exchange(halves, outs, {}, 2 * n * per + n, 0, start, finish, mid=mid, mid_at=mid_at, sibling=True, chips=True)


def _whole(gathered):
    return [g.reshape(g.shape[0], g.shape[1] * g.shape[2], g.shape[3]) for g in gathered]


def _swap_halves(gs):
    n = len(gs)
    halves = [g.reshape(g.shape[0], 2, g.shape[1] // 2, g.shape[2]) for g in gs]

    def copies(ins, outs, sems):
        x, y, c = _pos()
        sib = (x, y, 1 - c)
        return [_remote(ins[a].at[:, 1 - c], outs[a], sems, a, sib) for a in range(n)]

    def start(ins, outs, sems):
        for cp in copies(ins, outs, sems):
            cp.start()

    def finish(ins, outs, sems):
        for cp in copies(ins, outs, sems):
            cp.wait()

    outs = [jax.ShapeDtypeStruct((g.shape[0], g.shape[1] // 2, g.shape[2]), g.dtype) for g in gs]
    return _Exchange(halves, outs, {}, n, 0, start, finish, sibling=True)


def _chip_exchange(ps):
    n = len(ps)
    per = N_CHIPS - 1

    def copies(ins, outs, sems):
        x, y, c = _pos()
        peers, pjs = _chip_peers(x, y, c)
        return [_remote(ins[a].at[pjs[k]], outs[a].at[k], sems, a * per + k, peer)
                for a in range(n) for k, peer in enumerate(peers)]

    def start(ins, outs, sems):
        for cp in copies(ins, outs, sems):
            cp.start()

    def finish(ins, outs, sems):
        for cp in copies(ins, outs, sems):
            cp.wait()

    outs = [jax.ShapeDtypeStruct((per,) + p.shape[1:], p.dtype) for p in ps]
    return _Exchange(ps, outs, {}, n * per, 0, start, finish, chips=True)


def _sibling_share(fs):
    n = len(fs)

    def copies(outs, sems):
        x, y, c = _pos()
        sib = (x, y, 1 - c)
        sends = [_remote(outs[a].at[c], outs[a].at[c], sems, a, sib) for a in range(n)]
        recvs = [_remote(outs[a].at[1 - c], outs[a].at[1 - c], sems, a, sib) for a in range(n)]
        return sends, recvs

    def start(ins, outs, sems):
        for cp in copies(outs, sems)[0]:
            cp.start()

    def finish(ins, outs, sems):
        sends, recvs = copies(outs, sems)
        for cp in recvs:
            cp.wait_recv()
        for cp in sends:
            cp.wait_send()

    outs = [jax.ShapeDtypeStruct(f.shape, f.dtype) for f in fs]
    return _Exchange(fs, outs, {a: a for a in range(n)}, n, 0, start, finish, sibling=True)


def _run_exchange(ex, name):
    ni, no = len(ex.ins), len(ex.outs)

    def body(*refs):
        ins, outs, sems = refs[:ni], refs[ni:ni + no], _Sems(*refs[ni + no:])
        ex.handshake()
        ex.start(ins, outs, sems)
        if ex.mid is not None:
            ex.mid(ins, outs, sems)
        ex.finish(ins, outs, sems)

    return pl.pallas_call(
        body, name=name, out_shape=ex.outs, in_specs=[_ANY] * ni, out_specs=[_ANY] * no,
        input_output_aliases=ex.aliases, scratch_shapes=ex.scratch(),
        compiler_params=pltpu.CompilerParams(collective_id=ex.collective_id()),
    )(*ex.ins)


def _grid_call(body, name, grid, in_specs, out_specs, out_shape, scratch_shapes, args, exchange=None):
    ni, no = len(in_specs), len(out_specs)
    params = pltpu.CompilerParams(dimension_semantics=("arbitrary",) * len(grid), vmem_limit_bytes=VMEM_LIMIT)
    if exchange is None:
        outs = pl.pallas_call(body, name=name, grid=grid, in_specs=in_specs, out_specs=out_specs, out_shape=out_shape,
                              scratch_shapes=scratch_shapes, compiler_params=params)(*args)
        return list(outs), []
    ex = exchange
    nci, nco = len(ex.ins), len(ex.outs)

    def hosted(*refs):
        cin = refs[ni:ni + nci]
        cout = refs[ni + nci + no:ni + nci + no + nco]
        sems = _Sems(*refs[len(refs) - 3:])
        main = refs[:ni] + refs[ni + nci:ni + nci + no] + refs[ni + nci + no + nco:len(refs) - 3]
        ids = [pl.program_id(a) for a in range(len(grid))]
        first = functools.reduce(lambda p, q: p & q, [i == 0 for i in ids])
        last = functools.reduce(lambda p, q: p & q, [i == g - 1 for i, g in zip(ids, grid)])

        @pl.when(first)
        def _():
            ex.handshake()
            ex.start(cin, cout, sems)

        if ex.mid is not None:
            steps = functools.reduce(lambda p, q: p * q, grid)
            flat = functools.reduce(lambda p, q: p * q[1] + q[0], zip(ids[1:], grid[1:]), ids[0])

            @pl.when(flat == min(steps - 1, int(ex.mid_at * steps)))
            def _():
                ex.mid(cin, cout, sems)

        body(*main)

        @pl.when(last)
        def _():
            ex.finish(cin, cout, sems)

    outs = pl.pallas_call(
        hosted, name=name, grid=grid, in_specs=list(in_specs) + [_ANY] * nci, out_specs=list(out_specs) + [_ANY] * nco,
        out_shape=list(out_shape) + ex.outs, scratch_shapes=list(scratch_shapes) + ex.scratch(),
        input_output_aliases={ni + a: no + b for a, b in ex.aliases.items()},
        compiler_params=pltpu.CompilerParams(dimension_semantics=("arbitrary",) * len(grid), vmem_limit_bytes=VMEM_LIMIT,
                                             collective_id=ex.collective_id()),
    )(*args, *ex.ins)
    return list(outs[:no]), list(outs[no:])


SUM_BLOCK_BYTES = 4 * 1024 * 1024
ADAM_BLOCK_BYTES = 2 * 1024 * 1024


def _row_tile(rows, cols, itemsize, budget):
    best = None
    for t in range(16, rows + 1, 16):
        if rows % t == 0 and t * cols * itemsize <= budget:
            best = t
    return best if best is not None else rows


def _pair_sum(g, r1, cidx, name):
    nj, r, ccols = g.shape
    hr = r // 2
    tr = _row_tile(hr, ccols, 4, SUM_BLOCK_BYTES)
    nt = hr // tr

    def body(c_ref, g_ref, r_ref, o_ref):
        o_ref[...] = (g_ref[...].astype(F32) + r_ref[...].astype(F32)).astype(o_ref.dtype)

    return pl.pallas_call(
        body, name=name, out_shape=jax.ShapeDtypeStruct((nj, hr, ccols), g.dtype),
        grid_spec=pltpu.PrefetchScalarGridSpec(
            num_scalar_prefetch=1, grid=(nj, nt),
            in_specs=[pl.BlockSpec((None, tr, ccols), lambda j, i, c_ref: (j, c_ref[0] * nt + i, 0)),
                      pl.BlockSpec((None, tr, ccols), lambda j, i, c_ref: (j, i, 0))],
            out_specs=pl.BlockSpec((None, tr, ccols), lambda j, i, c_ref: (j, i, 0))),
        compiler_params=pltpu.CompilerParams(dimension_semantics=("arbitrary", "arbitrary"), vmem_limit_bytes=VMEM_LIMIT),
    )(cidx, g, r1)


def _chip_sum(p, r2, idx, name):
    nj, hr, ccols = p.shape
    tr = _row_tile(hr, ccols, 4, SUM_BLOCK_BYTES)
    nt = hr // tr

    def body(i_ref, p_ref, r_ref, o_ref):
        s = p_ref[...].astype(F32)
        for k in range(N_CHIPS - 1):
            s = s + r_ref[k].astype(F32)
        o_ref[...] = s

    return pl.pallas_call(
        body, name=name, out_shape=jax.ShapeDtypeStruct((2, hr, ccols), F32),
        grid_spec=pltpu.PrefetchScalarGridSpec(
            num_scalar_prefetch=1, grid=(nt,),
            in_specs=[pl.BlockSpec((None, tr, ccols), lambda i, i_ref: (i_ref[0], i, 0)),
                      pl.BlockSpec((N_CHIPS - 1, tr, ccols), lambda i, i_ref: (0, i, 0))],
            out_specs=pl.BlockSpec((None, tr, ccols), lambda i, i_ref: (i_ref[1], i, 0))),
        compiler_params=pltpu.CompilerParams(dimension_semantics=("arbitrary",), vmem_limit_bytes=VMEM_LIMIT),
    )(idx, p, r2)


def _adam_math(w, g, m, v):
    m2 = ADAM_B1 * m + (1.0 - ADAM_B1) * g
    v2 = ADAM_B2 * v + (1.0 - ADAM_B2) * (g * g)
    m_hat = m2 / (1.0 - ADAM_B1 ** ADAM_STEP)
    v_hat = v2 / (1.0 - ADAM_B2 ** ADAM_STEP)
    delta = -ADAM_LR * (m_hat / (jnp.sqrt(v_hat) + ADAM_EPS) + ADAM_WD * w)
    return delta, m2, v2


def _adamw_layers(w, m, v, gs, name):
    nl, r, ccols = w.shape
    ng = len(gs)
    tr = _row_tile(r, ccols, 4, ADAM_BLOCK_BYTES)
    nt = r // tr

    def body(w_ref, m_ref, v_ref, *rest):
        g_refs, (go_ref, d_ref, mo_ref, vo_ref) = rest[:ng], rest[ng:]
        l = pl.program_id(0)
        g = g_refs[0][...]
        for k in range(1, ng):
            g = jnp.where(l == k, g_refs[k][...], g)
        delta, m2, v2 = _adam_math(w_ref[...], g, m_ref[...], v_ref[...])
        go_ref[...] = g
        d_ref[...] = delta
        mo_ref[...] = m2
        vo_ref[...] = v2

    big = pl.BlockSpec((None, tr, ccols), lambda l, i: (l, i, 0))

    def gspec(k):
        return pl.BlockSpec((tr, ccols), lambda l, i: (jnp.where(l == k, i, jnp.where(l < k, 0, nt - 1)), 0))

    assert ng == nl
    return _grid_call(body, name, (nl, nt), in_specs=[big, big, big] + [gspec(k) for k in range(ng)],
                      out_specs=[big, big, big, big], out_shape=[jax.ShapeDtypeStruct(w.shape, F32)] * 4,
                      scratch_shapes=[], args=(w, m, v, *gs))[0]


def _adamw_flat(w, g, m, v, name):
    r, ccols = w.shape

    def body(w_ref, g_ref, m_ref, v_ref, d_ref, mo_ref, vo_ref):
        delta, m2, v2 = _adam_math(w_ref[...], g_ref[...], m_ref[...], v_ref[...])
        d_ref[...] = delta
        mo_ref[...] = m2
        vo_ref[...] = v2

    return pl.pallas_call(
        body, name=name, out_shape=[jax.ShapeDtypeStruct((r, ccols), F32)] * 3,
        in_specs=[_VMEM] * 4, out_specs=[_VMEM] * 3,
        compiler_params=pltpu.CompilerParams(vmem_limit_bytes=VMEM_LIMIT),
    )(w, g, m, v)


def _ada_forward(c_all, ada_w, ada_b_cols, name, exchange=None):
    nl, d, ncols = ada_w.shape
    bg = c_all.shape[0]
    tn = 512 if ncols % 512 == 0 else ncols

    def body(c_ref, w_ref, b_ref, o_ref):
        cv = c_ref[...]
        ca = (cv * _sigmoid(cv)).astype(BF16)
        o_ref[...] = jnp.dot(ca, w_ref[...].astype(BF16), preferred_element_type=F32) + b_ref[...]

    outs, got = _grid_call(
        body, name, (nl, ncols // tn),
        in_specs=[pl.BlockSpec((bg, d), lambda l, j: (0, 0)),
                  pl.BlockSpec((None, d, tn), lambda l, j: (l, 0, j)),
                  pl.BlockSpec((None, 1, tn), lambda l, j: (l, 0, j))],
        out_specs=[pl.BlockSpec((None, bg, tn), lambda l, j: (l, 0, j))],
        out_shape=[jax.ShapeDtypeStruct((nl, bg, ncols), F32)], scratch_shapes=[], args=(c_all, ada_w, ada_b_cols),
        exchange=exchange)
    return outs[0], got


def _ada_update(c_all, dmod_cols, w, m, v, name, exchange=None):
    nl, d, ncols = w.shape
    bg = c_all.shape[0]
    tn = 512 if ncols % 512 == 0 else ncols

    def body(c_ref, dm_ref, w_ref, m_ref, v_ref, go_ref, d_ref, mo_ref, vo_ref):
        cv = c_ref[...]
        ca = (cv * _sigmoid(cv)).astype(BF16)
        g = lax.dot_general(ca, dm_ref[...].astype(BF16), (((0,), (0,)), ((), ())), preferred_element_type=F32)
        delta, m2, v2 = _adam_math(w_ref[...], g, m_ref[...], v_ref[...])
        go_ref[...] = g
        d_ref[...] = delta
        mo_ref[...] = m2
        vo_ref[...] = v2

    big = pl.BlockSpec((None, d, tn), lambda l, j: (l, 0, j))
    return _grid_call(
        body, name, (nl, ncols // tn),
        in_specs=[pl.BlockSpec((bg, d), lambda l, j: (0, 0)),
                  pl.BlockSpec((None, bg, tn), lambda l, j: (l, 0, j)), big, big, big],
        out_specs=[big, big, big, big], out_shape=[jax.ShapeDtypeStruct(w.shape, F32)] * 4,
        scratch_shapes=[], args=(c_all, dmod_cols, w, m, v), exchange=exchange)


def _load_weights(first, pairs, sems):
    @pl.when(first)
    def _():
        cps = [pltpu.make_async_copy(src, dst, sems.at[k]) for k, (src, dst) in enumerate(pairs)]
        for cp in cps:
            cp.start()
        for cp in cps:
            cp.wait()


def _ada_norm(xv, g, sc, sh):
    r = lax.rsqrt(jnp.mean(xv * xv, axis=-1, keepdims=True) + EPS)
    xn = xv * r
    return (xn * g) * (1.0 + sc) + sh, xn, r


def _ada_norm_bwd(dh, xn, r, g, sc):
    d_sh = _colsum(dh)
    d_sc = _colsum(dh * (xn * g))
    dxg = dh * (1.0 + sc)
    d_g = _colsum(dxg * xn)
    gd = dxg * g
    dx = r * (gd - xn * jnp.mean(gd * xn, axis=-1, keepdims=True))
    return dx, d_sh, d_sc, d_g


def _gated_residual_bwd(dxo, o, g_post, gt):
    r = lax.rsqrt(jnp.mean(o * o, axis=-1, keepdims=True) + EPS)
    on = o * r
    d_gt = _colsum(dxo * (on * g_post))
    dy = dxo * (1.0 + gt)
    d_gp = _colsum(dy * on)
    gd = dy * g_post
    do = r * (gd - on * jnp.mean(gd * on, axis=-1, keepdims=True))
    return do, d_gt, d_gp


def _seq_positions(i, tm, width):
    return i * tm + lax.broadcasted_iota(jnp.int32, (tm, width), 0)


def _fill_phases(ext, phases):
    rows = ext.shape[0]
    ev = ext[...]
    for r in range(1, SUBLANES):
        phases[r - 1] = pltpu.roll(ev, rows - r, axis=0)


def _shifted_rows(ext, phases, offset, n):
    q, r = divmod(offset, SUBLANES)
    if r == 0:
        return ext[pl.ds(q * SUBLANES, n), :]
    return phases[r - 1, pl.ds(q * SUBLANES, n), :]


def _rows_before(halo, cur, shift):
    e = jnp.concatenate([halo, cur], axis=0)
    return pltpu.roll(e, shift, axis=0)[halo.shape[0]:, :]


def _rows_after(cur, halo, shift):
    e = jnp.concatenate([cur, halo], axis=0)
    return pltpu.roll(e, e.shape[0] - shift, axis=0)[:cur.shape[0], :]


def _mixer_forward(x, mod, vec_d, vec_c, cw, pw, win_g, wout_g, taps, tm, name, exchange=None):
    nb, s, d = x.shape
    n = s // tm
    nj, _, dcol = win_g.shape
    din = nj * dcol
    dc = vec_c.shape[-1]
    dpool = din - 2 * dc
    dmix = dc + dpool
    ro = wout_g.shape[1]
    ngrp = dpool // LANES

    def body(x_ref, mod_ref, vd_ref, vc_ref, cw_ref, pw_ref, win_hbm, wout_hbm,
             xo_ref, h_ref, u_ref, ac_ref, dp_ref, z_ref, o_ref,
             win_v, wout_v, ext_a, ext_p, phases, sems):
        b, i = pl.program_id(0), pl.program_id(1)
        pairs = [(win_hbm.at[j], win_v.at[:, pl.ds(j * dcol, dcol)]) for j in range(nj)]
        pairs += [(wout_hbm.at[j], wout_v.at[pl.ds(j * ro, ro), :]) for j in range(nj)]
        _load_weights((b == 0) & (i == 0), pairs, sems)

        xv = x_ref[...]
        h, _, _ = _ada_norm(xv, vd_ref[0:1, :], mod_ref[1:2, :], mod_ref[0:1, :])
        hb = h.astype(BF16)
        h_ref[...] = hb
        u = jnp.dot(hb, win_v[...], preferred_element_type=F32)
        u_ref[...] = u.astype(BF16)
        ag = u[:, :dc] * _sigmoid(u[:, dc:2 * dc])
        up = u[:, 2 * dc:]

        @pl.when(i == 0)
        def _():
            ext_a[0:HALO, :] = jnp.zeros((HALO, dc), F32)
            ext_p[0:HALO, :] = jnp.zeros((HALO, dpool), F32)

        @pl.when(i > 0)
        def _():
            ext_a[0:HALO, :] = ext_a[tm:tm + HALO, :]
            ext_p[0:HALO, :] = ext_p[tm:tm + HALO, :]

        ext_a[HALO:HALO + tm, :] = ag
        ext_p[HALO:HALO + tm, :] = up

        acc = jnp.broadcast_to(vc_ref[0:1, :], (tm, dc))
        _fill_phases(ext_a, phases)
        for k in range(taps):
            acc = acc + cw_ref[k:k + 1, :] * _shifted_rows(ext_a, phases, HALO - (taps - 1) + k, tm)
        ac_ref[...] = acc.astype(BF16)
        mu = jnp.mean(acc, axis=-1, keepdims=True)
        xc = acc - mu
        var = jnp.mean(xc * xc, axis=-1, keepdims=True)
        al = (xc * lax.rsqrt(var + EPS)) * vc_ref[1:2, :] + vc_ref[2:3, :]
        a = al * _sigmoid(al)

        pos = _seq_positions(i, tm, LANES)
        parts = [a.astype(BF16)]
        for g in range(ngrp):
            w = POOL_WINDOWS[g]
            cols = slice(g * LANES, (g + 1) * LANES)
            sw = ext_p[:, cols]
            step = 1
            while step < w:
                sw = sw + pltpu.roll(sw, step, axis=0)
                step *= 2
            cnt = jnp.minimum(pos + 1, w).astype(F32)
            dg = (sw[HALO:, :] / cnt - up[:, cols]).astype(BF16)
            dp_ref[:, cols] = dg
            q = jnp.dot(dg, pw_ref[g], preferred_element_type=F32)
            parts.append((q * vc_ref[3:4, cols]).astype(BF16))
        z = jnp.concatenate(parts, axis=-1)
        z_ref[...] = z
        o = jnp.dot(z, wout_v[...], preferred_element_type=F32)
        o_ref[...] = o
        r2 = lax.rsqrt(jnp.mean(o * o, axis=-1, keepdims=True) + EPS)
        xo_ref[...] = xv + (1.0 + mod_ref[2:3, :]) * ((o * r2) * vd_ref[1:2, :])

    def tile(width):
        return pl.BlockSpec((None, tm, width), lambda b, i: (b, i, 0))

    return _grid_call(
        body, name, (nb, n),
        in_specs=[tile(d), pl.BlockSpec((None, 8, d), lambda b, i: (b, 0, 0)), _full(vec_d.shape), _full(vec_c.shape),
                  _full(cw.shape), _full(pw.shape), _ANY, _ANY],
        out_specs=[tile(d), tile(d), tile(din), tile(dc), tile(dpool), tile(dmix), tile(d)],
        out_shape=[jax.ShapeDtypeStruct((nb, s, d), F32), jax.ShapeDtypeStruct((nb, s, d), BF16),
                   jax.ShapeDtypeStruct((nb, s, din), BF16), jax.ShapeDtypeStruct((nb, s, dc), BF16),
                   jax.ShapeDtypeStruct((nb, s, dpool), BF16), jax.ShapeDtypeStruct((nb, s, dmix), BF16),
                   jax.ShapeDtypeStruct((nb, s, d), F32)],
        scratch_shapes=[pltpu.VMEM((d, din), BF16), pltpu.VMEM((dmix, d), BF16),
                        pltpu.VMEM((HALO + tm, dc), F32), pltpu.VMEM((HALO + tm, dpool), F32),
                        pltpu.VMEM((SUBLANES - 1, HALO + tm, dc), F32), pltpu.SemaphoreType.DMA((2 * nj,))],
        args=(x, mod, vec_d, vec_c, cw, pw, win_g, wout_g), exchange=exchange)


def _mixer_backward(dxo, x, o, u, ac, dpl, mod, vec_d, vec_c, cw, pw, win_g, wout_g, taps, tm, name, exchange=None):
    nb, s, d = x.shape
    n = s // tm
    nj, _, dcol = win_g.shape
    din = nj * dcol
    dc = vec_c.shape[-1]
    dpool = din - 2 * dc
    dmix = dc + dpool
    ro = wout_g.shape[1]
    ngrp = dpool // LANES
    rext = tm + HALO

    def body(dxo_ref, x_ref, o_ref, u_ref, ac_ref, dp_ref, mod_ref, vd_ref, vc_ref, cw_ref, pw_ref, win_hbm, wout_hbm,
             dx_ref, du_ref, dob_ref, rowd_ref, rowb_ref, rowc_ref, dcw_ref, dpw_ref,
             win_v, wout_v, ext_a, ext_p, phases, sems):
        b, i = pl.program_id(0), pl.program_id(1)
        first = (b == 0) & (i == 0)
        pairs = [(win_hbm.at[j], win_v.at[:, pl.ds(j * dcol, dcol)]) for j in range(nj)]
        pairs += [(wout_hbm.at[j], wout_v.at[pl.ds(j * ro, ro), :]) for j in range(nj)]
        _load_weights(first, pairs, sems)

        @pl.when(first)
        def _():
            rowd_ref[...] = jnp.zeros_like(rowd_ref)
            rowc_ref[...] = jnp.zeros_like(rowc_ref)
            dcw_ref[...] = jnp.zeros_like(dcw_ref)
            dpw_ref[...] = jnp.zeros_like(dpw_ref)

        @pl.when(i == 0)
        def _():
            rowb_ref[...] = jnp.zeros_like(rowb_ref)
            ext_a[tm:rext, :] = jnp.zeros((HALO, dc), F32)
            ext_p[tm:rext, :] = jnp.zeros((HALO, dpool), F32)

        @pl.when(i > 0)
        def _():
            ext_a[tm:rext, :] = ext_a[0:HALO, :]
            ext_p[tm:rext, :] = ext_p[0:HALO, :]

        g_pre, g_post = vd_ref[0:1, :], vd_ref[1:2, :]
        sh, sc, gt = mod_ref[0:1, :], mod_ref[1:2, :], mod_ref[2:3, :]
        do, d_gt, d_gp = _gated_residual_bwd(dxo_ref[...], o_ref[...], g_post, gt)
        dob = do.astype(BF16)
        dob_ref[...] = dob
        dz = lax.dot_general(dob, wout_v[...], (((1,), (1,)), ((), ())), preferred_element_type=F32)

        acv = ac_ref[...].astype(F32)
        mu = jnp.mean(acv, axis=-1, keepdims=True)
        xc = acv - mu
        rstd = lax.rsqrt(jnp.mean(xc * xc, axis=-1, keepdims=True) + EPS)
        an = xc * rstd
        lg = vc_ref[1:2, :]
        al = an * lg + vc_ref[2:3, :]
        sg = _sigmoid(al)
        dal = dz[:, :dc] * (sg * (1.0 + al * (1.0 - sg)))
        d_lg = _colsum(dal * an)
        d_lb = _colsum(dal)
        dan = dal * lg
        dac = rstd * (dan - jnp.mean(dan, axis=-1, keepdims=True) - an * jnp.mean(dan * an, axis=-1, keepdims=True))
        d_cb = _colsum(dac)
        ext_a[0:tm, :] = dac
        uv = u_ref[:, 0:dc].astype(F32)
        sgu = _sigmoid(u_ref[:, dc:2 * dc].astype(F32))
        ag = uv * sgu
        dag = jnp.zeros((tm, dc), F32)
        _fill_phases(ext_a, phases)
        for k in range(taps):
            sl = _shifted_rows(ext_a, phases, taps - 1 - k, tm)
            dag = dag + cw_ref[k:k + 1, :] * sl
            dcw_ref[k:k + 1, :] += _colsum(ag * sl)
        du_ref[:, 0:dc] = (dag * sgu).astype(BF16)
        du_ref[:, dc:2 * dc] = (dag * uv * (sgu * (1.0 - sgu))).astype(BF16)

        pos = _seq_positions(n - 1 - i, tm, LANES)
        d_ps = []
        for g in range(ngrp):
            w = POOL_WINDOWS[g]
            cols = slice(g * LANES, (g + 1) * LANES)
            gcols = slice(dc + g * LANES, dc + (g + 1) * LANES)
            dgb = dp_ref[:, cols]
            q = jnp.dot(dgb, pw_ref[g], preferred_element_type=F32)
            dpg = dz[:, gcols]
            d_ps.append(_colsum(dpg * q))
            dq = (dpg * vc_ref[3:4, cols]).astype(BF16)
            dpw_ref[g] += lax.dot_general(dgb, dq, (((0,), (0,)), ((), ())), preferred_element_type=F32)
            dd = lax.dot_general(dq, pw_ref[g], (((1,), (1,)), ((), ())), preferred_element_type=F32)
            cnt = jnp.minimum(pos + 1, w).astype(F32)
            ext_p[0:tm, cols] = dd / cnt
            sw = ext_p[:, cols]
            step = 1
            while step < w:
                sw = sw + pltpu.roll(sw, rext - step, axis=0)
                step *= 2
            du_ref[:, 2 * dc + g * LANES:2 * dc + (g + 1) * LANES] = (sw[0:tm, :] - dd).astype(BF16)
        rowc_ref[0:1, :] += d_cb
        rowc_ref[1:2, :] += d_lg
        rowc_ref[2:3, :] += d_lb
        rowc_ref[3:4, :] += jnp.concatenate(d_ps, axis=-1)

        dh = lax.dot_general(du_ref[...], win_v[...], (((1,), (1,)), ((), ())), preferred_element_type=F32)
        _, xn, r1 = _ada_norm(x_ref[...], g_pre, sc, sh)
        dxb, d_sh, d_sc, d_g = _ada_norm_bwd(dh, xn, r1, g_pre, sc)
        dx_ref[...] = dxo_ref[...] + dxb
        rowd_ref[0:1, :] += d_g
        rowd_ref[1:2, :] += d_gp
        rowb_ref[0:1, :] += d_sh
        rowb_ref[1:2, :] += d_sc
        rowb_ref[2:3, :] += d_gt

    def tile(width):
        return pl.BlockSpec((None, tm, width), lambda b, i: (b, n - 1 - i, 0))

    return _grid_call(
        body, name, (nb, n),
        in_specs=[tile(d), tile(d), tile(d), tile(din), tile(dc), tile(dpool),
                  pl.BlockSpec((None, 8, d), lambda b, i: (b, 0, 0)), _full(vec_d.shape), _full(vec_c.shape),
                  _full(cw.shape), _full(pw.shape), _ANY, _ANY],
        out_specs=[tile(d), tile(din), tile(d), _full((8, d)), pl.BlockSpec((None, 8, d), lambda b, i: (b, 0, 0)),
                   _full((8, dc)), _full((HALO, dc)), _full(pw.shape)],
        out_shape=[jax.ShapeDtypeStruct((nb, s, d), F32), jax.ShapeDtypeStruct((nb, s, din), BF16),
                   jax.ShapeDtypeStruct((nb, s, d), BF16), jax.ShapeDtypeStruct((8, d), F32),
                   jax.ShapeDtypeStruct((nb, 8, d), F32), jax.ShapeDtypeStruct((8, dc), F32),
                   jax.ShapeDtypeStruct((HALO, dc), F32), jax.ShapeDtypeStruct(pw.shape, F32)],
        scratch_shapes=[pltpu.VMEM((d, din), BF16), pltpu.VMEM((dmix, d), BF16),
                        pltpu.VMEM((rext, dc), F32), pltpu.VMEM((rext, dpool), F32),
                        pltpu.VMEM((SUBLANES - 1, rext, dc), F32), pltpu.SemaphoreType.DMA((2 * nj,))],
        args=(dxo, x, o, u, ac, dpl, mod, vec_d, vec_c, cw, pw, win_g, wout_g), exchange=exchange)


def _ffn_forward(x, mod, vec_d, fw, wup_g, wdn_g, tm, name, exchange=None, target=None):
    nb, s, d = x.shape
    n = s // tm
    nj, _, ucol = wup_g.shape
    f2 = nj * ucol
    dff = f2 // 2
    rd = wdn_g.shape[1]
    nq = nj // 2
    cs = dff // nq

    with_loss = target is not None

    def body(*refs):
        refs = list(refs)
        x_ref, mod_ref, vd_ref, fw_ref = refs[:4]
        t_ref = refs.pop(4) if with_loss else None
        wup_hbm, wdn_hbm, xo_ref, h_ref, u_ref, uc_ref, hid_ref, o_ref = refs[4:12]
        sq_ref = refs.pop(12) if with_loss else None
        wup_v, wdn_v, prev_u, sems = refs[12:]
        b, i = pl.program_id(0), pl.program_id(1)
        pairs = [(wup_hbm.at[j], wup_v.at[:, pl.ds(j * ucol, ucol)]) for j in range(nj)]
        pairs += [(wdn_hbm.at[j], wdn_v.at[pl.ds(j * rd, rd), :]) for j in range(nj)]
        _load_weights((b == 0) & (i == 0), pairs, sems)

        if with_loss:
            @pl.when((b == 0) & (i == 0))
            def _():
                sq_ref[...] = jnp.zeros_like(sq_ref)

        @pl.when(i == 0)
        def _():
            prev_u[...] = jnp.zeros_like(prev_u)

        xv = x_ref[...]
        h, _, _ = _ada_norm(xv, vd_ref[2:3, :], mod_ref[4:5, :], mod_ref[3:4, :])
        hb = h.astype(BF16)
        h_ref[...] = hb

        def conv(cols):
            uc = jnp.dot(hb, wup_v[:, cols], preferred_element_type=F32)
            u_ref[:, cols] = uc.astype(BF16)
            before = prev_u[:, cols]
            prev_u[:, cols] = uc[tm - FHALO:, :]
            out = (fw_ref[3:4, cols] + fw_ref[2:3, cols] * uc + fw_ref[1:2, cols] * _rows_before(before, uc, 1)
                   + fw_ref[0:1, cols] * _rows_before(before, uc, 2))
            uc_ref[:, cols] = out.astype(BF16)
            return out

        o = jnp.zeros((tm, d), F32)
        for q in range(nq):
            val = conv(pl.ds(q * cs, cs))
            gate = conv(pl.ds(dff + q * cs, cs))
            hid = ((gate * _sigmoid(gate)) * val).astype(BF16)
            hid_ref[:, pl.ds(q * cs, cs)] = hid
            o = o + jnp.dot(hid, wdn_v[pl.ds(q * cs, cs), :], preferred_element_type=F32)
        o_ref[...] = o
        r2 = lax.rsqrt(jnp.mean(o * o, axis=-1, keepdims=True) + EPS)
        y = xv + (1.0 + mod_ref[5:6, :]) * ((o * r2) * vd_ref[3:4, :])
        if with_loss:
            e = y - t_ref[...]
            xo_ref[...] = e * (1.0 / d)
            sq_ref[0:1, :] += _colsum(e * e)
        else:
            xo_ref[...] = y

    def tile(width):
        return pl.BlockSpec((None, tm, width), lambda b, i: (b, i, 0))

    loss_in = [tile(d)] if with_loss else []
    return _grid_call(
        body, name, (nb, n),
        in_specs=[tile(d), pl.BlockSpec((None, 8, d), lambda b, i: (b, 0, 0)), _full(vec_d.shape), _full(fw.shape)]
        + loss_in + [_ANY, _ANY],
        out_specs=[tile(d), tile(d), tile(f2), tile(f2), tile(dff), tile(d)] + ([_full((8, d))] if with_loss else []),
        out_shape=[jax.ShapeDtypeStruct((nb, s, d), F32), jax.ShapeDtypeStruct((nb, s, d), BF16),
                   jax.ShapeDtypeStruct((nb, s, f2), BF16), jax.ShapeDtypeStruct((nb, s, f2), BF16),
                   jax.ShapeDtypeStruct((nb, s, dff), BF16), jax.ShapeDtypeStruct((nb, s, d), F32)]
        + ([jax.ShapeDtypeStruct((8, d), F32)] if with_loss else []),
        scratch_shapes=[pltpu.VMEM((d, f2), BF16), pltpu.VMEM((dff, d), BF16),
                        pltpu.VMEM((FHALO, f2), F32), pltpu.SemaphoreType.DMA((2 * nj,))],
        args=(x, mod, vec_d, fw) + ((target,) if with_loss else ()) + (wup_g, wdn_g), exchange=exchange)


def _ffn_backward(dxo, x, o, u, uc, mod, vec_d, fw, wup_g, wdn_g, tm, name, exchange=None):
    nb, s, d = x.shape
    n = s // tm
    nj, _, ucol = wup_g.shape
    f2 = nj * ucol
    dff = f2 // 2
    rd = wdn_g.shape[1]
    nq = nj // 2
    cs = dff // nq

    def body(dxo_ref, x_ref, o_ref, u_ref, uc_ref, mod_ref, vd_ref, fw_ref, wup_hbm, wdn_hbm,
             dx_ref, du_ref, dob_ref, rowd_ref, rowb_ref, dfw_ref,
             wup_v, wdn_v, next_d, sems):
        b, i = pl.program_id(0), pl.program_id(1)
        first = (b == 0) & (i == 0)
        pairs = [(wup_hbm.at[j], wup_v.at[:, pl.ds(j * ucol, ucol)]) for j in range(nj)]
        pairs += [(wdn_hbm.at[j], wdn_v.at[pl.ds(j * rd, rd), :]) for j in range(nj)]
        _load_weights(first, pairs, sems)

        @pl.when(first)
        def _():
            rowd_ref[...] = jnp.zeros_like(rowd_ref)
            dfw_ref[...] = jnp.zeros_like(dfw_ref)

        @pl.when(i == 0)
        def _():
            rowb_ref[...] = jnp.zeros_like(rowb_ref)
            next_d[...] = jnp.zeros_like(next_d)

        g_pre, g_post = vd_ref[2:3, :], vd_ref[3:4, :]
        sh, sc, gt = mod_ref[3:4, :], mod_ref[4:5, :], mod_ref[5:6, :]
        do, d_gt, d_gp = _gated_residual_bwd(dxo_ref[...], o_ref[...], g_post, gt)
        dob = do.astype(BF16)
        dob_ref[...] = dob

        def conv_bwd(cols, duc):
            uc = u_ref[:, cols].astype(F32)
            after = next_d[:, cols]
            next_d[:, cols] = duc[0:FHALO, :]
            d1 = _rows_after(duc, after, 1)
            d2 = _rows_after(duc, after, 2)
            dfw_ref[3:4, cols] += _colsum(duc)
            dfw_ref[2:3, cols] += _colsum(uc * duc)
            dfw_ref[1:2, cols] += _colsum(uc * d1)
            dfw_ref[0:1, cols] += _colsum(uc * d2)
            ob = (fw_ref[2:3, cols] * duc + fw_ref[1:2, cols] * d1 + fw_ref[0:1, cols] * d2).astype(BF16)
            du_ref[:, cols] = ob
            return lax.dot_general(ob, wup_v[:, cols], (((1,), (1,)), ((), ())), preferred_element_type=F32)

        dh = jnp.zeros((tm, d), F32)
        for q in range(nq):
            vcols = pl.ds(q * cs, cs)
            gcols = pl.ds(dff + q * cs, cs)
            dhid = lax.dot_general(dob, wdn_v[vcols, :], (((1,), (1,)), ((), ())), preferred_element_type=F32)
            val = uc_ref[:, vcols].astype(F32)
            gate = uc_ref[:, gcols].astype(F32)
            sg = _sigmoid(gate)
            act = gate * sg
            dval = dhid * act
            dgate = (dhid * val) * (sg + act * (1.0 - sg))
            dh = dh + conv_bwd(vcols, dval)
            dh = dh + conv_bwd(gcols, dgate)

        _, xn, r1 = _ada_norm(x_ref[...], g_pre, sc, sh)
        dxb, d_sh, d_sc, d_g = _ada_norm_bwd(dh, xn, r1, g_pre, sc)
        dx_ref[...] = dxo_ref[...] + dxb
        rowd_ref[2:3, :] += d_g
        rowd_ref[3:4, :] += d_gp
        rowb_ref[3:4, :] += d_sh
        rowb_ref[4:5, :] += d_sc
        rowb_ref[5:6, :] += d_gt

    def tile(width):
        return pl.BlockSpec((None, tm, width), lambda b, i: (b, n - 1 - i, 0))

    return _grid_call(
        body, name, (nb, n),
        in_specs=[tile(d), tile(d), tile(d), tile(f2), tile(f2), pl.BlockSpec((None, 8, d), lambda b, i: (b, 0, 0)),
                  _full(vec_d.shape), _full(fw.shape), _ANY, _ANY],
        out_specs=[tile(d), tile(f2), tile(d), _full((8, d)), pl.BlockSpec((None, 8, d), lambda b, i: (b, 0, 0)),
                   _full(fw.shape)],
        out_shape=[jax.ShapeDtypeStruct((nb, s, d), F32), jax.ShapeDtypeStruct((nb, s, f2), BF16),
                   jax.ShapeDtypeStruct((nb, s, d), BF16), jax.ShapeDtypeStruct((8, d), F32),
                   jax.ShapeDtypeStruct((nb, 8, d), F32), jax.ShapeDtypeStruct(fw.shape, F32)],
        scratch_shapes=[pltpu.VMEM((d, f2), BF16), pltpu.VMEM((dff, d), BF16),
                        pltpu.VMEM((FHALO, f2), F32), pltpu.SemaphoreType.DMA((2 * nj,))],
        args=(dxo, x, o, u, uc, mod, vec_d, fw, wup_g, wdn_g), exchange=exchange)


def _weight_grad(a, b, nblk, split, tt, name, exchange=None):
    t, ka = a.shape
    nb_ = b.shape[1]
    nk = t // tt
    if split == "cols":
        wa, wb, grid = ka, nb_ // nblk, (1, nk)
        a_spec = pl.BlockSpec((tt, ka), lambda j, k: (k, 0))
        b_spec = pl.BlockSpec((tt, nb_), lambda j, k: (k, 0))
        o_spec = pl.BlockSpec((nblk, wa, wb), lambda j, k: (0, 0, 0))
        acc_shape = (ka, nb_)
    elif split == "b":
        wa, wb, grid = ka, nb_ // nblk, (nblk, nk)
        a_spec = pl.BlockSpec((tt, wa), lambda j, k: (k, 0))
        b_spec = pl.BlockSpec((tt, wb), lambda j, k: (k, j))
        o_spec = pl.BlockSpec((None, wa, wb), lambda j, k: (j, 0, 0))
        acc_shape = (wa, wb)
    else:
        wa, wb, grid = ka // nblk, nb_, (nblk, nk)
        a_spec = pl.BlockSpec((tt, wa), lambda j, k: (k, j))
        b_spec = pl.BlockSpec((tt, wb), lambda j, k: (k, 0))
        o_spec = pl.BlockSpec((None, wa, wb), lambda j, k: (j, 0, 0))
        acc_shape = (wa, wb)

    def body(a_ref, b_ref, o_ref, acc):
        k = pl.program_id(1)
        prod = lax.dot_general(a_ref[...], b_ref[...], (((0,), (0,)), ((), ())), preferred_element_type=F32)

        @pl.when(k == 0)
        def _():
            acc[...] = prod

        @pl.when(k > 0)
        def _():
            acc[...] += prod

        @pl.when(k == nk - 1)
        def _():
            if split == "cols":
                for j in range(nblk):
                    o_ref[j] = acc[:, j * wb:(j + 1) * wb].astype(o_ref.dtype)
            else:
                o_ref[...] = acc[...].astype(o_ref.dtype)

    outs, exo = _grid_call(body, name, grid, in_specs=[a_spec, b_spec], out_specs=[o_spec],
                           out_shape=[jax.ShapeDtypeStruct((nblk, wa, wb), BF16)],
                           scratch_shapes=[pltpu.VMEM(acc_shape, F32)], args=(a, b), exchange=exchange)
    return outs[0], exo


def _rows128(a):
    return a.reshape(-1, LANES)


class _ReduceScatter:
    def __init__(self, gs, cidx, idx, tag):
        self.gs, self.cidx, self.idx, self.tag = gs, cidx, idx, tag

    def swap(self):
        return _swap_halves(self.gs)

    def after_swap(self, r1):
        self.ps = [_pair_sum(g, r, self.cidx, name=f"rs_pair_{self.tag}_{a}") for a, (g, r) in enumerate(zip(self.gs, r1))]

    def chips(self):
        return _chip_exchange(self.ps)

    def after_chips(self, r2):
        self.fh = [_chip_sum(p, r, self.idx, name=f"rs_sum_{self.tag}_{a}") for a, (p, r) in enumerate(zip(self.ps, r2))]

    def share(self):
        return _sibling_share(self.fh)

    @staticmethod
    def result(fs):
        return [f.reshape(f.shape[0] * f.shape[1], f.shape[2]) for f in fs]


def kernel(x, c, ada_w, ada_b, pre_mix_g, post_mix_g, w_in, conv_w, conv_b, conv_ln_g, conv_ln_b, pool_w, pool_scale, w_out, pre_ffn_g, post_ffn_g, ffn_up, ffn_conv_w, ffn_conv_b, ffn_down, loss_target, m_ada_w, m_ada_b, m_pre_mix_g, m_post_mix_g, m_w_in, m_conv_w, m_conv_b, m_conv_ln_g, m_conv_ln_b, m_pool_w, m_pool_scale, m_w_out, m_pre_ffn_g, m_post_ffn_g, m_ffn_up, m_ffn_conv_w, m_ffn_conv_b, m_ffn_down, v_ada_w, v_ada_b, v_pre_mix_g, v_post_mix_g, v_w_in, v_conv_w, v_conv_b, v_conv_ln_g, v_conv_ln_b, v_pool_w, v_pool_scale, v_w_out, v_pre_ffn_g, v_post_ffn_g, v_ffn_up, v_ffn_conv_w, v_ffn_conv_b, v_ffn_down):
    nb, s, d = x.shape
    nl = w_in.shape[0]
    taps = conv_w.shape[1]
    ccol = conv_w.shape[2]
    dc = conv_b.shape[1]
    fcol = ffn_conv_w.shape[2]
    f2 = ffn_conv_b.shape[1]
    nmod = ada_b.shape[1] // d
    acol = ada_w.shape[2]
    tm = min(MLP_TILE_ROWS, s)
    tm_mix = min(MIXER_TILE_ROWS, s)
    tt = min(GRAD_CHUNK_ROWS, (nb * s) // 2)

    xi, yi, ci = _pos()
    jm = 2 * xi + yi
    cidx = jnp.reshape(ci, (1,)).astype(jnp.int32)
    idx = jnp.stack([jm, ci]).astype(jnp.int32)

    win_b, wout_b, wup_b, wdn_b = (w.astype(BF16) for w in (w_in, w_out, ffn_up, ffn_down))

    def others(l):
        return [win_b[l], wout_b[l], wdn_b[l]]

    n_cw, n_fw, n_c = nl * taps * ccol, nl * 3 * fcol, nb * d
    packed = jnp.concatenate([conv_w.reshape(-1), ffn_conv_w.reshape(-1), c.reshape(-1)])
    got = _gather8(_rows128(packed), name="gather_small").reshape(N_DEV, -1)
    chips = got[0::2]
    cw_full = chips[:, :n_cw].reshape(N_CHIPS, nl, taps, ccol).transpose(1, 2, 0, 3).reshape(nl, taps, dc)
    fw_full = chips[:, n_cw:n_cw + n_fw].reshape(N_CHIPS, nl, 3, fcol).transpose(1, 2, 0, 3).reshape(nl, 3, f2)
    c_all = got[:, n_cw + n_fw:].reshape(N_DEV * nb, d)

    ada_b_cols = lax.dynamic_slice_in_dim(ada_b, jm * acol, acol, axis=1).reshape(nl, 1, acol)
    mod_cols, first_weights = _ada_forward(c_all, ada_w, ada_b_cols, name="ada_forward",
                                           exchange=_gather(others(0), mid_at=0.9))
    by_owner = mod_cols.reshape(nl, N_DEV, nb, acol).transpose(1, 0, 2, 3).reshape(N_DEV, -1, LANES)
    mod_own = _rows_to_owners(by_owner, name="mod_to_owners").reshape(N_CHIPS, nl, nb, acol)
    mod_own = mod_own.transpose(1, 2, 0, 3).reshape(nl, nb, nmod, d)
    mod_own = jnp.pad(mod_own, ((0, 0), (0, 0), (0, 8 - nmod), (0, 0)))

    vec_d = jnp.stack([pre_mix_g, post_mix_g, pre_ffn_g, post_ffn_g], axis=1)
    vec_c = jnp.stack([conv_b, conv_ln_g, conv_ln_b, pool_scale], axis=1)
    cw_pad = jnp.pad(cw_full, ((0, 0), (0, HALO - taps), (0, 0)))
    fw_rows = jnp.concatenate([fw_full, ffn_conv_b[:, None, :], jnp.zeros((nl, 4, f2), F32)], axis=1)
    pw_b = pool_w.astype(BF16)

    win_g, wout_g, wdn_g = _whole(first_weights)
    saved = []
    xs = x
    for l in range(nl):
        (x1, h1, u1, ac1, dp1, z1, o1), got = _mixer_forward(
            xs, mod_own[l], vec_d[l], vec_c[l], cw_pad[l], pw_b[l], win_g, wout_g, taps, tm_mix, name=f"mixer_fwd_{l}",
            exchange=_gather([wup_b[l]], mid_at=0.9))
        wup_g, = _whole(got)
        last = l + 1 == nl
        (x2, h2, u2, uc2, hid2, o2, *sq), nxt = _ffn_forward(
            x1, mod_own[l], vec_d[l], fw_rows[l], wup_g, wdn_g, tm, name=f"ffn_fwd_{l}",
            exchange=None if last else _gather(others(l + 1), mid_at=0.6), target=loss_target if last else None)
        saved.append((xs, h1, u1, ac1, dp1, z1, o1, x1, h2, u2, uc2, hid2, o2, win_g, wout_g, wup_g, wdn_g))
        if not last:
            win_g, wout_g, wdn_g = _whole(nxt)
        xs = x2

    dx = xs
    loss = lax.psum(0.5 * jnp.sum(sq[0]) / d, ("x", "y", "c"))

    flat = lambda a: a.reshape(nb * s, a.shape[-1])
    small = [None] * nl
    big_mlp, big_mix = [None] * nl, [None] * nl
    mlp = mix = None
    for l in reversed(range(nl)):
        x0, h1, u1, ac1, dp1, z1, o1, x1, h2, u2, uc2, hid2, o2, win_g, wout_g, wup_g, wdn_g = saved[l]
        (dx, du2, do2, rowd2, rowb2, dfw), got = _ffn_backward(
            dx, x1, o2, u2, uc2, mod_own[l], vec_d[l], fw_rows[l], wup_g, wdn_g, tm, name=f"ffn_bwd_{l}",
            exchange=_combine([mlp.chips(), mix.swap()]) if mlp else None)
        if mlp:
            mlp.after_chips(got[:2])
            mix.after_swap(got[2:])
        g_up, got = _weight_grad(flat(h2), flat(du2), N_CHIPS, "b", tt, name=f"grad_ffn_up_{l}",
                                 exchange=_combine([mlp.share(), mix.chips()]) if mlp else None)
        if mlp:
            big_mlp[l + 1] = mlp.result(got[:2])
            mix.after_chips(got[2:])
        g_dn, _ = _weight_grad(flat(hid2), flat(do2), 2, "a", tt, name=f"grad_ffn_down_{l}")
        mlp_above, mlp = mlp, _ReduceScatter([g_up, g_dn.reshape(N_CHIPS, -1, d)], cidx, idx, f"mlp_{l}")
        if l == 0:
            mlp.after_swap(_run_exchange(mlp.swap(), name="rs_swap_mlp_0"))
        first = mlp.swap() if l > 0 else mlp.chips()
        (dx, du1, do1, rowd1, rowb1, rowc, dcw, dpw), got = _mixer_backward(
            dx, x0, o1, u1, ac1, dp1, mod_own[l], vec_d[l], vec_c[l], cw_pad[l], pw_b[l], win_g, wout_g, taps, tm_mix,
            name=f"mixer_bwd_{l}", exchange=_combine([first, mix.share()]) if mlp_above else first)
        if l > 0:
            mlp.after_swap(got[:2])
        else:
            mlp.after_chips(got[:2])
        if mlp_above:
            big_mix[l + 1] = mix.result(got[2:])
        g_in, got = _weight_grad(flat(h1), flat(du1), N_CHIPS, "cols", tt, name=f"grad_w_in_{l}",
                                 exchange=mlp.share() if l == 0 else None)
        if l == 0:
            big_mlp[0] = mlp.result(got)
        g_out, _ = _weight_grad(flat(z1), flat(do1), 1, "cols", tt, name=f"grad_w_out_{l}")
        mix = _ReduceScatter([g_in, g_out.reshape(N_CHIPS, -1, d)], cidx, idx, f"mix_{l}")
        small[l] = dict(rowd=rowd1 + rowd2, rowb=rowb1 + rowb2, rowc=rowc, dcw=dcw[:taps], dpw=dpw, dfw=dfw)
    mix.after_swap(_run_exchange(mix.swap(), name="rs_swap_mix_0"))

    dmod_own = jnp.stack([small[l]["rowb"][:, :nmod, :] for l in range(nl)])
    dmod_all, got = _gather8(_rows128(dmod_own), name="gather_dmod", exchange=mix.chips())
    mix.after_chips(got)
    big_mix[0] = mix.result(_run_exchange(mix.share(), name="rs_share_mix_0"))
    dmod_all = dmod_all.reshape(N_DEV, nl, nb, nmod * d)
    dmod_all = dmod_all.transpose(1, 0, 2, 3).reshape(nl, N_DEV * nb, nmod * d)
    dmod_cols = lax.dynamic_slice_in_dim(dmod_all, jm * acol, acol, axis=2)
    (g_ada_w, d_ada_w, nm_ada_w, nv_ada_w), _ = _ada_update(c_all, dmod_cols, ada_w, m_ada_w, v_ada_w, name="ada_update")

    def st(key, row=None):
        return jnp.stack([small[l][key] if row is None else small[l][key][row] for l in range(nl)])

    local = {
        "ada_b": dmod_own.sum(axis=1).reshape(nl, nmod * d),
        "pre_mix_g": st("rowd", 0), "post_mix_g": st("rowd", 1),
        "conv_b": st("rowc", 0), "conv_ln_g": st("rowc", 1), "conv_ln_b": st("rowc", 2),
        "pool_w": st("dpw"), "pool_scale": st("rowc", 3),
        "pre_ffn_g": st("rowd", 2), "post_ffn_g": st("rowd", 3),
        "ffn_conv_b": st("dfw", 3), "conv_w": st("dcw"), "ffn_conv_w": jnp.stack([small[l]["dfw"][:3] for l in range(nl)]),
    }
    names = list(local)
    sizes = [local[k].size for k in names]
    pad = -sum(sizes) % (4 * SUBLANES * LANES)
    packed = jnp.concatenate([local[k].reshape(-1) for k in names] + [jnp.zeros((pad,), F32)])
    summed = _allreduce8(_rows128(packed), name="allreduce_small").reshape(-1)
    grads, off = {}, 0
    for k, sz in zip(names, sizes):
        grads[k] = summed[off:off + sz].reshape(local[k].shape)
        off += sz
    grads["conv_w"] = lax.dynamic_slice_in_dim(grads["conv_w"], jm * ccol, ccol, axis=2)
    grads["ffn_conv_w"] = lax.dynamic_slice_in_dim(grads["ffn_conv_w"], jm * fcol, fcol, axis=2)

    params = dict(ada_b=(ada_b, m_ada_b, v_ada_b), pre_mix_g=(pre_mix_g, m_pre_mix_g, v_pre_mix_g),
                  post_mix_g=(post_mix_g, m_post_mix_g, v_post_mix_g), conv_b=(conv_b, m_conv_b, v_conv_b),
                  conv_ln_g=(conv_ln_g, m_conv_ln_g, v_conv_ln_g), conv_ln_b=(conv_ln_b, m_conv_ln_b, v_conv_ln_b),
                  pool_w=(pool_w, m_pool_w, v_pool_w), pool_scale=(pool_scale, m_pool_scale, v_pool_scale),
                  pre_ffn_g=(pre_ffn_g, m_pre_ffn_g, v_pre_ffn_g), post_ffn_g=(post_ffn_g, m_post_ffn_g, v_post_ffn_g),
                  ffn_conv_b=(ffn_conv_b, m_ffn_conv_b, v_ffn_conv_b), conv_w=(conv_w, m_conv_w, v_conv_w),
                  ffn_conv_w=(ffn_conv_w, m_ffn_conv_w, v_ffn_conv_w))
    pack = lambda i, g=None: _rows128(jnp.concatenate([(grads[k] if g else params[k][i]).reshape(-1) for k in names]))
    sd, sm, sv = _adamw_flat(pack(0), pack(0, True), pack(1), pack(2), name="adamw_small")
    outs = {}
    off = 0
    for k in names:
        shape, sz = params[k][0].shape, params[k][0].size
        outs[k] = (grads[k],) + tuple(a.reshape(-1)[off:off + sz].reshape(shape) for a in (sd, sm, sv))
        off += sz

    outs["ada_w"] = (g_ada_w, d_ada_w, nm_ada_w, nv_ada_w)
    for k, w, m, v, gs in [("w_in", w_in, m_w_in, v_w_in, [big_mix[l][0] for l in range(nl)]),
                           ("w_out", w_out, m_w_out, v_w_out, [big_mix[l][1] for l in range(nl)]),
                           ("ffn_up", ffn_up, m_ffn_up, v_ffn_up, [big_mlp[l][0] for l in range(nl)]),
                           ("ffn_down", ffn_down, m_ffn_down, v_ffn_down, [big_mlp[l][1] for l in range(nl)])]:
        outs[k] = tuple(_adamw_layers(w, m, v, gs, name=f"adamw_{k}"))

    order = ["ada_w", "ada_b", "pre_mix_g", "post_mix_g", "w_in", "conv_w", "conv_b", "conv_ln_g", "conv_ln_b", "pool_w",
             "pool_scale", "w_out", "pre_ffn_g", "post_ffn_g", "ffn_up", "ffn_conv_w", "ffn_conv_b", "ffn_down"]
    return (loss, dx) + tuple(outs[k][i] for i in range(4) for k in order)
```

```python
import functools

import jax
import jax.numpy as jnp
from jax import lax
from jax.experimental import pallas as pl
from jax.experimental.pallas import tpu as pltpu

F32 = jnp.float32
BF16 = jnp.bfloat16
MESH = pl.DeviceIdType.MESH

EPS = 1e-6
POOL_WINDOWS = (2, 4, 8, 16)
ADAM_LR = 0.001
ADAM_B1 = 0.9
ADAM_B2 = 0.999
ADAM_EPS = 1e-08
ADAM_WD = 0.01
ADAM_STEP = 10

N_CHIPS = 4
N_DEV = 8
LANES = 128
SUBLANES = 8
HALO = 32
FHALO = 8
VMEM_LIMIT = 60 * 1024 * 1024
MLP_TILE_ROWS = 256
MIXER_TILE_ROWS = 512
GRAD_CHUNK_ROWS = 2048


def _pos():
    return lax.axis_index("x"), lax.axis_index("y"), lax.axis_index("c")


def _flip(v, f):
    return 1 - v if f else v


def _full(shape):
    nd = len(shape)
    return pl.BlockSpec(shape, lambda *_: (0,) * nd)


_ANY = pl.BlockSpec(memory_space=pl.ANY)
_VMEM = pl.BlockSpec(memory_space=pltpu.VMEM)


def _sigmoid(v):
    return 1.0 / (1.0 + jnp.exp(-v))


def _colsum(v):
    return jnp.sum(v, axis=0, keepdims=True)


def _gather8(v, name, exchange=None):
    r, ccols = v.shape
    ex = exchange
    nci, nco = (len(ex.ins), len(ex.outs)) if ex else (0, 0)

    def body(*refs):
        v_ref, cin, out_ref, cout = refs[0], refs[1:1 + nci], refs[1 + nci], refs[2 + nci:2 + nci + nco]
        send_sems, recv_sems, local_sem = refs[2 + nci + nco:5 + nci + nco]
        if ex:
            sems = _Sems(*refs[5 + nci + nco:])
            ex.start(cin, cout, sems)
        x, y, c = _pos()
        me = 4 * x + 2 * y + c
        mine = pltpu.make_async_copy(v_ref, out_ref.at[me], local_sem)
        mine.start()
        peers = [(_flip(x, (k >> 2) & 1), _flip(y, (k >> 1) & 1), _flip(c, k & 1)) for k in range(1, N_DEV)]
        sends = []
        for k, peer in enumerate(peers):
            cp = pltpu.make_async_remote_copy(src_ref=v_ref, dst_ref=out_ref.at[me], send_sem=send_sems.at[k],
                                              recv_sem=recv_sems.at[k], device_id=peer, device_id_type=MESH)
            cp.start()
            sends.append(cp)
        for k, peer in enumerate(peers):
            pidx = 4 * peer[0] + 2 * peer[1] + peer[2]
            pltpu.make_async_remote_copy(src_ref=v_ref, dst_ref=out_ref.at[pidx], send_sem=send_sems.at[k],
                                         recv_sem=recv_sems.at[k], device_id=peer, device_id_type=MESH).wait_recv()
        for cp in sends:
            cp.wait_send()
        mine.wait()
        if ex:
            if ex.mid is not None:
                ex.mid(cin, cout, sems)
            ex.finish(cin, cout, sems)

    outs = pl.pallas_call(
        body, name=name, out_shape=[jax.ShapeDtypeStruct((N_DEV, r, ccols), v.dtype)] + (ex.outs if ex else []),
        in_specs=[_VMEM] + [_ANY] * nci, out_specs=[_VMEM] + [_ANY] * nco,
        scratch_shapes=[pltpu.SemaphoreType.DMA((N_DEV - 1,)), pltpu.SemaphoreType.DMA((N_DEV - 1,)),
                        pltpu.SemaphoreType.DMA(())] + (ex.scratch() if ex else []),
        input_output_aliases={1 + a: 1 + b for a, b in ex.aliases.items()} if ex else {},
        compiler_params=pltpu.CompilerParams(vmem_limit_bytes=VMEM_LIMIT),
    )(v, *(ex.ins if ex else []))
    return (outs[0], list(outs[1:])) if ex else outs[0]


def _rows_to_owners(v, name):
    _, r, ccols = v.shape

    def body(v_ref, out_ref, send_sems, recv_sems, local_sem):
        x, y, c = _pos()
        jm = 2 * x + y
        mine = pltpu.make_async_copy(v_ref.at[2 * jm + c], out_ref.at[jm], local_sem)
        mine.start()
        peers, pjs = _chip_peers(x, y, c)
        sends = []
        for k, peer in enumerate(peers):
            cp = pltpu.make_async_remote_copy(src_ref=v_ref.at[2 * pjs[k] + c], dst_ref=out_ref.at[jm],
                                              send_sem=send_sems.at[k], recv_sem=recv_sems.at[k],
                                              device_id=peer, device_id_type=MESH)
            cp.start()
            sends.append(cp)
        for k, peer in enumerate(peers):
            pltpu.make_async_remote_copy(src_ref=v_ref.at[0], dst_ref=out_ref.at[pjs[k]], send_sem=send_sems.at[k],
                                         recv_sem=recv_sems.at[k], device_id=peer, device_id_type=MESH).wait_recv()
        for cp in sends:
            cp.wait_send()
        mine.wait()

    return pl.pallas_call(
        body, name=name, out_shape=jax.ShapeDtypeStruct((N_CHIPS, r, ccols), v.dtype),
        in_specs=[_VMEM], out_specs=_VMEM,
        scratch_shapes=[pltpu.SemaphoreType.DMA((N_CHIPS - 1,)), pltpu.SemaphoreType.DMA((N_CHIPS - 1,)),
                        pltpu.SemaphoreType.DMA(())],
        compiler_params=pltpu.CompilerParams(vmem_limit_bytes=VMEM_LIMIT),
    )(v)


def _allreduce8(v, name):
    r, ccols = v.shape
    h = r // 2
    q = h // 2

    def body(v_ref, out_ref, whole, part, done, send_sems, recv_sems):
        x, y, c = _pos()
        sib = (x, y, 1 - c)
        mine = pl.ds(pl.multiple_of(c * h, SUBLANES), h)
        theirs = pl.ds(pl.multiple_of((1 - c) * h, SUBLANES), h)
        quarters = [pl.ds(pl.multiple_of(c * h + k * q, SUBLANES), q) for k in range(2)]
        along_x, along_y = (1 - x, y, c), (x, 1 - y, c)

        def exchange(pairs):
            cps = [pltpu.make_async_remote_copy(src_ref=src, dst_ref=dst, send_sem=send_sems.at[k], recv_sem=recv_sems.at[k],
                                                device_id=peer, device_id_type=MESH) for src, dst, k, peer in pairs]
            for cp in cps:
                cp.start()
            for cp in cps:
                cp.wait()

        exchange([(v_ref, whole, 0, sib)])
        out_ref[...] = v_ref[...] + whole[...]
        for stage, peers in enumerate(((along_x, along_y), (along_y, along_x))):
            exchange([(out_ref.at[quarters[k]], part.at[2 * stage + k], 1 + 2 * stage + k, peers[k]) for k in range(2)])
            for k in range(2):
                out_ref[quarters[k], :] = out_ref[quarters[k], :] + part[2 * stage + k]
        exchange([(out_ref.at[mine], done, 5, sib)])
        out_ref[theirs, :] = done[...]

    return pl.pallas_call(
        body, name=name, out_shape=jax.ShapeDtypeStruct((r, ccols), v.dtype),
        in_specs=[_VMEM], out_specs=_VMEM,
        scratch_shapes=[pltpu.VMEM((r, ccols), v.dtype), pltpu.VMEM((4, q, ccols), v.dtype), pltpu.VMEM((h, ccols), v.dtype),
                        pltpu.SemaphoreType.DMA((6,)), pltpu.SemaphoreType.DMA((6,))],
        compiler_params=pltpu.CompilerParams(vmem_limit_bytes=VMEM_LIMIT),
    )(v)


def _chip_peers(x, y, c):
    peers = [(_flip(x, (k >> 1) & 1), _flip(y, k & 1), c) for k in range(1, N_CHIPS)]
    return peers, [2 * p[0] + p[1] for p in peers]


class _Exchange:
    def __init__(self, ins, outs, aliases, n_sems, n_local, start, finish, mid=None, mid_at=1.0, sibling=False, chips=False):
        self.ins, self.outs, self.aliases = list(ins), list(outs), dict(aliases)
        self.n_sems, self.n_local, self.start, self.finish = n_sems, n_local, start, finish
        self.mid, self.mid_at = mid, mid_at
        self.sibling, self.chips = sibling, chips

    def collective_id(self):
        return {(True, False): 1, (False, True): 2, (True, True): 3}[(self.sibling, self.chips)]

    def handshake(self):
        x, y, c = _pos()
        peers = ([(x, y, 1 - c)] if self.sibling else []) + (_chip_peers(x, y, c)[0] if self.chips else [])
        barrier = pltpu.get_barrier_semaphore()
        for peer in peers:
            pl.semaphore_signal(barrier, inc=1, device_id=peer, device_id_type=MESH)
        pl.semaphore_wait(barrier, len(peers))

    def scratch(self):
        return [pltpu.SemaphoreType.DMA((self.n_sems,)), pltpu.SemaphoreType.DMA((self.n_sems,)),
                pltpu.SemaphoreType.DMA((max(self.n_local, 1),))]


class _Sems:
    def __init__(self, send, recv, local, base=0, lbase=0):
        self.send, self.recv, self.loc, self.base, self.lbase = send, recv, local, base, lbase

    def shifted(self, by, lby):
        return _Sems(self.send, self.recv, self.loc, self.base + by, self.lbase + lby)

    def local(self, k):
        return self.loc.at[self.lbase + k]


def _remote(src, dst, sems, k, peer):
    return pltpu.make_async_remote_copy(src_ref=src, dst_ref=dst, send_sem=sems.send.at[sems.base + k],
                                        recv_sem=sems.recv.at[sems.base + k], device_id=peer, device_id_type=MESH)


def _combine(exs):
    ins = [a for ex in exs for a in ex.ins]
    outs = [o for ex in exs for o in ex.outs]
    aliases, spans, ni, no, ns, nloc = {}, [], 0, 0, 0, 0
    for ex in exs:
        aliases.update({ni + a: no + b for a, b in ex.aliases.items()})
        spans.append((ni, no, ns, nloc))
        ni, no, ns, nloc = ni + len(ex.ins), no + len(ex.outs), ns + ex.n_sems, nloc + ex.n_local

    def each(which):
        def run(ins_, outs_, sems):
            for ex, (i0, o0, s0, l0) in zip(exs, spans):
                stage = getattr(ex, which)
                if stage is not None:
                    stage(ins_[i0:i0 + len(ex.ins)], outs_[o0:o0 + len(ex.outs)], sems.shifted(s0, l0))
        return run

    mids = [ex.mid_at for ex in exs if ex.mid is not None]
    return _Exchange(ins, outs, aliases, ns, nloc, each("start"), each("finish"),
                     mid=each("mid") if mids else None, mid_at=max(mids) if mids else 1.0,
                     sibling=any(ex.sibling for ex in exs), chips=any(ex.chips for ex in exs))


def _gather(shards, mid_at=1.0):
    n = len(shards)
    per = N_CHIPS - 1
    halves = [s.reshape(2, s.shape[0] // 2, s.shape[1]) for s in shards]

    def copies(ins, outs, sems):
        x, y, c = _pos()
        jm = 2 * x + y
        sib = (x, y, 1 - c)
        peers, pjs = _chip_peers(x, y, c)
        sends, recvs, passes, passed = [], [], [], []
        for a in range(n):
            own = _remote(ins[a], outs[a].at[jm], sems, 2 * n * per + a, sib)
            sends.append(own)
            passed.append(own)
            for k, peer in enumerate(peers):
                landed, theirs = outs[a].at[pjs[k], c], outs[a].at[pjs[k], 1 - c]
                sends.append(_remote(ins[a].at[c], outs[a].at[jm, c], sems, 2 * (a * per + k), peer))
                recvs.append(_remote(landed, landed, sems, 2 * (a * per + k), peer))
                passes.append(_remote(landed, landed, sems, 2 * (a * per + k) + 1, sib))
                passed.append(_remote(theirs, theirs, sems, 2 * (a * per + k) + 1, sib))
        return sends, recvs, passes, passed

    def start(ins, outs, sems):
        for cp in copies(ins, outs, sems)[0]:
            cp.start()

    def mid(ins, outs, sems):
        _, recvs, passes, _ = copies(ins, outs, sems)
        for got, fwd in zip(recvs, passes):
            got.wait_recv()
            fwd.start()

    def finish(ins, outs, sems):
        sends, _, passes, passed = copies(ins, outs, sems)
        for cp in passed:
            cp.wait_recv()
        for cp in sends + passes:
            cp.wait_send()

    outs = [jax.ShapeDtypeStruct((N_CHIPS,) + h.shape, h.dtype) for h in halves]
    return _Exchange(halves, outs, {}, 2 * n * per + n, 0, start, finish, mid=mid, mid_at=mid_at, sibling=True, chips=True)


def _whole(gathered):
    return [g.reshape(g.shape[0], g.shape[1] * g.shape[2], g.shape[3]) for g in gathered]


def _swap_halves(gs):
    n = len(gs)
    halves = [g.reshape(g.shape[0], 2, g.shape[1] // 2, g.shape[2]) for g in gs]

    def copies(ins, outs, sems):
        x, y, c = _pos()
        sib = (x, y, 1 - c)
        return [_remote(ins[a].at[:, 1 - c], outs[a], sems, a, sib) for a in range(n)]

    def start(ins, outs, sems):
        for cp in copies(ins, outs, sems):
            cp.start()

    def finish(ins, outs, sems):
        for cp in copies(ins, outs, sems):
            cp.wait()

    outs = [jax.ShapeDtypeStruct((g.shape[0], g.shape[1] // 2, g.shape[2]), g.dtype) for g in gs]
    return _Exchange(halves, outs, {}, n, 0, start, finish, sibling=True)


def _chip_exchange(ps):
    n = len(ps)
    per = N_CHIPS - 1

    def copies(ins, outs, sems):
        x, y, c = _pos()
        peers, pjs = _chip_peers(x, y, c)
        return [_remote(ins[a].at[pjs[k]], outs[a].at[k], sems, a * per + k, peer)
                for a in range(n) for k, peer in enumerate(peers)]

    def start(ins, outs, sems):
        for cp in copies(ins, outs, sems):
            cp.start()

    def finish(ins, outs, sems):
        for cp in copies(ins, outs, sems):
            cp.wait()

    outs = [jax.ShapeDtypeStruct((per,) + p.shape[1:], p.dtype) for p in ps]
    return _Exchange(ps, outs, {}, n * per, 0, start, finish, chips=True)


def _sibling_share(fs):
    n = len(fs)

    def copies(outs, sems):
        x, y, c = _pos()
        sib = (x, y, 1 - c)
        sends = [_remote(outs[a].at[c], outs[a].at[c], sems, a, sib) for a in range(n)]
        recvs = [_remote(outs[a].at[1 - c], outs[a].at[1 - c], sems, a, sib) for a in range(n)]
        return sends, recvs

    def start(ins, outs, sems):
        for cp in copies(outs, sems)[0]:
            cp.start()

    def finish(ins, outs, sems):
        sends, recvs = copies(outs, sems)
        for cp in recvs:
            cp.wait_recv()
        for cp in sends:
            cp.wait_send()

    outs = [jax.ShapeDtypeStruct(f.shape, f.dtype) for f in fs]
    return _Exchange(fs, outs, {a: a for a in range(n)}, n, 0, start, finish, sibling=True)


def _run_exchange(ex, name):
    ni, no = len(ex.ins), len(ex.outs)

    def body(*refs):
        ins, outs, sems = refs[:ni], refs[ni:ni + no], _Sems(*refs[ni + no:])
        ex.handshake()
        ex.start(ins, outs, sems)
        if ex.mid is not None:
            ex.mid(ins, outs, sems)
        ex.finish(ins, outs, sems)

    return pl.pallas_call(
        body, name=name, out_shape=ex.outs, in_specs=[_ANY] * ni, out_specs=[_ANY] * no,
        input_output_aliases=ex.aliases, scratch_shapes=ex.scratch(),
        compiler_params=pltpu.CompilerParams(collective_id=ex.collective_id()),
    )(*ex.ins)


def _grid_call(body, name, grid, in_specs, out_specs, out_shape, scratch_shapes, args, exchange=None):
    ni, no = len(in_specs), len(out_specs)
    params = pltpu.CompilerParams(dimension_semantics=("arbitrary",) * len(grid), vmem_limit_bytes=VMEM_LIMIT)
    if exchange is None:
        outs = pl.pallas_call(body, name=name, grid=grid, in_specs=in_specs, out_specs=out_specs, out_shape=out_shape,
                              scratch_shapes=scratch_shapes, compiler_params=params)(*args)
        return list(outs), []
    ex = exchange
    nci, nco = len(ex.ins), len(ex.outs)

    def hosted(*refs):
        cin = refs[ni:ni + nci]
        cout = refs[ni + nci + no:ni + nci + no + nco]
        sems = _Sems(*refs[len(refs) - 3:])
        main = refs[:ni] + refs[ni + nci:ni + nci + no] + refs[ni + nci + no + nco:len(refs) - 3]
        ids = [pl.program_id(a) for a in range(len(grid))]
        first = functools.reduce(lambda p, q: p & q, [i == 0 for i in ids])
        last = functools.reduce(lambda p, q: p & q, [i == g - 1 for i, g in zip(ids, grid)])

        @pl.when(first)
        def _():
            ex.handshake()
            ex.start(cin, cout, sems)

        if ex.mid is not None:
            steps = functools.reduce(lambda p, q: p * q, grid)
            flat = functools.reduce(lambda p, q: p * q[1] + q[0], zip(ids[1:], grid[1:]), ids[0])

            @pl.when(flat == min(steps - 1, int(ex.mid_at * steps)))
            def _():
                ex.mid(cin, cout, sems)

        body(*main)

        @pl.when(last)
        def _():
            ex.finish(cin, cout, sems)

    outs = pl.pallas_call(
        hosted, name=name, grid=grid, in_specs=list(in_specs) + [_ANY] * nci, out_specs=list(out_specs) + [_ANY] * nco,
        out_shape=list(out_shape) + ex.outs, scratch_shapes=list(scratch_shapes) + ex.scratch(),
        input_output_aliases={ni + a: no + b for a, b in ex.aliases.items()},
        compiler_params=pltpu.CompilerParams(dimension_semantics=("arbitrary",) * len(grid), vmem_limit_bytes=VMEM_LIMIT,
                                             collective_id=ex.collective_id()),
    )(*args, *ex.ins)
    return list(outs[:no]), list(outs[no:])


SUM_BLOCK_BYTES = 4 * 1024 * 1024
ADAM_BLOCK_BYTES = 2 * 1024 * 1024


def _row_tile(rows, cols, itemsize, budget):
    best = None
    for t in range(16, rows + 1, 16):
        if rows % t == 0 and t * cols * itemsize <= budget:
            best = t
    return best if best is not None else rows


def _pair_sum(g, r1, cidx, name):
    nj, r, ccols = g.shape
    hr = r // 2
    tr = _row_tile(hr, ccols, 4, SUM_BLOCK_BYTES)
    nt = hr // tr

    def body(c_ref, g_ref, r_ref, o_ref):
        o_ref[...] = (g_ref[...].astype(F32) + r_ref[...].astype(F32)).astype(o_ref.dtype)

    return pl.pallas_call(
        body, name=name, out_shape=jax.ShapeDtypeStruct((nj, hr, ccols), g.dtype),
        grid_spec=pltpu.PrefetchScalarGridSpec(
            num_scalar_prefetch=1, grid=(nj, nt),
            in_specs=[pl.BlockSpec((None, tr, ccols), lambda j, i, c_ref: (j, c_ref[0] * nt + i, 0)),
                      pl.BlockSpec((None, tr, ccols), lambda j, i, c_ref: (j, i, 0))],
            out_specs=pl.BlockSpec((None, tr, ccols), lambda j, i, c_ref: (j, i, 0))),
        compiler_params=pltpu.CompilerParams(dimension_semantics=("arbitrary", "arbitrary"), vmem_limit_bytes=VMEM_LIMIT),
    )(cidx, g, r1)


def _chip_sum(p, r2, idx, name):
    nj, hr, ccols = p.shape
    tr = _row_tile(hr, ccols, 4, SUM_BLOCK_BYTES)
    nt = hr // tr

    def body(i_ref, p_ref, r_ref, o_ref):
        s = p_ref[...].astype(F32)
        for k in range(N_CHIPS - 1):
            s = s + r_ref[k].astype(F32)
        o_ref[...] = s

    return pl.pallas_call(
        body, name=name, out_shape=jax.ShapeDtypeStruct((2, hr, ccols), F32),
        grid_spec=pltpu.PrefetchScalarGridSpec(
            num_scalar_prefetch=1, grid=(nt,),
            in_specs=[pl.BlockSpec((None, tr, ccols), lambda i, i_ref: (i_ref[0], i, 0)),
                      pl.BlockSpec((N_CHIPS - 1, tr, ccols), lambda i, i_ref: (0, i, 0))],
            out_specs=pl.BlockSpec((None, tr, ccols), lambda i, i_ref: (i_ref[1], i, 0))),
        compiler_params=pltpu.CompilerParams(dimension_semantics=("arbitrary",), vmem_limit_bytes=VMEM_LIMIT),
    )(idx, p, r2)


def _adam_math(w, g, m, v):
    m2 = ADAM_B1 * m + (1.0 - ADAM_B1) * g
    v2 = ADAM_B2 * v + (1.0 - ADAM_B2) * (g * g)
    m_hat = m2 / (1.0 - ADAM_B1 ** ADAM_STEP)
    v_hat = v2 / (1.0 - ADAM_B2 ** ADAM_STEP)
    delta = -ADAM_LR * (m_hat / (jnp.sqrt(v_hat) + ADAM_EPS) + ADAM_WD * w)
    return delta, m2, v2


def _adamw_layers(w, m, v, gs, name):
    nl, r, ccols = w.shape
    ng = len(gs)
    tr = _row_tile(r, ccols, 4, ADAM_BLOCK_BYTES)
    nt = r // tr

    def body(w_ref, m_ref, v_ref, *rest):
        g_refs, (go_ref, d_ref, mo_ref, vo_ref) = rest[:ng], rest[ng:]
        l = pl.program_id(0)
        g = g_refs[0][...]
        for k in range(1, ng):
            g = jnp.where(l == k, g_refs[k][...], g)
        delta, m2, v2 = _adam_math(w_ref[...], g, m_ref[...], v_ref[...])
        go_ref[...] = g
        d_ref[...] = delta
        mo_ref[...] = m2
        vo_ref[...] = v2

    big = pl.BlockSpec((None, tr, ccols), lambda l, i: (l, i, 0))

    def gspec(k):
        return pl.BlockSpec((tr, ccols), lambda l, i: (jnp.where(l == k, i, jnp.where(l < k, 0, nt - 1)), 0))

    assert ng == nl
    return _grid_call(body, name, (nl, nt), in_specs=[big, big, big] + [gspec(k) for k in range(ng)],
                      out_specs=[big, big, big, big], out_shape=[jax.ShapeDtypeStruct(w.shape, F32)] * 4,
                      scratch_shapes=[], args=(w, m, v, *gs))[0]


def _adamw_flat(w, g, m, v, name):
    r, ccols = w.shape

    def body(w_ref, g_ref, m_ref, v_ref, d_ref, mo_ref, vo_ref):
        delta, m2, v2 = _adam_math(w_ref[...], g_ref[...], m_ref[...], v_ref[...])
        d_ref[...] = delta
        mo_ref[...] = m2
        vo_ref[...] = v2

    return pl.pallas_call(
        body, name=name, out_shape=[jax.ShapeDtypeStruct((r, ccols), F32)] * 3,
        in_specs=[_VMEM] * 4, out_specs=[_VMEM] * 3,
        compiler_params=pltpu.CompilerParams(vmem_limit_bytes=VMEM_LIMIT),
    )(w, g, m, v)


def _ada_forward(c_all, ada_w, ada_b_cols, name, exchange=None):
    nl, d, ncols = ada_w.shape
    bg = c_all.shape[0]
    tn = 512 if ncols % 512 == 0 else ncols

    def body(c_ref, w_ref, b_ref, o_ref):
        cv = c_ref[...]
        ca = (cv * _sigmoid(cv)).astype(BF16)
        o_ref[...] = jnp.dot(ca, w_ref[...].astype(BF16), preferred_element_type=F32) + b_ref[...]

    outs, got = _grid_call(
        body, name, (nl, ncols // tn),
        in_specs=[pl.BlockSpec((bg, d), lambda l, j: (0, 0)),
                  pl.BlockSpec((None, d, tn), lambda l, j: (l, 0, j)),
                  pl.BlockSpec((None, 1, tn), lambda l, j: (l, 0, j))],
        out_specs=[pl.BlockSpec((None, bg, tn), lambda l, j: (l, 0, j))],
        out_shape=[jax.ShapeDtypeStruct((nl, bg, ncols), F32)], scratch_shapes=[], args=(c_all, ada_w, ada_b_cols),
        exchange=exchange)
    return outs[0], got


def _ada_update(c_all, dmod_cols, w, m, v, name, exchange=None):
    nl, d, ncols = w.shape
    bg = c_all.shape[0]
    tn = 512 if ncols % 512 == 0 else ncols

    def body(c_ref, dm_ref, w_ref, m_ref, v_ref, go_ref, d_ref, mo_ref, vo_ref):
        cv = c_ref[...]
        ca = (cv * _sigmoid(cv)).astype(BF16)
        g = lax.dot_general(ca, dm_ref[...].astype(BF16), (((0,), (0,)), ((), ())), preferred_element_type=F32)
        delta, m2, v2 = _adam_math(w_ref[...], g, m_ref[...], v_ref[...])
        go_ref[...] = g
        d_ref[...] = delta
        mo_ref[...] = m2
        vo_ref[...] = v2

    big = pl.BlockSpec((None, d, tn), lambda l, j: (l, 0, j))
    return _grid_call(
        body, name, (nl, ncols // tn),
        in_specs=[pl.BlockSpec((bg, d), lambda l, j: (0, 0)),
                  pl.BlockSpec((None, bg, tn), lambda l, j: (l, 0, j)), big, big, big],
        out_specs=[big, big, big, big], out_shape=[jax.ShapeDtypeStruct(w.shape, F32)] * 4,
        scratch_shapes=[], args=(c_all, dmod_cols, w, m, v), exchange=exchange)


def _load_weights(first, pairs, sems):
    @pl.when(first)
    def _():
        cps = [pltpu.make_async_copy(src, dst, sems.at[k]) for k, (src, dst) in enumerate(pairs)]
        for cp in cps:
            cp.start()
        for cp in cps:
            cp.wait()


def _ada_norm(xv, g, sc, sh):
    r = lax.rsqrt(jnp.mean(xv * xv, axis=-1, keepdims=True) + EPS)
    xn = xv * r
    return (xn * g) * (1.0 + sc) + sh, xn, r


def _ada_norm_bwd(dh, xn, r, g, sc):
    d_sh = _colsum(dh)
    d_sc = _colsum(dh * (xn * g))
    dxg = dh * (1.0 + sc)
    d_g = _colsum(dxg * xn)
    gd = dxg * g
    dx = r * (gd - xn * jnp.mean(gd * xn, axis=-1, keepdims=True))
    return dx, d_sh, d_sc, d_g


def _gated_residual_bwd(dxo, o, g_post, gt):
    r = lax.rsqrt(jnp.mean(o * o, axis=-1, keepdims=True) + EPS)
    on = o * r
    d_gt = _colsum(dxo * (on * g_post))
    dy = dxo * (1.0 + gt)
    d_gp = _colsum(dy * on)
    gd = dy * g_post
    do = r * (gd - on * jnp.mean(gd * on, axis=-1, keepdims=True))
    return do, d_gt, d_gp


def _seq_positions(i, tm, width):
    return i * tm + lax.broadcasted_iota(jnp.int32, (tm, width), 0)


def _fill_phases(ext, phases):
    rows = ext.shape[0]
    ev = ext[...]
    for r in range(1, SUBLANES):
        phases[r - 1] = pltpu.roll(ev, rows - r, axis=0)


def _shifted_rows(ext, phases, offset, n):
    q, r = divmod(offset, SUBLANES)
    if r == 0:
        return ext[pl.ds(q * SUBLANES, n), :]
    return phases[r - 1, pl.ds(q * SUBLANES, n), :]


def _rows_before(halo, cur, shift):
    e = jnp.concatenate([halo, cur], axis=0)
    return pltpu.roll(e, shift, axis=0)[halo.shape[0]:, :]


def _rows_after(cur, halo, shift):
    e = jnp.concatenate([cur, halo], axis=0)
    return pltpu.roll(e, e.shape[0] - shift, axis=0)[:cur.shape[0], :]


def _mixer_forward(x, mod, vec_d, vec_c, cw, pw, win_g, wout_g, taps, tm, name, exchange=None):
    nb, s, d = x.shape
    n = s // tm
    nj, _, dcol = win_g.shape
    din = nj * dcol
    dc = vec_c.shape[-1]
    dpool = din - 2 * dc
    dmix = dc + dpool
    ro = wout_g.shape[1]
    ngrp = dpool // LANES

    def body(x_ref, mod_ref, vd_ref, vc_ref, cw_ref, pw_ref, win_hbm, wout_hbm,
             xo_ref, h_ref, u_ref, ac_ref, dp_ref, z_ref, o_ref,
             win_v, wout_v, ext_a, ext_p, phases, sems):
        b, i = pl.program_id(0), pl.program_id(1)
        pairs = [(win_hbm.at[j], win_v.at[:, pl.ds(j * dcol, dcol)]) for j in range(nj)]
        pairs += [(wout_hbm.at[j], wout_v.at[pl.ds(j * ro, ro), :]) for j in range(nj)]
        _load_weights((b == 0) & (i == 0), pairs, sems)

        xv = x_ref[...]
        h, _, _ = _ada_norm(xv, vd_ref[0:1, :], mod_ref[1:2, :], mod_ref[0:1, :])
        hb = h.astype(BF16)
        h_ref[...] = hb
        u = jnp.dot(hb, win_v[...], preferred_element_type=F32)
        u_ref[...] = u.astype(BF16)
        ag = u[:, :dc] * _sigmoid(u[:, dc:2 * dc])
        up = u[:, 2 * dc:]

        @pl.when(i == 0)
        def _():
            ext_a[0:HALO, :] = jnp.zeros((HALO, dc), F32)
            ext_p[0:HALO, :] = jnp.zeros((HALO, dpool), F32)

        @pl.when(i > 0)
        def _():
            ext_a[0:HALO, :] = ext_a[tm:tm + HALO, :]
            ext_p[0:HALO, :] = ext_p[tm:tm + HALO, :]

        ext_a[HALO:HALO + tm, :] = ag
        ext_p[HALO:HALO + tm, :] = up

        acc = jnp.broadcast_to(vc_ref[0:1, :], (tm, dc))
        _fill_phases(ext_a, phases)
        for k in range(taps):
            acc = acc + cw_ref[k:k + 1, :] * _shifted_rows(ext_a, phases, HALO - (taps - 1) + k, tm)
        ac_ref[...] = acc.astype(BF16)
        mu = jnp.mean(acc, axis=-1, keepdims=True)
        xc = acc - mu
        var = jnp.mean(xc * xc, axis=-1, keepdims=True)
        al = (xc * lax.rsqrt(var + EPS)) * vc_ref[1:2, :] + vc_ref[2:3, :]
        a = al * _sigmoid(al)

        pos = _seq_positions(i, tm, LANES)
        parts = [a.astype(BF16)]
        for g in range(ngrp):
            w = POOL_WINDOWS[g]
            cols = slice(g * LANES, (g + 1) * LANES)
            sw = ext_p[:, cols]
            step = 1
            while step < w:
                sw = sw + pltpu.roll(sw, step, axis=0)
                step *= 2
            cnt = jnp.minimum(pos + 1, w).astype(F32)
            dg = (sw[HALO:, :] / cnt - up[:, cols]).astype(BF16)
            dp_ref[:, cols] = dg
            q = jnp.dot(dg, pw_ref[g], preferred_element_type=F32)
            parts.append((q * vc_ref[3:4, cols]).astype(BF16))
        z = jnp.concatenate(parts, axis=-1)
        z_ref[...] = z
        o = jnp.dot(z, wout_v[...], preferred_element_type=F32)
        o_ref[...] = o
        r2 = lax.rsqrt(jnp.mean(o * o, axis=-1, keepdims=True) + EPS)
        xo_ref[...] = xv + (1.0 + mod_ref[2:3, :]) * ((o * r2) * vd_ref[1:2, :])

    def tile(width):
        return pl.BlockSpec((None, tm, width), lambda b, i: (b, i, 0))

    return _grid_call(
        body, name, (nb, n),
        in_specs=[tile(d), pl.BlockSpec((None, 8, d), lambda b, i: (b, 0, 0)), _full(vec_d.shape), _full(vec_c.shape),
                  _full(cw.shape), _full(pw.shape), _ANY, _ANY],
        out_specs=[tile(d), tile(d), tile(din), tile(dc), tile(dpool), tile(dmix), tile(d)],
        out_shape=[jax.ShapeDtypeStruct((nb, s, d), F32), jax.ShapeDtypeStruct((nb, s, d), BF16),
                   jax.ShapeDtypeStruct((nb, s, din), BF16), jax.ShapeDtypeStruct((nb, s, dc), BF16),
                   jax.ShapeDtypeStruct((nb, s, dpool), BF16), jax.ShapeDtypeStruct((nb, s, dmix), BF16),
                   jax.ShapeDtypeStruct((nb, s, d), F32)],
        scratch_shapes=[pltpu.VMEM((d, din), BF16), pltpu.VMEM((dmix, d), BF16),
                        pltpu.VMEM((HALO + tm, dc), F32), pltpu.VMEM((HALO + tm, dpool), F32),
                        pltpu.VMEM((SUBLANES - 1, HALO + tm, dc), F32), pltpu.SemaphoreType.DMA((2 * nj,))],
        args=(x, mod, vec_d, vec_c, cw, pw, win_g, wout_g), exchange=exchange)


def _mixer_backward(dxo, x, o, u, ac, dpl, mod, vec_d, vec_c, cw, pw, win_g, wout_g, taps, tm, name, exchange=None):
    nb, s, d = x.shape
    n = s // tm
    nj, _, dcol = win_g.shape
    din = nj * dcol
    dc = vec_c.shape[-1]
    dpool = din - 2 * dc
    dmix = dc + dpool
    ro = wout_g.shape[1]
    ngrp = dpool // LANES
    rext = tm + HALO

    def body(dxo_ref, x_ref, o_ref, u_ref, ac_ref, dp_ref, mod_ref, vd_ref, vc_ref, cw_ref, pw_ref, win_hbm, wout_hbm,
             dx_ref, du_ref, dob_ref, rowd_ref, rowb_ref, rowc_ref, dcw_ref, dpw_ref,
             win_v, wout_v, ext_a, ext_p, phases, sems):
        b, i = pl.program_id(0), pl.program_id(1)
        first = (b == 0) & (i == 0)
        pairs = [(win_hbm.at[j], win_v.at[:, pl.ds(j * dcol, dcol)]) for j in range(nj)]
        pairs += [(wout_hbm.at[j], wout_v.at[pl.ds(j * ro, ro), :]) for j in range(nj)]
        _load_weights(first, pairs, sems)

        @pl.when(first)
        def _():
            rowd_ref[...] = jnp.zeros_like(rowd_ref)
            rowc_ref[...] = jnp.zeros_like(rowc_ref)
            dcw_ref[...] = jnp.zeros_like(dcw_ref)
            dpw_ref[...] = jnp.zeros_like(dpw_ref)

        @pl.when(i == 0)
        def _():
            rowb_ref[...] = jnp.zeros_like(rowb_ref)
            ext_a[tm:rext, :] = jnp.zeros((HALO, dc), F32)
            ext_p[tm:rext, :] = jnp.zeros((HALO, dpool), F32)

        @pl.when(i > 0)
        def _():
            ext_a[tm:rext, :] = ext_a[0:HALO, :]
            ext_p[tm:rext, :] = ext_p[0:HALO, :]

        g_pre, g_post = vd_ref[0:1, :], vd_ref[1:2, :]
        sh, sc, gt = mod_ref[0:1, :], mod_ref[1:2, :], mod_ref[2:3, :]
        do, d_gt, d_gp = _gated_residual_bwd(dxo_ref[...], o_ref[...], g_post, gt)
        dob = do.astype(BF16)
        dob_ref[...] = dob
        dz = lax.dot_general(dob, wout_v[...], (((1,), (1,)), ((), ())), preferred_element_type=F32)

        acv = ac_ref[...].astype(F32)
        mu = jnp.mean(acv, axis=-1, keepdims=True)
        xc = acv - mu
        rstd = lax.rsqrt(jnp.mean(xc * xc, axis=-1, keepdims=True) + EPS)
        an = xc * rstd
        lg = vc_ref[1:2, :]
        al = an * lg + vc_ref[2:3, :]
        sg = _sigmoid(al)
        dal = dz[:, :dc] * (sg * (1.0 + al * (1.0 - sg)))
        d_lg = _colsum(dal * an)
        d_lb = _colsum(dal)
        dan = dal * lg
        dac = rstd * (dan - jnp.mean(dan, axis=-1, keepdims=True) - an * jnp.mean(dan * an, axis=-1, keepdims=True))
        d_cb = _colsum(dac)
        ext_a[0:tm, :] = dac
        uv = u_ref[:, 0:dc].astype(F32)
        sgu = _sigmoid(u_ref[:, dc:2 * dc].astype(F32))
        ag = uv * sgu
        dag = jnp.zeros((tm, dc), F32)
        _fill_phases(ext_a, phases)
        for k in range(taps):
            sl = _shifted_rows(ext_a, phases, taps - 1 - k, tm)
            dag = dag + cw_ref[k:k + 1, :] * sl
            dcw_ref[k:k + 1, :] += _colsum(ag * sl)
        du_ref[:, 0:dc] = (dag * sgu).astype(BF16)
        du_ref[:, dc:2 * dc] = (dag * uv * (sgu * (1.0 - sgu))).astype(BF16)

        pos = _seq_positions(n - 1 - i, tm, LANES)
        d_ps = []
        for g in range(ngrp):
            w = POOL_WINDOWS[g]
            cols = slice(g * LANES, (g + 1) * LANES)
            gcols = slice(dc + g * LANES, dc + (g + 1) * LANES)
            dgb = dp_ref[:, cols]
            q = jnp.dot(dgb, pw_ref[g], preferred_element_type=F32)
            dpg = dz[:, gcols]
            d_ps.append(_colsum(dpg * q))
            dq = (dpg * vc_ref[3:4, cols]).astype(BF16)
            dpw_ref[g] += lax.dot_general(dgb, dq, (((0,), (0,)), ((), ())), preferred_element_type=F32)
            dd = lax.dot_general(dq, pw_ref[g], (((1,), (1,)), ((), ())), preferred_element_type=F32)
            cnt = jnp.minimum(pos + 1, w).astype(F32)
            ext_p[0:tm, cols] = dd / cnt
            sw = ext_p[:, cols]
            step = 1
            while step < w:
                sw = sw + pltpu.roll(sw, rext - step, axis=0)
                step *= 2
            du_ref[:, 2 * dc + g * LANES:2 * dc + (g + 1) * LANES] = (sw[0:tm, :] - dd).astype(BF16)
        rowc_ref[0:1, :] += d_cb
        rowc_ref[1:2, :] += d_lg
        rowc_ref[2:3, :] += d_lb
        rowc_ref[3:4, :] += jnp.concatenate(d_ps, axis=-1)

        dh = lax.dot_general(du_ref[...], win_v[...], (((1,), (1,)), ((), ())), preferred_element_type=F32)
        _, xn, r1 = _ada_norm(x_ref[...], g_pre, sc, sh)
        dxb, d_sh, d_sc, d_g = _ada_norm_bwd(dh, xn, r1, g_pre, sc)
        dx_ref[...] = dxo_ref[...] + dxb
        rowd_ref[0:1, :] += d_g
        rowd_ref[1:2, :] += d_gp
        rowb_ref[0:1, :] += d_sh
        rowb_ref[1:2, :] += d_sc
        rowb_ref[2:3, :] += d_gt

    def tile(width):
        return pl.BlockSpec((None, tm, width), lambda b, i: (b, n - 1 - i, 0))

    return _grid_call(
        body, name, (nb, n),
        in_specs=[tile(d), tile(d), tile(d), tile(din), tile(dc), tile(dpool),
                  pl.BlockSpec((None, 8, d), lambda b, i: (b, 0, 0)), _full(vec_d.shape), _full(vec_c.shape),
                  _full(cw.shape), _full(pw.shape), _ANY, _ANY],
        out_specs=[tile(d), tile(din), tile(d), _full((8, d)), pl.BlockSpec((None, 8, d), lambda b, i: (b, 0, 0)),
                   _full((8, dc)), _full((HALO, dc)), _full(pw.shape)],
        out_shape=[jax.ShapeDtypeStruct((nb, s, d), F32), jax.ShapeDtypeStruct((nb, s, din), BF16),
                   jax.ShapeDtypeStruct((nb, s, d), BF16), jax.ShapeDtypeStruct((8, d), F32),
                   jax.ShapeDtypeStruct((nb, 8, d), F32), jax.ShapeDtypeStruct((8, dc), F32),
                   jax.ShapeDtypeStruct((HALO, dc), F32), jax.ShapeDtypeStruct(pw.shape, F32)],
        scratch_shapes=[pltpu.VMEM((d, din), BF16), pltpu.VMEM((dmix, d), BF16),
                        pltpu.VMEM((rext, dc), F32), pltpu.VMEM((rext, dpool), F32),
                        pltpu.VMEM((SUBLANES - 1, rext, dc), F32), pltpu.SemaphoreType.DMA((2 * nj,))],
        args=(dxo, x, o, u, ac, dpl, mod, vec_d, vec_c, cw, pw, win_g, wout_g), exchange=exchange)


def _ffn_forward(x, mod, vec_d, fw, wup_g, wdn_g, tm, name, exchange=None, target=None):
    nb, s, d = x.shape
    n = s // tm
    nj, _, ucol = wup_g.shape
    f2 = nj * ucol
    dff = f2 // 2
    rd = wdn_g.shape[1]
    nq = nj // 2
    cs = dff // nq

    with_loss = target is not None

    def body(*refs):
        refs = list(refs)
        x_ref, mod_ref, vd_ref, fw_ref = refs[:4]
        t_ref = refs.pop(4) if with_loss else None
        wup_hbm, wdn_hbm, xo_ref, h_ref, u_ref, uc_ref, hid_ref, o_ref = refs[4:12]
        sq_ref = refs.pop(12) if with_loss else None
        wup_v, wdn_v, prev_u, sems = refs[12:]
        b, i = pl.program_id(0), pl.program_id(1)
        pairs = [(wup_hbm.at[j], wup_v.at[:, pl.ds(j * ucol, ucol)]) for j in range(nj)]
        pairs += [(wdn_hbm.at[j], wdn_v.at[pl.ds(j * rd, rd), :]) for j in range(nj)]
        _load_weights((b == 0) & (i == 0), pairs, sems)

        if with_loss:
            @pl.when((b == 0) & (i == 0))
            def _():
                sq_ref[...] = jnp.zeros_like(sq_ref)

        @pl.when(i == 0)
        def _():
            prev_u[...] = jnp.zeros_like(prev_u)

        xv = x_ref[...]
        h, _, _ = _ada_norm(xv, vd_ref[2:3, :], mod_ref[4:5, :], mod_ref[3:4, :])
        hb = h.astype(BF16)
        h_ref[...] = hb

        def conv(cols):
            uc = jnp.dot(hb, wup_v[:, cols], preferred_element_type=F32)
            u_ref[:, cols] = uc.astype(BF16)
            before = prev_u[:, cols]
            prev_u[:, cols] = uc[tm - FHALO:, :]
            out = (fw_ref[3:4, cols] + fw_ref[2:3, cols] * uc + fw_ref[1:2, cols] * _rows_before(before, uc, 1)
                   + fw_ref[0:1, cols] * _rows_before(before, uc, 2))
            uc_ref[:, cols] = out.astype(BF16)
            return out

        o = jnp.zeros((tm, d), F32)
        for q in range(nq):
            val = conv(pl.ds(q * cs, cs))
            gate = conv(pl.ds(dff + q * cs, cs))
            hid = ((gate * _sigmoid(gate)) * val).astype(BF16)
            hid_ref[:, pl.ds(q * cs, cs)] = hid
            o = o + jnp.dot(hid, wdn_v[pl.ds(q * cs, cs), :], preferred_element_type=F32)
        o_ref[...] = o
        r2 = lax.rsqrt(jnp.mean(o * o, axis=-1, keepdims=True) + EPS)
        y = xv + (1.0 + mod_ref[5:6, :]) * ((o * r2) * vd_ref[3:4, :])
        if with_loss:
            e = y - t_ref[...]
            xo_ref[...] = e * (1.0 / d)
            sq_ref[0:1, :] += _colsum(e * e)
        else:
            xo_ref[...] = y

    def tile(width):
        return pl.BlockSpec((None, tm, width), lambda b, i: (b, i, 0))

    loss_in = [tile(d)] if with_loss else []
    return _grid_call(
        body, name, (nb, n),
        in_specs=[tile(d), pl.BlockSpec((None, 8, d), lambda b, i: (b, 0, 0)), _full(vec_d.shape), _full(fw.shape)]
        + loss_in + [_ANY, _ANY],
        out_specs=[tile(d), tile(d), tile(f2), tile(f2), tile(dff), tile(d)] + ([_full((8, d))] if with_loss else []),
        out_shape=[jax.ShapeDtypeStruct((nb, s, d), F32), jax.ShapeDtypeStruct((nb, s, d), BF16),
                   jax.ShapeDtypeStruct((nb, s, f2), BF16), jax.ShapeDtypeStruct((nb, s, f2), BF16),
                   jax.ShapeDtypeStruct((nb, s, dff), BF16), jax.ShapeDtypeStruct((nb, s, d), F32)]
        + ([jax.ShapeDtypeStruct((8, d), F32)] if with_loss else []),
        scratch_shapes=[pltpu.VMEM((d, f2), BF16), pltpu.VMEM((dff, d), BF16),
                        pltpu.VMEM((FHALO, f2), F32), pltpu.SemaphoreType.DMA((2 * nj,))],
        args=(x, mod, vec_d, fw) + ((target,) if with_loss else ()) + (wup_g, wdn_g), exchange=exchange)


def _ffn_backward(dxo, x, o, u, uc, mod, vec_d, fw, wup_g, wdn_g, tm, name, exchange=None):
    nb, s, d = x.shape
    n = s // tm
    nj, _, ucol = wup_g.shape
    f2 = nj * ucol
    dff = f2 // 2
    rd = wdn_g.shape[1]
    nq = nj // 2
    cs = dff // nq

    def body(dxo_ref, x_ref, o_ref, u_ref, uc_ref, mod_ref, vd_ref, fw_ref, wup_hbm, wdn_hbm,
             dx_ref, du_ref, dob_ref, rowd_ref, rowb_ref, dfw_ref,
             wup_v, wdn_v, next_d, sems):
        b, i = pl.program_id(0), pl.program_id(1)
        first = (b == 0) & (i == 0)
        pairs = [(wup_hbm.at[j], wup_v.at[:, pl.ds(j * ucol, ucol)]) for j in range(nj)]
        pairs += [(wdn_hbm.at[j], wdn_v.at[pl.ds(j * rd, rd), :]) for j in range(nj)]
        _load_weights(first, pairs, sems)

        @pl.when(first)
        def _():
            rowd_ref[...] = jnp.zeros_like(rowd_ref)
            dfw_ref[...] = jnp.zeros_like(dfw_ref)

        @pl.when(i == 0)
        def _():
            rowb_ref[...] = jnp.zeros_like(rowb_ref)
            next_d[...] = jnp.zeros_like(next_d)

        g_pre, g_post = vd_ref[2:3, :], vd_ref[3:4, :]
        sh, sc, gt = mod_ref[3:4, :], mod_ref[4:5, :], mod_ref[5:6, :]
        do, d_gt, d_gp = _gated_residual_bwd(dxo_ref[...], o_ref[...], g_post, gt)
        dob = do.astype(BF16)
        dob_ref[...] = dob

        def conv_bwd(cols, duc):
            uc = u_ref[:, cols].astype(F32)
            after = next_d[:, cols]
            next_d[:, cols] = duc[0:FHALO, :]
            d1 = _rows_after(duc, after, 1)
            d2 = _rows_after(duc, after, 2)
            dfw_ref[3:4, cols] += _colsum(duc)
            dfw_ref[2:3, cols] += _colsum(uc * duc)
            dfw_ref[1:2, cols] += _colsum(uc * d1)
            dfw_ref[0:1, cols] += _colsum(uc * d2)
            ob = (fw_ref[2:3, cols] * duc + fw_ref[1:2, cols] * d1 + fw_ref[0:1, cols] * d2).astype(BF16)
            du_ref[:, cols] = ob
            return lax.dot_general(ob, wup_v[:, cols], (((1,), (1,)), ((), ())), preferred_element_type=F32)

        dh = jnp.zeros((tm, d), F32)
        for q in range(nq):
            vcols = pl.ds(q * cs, cs)
            gcols = pl.ds(dff + q * cs, cs)
            dhid = lax.dot_general(dob, wdn_v[vcols, :], (((1,), (1,)), ((), ())), preferred_element_type=F32)
            val = uc_ref[:, vcols].astype(F32)
            gate = uc_ref[:, gcols].astype(F32)
            sg = _sigmoid(gate)
            act = gate * sg
            dval = dhid * act
            dgate = (dhid * val) * (sg + act * (1.0 - sg))
            dh = dh + conv_bwd(vcols, dval)
            dh = dh + conv_bwd(gcols, dgate)

        _, xn, r1 = _ada_norm(x_ref[...], g_pre, sc, sh)
        dxb, d_sh, d_sc, d_g = _ada_norm_bwd(dh, xn, r1, g_pre, sc)
        dx_ref[...] = dxo_ref[...] + dxb
        rowd_ref[2:3, :] += d_g
        rowd_ref[3:4, :] += d_gp
        rowb_ref[3:4, :] += d_sh
        rowb_ref[4:5, :] += d_sc
        rowb_ref[5:6, :] += d_gt

    def tile(width):
        return pl.BlockSpec((None, tm, width), lambda b, i: (b, n - 1 - i, 0))

    return _grid_call(
        body, name, (nb, n),
        in_specs=[tile(d), tile(d), tile(d), tile(f2), tile(f2), pl.BlockSpec((None, 8, d), lambda b, i: (b, 0, 0)),
                  _full(vec_d.shape), _full(fw.shape), _ANY, _ANY],
        out_specs=[tile(d), tile(f2), tile(d), _full((8, d)), pl.BlockSpec((None, 8, d), lambda b, i: (b, 0, 0)),
                   _full(fw.shape)],
        out_shape=[jax.ShapeDtypeStruct((nb, s, d), F32), jax.ShapeDtypeStruct((nb, s, f2), BF16),
                   jax.ShapeDtypeStruct((nb, s, d), BF16), jax.ShapeDtypeStruct((8, d), F32),
                   jax.ShapeDtypeStruct((nb, 8, d), F32), jax.ShapeDtypeStruct(fw.shape, F32)],
        scratch_shapes=[pltpu.VMEM((d, f2), BF16), pltpu.VMEM((dff, d), BF16),
                        pltpu.VMEM((FHALO, f2), F32), pltpu.SemaphoreType.DMA((2 * nj,))],
        args=(dxo, x, o, u, uc, mod, vec_d, fw, wup_g, wdn_g), exchange=exchange)


def _weight_grad(a, b, nblk, split, tt, name, exchange=None):
    t, ka = a.shape
    nb_ = b.shape[1]
    nk = t // tt
    if split == "cols":
        wa, wb, grid = ka, nb_ // nblk, (1, nk)
        a_spec = pl.BlockSpec((tt, ka), lambda j, k: (k, 0))
        b_spec = pl.BlockSpec((tt, nb_), lambda j, k: (k, 0))
        o_spec = pl.BlockSpec((nblk, wa, wb), lambda j, k: (0, 0, 0))
        acc_shape = (ka, nb_)
    elif split == "b":
        wa, wb, grid = ka, nb_ // nblk, (nblk, nk)
        a_spec = pl.BlockSpec((tt, wa), lambda j, k: (k, 0))
        b_spec = pl.BlockSpec((tt, wb), lambda j, k: (k, j))
        o_spec = pl.BlockSpec((None, wa, wb), lambda j, k: (j, 0, 0))
        acc_shape = (wa, wb)
    else:
        wa, wb, grid = ka // nblk, nb_, (nblk, nk)
        a_spec = pl.BlockSpec((tt, wa), lambda j, k: (k, j))
        b_spec = pl.BlockSpec((tt, wb), lambda j, k: (k, 0))
        o_spec = pl.BlockSpec((None, wa, wb), lambda j, k: (j, 0, 0))
        acc_shape = (wa, wb)

    def body(a_ref, b_ref, o_ref, acc):
        k = pl.program_id(1)
        prod = lax.dot_general(a_ref[...], b_ref[...], (((0,), (0,)), ((), ())), preferred_element_type=F32)

        @pl.when(k == 0)
        def _():
            acc[...] = prod

        @pl.when(k > 0)
        def _():
            acc[...] += prod

        @pl.when(k == nk - 1)
        def _():
            if split == "cols":
                for j in range(nblk):
                    o_ref[j] = acc[:, j * wb:(j + 1) * wb].astype(o_ref.dtype)
            else:
                o_ref[...] = acc[...].astype(o_ref.dtype)

    outs, exo = _grid_call(body, name, grid, in_specs=[a_spec, b_spec], out_specs=[o_spec],
                           out_shape=[jax.ShapeDtypeStruct((nblk, wa, wb), BF16)],
                           scratch_shapes=[pltpu.VMEM(acc_shape, F32)], args=(a, b), exchange=exchange)
    return outs[0], exo


def _rows128(a):
    return a.reshape(-1, LANES)


class _ReduceScatter:
    def __init__(self, gs, cidx, idx, tag):
        self.gs, self.cidx, self.idx, self.tag = gs, cidx, idx, tag

    def swap(self):
        return _swap_halves(self.gs)

    def after_swap(self, r1):
        self.ps = [_pair_sum(g, r, self.cidx, name=f"rs_pair_{self.tag}_{a}") for a, (g, r) in enumerate(zip(self.gs, r1))]

    def chips(self):
        return _chip_exchange(self.ps)

    def after_chips(self, r2):
        self.fh = [_chip_sum(p, r, self.idx, name=f"rs_sum_{self.tag}_{a}") for a, (p, r) in enumerate(zip(self.ps, r2))]

    def share(self):
        return _sibling_share(self.fh)

    @staticmethod
    def result(fs):
        return [f.reshape(f.shape[0] * f.shape[1], f.shape[2]) for f in fs]


def kernel(x, c, ada_w, ada_b, pre_mix_g, post_mix_g, w_in, conv_w, conv_b, conv_ln_g, conv_ln_b, pool_w, pool_scale, w_out, pre_ffn_g, post_ffn_g, ffn_up, ffn_conv_w, ffn_conv_b, ffn_down, loss_target, m_ada_w, m_ada_b, m_pre_mix_g, m_post_mix_g, m_w_in, m_conv_w, m_conv_b, m_conv_ln_g, m_conv_ln_b, m_pool_w, m_pool_scale, m_w_out, m_pre_ffn_g, m_post_ffn_g, m_ffn_up, m_ffn_conv_w, m_ffn_conv_b, m_ffn_down, v_ada_w, v_ada_b, v_pre_mix_g, v_post_mix_g, v_w_in, v_conv_w, v_conv_b, v_conv_ln_g, v_conv_ln_b, v_pool_w, v_pool_scale, v_w_out, v_pre_ffn_g, v_post_ffn_g, v_ffn_up, v_ffn_conv_w, v_ffn_conv_b, v_ffn_down):
    nb, s, d = x.shape
    nl = w_in.shape[0]
    taps = conv_w.shape[1]
    ccol = conv_w.shape[2]
    dc = conv_b.shape[1]
    fcol = ffn_conv_w.shape[2]
    f2 = ffn_conv_b.shape[1]
    nmod = ada_b.shape[1] // d
    acol = ada_w.shape[2]
    tm = min(MLP_TILE_ROWS, s)
    tm_mix = min(MIXER_TILE_ROWS, s)
    tt = min(GRAD_CHUNK_ROWS, (nb * s) // 2)

    xi, yi, ci = _pos()
    jm = 2 * xi + yi
    cidx = jnp.reshape(ci, (1,)).astype(jnp.int32)
    idx = jnp.stack([jm, ci]).astype(jnp.int32)

    win_b, wout_b, wup_b, wdn_b = (w.astype(BF16) for w in (w_in, w_out, ffn_up, ffn_down))

    def others(l):
        return [win_b[l], wout_b[l], wdn_b[l]]

    n_cw, n_fw, n_c = nl * taps * ccol, nl * 3 * fcol, nb * d
    packed = jnp.concatenate([conv_w.reshape(-1), ffn_conv_w.reshape(-1), c.reshape(-1)])
    got = _gather8(_rows128(packed), name="gather_small").reshape(N_DEV, -1)
    chips = got[0::2]
    cw_full = chips[:, :n_cw].reshape(N_CHIPS, nl, taps, ccol).transpose(1, 2, 0, 3).reshape(nl, taps, dc)
    fw_full = chips[:, n_cw:n_cw + n_fw].reshape(N_CHIPS, nl, 3, fcol).transpose(1, 2, 0, 3).reshape(nl, 3, f2)
    c_all = got[:, n_cw + n_fw:].reshape(N_DEV * nb, d)

    ada_b_cols = lax.dynamic_slice_in_dim(ada_b, jm * acol, acol, axis=1).reshape(nl, 1, acol)
    mod_cols, first_weights = _ada_forward(c_all, ada_w, ada_b_cols, name="ada_forward",
                                           exchange=_gather(others(0), mid_at=0.9))
    by_owner = mod_cols.reshape(nl, N_DEV, nb, acol).transpose(1, 0, 2, 3).reshape(N_DEV, -1, LANES)
    mod_own = _rows_to_owners(by_owner, name="mod_to_owners").reshape(N_CHIPS, nl, nb, acol)
    mod_own = mod_own.transpose(1, 2, 0, 3).reshape(nl, nb, nmod, d)
    mod_own = jnp.pad(mod_own, ((0, 0), (0, 0), (0, 8 - nmod), (0, 0)))

    vec_d = jnp.stack([pre_mix_g, post_mix_g, pre_ffn_g, post_ffn_g], axis=1)
    vec_c = jnp.stack([conv_b, conv_ln_g, conv_ln_b, pool_scale], axis=1)
    cw_pad = jnp.pad(cw_full, ((0, 0), (0, HALO - taps), (0, 0)))
    fw_rows = jnp.concatenate([fw_full, ffn_conv_b[:, None, :], jnp.zeros((nl, 4, f2), F32)], axis=1)
    pw_b = pool_w.astype(BF16)

    win_g, wout_g, wdn_g = _whole(first_weights)
    saved = []
    xs = x
    for l in range(nl):
        (x1, h1, u1, ac1, dp1, z1, o1), got = _mixer_forward(
            xs, mod_own[l], vec_d[l], vec_c[l], cw_pad[l], pw_b[l], win_g, wout_g, taps, tm_mix, name=f"mixer_fwd_{l}",
            exchange=_gather([wup_b[l]], mid_at=0.9))
        wup_g, = _whole(got)
        last = l + 1 == nl
        (x2, h2, u2, uc2, hid2, o2, *sq), nxt = _ffn_forward(
            x1, mod_own[l], vec_d[l], fw_rows[l], wup_g, wdn_g, tm, name=f"ffn_fwd_{l}",
            exchange=None if last else _gather(others(l + 1), mid_at=0.6), target=loss_target if last else None)
        saved.append((xs, h1, u1, ac1, dp1, z1, o1, x1, h2, u2, uc2, hid2, o2, win_g, wout_g, wup_g, wdn_g))
        if not last:
            win_g, wout_g, wdn_g = _whole(nxt)
        xs = x2

    dx = xs
    loss_here = (0.5 * jnp.sum(sq[0]) / d).reshape(1)

    flat = lambda a: a.reshape(nb * s, a.shape[-1])
    small = [None] * nl
    big_mlp, big_mix = [None] * nl, [None] * nl
    mlp = mix = None
    for l in reversed(range(nl)):
        x0, h1, u1, ac1, dp1, z1, o1, x1, h2, u2, uc2, hid2, o2, win_g, wout_g, wup_g, wdn_g = saved[l]
        (dx, du2, do2, rowd2, rowb2, dfw), got = _ffn_backward(
            dx, x1, o2, u2, uc2, mod_own[l], vec_d[l], fw_rows[l], wup_g, wdn_g, tm, name=f"ffn_bwd_{l}",
            exchange=_combine([mlp.chips(), mix.swap()]) if mlp else None)
        if mlp:
            mlp.after_chips(got[:2])
            mix.after_swap(got[2:])
        g_up, got = _weight_grad(flat(h2), flat(du2), N_CHIPS, "b", tt, name=f"grad_ffn_up_{l}",
                                 exchange=_combine([mlp.share(), mix.chips()]) if mlp else None)
        if mlp:
            big_mlp[l + 1] = mlp.result(got[:2])
            mix.after_chips(got[2:])
        g_dn, got = _weight_grad(flat(hid2), flat(do2), 2, "a", tt, name=f"grad_ffn_down_{l}",
                                 exchange=_swap_halves([g_up]) if l == 0 else None)
        g_dn = g_dn.reshape(N_CHIPS, -1, d)
        mlp_above, mlp = mlp, _ReduceScatter([g_up, g_dn], cidx, idx, f"mlp_{l}")
        if l == 0:
            mlp.after_swap(list(got) + list(_run_exchange(_swap_halves([g_dn]), name="rs_swap_down_0")))
        first = mlp.swap() if l > 0 else mlp.chips()
        (dx, du1, do1, rowd1, rowb1, rowc, dcw, dpw), got = _mixer_backward(
            dx, x0, o1, u1, ac1, dp1, mod_own[l], vec_d[l], vec_c[l], cw_pad[l], pw_b[l], win_g, wout_g, taps, tm_mix,
            name=f"mixer_bwd_{l}", exchange=_combine([first, mix.share()]) if mlp_above else first)
        if l > 0:
            mlp.after_swap(got[:2])
        else:
            mlp.after_chips(got[:2])
        if mlp_above:
            big_mix[l + 1] = mix.result(got[2:])
        g_in, got = _weight_grad(flat(h1), flat(du1), N_CHIPS, "cols", tt, name=f"grad_w_in_{l}",
                                 exchange=mlp.share() if l == 0 else None)
        if l == 0:
            big_mlp[0] = mlp.result(got)
        g_out, _ = _weight_grad(flat(z1), flat(do1), 1, "cols", tt, name=f"grad_w_out_{l}")
        mix = _ReduceScatter([g_in, g_out.reshape(N_CHIPS, -1, d)], cidx, idx, f"mix_{l}")
        small[l] = dict(rowd=rowd1 + rowd2, rowb=rowb1 + rowb2, rowc=rowc, dcw=dcw[:taps], dpw=dpw, dfw=dfw)
    mix.after_swap(_run_exchange(mix.swap(), name="rs_swap_mix_0"))

    dmod_own = jnp.stack([small[l]["rowb"][:, :nmod, :] for l in range(nl)])
    dmod_all, got = _gather8(_rows128(dmod_own), name="gather_dmod", exchange=mix.chips())
    mix.after_chips(got)
    big_mix[0] = mix.result(_run_exchange(mix.share(), name="rs_share_mix_0"))
    dmod_all = dmod_all.reshape(N_DEV, nl, nb, nmod * d)
    dmod_all = dmod_all.transpose(1, 0, 2, 3).reshape(nl, N_DEV * nb, nmod * d)
    dmod_cols = lax.dynamic_slice_in_dim(dmod_all, jm * acol, acol, axis=2)
    (g_ada_w, d_ada_w, nm_ada_w, nv_ada_w), _ = _ada_update(c_all, dmod_cols, ada_w, m_ada_w, v_ada_w, name="ada_update")

    def st(key, row=None):
        return jnp.stack([small[l][key] if row is None else small[l][key][row] for l in range(nl)])

    local = {
        "ada_b": dmod_own.sum(axis=1).reshape(nl, nmod * d),
        "pre_mix_g": st("rowd", 0), "post_mix_g": st("rowd", 1),
        "conv_b": st("rowc", 0), "conv_ln_g": st("rowc", 1), "conv_ln_b": st("rowc", 2),
        "pool_w": st("dpw"), "pool_scale": st("rowc", 3),
        "pre_ffn_g": st("rowd", 2), "post_ffn_g": st("rowd", 3),
        "ffn_conv_b": st("dfw", 3), "conv_w": st("dcw"), "ffn_conv_w": jnp.stack([small[l]["dfw"][:3] for l in range(nl)]),
    }
    names = list(local)
    sizes = [local[k].size for k in names]
    pad = -(sum(sizes) + 1) % (4 * SUBLANES * LANES)
    packed = jnp.concatenate([local[k].reshape(-1) for k in names] + [loss_here, jnp.zeros((pad,), F32)])
    summed = _allreduce8(_rows128(packed), name="allreduce_small").reshape(-1)
    loss = summed[sum(sizes)]
    grads, off = {}, 0
    for k, sz in zip(names, sizes):
        grads[k] = summed[off:off + sz].reshape(local[k].shape)
        off += sz
    grads["conv_w"] = lax.dynamic_slice_in_dim(grads["conv_w"], jm * ccol, ccol, axis=2)
    grads["ffn_conv_w"] = lax.dynamic_slice_in_dim(grads["ffn_conv_w"], jm * fcol, fcol, axis=2)

    params = dict(ada_b=(ada_b, m_ada_b, v_ada_b), pre_mix_g=(pre_mix_g, m_pre_mix_g, v_pre_mix_g),
                  post_mix_g=(post_mix_g, m_post_mix_g, v_post_mix_g), conv_b=(conv_b, m_conv_b, v_conv_b),
                  conv_ln_g=(conv_ln_g, m_conv_ln_g, v_conv_ln_g), conv_ln_b=(conv_ln_b, m_conv_ln_b, v_conv_ln_b),
                  pool_w=(pool_w, m_pool_w, v_pool_w), pool_scale=(pool_scale, m_pool_scale, v_pool_scale),
                  pre_ffn_g=(pre_ffn_g, m_pre_ffn_g, v_pre_ffn_g), post_ffn_g=(post_ffn_g, m_post_ffn_g, v_post_ffn_g),
                  ffn_conv_b=(ffn_conv_b, m_ffn_conv_b, v_ffn_conv_b), conv_w=(conv_w, m_conv_w, v_conv_w),
                  ffn_conv_w=(ffn_conv_w, m_ffn_conv_w, v_ffn_conv_w))
    pack = lambda i, g=None: _rows128(jnp.concatenate([(grads[k] if g else params[k][i]).reshape(-1) for k in names]))
    sd, sm, sv = _adamw_flat(pack(0), pack(0, True), pack(1), pack(2), name="adamw_small")
    outs = {}
    off = 0
    for k in names:
        shape, sz = params[k][0].shape, params[k][0].size
        outs[k] = (grads[k],) + tuple(a.reshape(-1)[off:off + sz].reshape(shape) for a in (sd, sm, sv))
        off += sz

    outs["ada_w"] = (g_ada_w, d_ada_w, nm_ada_w, nv_ada_w)
    for k, w, m, v, gs in [("w_in", w_in, m_w_in, v_w_in, [big_mix[l][0] for l in range(nl)]),
                           ("w_out", w_out, m_w_out, v_w_out, [big_mix[l][1] for l in range(nl)]),
                           ("ffn_up", ffn_up, m_ffn_up, v_ffn_up, [big_mlp[l][0] for l in range(nl)]),
                           ("ffn_down", ffn_down, m_ffn_down, v_ffn_down, [big_mlp[l][1] for l in range(nl)])]:
        outs[k] = tuple(_adamw_layers(w, m, v, gs, name=f"adamw_{k}"))

    order = ["ada_w", "ada_b", "pre_mix_g", "post_mix_g", "w_in", "conv_w", "conv_b", "conv_ln_g", "conv_ln_b", "pool_w",
             "pool_scale", "w_out", "pre_ffn_g", "post_ffn_g", "ffn_up", "ffn_conv_w", "ffn_conv_b", "ffn_down"]
    return (loss, dx) + tuple(outs[k][i] for i in range(4) for k in order)
```

```python
import functools

import jax
import jax.numpy as jnp
from jax import lax
from jax.experimental import pallas as pl
from jax.experimental.pallas import tpu as pltpu

F32 = jnp.float32
BF16 = jnp.bfloat16
MESH = pl.DeviceIdType.MESH

EPS = 1e-6
POOL_WINDOWS = (2, 4, 8, 16)
ADAM_LR = 0.001
ADAM_B1 = 0.9
ADAM_B2 = 0.999
ADAM_EPS = 1e-08
ADAM_WD = 0.01
ADAM_STEP = 10

N_CHIPS = 4
N_DEV = 8
LANES = 128
SUBLANES = 8
HALO = 32
FHALO = 8
VMEM_LIMIT = 60 * 1024 * 1024
MLP_TILE_ROWS = 256
MIXER_TILE_ROWS = 512
GRAD_CHUNK_ROWS = 2048


def _pos():
    return lax.axis_index("x"), lax.axis_index("y"), lax.axis_index("c")


def _flip(v, f):
    return 1 - v if f else v


def _full(shape):
    nd = len(shape)
    return pl.BlockSpec(shape, lambda *_: (0,) * nd)


_ANY = pl.BlockSpec(memory_space=pl.ANY)
_VMEM = pl.BlockSpec(memory_space=pltpu.VMEM)


def _sigmoid(v):
    return 1.0 / (1.0 + jnp.exp(-v))


def _colsum(v):
    return jnp.sum(v, axis=0, keepdims=True)


def _gather8(v, name, exchange=None):
    r, ccols = v.shape
    ex = exchange
    nci, nco = (len(ex.ins), len(ex.outs)) if ex else (0, 0)

    def body(*refs):
        v_ref, cin, out_ref, cout = refs[0], refs[1:1 + nci], refs[1 + nci], refs[2 + nci:2 + nci + nco]
        send_sems, recv_sems, local_sem = refs[2 + nci + nco:5 + nci + nco]
        if ex:
            sems = _Sems(*refs[5 + nci + nco:])
            ex.start(cin, cout, sems)
        x, y, c = _pos()
        me = 4 * x + 2 * y + c
        mine = pltpu.make_async_copy(v_ref, out_ref.at[me], local_sem)
        mine.start()
        peers = [(_flip(x, (k >> 2) & 1), _flip(y, (k >> 1) & 1), _flip(c, k & 1)) for k in range(1, N_DEV)]
        sends = []
        for k, peer in enumerate(peers):
            cp = pltpu.make_async_remote_copy(src_ref=v_ref, dst_ref=out_ref.at[me], send_sem=send_sems.at[k],
                                              recv_sem=recv_sems.at[k], device_id=peer, device_id_type=MESH)
            cp.start()
            sends.append(cp)
        for k, peer in enumerate(peers):
            pidx = 4 * peer[0] + 2 * peer[1] + peer[2]
            pltpu.make_async_remote_copy(src_ref=v_ref, dst_ref=out_ref.at[pidx], send_sem=send_sems.at[k],
                                         recv_sem=recv_sems.at[k], device_id=peer, device_id_type=MESH).wait_recv()
        for cp in sends:
            cp.wait_send()
        mine.wait()
        if ex:
            if ex.mid is not None:
                ex.mid(cin, cout, sems)
            ex.finish(cin, cout, sems)

    outs = pl.pallas_call(
        body, name=name, out_shape=[jax.ShapeDtypeStruct((N_DEV, r, ccols), v.dtype)] + (ex.outs if ex else []),
        in_specs=[_VMEM] + [_ANY] * nci, out_specs=[_VMEM] + [_ANY] * nco,
        scratch_shapes=[pltpu.SemaphoreType.DMA((N_DEV - 1,)), pltpu.SemaphoreType.DMA((N_DEV - 1,)),
                        pltpu.SemaphoreType.DMA(())] + (ex.scratch() if ex else []),
        input_output_aliases={1 + a: 1 + b for a, b in ex.aliases.items()} if ex else {},
        compiler_params=pltpu.CompilerParams(vmem_limit_bytes=VMEM_LIMIT),
    )(v, *(ex.ins if ex else []))
    return (outs[0], list(outs[1:])) if ex else outs[0]


def _rows_to_owners(v, name):
    _, r, ccols = v.shape

    def body(v_ref, out_ref, send_sems, recv_sems, local_sem):
        x, y, c = _pos()
        jm = 2 * x + y
        mine = pltpu.make_async_copy(v_ref.at[2 * jm + c], out_ref.at[jm], local_sem)
        mine.start()
        peers, pjs = _chip_peers(x, y, c)
        sends = []
        for k, peer in enumerate(peers):
            cp = pltpu.make_async_remote_copy(src_ref=v_ref.at[2 * pjs[k] + c], dst_ref=out_ref.at[jm],
                                              send_sem=send_sems.at[k], recv_sem=recv_sems.at[k],
                                              device_id=peer, device_id_type=MESH)
            cp.start()
            sends.append(cp)
        for k, peer in enumerate(peers):
            pltpu.make_async_remote_copy(src_ref=v_ref.at[0], dst_ref=out_ref.at[pjs[k]], send_sem=send_sems.at[k],
                                         recv_sem=recv_sems.at[k], device_id=peer, device_id_type=MESH).wait_recv()
        for cp in sends:
            cp.wait_send()
        mine.wait()

    return pl.pallas_call(
        body, name=name, out_shape=jax.ShapeDtypeStruct((N_CHIPS, r, ccols), v.dtype),
        in_specs=[_VMEM], out_specs=_VMEM,
        scratch_shapes=[pltpu.SemaphoreType.DMA((N_CHIPS - 1,)), pltpu.SemaphoreType.DMA((N_CHIPS - 1,)),
                        pltpu.SemaphoreType.DMA(())],
        compiler_params=pltpu.CompilerParams(vmem_limit_bytes=VMEM_LIMIT),
    )(v)


def _allreduce8(v, name):
    r, ccols = v.shape
    h = r // 2
    q = h // 2

    def body(v_ref, out_ref, whole, part, done, send_sems, recv_sems):
        x, y, c = _pos()
        sib = (x, y, 1 - c)
        mine = pl.ds(pl.multiple_of(c * h, SUBLANES), h)
        theirs = pl.ds(pl.multiple_of((1 - c) * h, SUBLANES), h)
        quarters = [pl.ds(pl.multiple_of(c * h + k * q, SUBLANES), q) for k in range(2)]
        along_x, along_y = (1 - x, y, c), (x, 1 - y, c)

        def exchange(pairs):
            cps = [pltpu.make_async_remote_copy(src_ref=src, dst_ref=dst, send_sem=send_sems.at[k], recv_sem=recv_sems.at[k],
                                                device_id=peer, device_id_type=MESH) for src, dst, k, peer in pairs]
            for cp in cps:
                cp.start()
            for cp in cps:
                cp.wait()

        exchange([(v_ref, whole, 0, sib)])
        out_ref[...] = v_ref[...] + whole[...]
        for stage, peers in enumerate(((along_x, along_y), (along_y, along_x))):
            exchange([(out_ref.at[quarters[k]], part.at[2 * stage + k], 1 + 2 * stage + k, peers[k]) for k in range(2)])
            for k in range(2):
                out_ref[quarters[k], :] = out_ref[quarters[k], :] + part[2 * stage + k]
        exchange([(out_ref.at[mine], done, 5, sib)])
        out_ref[theirs, :] = done[...]

    return pl.pallas_call(
        body, name=name, out_shape=jax.ShapeDtypeStruct((r, ccols), v.dtype),
        in_specs=[_VMEM], out_specs=_VMEM,
        scratch_shapes=[pltpu.VMEM((r, ccols), v.dtype), pltpu.VMEM((4, q, ccols), v.dtype), pltpu.VMEM((h, ccols), v.dtype),
                        pltpu.SemaphoreType.DMA((6,)), pltpu.SemaphoreType.DMA((6,))],
        compiler_params=pltpu.CompilerParams(vmem_limit_bytes=VMEM_LIMIT),
    )(v)


def _chip_peers(x, y, c):
    peers = [(_flip(x, (k >> 1) & 1), _flip(y, k & 1), c) for k in range(1, N_CHIPS)]
    return peers, [2 * p[0] + p[1] for p in peers]


class _Exchange:
    def __init__(self, ins, outs, aliases, n_sems, n_local, start, finish, mid=None, mid_at=1.0, sibling=False, chips=False):
        self.ins, self.outs, self.aliases = list(ins), list(outs), dict(aliases)
        self.n_sems, self.n_local, self.start, self.finish = n_sems, n_local, start, finish
        self.mid, self.mid_at = mid, mid_at
        self.sibling, self.chips = sibling, chips

    def collective_id(self):
        return {(True, False): 1, (False, True): 2, (True, True): 3}[(self.sibling, self.chips)]

    def handshake(self):
        x, y, c = _pos()
        peers = ([(x, y, 1 - c)] if self.sibling else []) + (_chip_peers(x, y, c)[0] if self.chips else [])
        barrier = pltpu.get_barrier_semaphore()
        for peer in peers:
            pl.semaphore_signal(barrier, inc=1, device_id=peer, device_id_type=MESH)
        pl.semaphore_wait(barrier, len(peers))

    def scratch(self):
        return [pltpu.SemaphoreType.DMA((self.n_sems,)), pltpu.SemaphoreType.DMA((self.n_sems,)),
                pltpu.SemaphoreType.DMA((max(self.n_local, 1),))]


class _Sems:
    def __init__(self, send, recv, local, base=0, lbase=0):
        self.send, self.recv, self.loc, self.base, self.lbase = send, recv, local, base, lbase

    def shifted(self, by, lby):
        return _Sems(self.send, self.recv, self.loc, self.base + by, self.lbase + lby)

    def local(self, k):
        return self.loc.at[self.lbase + k]


def _remote(src, dst, sems, k, peer):
    return pltpu.make_async_remote_copy(src_ref=src, dst_ref=dst, send_sem=sems.send.at[sems.base + k],
                                        recv_sem=sems.recv.at[sems.base + k], device_id=peer, device_id_type=MESH)


def _combine(exs):
    ins = [a for ex in exs for a in ex.ins]
    outs = [o for ex in exs for o in ex.outs]
    aliases, spans, ni, no, ns, nloc = {}, [], 0, 0, 0, 0
    for ex in exs:
        aliases.update({ni + a: no + b for a, b in ex.aliases.items()})
        spans.append((ni, no, ns, nloc))
        ni, no, ns, nloc = ni + len(ex.ins), no + len(ex.outs), ns + ex.n_sems, nloc + ex.n_local

    def each(which):
        def run(ins_, outs_, sems):
            for ex, (i0, o0, s0, l0) in zip(exs, spans):
                stage = getattr(ex, which)
                if stage is not None:
                    stage(ins_[i0:i0 + len(ex.ins)], outs_[o0:o0 + len(ex.outs)], sems.shifted(s0, l0))
        return run

    mids = [ex.mid_at for ex in exs if ex.mid is not None]
    return _Exchange(ins, outs, aliases, ns, nloc, each("start"), each("finish"),
                     mid=each("mid") if mids else None, mid_at=max(mids) if mids else 1.0,
                     sibling=any(ex.sibling for ex in exs), chips=any(ex.chips for ex in exs))


def _gather(shards, mid_at=1.0):
    n = len(shards)
    per = N_CHIPS - 1
    halves = [s.reshape(2, s.shape[0] // 2, s.shape[1]) for s in shards]

    def copies(ins, outs, sems):
        x, y, c = _pos()
        jm = 2 * x + y
        sib = (x, y, 1 - c)
        peers, pjs = _chip_peers(x, y, c)
        sends, recvs, passes, passed = [], [], [], []
        for a in range(n):
            own = _remote(ins[a], outs[a].at[jm], sems, 2 * n * per + a, sib)
            sends.append(own)
            passed.append(own)
            for k, peer in enumerate(peers):
                landed, theirs = outs[a].at[pjs[k], c], outs[a].at[pjs[k], 1 - c]
                sends.append(_remote(ins[a].at[c], outs[a].at[jm, c], sems, 2 * (a * per + k), peer))
                recvs.append(_remote(landed, landed, sems, 2 * (a * per + k), peer))
                passes.append(_remote(landed, landed, sems, 2 * (a * per + k) + 1, sib))
                passed.append(_remote(theirs, theirs, sems, 2 * (a * per + k) + 1, sib))
        return sends, recvs, passes, passed

    def start(ins, outs, sems):
        for cp in copies(ins, outs, sems)[0]:
            cp.start()

    def mid(ins, outs, sems):
        _, recvs, passes, _ = copies(ins, outs, sems)
        for got, fwd in zip(recvs, passes):
            got.wait_recv()
            fwd.start()

    def finish(ins, outs, sems):
        sends, _, passes, passed = copies(ins, outs, sems)
        for cp in passed:
            cp.wait_recv()
        for cp in sends + passes:
            cp.wait_send()

    outs = [jax.ShapeDtypeStruct((N_CHIPS,) + h.shape, h.dtype) for h in halves]
    return _Exchange(halves, outs, {}, 2 * n * per + n, 0, start, finish, mid=mid, mid_at=mid_at, sibling=True, chips=True)


def _whole(gathered):
    return [g.reshape(g.shape[0], g.shape[1] * g.shape[2], g.shape[3]) for g in gathered]


def _swap_halves(gs):
    n = len(gs)
    halves = [g.reshape(g.shape[0], 2, g.shape[1] // 2, g.shape[2]) for g in gs]

    def copies(ins, outs, sems):
        x, y, c = _pos()
        sib = (x, y, 1 - c)
        return [_remote(ins[a].at[:, 1 - c], outs[a], sems, a, sib) for a in range(n)]

    def start(ins, outs, sems):
        for cp in copies(ins, outs, sems):
            cp.start()

    def finish(ins, outs, sems):
        for cp in copies(ins, outs, sems):
            cp.wait()

    outs = [jax.ShapeDtypeStruct((g.shape[0], g.shape[1] // 2, g.shape[2]), g.dtype) for g in gs]
    return _Exchange(halves, outs, {}, n, 0, start, finish, sibling=True)


def _chip_exchange(ps):
    n = len(ps)
    per = N_CHIPS - 1

    def copies(ins, outs, sems):
        x, y, c = _pos()
        peers, pjs = _chip_peers(x, y, c)
        return [_remote(ins[a].at[pjs[k]], outs[a].at[k], sems, a * per + k, peer)
                for a in range(n) for k, peer in enumerate(peers)]

    def start(ins, outs, sems):
        for cp in copies(ins, outs, sems):
            cp.start()

    def finish(ins, outs, sems):
        for cp in copies(ins, outs, sems):
            cp.wait()

    outs = [jax.ShapeDtypeStruct((per,) + p.shape[1:], p.dtype) for p in ps]
    return _Exchange(ps, outs, {}, n * per, 0, start, finish, chips=True)


def _sibling_share(fs):
    n = len(fs)

    def copies(outs, sems):
        x, y, c = _pos()
        sib = (x, y, 1 - c)
        sends = [_remote(outs[a].at[c], outs[a].at[c], sems, a, sib) for a in range(n)]
        recvs = [_remote(outs[a].at[1 - c], outs[a].at[1 - c], sems, a, sib) for a in range(n)]
        return sends, recvs

    def start(ins, outs, sems):
        for cp in copies(outs, sems)[0]:
            cp.start()

    def finish(ins, outs, sems):
        sends, recvs = copies(outs, sems)
        for cp in recvs:
            cp.wait_recv()
        for cp in sends:
            cp.wait_send()

    outs = [jax.ShapeDtypeStruct(f.shape, f.dtype) for f in fs]
    return _Exchange(fs, outs, {a: a for a in range(n)}, n, 0, start, finish, sibling=True)


def _run_exchange(ex, name):
    ni, no = len(ex.ins), len(ex.outs)

    def body(*refs):
        ins, outs, sems = refs[:ni], refs[ni:ni + no], _Sems(*refs[ni + no:])
        ex.handshake()
        ex.start(ins, outs, sems)
        if ex.mid is not None:
            ex.mid(ins, outs, sems)
        ex.finish(ins, outs, sems)

    return pl.pallas_call(
        body, name=name, out_shape=ex.outs, in_specs=[_ANY] * ni, out_specs=[_ANY] * no,
        input_output_aliases=ex.aliases, scratch_shapes=ex.scratch(),
        compiler_params=pltpu.CompilerParams(collective_id=ex.collective_id()),
    )(*ex.ins)


def _grid_call(body, name, grid, in_specs, out_specs, out_shape, scratch_shapes, args, exchange=None):
    ni, no = len(in_specs), len(out_specs)
    params = pltpu.CompilerParams(dimension_semantics=("arbitrary",) * len(grid), vmem_limit_bytes=VMEM_LIMIT)
    if exchange is None:
        outs = pl.pallas_call(body, name=name, grid=grid, in_specs=in_specs, out_specs=out_specs, out_shape=out_shape,
                              scratch_shapes=scratch_shapes, compiler_params=params)(*args)
        return list(outs), []
    ex = exchange
    nci, nco = len(ex.ins), len(ex.outs)

    def hosted(*refs):
        cin = refs[ni:ni + nci]
        cout = refs[ni + nci + no:ni + nci + no + nco]
        sems = _Sems(*refs[len(refs) - 3:])
        main = refs[:ni] + refs[ni + nci:ni + nci + no] + refs[ni + nci + no + nco:len(refs) - 3]
        ids = [pl.program_id(a) for a in range(len(grid))]
        first = functools.reduce(lambda p, q: p & q, [i == 0 for i in ids])
        last = functools.reduce(lambda p, q: p & q, [i == g - 1 for i, g in zip(ids, grid)])

        @pl.when(first)
        def _():
            ex.handshake()
            ex.start(cin, cout, sems)

        if ex.mid is not None:
            steps = functools.reduce(lambda p, q: p * q, grid)
            flat = functools.reduce(lambda p, q: p * q[1] + q[0], zip(ids[1:], grid[1:]), ids[0])

            @pl.when(flat == min(steps - 1, int(ex.mid_at * steps)))
            def _():
                ex.mid(cin, cout, sems)

        body(*main)

        @pl.when(last)
        def _():
            ex.finish(cin, cout, sems)

    outs = pl.pallas_call(
        hosted, name=name, grid=grid, in_specs=list(in_specs) + [_ANY] * nci, out_specs=list(out_specs) + [_ANY] * nco,
        out_shape=list(out_shape) + ex.outs, scratch_shapes=list(scratch_shapes) + ex.scratch(),
        input_output_aliases={ni + a: no + b for a, b in ex.aliases.items()},
        compiler_params=pltpu.CompilerParams(dimension_semantics=("arbitrary",) * len(grid), vmem_limit_bytes=VMEM_LIMIT,
                                             collective_id=ex.collective_id()),
    )(*args, *ex.ins)
    return list(outs[:no]), list(outs[no:])


SUM_BLOCK_BYTES = 4 * 1024 * 1024
ADAM_BLOCK_BYTES = 2 * 1024 * 1024


def _row_tile(rows, cols, itemsize, budget):
    best = None
    for t in range(16, rows + 1, 16):
        if rows % t == 0 and t * cols * itemsize <= budget:
            best = t
    return best if best is not None else rows


def _pair_sum(g, r1, cidx, name):
    nj, r, ccols = g.shape
    hr = r // 2
    tr = _row_tile(hr, ccols, 4, SUM_BLOCK_BYTES)
    nt = hr // tr

    def body(c_ref, g_ref, r_ref, o_ref):
        o_ref[...] = (g_ref[...].astype(F32) + r_ref[...].astype(F32)).astype(o_ref.dtype)

    return pl.pallas_call(
        body, name=name, out_shape=jax.ShapeDtypeStruct((nj, hr, ccols), g.dtype),
        grid_spec=pltpu.PrefetchScalarGridSpec(
            num_scalar_prefetch=1, grid=(nj, nt),
            in_specs=[pl.BlockSpec((None, tr, ccols), lambda j, i, c_ref: (j, c_ref[0] * nt + i, 0)),
                      pl.BlockSpec((None, tr, ccols), lambda j, i, c_ref: (j, i, 0))],
            out_specs=pl.BlockSpec((None, tr, ccols), lambda j, i, c_ref: (j, i, 0))),
        compiler_params=pltpu.CompilerParams(dimension_semantics=("arbitrary", "arbitrary"), vmem_limit_bytes=VMEM_LIMIT),
    )(cidx, g, r1)


def _chip_sum(p, r2, idx, name):
    nj, hr, ccols = p.shape
    tr = _row_tile(hr, ccols, 4, SUM_BLOCK_BYTES)
    nt = hr // tr

    def body(i_ref, p_ref, r_ref, o_ref):
        s = p_ref[...].astype(F32)
        for k in range(N_CHIPS - 1):
            s = s + r_ref[k].astype(F32)
        o_ref[...] = s

    return pl.pallas_call(
        body, name=name, out_shape=jax.ShapeDtypeStruct((2, hr, ccols), F32),
        grid_spec=pltpu.PrefetchScalarGridSpec(
            num_scalar_prefetch=1, grid=(nt,),
            in_specs=[pl.BlockSpec((None, tr, ccols), lambda i, i_ref: (i_ref[0], i, 0)),
                      pl.BlockSpec((N_CHIPS - 1, tr, ccols), lambda i, i_ref: (0, i, 0))],
            out_specs=pl.BlockSpec((None, tr, ccols), lambda i, i_ref: (i_ref[1], i, 0))),
        compiler_params=pltpu.CompilerParams(dimension_semantics=("arbitrary",), vmem_limit_bytes=VMEM_LIMIT),
    )(idx, p, r2)


def _adam_math(w, g, m, v):
    m2 = ADAM_B1 * m + (1.0 - ADAM_B1) * g
    v2 = ADAM_B2 * v + (1.0 - ADAM_B2) * (g * g)
    m_hat = m2 / (1.0 - ADAM_B1 ** ADAM_STEP)
    v_hat = v2 / (1.0 - ADAM_B2 ** ADAM_STEP)
    delta = -ADAM_LR * (m_hat / (jnp.sqrt(v_hat) + ADAM_EPS) + ADAM_WD * w)
    return delta, m2, v2


def _adamw_layers(w, m, v, gs, name):
    nl, r, ccols = w.shape
    ng = len(gs)
    tr = _row_tile(r, ccols, 4, ADAM_BLOCK_BYTES)
    nt = r // tr

    def body(w_ref, m_ref, v_ref, *rest):
        g_refs, (go_ref, d_ref, mo_ref, vo_ref) = rest[:ng], rest[ng:]
        l = pl.program_id(0)
        g = g_refs[0][...]
        for k in range(1, ng):
            g = jnp.where(l == k, g_refs[k][...], g)
        delta, m2, v2 = _adam_math(w_ref[...], g, m_ref[...], v_ref[...])
        go_ref[...] = g
        d_ref[...] = delta
        mo_ref[...] = m2
        vo_ref[...] = v2

    big = pl.BlockSpec((None, tr, ccols), lambda l, i: (l, i, 0))

    def gspec(k):
        return pl.BlockSpec((tr, ccols), lambda l, i: (jnp.where(l == k, i, jnp.where(l < k, 0, nt - 1)), 0))

    assert ng == nl
    return _grid_call(body, name, (nl, nt), in_specs=[big, big, big] + [gspec(k) for k in range(ng)],
                      out_specs=[big, big, big, big], out_shape=[jax.ShapeDtypeStruct(w.shape, F32)] * 4,
                      scratch_shapes=[], args=(w, m, v, *gs))[0]


def _adamw_flat(w, g, m, v, name):
    r, ccols = w.shape

    def body(w_ref, g_ref, m_ref, v_ref, d_ref, mo_ref, vo_ref):
        delta, m2, v2 = _adam_math(w_ref[...], g_ref[...], m_ref[...], v_ref[...])
        d_ref[...] = delta
        mo_ref[...] = m2
        vo_ref[...] = v2

    return pl.pallas_call(
        body, name=name, out_shape=[jax.ShapeDtypeStruct((r, ccols), F32)] * 3,
        in_specs=[_VMEM] * 4, out_specs=[_VMEM] * 3,
        compiler_params=pltpu.CompilerParams(vmem_limit_bytes=VMEM_LIMIT),
    )(w, g, m, v)


def _ada_forward(c_all, ada_w, ada_b_cols, name, exchange=None):
    nl, d, ncols = ada_w.shape
    bg = c_all.shape[0]
    tn = 512 if ncols % 512 == 0 else ncols

    def body(c_ref, w_ref, b_ref, o_ref):
        cv = c_ref[...]
        ca = (cv * _sigmoid(cv)).astype(BF16)
        o_ref[...] = jnp.dot(ca, w_ref[...].astype(BF16), preferred_element_type=F32) + b_ref[...]

    outs, got = _grid_call(
        body, name, (nl, ncols // tn),
        in_specs=[pl.BlockSpec((bg, d), lambda l, j: (0, 0)),
                  pl.BlockSpec((None, d, tn), lambda l, j: (l, 0, j)),
                  pl.BlockSpec((None, 1, tn), lambda l, j: (l, 0, j))],
        out_specs=[pl.BlockSpec((None, bg, tn), lambda l, j: (l, 0, j))],
        out_shape=[jax.ShapeDtypeStruct((nl, bg, ncols), F32)], scratch_shapes=[], args=(c_all, ada_w, ada_b_cols),
        exchange=exchange)
    return outs[0], got


def _ada_update(c_all, dmod_cols, w, m, v, name, exchange=None):
    nl, d, ncols = w.shape
    bg = c_all.shape[0]
    tn = 512 if ncols % 512 == 0 else ncols

    def body(c_ref, dm_ref, w_ref, m_ref, v_ref, go_ref, d_ref, mo_ref, vo_ref):
        cv = c_ref[...]
        ca = (cv * _sigmoid(cv)).astype(BF16)
        g = lax.dot_general(ca, dm_ref[...].astype(BF16), (((0,), (0,)), ((), ())), preferred_element_type=F32)
        delta, m2, v2 = _adam_math(w_ref[...], g, m_ref[...], v_ref[...])
        go_ref[...] = g
        d_ref[...] = delta
        mo_ref[...] = m2
        vo_ref[...] = v2

    big = pl.BlockSpec((None, d, tn), lambda l, j: (l, 0, j))
    return _grid_call(
        body, name, (nl, ncols // tn),
        in_specs=[pl.BlockSpec((bg, d), lambda l, j: (0, 0)),
                  pl.BlockSpec((None, bg, tn), lambda l, j: (l, 0, j)), big, big, big],
        out_specs=[big, big, big, big], out_shape=[jax.ShapeDtypeStruct(w.shape, F32)] * 4,
        scratch_shapes=[], args=(c_all, dmod_cols, w, m, v), exchange=exchange)


def _load_weights(first, pairs, sems):
    @pl.when(first)
    def _():
        cps = [pltpu.make_async_copy(src, dst, sems.at[k]) for k, (src, dst) in enumerate(pairs)]
        for cp in cps:
            cp.start()
        for cp in cps:
            cp.wait()


def _ada_norm(xv, g, sc, sh):
    r = lax.rsqrt(jnp.mean(xv * xv, axis=-1, keepdims=True) + EPS)
    xn = xv * r
    return (xn * g) * (1.0 + sc) + sh, xn, r


def _ada_norm_bwd(dh, xn, r, g, sc):
    d_sh = _colsum(dh)
    d_sc = _colsum(dh * (xn * g))
    dxg = dh * (1.0 + sc)
    d_g = _colsum(dxg * xn)
    gd = dxg * g
    dx = r * (gd - xn * jnp.mean(gd * xn, axis=-1, keepdims=True))
    return dx, d_sh, d_sc, d_g


def _gated_residual_bwd(dxo, o, g_post, gt):
    r = lax.rsqrt(jnp.mean(o * o, axis=-1, keepdims=True) + EPS)
    on = o * r
    d_gt = _colsum(dxo * (on * g_post))
    dy = dxo * (1.0 + gt)
    d_gp = _colsum(dy * on)
    gd = dy * g_post
    do = r * (gd - on * jnp.mean(gd * on, axis=-1, keepdims=True))
    return do, d_gt, d_gp


def _seq_positions(i, tm, width):
    return i * tm + lax.broadcasted_iota(jnp.int32, (tm, width), 0)


def _fill_phases(ext, phases):
    rows = ext.shape[0]
    ev = ext[...]
    for r in range(1, SUBLANES):
        phases[r - 1] = pltpu.roll(ev, rows - r, axis=0)


def _shifted_rows(ext, phases, offset, n):
    q, r = divmod(offset, SUBLANES)
    if r == 0:
        return ext[pl.ds(q * SUBLANES, n), :]
    return phases[r - 1, pl.ds(q * SUBLANES, n), :]


def _rows_before(halo, cur, shift):
    e = jnp.concatenate([halo, cur], axis=0)
    return pltpu.roll(e, shift, axis=0)[halo.shape[0]:, :]


def _rows_after(cur, halo, shift):
    e = jnp.concatenate([cur, halo], axis=0)
    return pltpu.roll(e, e.shape[0] - shift, axis=0)[:cur.shape[0], :]


def _mixer_forward(x, mod, vec_d, vec_c, cw, pw, win_g, wout_g, taps, tm, name, exchange=None):
    nb, s, d = x.shape
    n = s // tm
    nj, _, dcol = win_g.shape
    din = nj * dcol
    dc = vec_c.shape[-1]
    dpool = din - 2 * dc
    dmix = dc + dpool
    ro = wout_g.shape[1]
    ngrp = dpool // LANES

    def body(x_ref, mod_ref, vd_ref, vc_ref, cw_ref, pw_ref, win_hbm, wout_hbm,
             xo_ref, h_ref, u_ref, ac_ref, dp_ref, z_ref, o_ref,
             win_v, wout_v, ext_a, ext_p, phases, sems):
        b, i = pl.program_id(0), pl.program_id(1)
        pairs = [(win_hbm.at[j], win_v.at[:, pl.ds(j * dcol, dcol)]) for j in range(nj)]
        pairs += [(wout_hbm.at[j], wout_v.at[pl.ds(j * ro, ro), :]) for j in range(nj)]
        _load_weights((b == 0) & (i == 0), pairs, sems)

        xv = x_ref[...]
        h, _, _ = _ada_norm(xv, vd_ref[0:1, :], mod_ref[1:2, :], mod_ref[0:1, :])
        hb = h.astype(BF16)
        h_ref[...] = hb
        u = jnp.dot(hb, win_v[...], preferred_element_type=F32)
        u_ref[...] = u.astype(BF16)
        ag = u[:, :dc] * _sigmoid(u[:, dc:2 * dc])
        up = u[:, 2 * dc:]

        @pl.when(i == 0)
        def _():
            ext_a[0:HALO, :] = jnp.zeros((HALO, dc), F32)
            ext_p[0:HALO, :] = jnp.zeros((HALO, dpool), F32)

        @pl.when(i > 0)
        def _():
            ext_a[0:HALO, :] = ext_a[tm:tm + HALO, :]
            ext_p[0:HALO, :] = ext_p[tm:tm + HALO, :]

        ext_a[HALO:HALO + tm, :] = ag
        ext_p[HALO:HALO + tm, :] = up

        acc = jnp.broadcast_to(vc_ref[0:1, :], (tm, dc))
        _fill_phases(ext_a, phases)
        for k in range(taps):
            acc = acc + cw_ref[k:k + 1, :] * _shifted_rows(ext_a, phases, HALO - (taps - 1) + k, tm)
        ac_ref[...] = acc.astype(BF16)
        mu = jnp.mean(acc, axis=-1, keepdims=True)
        xc = acc - mu
        var = jnp.mean(xc * xc, axis=-1, keepdims=True)
        al = (xc * lax.rsqrt(var + EPS)) * vc_ref[1:2, :] + vc_ref[2:3, :]
        a = al * _sigmoid(al)

        pos = _seq_positions(i, tm, LANES)
        parts = [a.astype(BF16)]
        for g in range(ngrp):
            w = POOL_WINDOWS[g]
            cols = slice(g * LANES, (g + 1) * LANES)
            sw = ext_p[:, cols]
            step = 1
            while step < w:
                sw = sw + pltpu.roll(sw, step, axis=0)
                step *= 2
            cnt = jnp.minimum(pos + 1, w).astype(F32)
            dg = (sw[HALO:, :] / cnt - up[:, cols]).astype(BF16)
            dp_ref[:, cols] = dg
            q = jnp.dot(dg, pw_ref[g], preferred_element_type=F32)
            parts.append((q * vc_ref[3:4, cols]).astype(BF16))
        z = jnp.concatenate(parts, axis=-1)
        z_ref[...] = z
        o = jnp.dot(z, wout_v[...], preferred_element_type=F32)
        o_ref[...] = o
        r2 = lax.rsqrt(jnp.mean(o * o, axis=-1, keepdims=True) + EPS)
        xo_ref[...] = xv + (1.0 + mod_ref[2:3, :]) * ((o * r2) * vd_ref[1:2, :])

    def tile(width):
        return pl.BlockSpec((None, tm, width), lambda b, i: (b, i, 0))

    return _grid_call(
        body, name, (nb, n),
        in_specs=[tile(d), pl.BlockSpec((None, 8, d), lambda b, i: (b, 0, 0)), _full(vec_d.shape), _full(vec_c.shape),
                  _full(cw.shape), _full(pw.shape), _ANY, _ANY],
        out_specs=[tile(d), tile(d), tile(din), tile(dc), tile(dpool), tile(dmix), tile(d)],
        out_shape=[jax.ShapeDtypeStruct((nb, s, d), F32), jax.ShapeDtypeStruct((nb, s, d), BF16),
                   jax.ShapeDtypeStruct((nb, s, din), BF16), jax.ShapeDtypeStruct((nb, s, dc), BF16),
                   jax.ShapeDtypeStruct((nb, s, dpool), BF16), jax.ShapeDtypeStruct((nb, s, dmix), BF16),
                   jax.ShapeDtypeStruct((nb, s, d), F32)],
        scratch_shapes=[pltpu.VMEM((d, din), BF16), pltpu.VMEM((dmix, d), BF16),
                        pltpu.VMEM((HALO + tm, dc), F32), pltpu.VMEM((HALO + tm, dpool), F32),
                        pltpu.VMEM((SUBLANES - 1, HALO + tm, dc), F32), pltpu.SemaphoreType.DMA((2 * nj,))],
        args=(x, mod, vec_d, vec_c, cw, pw, win_g, wout_g), exchange=exchange)


def _mixer_backward(dxo, x, o, u, ac, dpl, mod, vec_d, vec_c, cw, pw, win_g, wout_g, taps, tm, name, exchange=None):
    nb, s, d = x.shape
    n = s // tm
    nj, _, dcol = win_g.shape
    din = nj * dcol
    dc = vec_c.shape[-1]
    dpool = din - 2 * dc
    dmix = dc + dpool
    ro = wout_g.shape[1]
    ngrp = dpool // LANES
    rext = tm + HALO

    def body(dxo_ref, x_ref, o_ref, u_ref, ac_ref, dp_ref, mod_ref, vd_ref, vc_ref, cw_ref, pw_ref, win_hbm, wout_hbm,
             dx_ref, du_ref, dob_ref, rowd_ref, rowb_ref, rowc_ref, dcw_ref, dpw_ref,
             win_v, wout_v, ext_a, ext_p, phases, sems):
        b, i = pl.program_id(0), pl.program_id(1)
        first = (b == 0) & (i == 0)
        pairs = [(win_hbm.at[j], win_v.at[:, pl.ds(j * dcol, dcol)]) for j in range(nj)]
        pairs += [(wout_hbm.at[j], wout_v.at[pl.ds(j * ro, ro), :]) for j in range(nj)]
        _load_weights(first, pairs, sems)

        @pl.when(first)
        def _():
            rowd_ref[...] = jnp.zeros_like(rowd_ref)
            rowc_ref[...] = jnp.zeros_like(rowc_ref)
            dcw_ref[...] = jnp.zeros_like(dcw_ref)
            dpw_ref[...] = jnp.zeros_like(dpw_ref)

        @pl.when(i == 0)
        def _():
            rowb_ref[...] = jnp.zeros_like(rowb_ref)
            ext_a[tm:rext, :] = jnp.zeros((HALO, dc), F32)
            ext_p[tm:rext, :] = jnp.zeros((HALO, dpool), F32)

        @pl.when(i > 0)
        def _():
            ext_a[tm:rext, :] = ext_a[0:HALO, :]
            ext_p[tm:rext, :] = ext_p[0:HALO, :]

        g_pre, g_post = vd_ref[0:1, :], vd_ref[1:2, :]
        sh, sc, gt = mod_ref[0:1, :], mod_ref[1:2, :], mod_ref[2:3, :]
        do, d_gt, d_gp = _gated_residual_bwd(dxo_ref[...], o_ref[...], g_post, gt)
        dob = do.astype(BF16)
        dob_ref[...] = dob
        dz = lax.dot_general(dob, wout_v[...], (((1,), (1,)), ((), ())), preferred_element_type=F32)

        acv = ac_ref[...].astype(F32)
        mu = jnp.mean(acv, axis=-1, keepdims=True)
        xc = acv - mu
        rstd = lax.rsqrt(jnp.mean(xc * xc, axis=-1, keepdims=True) + EPS)
        an = xc * rstd
        lg = vc_ref[1:2, :]
        al = an * lg + vc_ref[2:3, :]
        sg = _sigmoid(al)
        dal = dz[:, :dc] * (sg * (1.0 + al * (1.0 - sg)))
        d_lg = _colsum(dal * an)
        d_lb = _colsum(dal)
        dan = dal * lg
        dac = rstd * (dan - jnp.mean(dan, axis=-1, keepdims=True) - an * jnp.mean(dan * an, axis=-1, keepdims=True))
        d_cb = _colsum(dac)
        ext_a[0:tm, :] = dac
        uv = u_ref[:, 0:dc].astype(F32)
        sgu = _sigmoid(u_ref[:, dc:2 * dc].astype(F32))
        ag = uv * sgu
        dag = jnp.zeros((tm, dc), F32)
        _fill_phases(ext_a, phases)
        for k in range(taps):
            sl = _shifted_rows(ext_a, phases, taps - 1 - k, tm)
            dag = dag + cw_ref[k:k + 1, :] * sl
            dcw_ref[k:k + 1, :] += _colsum(ag * sl)
        du_ref[:, 0:dc] = (dag * sgu).astype(BF16)
        du_ref[:, dc:2 * dc] = (dag * uv * (sgu * (1.0 - sgu))).astype(BF16)

        pos = _seq_positions(n - 1 - i, tm, LANES)
        d_ps = []
        for g in range(ngrp):
            w = POOL_WINDOWS[g]
            cols = slice(g * LANES, (g + 1) * LANES)
            gcols = slice(dc + g * LANES, dc + (g + 1) * LANES)
            dgb = dp_ref[:, cols]
            q = jnp.dot(dgb, pw_ref[g], preferred_element_type=F32)
            dpg = dz[:, gcols]
            d_ps.append(_colsum(dpg * q))
            dq = (dpg * vc_ref[3:4, cols]).astype(BF16)
            dpw_ref[g] += lax.dot_general(dgb, dq, (((0,), (0,)), ((), ())), preferred_element_type=F32)
            dd = lax.dot_general(dq, pw_ref[g], (((1,), (1,)), ((), ())), preferred_element_type=F32)
            cnt = jnp.minimum(pos + 1, w).astype(F32)
            ext_p[0:tm, cols] = dd / cnt
            sw = ext_p[:, cols]
            step = 1
            while step < w:
                sw = sw + pltpu.roll(sw, rext - step, axis=0)
                step *= 2
            du_ref[:, 2 * dc + g * LANES:2 * dc + (g + 1) * LANES] = (sw[0:tm, :] - dd).astype(BF16)
        rowc_ref[0:1, :] += d_cb
        rowc_ref[1:2, :] += d_lg
        rowc_ref[2:3, :] += d_lb
        rowc_ref[3:4, :] += jnp.concatenate(d_ps, axis=-1)

        dh = lax.dot_general(du_ref[...], win_v[...], (((1,), (1,)), ((), ())), preferred_element_type=F32)
        _, xn, r1 = _ada_norm(x_ref[...], g_pre, sc, sh)
        dxb, d_sh, d_sc, d_g = _ada_norm_bwd(dh, xn, r1, g_pre, sc)
        dx_ref[...] = dxo_ref[...] + dxb
        rowd_ref[0:1, :] += d_g
        rowd_ref[1:2, :] += d_gp
        rowb_ref[0:1, :] += d_sh
        rowb_ref[1:2, :] += d_sc
        rowb_ref[2:3, :] += d_gt

    def tile(width):
        return pl.BlockSpec((None, tm, width), lambda b, i: (b, n - 1 - i, 0))

    return _grid_call(
        body, name, (nb, n),
        in_specs=[tile(d), tile(d), tile(d), tile(din), tile(dc), tile(dpool),
                  pl.BlockSpec((None, 8, d), lambda b, i: (b, 0, 0)), _full(vec_d.shape), _full(vec_c.shape),
                  _full(cw.shape), _full(pw.shape), _ANY, _ANY],
        out_specs=[tile(d), tile(din), tile(d), _full((8, d)), pl.BlockSpec((None, 8, d), lambda b, i: (b, 0, 0)),
                   _full((8, dc)), _full((HALO, dc)), _full(pw.shape)],
        out_shape=[jax.ShapeDtypeStruct((nb, s, d), F32), jax.ShapeDtypeStruct((nb, s, din), BF16),
                   jax.ShapeDtypeStruct((nb, s, d), BF16), jax.ShapeDtypeStruct((8, d), F32),
                   jax.ShapeDtypeStruct((nb, 8, d), F32), jax.ShapeDtypeStruct((8, dc), F32),
                   jax.ShapeDtypeStruct((HALO, dc), F32), jax.ShapeDtypeStruct(pw.shape, F32)],
        scratch_shapes=[pltpu.VMEM((d, din), BF16), pltpu.VMEM((dmix, d), BF16),
                        pltpu.VMEM((rext, dc), F32), pltpu.VMEM((rext, dpool), F32),
                        pltpu.VMEM((SUBLANES - 1, rext, dc), F32), pltpu.SemaphoreType.DMA((2 * nj,))],
        args=(dxo, x, o, u, ac, dpl, mod, vec_d, vec_c, cw, pw, win_g, wout_g), exchange=exchange)


def _ffn_forward(x, mod, vec_d, fw, wup_g, wdn_g, tm, name, exchange=None, target=None):
    nb, s, d = x.shape
    n = s // tm
    nj, _, ucol = wup_g.shape
    f2 = nj * ucol
    dff = f2 // 2
    rd = wdn_g.shape[1]
    nq = nj // 2
    cs = dff // nq

    with_loss = target is not None

    def body(*refs):
        refs = list(refs)
        x_ref, mod_ref, vd_ref, fw_ref = refs[:4]
        t_ref = refs.pop(4) if with_loss else None
        wup_hbm, wdn_hbm, xo_ref, h_ref, u_ref, uc_ref, hid_ref, o_ref = refs[4:12]
        sq_ref = refs.pop(12) if with_loss else None
        wup_v, wdn_v, prev_u, sems = refs[12:]
        b, i = pl.program_id(0), pl.program_id(1)
        pairs = [(wup_hbm.at[j], wup_v.at[:, pl.ds(j * ucol, ucol)]) for j in range(nj)]
        pairs += [(wdn_hbm.at[j], wdn_v.at[pl.ds(j * rd, rd), :]) for j in range(nj)]
        _load_weights((b == 0) & (i == 0), pairs, sems)

        if with_loss:
            @pl.when((b == 0) & (i == 0))
            def _():
                sq_ref[...] = jnp.zeros_like(sq_ref)

        @pl.when(i == 0)
        def _():
            prev_u[...] = jnp.zeros_like(prev_u)

        xv = x_ref[...]
        h, _, _ = _ada_norm(xv, vd_ref[2:3, :], mod_ref[4:5, :], mod_ref[3:4, :])
        hb = h.astype(BF16)
        h_ref[...] = hb

        def conv(cols):
            uc = jnp.dot(hb, wup_v[:, cols], preferred_element_type=F32)
            u_ref[:, cols] = uc.astype(BF16)
            before = prev_u[:, cols]
            prev_u[:, cols] = uc[tm - FHALO:, :]
            out = (fw_ref[3:4, cols] + fw_ref[2:3, cols] * uc + fw_ref[1:2, cols] * _rows_before(before, uc, 1)
                   + fw_ref[0:1, cols] * _rows_before(before, uc, 2))
            uc_ref[:, cols] = out.astype(BF16)
            return out

        o = jnp.zeros((tm, d), F32)
        for q in range(nq):
            val = conv(pl.ds(q * cs, cs))
            gate = conv(pl.ds(dff + q * cs, cs))
            hid = ((gate * _sigmoid(gate)) * val).astype(BF16)
            hid_ref[:, pl.ds(q * cs, cs)] = hid
            o = o + jnp.dot(hid, wdn_v[pl.ds(q * cs, cs), :], preferred_element_type=F32)
        o_ref[...] = o
        r2 = lax.rsqrt(jnp.mean(o * o, axis=-1, keepdims=True) + EPS)
        y = xv + (1.0 + mod_ref[5:6, :]) * ((o * r2) * vd_ref[3:4, :])
        if with_loss:
            e = y - t_ref[...]
            xo_ref[...] = e * (1.0 / d)
            sq_ref[0:1, :] += _colsum(e * e)
        else:
            xo_ref[...] = y

    def tile(width):
        return pl.BlockSpec((None, tm, width), lambda b, i: (b, i, 0))

    loss_in = [tile(d)] if with_loss else []
    return _grid_call(
        body, name, (nb, n),
        in_specs=[tile(d), pl.BlockSpec((None, 8, d), lambda b, i: (b, 0, 0)), _full(vec_d.shape), _full(fw.shape)]
        + loss_in + [_ANY, _ANY],
        out_specs=[tile(d), tile(d), tile(f2), tile(f2), tile(dff), tile(d)] + ([_full((8, d))] if with_loss else []),
        out_shape=[jax.ShapeDtypeStruct((nb, s, d), F32), jax.ShapeDtypeStruct((nb, s, d), BF16),
                   jax.ShapeDtypeStruct((nb, s, f2), BF16), jax.ShapeDtypeStruct((nb, s, f2), BF16),
                   jax.ShapeDtypeStruct((nb, s, dff), BF16), jax.ShapeDtypeStruct((nb, s, d), F32)]
        + ([jax.ShapeDtypeStruct((8, d), F32)] if with_loss else []),
        scratch_shapes=[pltpu.VMEM((d, f2), BF16), pltpu.VMEM((dff, d), BF16),
                        pltpu.VMEM((FHALO, f2), F32), pltpu.SemaphoreType.DMA((2 * nj,))],
        args=(x, mod, vec_d, fw) + ((target,) if with_loss else ()) + (wup_g, wdn_g), exchange=exchange)


def _ffn_backward(dxo, x, o, u, uc, mod, vec_d, fw, wup_g, wdn_g, tm, name, exchange=None):
    nb, s, d = x.shape
    n = s // tm
    nj, _, ucol = wup_g.shape
    f2 = nj * ucol
    dff = f2 // 2
    rd = wdn_g.shape[1]
    nq = nj // 2
    cs = dff // nq

    def body(dxo_ref, x_ref, o_ref, u_ref, uc_ref, mod_ref, vd_ref, fw_ref, wup_hbm, wdn_hbm,
             dx_ref, du_ref, dob_ref, rowd_ref, rowb_ref, dfw_ref,
             wup_v, wdn_v, next_d, sems):
        b, i = pl.program_id(0), pl.program_id(1)
        first = (b == 0) & (i == 0)
        pairs = [(wup_hbm.at[j], wup_v.at[:, pl.ds(j * ucol, ucol)]) for j in range(nj)]
        pairs += [(wdn_hbm.at[j], wdn_v.at[pl.ds(j * rd, rd), :]) for j in range(nj)]
        _load_weights(first, pairs, sems)

        @pl.when(first)
        def _():
            rowd_ref[...] = jnp.zeros_like(rowd_ref)
            dfw_ref[...] = jnp.zeros_like(dfw_ref)

        @pl.when(i == 0)
        def _():
            rowb_ref[...] = jnp.zeros_like(rowb_ref)
            next_d[...] = jnp.zeros_like(next_d)

        g_pre, g_post = vd_ref[2:3, :], vd_ref[3:4, :]
        sh, sc, gt = mod_ref[3:4, :], mod_ref[4:5, :], mod_ref[5:6, :]
        do, d_gt, d_gp = _gated_residual_bwd(dxo_ref[...], o_ref[...], g_post, gt)
        dob = do.astype(BF16)
        dob_ref[...] = dob

        def conv_bwd(cols, duc):
            uc = u_ref[:, cols].astype(F32)
            after = next_d[:, cols]
            next_d[:, cols] = duc[0:FHALO, :]
            d1 = _rows_after(duc, after, 1)
            d2 = _rows_after(duc, after, 2)
            dfw_ref[3:4, cols] += _colsum(duc)
            dfw_ref[2:3, cols] += _colsum(uc * duc)
            dfw_ref[1:2, cols] += _colsum(uc * d1)
            dfw_ref[0:1, cols] += _colsum(uc * d2)
            ob = (fw_ref[2:3, cols] * duc + fw_ref[1:2, cols] * d1 + fw_ref[0:1, cols] * d2).astype(BF16)
            du_ref[:, cols] = ob
            return lax.dot_general(ob, wup_v[:, cols], (((1,), (1,)), ((), ())), preferred_element_type=F32)

        dh = jnp.zeros((tm, d), F32)
        for q in range(nq):
            vcols = pl.ds(q * cs, cs)
            gcols = pl.ds(dff + q * cs, cs)
            dhid = lax.dot_general(dob, wdn_v[vcols, :], (((1,), (1,)), ((), ())), preferred_element_type=F32)
            val = uc_ref[:, vcols].astype(F32)
            gate = uc_ref[:, gcols].astype(F32)
            sg = _sigmoid(gate)
            act = gate * sg
            dval = dhid * act
            dgate = (dhid * val) * (sg + act * (1.0 - sg))
            dh = dh + conv_bwd(vcols, dval)
            dh = dh + conv_bwd(gcols, dgate)

        _, xn, r1 = _ada_norm(x_ref[...], g_pre, sc, sh)
        dxb, d_sh, d_sc, d_g = _ada_norm_bwd(dh, xn, r1, g_pre, sc)
        dx_ref[...] = dxo_ref[...] + dxb
        rowd_ref[2:3, :] += d_g
        rowd_ref[3:4, :] += d_gp
        rowb_ref[3:4, :] += d_sh
        rowb_ref[4:5, :] += d_sc
        rowb_ref[5:6, :] += d_gt

    def tile(width):
        return pl.BlockSpec((None, tm, width), lambda b, i: (b, n - 1 - i, 0))

    return _grid_call(
        body, name, (nb, n),
        in_specs=[tile(d), tile(d), tile(d), tile(f2), tile(f2), pl.BlockSpec((None, 8, d), lambda b, i: (b, 0, 0)),
                  _full(vec_d.shape), _full(fw.shape), _ANY, _ANY],
        out_specs=[tile(d), tile(f2), tile(d), _full((8, d)), pl.BlockSpec((None, 8, d), lambda b, i: (b, 0, 0)),
                   _full(fw.shape)],
        out_shape=[jax.ShapeDtypeStruct((nb, s, d), F32), jax.ShapeDtypeStruct((nb, s, f2), BF16),
                   jax.ShapeDtypeStruct((nb, s, d), BF16), jax.ShapeDtypeStruct((8, d), F32),
                   jax.ShapeDtypeStruct((nb, 8, d), F32), jax.ShapeDtypeStruct(fw.shape, F32)],
        scratch_shapes=[pltpu.VMEM((d, f2), BF16), pltpu.VMEM((dff, d), BF16),
                        pltpu.VMEM((FHALO, f2), F32), pltpu.SemaphoreType.DMA((2 * nj,))],
        args=(dxo, x, o, u, uc, mod, vec_d, fw, wup_g, wdn_g), exchange=exchange)


def _weight_grad(a, b, nblk, split, tt, name, exchange=None):
    t, ka = a.shape
    nb_ = b.shape[1]
    nk = t // tt
    if split == "cols":
        wa, wb, grid = ka, nb_ // nblk, (1, nk)
        a_spec = pl.BlockSpec((tt, ka), lambda j, k: (k, 0))
        b_spec = pl.BlockSpec((tt, nb_), lambda j, k: (k, 0))
        o_spec = pl.BlockSpec((nblk, wa, wb), lambda j, k: (0, 0, 0))
        acc_shape = (ka, nb_)
    elif split == "b":
        wa, wb, grid = ka, nb_ // nblk, (nblk, nk)
        a_spec = pl.BlockSpec((tt, wa), lambda j, k: (k, 0))
        b_spec = pl.BlockSpec((tt, wb), lambda j, k: (k, j))
        o_spec = pl.BlockSpec((None, wa, wb), lambda j, k: (j, 0, 0))
        acc_shape = (wa, wb)
    else:
        wa, wb, grid = ka // nblk, nb_, (nblk, nk)
        a_spec = pl.BlockSpec((tt, wa), lambda j, k: (k, j))
        b_spec = pl.BlockSpec((tt, wb), lambda j, k: (k, 0))
        o_spec = pl.BlockSpec((None, wa, wb), lambda j, k: (j, 0, 0))
        acc_shape = (wa, wb)

    def body(a_ref, b_ref, o_ref, acc):
        k = pl.program_id(1)
        prod = lax.dot_general(a_ref[...], b_ref[...], (((0,), (0,)), ((), ())), preferred_element_type=F32)

        @pl.when(k == 0)
        def _():
            acc[...] = prod

        @pl.when(k > 0)
        def _():
            acc[...] += prod

        @pl.when(k == nk - 1)
        def _():
            if split == "cols":
                for j in range(nblk):
                    o_ref[j] = acc[:, j * wb:(j + 1) * wb].astype(o_ref.dtype)
            else:
                o_ref[...] = acc[...].astype(o_ref.dtype)

    outs, exo = _grid_call(body, name, grid, in_specs=[a_spec, b_spec], out_specs=[o_spec],
                           out_shape=[jax.ShapeDtypeStruct((nblk, wa, wb), BF16)],
                           scratch_shapes=[pltpu.VMEM(acc_shape, F32)], args=(a, b), exchange=exchange)
    return outs[0], exo


def _rows128(a):
    return a.reshape(-1, LANES)


class _ReduceScatter:
    def __init__(self, gs, cidx, idx, tag):
        self.gs, self.cidx, self.idx, self.tag = gs, cidx, idx, tag

    def swap(self):
        return _swap_halves(self.gs)

    def after_swap(self, r1):
        self.ps = [_pair_sum(g, r, self.cidx, name=f"rs_pair_{self.tag}_{a}") for a, (g, r) in enumerate(zip(self.gs, r1))]

    def chips(self):
        return _chip_exchange(self.ps)

    def after_chips(self, r2):
        self.fh = [_chip_sum(p, r, self.idx, name=f"rs_sum_{self.tag}_{a}") for a, (p, r) in enumerate(zip(self.ps, r2))]

    def share(self):
        return _sibling_share(self.fh)

    @staticmethod
    def result(fs):
        return [f.reshape(f.shape[0] * f.shape[1], f.shape[2]) for f in fs]


def kernel(x, c, ada_w, ada_b, pre_mix_g, post_mix_g, w_in, conv_w, conv_b, conv_ln_g, conv_ln_b, pool_w, pool_scale, w_out, pre_ffn_g, post_ffn_g, ffn_up, ffn_conv_w, ffn_conv_b, ffn_down, loss_target, m_ada_w, m_ada_b, m_pre_mix_g, m_post_mix_g, m_w_in, m_conv_w, m_conv_b, m_conv_ln_g, m_conv_ln_b, m_pool_w, m_pool_scale, m_w_out, m_pre_ffn_g, m_post_ffn_g, m_ffn_up, m_ffn_conv_w, m_ffn_conv_b, m_ffn_down, v_ada_w, v_ada_b, v_pre_mix_g, v_post_mix_g, v_w_in, v_conv_w, v_conv_b, v_conv_ln_g, v_conv_ln_b, v_pool_w, v_pool_scale, v_w_out, v_pre_ffn_g, v_post_ffn_g, v_ffn_up, v_ffn_conv_w, v_ffn_conv_b, v_ffn_down):
    nb, s, d = x.shape
    nl = w_in.shape[0]
    taps = conv_w.shape[1]
    ccol = conv_w.shape[2]
    dc = conv_b.shape[1]
    fcol = ffn_conv_w.shape[2]
    f2 = ffn_conv_b.shape[1]
    nmod = ada_b.shape[1] // d
    acol = ada_w.shape[2]
    tm = min(MLP_TILE_ROWS, s)
    tm_mix = min(MIXER_TILE_ROWS, s)
    tt = min(GRAD_CHUNK_ROWS, (nb * s) // 2)

    xi, yi, ci = _pos()
    jm = 2 * xi + yi
    cidx = jnp.reshape(ci, (1,)).astype(jnp.int32)
    idx = jnp.stack([jm, ci]).astype(jnp.int32)

    win_b, wout_b, wup_b, wdn_b = (w.astype(BF16) for w in (w_in, w_out, ffn_up, ffn_down))

    def others(l):
        return [win_b[l], wout_b[l], wdn_b[l]]

    n_cw, n_fw, n_c = nl * taps * ccol, nl * 3 * fcol, nb * d
    packed = jnp.concatenate([conv_w.reshape(-1), ffn_conv_w.reshape(-1), c.reshape(-1)])
    got = _gather8(_rows128(packed), name="gather_small").reshape(N_DEV, -1)
    chips = got[0::2]
    cw_full = chips[:, :n_cw].reshape(N_CHIPS, nl, taps, ccol).transpose(1, 2, 0, 3).reshape(nl, taps, dc)
    fw_full = chips[:, n_cw:n_cw + n_fw].reshape(N_CHIPS, nl, 3, fcol).transpose(1, 2, 0, 3).reshape(nl, 3, f2)
    c_all = got[:, n_cw + n_fw:].reshape(N_DEV * nb, d)

    ada_b_cols = lax.dynamic_slice_in_dim(ada_b, jm * acol, acol, axis=1).reshape(nl, 1, acol)
    mod_cols, first_weights = _ada_forward(c_all, ada_w, ada_b_cols, name="ada_forward",
                                           exchange=_gather(others(0), mid_at=0.9))
    by_owner = mod_cols.reshape(nl, N_DEV, nb, acol).transpose(1, 0, 2, 3).reshape(N_DEV, -1, LANES)
    mod_own = _rows_to_owners(by_owner, name="mod_to_owners").reshape(N_CHIPS, nl, nb, acol)
    mod_own = mod_own.transpose(1, 2, 0, 3).reshape(nl, nb, nmod, d)
    mod_own = jnp.pad(mod_own, ((0, 0), (0, 0), (0, 8 - nmod), (0, 0)))

    vec_d = jnp.stack([pre_mix_g, post_mix_g, pre_ffn_g, post_ffn_g], axis=1)
    vec_c = jnp.stack([conv_b, conv_ln_g, conv_ln_b, pool_scale], axis=1)
    cw_pad = jnp.pad(cw_full, ((0, 0), (0, HALO - taps), (0, 0)))
    fw_rows = jnp.concatenate([fw_full, ffn_conv_b[:, None, :], jnp.zeros((nl, 4, f2), F32)], axis=1)
    pw_b = pool_w.astype(BF16)

    win_g, wout_g, wdn_g = _whole(first_weights)
    saved = []
    xs = x
    for l in range(nl):
        (x1, h1, u1, ac1, dp1, z1, o1), got = _mixer_forward(
            xs, mod_own[l], vec_d[l], vec_c[l], cw_pad[l], pw_b[l], win_g, wout_g, taps, tm_mix, name=f"mixer_fwd_{l}",
            exchange=_gather([wup_b[l]], mid_at=0.9))
        wup_g, = _whole(got)
        last = l + 1 == nl
        (x2, h2, u2, uc2, hid2, o2, *sq), nxt = _ffn_forward(
            x1, mod_own[l], vec_d[l], fw_rows[l], wup_g, wdn_g, tm, name=f"ffn_fwd_{l}",
            exchange=None if last else _gather(others(l + 1), mid_at=0.6), target=loss_target if last else None)
        saved.append((xs, h1, u1, ac1, dp1, z1, o1, x1, h2, u2, uc2, hid2, o2, win_g, wout_g, wup_g, wdn_g))
        if not last:
            win_g, wout_g, wdn_g = _whole(nxt)
        xs = x2

    dx = xs
    loss = lax.psum(0.5 * jnp.sum(sq[0]) / d, ("x", "y", "c"))

    flat = lambda a: a.reshape(nb * s, a.shape[-1])
    small = [None] * nl
    big_mlp, big_mix = [None] * nl, [None] * nl
    mlp = mix = None
    for l in reversed(range(nl)):
        x0, h1, u1, ac1, dp1, z1, o1, x1, h2, u2, uc2, hid2, o2, win_g, wout_g, wup_g, wdn_g = saved[l]
        (dx, du2, do2, rowd2, rowb2, dfw), got = _ffn_backward(
            dx, x1, o2, u2, uc2, mod_own[l], vec_d[l], fw_rows[l], wup_g, wdn_g, tm, name=f"ffn_bwd_{l}",
            exchange=_combine([mlp.chips(), mix.swap()]) if mlp else None)
        if mlp:
            mlp.after_chips(got[:2])
            mix.after_swap(got[2:])
        g_up, got = _weight_grad(flat(h2), flat(du2), N_CHIPS, "b", tt, name=f"grad_ffn_up_{l}",
                                 exchange=_combine([mlp.share(), mix.chips()]) if mlp else None)
        if mlp:
            big_mlp[l + 1] = mlp.result(got[:2])
            mix.after_chips(got[2:])
        g_dn, got = _weight_grad(flat(hid2), flat(do2), 2, "a", tt, name=f"grad_ffn_down_{l}",
                                 exchange=_swap_halves([g_up]) if l == 0 else None)
        g_dn = g_dn.reshape(N_CHIPS, -1, d)
        mlp_above, mlp = mlp, _ReduceScatter([g_up, g_dn], cidx, idx, f"mlp_{l}")
        if l == 0:
            mlp.after_swap(list(got) + list(_run_exchange(_swap_halves([g_dn]), name="rs_swap_down_0")))
        first = mlp.swap() if l > 0 else mlp.chips()
        (dx, du1, do1, rowd1, rowb1, rowc, dcw, dpw), got = _mixer_backward(
            dx, x0, o1, u1, ac1, dp1, mod_own[l], vec_d[l], vec_c[l], cw_pad[l], pw_b[l], win_g, wout_g, taps, tm_mix,
            name=f"mixer_bwd_{l}", exchange=_combine([first, mix.share()]) if mlp_above else first)
        if l > 0:
            mlp.after_swap(got[:2])
        else:
            mlp.after_chips(got[:2])
        if mlp_above:
            big_mix[l + 1] = mix.result(got[2:])
        g_in, got = _weight_grad(flat(h1), flat(du1), N_CHIPS, "cols", tt, name=f"grad_w_in_{l}",
                                 exchange=mlp.share() if l == 0 else None)
        if l == 0:
            big_mlp[0] = mlp.result(got)
        g_out, _ = _weight_grad(flat(z1), flat(do1), 1, "cols", tt, name=f"grad_w_out_{l}")
        mix = _ReduceScatter([g_in, g_out.reshape(N_CHIPS, -1, d)], cidx, idx, f"mix_{l}")
        small[l] = dict(rowd=rowd1 + rowd2, rowb=rowb1 + rowb2, rowc=rowc, dcw=dcw[:taps], dpw=dpw, dfw=dfw)
    mix.after_swap(_run_exchange(mix.swap(), name="rs_swap_mix_0"))

    dmod_own = jnp.stack([small[l]["rowb"][:, :nmod, :] for l in range(nl)])
    dmod_all, got = _gather8(_rows128(dmod_own), name="gather_dmod", exchange=mix.chips())
    mix.after_chips(got)
    big_mix[0] = mix.result(_run_exchange(mix.share(), name="rs_share_mix_0"))
    dmod_all = dmod_all.reshape(N_DEV, nl, nb, nmod * d)
    dmod_all = dmod_all.transpose(1, 0, 2, 3).reshape(nl, N_DEV * nb, nmod * d)
    dmod_cols = lax.dynamic_slice_in_dim(dmod_all, jm * acol, acol, axis=2)
    (g_ada_w, d_ada_w, nm_ada_w, nv_ada_w), _ = _ada_update(c_all, dmod_cols, ada_w, m_ada_w, v_ada_w, name="ada_update")

    def st(key, row=None):
        return jnp.stack([small[l][key] if row is None else small[l][key][row] for l in range(nl)])

    local = {
        "ada_b": dmod_own.sum(axis=1).reshape(nl, nmod * d),
        "pre_mix_g": st("rowd", 0), "post_mix_g": st("rowd", 1),
        "conv_b": st("rowc", 0), "conv_ln_g": st("rowc", 1), "conv_ln_b": st("rowc", 2),
        "pool_w": st("dpw"), "pool_scale": st("rowc", 3),
        "pre_ffn_g": st("rowd", 2), "post_ffn_g": st("rowd", 3),
        "ffn_conv_b": st("dfw", 3), "conv_w": st("dcw"), "ffn_conv_w": jnp.stack([small[l]["dfw"][:3] for l in range(nl)]),
    }
    names = list(local)
    sizes = [local[k].size for k in names]
    pad = -sum(sizes) % (4 * SUBLANES * LANES)
    packed = jnp.concatenate([local[k].reshape(-1) for k in names] + [jnp.zeros((pad,), F32)])
    summed = _allreduce8(_rows128(packed), name="allreduce_small").reshape(-1)
    grads, off = {}, 0
    for k, sz in zip(names, sizes):
        grads[k] = summed[off:off + sz].reshape(local[k].shape)
        off += sz
    grads["conv_w"] = lax.dynamic_slice_in_dim(grads["conv_w"], jm * ccol, ccol, axis=2)
    grads["ffn_conv_w"] = lax.dynamic_slice_in_dim(grads["ffn_conv_w"], jm * fcol, fcol, axis=2)

    params = dict(ada_b=(ada_b, m_ada_b, v_ada_b), pre_mix_g=(pre_mix_g, m_pre_mix_g, v_pre_mix_g),
                  post_mix_g=(post_mix_g, m_post_mix_g, v_post_mix_g), conv_b=(conv_b, m_conv_b, v_conv_b),
                  conv_ln_g=(conv_ln_g, m_conv_ln_g, v_conv_ln_g), conv_ln_b=(conv_ln_b, m_conv_ln_b, v_conv_ln_b),
                  pool_w=(pool_w, m_pool_w, v_pool_w), pool_scale=(pool_scale, m_pool_scale, v_pool_scale),
                  pre_ffn_g=(pre_ffn_g, m_pre_ffn_g, v_pre_ffn_g), post_ffn_g=(post_ffn_g, m_post_ffn_g, v_post_ffn_g),
                  ffn_conv_b=(ffn_conv_b, m_ffn_conv_b, v_ffn_conv_b), conv_w=(conv_w, m_conv_w, v_conv_w),
                  ffn_conv_w=(ffn_conv_w, m_ffn_conv_w, v_ffn_conv_w))
    pack = lambda i, g=None: _rows128(jnp.concatenate([(grads[k] if g else params[k][i]).reshape(-1) for k in names]))
    sd, sm, sv = _adamw_flat(pack(0), pack(0, True), pack(1), pack(2), name="adamw_small")
    outs = {}
    off = 0
    for k in names:
        shape, sz = params[k][0].shape, params[k][0].size
        outs[k] = (grads[k],) + tuple(a.reshape(-1)[off:off + sz].reshape(shape) for a in (sd, sm, sv))
        off += sz

    outs["ada_w"] = (g_ada_w, d_ada_w, nm_ada_w, nv_ada_w)
    for k, w, m, v, gs in [("w_in", w_in, m_w_in, v_w_in, [big_mix[l][0] for l in range(nl)]),
                           ("w_out", w_out, m_w_out, v_w_out, [big_mix[l][1] for l in range(nl)]),
                           ("ffn_up", ffn_up, m_ffn_up, v_ffn_up, [big_mlp[l][0] for l in range(nl)]),
                           ("ffn_down", ffn_down, m_ffn_down, v_ffn_down, [big_mlp[l][1] for l in range(nl)])]:
        outs[k] = tuple(_adamw_layers(w, m, v, gs, name=f"adamw_{k}"))

    order = ["ada_w", "ada_b", "pre_mix_g", "post_mix_g", "w_in", "conv_w", "conv_b", "conv_ln_g", "conv_ln_b", "pool_w",
             "pool_scale", "w_out", "pre_ffn_g", "post_ffn_g", "ffn_up", "ffn_conv_w", "ffn_conv_b", "ffn_down"]
    return (loss, dx) + tuple(outs[k][i] for i in range(4) for k in order)
```

```python
import functools

import jax
import jax.numpy as jnp
from jax import lax
from jax.experimental import pallas as pl
from jax.experimental.pallas import tpu as pltpu

F32 = jnp.float32
BF16 = jnp.bfloat16
MESH = pl.DeviceIdType.MESH

EPS = 1e-6
POOL_WINDOWS = (2, 4, 8, 16)
ADAM_LR = 0.001
ADAM_B1 = 0.9
ADAM_B2 = 0.999
ADAM_EPS = 1e-08
ADAM_WD = 0.01
ADAM_STEP = 10

N_CHIPS = 4
N_DEV = 8
LANES = 128
SUBLANES = 8
HALO = 32
FHALO = 8
VMEM_LIMIT = 60 * 1024 * 1024
MLP_TILE_ROWS = 256
MIXER_TILE_ROWS = 512
GRAD_CHUNK_ROWS = 2048


def _pos():
    return lax.axis_index("x"), lax.axis_index("y"), lax.axis_index("c")


def _flip(v, f):
    return 1 - v if f else v


def _full(shape):
    nd = len(shape)
    return pl.BlockSpec(shape, lambda *_: (0,) * nd)


_ANY = pl.BlockSpec(memory_space=pl.ANY)
_VMEM = pl.BlockSpec(memory_space=pltpu.VMEM)


def _sigmoid(v):
    return 1.0 / (1.0 + jnp.exp(-v))


def _colsum(v):
    return jnp.sum(v, axis=0, keepdims=True)


def _gather8(v, name, exchange=None):
    r, ccols = v.shape
    ex = exchange
    nci, nco = (len(ex.ins), len(ex.outs)) if ex else (0, 0)

    def body(*refs):
        v_ref, cin, out_ref, cout = refs[0], refs[1:1 + nci], refs[1 + nci], refs[2 + nci:2 + nci + nco]
        send_sems, recv_sems, local_sem = refs[2 + nci + nco:5 + nci + nco]
        if ex:
            sems = _Sems(*refs[5 + nci + nco:])
            ex.start(cin, cout, sems)
        x, y, c = _pos()
        me = 4 * x + 2 * y + c
        mine = pltpu.make_async_copy(v_ref, out_ref.at[me], local_sem)
        mine.start()
        peers = [(_flip(x, (k >> 2) & 1), _flip(y, (k >> 1) & 1), _flip(c, k & 1)) for k in range(1, N_DEV)]
        sends = []
        for k, peer in enumerate(peers):
            cp = pltpu.make_async_remote_copy(src_ref=v_ref, dst_ref=out_ref.at[me], send_sem=send_sems.at[k],
                                              recv_sem=recv_sems.at[k], device_id=peer, device_id_type=MESH)
            cp.start()
            sends.append(cp)
        for k, peer in enumerate(peers):
            pidx = 4 * peer[0] + 2 * peer[1] + peer[2]
            pltpu.make_async_remote_copy(src_ref=v_ref, dst_ref=out_ref.at[pidx], send_sem=send_sems.at[k],
                                         recv_sem=recv_sems.at[k], device_id=peer, device_id_type=MESH).wait_recv()
        for cp in sends:
            cp.wait_send()
        mine.wait()
        if ex:
            if ex.mid is not None:
                ex.mid(cin, cout, sems)
            ex.finish(cin, cout, sems)

    outs = pl.pallas_call(
        body, name=name, out_shape=[jax.ShapeDtypeStruct((N_DEV, r, ccols), v.dtype)] + (ex.outs if ex else []),
        in_specs=[_VMEM] + [_ANY] * nci, out_specs=[_VMEM] + [_ANY] * nco,
        scratch_shapes=[pltpu.SemaphoreType.DMA((N_DEV - 1,)), pltpu.SemaphoreType.DMA((N_DEV - 1,)),
                        pltpu.SemaphoreType.DMA(())] + (ex.scratch() if ex else []),
        input_output_aliases={1 + a: 1 + b for a, b in ex.aliases.items()} if ex else {},
        compiler_params=pltpu.CompilerParams(vmem_limit_bytes=VMEM_LIMIT),
    )(v, *(ex.ins if ex else []))
    return (outs[0], list(outs[1:])) if ex else outs[0]


def _rows_to_owners(v, name):
    _, r, ccols = v.shape

    def body(v_ref, out_ref, send_sems, recv_sems, local_sem):
        x, y, c = _pos()
        jm = 2 * x + y
        mine = pltpu.make_async_copy(v_ref.at[2 * jm + c], out_ref.at[jm], local_sem)
        mine.start()
        peers, pjs = _chip_peers(x, y, c)
        sends = []
        for k, peer in enumerate(peers):
            cp = pltpu.make_async_remote_copy(src_ref=v_ref.at[2 * pjs[k] + c], dst_ref=out_ref.at[jm],
                                              send_sem=send_sems.at[k], recv_sem=recv_sems.at[k],
                                              device_id=peer, device_id_type=MESH)
            cp.start()
            sends.append(cp)
        for k, peer in enumerate(peers):
            pltpu.make_async_remote_copy(src_ref=v_ref.at[0], dst_ref=out_ref.at[pjs[k]], send_sem=send_sems.at[k],
                                         recv_sem=recv_sems.at[k], device_id=peer, device_id_type=MESH).wait_recv()
        for cp in sends:
            cp.wait_send()
        mine.wait()

    return pl.pallas_call(
        body, name=name, out_shape=jax.ShapeDtypeStruct((N_CHIPS, r, ccols), v.dtype),
        in_specs=[_VMEM], out_specs=_VMEM,
        scratch_shapes=[pltpu.SemaphoreType.DMA((N_CHIPS - 1,)), pltpu.SemaphoreType.DMA((N_CHIPS - 1,)),
                        pltpu.SemaphoreType.DMA(())],
        compiler_params=pltpu.CompilerParams(vmem_limit_bytes=VMEM_LIMIT),
    )(v)


def _allreduce8(v, name):
    r, ccols = v.shape
    h = r // 2
    q = h // 2

    def body(v_ref, out_ref, whole, part, done, send_sems, recv_sems):
        x, y, c = _pos()
        sib = (x, y, 1 - c)
        mine = pl.ds(pl.multiple_of(c * h, SUBLANES), h)
        theirs = pl.ds(pl.multiple_of((1 - c) * h, SUBLANES), h)
        quarters = [pl.ds(pl.multiple_of(c * h + k * q, SUBLANES), q) for k in range(2)]
        along_x, along_y = (1 - x, y, c), (x, 1 - y, c)

        def exchange(pairs):
            cps = [pltpu.make_async_remote_copy(src_ref=src, dst_ref=dst, send_sem=send_sems.at[k], recv_sem=recv_sems.at[k],
                                                device_id=peer, device_id_type=MESH) for src, dst, k, peer in pairs]
            for cp in cps:
                cp.start()
            for cp in cps:
                cp.wait()

        exchange([(v_ref, whole, 0, sib)])
        out_ref[...] = v_ref[...] + whole[...]
        for stage, peers in enumerate(((along_x, along_y), (along_y, along_x))):
            exchange([(out_ref.at[quarters[k]], part.at[2 * stage + k], 1 + 2 * stage + k, peers[k]) for k in range(2)])
            for k in range(2):
                out_ref[quarters[k], :] = out_ref[quarters[k], :] + part[2 * stage + k]
        exchange([(out_ref.at[mine], done, 5, sib)])
        out_ref[theirs, :] = done[...]

    return pl.pallas_call(
        body, name=name, out_shape=jax.ShapeDtypeStruct((r, ccols), v.dtype),
        in_specs=[_VMEM], out_specs=_VMEM,
        scratch_shapes=[pltpu.VMEM((r, ccols), v.dtype), pltpu.VMEM((4, q, ccols), v.dtype), pltpu.VMEM((h, ccols), v.dtype),
                        pltpu.SemaphoreType.DMA((6,)), pltpu.SemaphoreType.DMA((6,))],
        compiler_params=pltpu.CompilerParams(vmem_limit_bytes=VMEM_LIMIT),
    )(v)


def _chip_peers(x, y, c):
    peers = [(_flip(x, (k >> 1) & 1), _flip(y, k & 1), c) for k in range(1, N_CHIPS)]
    return peers, [2 * p[0] + p[1] for p in peers]


class _Exchange:
    def __init__(self, ins, outs, aliases, n_sems, n_local, start, finish, mid=None, mid_at=1.0, sibling=False, chips=False):
        self.ins, self.outs, self.aliases = list(ins), list(outs), dict(aliases)
        self.n_sems, self.n_local, self.start, self.finish = n_sems, n_local, start, finish
        self.mid, self.mid_at = mid, mid_at
        self.sibling, self.chips = sibling, chips

    def collective_id(self):
        return {(True, False): 1, (False, True): 2, (True, True): 3}[(self.sibling, self.chips)]

    def handshake(self):
        x, y, c = _pos()
        peers = ([(x, y, 1 - c)] if self.sibling else []) + (_chip_peers(x, y, c)[0] if self.chips else [])
        barrier = pltpu.get_barrier_semaphore()
        for peer in peers:
            pl.semaphore_signal(barrier, inc=1, device_id=peer, device_id_type=MESH)
        pl.semaphore_wait(barrier, len(peers))

    def scratch(self):
        return [pltpu.SemaphoreType.DMA((self.n_sems,)), pltpu.SemaphoreType.DMA((self.n_sems,)),
                pltpu.SemaphoreType.DMA((max(self.n_local, 1),))]


class _Sems:
    def __init__(self, send, recv, local, base=0, lbase=0):
        self.send, self.recv, self.loc, self.base, self.lbase = send, recv, local, base, lbase

    def shifted(self, by, lby):
        return _Sems(self.send, self.recv, self.loc, self.base + by, self.lbase + lby)

    def local(self, k):
        return self.loc.at[self.lbase + k]


def _remote(src, dst, sems, k, peer):
    return pltpu.make_async_remote_copy(src_ref=src, dst_ref=dst, send_sem=sems.send.at[sems.base + k],
                                        recv_sem=sems.recv.at[sems.base + k], device_id=peer, device_id_type=MESH)


def _combine(exs):
    ins = [a for ex in exs for a in ex.ins]
    outs = [o for ex in exs for o in ex.outs]
    aliases, spans, ni, no, ns, nloc = {}, [], 0, 0, 0, 0
    for ex in exs:
        aliases.update({ni + a: no + b for a, b in ex.aliases.items()})
        spans.append((ni, no, ns, nloc))
        ni, no, ns, nloc = ni + len(ex.ins), no + len(ex.outs), ns + ex.n_sems, nloc + ex.n_local

    def each(which):
        def run(ins_, outs_, sems):
            for ex, (i0, o0, s0, l0) in zip(exs, spans):
                stage = getattr(ex, which)
                if stage is not None:
                    stage(ins_[i0:i0 + len(ex.ins)], outs_[o0:o0 + len(ex.outs)], sems.shifted(s0, l0))
        return run

    mids = [ex.mid_at for ex in exs if ex.mid is not None]
    return _Exchange(ins, outs, aliases, ns, nloc, each("start"), each("finish"),
                     mid=each("mid") if mids else None, mid_at=max(mids) if mids else 1.0,
                     sibling=any(ex.sibling for ex in exs), chips=any(ex.chips for ex in exs))


def _gather(shards, mid_at=1.0):
    n = len(shards)
    per = N_CHIPS - 1
    halves = [s.reshape(2, s.shape[0] // 2, s.shape[1]) for s in shards]

    def copies(ins, outs, sems):
        x, y, c = _pos()
        jm = 2 * x + y
        sib = (x, y, 1 - c)
        peers, pjs = _chip_peers(x, y, c)
        sends, recvs, passes, passed = [], [], [], []
        for a in range(n):
            own = _remote(ins[a], outs[a].at[jm], sems, 2 * n * per + a, sib)
            sends.append(own)
            passed.append(own)
            for k, peer in enumerate(peers):
                landed, theirs = outs[a].at[pjs[k], c], outs[a].at[pjs[k], 1 - c]
                sends.append(_remote(ins[a].at[c], outs[a].at[jm, c], sems, 2 * (a * per + k), peer))
                recvs.append(_remote(landed, landed, sems, 2 * (a * per + k), peer))
                passes.append(_remote(landed, landed, sems, 2 * (a * per + k) + 1, sib))
                passed.append(_remote(theirs, theirs, sems, 2 * (a * per + k) + 1, sib))
        return sends, recvs, passes, passed

    def start(ins, outs, sems):
        for cp in copies(ins, outs, sems)[0]:
            cp.start()

    def mid(ins, outs, sems):
        _, recvs, passes, _ = copies(ins, outs, sems)
        for got, fwd in zip(recvs, passes):
            got.wait_recv()
            fwd.start()

    def finish(ins, outs, sems):
        sends, _, passes, passed = copies(ins, outs, sems)
        for cp in passed:
            cp.wait_recv()
        for cp in sends + passes:
            cp.wait_send()

    outs = [jax.ShapeDtypeStruct((N_CHIPS,) + h.shape, h.dtype) for h in halves]
    return _Exchange(halves, outs, {}, 2 * n * per + n, 0, start, finish, mid=mid, mid_at=mid_at, sibling=True, chips=True)


def _whole(gathered):
    return [g.reshape(g.shape[0], g.shape[1] * g.shape[2], g.shape[3]) for g in gathered]


def _swap_halves(gs):
    n = len(gs)
    halves = [g.reshape(g.shape[0], 2, g.shape[1] // 2, g.shape[2]) for g in gs]

    def copies(ins, outs, sems):
        x, y, c = _pos()
        sib = (x, y, 1 - c)
        return [_remote(ins[a].at[:, 1 - c], outs[a], sems, a, sib) for a in range(n)]

    def start(ins, outs, sems):
        for cp in copies(ins, outs, sems):
            cp.start()

    def finish(ins, outs, sems):
        for cp in copies(ins, outs, sems):
            cp.wait()

    outs = [jax.ShapeDtypeStruct((g.shape[0], g.shape[1] // 2, g.shape[2]), g.dtype) for g in gs]
    return _Exchange(halves, outs, {}, n, 0, start, finish, sibling=True)


def _chip_exchange(ps):
    n = len(ps)
    per = N_CHIPS - 1

    def copies(ins, outs, sems):
        x, y, c = _pos()
        peers, pjs = _chip_peers(x, y, c)
        return [_remote(ins[a].at[pjs[k]], outs[a].at[k], sems, a * per + k, peer)
                for a in range(n) for k, peer in enumerate(peers)]

    def start(ins, outs, sems):
        for cp in copies(ins, outs, sems):
            cp.start()

    def finish(ins, outs, sems):
        for cp in copies(ins, outs, sems):
            cp.wait()

    outs = [jax.ShapeDtypeStruct((per,) + p.shape[1:], p.dtype) for p in ps]
    return _Exchange(ps, outs, {}, n * per, 0, start, finish, chips=True)


def _sibling_share(fs):
    n = len(fs)

    def copies(outs, sems):
        x, y, c = _pos()
        sib = (x, y, 1 - c)
        sends = [_remote(outs[a].at[c], outs[a].at[c], sems, a, sib) for a in range(n)]
        recvs = [_remote(outs[a].at[1 - c], outs[a].at[1 - c], sems, a, sib) for a in range(n)]
        return sends, recvs

    def start(ins, outs, sems):
        for cp in copies(outs, sems)[0]:
            cp.start()

    def finish(ins, outs, sems):
        sends, recvs = copies(outs, sems)
        for cp in recvs:
            cp.wait_recv()
        for cp in sends:
            cp.wait_send()

    outs = [jax.ShapeDtypeStruct(f.shape, f.dtype) for f in fs]
    return _Exchange(fs, outs, {a: a for a in range(n)}, n, 0, start, finish, sibling=True)


def _run_exchange(ex, name):
    ni, no = len(ex.ins), len(ex.outs)

    def body(*refs):
        ins, outs, sems = refs[:ni], refs[ni:ni + no], _Sems(*refs[ni + no:])
        ex.handshake()
        ex.start(ins, outs, sems)
        if ex.mid is not None:
            ex.mid(ins, outs, sems)
        ex.finish(ins, outs, sems)

    return pl.pallas_call(
        body, name=name, out_shape=ex.outs, in_specs=[_ANY] * ni, out_specs=[_ANY] * no,
        input_output_aliases=ex.aliases, scratch_shapes=ex.scratch(),
        compiler_params=pltpu.CompilerParams(collective_id=ex.collective_id()),
    )(*ex.ins)


def _grid_call(body, name, grid, in_specs, out_specs, out_shape, scratch_shapes, args, exchange=None):
    ni, no = len(in_specs), len(out_specs)
    params = pltpu.CompilerParams(dimension_semantics=("arbitrary",) * len(grid), vmem_limit_bytes=VMEM_LIMIT)
    if exchange is None:
        outs = pl.pallas_call(body, name=name, grid=grid, in_specs=in_specs, out_specs=out_specs, out_shape=out_shape,
                              scratch_shapes=scratch_shapes, compiler_params=params)(*args)
        return list(outs), []
    ex = exchange
    nci, nco = len(ex.ins), len(ex.outs)

    def hosted(*refs):
        cin = refs[ni:ni + nci]
        cout = refs[ni + nci + no:ni + nci + no + nco]
        sems = _Sems(*refs[len(refs) - 3:])
        main = refs[:ni] + refs[ni + nci:ni + nci + no] + refs[ni + nci + no + nco:len(refs) - 3]
        ids = [pl.program_id(a) for a in range(len(grid))]
        first = functools.reduce(lambda p, q: p & q, [i == 0 for i in ids])
        last = functools.reduce(lambda p, q: p & q, [i == g - 1 for i, g in zip(ids, grid)])

        @pl.when(first)
        def _():
            ex.handshake()
            ex.start(cin, cout, sems)

        if ex.mid is not None:
            steps = functools.reduce(lambda p, q: p * q, grid)
            flat = functools.reduce(lambda p, q: p * q[1] + q[0], zip(ids[1:], grid[1:]), ids[0])

            @pl.when(flat == min(steps - 1, int(ex.mid_at * steps)))
            def _():
                ex.mid(cin, cout, sems)

        body(*main)

        @pl.when(last)
        def _():
            ex.finish(cin, cout, sems)

    outs = pl.pallas_call(
        hosted, name=name, grid=grid, in_specs=list(in_specs) + [_ANY] * nci, out_specs=list(out_specs) + [_ANY] * nco,
        out_shape=list(out_shape) + ex.outs, scratch_shapes=list(scratch_shapes) + ex.scratch(),
        input_output_aliases={ni + a: no + b for a, b in ex.aliases.items()},
        compiler_params=pltpu.CompilerParams(dimension_semantics=("arbitrary",) * len(grid), vmem_limit_bytes=VMEM_LIMIT,
                                             collective_id=ex.collective_id()),
    )(*args, *ex.ins)
    return list(outs[:no]), list(outs[no:])


SUM_BLOCK_BYTES = 4 * 1024 * 1024
ADAM_BLOCK_BYTES = 2 * 1024 * 1024


def _row_tile(rows, cols, itemsize, budget):
    best = None
    for t in range(16, rows + 1, 16):
        if rows % t == 0 and t * cols * itemsize <= budget:
            best = t
    return best if best is not None else rows


def _turns(step_counts):
    offs, total = [], 0
    for n in step_counts:
        offs.append(total)
        total += n
    own = [lambda s, o=o, n=n: jnp.clip(s - o, 0, n - 1) for o, n in zip(offs, step_counts)]
    mine = [lambda s, o=o, n=n: (s >= o) & (s < o + n) for o, n in zip(offs, step_counts)]
    return total, own, mine


def _pair_sums(gs, r1s, cidx, name):
    n = len(gs)
    tiles = [_row_tile(g.shape[1] // 2, g.shape[2], 4, SUM_BLOCK_BYTES) for g in gs]
    nts = [g.shape[1] // 2 // tr for g, tr in zip(gs, tiles)]
    total, own, mine = _turns([g.shape[0] * nt for g, nt in zip(gs, nts)])

    def body(c_ref, *refs):
        s = pl.program_id(0)
        for a in range(n):
            g_ref, r_ref, o_ref = refs[2 * a], refs[2 * a + 1], refs[2 * n + a]

            @pl.when(mine[a](s))
            def _(g_ref=g_ref, r_ref=r_ref, o_ref=o_ref):
                o_ref[...] = (g_ref[...].astype(F32) + r_ref[...].astype(F32)).astype(o_ref.dtype)

    in_specs, out_specs = [], []
    for a, (g, tr, nt) in enumerate(zip(gs, tiles, nts)):
        blk = (None, tr, g.shape[2])
        in_specs += [pl.BlockSpec(blk, lambda s, c_ref, a=a, nt=nt: (own[a](s) // nt, c_ref[0] * nt + own[a](s) % nt, 0)),
                     pl.BlockSpec(blk, lambda s, c_ref, a=a, nt=nt: (own[a](s) // nt, own[a](s) % nt, 0))]
        out_specs.append(pl.BlockSpec(blk, lambda s, c_ref, a=a, nt=nt: (own[a](s) // nt, own[a](s) % nt, 0)))
    return pl.pallas_call(
        body, name=name, out_shape=[jax.ShapeDtypeStruct((g.shape[0], g.shape[1] // 2, g.shape[2]), g.dtype) for g in gs],
        grid_spec=pltpu.PrefetchScalarGridSpec(num_scalar_prefetch=1, grid=(total,), in_specs=in_specs, out_specs=out_specs),
        compiler_params=pltpu.CompilerParams(dimension_semantics=("arbitrary",), vmem_limit_bytes=VMEM_LIMIT),
    )(cidx, *[x for pair in zip(gs, r1s) for x in pair])


def _chip_sums(ps, r2s, idx, name):
    n = len(ps)
    tiles = [_row_tile(p.shape[1], p.shape[2], 4, SUM_BLOCK_BYTES) for p in ps]
    total, own, mine = _turns([p.shape[1] // tr for p, tr in zip(ps, tiles)])

    def body(i_ref, *refs):
        s = pl.program_id(0)
        for a in range(n):
            p_ref, r_ref, o_ref = refs[2 * a], refs[2 * a + 1], refs[2 * n + a]

            @pl.when(mine[a](s))
            def _(p_ref=p_ref, r_ref=r_ref, o_ref=o_ref):
                acc = p_ref[...].astype(F32)
                for k in range(N_CHIPS - 1):
                    acc = acc + r_ref[k].astype(F32)
                o_ref[...] = acc

    in_specs, out_specs = [], []
    for a, (p, tr) in enumerate(zip(ps, tiles)):
        ccols = p.shape[2]
        in_specs += [pl.BlockSpec((None, tr, ccols), lambda s, i_ref, a=a: (i_ref[0], own[a](s), 0)),
                     pl.BlockSpec((N_CHIPS - 1, tr, ccols), lambda s, i_ref, a=a: (0, own[a](s), 0))]
        out_specs.append(pl.BlockSpec((None, tr, ccols), lambda s, i_ref, a=a: (i_ref[1], own[a](s), 0)))
    return pl.pallas_call(
        body, name=name, out_shape=[jax.ShapeDtypeStruct((2, p.shape[1], p.shape[2]), F32) for p in ps],
        grid_spec=pltpu.PrefetchScalarGridSpec(num_scalar_prefetch=1, grid=(total,), in_specs=in_specs, out_specs=out_specs),
        compiler_params=pltpu.CompilerParams(dimension_semantics=("arbitrary",), vmem_limit_bytes=VMEM_LIMIT),
    )(idx, *[x for pair in zip(ps, r2s) for x in pair])


def _adam_math(w, g, m, v):
    m2 = ADAM_B1 * m + (1.0 - ADAM_B1) * g
    v2 = ADAM_B2 * v + (1.0 - ADAM_B2) * (g * g)
    m_hat = m2 / (1.0 - ADAM_B1 ** ADAM_STEP)
    v_hat = v2 / (1.0 - ADAM_B2 ** ADAM_STEP)
    delta = -ADAM_LR * (m_hat / (jnp.sqrt(v_hat) + ADAM_EPS) + ADAM_WD * w)
    return delta, m2, v2


def _adamw_layers(w, m, v, gs, name):
    nl, r, ccols = w.shape
    ng = len(gs)
    tr = _row_tile(r, ccols, 4, ADAM_BLOCK_BYTES)
    nt = r // tr

    def body(w_ref, m_ref, v_ref, *rest):
        g_refs, (go_ref, d_ref, mo_ref, vo_ref) = rest[:ng], rest[ng:]
        l = pl.program_id(0)
        g = g_refs[0][...]
        for k in range(1, ng):
            g = jnp.where(l == k, g_refs[k][...], g)
        delta, m2, v2 = _adam_math(w_ref[...], g, m_ref[...], v_ref[...])
        go_ref[...] = g
        d_ref[...] = delta
        mo_ref[...] = m2
        vo_ref[...] = v2

    big = pl.BlockSpec((None, tr, ccols), lambda l, i: (l, i, 0))

    def gspec(k):
        return pl.BlockSpec((tr, ccols), lambda l, i: (jnp.where(l == k, i, jnp.where(l < k, 0, nt - 1)), 0))

    assert ng == nl
    return _grid_call(body, name, (nl, nt), in_specs=[big, big, big] + [gspec(k) for k in range(ng)],
                      out_specs=[big, big, big, big], out_shape=[jax.ShapeDtypeStruct(w.shape, F32)] * 4,
                      scratch_shapes=[], args=(w, m, v, *gs))[0]


def _adamw_flat(w, g, m, v, name):
    r, ccols = w.shape

    def body(w_ref, g_ref, m_ref, v_ref, d_ref, mo_ref, vo_ref):
        delta, m2, v2 = _adam_math(w_ref[...], g_ref[...], m_ref[...], v_ref[...])
        d_ref[...] = delta
        mo_ref[...] = m2
        vo_ref[...] = v2

    return pl.pallas_call(
        body, name=name, out_shape=[jax.ShapeDtypeStruct((r, ccols), F32)] * 3,
        in_specs=[_VMEM] * 4, out_specs=[_VMEM] * 3,
        compiler_params=pltpu.CompilerParams(vmem_limit_bytes=VMEM_LIMIT),
    )(w, g, m, v)


def _ada_forward(c_all, ada_w, ada_b_cols, name, exchange=None):
    nl, d, ncols = ada_w.shape
    bg = c_all.shape[0]
    tn = 512 if ncols % 512 == 0 else ncols

    def body(c_ref, w_ref, b_ref, o_ref):
        cv = c_ref[...]
        ca = (cv * _sigmoid(cv)).astype(BF16)
        o_ref[...] = jnp.dot(ca, w_ref[...].astype(BF16), preferred_element_type=F32) + b_ref[...]

    outs, got = _grid_call(
        body, name, (nl, ncols // tn),
        in_specs=[pl.BlockSpec((bg, d), lambda l, j: (0, 0)),
                  pl.BlockSpec((None, d, tn), lambda l, j: (l, 0, j)),
                  pl.BlockSpec((None, 1, tn), lambda l, j: (l, 0, j))],
        out_specs=[pl.BlockSpec((None, bg, tn), lambda l, j: (l, 0, j))],
        out_shape=[jax.ShapeDtypeStruct((nl, bg, ncols), F32)], scratch_shapes=[], args=(c_all, ada_w, ada_b_cols),
        exchange=exchange)
    return outs[0], got


def _ada_update(c_all, dmod_cols, w, m, v, name, exchange=None):
    nl, d, ncols = w.shape
    bg = c_all.shape[0]
    tn = 512 if ncols % 512 == 0 else ncols

    def body(c_ref, dm_ref, w_ref, m_ref, v_ref, go_ref, d_ref, mo_ref, vo_ref):
        cv = c_ref[...]
        ca = (cv * _sigmoid(cv)).astype(BF16)
        g = lax.dot_general(ca, dm_ref[...].astype(BF16), (((0,), (0,)), ((), ())), preferred_element_type=F32)
        delta, m2, v2 = _adam_math(w_ref[...], g, m_ref[...], v_ref[...])
        go_ref[...] = g
        d_ref[...] = delta
        mo_ref[...] = m2
        vo_ref[...] = v2

    big = pl.BlockSpec((None, d, tn), lambda l, j: (l, 0, j))
    return _grid_call(
        body, name, (nl, ncols // tn),
        in_specs=[pl.BlockSpec((bg, d), lambda l, j: (0, 0)),
                  pl.BlockSpec((None, bg, tn), lambda l, j: (l, 0, j)), big, big, big],
        out_specs=[big, big, big, big], out_shape=[jax.ShapeDtypeStruct(w.shape, F32)] * 4,
        scratch_shapes=[], args=(c_all, dmod_cols, w, m, v), exchange=exchange)


def _load_weights(first, pairs, sems):
    @pl.when(first)
    def _():
        cps = [pltpu.make_async_copy(src, dst, sems.at[k]) for k, (src, dst) in enumerate(pairs)]
        for cp in cps:
            cp.start()
        for cp in cps:
            cp.wait()


def _ada_norm(xv, g, sc, sh):
    r = lax.rsqrt(jnp.mean(xv * xv, axis=-1, keepdims=True) + EPS)
    xn = xv * r
    return (xn * g) * (1.0 + sc) + sh, xn, r


def _ada_norm_bwd(dh, xn, r, g, sc):
    d_sh = _colsum(dh)
    d_sc = _colsum(dh * (xn * g))
    dxg = dh * (1.0 + sc)
    d_g = _colsum(dxg * xn)
    gd = dxg * g
    dx = r * (gd - xn * jnp.mean(gd * xn, axis=-1, keepdims=True))
    return dx, d_sh, d_sc, d_g


def _gated_residual_bwd(dxo, o, g_post, gt):
    r = lax.rsqrt(jnp.mean(o * o, axis=-1, keepdims=True) + EPS)
    on = o * r
    d_gt = _colsum(dxo * (on * g_post))
    dy = dxo * (1.0 + gt)
    d_gp = _colsum(dy * on)
    gd = dy * g_post
    do = r * (gd - on * jnp.mean(gd * on, axis=-1, keepdims=True))
    return do, d_gt, d_gp


def _seq_positions(i, tm, width):
    return i * tm + lax.broadcasted_iota(jnp.int32, (tm, width), 0)


def _fill_phases(ext, phases):
    rows = ext.shape[0]
    ev = ext[...]
    for r in range(1, SUBLANES):
        phases[r - 1] = pltpu.roll(ev, rows - r, axis=0)


def _shifted_rows(ext, phases, offset, n):
    q, r = divmod(offset, SUBLANES)
    if r == 0:
        return ext[pl.ds(q * SUBLANES, n), :]
    return phases[r - 1, pl.ds(q * SUBLANES, n), :]


def _rows_before(halo, cur, shift):
    e = jnp.concatenate([halo, cur], axis=0)
    return pltpu.roll(e, shift, axis=0)[halo.shape[0]:, :]


def _rows_after(cur, halo, shift):
    e = jnp.concatenate([cur, halo], axis=0)
    return pltpu.roll(e, e.shape[0] - shift, axis=0)[:cur.shape[0], :]


def _mixer_forward(x, mod, vec_d, vec_c, cw, pw, win_g, wout_g, taps, tm, name, exchange=None):
    nb, s, d = x.shape
    n = s // tm
    nj, _, dcol = win_g.shape
    din = nj * dcol
    dc = vec_c.shape[-1]
    dpool = din - 2 * dc
    dmix = dc + dpool
    ro = wout_g.shape[1]
    ngrp = dpool // LANES

    def body(x_ref, mod_ref, vd_ref, vc_ref, cw_ref, pw_ref, win_hbm, wout_hbm,
             xo_ref, h_ref, u_ref, ac_ref, dp_ref, z_ref, o_ref,
             win_v, wout_v, ext_a, ext_p, phases, sems):
        b, i = pl.program_id(0), pl.program_id(1)
        pairs = [(win_hbm.at[j], win_v.at[:, pl.ds(j * dcol, dcol)]) for j in range(nj)]
        pairs += [(wout_hbm.at[j], wout_v.at[pl.ds(j * ro, ro), :]) for j in range(nj)]
        _load_weights((b == 0) & (i == 0), pairs, sems)

        xv = x_ref[...]
        h, _, _ = _ada_norm(xv, vd_ref[0:1, :], mod_ref[1:2, :], mod_ref[0:1, :])
        hb = h.astype(BF16)
        h_ref[...] = hb
        u = jnp.dot(hb, win_v[...], preferred_element_type=F32)
        u_ref[...] = u.astype(BF16)
        ag = u[:, :dc] * _sigmoid(u[:, dc:2 * dc])
        up = u[:, 2 * dc:]

        @pl.when(i == 0)
        def _():
            ext_a[0:HALO, :] = jnp.zeros((HALO, dc), F32)
            ext_p[0:HALO, :] = jnp.zeros((HALO, dpool), F32)

        @pl.when(i > 0)
        def _():
            ext_a[0:HALO, :] = ext_a[tm:tm + HALO, :]
            ext_p[0:HALO, :] = ext_p[tm:tm + HALO, :]

        ext_a[HALO:HALO + tm, :] = ag
        ext_p[HALO:HALO + tm, :] = up

        acc = jnp.broadcast_to(vc_ref[0:1, :], (tm, dc))
        _fill_phases(ext_a, phases)
        for k in range(taps):
            acc = acc + cw_ref[k:k + 1, :] * _shifted_rows(ext_a, phases, HALO - (taps - 1) + k, tm)
        ac_ref[...] = acc.astype(BF16)
        mu = jnp.mean(acc, axis=-1, keepdims=True)
        xc = acc - mu
        var = jnp.mean(xc * xc, axis=-1, keepdims=True)
        al = (xc * lax.rsqrt(var + EPS)) * vc_ref[1:2, :] + vc_ref[2:3, :]
        a = al * _sigmoid(al)

        pos = _seq_positions(i, tm, LANES)
        parts = [a.astype(BF16)]
        for g in range(ngrp):
            w = POOL_WINDOWS[g]
            cols = slice(g * LANES, (g + 1) * LANES)
            sw = ext_p[:, cols]
            step = 1
            while step < w:
                sw = sw + pltpu.roll(sw, step, axis=0)
                step *= 2
            cnt = jnp.minimum(pos + 1, w).astype(F32)
            dg = (sw[HALO:, :] / cnt - up[:, cols]).astype(BF16)
            dp_ref[:, cols] = dg
            q = jnp.dot(dg, pw_ref[g], preferred_element_type=F32)
            parts.append((q * vc_ref[3:4, cols]).astype(BF16))
        z = jnp.concatenate(parts, axis=-1)
        z_ref[...] = z
        o = jnp.dot(z, wout_v[...], preferred_element_type=F32)
        o_ref[...] = o
        r2 = lax.rsqrt(jnp.mean(o * o, axis=-1, keepdims=True) + EPS)
        xo_ref[...] = xv + (1.0 + mod_ref[2:3, :]) * ((o * r2) * vd_ref[1:2, :])

    def tile(width):
        return pl.BlockSpec((None, tm, width), lambda b, i: (b, i, 0))

    return _grid_call(
        body, name, (nb, n),
        in_specs=[tile(d), pl.BlockSpec((None, 8, d), lambda b, i: (b, 0, 0)), _full(vec_d.shape), _full(vec_c.shape),
                  _full(cw.shape), _full(pw.shape), _ANY, _ANY],
        out_specs=[tile(d), tile(d), tile(din), tile(dc), tile(dpool), tile(dmix), tile(d)],
        out_shape=[jax.ShapeDtypeStruct((nb, s, d), F32), jax.ShapeDtypeStruct((nb, s, d), BF16),
                   jax.ShapeDtypeStruct((nb, s, din), BF16), jax.ShapeDtypeStruct((nb, s, dc), BF16),
                   jax.ShapeDtypeStruct((nb, s, dpool), BF16), jax.ShapeDtypeStruct((nb, s, dmix), BF16),
                   jax.ShapeDtypeStruct((nb, s, d), F32)],
        scratch_shapes=[pltpu.VMEM((d, din), BF16), pltpu.VMEM((dmix, d), BF16),
                        pltpu.VMEM((HALO + tm, dc), F32), pltpu.VMEM((HALO + tm, dpool), F32),
                        pltpu.VMEM((SUBLANES - 1, HALO + tm, dc), F32), pltpu.SemaphoreType.DMA((2 * nj,))],
        args=(x, mod, vec_d, vec_c, cw, pw, win_g, wout_g), exchange=exchange)


def _mixer_backward(dxo, x, o, u, ac, dpl, mod, vec_d, vec_c, cw, pw, win_g, wout_g, taps, tm, name, exchange=None):
    nb, s, d = x.shape
    n = s // tm
    nj, _, dcol = win_g.shape
    din = nj * dcol
    dc = vec_c.shape[-1]
    dpool = din - 2 * dc
    dmix = dc + dpool
    ro = wout_g.shape[1]
    ngrp = dpool // LANES
    rext = tm + HALO

    def body(dxo_ref, x_ref, o_ref, u_ref, ac_ref, dp_ref, mod_ref, vd_ref, vc_ref, cw_ref, pw_ref, win_hbm, wout_hbm,
             dx_ref, du_ref, dob_ref, rowd_ref, rowb_ref, rowc_ref, dcw_ref, dpw_ref,
             win_v, wout_v, ext_a, ext_p, phases, sems):
        b, i = pl.program_id(0), pl.program_id(1)
        first = (b == 0) & (i == 0)
        pairs = [(win_hbm.at[j], win_v.at[:, pl.ds(j * dcol, dcol)]) for j in range(nj)]
        pairs += [(wout_hbm.at[j], wout_v.at[pl.ds(j * ro, ro), :]) for j in range(nj)]
        _load_weights(first, pairs, sems)

        @pl.when(first)
        def _():
            rowd_ref[...] = jnp.zeros_like(rowd_ref)
            rowc_ref[...] = jnp.zeros_like(rowc_ref)
            dcw_ref[...] = jnp.zeros_like(dcw_ref)
            dpw_ref[...] = jnp.zeros_like(dpw_ref)

        @pl.when(i == 0)
        def _():
            rowb_ref[...] = jnp.zeros_like(rowb_ref)
            ext_a[tm:rext, :] = jnp.zeros((HALO, dc), F32)
            ext_p[tm:rext, :] = jnp.zeros((HALO, dpool), F32)

        @pl.when(i > 0)
        def _():
            ext_a[tm:rext, :] = ext_a[0:HALO, :]
            ext_p[tm:rext, :] = ext_p[0:HALO, :]

        g_pre, g_post = vd_ref[0:1, :], vd_ref[1:2, :]
        sh, sc, gt = mod_ref[0:1, :], mod_ref[1:2, :], mod_ref[2:3, :]
        do, d_gt, d_gp = _gated_residual_bwd(dxo_ref[...], o_ref[...], g_post, gt)
        dob = do.astype(BF16)
        dob_ref[...] = dob
        dz = lax.dot_general(dob, wout_v[...], (((1,), (1,)), ((), ())), preferred_element_type=F32)

        acv = ac_ref[...].astype(F32)
        mu = jnp.mean(acv, axis=-1, keepdims=True)
        xc = acv - mu
        rstd = lax.rsqrt(jnp.mean(xc * xc, axis=-1, keepdims=True) + EPS)
        an = xc * rstd
        lg = vc_ref[1:2, :]
        al = an * lg + vc_ref[2:3, :]
        sg = _sigmoid(al)
        dal = dz[:, :dc] * (sg * (1.0 + al * (1.0 - sg)))
        d_lg = _colsum(dal * an)
        d_lb = _colsum(dal)
        dan = dal * lg
        dac = rstd * (dan - jnp.mean(dan, axis=-1, keepdims=True) - an * jnp.mean(dan * an, axis=-1, keepdims=True))
        d_cb = _colsum(dac)
        ext_a[0:tm, :] = dac
        uv = u_ref[:, 0:dc].astype(F32)
        sgu = _sigmoid(u_ref[:, dc:2 * dc].astype(F32))
        ag = uv * sgu
        dag = jnp.zeros((tm, dc), F32)
        _fill_phases(ext_a, phases)
        for k in range(taps):
            sl = _shifted_rows(ext_a, phases, taps - 1 - k, tm)
            dag = dag + cw_ref[k:k + 1, :] * sl
            dcw_ref[k:k + 1, :] += _colsum(ag * sl)
        du_ref[:, 0:dc] = (dag * sgu).astype(BF16)
        du_ref[:, dc:2 * dc] = (dag * uv * (sgu * (1.0 - sgu))).astype(BF16)

        pos = _seq_positions(n - 1 - i, tm, LANES)
        d_ps = []
        for g in range(ngrp):
            w = POOL_WINDOWS[g]
            cols = slice(g * LANES, (g + 1) * LANES)
            gcols = slice(dc + g * LANES, dc + (g + 1) * LANES)
            dgb = dp_ref[:, cols]
            q = jnp.dot(dgb, pw_ref[g], preferred_element_type=F32)
            dpg = dz[:, gcols]
            d_ps.append(_colsum(dpg * q))
            dq = (dpg * vc_ref[3:4, cols]).astype(BF16)
            dpw_ref[g] += lax.dot_general(dgb, dq, (((0,), (0,)), ((), ())), preferred_element_type=F32)
            dd = lax.dot_general(dq, pw_ref[g], (((1,), (1,)), ((), ())), preferred_element_type=F32)
            cnt = jnp.minimum(pos + 1, w).astype(F32)
            ext_p[0:tm, cols] = dd / cnt
            sw = ext_p[:, cols]
            step = 1
            while step < w:
                sw = sw + pltpu.roll(sw, rext - step, axis=0)
                step *= 2
            du_ref[:, 2 * dc + g * LANES:2 * dc + (g + 1) * LANES] = (sw[0:tm, :] - dd).astype(BF16)
        rowc_ref[0:1, :] += d_cb
        rowc_ref[1:2, :] += d_lg
        rowc_ref[2:3, :] += d_lb
        rowc_ref[3:4, :] += jnp.concatenate(d_ps, axis=-1)

        dh = lax.dot_general(du_ref[...], win_v[...], (((1,), (1,)), ((), ())), preferred_element_type=F32)
        _, xn, r1 = _ada_norm(x_ref[...], g_pre, sc, sh)
        dxb, d_sh, d_sc, d_g = _ada_norm_bwd(dh, xn, r1, g_pre, sc)
        dx_ref[...] = dxo_ref[...] + dxb
        rowd_ref[0:1, :] += d_g
        rowd_ref[1:2, :] += d_gp
        rowb_ref[0:1, :] += d_sh
        rowb_ref[1:2, :] += d_sc
        rowb_ref[2:3, :] += d_gt

    def tile(width):
        return pl.BlockSpec((None, tm, width), lambda b, i: (b, n - 1 - i, 0))

    return _grid_call(
        body, name, (nb, n),
        in_specs=[tile(d), tile(d), tile(d), tile(din), tile(dc), tile(dpool),
                  pl.BlockSpec((None, 8, d), lambda b, i: (b, 0, 0)), _full(vec_d.shape), _full(vec_c.shape),
                  _full(cw.shape), _full(pw.shape), _ANY, _ANY],
        out_specs=[tile(d), tile(din), tile(d), _full((8, d)), pl.BlockSpec((None, 8, d), lambda b, i: (b, 0, 0)),
                   _full((8, dc)), _full((HALO, dc)), _full(pw.shape)],
        out_shape=[jax.ShapeDtypeStruct((nb, s, d), F32), jax.ShapeDtypeStruct((nb, s, din), BF16),
                   jax.ShapeDtypeStruct((nb, s, d), BF16), jax.ShapeDtypeStruct((8, d), F32),
                   jax.ShapeDtypeStruct((nb, 8, d), F32), jax.ShapeDtypeStruct((8, dc), F32),
                   jax.ShapeDtypeStruct((HALO, dc), F32), jax.ShapeDtypeStruct(pw.shape, F32)],
        scratch_shapes=[pltpu.VMEM((d, din), BF16), pltpu.VMEM((dmix, d), BF16),
                        pltpu.VMEM((rext, dc), F32), pltpu.VMEM((rext, dpool), F32),
                        pltpu.VMEM((SUBLANES - 1, rext, dc), F32), pltpu.SemaphoreType.DMA((2 * nj,))],
        args=(dxo, x, o, u, ac, dpl, mod, vec_d, vec_c, cw, pw, win_g, wout_g), exchange=exchange)


def _ffn_forward(x, mod, vec_d, fw, wup_g, wdn_g, tm, name, exchange=None, target=None):
    nb, s, d = x.shape
    n = s // tm
    nj, _, ucol = wup_g.shape
    f2 = nj * ucol
    dff = f2 // 2
    rd = wdn_g.shape[1]
    nq = nj // 2
    cs = dff // nq

    with_loss = target is not None

    def body(*refs):
        refs = list(refs)
        x_ref, mod_ref, vd_ref, fw_ref = refs[:4]
        t_ref = refs.pop(4) if with_loss else None
        wup_hbm, wdn_hbm, xo_ref, h_ref, u_ref, uc_ref, hid_ref, o_ref = refs[4:12]
        sq_ref = refs.pop(12) if with_loss else None
        wup_v, wdn_v, prev_u, sems = refs[12:]
        b, i = pl.program_id(0), pl.program_id(1)
        pairs = [(wup_hbm.at[j], wup_v.at[:, pl.ds(j * ucol, ucol)]) for j in range(nj)]
        pairs += [(wdn_hbm.at[j], wdn_v.at[pl.ds(j * rd, rd), :]) for j in range(nj)]
        _load_weights((b == 0) & (i == 0), pairs, sems)

        if with_loss:
            @pl.when((b == 0) & (i == 0))
            def _():
                sq_ref[...] = jnp.zeros_like(sq_ref)

        @pl.when(i == 0)
        def _():
            prev_u[...] = jnp.zeros_like(prev_u)

        xv = x_ref[...]
        h, _, _ = _ada_norm(xv, vd_ref[2:3, :], mod_ref[4:5, :], mod_ref[3:4, :])
        hb = h.astype(BF16)
        h_ref[...] = hb

        def conv(cols):
            uc = jnp.dot(hb, wup_v[:, cols], preferred_element_type=F32)
            u_ref[:, cols] = uc.astype(BF16)
            before = prev_u[:, cols]
            prev_u[:, cols] = uc[tm - FHALO:, :]
            out = (fw_ref[3:4, cols] + fw_ref[2:3, cols] * uc + fw_ref[1:2, cols] * _rows_before(before, uc, 1)
                   + fw_ref[0:1, cols] * _rows_before(before, uc, 2))
            uc_ref[:, cols] = out.astype(BF16)
            return out

        o = jnp.zeros((tm, d), F32)
        for q in range(nq):
            val = conv(pl.ds(q * cs, cs))
            gate = conv(pl.ds(dff + q * cs, cs))
            hid = ((gate * _sigmoid(gate)) * val).astype(BF16)
            hid_ref[:, pl.ds(q * cs, cs)] = hid
            o = o + jnp.dot(hid, wdn_v[pl.ds(q * cs, cs), :], preferred_element_type=F32)
        o_ref[...] = o
        r2 = lax.rsqrt(jnp.mean(o * o, axis=-1, keepdims=True) + EPS)
        y = xv + (1.0 + mod_ref[5:6, :]) * ((o * r2) * vd_ref[3:4, :])
        if with_loss:
            e = y - t_ref[...]
            xo_ref[...] = e * (1.0 / d)
            sq_ref[0:1, :] += _colsum(e * e)
        else:
            xo_ref[...] = y

    def tile(width):
        return pl.BlockSpec((None, tm, width), lambda b, i: (b, i, 0))

    loss_in = [tile(d)] if with_loss else []
    return _grid_call(
        body, name, (nb, n),
        in_specs=[tile(d), pl.BlockSpec((None, 8, d), lambda b, i: (b, 0, 0)), _full(vec_d.shape), _full(fw.shape)]
        + loss_in + [_ANY, _ANY],
        out_specs=[tile(d), tile(d), tile(f2), tile(f2), tile(dff), tile(d)] + ([_full((8, d))] if with_loss else []),
        out_shape=[jax.ShapeDtypeStruct((nb, s, d), F32), jax.ShapeDtypeStruct((nb, s, d), BF16),
                   jax.ShapeDtypeStruct((nb, s, f2), BF16), jax.ShapeDtypeStruct((nb, s, f2), BF16),
                   jax.ShapeDtypeStruct((nb, s, dff), BF16), jax.ShapeDtypeStruct((nb, s, d), F32)]
        + ([jax.ShapeDtypeStruct((8, d), F32)] if with_loss else []),
        scratch_shapes=[pltpu.VMEM((d, f2), BF16), pltpu.VMEM((dff, d), BF16),
                        pltpu.VMEM((FHALO, f2), F32), pltpu.SemaphoreType.DMA((2 * nj,))],
        args=(x, mod, vec_d, fw) + ((target,) if with_loss else ()) + (wup_g, wdn_g), exchange=exchange)


def _ffn_backward(dxo, x, o, u, uc, mod, vec_d, fw, wup_g, wdn_g, tm, name, exchange=None):
    nb, s, d = x.shape
    n = s // tm
    nj, _, ucol = wup_g.shape
    f2 = nj * ucol
    dff = f2 // 2
    rd = wdn_g.shape[1]
    nq = nj // 2
    cs = dff // nq

    def body(dxo_ref, x_ref, o_ref, u_ref, uc_ref, mod_ref, vd_ref, fw_ref, wup_hbm, wdn_hbm,
             dx_ref, du_ref, dob_ref, rowd_ref, rowb_ref, dfw_ref,
             wup_v, wdn_v, next_d, sems):
        b, i = pl.program_id(0), pl.program_id(1)
        first = (b == 0) & (i == 0)
        pairs = [(wup_hbm.at[j], wup_v.at[:, pl.ds(j * ucol, ucol)]) for j in range(nj)]
        pairs += [(wdn_hbm.at[j], wdn_v.at[pl.ds(j * rd, rd), :]) for j in range(nj)]
        _load_weights(first, pairs, sems)

        @pl.when(first)
        def _():
            rowd_ref[...] = jnp.zeros_like(rowd_ref)
            dfw_ref[...] = jnp.zeros_like(dfw_ref)

        @pl.when(i == 0)
        def _():
            rowb_ref[...] = jnp.zeros_like(rowb_ref)
            next_d[...] = jnp.zeros_like(next_d)

        g_pre, g_post = vd_ref[2:3, :], vd_ref[3:4, :]
        sh, sc, gt = mod_ref[3:4, :], mod_ref[4:5, :], mod_ref[5:6, :]
        do, d_gt, d_gp = _gated_residual_bwd(dxo_ref[...], o_ref[...], g_post, gt)
        dob = do.astype(BF16)
        dob_ref[...] = dob

        def conv_bwd(cols, duc):
            uc = u_ref[:, cols].astype(F32)
            after = next_d[:, cols]
            next_d[:, cols] = duc[0:FHALO, :]
            d1 = _rows_after(duc, after, 1)
            d2 = _rows_after(duc, after, 2)
            dfw_ref[3:4, cols] += _colsum(duc)
            dfw_ref[2:3, cols] += _colsum(uc * duc)
            dfw_ref[1:2, cols] += _colsum(uc * d1)
            dfw_ref[0:1, cols] += _colsum(uc * d2)
            ob = (fw_ref[2:3, cols] * duc + fw_ref[1:2, cols] * d1 + fw_ref[0:1, cols] * d2).astype(BF16)
            du_ref[:, cols] = ob
            return lax.dot_general(ob, wup_v[:, cols], (((1,), (1,)), ((), ())), preferred_element_type=F32)

        dh = jnp.zeros((tm, d), F32)
        for q in range(nq):
            vcols = pl.ds(q * cs, cs)
            gcols = pl.ds(dff + q * cs, cs)
            dhid = lax.dot_general(dob, wdn_v[vcols, :], (((1,), (1,)), ((), ())), preferred_element_type=F32)
            val = uc_ref[:, vcols].astype(F32)
            gate = uc_ref[:, gcols].astype(F32)
            sg = _sigmoid(gate)
            act = gate * sg
            dval = dhid * act
            dgate = (dhid * val) * (sg + act * (1.0 - sg))
            dh = dh + conv_bwd(vcols, dval)
            dh = dh + conv_bwd(gcols, dgate)

        _, xn, r1 = _ada_norm(x_ref[...], g_pre, sc, sh)
        dxb, d_sh, d_sc, d_g = _ada_norm_bwd(dh, xn, r1, g_pre, sc)
        dx_ref[...] = dxo_ref[...] + dxb
        rowd_ref[2:3, :] += d_g
        rowd_ref[3:4, :] += d_gp
        rowb_ref[3:4, :] += d_sh
        rowb_ref[4:5, :] += d_sc
        rowb_ref[5:6, :] += d_gt

    def tile(width):
        return pl.BlockSpec((None, tm, width), lambda b, i: (b, n - 1 - i, 0))

    return _grid_call(
        body, name, (nb, n),
        in_specs=[tile(d), tile(d), tile(d), tile(f2), tile(f2), pl.BlockSpec((None, 8, d), lambda b, i: (b, 0, 0)),
                  _full(vec_d.shape), _full(fw.shape), _ANY, _ANY],
        out_specs=[tile(d), tile(f2), tile(d), _full((8, d)), pl.BlockSpec((None, 8, d), lambda b, i: (b, 0, 0)),
                   _full(fw.shape)],
        out_shape=[jax.ShapeDtypeStruct((nb, s, d), F32), jax.ShapeDtypeStruct((nb, s, f2), BF16),
                   jax.ShapeDtypeStruct((nb, s, d), BF16), jax.ShapeDtypeStruct((8, d), F32),
                   jax.ShapeDtypeStruct((nb, 8, d), F32), jax.ShapeDtypeStruct(fw.shape, F32)],
        scratch_shapes=[pltpu.VMEM((d, f2), BF16), pltpu.VMEM((dff, d), BF16),
                        pltpu.VMEM((FHALO, f2), F32), pltpu.SemaphoreType.DMA((2 * nj,))],
        args=(dxo, x, o, u, uc, mod, vec_d, fw, wup_g, wdn_g), exchange=exchange)


def _weight_grad(a, b, nblk, split, tt, name, exchange=None):
    t, ka = a.shape
    nb_ = b.shape[1]
    nk = t // tt
    if split == "cols":
        wa, wb, grid = ka, nb_ // nblk, (1, nk)
        a_spec = pl.BlockSpec((tt, ka), lambda j, k: (k, 0))
        b_spec = pl.BlockSpec((tt, nb_), lambda j, k: (k, 0))
        o_spec = pl.BlockSpec((nblk, wa, wb), lambda j, k: (0, 0, 0))
        acc_shape = (ka, nb_)
    elif split == "b":
        wa, wb, grid = ka, nb_ // nblk, (nblk, nk)
        a_spec = pl.BlockSpec((tt, wa), lambda j, k: (k, 0))
        b_spec = pl.BlockSpec((tt, wb), lambda j, k: (k, j))
        o_spec = pl.BlockSpec((None, wa, wb), lambda j, k: (j, 0, 0))
        acc_shape = (wa, wb)
    else:
        wa, wb, grid = ka // nblk, nb_, (nblk, nk)
        a_spec = pl.BlockSpec((tt, wa), lambda j, k: (k, j))
        b_spec = pl.BlockSpec((tt, wb), lambda j, k: (k, 0))
        o_spec = pl.BlockSpec((None, wa, wb), lambda j, k: (j, 0, 0))
        acc_shape = (wa, wb)

    def body(a_ref, b_ref, o_ref, acc):
        k = pl.program_id(1)
        prod = lax.dot_general(a_ref[...], b_ref[...], (((0,), (0,)), ((), ())), preferred_element_type=F32)

        @pl.when(k == 0)
        def _():
            acc[...] = prod

        @pl.when(k > 0)
        def _():
            acc[...] += prod

        @pl.when(k == nk - 1)
        def _():
            if split == "cols":
                for j in range(nblk):
                    o_ref[j] = acc[:, j * wb:(j + 1) * wb].astype(o_ref.dtype)
            else:
                o_ref[...] = acc[...].astype(o_ref.dtype)

    outs, exo = _grid_call(body, name, grid, in_specs=[a_spec, b_spec], out_specs=[o_spec],
                           out_shape=[jax.ShapeDtypeStruct((nblk, wa, wb), BF16)],
                           scratch_shapes=[pltpu.VMEM(acc_shape, F32)], args=(a, b), exchange=exchange)
    return outs[0], exo


def _rows128(a):
    return a.reshape(-1, LANES)


class _ReduceScatter:
    def __init__(self, gs, cidx, idx, tag):
        self.gs, self.cidx, self.idx, self.tag = gs, cidx, idx, tag

    def swap(self):
        return _swap_halves(self.gs)

    def after_swap(self, r1):
        self.ps = _pair_sums(self.gs, r1, self.cidx, name=f"rs_pair_{self.tag}")

    def chips(self):
        return _chip_exchange(self.ps)

    def after_chips(self, r2):
        self.fh = _chip_sums(self.ps, r2, self.idx, name=f"rs_sum_{self.tag}")

    def share(self):
        return _sibling_share(self.fh)

    @staticmethod
    def result(fs):
        return [f.reshape(f.shape[0] * f.shape[1], f.shape[2]) for f in fs]


def kernel(x, c, ada_w, ada_b, pre_mix_g, post_mix_g, w_in, conv_w, conv_b, conv_ln_g, conv_ln_b, pool_w, pool_scale, w_out, pre_ffn_g, post_ffn_g, ffn_up, ffn_conv_w, ffn_conv_b, ffn_down, loss_target, m_ada_w, m_ada_b, m_pre_mix_g, m_post_mix_g, m_w_in, m_conv_w, m_conv_b, m_conv_ln_g, m_conv_ln_b, m_pool_w, m_pool_scale, m_w_out, m_pre_ffn_g, m_post_ffn_g, m_ffn_up, m_ffn_conv_w, m_ffn_conv_b, m_ffn_down, v_ada_w, v_ada_b, v_pre_mix_g, v_post_mix_g, v_w_in, v_conv_w, v_conv_b, v_conv_ln_g, v_conv_ln_b, v_pool_w, v_pool_scale, v_w_out, v_pre_ffn_g, v_post_ffn_g, v_ffn_up, v_ffn_conv_w, v_ffn_conv_b, v_ffn_down):
    nb, s, d = x.shape
    nl = w_in.shape[0]
    taps = conv_w.shape[1]
    ccol = conv_w.shape[2]
    dc = conv_b.shape[1]
    fcol = ffn_conv_w.shape[2]
    f2 = ffn_conv_b.shape[1]
    nmod = ada_b.shape[1] // d
    acol = ada_w.shape[2]
    tm = min(MLP_TILE_ROWS, s)
    tm_mix = min(MIXER_TILE_ROWS, s)
    tt = min(GRAD_CHUNK_ROWS, (nb * s) // 2)

    xi, yi, ci = _pos()
    jm = 2 * xi + yi
    cidx = jnp.reshape(ci, (1,)).astype(jnp.int32)
    idx = jnp.stack([jm, ci]).astype(jnp.int32)

    win_b, wout_b, wup_b, wdn_b = (w.astype(BF16) for w in (w_in, w_out, ffn_up, ffn_down))

    def others(l):
        return [win_b[l], wout_b[l], wdn_b[l]]

    n_cw, n_fw, n_c = nl * taps * ccol, nl * 3 * fcol, nb * d
    packed = jnp.concatenate([conv_w.reshape(-1), ffn_conv_w.reshape(-1), c.reshape(-1)])
    got = _gather8(_rows128(packed), name="gather_small").reshape(N_DEV, -1)
    chips = got[0::2]
    cw_full = chips[:, :n_cw].reshape(N_CHIPS, nl, taps, ccol).transpose(1, 2, 0, 3).reshape(nl, taps, dc)
    fw_full = chips[:, n_cw:n_cw + n_fw].reshape(N_CHIPS, nl, 3, fcol).transpose(1, 2, 0, 3).reshape(nl, 3, f2)
    c_all = got[:, n_cw + n_fw:].reshape(N_DEV * nb, d)

    ada_b_cols = lax.dynamic_slice_in_dim(ada_b, jm * acol, acol, axis=1).reshape(nl, 1, acol)
    mod_cols, first_weights = _ada_forward(c_all, ada_w, ada_b_cols, name="ada_forward",
                                           exchange=_gather(others(0), mid_at=0.9))
    by_owner = mod_cols.reshape(nl, N_DEV, nb, acol).transpose(1, 0, 2, 3).reshape(N_DEV, -1, LANES)
    mod_own = _rows_to_owners(by_owner, name="mod_to_owners").reshape(N_CHIPS, nl, nb, acol)
    mod_own = mod_own.transpose(1, 2, 0, 3).reshape(nl, nb, nmod, d)
    mod_own = jnp.pad(mod_own, ((0, 0), (0, 0), (0, 8 - nmod), (0, 0)))

    vec_d = jnp.stack([pre_mix_g, post_mix_g, pre_ffn_g, post_ffn_g], axis=1)
    vec_c = jnp.stack([conv_b, conv_ln_g, conv_ln_b, pool_scale], axis=1)
    cw_pad = jnp.pad(cw_full, ((0, 0), (0, HALO - taps), (0, 0)))
    fw_rows = jnp.concatenate([fw_full, ffn_conv_b[:, None, :], jnp.zeros((nl, 4, f2), F32)], axis=1)
    pw_b = pool_w.astype(BF16)

    win_g, wout_g, wdn_g = _whole(first_weights)
    saved = []
    xs = x
    for l in range(nl):
        (x1, h1, u1, ac1, dp1, z1, o1), got = _mixer_forward(
            xs, mod_own[l], vec_d[l], vec_c[l], cw_pad[l], pw_b[l], win_g, wout_g, taps, tm_mix, name=f"mixer_fwd_{l}",
            exchange=_gather([wup_b[l]], mid_at=0.9))
        wup_g, = _whole(got)
        last = l + 1 == nl
        (x2, h2, u2, uc2, hid2, o2, *sq), nxt = _ffn_forward(
            x1, mod_own[l], vec_d[l], fw_rows[l], wup_g, wdn_g, tm, name=f"ffn_fwd_{l}",
            exchange=None if last else _gather(others(l + 1), mid_at=0.6), target=loss_target if last else None)
        saved.append((xs, h1, u1, ac1, dp1, z1, o1, x1, h2, u2, uc2, hid2, o2, win_g, wout_g, wup_g, wdn_g))
        if not last:
            win_g, wout_g, wdn_g = _whole(nxt)
        xs = x2

    dx = xs
    loss = lax.psum(0.5 * jnp.sum(sq[0]) / d, ("x", "y", "c"))

    flat = lambda a: a.reshape(nb * s, a.shape[-1])
    small = [None] * nl
    big_mlp, big_mix = [None] * nl, [None] * nl
    mlp = mix = None
    for l in reversed(range(nl)):
        x0, h1, u1, ac1, dp1, z1, o1, x1, h2, u2, uc2, hid2, o2, win_g, wout_g, wup_g, wdn_g = saved[l]
        (dx, du2, do2, rowd2, rowb2, dfw), got = _ffn_backward(
            dx, x1, o2, u2, uc2, mod_own[l], vec_d[l], fw_rows[l], wup_g, wdn_g, tm, name=f"ffn_bwd_{l}",
            exchange=_combine([mlp.chips(), mix.swap()]) if mlp else None)
        if mlp:
            mlp.after_chips(got[:2])
            mix.after_swap(got[2:])
        g_up, got = _weight_grad(flat(h2), flat(du2), N_CHIPS, "b", tt, name=f"grad_ffn_up_{l}",
                                 exchange=_combine([mlp.share(), mix.chips()]) if mlp else None)
        if mlp:
            big_mlp[l + 1] = mlp.result(got[:2])
            mix.after_chips(got[2:])
        g_dn, got = _weight_grad(flat(hid2), flat(do2), 2, "a", tt, name=f"grad_ffn_down_{l}",
                                 exchange=_swap_halves([g_up]) if l == 0 else None)
        g_dn = g_dn.reshape(N_CHIPS, -1, d)
        mlp_above, mlp = mlp, _ReduceScatter([g_up, g_dn], cidx, idx, f"mlp_{l}")
        if l == 0:
            mlp.after_swap(list(got) + list(_run_exchange(_swap_halves([g_dn]), name="rs_swap_down_0")))
        first = mlp.swap() if l > 0 else mlp.chips()
        (dx, du1, do1, rowd1, rowb1, rowc, dcw, dpw), got = _mixer_backward(
            dx, x0, o1, u1, ac1, dp1, mod_own[l], vec_d[l], vec_c[l], cw_pad[l], pw_b[l], win_g, wout_g, taps, tm_mix,
            name=f"mixer_bwd_{l}", exchange=_combine([first, mix.share()]) if mlp_above else first)
        if l > 0:
            mlp.after_swap(got[:2])
        else:
            mlp.after_chips(got[:2])
        if mlp_above:
            big_mix[l + 1] = mix.result(got[2:])
        g_in, got = _weight_grad(flat(h1), flat(du1), N_CHIPS, "cols", tt, name=f"grad_w_in_{l}",
                                 exchange=mlp.share() if l == 0 else None)
        if l == 0:
            big_mlp[0] = mlp.result(got)
        g_out, _ = _weight_grad(flat(z1), flat(do1), 1, "cols", tt, name=f"grad_w_out_{l}")
        mix = _ReduceScatter([g_in, g_out.reshape(N_CHIPS, -1, d)], cidx, idx, f"mix_{l}")
        small[l] = dict(rowd=rowd1 + rowd2, rowb=rowb1 + rowb2, rowc=rowc, dcw=dcw[:taps], dpw=dpw, dfw=dfw)
    mix.after_swap(_run_exchange(mix.swap(), name="rs_swap_mix_0"))

    dmod_own = jnp.stack([small[l]["rowb"][:, :nmod, :] for l in range(nl)])
    dmod_all, got = _gather8(_rows128(dmod_own), name="gather_dmod", exchange=mix.chips())
    mix.after_chips(got)
    big_mix[0] = mix.result(_run_exchange(mix.share(), name="rs_share_mix_0"))
    dmod_all = dmod_all.reshape(N_DEV, nl, nb, nmod * d)
    dmod_all = dmod_all.transpose(1, 0, 2, 3).reshape(nl, N_DEV * nb, nmod * d)
    dmod_cols = lax.dynamic_slice_in_dim(dmod_all, jm * acol, acol, axis=2)
    (g_ada_w, d_ada_w, nm_ada_w, nv_ada_w), _ = _ada_update(c_all, dmod_cols, ada_w, m_ada_w, v_ada_w, name="ada_update")

    def st(key, row=None):
        return jnp.stack([small[l][key] if row is None else small[l][key][row] for l in range(nl)])

    local = {
        "ada_b": dmod_own.sum(axis=1).reshape(nl, nmod * d),
        "pre_mix_g": st("rowd", 0), "post_mix_g": st("rowd", 1),
        "conv_b": st("rowc", 0), "conv_ln_g": st("rowc", 1), "conv_ln_b": st("rowc", 2),
        "pool_w": st("dpw"), "pool_scale": st("rowc", 3),
        "pre_ffn_g": st("rowd", 2), "post_ffn_g": st("rowd", 3),
        "ffn_conv_b": st("dfw", 3), "conv_w": st("dcw"), "ffn_conv_w": jnp.stack([small[l]["dfw"][:3] for l in range(nl)]),
    }
    names = list(local)
    sizes = [local[k].size for k in names]
    pad = -sum(sizes) % (4 * SUBLANES * LANES)
    packed = jnp.concatenate([local[k].reshape(-1) for k in names] + [jnp.zeros((pad,), F32)])
    summed = _allreduce8(_rows128(packed), name="allreduce_small").reshape(-1)
    grads, off = {}, 0
    for k, sz in zip(names, sizes):
        grads[k] = summed[off:off + sz].reshape(local[k].shape)
        off += sz
    grads["conv_w"] = lax.dynamic_slice_in_dim(grads["conv_w"], jm * ccol, ccol, axis=2)
    grads["ffn_conv_w"] = lax.dynamic_slice_in_dim(grads["ffn_conv_w"], jm * fcol, fcol, axis=2)

    params = dict(ada_b=(ada_b, m_ada_b, v_ada_b), pre_mix_g=(pre_mix_g, m_pre_mix_g, v_pre_mix_g),
                  post_mix_g=(post_mix_g, m_post_mix_g, v_post_mix_g), conv_b=(conv_b, m_conv_b, v_conv_b),
                  conv_ln_g=(conv_ln_g, m_conv_ln_g, v_conv_ln_g), conv_ln_b=(conv_ln_b, m_conv_ln_b, v_conv_ln_b),
                  pool_w=(pool_w, m_pool_w, v_pool_w), pool_scale=(pool_scale, m_pool_scale, v_pool_scale),
                  pre_ffn_g=(pre_ffn_g, m_pre_ffn_g, v_pre_ffn_g), post_ffn_g=(post_ffn_g, m_post_ffn_g, v_post_ffn_g),
                  ffn_conv_b=(ffn_conv_b, m_ffn_conv_b, v_ffn_conv_b), conv_w=(conv_w, m_conv_w, v_conv_w),
                  ffn_conv_w=(ffn_conv_w, m_ffn_conv_w, v_ffn_conv_w))
    pack = lambda i, g=None: _rows128(jnp.concatenate([(grads[k] if g else params[k][i]).reshape(-1) for k in names]))
    sd, sm, sv = _adamw_flat(pack(0), pack(0, True), pack(1), pack(2), name="adamw_small")
    outs = {}
    off = 0
    for k in names:
        shape, sz = params[k][0].shape, params[k][0].size
        outs[k] = (grads[k],) + tuple(a.reshape(-1)[off:off + sz].reshape(shape) for a in (sd, sm, sv))
        off += sz

    outs["ada_w"] = (g_ada_w, d_ada_w, nm_ada_w, nv_ada_w)
    for k, w, m, v, gs in [("w_in", w_in, m_w_in, v_w_in, [big_mix[l][0] for l in range(nl)]),
                           ("w_out", w_out, m_w_out, v_w_out, [big_mix[l][1] for l in range(nl)]),
                           ("ffn_up", ffn_up, m_ffn_up, v_ffn_up, [big_mlp[l][0] for l in range(nl)]),
                           ("ffn_down", ffn_down, m_ffn_down, v_ffn_down, [big_mlp[l][1] for l in range(nl)])]:
        outs[k] = tuple(_adamw_layers(w, m, v, gs, name=f"adamw_{k}"))

    order = ["ada_w", "ada_b", "pre_mix_g", "post_mix_g", "w_in", "conv_w", "conv_b", "conv_ln_g", "conv_ln_b", "pool_w",
             "pool_scale", "w_out", "pre_ffn_g", "post_ffn_g", "ffn_up", "ffn_conv_w", "ffn_conv_b", "ffn_down"]
    return (loss, dx) + tuple(outs[k][i] for i in range(4) for k in order)
```

```python
import functools

import jax
import jax.numpy as jnp
from jax import lax
from jax.experimental import pallas as pl
from jax.experimental.pallas import tpu as pltpu

F32 = jnp.float32
BF16 = jnp.bfloat16
MESH = pl.DeviceIdType.MESH

EPS = 1e-6
POOL_WINDOWS = (2, 4, 8, 16)
ADAM_LR = 0.001
ADAM_B1 = 0.9
ADAM_B2 = 0.999
ADAM_EPS = 1e-08
ADAM_WD = 0.01
ADAM_STEP = 10

N_CHIPS = 4
N_DEV = 8
LANES = 128
SUBLANES = 8
HALO = 32
FHALO = 8
VMEM_LIMIT = 60 * 1024 * 1024
MLP_TILE_ROWS = 256
MIXER_TILE_ROWS = 512
GRAD_CHUNK_ROWS = 2048


def _pos():
    return lax.axis_index("x"), lax.axis_index("y"), lax.axis_index("c")


def _flip(v, f):
    return 1 - v if f else v


def _full(shape):
    nd = len(shape)
    return pl.BlockSpec(shape, lambda *_: (0,) * nd)


_ANY = pl.BlockSpec(memory_space=pl.ANY)
_VMEM = pl.BlockSpec(memory_space=pltpu.VMEM)


def _sigmoid(v):
    return 1.0 / (1.0 + jnp.exp(-v))


def _colsum(v):
    return jnp.sum(v, axis=0, keepdims=True)


def _gather8(v, name, exchange=None):
    r, ccols = v.shape
    ex = exchange
    nci, nco = (len(ex.ins), len(ex.outs)) if ex else (0, 0)

    def body(*refs):
        v_ref, cin, out_ref, cout = refs[0], refs[1:1 + nci], refs[1 + nci], refs[2 + nci:2 + nci + nco]
        send_sems, recv_sems, local_sem = refs[2 + nci + nco:5 + nci + nco]
        if ex:
            sems = _Sems(*refs[5 + nci + nco:])
            ex.start(cin, cout, sems)
        x, y, c = _pos()
        me = 4 * x + 2 * y + c
        mine = pltpu.make_async_copy(v_ref, out_ref.at[me], local_sem)
        mine.start()
        peers = [(_flip(x, (k >> 2) & 1), _flip(y, (k >> 1) & 1), _flip(c, k & 1)) for k in range(1, N_DEV)]
        sends = []
        for k, peer in enumerate(peers):
            cp = pltpu.make_async_remote_copy(src_ref=v_ref, dst_ref=out_ref.at[me], send_sem=send_sems.at[k],
                                              recv_sem=recv_sems.at[k], device_id=peer, device_id_type=MESH)
            cp.start()
            sends.append(cp)
        for k, peer in enumerate(peers):
            pidx = 4 * peer[0] + 2 * peer[1] + peer[2]
            pltpu.make_async_remote_copy(src_ref=v_ref, dst_ref=out_ref.at[pidx], send_sem=send_sems.at[k],
                                         recv_sem=recv_sems.at[k], device_id=peer, device_id_type=MESH).wait_recv()
        for cp in sends:
            cp.wait_send()
        mine.wait()
        if ex:
            if ex.mid is not None:
                ex.mid(cin, cout, sems)
            ex.finish(cin, cout, sems)

    outs = pl.pallas_call(
        body, name=name, out_shape=[jax.ShapeDtypeStruct((N_DEV, r, ccols), v.dtype)] + (ex.outs if ex else []),
        in_specs=[_VMEM] + [_ANY] * nci, out_specs=[_VMEM] + [_ANY] * nco,
        scratch_shapes=[pltpu.SemaphoreType.DMA((N_DEV - 1,)), pltpu.SemaphoreType.DMA((N_DEV - 1,)),
                        pltpu.SemaphoreType.DMA(())] + (ex.scratch() if ex else []),
        input_output_aliases={1 + a: 1 + b for a, b in ex.aliases.items()} if ex else {},
        compiler_params=pltpu.CompilerParams(vmem_limit_bytes=VMEM_LIMIT),
    )(v, *(ex.ins if ex else []))
    return (outs[0], list(outs[1:])) if ex else outs[0]


def _rows_to_owners(v, name):
    _, r, ccols = v.shape

    def body(v_ref, out_ref, send_sems, recv_sems, local_sem):
        x, y, c = _pos()
        jm = 2 * x + y
        mine = pltpu.make_async_copy(v_ref.at[2 * jm + c], out_ref.at[jm], local_sem)
        mine.start()
        peers, pjs = _chip_peers(x, y, c)
        sends = []
        for k, peer in enumerate(peers):
            cp = pltpu.make_async_remote_copy(src_ref=v_ref.at[2 * pjs[k] + c], dst_ref=out_ref.at[jm],
                                              send_sem=send_sems.at[k], recv_sem=recv_sems.at[k],
                                              device_id=peer, device_id_type=MESH)
            cp.start()
            sends.append(cp)
        for k, peer in enumerate(peers):
            pltpu.make_async_remote_copy(src_ref=v_ref.at[0], dst_ref=out_ref.at[pjs[k]], send_sem=send_sems.at[k],
                                         recv_sem=recv_sems.at[k], device_id=peer, device_id_type=MESH).wait_recv()
        for cp in sends:
            cp.wait_send()
        mine.wait()

    return pl.pallas_call(
        body, name=name, out_shape=jax.ShapeDtypeStruct((N_CHIPS, r, ccols), v.dtype),
        in_specs=[_VMEM], out_specs=_VMEM,
        scratch_shapes=[pltpu.SemaphoreType.DMA((N_CHIPS - 1,)), pltpu.SemaphoreType.DMA((N_CHIPS - 1,)),
                        pltpu.SemaphoreType.DMA(())],
        compiler_params=pltpu.CompilerParams(vmem_limit_bytes=VMEM_LIMIT),
    )(v)


def _allreduce8(v, name):
    r, ccols = v.shape
    h = r // 2
    q = h // 2

    def body(v_ref, out_ref, whole, part, done, send_sems, recv_sems):
        x, y, c = _pos()
        sib = (x, y, 1 - c)
        mine = pl.ds(pl.multiple_of(c * h, SUBLANES), h)
        theirs = pl.ds(pl.multiple_of((1 - c) * h, SUBLANES), h)
        quarters = [pl.ds(pl.multiple_of(c * h + k * q, SUBLANES), q) for k in range(2)]
        along_x, along_y = (1 - x, y, c), (x, 1 - y, c)

        def exchange(pairs):
            cps = [pltpu.make_async_remote_copy(src_ref=src, dst_ref=dst, send_sem=send_sems.at[k], recv_sem=recv_sems.at[k],
                                                device_id=peer, device_id_type=MESH) for src, dst, k, peer in pairs]
            for cp in cps:
                cp.start()
            for cp in cps:
                cp.wait()

        exchange([(v_ref, whole, 0, sib)])
        out_ref[...] = v_ref[...] + whole[...]
        for stage, peers in enumerate(((along_x, along_y), (along_y, along_x))):
            exchange([(out_ref.at[quarters[k]], part.at[2 * stage + k], 1 + 2 * stage + k, peers[k]) for k in range(2)])
            for k in range(2):
                out_ref[quarters[k], :] = out_ref[quarters[k], :] + part[2 * stage + k]
        exchange([(out_ref.at[mine], done, 5, sib)])
        out_ref[theirs, :] = done[...]

    return pl.pallas_call(
        body, name=name, out_shape=jax.ShapeDtypeStruct((r, ccols), v.dtype),
        in_specs=[_VMEM], out_specs=_VMEM,
        scratch_shapes=[pltpu.VMEM((r, ccols), v.dtype), pltpu.VMEM((4, q, ccols), v.dtype), pltpu.VMEM((h, ccols), v.dtype),
                        pltpu.SemaphoreType.DMA((6,)), pltpu.SemaphoreType.DMA((6,))],
        compiler_params=pltpu.CompilerParams(vmem_limit_bytes=VMEM_LIMIT),
    )(v)


def _chip_peers(x, y, c):
    peers = [(_flip(x, (k >> 1) & 1), _flip(y, k & 1), c) for k in range(1, N_CHIPS)]
    return peers, [2 * p[0] + p[1] for p in peers]


class _Exchange:
    def __init__(self, ins, outs, aliases, n_sems, n_local, start, finish, mid=None, mid_at=1.0, sibling=False, chips=False):
        self.ins, self.outs, self.aliases = list(ins), list(outs), dict(aliases)
        self.n_sems, self.n_local, self.start, self.finish = n_sems, n_local, start, finish
        self.mid, self.mid_at = mid, mid_at
        self.sibling, self.chips = sibling, chips

    def collective_id(self):
        return {(True, False): 1, (False, True): 2, (True, True): 3}[(self.sibling, self.chips)]

    def handshake(self):
        x, y, c = _pos()
        peers = ([(x, y, 1 - c)] if self.sibling else []) + (_chip_peers(x, y, c)[0] if self.chips else [])
        barrier = pltpu.get_barrier_semaphore()
        for peer in peers:
            pl.semaphore_signal(barrier, inc=1, device_id=peer, device_id_type=MESH)
        pl.semaphore_wait(barrier, len(peers))

    def scratch(self):
        return [pltpu.SemaphoreType.DMA((self.n_sems,)), pltpu.SemaphoreType.DMA((self.n_sems,)),
                pltpu.SemaphoreType.DMA((max(self.n_local, 1),))]


class _Sems:
    def __init__(self, send, recv, local, base=0, lbase=0):
        self.send, self.recv, self.loc, self.base, self.lbase = send, recv, local, base, lbase

    def shifted(self, by, lby):
        return _Sems(self.send, self.recv, self.loc, self.base + by, self.lbase + lby)

    def local(self, k):
        return self.loc.at[self.lbase + k]


def _remote(src, dst, sems, k, peer):
    return pltpu.make_async_remote_copy(src_ref=src, dst_ref=dst, send_sem=sems.send.at[sems.base + k],
                                        recv_sem=sems.recv.at[sems.base + k], device_id=peer, device_id_type=MESH)


def _combine(exs):
    ins = [a for ex in exs for a in ex.ins]
    outs = [o for ex in exs for o in ex.outs]
    aliases, spans, ni, no, ns, nloc = {}, [], 0, 0, 0, 0
    for ex in exs:
        aliases.update({ni + a: no + b for a, b in ex.aliases.items()})
        spans.append((ni, no, ns, nloc))
        ni, no, ns, nloc = ni + len(ex.ins), no + len(ex.outs), ns + ex.n_sems, nloc + ex.n_local

    def each(which):
        def run(ins_, outs_, sems):
            for ex, (i0, o0, s0, l0) in zip(exs, spans):
                stage = getattr(ex, which)
                if stage is not None:
                    stage(ins_[i0:i0 + len(ex.ins)], outs_[o0:o0 + len(ex.outs)], sems.shifted(s0, l0))
        return run

    mids = [ex.mid_at for ex in exs if ex.mid is not None]
    return _Exchange(ins, outs, aliases, ns, nloc, each("start"), each("finish"),
                     mid=each("mid") if mids else None, mid_at=max(mids) if mids else 1.0,
                     sibling=any(ex.sibling for ex in exs), chips=any(ex.chips for ex in exs))


def _gather(shards, mid_at=1.0):
    n = len(shards)
    per = N_CHIPS - 1
    halves = [s.reshape(2, s.shape[0] // 2, s.shape[1]) for s in shards]

    def copies(ins, outs, sems):
        x, y, c = _pos()
        jm = 2 * x + y
        sib = (x, y, 1 - c)
        peers, pjs = _chip_peers(x, y, c)
        sends, recvs, passes, passed = [], [], [], []
        for a in range(n):
            own = _remote(ins[a], outs[a].at[jm], sems, 2 * n * per + a, sib)
            sends.append(own)
            passed.append(own)
            for k, peer in enumerate(peers):
                landed, theirs = outs[a].at[pjs[k], c], outs[a].at[pjs[k], 1 - c]
                sends.append(_remote(ins[a].at[c], outs[a].at[jm, c], sems, 2 * (a * per + k), peer))
                recvs.append(_remote(landed, landed, sems, 2 * (a * per + k), peer))
                passes.append(_remote(landed, landed, sems, 2 * (a * per + k) + 1, sib))
                passed.append(_remote(theirs, theirs, sems, 2 * (a * per + k) + 1, sib))
        return sends, recvs, passes, passed

    def start(ins, outs, sems):
        for cp in copies(ins, outs, sems)[0]:
            cp.start()

    def mid(ins, outs, sems):
        _, recvs, passes, _ = copies(ins, outs, sems)
        for got, fwd in zip(recvs, passes):
            got.wait_recv()
            fwd.start()

    def finish(ins, outs, sems):
        sends, _, passes, passed = copies(ins, outs, sems)
        for cp in passed:
            cp.wait_recv()
        for cp in sends + passes:
            cp.wait_send()

    outs = [jax.ShapeDtypeStruct((N_CHIPS,) + h.shape, h.dtype) for h in halves]
    return _Exchange(halves, outs, {}, 2 * n * per + n, 0, start, finish, mid=mid, mid_at=mid_at, sibling=True, chips=True)


def _whole(gathered):
    return [g.reshape(g.shape[0], g.shape[1] * g.shape[2], g.shape[3]) for g in gathered]


def _swap_halves(gs):
    n = len(gs)
    halves = [g.reshape(g.shape[0], 2, g.shape[1] // 2, g.shape[2]) for g in gs]

    def copies(ins, outs, sems):
        x, y, c = _pos()
        sib = (x, y, 1 - c)
        return [_remote(ins[a].at[:, 1 - c], outs[a], sems, a, sib) for a in range(n)]

    def start(ins, outs, sems):
        for cp in copies(ins, outs, sems):
            cp.start()

    def finish(ins, outs, sems):
        for cp in copies(ins, outs, sems):
            cp.wait()

    outs = [jax.ShapeDtypeStruct((g.shape[0], g.shape[1] // 2, g.shape[2]), g.dtype) for g in gs]
    return _Exchange(halves, outs, {}, n, 0, start, finish, sibling=True)


def _chip_exchange(ps):
    n = len(ps)
    per = N_CHIPS - 1

    def copies(ins, outs, sems):
        x, y, c = _pos()
        peers, pjs = _chip_peers(x, y, c)
        return [_remote(ins[a].at[pjs[k]], outs[a].at[k], sems, a * per + k, peer)
                for a in range(n) for k, peer in enumerate(peers)]

    def start(ins, outs, sems):
        for cp in copies(ins, outs, sems):
            cp.start()

    def finish(ins, outs, sems):
        for cp in copies(ins, outs, sems):
            cp.wait()

    outs = [jax.ShapeDtypeStruct((per,) + p.shape[1:], p.dtype) for p in ps]
    return _Exchange(ps, outs, {}, n * per, 0, start, finish, chips=True)


def _sibling_share(fs):
    n = len(fs)

    def copies(outs, sems):
        x, y, c = _pos()
        sib = (x, y, 1 - c)
        sends = [_remote(outs[a].at[c], outs[a].at[c], sems, a, sib) for a in range(n)]
        recvs = [_remote(outs[a].at[1 - c], outs[a].at[1 - c], sems, a, sib) for a in range(n)]
        return sends, recvs

    def start(ins, outs, sems):
        for cp in copies(outs, sems)[0]:
            cp.start()

    def finish(ins, outs, sems):
        sends, recvs = copies(outs, sems)
        for cp in recvs:
            cp.wait_recv()
        for cp in sends:
            cp.wait_send()

    outs = [jax.ShapeDtypeStruct(f.shape, f.dtype) for f in fs]
    return _Exchange(fs, outs, {a: a for a in range(n)}, n, 0, start, finish, sibling=True)


def _run_exchange(ex, name):
    ni, no = len(ex.ins), len(ex.outs)

    def body(*refs):
        ins, outs, sems = refs[:ni], refs[ni:ni + no], _Sems(*refs[ni + no:])
        ex.handshake()
        ex.start(ins, outs, sems)
        if ex.mid is not None:
            ex.mid(ins, outs, sems)
        ex.finish(ins, outs, sems)

    return pl.pallas_call(
        body, name=name, out_shape=ex.outs, in_specs=[_ANY] * ni, out_specs=[_ANY] * no,
        input_output_aliases=ex.aliases, scratch_shapes=ex.scratch(),
        compiler_params=pltpu.CompilerParams(collective_id=ex.collective_id()),
    )(*ex.ins)


def _grid_call(body, name, grid, in_specs, out_specs, out_shape, scratch_shapes, args, exchange=None):
    ni, no = len(in_specs), len(out_specs)
    params = pltpu.CompilerParams(dimension_semantics=("arbitrary",) * len(grid), vmem_limit_bytes=VMEM_LIMIT)
    if exchange is None:
        outs = pl.pallas_call(body, name=name, grid=grid, in_specs=in_specs, out_specs=out_specs, out_shape=out_shape,
                              scratch_shapes=scratch_shapes, compiler_params=params)(*args)
        return list(outs), []
    ex = exchange
    nci, nco = len(ex.ins), len(ex.outs)

    def hosted(*refs):
        cin = refs[ni:ni + nci]
        cout = refs[ni + nci + no:ni + nci + no + nco]
        sems = _Sems(*refs[len(refs) - 3:])
        main = refs[:ni] + refs[ni + nci:ni + nci + no] + refs[ni + nci + no + nco:len(refs) - 3]
        ids = [pl.program_id(a) for a in range(len(grid))]
        first = functools.reduce(lambda p, q: p & q, [i == 0 for i in ids])
        last = functools.reduce(lambda p, q: p & q, [i == g - 1 for i, g in zip(ids, grid)])

        @pl.when(first)
        def _():
            ex.handshake()
            ex.start(cin, cout, sems)

        if ex.mid is not None:
            steps = functools.reduce(lambda p, q: p * q, grid)
            flat = functools.reduce(lambda p, q: p * q[1] + q[0], zip(ids[1:], grid[1:]), ids[0])

            @pl.when(flat == min(steps - 1, int(ex.mid_at * steps)))
            def _():
                ex.mid(cin, cout, sems)

        body(*main)

        @pl.when(last)
        def _():
            ex.finish(cin, cout, sems)

    outs = pl.pallas_call(
        hosted, name=name, grid=grid, in_specs=list(in_specs) + [_ANY] * nci, out_specs=list(out_specs) + [_ANY] * nco,
        out_shape=list(out_shape) + ex.outs, scratch_shapes=list(scratch_shapes) + ex.scratch(),
        input_output_aliases={ni + a: no + b for a, b in ex.aliases.items()},
        compiler_params=pltpu.CompilerParams(dimension_semantics=("arbitrary",) * len(grid), vmem_limit_bytes=VMEM_LIMIT,
                                             collective_id=ex.collective_id()),
    )(*args, *ex.ins)
    return list(outs[:no]), list(outs[no:])


SUM_BLOCK_BYTES = 4 * 1024 * 1024
ADAM_BLOCK_BYTES = 2 * 1024 * 1024


def _row_tile(rows, cols, itemsize, budget):
    best = None
    for t in range(16, rows + 1, 16):
        if rows % t == 0 and t * cols * itemsize <= budget:
            best = t
    return best if best is not None else rows


def _turns(step_counts):
    offs, total = [], 0
    for n in step_counts:
        offs.append(total)
        total += n
    own = [lambda s, o=o, n=n: jnp.clip(s - o, 0, n - 1) for o, n in zip(offs, step_counts)]
    mine = [lambda s, o=o, n=n: (s >= o) & (s < o + n) for o, n in zip(offs, step_counts)]
    return total, own, mine


def _pair_sums(gs, r1s, cidx, name):
    n = len(gs)
    tiles = [_row_tile(g.shape[1] // 2, g.shape[2], 4, SUM_BLOCK_BYTES) for g in gs]
    nts = [g.shape[1] // 2 // tr for g, tr in zip(gs, tiles)]
    total, own, mine = _turns([g.shape[0] * nt for g, nt in zip(gs, nts)])

    def body(c_ref, *refs):
        s = pl.program_id(0)
        for a in range(n):
            g_ref, r_ref, o_ref = refs[2 * a], refs[2 * a + 1], refs[2 * n + a]

            @pl.when(mine[a](s))
            def _(g_ref=g_ref, r_ref=r_ref, o_ref=o_ref):
                o_ref[...] = (g_ref[...].astype(F32) + r_ref[...].astype(F32)).astype(o_ref.dtype)

    in_specs, out_specs = [], []
    for a, (g, tr, nt) in enumerate(zip(gs, tiles, nts)):
        blk = (None, tr, g.shape[2])
        in_specs += [pl.BlockSpec(blk, lambda s, c_ref, a=a, nt=nt: (own[a](s) // nt, c_ref[0] * nt + own[a](s) % nt, 0)),
                     pl.BlockSpec(blk, lambda s, c_ref, a=a, nt=nt: (own[a](s) // nt, own[a](s) % nt, 0))]
        out_specs.append(pl.BlockSpec(blk, lambda s, c_ref, a=a, nt=nt: (own[a](s) // nt, own[a](s) % nt, 0)))
    return pl.pallas_call(
        body, name=name, out_shape=[jax.ShapeDtypeStruct((g.shape[0], g.shape[1] // 2, g.shape[2]), g.dtype) for g in gs],
        grid_spec=pltpu.PrefetchScalarGridSpec(num_scalar_prefetch=1, grid=(total,), in_specs=in_specs, out_specs=out_specs),
        compiler_params=pltpu.CompilerParams(dimension_semantics=("arbitrary",), vmem_limit_bytes=VMEM_LIMIT),
    )(cidx, *[x for pair in zip(gs, r1s) for x in pair])


def _chip_sums(ps, r2s, idx, name):
    n = len(ps)
    tiles = [_row_tile(p.shape[1], p.shape[2], 4, SUM_BLOCK_BYTES) for p in ps]
    total, own, mine = _turns([p.shape[1] // tr for p, tr in zip(ps, tiles)])

    def body(i_ref, *refs):
        s = pl.program_id(0)
        for a in range(n):
            p_ref, r_ref, o_ref = refs[2 * a], refs[2 * a + 1], refs[2 * n + a]

            @pl.when(mine[a](s))
            def _(p_ref=p_ref, r_ref=r_ref, o_ref=o_ref):
                acc = p_ref[...].astype(F32)
                for k in range(N_CHIPS - 1):
                    acc = acc + r_ref[k].astype(F32)
                o_ref[...] = acc

    in_specs, out_specs = [], []
    for a, (p, tr) in enumerate(zip(ps, tiles)):
        ccols = p.shape[2]
        in_specs += [pl.BlockSpec((None, tr, ccols), lambda s, i_ref, a=a: (i_ref[0], own[a](s), 0)),
                     pl.BlockSpec((N_CHIPS - 1, tr, ccols), lambda s, i_ref, a=a: (0, own[a](s), 0))]
        out_specs.append(pl.BlockSpec((None, tr, ccols), lambda s, i_ref, a=a: (i_ref[1], own[a](s), 0)))
    return pl.pallas_call(
        body, name=name, out_shape=[jax.ShapeDtypeStruct((2, p.shape[1], p.shape[2]), F32) for p in ps],
        grid_spec=pltpu.PrefetchScalarGridSpec(num_scalar_prefetch=1, grid=(total,), in_specs=in_specs, out_specs=out_specs),
        compiler_params=pltpu.CompilerParams(dimension_semantics=("arbitrary",), vmem_limit_bytes=VMEM_LIMIT),
    )(idx, *[x for pair in zip(ps, r2s) for x in pair])


def _adam_math(w, g, m, v):
    m2 = ADAM_B1 * m + (1.0 - ADAM_B1) * g
    v2 = ADAM_B2 * v + (1.0 - ADAM_B2) * (g * g)
    m_hat = m2 / (1.0 - ADAM_B1 ** ADAM_STEP)
    v_hat = v2 / (1.0 - ADAM_B2 ** ADAM_STEP)
    delta = -ADAM_LR * (m_hat / (jnp.sqrt(v_hat) + ADAM_EPS) + ADAM_WD * w)
    return delta, m2, v2


def _adamw_layers(w, m, v, gs, name):
    nl, r, ccols = w.shape
    ng = len(gs)
    tr = _row_tile(r, ccols, 4, ADAM_BLOCK_BYTES)
    nt = r // tr

    def body(w_ref, m_ref, v_ref, *rest):
        g_refs, (go_ref, d_ref, mo_ref, vo_ref) = rest[:ng], rest[ng:]
        l = pl.program_id(0)
        g = g_refs[0][...]
        for k in range(1, ng):
            g = jnp.where(l == k, g_refs[k][...], g)
        delta, m2, v2 = _adam_math(w_ref[...], g, m_ref[...], v_ref[...])
        go_ref[...] = g
        d_ref[...] = delta
        mo_ref[...] = m2
        vo_ref[...] = v2

    big = pl.BlockSpec((None, tr, ccols), lambda l, i: (l, i, 0))

    def gspec(k):
        return pl.BlockSpec((tr, ccols), lambda l, i: (jnp.where(l == k, i, jnp.where(l < k, 0, nt - 1)), 0))

    assert ng == nl
    return _grid_call(body, name, (nl, nt), in_specs=[big, big, big] + [gspec(k) for k in range(ng)],
                      out_specs=[big, big, big, big], out_shape=[jax.ShapeDtypeStruct(w.shape, F32)] * 4,
                      scratch_shapes=[], args=(w, m, v, *gs))[0]


def _adamw_flat(w, g, m, v, name):
    r, ccols = w.shape

    def body(w_ref, g_ref, m_ref, v_ref, d_ref, mo_ref, vo_ref):
        delta, m2, v2 = _adam_math(w_ref[...], g_ref[...], m_ref[...], v_ref[...])
        d_ref[...] = delta
        mo_ref[...] = m2
        vo_ref[...] = v2

    return pl.pallas_call(
        body, name=name, out_shape=[jax.ShapeDtypeStruct((r, ccols), F32)] * 3,
        in_specs=[_VMEM] * 4, out_specs=[_VMEM] * 3,
        compiler_params=pltpu.CompilerParams(vmem_limit_bytes=VMEM_LIMIT),
    )(w, g, m, v)


def _ada_forward(c_all, ada_w, ada_b_cols, name, exchange=None):
    nl, d, ncols = ada_w.shape
    bg = c_all.shape[0]
    tn = 512 if ncols % 512 == 0 else ncols

    def body(c_ref, w_ref, b_ref, o_ref):
        cv = c_ref[...]
        ca = (cv * _sigmoid(cv)).astype(BF16)
        o_ref[...] = jnp.dot(ca, w_ref[...].astype(BF16), preferred_element_type=F32) + b_ref[...]

    outs, got = _grid_call(
        body, name, (nl, ncols // tn),
        in_specs=[pl.BlockSpec((bg, d), lambda l, j: (0, 0)),
                  pl.BlockSpec((None, d, tn), lambda l, j: (l, 0, j)),
                  pl.BlockSpec((None, 1, tn), lambda l, j: (l, 0, j))],
        out_specs=[pl.BlockSpec((None, bg, tn), lambda l, j: (l, 0, j))],
        out_shape=[jax.ShapeDtypeStruct((nl, bg, ncols), F32)], scratch_shapes=[], args=(c_all, ada_w, ada_b_cols),
        exchange=exchange)
    return outs[0], got


def _ada_update(c_all, dmod_cols, w, m, v, name, exchange=None):
    nl, d, ncols = w.shape
    bg = c_all.shape[0]
    tn = 512 if ncols % 512 == 0 else ncols

    def body(c_ref, dm_ref, w_ref, m_ref, v_ref, go_ref, d_ref, mo_ref, vo_ref):
        cv = c_ref[...]
        ca = (cv * _sigmoid(cv)).astype(BF16)
        g = lax.dot_general(ca, dm_ref[...].astype(BF16), (((0,), (0,)), ((), ())), preferred_element_type=F32)
        delta, m2, v2 = _adam_math(w_ref[...], g, m_ref[...], v_ref[...])
        go_ref[...] = g
        d_ref[...] = delta
        mo_ref[...] = m2
        vo_ref[...] = v2

    big = pl.BlockSpec((None, d, tn), lambda l, j: (l, 0, j))
    return _grid_call(
        body, name, (nl, ncols // tn),
        in_specs=[pl.BlockSpec((bg, d), lambda l, j: (0, 0)),
                  pl.BlockSpec((None, bg, tn), lambda l, j: (l, 0, j)), big, big, big],
        out_specs=[big, big, big, big], out_shape=[jax.ShapeDtypeStruct(w.shape, F32)] * 4,
        scratch_shapes=[], args=(c_all, dmod_cols, w, m, v), exchange=exchange)


def _load_weights(first, pairs, sems):
    @pl.when(first)
    def _():
        cps = [pltpu.make_async_copy(src, dst, sems.at[k]) for k, (src, dst) in enumerate(pairs)]
        for cp in cps:
            cp.start()
        for cp in cps:
            cp.wait()


def _ada_norm(xv, g, sc, sh):
    r = lax.rsqrt(jnp.mean(xv * xv, axis=-1, keepdims=True) + EPS)
    xn = xv * r
    return (xn * g) * (1.0 + sc) + sh, xn, r


def _ada_norm_bwd(dh, xn, r, g, sc):
    d_sh = _colsum(dh)
    d_sc = _colsum(dh * (xn * g))
    dxg = dh * (1.0 + sc)
    d_g = _colsum(dxg * xn)
    gd = dxg * g
    dx = r * (gd - xn * jnp.mean(gd * xn, axis=-1, keepdims=True))
    return dx, d_sh, d_sc, d_g


def _gated_residual_bwd(dxo, o, g_post, gt):
    r = lax.rsqrt(jnp.mean(o * o, axis=-1, keepdims=True) + EPS)
    on = o * r
    d_gt = _colsum(dxo * (on * g_post))
    dy = dxo * (1.0 + gt)
    d_gp = _colsum(dy * on)
    gd = dy * g_post
    do = r * (gd - on * jnp.mean(gd * on, axis=-1, keepdims=True))
    return do, d_gt, d_gp


def _seq_positions(i, tm, width):
    return i * tm + lax.broadcasted_iota(jnp.int32, (tm, width), 0)


def _fill_phases(ext, phases):
    rows = ext.shape[0]
    ev = ext[...]
    for r in range(1, SUBLANES):
        phases[r - 1] = pltpu.roll(ev, rows - r, axis=0)


def _shifted_rows(ext, phases, offset, n):
    q, r = divmod(offset, SUBLANES)
    if r == 0:
        return ext[pl.ds(q * SUBLANES, n), :]
    return phases[r - 1, pl.ds(q * SUBLANES, n), :]


def _rows_before(halo, cur, shift):
    e = jnp.concatenate([halo, cur], axis=0)
    return pltpu.roll(e, shift, axis=0)[halo.shape[0]:, :]


def _rows_after(cur, halo, shift):
    e = jnp.concatenate([cur, halo], axis=0)
    return pltpu.roll(e, e.shape[0] - shift, axis=0)[:cur.shape[0], :]


def _mixer_forward(x, mod, vec_d, vec_c, cw, pw, win_g, wout_g, taps, tm, name, exchange=None):
    nb, s, d = x.shape
    n = s // tm
    nj, _, dcol = win_g.shape
    din = nj * dcol
    dc = vec_c.shape[-1]
    dpool = din - 2 * dc
    dmix = dc + dpool
    ro = wout_g.shape[1]
    ngrp = dpool // LANES

    def body(x_ref, mod_ref, vd_ref, vc_ref, cw_ref, pw_ref, win_hbm, wout_hbm,
             xo_ref, h_ref, u_ref, ac_ref, dp_ref, z_ref, o_ref,
             win_v, wout_v, ext_a, ext_p, phases, sems):
        b, i = pl.program_id(0), pl.program_id(1)
        pairs = [(win_hbm.at[j], win_v.at[:, pl.ds(j * dcol, dcol)]) for j in range(nj)]
        pairs += [(wout_hbm.at[j], wout_v.at[pl.ds(j * ro, ro), :]) for j in range(nj)]
        _load_weights((b == 0) & (i == 0), pairs, sems)

        xv = x_ref[...]
        h, _, _ = _ada_norm(xv, vd_ref[0:1, :], mod_ref[1:2, :], mod_ref[0:1, :])
        hb = h.astype(BF16)
        h_ref[...] = hb
        u = jnp.dot(hb, win_v[...], preferred_element_type=F32)
        u_ref[...] = u.astype(BF16)
        ag = u[:, :dc] * _sigmoid(u[:, dc:2 * dc])
        up = u[:, 2 * dc:]

        @pl.when(i == 0)
        def _():
            ext_a[0:HALO, :] = jnp.zeros((HALO, dc), F32)
            ext_p[0:HALO, :] = jnp.zeros((HALO, dpool), F32)

        @pl.when(i > 0)
        def _():
            ext_a[0:HALO, :] = ext_a[tm:tm + HALO, :]
            ext_p[0:HALO, :] = ext_p[tm:tm + HALO, :]

        ext_a[HALO:HALO + tm, :] = ag
        ext_p[HALO:HALO + tm, :] = up

        acc = jnp.broadcast_to(vc_ref[0:1, :], (tm, dc))
        _fill_phases(ext_a, phases)
        for k in range(taps):
            acc = acc + cw_ref[k:k + 1, :] * _shifted_rows(ext_a, phases, HALO - (taps - 1) + k, tm)
        ac_ref[...] = acc.astype(BF16)
        mu = jnp.mean(acc, axis=-1, keepdims=True)
        xc = acc - mu
        var = jnp.mean(xc * xc, axis=-1, keepdims=True)
        al = (xc * lax.rsqrt(var + EPS)) * vc_ref[1:2, :] + vc_ref[2:3, :]
        a = al * _sigmoid(al)

        pos = _seq_positions(i, tm, LANES)
        parts = [a.astype(BF16)]
        for g in range(ngrp):
            w = POOL_WINDOWS[g]
            cols = slice(g * LANES, (g + 1) * LANES)
            sw = ext_p[:, cols]
            step = 1
            while step < w:
                sw = sw + pltpu.roll(sw, step, axis=0)
                step *= 2
            cnt = jnp.minimum(pos + 1, w).astype(F32)
            dg = (sw[HALO:, :] / cnt - up[:, cols]).astype(BF16)
            dp_ref[:, cols] = dg
            q = jnp.dot(dg, pw_ref[g], preferred_element_type=F32)
            parts.append((q * vc_ref[3:4, cols]).astype(BF16))
        z = jnp.concatenate(parts, axis=-1)
        z_ref[...] = z
        o = jnp.dot(z, wout_v[...], preferred_element_type=F32)
        o_ref[...] = o
        r2 = lax.rsqrt(jnp.mean(o * o, axis=-1, keepdims=True) + EPS)
        xo_ref[...] = xv + (1.0 + mod_ref[2:3, :]) * ((o * r2) * vd_ref[1:2, :])

    def tile(width):
        return pl.BlockSpec((None, tm, width), lambda b, i: (b, i, 0))

    return _grid_call(
        body, name, (nb, n),
        in_specs=[tile(d), pl.BlockSpec((None, 8, d), lambda b, i: (b, 0, 0)), _full(vec_d.shape), _full(vec_c.shape),
                  _full(cw.shape), _full(pw.shape), _ANY, _ANY],
        out_specs=[tile(d), tile(d), tile(din), tile(dc), tile(dpool), tile(dmix), tile(d)],
        out_shape=[jax.ShapeDtypeStruct((nb, s, d), F32), jax.ShapeDtypeStruct((nb, s, d), BF16),
                   jax.ShapeDtypeStruct((nb, s, din), BF16), jax.ShapeDtypeStruct((nb, s, dc), BF16),
                   jax.ShapeDtypeStruct((nb, s, dpool), BF16), jax.ShapeDtypeStruct((nb, s, dmix), BF16),
                   jax.ShapeDtypeStruct((nb, s, d), F32)],
        scratch_shapes=[pltpu.VMEM((d, din), BF16), pltpu.VMEM((dmix, d), BF16),
                        pltpu.VMEM((HALO + tm, dc), F32), pltpu.VMEM((HALO + tm, dpool), F32),
                        pltpu.VMEM((SUBLANES - 1, HALO + tm, dc), F32), pltpu.SemaphoreType.DMA((2 * nj,))],
        args=(x, mod, vec_d, vec_c, cw, pw, win_g, wout_g), exchange=exchange)


def _mixer_backward(dxo, x, o, u, ac, dpl, mod, vec_d, vec_c, cw, pw, win_g, wout_g, taps, tm, name, exchange=None):
    nb, s, d = x.shape
    n = s // tm
    nj, _, dcol = win_g.shape
    din = nj * dcol
    dc = vec_c.shape[-1]
    dpool = din - 2 * dc
    dmix = dc + dpool
    ro = wout_g.shape[1]
    ngrp = dpool // LANES
    rext = tm + HALO

    def body(dxo_ref, x_ref, o_ref, u_ref, ac_ref, dp_ref, mod_ref, vd_ref, vc_ref, cw_ref, pw_ref, win_hbm, wout_hbm,
             dx_ref, du_ref, dob_ref, rowd_ref, rowb_ref, rowc_ref, dcw_ref, dpw_ref,
             win_v, wout_v, ext_a, ext_p, phases, sems):
        b, i = pl.program_id(0), pl.program_id(1)
        first = (b == 0) & (i == 0)
        pairs = [(win_hbm.at[j], win_v.at[:, pl.ds(j * dcol, dcol)]) for j in range(nj)]
        pairs += [(wout_hbm.at[j], wout_v.at[pl.ds(j * ro, ro), :]) for j in range(nj)]
        _load_weights(first, pairs, sems)

        @pl.when(first)
        def _():
            rowd_ref[...] = jnp.zeros_like(rowd_ref)
            rowc_ref[...] = jnp.zeros_like(rowc_ref)
            dcw_ref[...] = jnp.zeros_like(dcw_ref)
            dpw_ref[...] = jnp.zeros_like(dpw_ref)

        @pl.when(i == 0)
        def _():
            rowb_ref[...] = jnp.zeros_like(rowb_ref)
            ext_a[tm:rext, :] = jnp.zeros((HALO, dc), F32)
            ext_p[tm:rext, :] = jnp.zeros((HALO, dpool), F32)

        @pl.when(i > 0)
        def _():
            ext_a[tm:rext, :] = ext_a[0:HALO, :]
            ext_p[tm:rext, :] = ext_p[0:HALO, :]

        g_pre, g_post = vd_ref[0:1, :], vd_ref[1:2, :]
        sh, sc, gt = mod_ref[0:1, :], mod_ref[1:2, :], mod_ref[2:3, :]
        do, d_gt, d_gp = _gated_residual_bwd(dxo_ref[...], o_ref[...], g_post, gt)
        dob = do.astype(BF16)
        dob_ref[...] = dob
        dz = lax.dot_general(dob, wout_v[...], (((1,), (1,)), ((), ())), preferred_element_type=F32)

        acv = ac_ref[...].astype(F32)
        mu = jnp.mean(acv, axis=-1, keepdims=True)
        xc = acv - mu
        rstd = lax.rsqrt(jnp.mean(xc * xc, axis=-1, keepdims=True) + EPS)
        an = xc * rstd
        lg = vc_ref[1:2, :]
        al = an * lg + vc_ref[2:3, :]
        sg = _sigmoid(al)
        dal = dz[:, :dc] * (sg * (1.0 + al * (1.0 - sg)))
        d_lg = _colsum(dal * an)
        d_lb = _colsum(dal)
        dan = dal * lg
        dac = rstd * (dan - jnp.mean(dan, axis=-1, keepdims=True) - an * jnp.mean(dan * an, axis=-1, keepdims=True))
        d_cb = _colsum(dac)
        ext_a[0:tm, :] = dac
        uv = u_ref[:, 0:dc].astype(F32)
        sgu = _sigmoid(u_ref[:, dc:2 * dc].astype(F32))
        ag = uv * sgu
        dag = jnp.zeros((tm, dc), F32)
        _fill_phases(ext_a, phases)
        for k in range(taps):
            sl = _shifted_rows(ext_a, phases, taps - 1 - k, tm)
            dag = dag + cw_ref[k:k + 1, :] * sl
            dcw_ref[k:k + 1, :] += _colsum(ag * sl)
        du_ref[:, 0:dc] = (dag * sgu).astype(BF16)
        du_ref[:, dc:2 * dc] = (dag * uv * (sgu * (1.0 - sgu))).astype(BF16)

        pos = _seq_positions(n - 1 - i, tm, LANES)
        d_ps = []
        for g in range(ngrp):
            w = POOL_WINDOWS[g]
            cols = slice(g * LANES, (g + 1) * LANES)
            gcols = slice(dc + g * LANES, dc + (g + 1) * LANES)
            dgb = dp_ref[:, cols]
            q = jnp.dot(dgb, pw_ref[g], preferred_element_type=F32)
            dpg = dz[:, gcols]
            d_ps.append(_colsum(dpg * q))
            dq = (dpg * vc_ref[3:4, cols]).astype(BF16)
            dpw_ref[g] += lax.dot_general(dgb, dq, (((0,), (0,)), ((), ())), preferred_element_type=F32)
            dd = lax.dot_general(dq, pw_ref[g], (((1,), (1,)), ((), ())), preferred_element_type=F32)
            cnt = jnp.minimum(pos + 1, w).astype(F32)
            ext_p[0:tm, cols] = dd / cnt
            sw = ext_p[:, cols]
            step = 1
            while step < w:
                sw = sw + pltpu.roll(sw, rext - step, axis=0)
                step *= 2
            du_ref[:, 2 * dc + g * LANES:2 * dc + (g + 1) * LANES] = (sw[0:tm, :] - dd).astype(BF16)
        rowc_ref[0:1, :] += d_cb
        rowc_ref[1:2, :] += d_lg
        rowc_ref[2:3, :] += d_lb
        rowc_ref[3:4, :] += jnp.concatenate(d_ps, axis=-1)

        dh = lax.dot_general(du_ref[...], win_v[...], (((1,), (1,)), ((), ())), preferred_element_type=F32)
        _, xn, r1 = _ada_norm(x_ref[...], g_pre, sc, sh)
        dxb, d_sh, d_sc, d_g = _ada_norm_bwd(dh, xn, r1, g_pre, sc)
        dx_ref[...] = dxo_ref[...] + dxb
        rowd_ref[0:1, :] += d_g
        rowd_ref[1:2, :] += d_gp
        rowb_ref[0:1, :] += d_sh
        rowb_ref[1:2, :] += d_sc
        rowb_ref[2:3, :] += d_gt

    def tile(width):
        return pl.BlockSpec((None, tm, width), lambda b, i: (b, n - 1 - i, 0))

    return _grid_call(
        body, name, (nb, n),
        in_specs=[tile(d), tile(d), tile(d), tile(din), tile(dc), tile(dpool),
                  pl.BlockSpec((None, 8, d), lambda b, i: (b, 0, 0)), _full(vec_d.shape), _full(vec_c.shape),
                  _full(cw.shape), _full(pw.shape), _ANY, _ANY],
        out_specs=[tile(d), tile(din), tile(d), _full((8, d)), pl.BlockSpec((None, 8, d), lambda b, i: (b, 0, 0)),
                   _full((8, dc)), _full((HALO, dc)), _full(pw.shape)],
        out_shape=[jax.ShapeDtypeStruct((nb, s, d), F32), jax.ShapeDtypeStruct((nb, s, din), BF16),
                   jax.ShapeDtypeStruct((nb, s, d), BF16), jax.ShapeDtypeStruct((8, d), F32),
                   jax.ShapeDtypeStruct((nb, 8, d), F32), jax.ShapeDtypeStruct((8, dc), F32),
                   jax.ShapeDtypeStruct((HALO, dc), F32), jax.ShapeDtypeStruct(pw.shape, F32)],
        scratch_shapes=[pltpu.VMEM((d, din), BF16), pltpu.VMEM((dmix, d), BF16),
                        pltpu.VMEM((rext, dc), F32), pltpu.VMEM((rext, dpool), F32),
                        pltpu.VMEM((SUBLANES - 1, rext, dc), F32), pltpu.SemaphoreType.DMA((2 * nj,))],
        args=(dxo, x, o, u, ac, dpl, mod, vec_d, vec_c, cw, pw, win_g, wout_g), exchange=exchange)


def _ffn_forward(x, mod, vec_d, fw, wup_g, wdn_g, tm, name, exchange=None, target=None):
    nb, s, d = x.shape
    n = s // tm
    nj, _, ucol = wup_g.shape
    f2 = nj * ucol
    dff = f2 // 2
    rd = wdn_g.shape[1]
    nq = nj // 2
    cs = dff // nq

    with_loss = target is not None

    def body(*refs):
        refs = list(refs)
        x_ref, mod_ref, vd_ref, fw_ref = refs[:4]
        t_ref = refs.pop(4) if with_loss else None
        wup_hbm, wdn_hbm, xo_ref, h_ref, u_ref, uc_ref, hid_ref, o_ref = refs[4:12]
        sq_ref = refs.pop(12) if with_loss else None
        wup_v, wdn_v, prev_u, sems = refs[12:]
        b, i = pl.program_id(0), pl.program_id(1)
        pairs = [(wup_hbm.at[j], wup_v.at[:, pl.ds(j * ucol, ucol)]) for j in range(nj)]
        pairs += [(wdn_hbm.at[j], wdn_v.at[pl.ds(j * rd, rd), :]) for j in range(nj)]
        _load_weights((b == 0) & (i == 0), pairs, sems)

        if with_loss:
            @pl.when((b == 0) & (i == 0))
            def _():
                sq_ref[...] = jnp.zeros_like(sq_ref)

        @pl.when(i == 0)
        def _():
            prev_u[...] = jnp.zeros_like(prev_u)

        xv = x_ref[...]
        h, _, _ = _ada_norm(xv, vd_ref[2:3, :], mod_ref[4:5, :], mod_ref[3:4, :])
        hb = h.astype(BF16)
        h_ref[...] = hb

        def conv(cols):
            uc = jnp.dot(hb, wup_v[:, cols], preferred_element_type=F32)
            u_ref[:, cols] = uc.astype(BF16)
            before = prev_u[:, cols]
            prev_u[:, cols] = uc[tm - FHALO:, :]
            out = (fw_ref[3:4, cols] + fw_ref[2:3, cols] * uc + fw_ref[1:2, cols] * _rows_before(before, uc, 1)
                   + fw_ref[0:1, cols] * _rows_before(before, uc, 2))
            uc_ref[:, cols] = out.astype(BF16)
            return out

        o = jnp.zeros((tm, d), F32)
        for q in range(nq):
            val = conv(pl.ds(q * cs, cs))
            gate = conv(pl.ds(dff + q * cs, cs))
            hid = ((gate * _sigmoid(gate)) * val).astype(BF16)
            hid_ref[:, pl.ds(q * cs, cs)] = hid
            o = o + jnp.dot(hid, wdn_v[pl.ds(q * cs, cs), :], preferred_element_type=F32)
        o_ref[...] = o
        r2 = lax.rsqrt(jnp.mean(o * o, axis=-1, keepdims=True) + EPS)
        y = xv + (1.0 + mod_ref[5:6, :]) * ((o * r2) * vd_ref[3:4, :])
        if with_loss:
            e = y - t_ref[...]
            xo_ref[...] = e * (1.0 / d)
            sq_ref[0:1, :] += _colsum(e * e)
        else:
            xo_ref[...] = y

    def tile(width):
        return pl.BlockSpec((None, tm, width), lambda b, i: (b, i, 0))

    loss_in = [tile(d)] if with_loss else []
    return _grid_call(
        body, name, (nb, n),
        in_specs=[tile(d), pl.BlockSpec((None, 8, d), lambda b, i: (b, 0, 0)), _full(vec_d.shape), _full(fw.shape)]
        + loss_in + [_ANY, _ANY],
        out_specs=[tile(d), tile(d), tile(f2), tile(f2), tile(dff), tile(d)] + ([_full((8, d))] if with_loss else []),
        out_shape=[jax.ShapeDtypeStruct((nb, s, d), F32), jax.ShapeDtypeStruct((nb, s, d), BF16),
                   jax.ShapeDtypeStruct((nb, s, f2), BF16), jax.ShapeDtypeStruct((nb, s, f2), BF16),
                   jax.ShapeDtypeStruct((nb, s, dff), BF16), jax.ShapeDtypeStruct((nb, s, d), F32)]
        + ([jax.ShapeDtypeStruct((8, d), F32)] if with_loss else []),
        scratch_shapes=[pltpu.VMEM((d, f2), BF16), pltpu.VMEM((dff, d), BF16),
                        pltpu.VMEM((FHALO, f2), F32), pltpu.SemaphoreType.DMA((2 * nj,))],
        args=(x, mod, vec_d, fw) + ((target,) if with_loss else ()) + (wup_g, wdn_g), exchange=exchange)


def _ffn_backward(dxo, x, o, u, uc, mod, vec_d, fw, wup_g, wdn_g, tm, name, exchange=None):
    nb, s, d = x.shape
    n = s // tm
    nj, _, ucol = wup_g.shape
    f2 = nj * ucol
    dff = f2 // 2
    rd = wdn_g.shape[1]
    nq = nj // 2
    cs = dff // nq

    def body(dxo_ref, x_ref, o_ref, u_ref, uc_ref, mod_ref, vd_ref, fw_ref, wup_hbm, wdn_hbm,
             dx_ref, du_ref, dob_ref, rowd_ref, rowb_ref, dfw_ref,
             wup_v, wdn_v, next_d, sems):
        b, i = pl.program_id(0), pl.program_id(1)
        first = (b == 0) & (i == 0)
        pairs = [(wup_hbm.at[j], wup_v.at[:, pl.ds(j * ucol, ucol)]) for j in range(nj)]
        pairs += [(wdn_hbm.at[j], wdn_v.at[pl.ds(j * rd, rd), :]) for j in range(nj)]
        _load_weights(first, pairs, sems)

        @pl.when(first)
        def _():
            rowd_ref[...] = jnp.zeros_like(rowd_ref)
            dfw_ref[...] = jnp.zeros_like(dfw_ref)

        @pl.when(i == 0)
        def _():
            rowb_ref[...] = jnp.zeros_like(rowb_ref)
            next_d[...] = jnp.zeros_like(next_d)

        g_pre, g_post = vd_ref[2:3, :], vd_ref[3:4, :]
        sh, sc, gt = mod_ref[3:4, :], mod_ref[4:5, :], mod_ref[5:6, :]
        do, d_gt, d_gp = _gated_residual_bwd(dxo_ref[...], o_ref[...], g_post, gt)
        dob = do.astype(BF16)
        dob_ref[...] = dob

        def conv_bwd(cols, duc):
            uc = u_ref[:, cols].astype(F32)
            after = next_d[:, cols]
            next_d[:, cols] = duc[0:FHALO, :]
            d1 = _rows_after(duc, after, 1)
            d2 = _rows_after(duc, after, 2)
            dfw_ref[3:4, cols] += _colsum(duc)
            dfw_ref[2:3, cols] += _colsum(uc * duc)
            dfw_ref[1:2, cols] += _colsum(uc * d1)
            dfw_ref[0:1, cols] += _colsum(uc * d2)
            ob = (fw_ref[2:3, cols] * duc + fw_ref[1:2, cols] * d1 + fw_ref[0:1, cols] * d2).astype(BF16)
            du_ref[:, cols] = ob
            return lax.dot_general(ob, wup_v[:, cols], (((1,), (1,)), ((), ())), preferred_element_type=F32)

        dh = jnp.zeros((tm, d), F32)
        for q in range(nq):
            vcols = pl.ds(q * cs, cs)
            gcols = pl.ds(dff + q * cs, cs)
            dhid = lax.dot_general(dob, wdn_v[vcols, :], (((1,), (1,)), ((), ())), preferred_element_type=F32)
            val = uc_ref[:, vcols].astype(F32)
            gate = uc_ref[:, gcols].astype(F32)
            sg = _sigmoid(gate)
            act = gate * sg
            dval = dhid * act
            dgate = (dhid * val) * (sg + act * (1.0 - sg))
            dh = dh + conv_bwd(vcols, dval)
            dh = dh + conv_bwd(gcols, dgate)

        _, xn, r1 = _ada_norm(x_ref[...], g_pre, sc, sh)
        dxb, d_sh, d_sc, d_g = _ada_norm_bwd(dh, xn, r1, g_pre, sc)
        dx_ref[...] = dxo_ref[...] + dxb
        rowd_ref[2:3, :] += d_g
        rowd_ref[3:4, :] += d_gp
        rowb_ref[3:4, :] += d_sh
        rowb_ref[4:5, :] += d_sc
        rowb_ref[5:6, :] += d_gt

    def tile(width):
        return pl.BlockSpec((None, tm, width), lambda b, i: (b, n - 1 - i, 0))

    return _grid_call(
        body, name, (nb, n),
        in_specs=[tile(d), tile(d), tile(d), tile(f2), tile(f2), pl.BlockSpec((None, 8, d), lambda b, i: (b, 0, 0)),
                  _full(vec_d.shape), _full(fw.shape), _ANY, _ANY],
        out_specs=[tile(d), tile(f2), tile(d), _full((8, d)), pl.BlockSpec((None, 8, d), lambda b, i: (b, 0, 0)),
                   _full(fw.shape)],
        out_shape=[jax.ShapeDtypeStruct((nb, s, d), F32), jax.ShapeDtypeStruct((nb, s, f2), BF16),
                   jax.ShapeDtypeStruct((nb, s, d), BF16), jax.ShapeDtypeStruct((8, d), F32),
                   jax.ShapeDtypeStruct((nb, 8, d), F32), jax.ShapeDtypeStruct(fw.shape, F32)],
        scratch_shapes=[pltpu.VMEM((d, f2), BF16), pltpu.VMEM((dff, d), BF16),
                        pltpu.VMEM((FHALO, f2), F32), pltpu.SemaphoreType.DMA((2 * nj,))],
        args=(dxo, x, o, u, uc, mod, vec_d, fw, wup_g, wdn_g), exchange=exchange)


def _weight_grad(a, b, nblk, split, tt, name, exchange=None):
    t, ka = a.shape
    nb_ = b.shape[1]
    nk = t // tt
    if split == "cols":
        wa, wb, grid = ka, nb_ // nblk, (1, nk)
        a_spec = pl.BlockSpec((tt, ka), lambda j, k: (k, 0))
        b_spec = pl.BlockSpec((tt, nb_), lambda j, k: (k, 0))
        o_spec = pl.BlockSpec((nblk, wa, wb), lambda j, k: (0, 0, 0))
        acc_shape = (ka, nb_)
    elif split == "b":
        wa, wb, grid = ka, nb_ // nblk, (nblk, nk)
        a_spec = pl.BlockSpec((tt, wa), lambda j, k: (k, 0))
        b_spec = pl.BlockSpec((tt, wb), lambda j, k: (k, j))
        o_spec = pl.BlockSpec((None, wa, wb), lambda j, k: (j, 0, 0))
        acc_shape = (wa, wb)
    else:
        wa, wb, grid = ka // nblk, nb_, (nblk, nk)
        a_spec = pl.BlockSpec((tt, wa), lambda j, k: (k, j))
        b_spec = pl.BlockSpec((tt, wb), lambda j, k: (k, 0))
        o_spec = pl.BlockSpec((None, wa, wb), lambda j, k: (j, 0, 0))
        acc_shape = (wa, wb)

    def body(a_ref, b_ref, o_ref, acc):
        k = pl.program_id(1)
        prod = lax.dot_general(a_ref[...], b_ref[...], (((0,), (0,)), ((), ())), preferred_element_type=F32)

        @pl.when(k == 0)
        def _():
            acc[...] = prod

        @pl.when(k > 0)
        def _():
            acc[...] += prod

        @pl.when(k == nk - 1)
        def _():
            if split == "cols":
                for j in range(nblk):
                    o_ref[j] = acc[:, j * wb:(j + 1) * wb].astype(o_ref.dtype)
            else:
                o_ref[...] = acc[...].astype(o_ref.dtype)

    outs, exo = _grid_call(body, name, grid, in_specs=[a_spec, b_spec], out_specs=[o_spec],
                           out_shape=[jax.ShapeDtypeStruct((nblk, wa, wb), BF16)],
                           scratch_shapes=[pltpu.VMEM(acc_shape, F32)], args=(a, b), exchange=exchange)
    return outs[0], exo


def _rows128(a):
    return a.reshape(-1, LANES)


class _ReduceScatter:
    def __init__(self, gs, cidx, idx, tag):
        self.gs, self.cidx, self.idx, self.tag = gs, cidx, idx, tag

    def swap(self):
        return _swap_halves(self.gs)

    def after_swap(self, r1):
        self.ps = _pair_sums(self.gs, r1, self.cidx, name=f"rs_pair_{self.tag}")

    def chips(self):
        return _chip_exchange(self.ps)

    def after_chips(self, r2):
        self.fh = _chip_sums(self.ps, r2, self.idx, name=f"rs_sum_{self.tag}")

    def share(self):
        return _sibling_share(self.fh)

    @staticmethod
    def result(fs):
        return [f.reshape(f.shape[0] * f.shape[1], f.shape[2]) for f in fs]


def kernel(x, c, ada_w, ada_b, pre_mix_g, post_mix_g, w_in, conv_w, conv_b, conv_ln_g, conv_ln_b, pool_w, pool_scale, w_out, pre_ffn_g, post_ffn_g, ffn_up, ffn_conv_w, ffn_conv_b, ffn_down, loss_target, m_ada_w, m_ada_b, m_pre_mix_g, m_post_mix_g, m_w_in, m_conv_w, m_conv_b, m_conv_ln_g, m_conv_ln_b, m_pool_w, m_pool_scale, m_w_out, m_pre_ffn_g, m_post_ffn_g, m_ffn_up, m_ffn_conv_w, m_ffn_conv_b, m_ffn_down, v_ada_w, v_ada_b, v_pre_mix_g, v_post_mix_g, v_w_in, v_conv_w, v_conv_b, v_conv_ln_g, v_conv_ln_b, v_pool_w, v_pool_scale, v_w_out, v_pre_ffn_g, v_post_ffn_g, v_ffn_up, v_ffn_conv_w, v_ffn_conv_b, v_ffn_down):
    nb, s, d = x.shape
    nl = w_in.shape[0]
    taps = conv_w.shape[1]
    ccol = conv_w.shape[2]
    dc = conv_b.shape[1]
    fcol = ffn_conv_w.shape[2]
    f2 = ffn_conv_b.shape[1]
    nmod = ada_b.shape[1] // d
    acol = ada_w.shape[2]
    tm = min(MLP_TILE_ROWS, s)
    tm_mix = min(MIXER_TILE_ROWS, s)
    tt = min(GRAD_CHUNK_ROWS, (nb * s) // 2)

    xi, yi, ci = _pos()
    jm = 2 * xi + yi
    cidx = jnp.reshape(ci, (1,)).astype(jnp.int32)
    idx = jnp.stack([jm, ci]).astype(jnp.int32)

    win_b, wout_b, wup_b, wdn_b = (w.astype(BF16) for w in (w_in, w_out, ffn_up, ffn_down))

    def others(l):
        return [win_b[l], wout_b[l], wdn_b[l]]

    n_cw, n_fw, n_c = nl * taps * ccol, nl * 3 * fcol, nb * d
    packed = jnp.concatenate([conv_w.reshape(-1), ffn_conv_w.reshape(-1), c.reshape(-1)])
    got = _gather8(_rows128(packed), name="gather_small").reshape(N_DEV, -1)
    chips = got[0::2]
    cw_full = chips[:, :n_cw].reshape(N_CHIPS, nl, taps, ccol).transpose(1, 2, 0, 3).reshape(nl, taps, dc)
    fw_full = chips[:, n_cw:n_cw + n_fw].reshape(N_CHIPS, nl, 3, fcol).transpose(1, 2, 0, 3).reshape(nl, 3, f2)
    c_all = got[:, n_cw + n_fw:].reshape(N_DEV * nb, d)

    ada_b_cols = lax.dynamic_slice_in_dim(ada_b, jm * acol, acol, axis=1).reshape(nl, 1, acol)
    mod_cols, first_weights = _ada_forward(c_all, ada_w, ada_b_cols, name="ada_forward",
                                           exchange=_gather(others(0), mid_at=0.9))
    by_owner = mod_cols.reshape(nl, N_DEV, nb, acol).transpose(1, 0, 2, 3).reshape(N_DEV, -1, LANES)
    mod_own = _rows_to_owners(by_owner, name="mod_to_owners").reshape(N_CHIPS, nl, nb, acol)
    mod_own = mod_own.transpose(1, 2, 0, 3).reshape(nl, nb, nmod, d)
    mod_own = jnp.pad(mod_own, ((0, 0), (0, 0), (0, 8 - nmod), (0, 0)))

    vec_d = jnp.stack([pre_mix_g, post_mix_g, pre_ffn_g, post_ffn_g], axis=1)
    vec_c = jnp.stack([conv_b, conv_ln_g, conv_ln_b, pool_scale], axis=1)
    cw_pad = jnp.pad(cw_full, ((0, 0), (0, HALO - taps), (0, 0)))
    fw_rows = jnp.concatenate([fw_full, ffn_conv_b[:, None, :], jnp.zeros((nl, 4, f2), F32)], axis=1)
    pw_b = pool_w.astype(BF16)

    win_g, wout_g, wdn_g = _whole(first_weights)
    saved = []
    xs = x
    for l in range(nl):
        (x1, h1, u1, ac1, dp1, z1, o1), got = _mixer_forward(
            xs, mod_own[l], vec_d[l], vec_c[l], cw_pad[l], pw_b[l], win_g, wout_g, taps, tm_mix, name=f"mixer_fwd_{l}",
            exchange=_gather([wup_b[l]], mid_at=0.9))
        wup_g, = _whole(got)
        last = l + 1 == nl
        (x2, h2, u2, uc2, hid2, o2, *sq), nxt = _ffn_forward(
            x1, mod_own[l], vec_d[l], fw_rows[l], wup_g, wdn_g, tm, name=f"ffn_fwd_{l}",
            exchange=None if last else _gather(others(l + 1), mid_at=0.6), target=loss_target if last else None)
        saved.append((xs, h1, u1, ac1, dp1, z1, o1, x1, h2, u2, uc2, hid2, o2, win_g, wout_g, wup_g, wdn_g))
        if not last:
            win_g, wout_g, wdn_g = _whole(nxt)
        xs = x2

    dx = xs
    loss = lax.psum(0.5 * jnp.sum(sq[0]) / d, ("x", "y", "c"))

    flat = lambda a: a.reshape(nb * s, a.shape[-1])
    small = [None] * nl
    big_mlp, big_mix = [None] * nl, [None] * nl
    mlp = mix = None
    for l in reversed(range(nl)):
        x0, h1, u1, ac1, dp1, z1, o1, x1, h2, u2, uc2, hid2, o2, win_g, wout_g, wup_g, wdn_g = saved[l]
        (dx, du2, do2, rowd2, rowb2, dfw), got = _ffn_backward(
            dx, x1, o2, u2, uc2, mod_own[l], vec_d[l], fw_rows[l], wup_g, wdn_g, tm, name=f"ffn_bwd_{l}",
            exchange=_combine([mlp.chips(), mix.swap()]) if mlp else None)
        if mlp:
            mlp.after_chips(got[:2])
            mix.after_swap(got[2:])
        g_up, got = _weight_grad(flat(h2), flat(du2), N_CHIPS, "b", 2 * tt, name=f"grad_ffn_up_{l}",
                                 exchange=_combine([mlp.share(), mix.chips()]) if mlp else None)
        if mlp:
            big_mlp[l + 1] = mlp.result(got[:2])
            mix.after_chips(got[2:])
        g_dn, got = _weight_grad(flat(hid2), flat(do2), 2, "a", 2 * tt, name=f"grad_ffn_down_{l}",
                                 exchange=_swap_halves([g_up]) if l == 0 else None)
        g_dn = g_dn.reshape(N_CHIPS, -1, d)
        mlp_above, mlp = mlp, _ReduceScatter([g_up, g_dn], cidx, idx, f"mlp_{l}")
        if l == 0:
            mlp.after_swap(list(got) + list(_run_exchange(_swap_halves([g_dn]), name="rs_swap_down_0")))
        first = mlp.swap() if l > 0 else mlp.chips()
        (dx, du1, do1, rowd1, rowb1, rowc, dcw, dpw), got = _mixer_backward(
            dx, x0, o1, u1, ac1, dp1, mod_own[l], vec_d[l], vec_c[l], cw_pad[l], pw_b[l], win_g, wout_g, taps, tm_mix,
            name=f"mixer_bwd_{l}", exchange=_combine([first, mix.share()]) if mlp_above else first)
        if l > 0:
            mlp.after_swap(got[:2])
        else:
            mlp.after_chips(got[:2])
        if mlp_above:
            big_mix[l + 1] = mix.result(got[2:])
        g_in, got = _weight_grad(flat(h1), flat(du1), N_CHIPS, "cols", tt, name=f"grad_w_in_{l}",
                                 exchange=mlp.share() if l == 0 else None)
        if l == 0:
            big_mlp[0] = mlp.result(got)
        g_out, _ = _weight_grad(flat(z1), flat(do1), 1, "cols", tt, name=f"grad_w_out_{l}")
        mix = _ReduceScatter([g_in, g_out.reshape(N_CHIPS, -1, d)], cidx, idx, f"mix_{l}")
        small[l] = dict(rowd=rowd1 + rowd2, rowb=rowb1 + rowb2, rowc=rowc, dcw=dcw[:taps], dpw=dpw, dfw=dfw)
    mix.after_swap(_run_exchange(mix.swap(), name="rs_swap_mix_0"))

    dmod_own = jnp.stack([small[l]["rowb"][:, :nmod, :] for l in range(nl)])
    dmod_all, got = _gather8(_rows128(dmod_own), name="gather_dmod", exchange=mix.chips())
    mix.after_chips(got)
    big_mix[0] = mix.result(_run_exchange(mix.share(), name="rs_share_mix_0"))
    dmod_all = dmod_all.reshape(N_DEV, nl, nb, nmod * d)
    dmod_all = dmod_all.transpose(1, 0, 2, 3).reshape(nl, N_DEV * nb, nmod * d)
    dmod_cols = lax.dynamic_slice_in_dim(dmod_all, jm * acol, acol, axis=2)
    (g_ada_w, d_ada_w, nm_ada_w, nv_ada_w), _ = _ada_update(c_all, dmod_cols, ada_w, m_ada_w, v_ada_w, name="ada_update")

    def st(key, row=None):
        return jnp.stack([small[l][key] if row is None else small[l][key][row] for l in range(nl)])

    local = {
        "ada_b": dmod_own.sum(axis=1).reshape(nl, nmod * d),
        "pre_mix_g": st("rowd", 0), "post_mix_g": st("rowd", 1),
        "conv_b": st("rowc", 0), "conv_ln_g": st("rowc", 1), "conv_ln_b": st("rowc", 2),
        "pool_w": st("dpw"), "pool_scale": st("rowc", 3),
        "pre_ffn_g": st("rowd", 2), "post_ffn_g": st("rowd", 3),
        "ffn_conv_b": st("dfw", 3), "conv_w": st("dcw"), "ffn_conv_w": jnp.stack([small[l]["dfw"][:3] for l in range(nl)]),
    }
    names = list(local)
    sizes = [local[k].size for k in names]
    pad = -sum(sizes) % (4 * SUBLANES * LANES)
    packed = jnp.concatenate([local[k].reshape(-1) for k in names] + [jnp.zeros((pad,), F32)])
    summed = _allreduce8(_rows128(packed), name="allreduce_small").reshape(-1)
    grads, off = {}, 0
    for k, sz in zip(names, sizes):
        grads[k] = summed[off:off + sz].reshape(local[k].shape)
        off += sz
    grads["conv_w"] = lax.dynamic_slice_in_dim(grads["conv_w"], jm * ccol, ccol, axis=2)
    grads["ffn_conv_w"] = lax.dynamic_slice_in_dim(grads["ffn_conv_w"], jm * fcol, fcol, axis=2)

    params = dict(ada_b=(ada_b, m_ada_b, v_ada_b), pre_mix_g=(pre_mix_g, m_pre_mix_g, v_pre_mix_g),
                  post_mix_g=(post_mix_g, m_post_mix_g, v_post_mix_g), conv_b=(conv_b, m_conv_b, v_conv_b),
                  conv_ln_g=(conv_ln_g, m_conv_ln_g, v_conv_ln_g), conv_ln_b=(conv_ln_b, m_conv_ln_b, v_conv_ln_b),
                  pool_w=(pool_w, m_pool_w, v_pool_w), pool_scale=(pool_scale, m_pool_scale, v_pool_scale),
                  pre_ffn_g=(pre_ffn_g, m_pre_ffn_g, v_pre_ffn_g), post_ffn_g=(post_ffn_g, m_post_ffn_g, v_post_ffn_g),
                  ffn_conv_b=(ffn_conv_b, m_ffn_conv_b, v_ffn_conv_b), conv_w=(conv_w, m_conv_w, v_conv_w),
                  ffn_conv_w=(ffn_conv_w, m_ffn_conv_w, v_ffn_conv_w))
    pack = lambda i, g=None: _rows128(jnp.concatenate([(grads[k] if g else params[k][i]).reshape(-1) for k in names]))
    sd, sm, sv = _adamw_flat(pack(0), pack(0, True), pack(1), pack(2), name="adamw_small")
    outs = {}
    off = 0
    for k in names:
        shape, sz = params[k][0].shape, params[k][0].size
        outs[k] = (grads[k],) + tuple(a.reshape(-1)[off:off + sz].reshape(shape) for a in (sd, sm, sv))
        off += sz

    outs["ada_w"] = (g_ada_w, d_ada_w, nm_ada_w, nv_ada_w)
    for k, w, m, v, gs in [("w_in", w_in, m_w_in, v_w_in, [big_mix[l][0] for l in range(nl)]),
                           ("w_out", w_out, m_w_out, v_w_out, [big_mix[l][1] for l in range(nl)]),
                           ("ffn_up", ffn_up, m_ffn_up, v_ffn_up, [big_mlp[l][0] for l in range(nl)]),
                           ("ffn_down", ffn_down, m_ffn_down, v_ffn_down, [big_mlp[l][1] for l in range(nl)])]:
        outs[k] = tuple(_adamw_layers(w, m, v, gs, name=f"adamw_{k}"))

    order = ["ada_w", "ada_b", "pre_mix_g", "post_mix_g", "w_in", "conv_w", "conv_b", "conv_ln_g", "conv_ln_b", "pool_w",
             "pool_scale", "w_out", "pre_ffn_g", "post_ffn_g", "ffn_up", "ffn_conv_w", "ffn_conv_b", "ffn_down"]
    return (loss, dx) + tuple(outs[k][i] for i in range(4) for k in order)
```

```python
import functools

import jax
import jax.numpy as jnp
from jax import lax
from jax.experimental import pallas as pl
from jax.experimental.pallas import tpu as pltpu

F32 = jnp.float32
BF16 = jnp.bfloat16
MESH = pl.DeviceIdType.MESH

EPS = 1e-6
POOL_WINDOWS = (2, 4, 8, 16)
ADAM_LR = 0.001
ADAM_B1 = 0.9
ADAM_B2 = 0.999
ADAM_EPS = 1e-08
ADAM_WD = 0.01
ADAM_STEP = 10

N_CHIPS = 4
N_DEV = 8
LANES = 128
SUBLANES = 8
HALO = 32
FHALO = 8
VMEM_LIMIT = 60 * 1024 * 1024
MLP_TILE_ROWS = 256
MIXER_TILE_ROWS = 512
GRAD_CHUNK_ROWS = 2048


def _pos():
    return lax.axis_index("x"), lax.axis_index("y"), lax.axis_index("c")


def _flip(v, f):
    return 1 - v if f else v


def _full(shape):
    nd = len(shape)
    return pl.BlockSpec(shape, lambda *_: (0,) * nd)


_ANY = pl.BlockSpec(memory_space=pl.ANY)
_VMEM = pl.BlockSpec(memory_space=pltpu.VMEM)


def _sigmoid(v):
    return 1.0 / (1.0 + jnp.exp(-v))


def _colsum(v):
    return jnp.sum(v, axis=0, keepdims=True)


def _gather8(v, name, exchange=None):
    r, ccols = v.shape
    ex = exchange
    nci, nco = (len(ex.ins), len(ex.outs)) if ex else (0, 0)

    def body(*refs):
        v_ref, cin, out_ref, cout = refs[0], refs[1:1 + nci], refs[1 + nci], refs[2 + nci:2 + nci + nco]
        send_sems, recv_sems, local_sem = refs[2 + nci + nco:5 + nci + nco]
        if ex:
            sems = _Sems(*refs[5 + nci + nco:])
            ex.start(cin, cout, sems)
        x, y, c = _pos()
        me = 4 * x + 2 * y + c
        mine = pltpu.make_async_copy(v_ref, out_ref.at[me], local_sem)
        mine.start()
        peers = [(_flip(x, (k >> 2) & 1), _flip(y, (k >> 1) & 1), _flip(c, k & 1)) for k in range(1, N_DEV)]
        sends = []
        for k, peer in enumerate(peers):
            cp = pltpu.make_async_remote_copy(src_ref=v_ref, dst_ref=out_ref.at[me], send_sem=send_sems.at[k],
                                              recv_sem=recv_sems.at[k], device_id=peer, device_id_type=MESH)
            cp.start()
            sends.append(cp)
        for k, peer in enumerate(peers):
            pidx = 4 * peer[0] + 2 * peer[1] + peer[2]
            pltpu.make_async_remote_copy(src_ref=v_ref, dst_ref=out_ref.at[pidx], send_sem=send_sems.at[k],
                                         recv_sem=recv_sems.at[k], device_id=peer, device_id_type=MESH).wait_recv()
        for cp in sends:
            cp.wait_send()
        mine.wait()
        if ex:
            if ex.mid is not None:
                ex.mid(cin, cout, sems)
            ex.finish(cin, cout, sems)

    outs = pl.pallas_call(
        body, name=name, out_shape=[jax.ShapeDtypeStruct((N_DEV, r, ccols), v.dtype)] + (ex.outs if ex else []),
        in_specs=[_VMEM] + [_ANY] * nci, out_specs=[_VMEM] + [_ANY] * nco,
        scratch_shapes=[pltpu.SemaphoreType.DMA((N_DEV - 1,)), pltpu.SemaphoreType.DMA((N_DEV - 1,)),
                        pltpu.SemaphoreType.DMA(())] + (ex.scratch() if ex else []),
        input_output_aliases={1 + a: 1 + b for a, b in ex.aliases.items()} if ex else {},
        compiler_params=pltpu.CompilerParams(vmem_limit_bytes=VMEM_LIMIT),
    )(v, *(ex.ins if ex else []))
    return (outs[0], list(outs[1:])) if ex else outs[0]


def _rows_to_owners(v, name):
    _, r, ccols = v.shape

    def body(v_ref, out_ref, send_sems, recv_sems, local_sem):
        x, y, c = _pos()
        jm = 2 * x + y
        mine = pltpu.make_async_copy(v_ref.at[2 * jm + c], out_ref.at[jm], local_sem)
        mine.start()
        peers, pjs = _chip_peers(x, y, c)
        sends = []
        for k, peer in enumerate(peers):
            cp = pltpu.make_async_remote_copy(src_ref=v_ref.at[2 * pjs[k] + c], dst_ref=out_ref.at[jm],
                                              send_sem=send_sems.at[k], recv_sem=recv_sems.at[k],
                                              device_id=peer, device_id_type=MESH)
            cp.start()
            sends.append(cp)
        for k, peer in enumerate(peers):
            pltpu.make_async_remote_copy(src_ref=v_ref.at[0], dst_ref=out_ref.at[pjs[k]], send_sem=send_sems.at[k],
                                         recv_sem=recv_sems.at[k], device_id=peer, device_id_type=MESH).wait_recv()
        for cp in sends:
            cp.wait_send()
        mine.wait()

    return pl.pallas_call(
        body, name=name, out_shape=jax.ShapeDtypeStruct((N_CHIPS, r, ccols), v.dtype),
        in_specs=[_VMEM], out_specs=_VMEM,
        scratch_shapes=[pltpu.SemaphoreType.DMA((N_CHIPS - 1,)), pltpu.SemaphoreType.DMA((N_CHIPS - 1,)),
                        pltpu.SemaphoreType.DMA(())],
        compiler_params=pltpu.CompilerParams(vmem_limit_bytes=VMEM_LIMIT),
    )(v)


def _allreduce8(v, name):
    r, ccols = v.shape
    h = r // 2
    q = h // 2

    def body(v_ref, out_ref, whole, part, done, send_sems, recv_sems):
        x, y, c = _pos()
        sib = (x, y, 1 - c)
        mine = pl.ds(pl.multiple_of(c * h, SUBLANES), h)
        theirs = pl.ds(pl.multiple_of((1 - c) * h, SUBLANES), h)
        quarters = [pl.ds(pl.multiple_of(c * h + k * q, SUBLANES), q) for k in range(2)]
        along_x, along_y = (1 - x, y, c), (x, 1 - y, c)

        def exchange(pairs):
            cps = [pltpu.make_async_remote_copy(src_ref=src, dst_ref=dst, send_sem=send_sems.at[k], recv_sem=recv_sems.at[k],
                                                device_id=peer, device_id_type=MESH) for src, dst, k, peer in pairs]
            for cp in cps:
                cp.start()
            for cp in cps:
                cp.wait()

        exchange([(v_ref, whole, 0, sib)])
        out_ref[...] = v_ref[...] + whole[...]
        for stage, peers in enumerate(((along_x, along_y), (along_y, along_x))):
            exchange([(out_ref.at[quarters[k]], part.at[2 * stage + k], 1 + 2 * stage + k, peers[k]) for k in range(2)])
            for k in range(2):
                out_ref[quarters[k], :] = out_ref[quarters[k], :] + part[2 * stage + k]
        exchange([(out_ref.at[mine], done, 5, sib)])
        out_ref[theirs, :] = done[...]

    return pl.pallas_call(
        body, name=name, out_shape=jax.ShapeDtypeStruct((r, ccols), v.dtype),
        in_specs=[_VMEM], out_specs=_VMEM,
        scratch_shapes=[pltpu.VMEM((r, ccols), v.dtype), pltpu.VMEM((4, q, ccols), v.dtype), pltpu.VMEM((h, ccols), v.dtype),
                        pltpu.SemaphoreType.DMA((6,)), pltpu.SemaphoreType.DMA((6,))],
        compiler_params=pltpu.CompilerParams(vmem_limit_bytes=VMEM_LIMIT),
    )(v)


def _chip_peers(x, y, c):
    peers = [(_flip(x, (k >> 1) & 1), _flip(y, k & 1), c) for k in range(1, N_CHIPS)]
    return peers, [2 * p[0] + p[1] for p in peers]


class _Exchange:
    def __init__(self, ins, outs, aliases, n_sems, n_local, start, finish, mid=None, mid_at=1.0, sibling=False, chips=False):
        self.ins, self.outs, self.aliases = list(ins), list(outs), dict(aliases)
        self.n_sems, self.n_local, self.start, self.finish = n_sems, n_local, start, finish
        self.mid, self.mid_at = mid, mid_at
        self.sibling, self.chips = sibling, chips

    def collective_id(self):
        return {(True, False): 1, (False, True): 2, (True, True): 3}[(self.sibling, self.chips)]

    def handshake(self):
        x, y, c = _pos()
        peers = ([(x, y, 1 - c)] if self.sibling else []) + (_chip_peers(x, y, c)[0] if self.chips else [])
        barrier = pltpu.get_barrier_semaphore()
        for peer in peers:
            pl.semaphore_signal(barrier, inc=1, device_id=peer, device_id_type=MESH)
        pl.semaphore_wait(barrier, len(peers))

    def scratch(self):
        return [pltpu.SemaphoreType.DMA((self.n_sems,)), pltpu.SemaphoreType.DMA((self.n_sems,)),
                pltpu.SemaphoreType.DMA((max(self.n_local, 1),))]


class _Sems:
    def __init__(self, send, recv, local, base=0, lbase=0):
        self.send, self.recv, self.loc, self.base, self.lbase = send, recv, local, base, lbase

    def shifted(self, by, lby):
        return _Sems(self.send, self.recv, self.loc, self.base + by, self.lbase + lby)

    def local(self, k):
        return self.loc.at[self.lbase + k]


def _remote(src, dst, sems, k, peer):
    return pltpu.make_async_remote_copy(src_ref=src, dst_ref=dst, send_sem=sems.send.at[sems.base + k],
                                        recv_sem=sems.recv.at[sems.base + k], device_id=peer, device_id_type=MESH)


def _combine(exs):
    ins = [a for ex in exs for a in ex.ins]
    outs = [o for ex in exs for o in ex.outs]
    aliases, spans, ni, no, ns, nloc = {}, [], 0, 0, 0, 0
    for ex in exs:
        aliases.update({ni + a: no + b for a, b in ex.aliases.items()})
        spans.append((ni, no, ns, nloc))
        ni, no, ns, nloc = ni + len(ex.ins), no + len(ex.outs), ns + ex.n_sems, nloc + ex.n_local

    def each(which):
        def run(ins_, outs_, sems):
            for ex, (i0, o0, s0, l0) in zip(exs, spans):
                stage = getattr(ex, which)
                if stage is not None:
                    stage(ins_[i0:i0 + len(ex.ins)], outs_[o0:o0 + len(ex.outs)], sems.shifted(s0, l0))
        return run

    mids = [ex.mid_at for ex in exs if ex.mid is not None]
    return _Exchange(ins, outs, aliases, ns, nloc, each("start"), each("finish"),
                     mid=each("mid") if mids else None, mid_at=max(mids) if mids else 1.0,
                     sibling=any(ex.sibling for ex in exs), chips=any(ex.chips for ex in exs))


def _gather(shards, mid_at=1.0):
    n = len(shards)
    per = N_CHIPS - 1
    halves = [s.reshape(2, s.shape[0] // 2, s.shape[1]) for s in shards]

    def copies(ins, outs, sems):
        x, y, c = _pos()
        jm = 2 * x + y
        sib = (x, y, 1 - c)
        peers, pjs = _chip_peers(x, y, c)
        sends, recvs, passes, passed = [], [], [], []
        for a in range(n):
            own = _remote(ins[a], outs[a].at[jm], sems, 2 * n * per + a, sib)
            sends.append(own)
            passed.append(own)
            for k, peer in enumerate(peers):
                landed, theirs = outs[a].at[pjs[k], c], outs[a].at[pjs[k], 1 - c]
                sends.append(_remote(ins[a].at[c], outs[a].at[jm, c], sems, 2 * (a * per + k), peer))
                recvs.append(_remote(landed, landed, sems, 2 * (a * per + k), peer))
                passes.append(_remote(landed, landed, sems, 2 * (a * per + k) + 1, sib))
                passed.append(_remote(theirs, theirs, sems, 2 * (a * per + k) + 1, sib))
        return sends, recvs, passes, passed

    def start(ins, outs, sems):
        for cp in copies(ins, outs, sems)[0]:
            cp.start()

    def mid(ins, outs, sems):
        _, recvs, passes, _ = copies(ins, outs, sems)
        for got, fwd in zip(recvs, passes):
            got.wait_recv()
            fwd.start()

    def finish(ins, outs, sems):
        sends, _, passes, passed = copies(ins, outs, sems)
        for cp in passed:
            cp.wait_recv()
        for cp in sends + passes:
            cp.wait_send()

    outs = [jax.ShapeDtypeStruct((N_CHIPS,) + h.shape, h.dtype) for h in halves]
    return _Exchange(halves, outs, {}, 2 * n * per + n, 0, start, finish, mid=mid, mid_at=mid_at, sibling=True, chips=True)


def _whole(gathered):
    return [g.reshape(g.shape[0], g.shape[1] * g.shape[2], g.shape[3]) for g in gathered]


def _swap_halves(gs):
    n = len(gs)
    halves = [g.reshape(g.shape[0], 2, g.shape[1] // 2, g.shape[2]) for g in gs]

    def copies(ins, outs, sems):
        x, y, c = _pos()
        sib = (x, y, 1 - c)
        return [_remote(ins[a].at[:, 1 - c], outs[a], sems, a, sib) for a in range(n)]

    def start(ins, outs, sems):
        for cp in copies(ins, outs, sems):
            cp.start()

    def finish(ins, outs, sems):
        for cp in copies(ins, outs, sems):
            cp.wait()

    outs = [jax.ShapeDtypeStruct((g.shape[0], g.shape[1] // 2, g.shape[2]), g.dtype) for g in gs]
    return _Exchange(halves, outs, {}, n, 0, start, finish, sibling=True)


def _chip_exchange(ps):
    n = len(ps)
    per = N_CHIPS - 1

    def copies(ins, outs, sems):
        x, y, c = _pos()
        peers, pjs = _chip_peers(x, y, c)
        return [_remote(ins[a].at[pjs[k]], outs[a].at[k], sems, a * per + k, peer)
                for a in range(n) for k, peer in enumerate(peers)]

    def start(ins, outs, sems):
        for cp in copies(ins, outs, sems):
            cp.start()

    def finish(ins, outs, sems):
        for cp in copies(ins, outs, sems):
            cp.wait()

    outs = [jax.ShapeDtypeStruct((per,) + p.shape[1:], p.dtype) for p in ps]
    return _Exchange(ps, outs, {}, n * per, 0, start, finish, chips=True)


def _sibling_share(fs):
    n = len(fs)

    def copies(outs, sems):
        x, y, c = _pos()
        sib = (x, y, 1 - c)
        sends = [_remote(outs[a].at[c], outs[a].at[c], sems, a, sib) for a in range(n)]
        recvs = [_remote(outs[a].at[1 - c], outs[a].at[1 - c], sems, a, sib) for a in range(n)]
        return sends, recvs

    def start(ins, outs, sems):
        for cp in copies(outs, sems)[0]:
            cp.start()

    def finish(ins, outs, sems):
        sends, recvs = copies(outs, sems)
        for cp in recvs:
            cp.wait_recv()
        for cp in sends:
            cp.wait_send()

    outs = [jax.ShapeDtypeStruct(f.shape, f.dtype) for f in fs]
    return _Exchange(fs, outs, {a: a for a in range(n)}, n, 0, start, finish, sibling=True)


def _run_exchange(ex, name):
    ni, no = len(ex.ins), len(ex.outs)

    def body(*refs):
        ins, outs, sems = refs[:ni], refs[ni:ni + no], _Sems(*refs[ni + no:])
        ex.handshake()
        ex.start(ins, outs, sems)
        if ex.mid is not None:
            ex.mid(ins, outs, sems)
        ex.finish(ins, outs, sems)

    return pl.pallas_call(
        body, name=name, out_shape=ex.outs, in_specs=[_ANY] * ni, out_specs=[_ANY] * no,
        input_output_aliases=ex.aliases, scratch_shapes=ex.scratch(),
        compiler_params=pltpu.CompilerParams(collective_id=ex.collective_id()),
    )(*ex.ins)


def _grid_call(body, name, grid, in_specs, out_specs, out_shape, scratch_shapes, args, exchange=None):
    ni, no = len(in_specs), len(out_specs)
    params = pltpu.CompilerParams(dimension_semantics=("arbitrary",) * len(grid), vmem_limit_bytes=VMEM_LIMIT)
    if exchange is None:
        outs = pl.pallas_call(body, name=name, grid=grid, in_specs=in_specs, out_specs=out_specs, out_shape=out_shape,
                              scratch_shapes=scratch_shapes, compiler_params=params)(*args)
        return list(outs), []
    ex = exchange
    nci, nco = len(ex.ins), len(ex.outs)

    def hosted(*refs):
        cin = refs[ni:ni + nci]
        cout = refs[ni + nci + no:ni + nci + no + nco]
        sems = _Sems(*refs[len(refs) - 3:])
        main = refs[:ni] + refs[ni + nci:ni + nci + no] + refs[ni + nci + no + nco:len(refs) - 3]
        ids = [pl.program_id(a) for a in range(len(grid))]
        first = functools.reduce(lambda p, q: p & q, [i == 0 for i in ids])
        last = functools.reduce(lambda p, q: p & q, [i == g - 1 for i, g in zip(ids, grid)])

        @pl.when(first)
        def _():
            ex.handshake()
            ex.start(cin, cout, sems)

        if ex.mid is not None:
            steps = functools.reduce(lambda p, q: p * q, grid)
            flat = functools.reduce(lambda p, q: p * q[1] + q[0], zip(ids[1:], grid[1:]), ids[0])

            @pl.when(flat == min(steps - 1, int(ex.mid_at * steps)))
            def _():
                ex.mid(cin, cout, sems)

        body(*main)

        @pl.when(last)
        def _():
            ex.finish(cin, cout, sems)

    outs = pl.pallas_call(
        hosted, name=name, grid=grid, in_specs=list(in_specs) + [_ANY] * nci, out_specs=list(out_specs) + [_ANY] * nco,
        out_shape=list(out_shape) + ex.outs, scratch_shapes=list(scratch_shapes) + ex.scratch(),
        input_output_aliases={ni + a: no + b for a, b in ex.aliases.items()},
        compiler_params=pltpu.CompilerParams(dimension_semantics=("arbitrary",) * len(grid), vmem_limit_bytes=VMEM_LIMIT,
                                             collective_id=ex.collective_id()),
    )(*args, *ex.ins)
    return list(outs[:no]), list(outs[no:])


SUM_BLOCK_BYTES = 4 * 1024 * 1024
ADAM_BLOCK_BYTES = 2 * 1024 * 1024


def _row_tile(rows, cols, itemsize, budget):
    best = None
    for t in range(16, rows + 1, 16):
        if rows % t == 0 and t * cols * itemsize <= budget:
            best = t
    return best if best is not None else rows


def _turns(step_counts):
    offs, total = [], 0
    for n in step_counts:
        offs.append(total)
        total += n
    own = [lambda s, o=o, n=n: jnp.clip(s - o, 0, n - 1) for o, n in zip(offs, step_counts)]
    mine = [lambda s, o=o, n=n: (s >= o) & (s < o + n) for o, n in zip(offs, step_counts)]
    return total, own, mine


def _pair_sums(gs, r1s, cidx, name):
    n = len(gs)
    tiles = [_row_tile(g.shape[1] // 2, g.shape[2], 4, SUM_BLOCK_BYTES) for g in gs]
    nts = [g.shape[1] // 2 // tr for g, tr in zip(gs, tiles)]
    total, own, mine = _turns([g.shape[0] * nt for g, nt in zip(gs, nts)])

    def body(c_ref, *refs):
        s = pl.program_id(0)
        for a in range(n):
            g_ref, r_ref, o_ref = refs[2 * a], refs[2 * a + 1], refs[2 * n + a]

            @pl.when(mine[a](s))
            def _(g_ref=g_ref, r_ref=r_ref, o_ref=o_ref):
                o_ref[...] = (g_ref[...].astype(F32) + r_ref[...].astype(F32)).astype(o_ref.dtype)

    in_specs, out_specs = [], []
    for a, (g, tr, nt) in enumerate(zip(gs, tiles, nts)):
        blk = (None, tr, g.shape[2])
        in_specs += [pl.BlockSpec(blk, lambda s, c_ref, a=a, nt=nt: (own[a](s) // nt, c_ref[0] * nt + own[a](s) % nt, 0)),
                     pl.BlockSpec(blk, lambda s, c_ref, a=a, nt=nt: (own[a](s) // nt, own[a](s) % nt, 0))]
        out_specs.append(pl.BlockSpec(blk, lambda s, c_ref, a=a, nt=nt: (own[a](s) // nt, own[a](s) % nt, 0)))
    return pl.pallas_call(
        body, name=name, out_shape=[jax.ShapeDtypeStruct((g.shape[0], g.shape[1] // 2, g.shape[2]), g.dtype) for g in gs],
        grid_spec=pltpu.PrefetchScalarGridSpec(num_scalar_prefetch=1, grid=(total,), in_specs=in_specs, out_specs=out_specs),
        compiler_params=pltpu.CompilerParams(dimension_semantics=("arbitrary",), vmem_limit_bytes=VMEM_LIMIT),
    )(cidx, *[x for pair in zip(gs, r1s) for x in pair])


def _chip_sums(ps, r2s, idx, name):
    n = len(ps)
    tiles = [_row_tile(p.shape[1], p.shape[2], 4, SUM_BLOCK_BYTES) for p in ps]
    total, own, mine = _turns([p.shape[1] // tr for p, tr in zip(ps, tiles)])

    def body(i_ref, *refs):
        s = pl.program_id(0)
        for a in range(n):
            p_ref, r_ref, o_ref = refs[2 * a], refs[2 * a + 1], refs[2 * n + a]

            @pl.when(mine[a](s))
            def _(p_ref=p_ref, r_ref=r_ref, o_ref=o_ref):
                acc = p_ref[...].astype(F32)
                for k in range(N_CHIPS - 1):
                    acc = acc + r_ref[k].astype(F32)
                o_ref[...] = acc

    in_specs, out_specs = [], []
    for a, (p, tr) in enumerate(zip(ps, tiles)):
        ccols = p.shape[2]
        in_specs += [pl.BlockSpec((None, tr, ccols), lambda s, i_ref, a=a: (i_ref[0], own[a](s), 0)),
                     pl.BlockSpec((N_CHIPS - 1, tr, ccols), lambda s, i_ref, a=a: (0, own[a](s), 0))]
        out_specs.append(pl.BlockSpec((None, tr, ccols), lambda s, i_ref, a=a: (i_ref[1], own[a](s), 0)))
    return pl.pallas_call(
        body, name=name, out_shape=[jax.ShapeDtypeStruct((2, p.shape[1], p.shape[2]), F32) for p in ps],
        grid_spec=pltpu.PrefetchScalarGridSpec(num_scalar_prefetch=1, grid=(total,), in_specs=in_specs, out_specs=out_specs),
        compiler_params=pltpu.CompilerParams(dimension_semantics=("arbitrary",), vmem_limit_bytes=VMEM_LIMIT),
    )(idx, *[x for pair in zip(ps, r2s) for x in pair])


def _adam_math(w, g, m, v):
    m2 = ADAM_B1 * m + (1.0 - ADAM_B1) * g
    v2 = ADAM_B2 * v + (1.0 - ADAM_B2) * (g * g)
    m_hat = m2 / (1.0 - ADAM_B1 ** ADAM_STEP)
    v_hat = v2 / (1.0 - ADAM_B2 ** ADAM_STEP)
    delta = -ADAM_LR * (m_hat / (jnp.sqrt(v_hat) + ADAM_EPS) + ADAM_WD * w)
    return delta, m2, v2


def _adamw_layers(w, m, v, gs, name):
    nl, r, ccols = w.shape
    ng = len(gs)
    tr = _row_tile(r, ccols, 4, ADAM_BLOCK_BYTES)
    nt = r // tr

    def body(w_ref, m_ref, v_ref, *rest):
        g_refs, (go_ref, d_ref, mo_ref, vo_ref) = rest[:ng], rest[ng:]
        l = pl.program_id(0)
        g = g_refs[0][...]
        for k in range(1, ng):
            g = jnp.where(l == k, g_refs[k][...], g)
        delta, m2, v2 = _adam_math(w_ref[...], g, m_ref[...], v_ref[...])
        go_ref[...] = g
        d_ref[...] = delta
        mo_ref[...] = m2
        vo_ref[...] = v2

    big = pl.BlockSpec((None, tr, ccols), lambda l, i: (l, i, 0))

    def gspec(k):
        return pl.BlockSpec((tr, ccols), lambda l, i: (jnp.where(l == k, i, jnp.where(l < k, 0, nt - 1)), 0))

    assert ng == nl
    return _grid_call(body, name, (nl, nt), in_specs=[big, big, big] + [gspec(k) for k in range(ng)],
                      out_specs=[big, big, big, big], out_shape=[jax.ShapeDtypeStruct(w.shape, F32)] * 4,
                      scratch_shapes=[], args=(w, m, v, *gs))[0]


def _adamw_flat(w, g, m, v, name):
    r, ccols = w.shape

    def body(w_ref, g_ref, m_ref, v_ref, d_ref, mo_ref, vo_ref):
        delta, m2, v2 = _adam_math(w_ref[...], g_ref[...], m_ref[...], v_ref[...])
        d_ref[...] = delta
        mo_ref[...] = m2
        vo_ref[...] = v2

    return pl.pallas_call(
        body, name=name, out_shape=[jax.ShapeDtypeStruct((r, ccols), F32)] * 3,
        in_specs=[_VMEM] * 4, out_specs=[_VMEM] * 3,
        compiler_params=pltpu.CompilerParams(vmem_limit_bytes=VMEM_LIMIT),
    )(w, g, m, v)


def _ada_forward(c_all, ada_w, ada_b_cols, name, exchange=None):
    nl, d, ncols = ada_w.shape
    bg = c_all.shape[0]
    tn = 512 if ncols % 512 == 0 else ncols

    def body(c_ref, w_ref, b_ref, o_ref):
        cv = c_ref[...]
        ca = (cv * _sigmoid(cv)).astype(BF16)
        o_ref[...] = jnp.dot(ca, w_ref[...].astype(BF16), preferred_element_type=F32) + b_ref[...]

    outs, got = _grid_call(
        body, name, (nl, ncols // tn),
        in_specs=[pl.BlockSpec((bg, d), lambda l, j: (0, 0)),
                  pl.BlockSpec((None, d, tn), lambda l, j: (l, 0, j)),
                  pl.BlockSpec((None, 1, tn), lambda l, j: (l, 0, j))],
        out_specs=[pl.BlockSpec((None, bg, tn), lambda l, j: (l, 0, j))],
        out_shape=[jax.ShapeDtypeStruct((nl, bg, ncols), F32)], scratch_shapes=[], args=(c_all, ada_w, ada_b_cols),
        exchange=exchange)
    return outs[0], got


def _ada_update(c_all, dmod_cols, w, m, v, name, exchange=None):
    nl, d, ncols = w.shape
    bg = c_all.shape[0]
    tn = 512 if ncols % 512 == 0 else ncols

    def body(c_ref, dm_ref, w_ref, m_ref, v_ref, go_ref, d_ref, mo_ref, vo_ref):
        cv = c_ref[...]
        ca = (cv * _sigmoid(cv)).astype(BF16)
        g = lax.dot_general(ca, dm_ref[...].astype(BF16), (((0,), (0,)), ((), ())), preferred_element_type=F32)
        delta, m2, v2 = _adam_math(w_ref[...], g, m_ref[...], v_ref[...])
        go_ref[...] = g
        d_ref[...] = delta
        mo_ref[...] = m2
        vo_ref[...] = v2

    big = pl.BlockSpec((None, d, tn), lambda l, j: (l, 0, j))
    return _grid_call(
        body, name, (nl, ncols // tn),
        in_specs=[pl.BlockSpec((bg, d), lambda l, j: (0, 0)),
                  pl.BlockSpec((None, bg, tn), lambda l, j: (l, 0, j)), big, big, big],
        out_specs=[big, big, big, big], out_shape=[jax.ShapeDtypeStruct(w.shape, F32)] * 4,
        scratch_shapes=[], args=(c_all, dmod_cols, w, m, v), exchange=exchange)


def _load_weights(first, pairs, sems):
    @pl.when(first)
    def _():
        cps = [pltpu.make_async_copy(src, dst, sems.at[k]) for k, (src, dst) in enumerate(pairs)]
        for cp in cps:
            cp.start()
        for cp in cps:
            cp.wait()


def _ada_norm(xv, g, sc, sh):
    r = lax.rsqrt(jnp.mean(xv * xv, axis=-1, keepdims=True) + EPS)
    xn = xv * r
    return (xn * g) * (1.0 + sc) + sh, xn, r


def _ada_norm_bwd(dh, xn, r, g, sc):
    d_sh = _colsum(dh)
    d_sc = _colsum(dh * (xn * g))
    dxg = dh * (1.0 + sc)
    d_g = _colsum(dxg * xn)
    gd = dxg * g
    dx = r * (gd - xn * jnp.mean(gd * xn, axis=-1, keepdims=True))
    return dx, d_sh, d_sc, d_g


def _gated_residual_bwd(dxo, o, g_post, gt):
    r = lax.rsqrt(jnp.mean(o * o, axis=-1, keepdims=True) + EPS)
    on = o * r
    d_gt = _colsum(dxo * (on * g_post))
    dy = dxo * (1.0 + gt)
    d_gp = _colsum(dy * on)
    gd = dy * g_post
    do = r * (gd - on * jnp.mean(gd * on, axis=-1, keepdims=True))
    return do, d_gt, d_gp


def _seq_positions(i, tm, width):
    return i * tm + lax.broadcasted_iota(jnp.int32, (tm, width), 0)


def _fill_phases(ext, phases):
    rows = ext.shape[0]
    ev = ext[...]
    for r in range(1, SUBLANES):
        phases[r - 1] = pltpu.roll(ev, rows - r, axis=0)


def _shifted_rows(ext, phases, offset, n):
    q, r = divmod(offset, SUBLANES)
    if r == 0:
        return ext[pl.ds(q * SUBLANES, n), :]
    return phases[r - 1, pl.ds(q * SUBLANES, n), :]


def _rows_before(halo, cur, shift):
    e = jnp.concatenate([halo, cur], axis=0)
    return pltpu.roll(e, shift, axis=0)[halo.shape[0]:, :]


def _rows_after(cur, halo, shift):
    e = jnp.concatenate([cur, halo], axis=0)
    return pltpu.roll(e, e.shape[0] - shift, axis=0)[:cur.shape[0], :]


def _mixer_forward(x, mod, vec_d, vec_c, cw, pw, win_g, wout_g, taps, tm, name, exchange=None):
    nb, s, d = x.shape
    n = s // tm
    nj, _, dcol = win_g.shape
    din = nj * dcol
    dc = vec_c.shape[-1]
    dpool = din - 2 * dc
    dmix = dc + dpool
    ro = wout_g.shape[1]
    ngrp = dpool // LANES

    def body(x_ref, mod_ref, vd_ref, vc_ref, cw_ref, pw_ref, win_hbm, wout_hbm,
             xo_ref, h_ref, u_ref, ac_ref, dp_ref, z_ref, o_ref,
             win_v, wout_v, ext_a, ext_p, phases, sems):
        b, i = pl.program_id(0), pl.program_id(1)
        pairs = [(win_hbm.at[j], win_v.at[:, pl.ds(j * dcol, dcol)]) for j in range(nj)]
        pairs += [(wout_hbm.at[j], wout_v.at[pl.ds(j * ro, ro), :]) for j in range(nj)]
        _load_weights((b == 0) & (i == 0), pairs, sems)

        xv = x_ref[...]
        h, _, _ = _ada_norm(xv, vd_ref[0:1, :], mod_ref[1:2, :], mod_ref[0:1, :])
        hb = h.astype(BF16)
        h_ref[...] = hb
        u = jnp.dot(hb, win_v[...], preferred_element_type=F32)
        u_ref[...] = u.astype(BF16)
        ag = u[:, :dc] * _sigmoid(u[:, dc:2 * dc])
        up = u[:, 2 * dc:]

        @pl.when(i == 0)
        def _():
            ext_a[0:HALO, :] = jnp.zeros((HALO, dc), F32)
            ext_p[0:HALO, :] = jnp.zeros((HALO, dpool), F32)

        @pl.when(i > 0)
        def _():
            ext_a[0:HALO, :] = ext_a[tm:tm + HALO, :]
            ext_p[0:HALO, :] = ext_p[tm:tm + HALO, :]

        ext_a[HALO:HALO + tm, :] = ag
        ext_p[HALO:HALO + tm, :] = up

        acc = jnp.broadcast_to(vc_ref[0:1, :], (tm, dc))
        _fill_phases(ext_a, phases)
        for k in range(taps):
            acc = acc + cw_ref[k:k + 1, :] * _shifted_rows(ext_a, phases, HALO - (taps - 1) + k, tm)
        ac_ref[...] = acc.astype(BF16)
        mu = jnp.mean(acc, axis=-1, keepdims=True)
        xc = acc - mu
        var = jnp.mean(xc * xc, axis=-1, keepdims=True)
        al = (xc * lax.rsqrt(var + EPS)) * vc_ref[1:2, :] + vc_ref[2:3, :]
        a = al * _sigmoid(al)

        pos = _seq_positions(i, tm, LANES)
        parts = [a.astype(BF16)]
        for g in range(ngrp):
            w = POOL_WINDOWS[g]
            cols = slice(g * LANES, (g + 1) * LANES)
            sw = ext_p[:, cols]
            step = 1
            while step < w:
                sw = sw + pltpu.roll(sw, step, axis=0)
                step *= 2
            cnt = jnp.minimum(pos + 1, w).astype(F32)
            dg = (sw[HALO:, :] / cnt - up[:, cols]).astype(BF16)
            dp_ref[:, cols] = dg
            q = jnp.dot(dg, pw_ref[g], preferred_element_type=F32)
            parts.append((q * vc_ref[3:4, cols]).astype(BF16))
        z = jnp.concatenate(parts, axis=-1)
        z_ref[...] = z
        o = jnp.dot(z, wout_v[...], preferred_element_type=F32)
        o_ref[...] = o
        r2 = lax.rsqrt(jnp.mean(o * o, axis=-1, keepdims=True) + EPS)
        xo_ref[...] = xv + (1.0 + mod_ref[2:3, :]) * ((o * r2) * vd_ref[1:2, :])

    def tile(width):
        return pl.BlockSpec((None, tm, width), lambda b, i: (b, i, 0))

    return _grid_call(
        body, name, (nb, n),
        in_specs=[tile(d), pl.BlockSpec((None, 8, d), lambda b, i: (b, 0, 0)), _full(vec_d.shape), _full(vec_c.shape),
                  _full(cw.shape), _full(pw.shape), _ANY, _ANY],
        out_specs=[tile(d), tile(d), tile(din), tile(dc), tile(dpool), tile(dmix), tile(d)],
        out_shape=[jax.ShapeDtypeStruct((nb, s, d), F32), jax.ShapeDtypeStruct((nb, s, d), BF16),
                   jax.ShapeDtypeStruct((nb, s, din), BF16), jax.ShapeDtypeStruct((nb, s, dc), BF16),
                   jax.ShapeDtypeStruct((nb, s, dpool), BF16), jax.ShapeDtypeStruct((nb, s, dmix), BF16),
                   jax.ShapeDtypeStruct((nb, s, d), F32)],
        scratch_shapes=[pltpu.VMEM((d, din), BF16), pltpu.VMEM((dmix, d), BF16),
                        pltpu.VMEM((HALO + tm, dc), F32), pltpu.VMEM((HALO + tm, dpool), F32),
                        pltpu.VMEM((SUBLANES - 1, HALO + tm, dc), F32), pltpu.SemaphoreType.DMA((2 * nj,))],
        args=(x, mod, vec_d, vec_c, cw, pw, win_g, wout_g), exchange=exchange)


def _mixer_backward(dxo, x, o, u, ac, dpl, mod, vec_d, vec_c, cw, pw, win_g, wout_g, taps, tm, name, exchange=None):
    nb, s, d = x.shape
    n = s // tm
    nj, _, dcol = win_g.shape
    din = nj * dcol
    dc = vec_c.shape[-1]
    dpool = din - 2 * dc
    dmix = dc + dpool
    ro = wout_g.shape[1]
    ngrp = dpool // LANES
    rext = tm + HALO

    def body(dxo_ref, x_ref, o_ref, u_ref, ac_ref, dp_ref, mod_ref, vd_ref, vc_ref, cw_ref, pw_ref, win_hbm, wout_hbm,
             dx_ref, du_ref, dob_ref, rowd_ref, rowb_ref, rowc_ref, dcw_ref, dpw_ref,
             win_v, wout_v, ext_a, ext_p, phases, sems):
        b, i = pl.program_id(0), pl.program_id(1)
        first = (b == 0) & (i == 0)
        pairs = [(win_hbm.at[j], win_v.at[:, pl.ds(j * dcol, dcol)]) for j in range(nj)]
        pairs += [(wout_hbm.at[j], wout_v.at[pl.ds(j * ro, ro), :]) for j in range(nj)]
        _load_weights(first, pairs, sems)

        @pl.when(first)
        def _():
            rowd_ref[...] = jnp.zeros_like(rowd_ref)
            rowc_ref[...] = jnp.zeros_like(rowc_ref)
            dcw_ref[...] = jnp.zeros_like(dcw_ref)
            dpw_ref[...] = jnp.zeros_like(dpw_ref)

        @pl.when(i == 0)
        def _():
            rowb_ref[...] = jnp.zeros_like(rowb_ref)
            ext_a[tm:rext, :] = jnp.zeros((HALO, dc), F32)
            ext_p[tm:rext, :] = jnp.zeros((HALO, dpool), F32)

        @pl.when(i > 0)
        def _():
            ext_a[tm:rext, :] = ext_a[0:HALO, :]
            ext_p[tm:rext, :] = ext_p[0:HALO, :]

        g_pre, g_post = vd_ref[0:1, :], vd_ref[1:2, :]
        sh, sc, gt = mod_ref[0:1, :], mod_ref[1:2, :], mod_ref[2:3, :]
        do, d_gt, d_gp = _gated_residual_bwd(dxo_ref[...], o_ref[...], g_post, gt)
        dob = do.astype(BF16)
        dob_ref[...] = dob
        dz = lax.dot_general(dob, wout_v[...], (((1,), (1,)), ((), ())), preferred_element_type=F32)

        acv = ac_ref[...].astype(F32)
        mu = jnp.mean(acv, axis=-1, keepdims=True)
        xc = acv - mu
        rstd = lax.rsqrt(jnp.mean(xc * xc, axis=-1, keepdims=True) + EPS)
        an = xc * rstd
        lg = vc_ref[1:2, :]
        al = an * lg + vc_ref[2:3, :]
        sg = _sigmoid(al)
        dal = dz[:, :dc] * (sg * (1.0 + al * (1.0 - sg)))
        d_lg = _colsum(dal * an)
        d_lb = _colsum(dal)
        dan = dal * lg
        dac = rstd * (dan - jnp.mean(dan, axis=-1, keepdims=True) - an * jnp.mean(dan * an, axis=-1, keepdims=True))
        d_cb = _colsum(dac)
        ext_a[0:tm, :] = dac
        uv = u_ref[:, 0:dc].astype(F32)
        sgu = _sigmoid(u_ref[:, dc:2 * dc].astype(F32))
        ag = uv * sgu
        dag = jnp.zeros((tm, dc), F32)
        _fill_phases(ext_a, phases)
        for k in range(taps):
            sl = _shifted_rows(ext_a, phases, taps - 1 - k, tm)
            dag = dag + cw_ref[k:k + 1, :] * sl
            dcw_ref[k:k + 1, :] += _colsum(ag * sl)
        du_ref[:, 0:dc] = (dag * sgu).astype(BF16)
        du_ref[:, dc:2 * dc] = (dag * uv * (sgu * (1.0 - sgu))).astype(BF16)

        pos = _seq_positions(n - 1 - i, tm, LANES)
        d_ps = []
        for g in range(ngrp):
            w = POOL_WINDOWS[g]
            cols = slice(g * LANES, (g + 1) * LANES)
            gcols = slice(dc + g * LANES, dc + (g + 1) * LANES)
            dgb = dp_ref[:, cols]
            q = jnp.dot(dgb, pw_ref[g], preferred_element_type=F32)
            dpg = dz[:, gcols]
            d_ps.append(_colsum(dpg * q))
            dq = (dpg * vc_ref[3:4, cols]).astype(BF16)
            dpw_ref[g] += lax.dot_general(dgb, dq, (((0,), (0,)), ((), ())), preferred_element_type=F32)
            dd = lax.dot_general(dq, pw_ref[g], (((1,), (1,)), ((), ())), preferred_element_type=F32)
            cnt = jnp.minimum(pos + 1, w).astype(F32)
            ext_p[0:tm, cols] = dd / cnt
            sw = ext_p[:, cols]
            step = 1
            while step < w:
                sw = sw + pltpu.roll(sw, rext - step, axis=0)
                step *= 2
            du_ref[:, 2 * dc + g * LANES:2 * dc + (g + 1) * LANES] = (sw[0:tm, :] - dd).astype(BF16)
        rowc_ref[0:1, :] += d_cb
        rowc_ref[1:2, :] += d_lg
        rowc_ref[2:3, :] += d_lb
        rowc_ref[3:4, :] += jnp.concatenate(d_ps, axis=-1)

        dh = lax.dot_general(du_ref[...], win_v[...], (((1,), (1,)), ((), ())), preferred_element_type=F32)
        _, xn, r1 = _ada_norm(x_ref[...], g_pre, sc, sh)
        dxb, d_sh, d_sc, d_g = _ada_norm_bwd(dh, xn, r1, g_pre, sc)
        dx_ref[...] = dxo_ref[...] + dxb
        rowd_ref[0:1, :] += d_g
        rowd_ref[1:2, :] += d_gp
        rowb_ref[0:1, :] += d_sh
        rowb_ref[1:2, :] += d_sc
        rowb_ref[2:3, :] += d_gt

    def tile(width):
        return pl.BlockSpec((None, tm, width), lambda b, i: (b, n - 1 - i, 0))

    return _grid_call(
        body, name, (nb, n),
        in_specs=[tile(d), tile(d), tile(d), tile(din), tile(dc), tile(dpool),
                  pl.BlockSpec((None, 8, d), lambda b, i: (b, 0, 0)), _full(vec_d.shape), _full(vec_c.shape),
                  _full(cw.shape), _full(pw.shape), _ANY, _ANY],
        out_specs=[tile(d), tile(din), tile(d), _full((8, d)), pl.BlockSpec((None, 8, d), lambda b, i: (b, 0, 0)),
                   _full((8, dc)), _full((HALO, dc)), _full(pw.shape)],
        out_shape=[jax.ShapeDtypeStruct((nb, s, d), F32), jax.ShapeDtypeStruct((nb, s, din), BF16),
                   jax.ShapeDtypeStruct((nb, s, d), BF16), jax.ShapeDtypeStruct((8, d), F32),
                   jax.ShapeDtypeStruct((nb, 8, d), F32), jax.ShapeDtypeStruct((8, dc), F32),
                   jax.ShapeDtypeStruct((HALO, dc), F32), jax.ShapeDtypeStruct(pw.shape, F32)],
        scratch_shapes=[pltpu.VMEM((d, din), BF16), pltpu.VMEM((dmix, d), BF16),
                        pltpu.VMEM((rext, dc), F32), pltpu.VMEM((rext, dpool), F32),
                        pltpu.VMEM((SUBLANES - 1, rext, dc), F32), pltpu.SemaphoreType.DMA((2 * nj,))],
        args=(dxo, x, o, u, ac, dpl, mod, vec_d, vec_c, cw, pw, win_g, wout_g), exchange=exchange)


def _ffn_forward(x, mod, vec_d, fw, wup_g, wdn_g, tm, name, exchange=None, target=None):
    nb, s, d = x.shape
    n = s // tm
    nj, _, ucol = wup_g.shape
    f2 = nj * ucol
    dff = f2 // 2
    rd = wdn_g.shape[1]
    nq = nj // 2
    cs = dff // nq

    with_loss = target is not None

    def body(*refs):
        refs = list(refs)
        x_ref, mod_ref, vd_ref, fw_ref = refs[:4]
        t_ref = refs.pop(4) if with_loss else None
        wup_hbm, wdn_hbm, xo_ref, h_ref, u_ref, uc_ref, hid_ref, o_ref = refs[4:12]
        sq_ref = refs.pop(12) if with_loss else None
        wup_v, wdn_v, prev_u, sems = refs[12:]
        b, i = pl.program_id(0), pl.program_id(1)
        pairs = [(wup_hbm.at[j], wup_v.at[:, pl.ds(j * ucol, ucol)]) for j in range(nj)]
        pairs += [(wdn_hbm.at[j], wdn_v.at[pl.ds(j * rd, rd), :]) for j in range(nj)]
        _load_weights((b == 0) & (i == 0), pairs, sems)

        if with_loss:
            @pl.when((b == 0) & (i == 0))
            def _():
                sq_ref[...] = jnp.zeros_like(sq_ref)

        @pl.when(i == 0)
        def _():
            prev_u[...] = jnp.zeros_like(prev_u)

        xv = x_ref[...]
        h, _, _ = _ada_norm(xv, vd_ref[2:3, :], mod_ref[4:5, :], mod_ref[3:4, :])
        hb = h.astype(BF16)
        h_ref[...] = hb

        def conv(cols):
            uc = jnp.dot(hb, wup_v[:, cols], preferred_element_type=F32)
            u_ref[:, cols] = uc.astype(BF16)
            before = prev_u[:, cols]
            prev_u[:, cols] = uc[tm - FHALO:, :]
            out = (fw_ref[3:4, cols] + fw_ref[2:3, cols] * uc + fw_ref[1:2, cols] * _rows_before(before, uc, 1)
                   + fw_ref[0:1, cols] * _rows_before(before, uc, 2))
            uc_ref[:, cols] = out.astype(BF16)
            return out

        o = jnp.zeros((tm, d), F32)
        for q in range(nq):
            val = conv(pl.ds(q * cs, cs))
            gate = conv(pl.ds(dff + q * cs, cs))
            hid = ((gate * _sigmoid(gate)) * val).astype(BF16)
            hid_ref[:, pl.ds(q * cs, cs)] = hid
            o = o + jnp.dot(hid, wdn_v[pl.ds(q * cs, cs), :], preferred_element_type=F32)
        o_ref[...] = o
        r2 = lax.rsqrt(jnp.mean(o * o, axis=-1, keepdims=True) + EPS)
        y = xv + (1.0 + mod_ref[5:6, :]) * ((o * r2) * vd_ref[3:4, :])
        if with_loss:
            e = y - t_ref[...]
            xo_ref[...] = e * (1.0 / d)
            sq_ref[0:1, :] += _colsum(e * e)
        else:
            xo_ref[...] = y

    def tile(width):
        return pl.BlockSpec((None, tm, width), lambda b, i: (b, i, 0))

    loss_in = [tile(d)] if with_loss else []
    return _grid_call(
        body, name, (nb, n),
        in_specs=[tile(d), pl.BlockSpec((None, 8, d), lambda b, i: (b, 0, 0)), _full(vec_d.shape), _full(fw.shape)]
        + loss_in + [_ANY, _ANY],
        out_specs=[tile(d), tile(d), tile(f2), tile(f2), tile(dff), tile(d)] + ([_full((8, d))] if with_loss else []),
        out_shape=[jax.ShapeDtypeStruct((nb, s, d), F32), jax.ShapeDtypeStruct((nb, s, d), BF16),
                   jax.ShapeDtypeStruct((nb, s, f2), BF16), jax.ShapeDtypeStruct((nb, s, f2), BF16),
                   jax.ShapeDtypeStruct((nb, s, dff), BF16), jax.ShapeDtypeStruct((nb, s, d), F32)]
        + ([jax.ShapeDtypeStruct((8, d), F32)] if with_loss else []),
        scratch_shapes=[pltpu.VMEM((d, f2), BF16), pltpu.VMEM((dff, d), BF16),
                        pltpu.VMEM((FHALO, f2), F32), pltpu.SemaphoreType.DMA((2 * nj,))],
        args=(x, mod, vec_d, fw) + ((target,) if with_loss else ()) + (wup_g, wdn_g), exchange=exchange)


def _ffn_backward(dxo, x, o, u, uc, mod, vec_d, fw, wup_g, wdn_g, tm, name, exchange=None):
    nb, s, d = x.shape
    n = s // tm
    nj, _, ucol = wup_g.shape
    f2 = nj * ucol
    dff = f2 // 2
    rd = wdn_g.shape[1]
    nq = nj // 2
    cs = dff // nq

    def body(dxo_ref, x_ref, o_ref, u_ref, uc_ref, mod_ref, vd_ref, fw_ref, wup_hbm, wdn_hbm,
             dx_ref, du_ref, dob_ref, rowd_ref, rowb_ref, dfw_ref,
             wup_v, wdn_v, next_d, sems):
        b, i = pl.program_id(0), pl.program_id(1)
        first = (b == 0) & (i == 0)
        pairs = [(wup_hbm.at[j], wup_v.at[:, pl.ds(j * ucol, ucol)]) for j in range(nj)]
        pairs += [(wdn_hbm.at[j], wdn_v.at[pl.ds(j * rd, rd), :]) for j in range(nj)]
        _load_weights(first, pairs, sems)

        @pl.when(first)
        def _():
            rowd_ref[...] = jnp.zeros_like(rowd_ref)
            dfw_ref[...] = jnp.zeros_like(dfw_ref)

        @pl.when(i == 0)
        def _():
            rowb_ref[...] = jnp.zeros_like(rowb_ref)
            next_d[...] = jnp.zeros_like(next_d)

        g_pre, g_post = vd_ref[2:3, :], vd_ref[3:4, :]
        sh, sc, gt = mod_ref[3:4, :], mod_ref[4:5, :], mod_ref[5:6, :]
        do, d_gt, d_gp = _gated_residual_bwd(dxo_ref[...], o_ref[...], g_post, gt)
        dob = do.astype(BF16)
        dob_ref[...] = dob

        hm = tm // 2

        def conv_bwd(cols, rows, duc, after):
            uc = u_ref[rows, cols].astype(F32)
            d1 = _rows_after(duc, after, 1)
            d2 = _rows_after(duc, after, 2)
            dfw_ref[3:4, cols] += _colsum(duc)
            dfw_ref[2:3, cols] += _colsum(uc * duc)
            dfw_ref[1:2, cols] += _colsum(uc * d1)
            dfw_ref[0:1, cols] += _colsum(uc * d2)
            du_ref[rows, cols] = (fw_ref[2:3, cols] * duc + fw_ref[1:2, cols] * d1 + fw_ref[0:1, cols] * d2).astype(BF16)

        dh = jnp.zeros((tm, d), F32)
        for q in range(nq):
            vcols = pl.ds(q * cs, cs)
            gcols = pl.ds(dff + q * cs, cs)
            dhid = lax.dot_general(dob, wdn_v[vcols, :], (((1,), (1,)), ((), ())), preferred_element_type=F32)
            after_v, after_g = next_d[:, vcols], next_d[:, gcols]
            for lo in (hm, 0):
                rows = pl.ds(lo, hm)
                dhid_h = dhid[lo:lo + hm, :]
                val = uc_ref[rows, vcols].astype(F32)
                gate = uc_ref[rows, gcols].astype(F32)
                sg = _sigmoid(gate)
                act = gate * sg
                dval = dhid_h * act
                dgate = (dhid_h * val) * (sg + act * (1.0 - sg))
                conv_bwd(vcols, rows, dval, after_v)
                conv_bwd(gcols, rows, dgate, after_g)
                after_v, after_g = dval[0:FHALO, :], dgate[0:FHALO, :]
            next_d[:, vcols] = after_v
            next_d[:, gcols] = after_g
            dh = dh + lax.dot_general(du_ref[:, vcols], wup_v[:, vcols], (((1,), (1,)), ((), ())), preferred_element_type=F32)
            dh = dh + lax.dot_general(du_ref[:, gcols], wup_v[:, gcols], (((1,), (1,)), ((), ())), preferred_element_type=F32)

        _, xn, r1 = _ada_norm(x_ref[...], g_pre, sc, sh)
        dxb, d_sh, d_sc, d_g = _ada_norm_bwd(dh, xn, r1, g_pre, sc)
        dx_ref[...] = dxo_ref[...] + dxb
        rowd_ref[2:3, :] += d_g
        rowd_ref[3:4, :] += d_gp
        rowb_ref[3:4, :] += d_sh
        rowb_ref[4:5, :] += d_sc
        rowb_ref[5:6, :] += d_gt

    def tile(width):
        return pl.BlockSpec((None, tm, width), lambda b, i: (b, n - 1 - i, 0))

    return _grid_call(
        body, name, (nb, n),
        in_specs=[tile(d), tile(d), tile(d), tile(f2), tile(f2), pl.BlockSpec((None, 8, d), lambda b, i: (b, 0, 0)),
                  _full(vec_d.shape), _full(fw.shape), _ANY, _ANY],
        out_specs=[tile(d), tile(f2), tile(d), _full((8, d)), pl.BlockSpec((None, 8, d), lambda b, i: (b, 0, 0)),
                   _full(fw.shape)],
        out_shape=[jax.ShapeDtypeStruct((nb, s, d), F32), jax.ShapeDtypeStruct((nb, s, f2), BF16),
                   jax.ShapeDtypeStruct((nb, s, d), BF16), jax.ShapeDtypeStruct((8, d), F32),
                   jax.ShapeDtypeStruct((nb, 8, d), F32), jax.ShapeDtypeStruct(fw.shape, F32)],
        scratch_shapes=[pltpu.VMEM((d, f2), BF16), pltpu.VMEM((dff, d), BF16),
                        pltpu.VMEM((FHALO, f2), F32), pltpu.SemaphoreType.DMA((2 * nj,))],
        args=(dxo, x, o, u, uc, mod, vec_d, fw, wup_g, wdn_g), exchange=exchange)


def _weight_grad(a, b, nblk, split, tt, name, exchange=None):
    t, ka = a.shape
    nb_ = b.shape[1]
    nk = t // tt
    if split == "cols":
        wa, wb, grid = ka, nb_ // nblk, (1, nk)
        a_spec = pl.BlockSpec((tt, ka), lambda j, k: (k, 0))
        b_spec = pl.BlockSpec((tt, nb_), lambda j, k: (k, 0))
        o_spec = pl.BlockSpec((nblk, wa, wb), lambda j, k: (0, 0, 0))
        acc_shape = (ka, nb_)
    elif split == "b":
        wa, wb, grid = ka, nb_ // nblk, (nblk, nk)
        a_spec = pl.BlockSpec((tt, wa), lambda j, k: (k, 0))
        b_spec = pl.BlockSpec((tt, wb), lambda j, k: (k, j))
        o_spec = pl.BlockSpec((None, wa, wb), lambda j, k: (j, 0, 0))
        acc_shape = (wa, wb)
    else:
        wa, wb, grid = ka // nblk, nb_, (nblk, nk)
        a_spec = pl.BlockSpec((tt, wa), lambda j, k: (k, j))
        b_spec = pl.BlockSpec((tt, wb), lambda j, k: (k, 0))
        o_spec = pl.BlockSpec((None, wa, wb), lambda j, k: (j, 0, 0))
        acc_shape = (wa, wb)

    def body(a_ref, b_ref, o_ref, acc):
        k = pl.program_id(1)
        prod = lax.dot_general(a_ref[...], b_ref[...], (((0,), (0,)), ((), ())), preferred_element_type=F32)

        @pl.when(k == 0)
        def _():
            acc[...] = prod

        @pl.when(k > 0)
        def _():
            acc[...] += prod

        @pl.when(k == nk - 1)
        def _():
            if split == "cols":
                for j in range(nblk):
                    o_ref[j] = acc[:, j * wb:(j + 1) * wb].astype(o_ref.dtype)
            else:
                o_ref[...] = acc[...].astype(o_ref.dtype)

    outs, exo = _grid_call(body, name, grid, in_specs=[a_spec, b_spec], out_specs=[o_spec],
                           out_shape=[jax.ShapeDtypeStruct((nblk, wa, wb), BF16)],
                           scratch_shapes=[pltpu.VMEM(acc_shape, F32)], args=(a, b), exchange=exchange)
    return outs[0], exo


def _rows128(a):
    return a.reshape(-1, LANES)


class _ReduceScatter:
    def __init__(self, gs, cidx, idx, tag):
        self.gs, self.cidx, self.idx, self.tag = gs, cidx, idx, tag

    def swap(self):
        return _swap_halves(self.gs)

    def after_swap(self, r1):
        self.ps = _pair_sums(self.gs, r1, self.cidx, name=f"rs_pair_{self.tag}")

    def chips(self):
        return _chip_exchange(self.ps)

    def after_chips(self, r2):
        self.fh = _chip_sums(self.ps, r2, self.idx, name=f"rs_sum_{self.tag}")

    def share(self):
        return _sibling_share(self.fh)

    @staticmethod
    def result(fs):
        return [f.reshape(f.shape[0] * f.shape[1], f.shape[2]) for f in fs]


def kernel(x, c, ada_w, ada_b, pre_mix_g, post_mix_g, w_in, conv_w, conv_b, conv_ln_g, conv_ln_b, pool_w, pool_scale, w_out, pre_ffn_g, post_ffn_g, ffn_up, ffn_conv_w, ffn_conv_b, ffn_down, loss_target, m_ada_w, m_ada_b, m_pre_mix_g, m_post_mix_g, m_w_in, m_conv_w, m_conv_b, m_conv_ln_g, m_conv_ln_b, m_pool_w, m_pool_scale, m_w_out, m_pre_ffn_g, m_post_ffn_g, m_ffn_up, m_ffn_conv_w, m_ffn_conv_b, m_ffn_down, v_ada_w, v_ada_b, v_pre_mix_g, v_post_mix_g, v_w_in, v_conv_w, v_conv_b, v_conv_ln_g, v_conv_ln_b, v_pool_w, v_pool_scale, v_w_out, v_pre_ffn_g, v_post_ffn_g, v_ffn_up, v_ffn_conv_w, v_ffn_conv_b, v_ffn_down):
    nb, s, d = x.shape
    nl = w_in.shape[0]
    taps = conv_w.shape[1]
    ccol = conv_w.shape[2]
    dc = conv_b.shape[1]
    fcol = ffn_conv_w.shape[2]
    f2 = ffn_conv_b.shape[1]
    nmod = ada_b.shape[1] // d
    acol = ada_w.shape[2]
    tm = min(MLP_TILE_ROWS, s)
    tm_mix = min(MIXER_TILE_ROWS, s)
    tt = min(GRAD_CHUNK_ROWS, (nb * s) // 2)

    xi, yi, ci = _pos()
    jm = 2 * xi + yi
    cidx = jnp.reshape(ci, (1,)).astype(jnp.int32)
    idx = jnp.stack([jm, ci]).astype(jnp.int32)

    win_b, wout_b, wup_b, wdn_b = (w.astype(BF16) for w in (w_in, w_out, ffn_up, ffn_down))

    def others(l):
        return [win_b[l], wout_b[l], wdn_b[l]]

    n_cw, n_fw, n_c = nl * taps * ccol, nl * 3 * fcol, nb * d
    packed = jnp.concatenate([conv_w.reshape(-1), ffn_conv_w.reshape(-1), c.reshape(-1)])
    got = _gather8(_rows128(packed), name="gather_small").reshape(N_DEV, -1)
    chips = got[0::2]
    cw_full = chips[:, :n_cw].reshape(N_CHIPS, nl, taps, ccol).transpose(1, 2, 0, 3).reshape(nl, taps, dc)
    fw_full = chips[:, n_cw:n_cw + n_fw].reshape(N_CHIPS, nl, 3, fcol).transpose(1, 2, 0, 3).reshape(nl, 3, f2)
    c_all = got[:, n_cw + n_fw:].reshape(N_DEV * nb, d)

    ada_b_cols = lax.dynamic_slice_in_dim(ada_b, jm * acol, acol, axis=1).reshape(nl, 1, acol)
    mod_cols, first_weights = _ada_forward(c_all, ada_w, ada_b_cols, name="ada_forward",
                                           exchange=_gather(others(0), mid_at=0.9))
    by_owner = mod_cols.reshape(nl, N_DEV, nb, acol).transpose(1, 0, 2, 3).reshape(N_DEV, -1, LANES)
    mod_own = _rows_to_owners(by_owner, name="mod_to_owners").reshape(N_CHIPS, nl, nb, acol)
    mod_own = mod_own.transpose(1, 2, 0, 3).reshape(nl, nb, nmod, d)
    mod_own = jnp.pad(mod_own, ((0, 0), (0, 0), (0, 8 - nmod), (0, 0)))

    vec_d = jnp.stack([pre_mix_g, post_mix_g, pre_ffn_g, post_ffn_g], axis=1)
    vec_c = jnp.stack([conv_b, conv_ln_g, conv_ln_b, pool_scale], axis=1)
    cw_pad = jnp.pad(cw_full, ((0, 0), (0, HALO - taps), (0, 0)))
    fw_rows = jnp.concatenate([fw_full, ffn_conv_b[:, None, :], jnp.zeros((nl, 4, f2), F32)], axis=1)
    pw_b = pool_w.astype(BF16)

    win_g, wout_g, wdn_g = _whole(first_weights)
    saved = []
    xs = x
    for l in range(nl):
        (x1, h1, u1, ac1, dp1, z1, o1), got = _mixer_forward(
            xs, mod_own[l], vec_d[l], vec_c[l], cw_pad[l], pw_b[l], win_g, wout_g, taps, tm_mix, name=f"mixer_fwd_{l}",
            exchange=_gather([wup_b[l]], mid_at=0.9))
        wup_g, = _whole(got)
        last = l + 1 == nl
        (x2, h2, u2, uc2, hid2, o2, *sq), nxt = _ffn_forward(
            x1, mod_own[l], vec_d[l], fw_rows[l], wup_g, wdn_g, tm, name=f"ffn_fwd_{l}",
            exchange=None if last else _gather(others(l + 1), mid_at=0.6), target=loss_target if last else None)
        saved.append((xs, h1, u1, ac1, dp1, z1, o1, x1, h2, u2, uc2, hid2, o2, win_g, wout_g, wup_g, wdn_g))
        if not last:
            win_g, wout_g, wdn_g = _whole(nxt)
        xs = x2

    dx = xs
    loss = lax.psum(0.5 * jnp.sum(sq[0]) / d, ("x", "y", "c"))

    flat = lambda a: a.reshape(nb * s, a.shape[-1])
    small = [None] * nl
    big_mlp, big_mix = [None] * nl, [None] * nl
    mlp = mix = None
    for l in reversed(range(nl)):
        x0, h1, u1, ac1, dp1, z1, o1, x1, h2, u2, uc2, hid2, o2, win_g, wout_g, wup_g, wdn_g = saved[l]
        (dx, du2, do2, rowd2, rowb2, dfw), got = _ffn_backward(
            dx, x1, o2, u2, uc2, mod_own[l], vec_d[l], fw_rows[l], wup_g, wdn_g, tm, name=f"ffn_bwd_{l}",
            exchange=_combine([mlp.chips(), mix.swap()]) if mlp else None)
        if mlp:
            mlp.after_chips(got[:2])
            mix.after_swap(got[2:])
        g_up, got = _weight_grad(flat(h2), flat(du2), N_CHIPS, "b", 2 * tt, name=f"grad_ffn_up_{l}",
                                 exchange=_combine([mlp.share(), mix.chips()]) if mlp else None)
        if mlp:
            big_mlp[l + 1] = mlp.result(got[:2])
            mix.after_chips(got[2:])
        g_dn, got = _weight_grad(flat(hid2), flat(do2), 2, "a", 2 * tt, name=f"grad_ffn_down_{l}",
                                 exchange=_swap_halves([g_up]) if l == 0 else None)
        g_dn = g_dn.reshape(N_CHIPS, -1, d)
        mlp_above, mlp = mlp, _ReduceScatter([g_up, g_dn], cidx, idx, f"mlp_{l}")
        if l == 0:
            mlp.after_swap(list(got) + list(_run_exchange(_swap_halves([g_dn]), name="rs_swap_down_0")))
        first = mlp.swap() if l > 0 else mlp.chips()
        (dx, du1, do1, rowd1, rowb1, rowc, dcw, dpw), got = _mixer_backward(
            dx, x0, o1, u1, ac1, dp1, mod_own[l], vec_d[l], vec_c[l], cw_pad[l], pw_b[l], win_g, wout_g, taps, tm_mix,
            name=f"mixer_bwd_{l}", exchange=_combine([first, mix.share()]) if mlp_above else first)
        if l > 0:
            mlp.after_swap(got[:2])
        else:
            mlp.after_chips(got[:2])
        if mlp_above:
            big_mix[l + 1] = mix.result(got[2:])
        g_in, got = _weight_grad(flat(h1), flat(du1), N_CHIPS, "cols", tt, name=f"grad_w_in_{l}",
                                 exchange=mlp.share() if l == 0 else None)
        if l == 0:
            big_mlp[0] = mlp.result(got)
        g_out, _ = _weight_grad(flat(z1), flat(do1), 1, "cols", tt, name=f"grad_w_out_{l}")
        mix = _ReduceScatter([g_in, g_out.reshape(N_CHIPS, -1, d)], cidx, idx, f"mix_{l}")
        small[l] = dict(rowd=rowd1 + rowd2, rowb=rowb1 + rowb2, rowc=rowc, dcw=dcw[:taps], dpw=dpw, dfw=dfw)
    mix.after_swap(_run_exchange(mix.swap(), name="rs_swap_mix_0"))

    dmod_own = jnp.stack([small[l]["rowb"][:, :nmod, :] for l in range(nl)])
    dmod_all, got = _gather8(_rows128(dmod_own), name="gather_dmod", exchange=mix.chips())
    mix.after_chips(got)
    big_mix[0] = mix.result(_run_exchange(mix.share(), name="rs_share_mix_0"))
    dmod_all = dmod_all.reshape(N_DEV, nl, nb, nmod * d)
    dmod_all = dmod_all.transpose(1, 0, 2, 3).reshape(nl, N_DEV * nb, nmod * d)
    dmod_cols = lax.dynamic_slice_in_dim(dmod_all, jm * acol, acol, axis=2)
    (g_ada_w, d_ada_w, nm_ada_w, nv_ada_w), _ = _ada_update(c_all, dmod_cols, ada_w, m_ada_w, v_ada_w, name="ada_update")

    def st(key, row=None):
        return jnp.stack([small[l][key] if row is None else small[l][key][row] for l in range(nl)])

    local = {
        "ada_b": dmod_own.sum(axis=1).reshape(nl, nmod * d),
        "pre_mix_g": st("rowd", 0), "post_mix_g": st("rowd", 1),
        "conv_b": st("rowc", 0), "conv_ln_g": st("rowc", 1), "conv_ln_b": st("rowc", 2),
        "pool_w": st("dpw"), "pool_scale": st("rowc", 3),
        "pre_ffn_g": st("rowd", 2), "post_ffn_g": st("rowd", 3),
        "ffn_conv_b": st("dfw", 3), "conv_w": st("dcw"), "ffn_conv_w": jnp.stack([small[l]["dfw"][:3] for l in range(nl)]),
    }
    names = list(local)
    sizes = [local[k].size for k in names]
    pad = -sum(sizes) % (4 * SUBLANES * LANES)
    packed = jnp.concatenate([local[k].reshape(-1) for k in names] + [jnp.zeros((pad,), F32)])
    summed = _allreduce8(_rows128(packed), name="allreduce_small").reshape(-1)
    grads, off = {}, 0
    for k, sz in zip(names, sizes):
        grads[k] = summed[off:off + sz].reshape(local[k].shape)
        off += sz
    grads["conv_w"] = lax.dynamic_slice_in_dim(grads["conv_w"], jm * ccol, ccol, axis=2)
    grads["ffn_conv_w"] = lax.dynamic_slice_in_dim(grads["ffn_conv_w"], jm * fcol, fcol, axis=2)

    params = dict(ada_b=(ada_b, m_ada_b, v_ada_b), pre_mix_g=(pre_mix_g, m_pre_mix_g, v_pre_mix_g),
                  post_mix_g=(post_mix_g, m_post_mix_g, v_post_mix_g), conv_b=(conv_b, m_conv_b, v_conv_b),
                  conv_ln_g=(conv_ln_g, m_conv_ln_g, v_conv_ln_g), conv_ln_b=(conv_ln_b, m_conv_ln_b, v_conv_ln_b),
                  pool_w=(pool_w, m_pool_w, v_pool_w), pool_scale=(pool_scale, m_pool_scale, v_pool_scale),
                  pre_ffn_g=(pre_ffn_g, m_pre_ffn_g, v_pre_ffn_g), post_ffn_g=(post_ffn_g, m_post_ffn_g, v_post_ffn_g),
                  ffn_conv_b=(ffn_conv_b, m_ffn_conv_b, v_ffn_conv_b), conv_w=(conv_w, m_conv_w, v_conv_w),
                  ffn_conv_w=(ffn_conv_w, m_ffn_conv_w, v_ffn_conv_w))
    pack = lambda i, g=None: _rows128(jnp.concatenate([(grads[k] if g else params[k][i]).reshape(-1) for k in names]))
    sd, sm, sv = _adamw_flat(pack(0), pack(0, True), pack(1), pack(2), name="adamw_small")
    outs = {}
    off = 0
    for k in names:
        shape, sz = params[k][0].shape, params[k][0].size
        outs[k] = (grads[k],) + tuple(a.reshape(-1)[off:off + sz].reshape(shape) for a in (sd, sm, sv))
        off += sz

    outs["ada_w"] = (g_ada_w, d_ada_w, nm_ada_w, nv_ada_w)
    for k, w, m, v, gs in [("w_in", w_in, m_w_in, v_w_in, [big_mix[l][0] for l in range(nl)]),
                           ("w_out", w_out, m_w_out, v_w_out, [big_mix[l][1] for l in range(nl)]),
                           ("ffn_up", ffn_up, m_ffn_up, v_ffn_up, [big_mlp[l][0] for l in range(nl)]),
                           ("ffn_down", ffn_down, m_ffn_down, v_ffn_down, [big_mlp[l][1] for l in range(nl)])]:
        outs[k] = tuple(_adamw_layers(w, m, v, gs, name=f"adamw_{k}"))

    order = ["ada_w", "ada_b", "pre_mix_g", "post_mix_g", "w_in", "conv_w", "conv_b", "conv_ln_g", "conv_ln_b", "pool_w",
             "pool_scale", "w_out", "pre_ffn_g", "post_ffn_g", "ffn_up", "ffn_conv_w", "ffn_conv_b", "ffn_down"]
    return (loss, dx) + tuple(outs[k][i] for i in range(4) for k in order)
```

```python
import functools

import jax
import jax.numpy as jnp
from jax import lax
from jax.experimental import pallas as pl
from jax.experimental.pallas import tpu as pltpu

F32 = jnp.float32
BF16 = jnp.bfloat16
MESH = pl.DeviceIdType.MESH

EPS = 1e-6
POOL_WINDOWS = (2, 4, 8, 16)
ADAM_LR = 0.001
ADAM_B1 = 0.9
ADAM_B2 = 0.999
ADAM_EPS = 1e-08
ADAM_WD = 0.01
ADAM_STEP = 10

N_CHIPS = 4
N_DEV = 8
LANES = 128
SUBLANES = 8
HALO = 32
FHALO = 8
VMEM_LIMIT = 60 * 1024 * 1024
MLP_TILE_ROWS = 256
MIXER_TILE_ROWS = 512
GRAD_CHUNK_ROWS = 2048


def _pos():
    return lax.axis_index("x"), lax.axis_index("y"), lax.axis_index("c")


def _flip(v, f):
    return 1 - v if f else v


def _full(shape):
    nd = len(shape)
    return pl.BlockSpec(shape, lambda *_: (0,) * nd)


_ANY = pl.BlockSpec(memory_space=pl.ANY)
_VMEM = pl.BlockSpec(memory_space=pltpu.VMEM)


def _sigmoid(v):
    return 1.0 / (1.0 + jnp.exp(-v))


def _colsum(v):
    return jnp.sum(v, axis=0, keepdims=True)


def _gather8(v, name, exchange=None):
    r, ccols = v.shape
    ex = exchange
    nci, nco = (len(ex.ins), len(ex.outs)) if ex else (0, 0)

    def body(*refs):
        v_ref, cin, out_ref, cout = refs[0], refs[1:1 + nci], refs[1 + nci], refs[2 + nci:2 + nci + nco]
        send_sems, recv_sems, local_sem = refs[2 + nci + nco:5 + nci + nco]
        if ex:
            sems = _Sems(*refs[5 + nci + nco:])
            ex.start(cin, cout, sems)
        x, y, c = _pos()
        me = 4 * x + 2 * y + c
        mine = pltpu.make_async_copy(v_ref, out_ref.at[me], local_sem)
        mine.start()
        peers = [(_flip(x, (k >> 2) & 1), _flip(y, (k >> 1) & 1), _flip(c, k & 1)) for k in range(1, N_DEV)]
        sends = []
        for k, peer in enumerate(peers):
            cp = pltpu.make_async_remote_copy(src_ref=v_ref, dst_ref=out_ref.at[me], send_sem=send_sems.at[k],
                                              recv_sem=recv_sems.at[k], device_id=peer, device_id_type=MESH)
            cp.start()
            sends.append(cp)
        for k, peer in enumerate(peers):
            pidx = 4 * peer[0] + 2 * peer[1] + peer[2]
            pltpu.make_async_remote_copy(src_ref=v_ref, dst_ref=out_ref.at[pidx], send_sem=send_sems.at[k],
                                         recv_sem=recv_sems.at[k], device_id=peer, device_id_type=MESH).wait_recv()
        for cp in sends:
            cp.wait_send()
        mine.wait()
        if ex:
            if ex.mid is not None:
                ex.mid(cin, cout, sems)
            ex.finish(cin, cout, sems)

    outs = pl.pallas_call(
        body, name=name, out_shape=[jax.ShapeDtypeStruct((N_DEV, r, ccols), v.dtype)] + (ex.outs if ex else []),
        in_specs=[_VMEM] + [_ANY] * nci, out_specs=[_VMEM] + [_ANY] * nco,
        scratch_shapes=[pltpu.SemaphoreType.DMA((N_DEV - 1,)), pltpu.SemaphoreType.DMA((N_DEV - 1,)),
                        pltpu.SemaphoreType.DMA(())] + (ex.scratch() if ex else []),
        input_output_aliases={1 + a: 1 + b for a, b in ex.aliases.items()} if ex else {},
        compiler_params=pltpu.CompilerParams(vmem_limit_bytes=VMEM_LIMIT),
    )(v, *(ex.ins if ex else []))
    return (outs[0], list(outs[1:])) if ex else outs[0]


def _rows_to_owners(v, name):
    _, r, ccols = v.shape

    def body(v_ref, out_ref, send_sems, recv_sems, local_sem):
        x, y, c = _pos()
        jm = 2 * x + y
        mine = pltpu.make_async_copy(v_ref.at[2 * jm + c], out_ref.at[jm], local_sem)
        mine.start()
        peers, pjs = _chip_peers(x, y, c)
        sends = []
        for k, peer in enumerate(peers):
            cp = pltpu.make_async_remote_copy(src_ref=v_ref.at[2 * pjs[k] + c], dst_ref=out_ref.at[jm],
                                              send_sem=send_sems.at[k], recv_sem=recv_sems.at[k],
                                              device_id=peer, device_id_type=MESH)
            cp.start()
            sends.append(cp)
        for k, peer in enumerate(peers):
            pltpu.make_async_remote_copy(src_ref=v_ref.at[0], dst_ref=out_ref.at[pjs[k]], send_sem=send_sems.at[k],
                                         recv_sem=recv_sems.at[k], device_id=peer, device_id_type=MESH).wait_recv()
        for cp in sends:
            cp.wait_send()
        mine.wait()

    return pl.pallas_call(
        body, name=name, out_shape=jax.ShapeDtypeStruct((N_CHIPS, r, ccols), v.dtype),
        in_specs=[_VMEM], out_specs=_VMEM,
        scratch_shapes=[pltpu.SemaphoreType.DMA((N_CHIPS - 1,)), pltpu.SemaphoreType.DMA((N_CHIPS - 1,)),
                        pltpu.SemaphoreType.DMA(())],
        compiler_params=pltpu.CompilerParams(vmem_limit_bytes=VMEM_LIMIT),
    )(v)


def _allreduce8(v, name):
    r, ccols = v.shape
    h = r // 2
    q = h // 2

    def body(v_ref, out_ref, whole, part, done, send_sems, recv_sems):
        x, y, c = _pos()
        sib = (x, y, 1 - c)
        mine = pl.ds(pl.multiple_of(c * h, SUBLANES), h)
        theirs = pl.ds(pl.multiple_of((1 - c) * h, SUBLANES), h)
        quarters = [pl.ds(pl.multiple_of(c * h + k * q, SUBLANES), q) for k in range(2)]
        along_x, along_y = (1 - x, y, c), (x, 1 - y, c)

        def exchange(pairs):
            cps = [pltpu.make_async_remote_copy(src_ref=src, dst_ref=dst, send_sem=send_sems.at[k], recv_sem=recv_sems.at[k],
                                                device_id=peer, device_id_type=MESH) for src, dst, k, peer in pairs]
            for cp in cps:
                cp.start()
            for cp in cps:
                cp.wait()

        exchange([(v_ref, whole, 0, sib)])
        out_ref[...] = v_ref[...] + whole[...]
        for stage, peers in enumerate(((along_x, along_y), (along_y, along_x))):
            exchange([(out_ref.at[quarters[k]], part.at[2 * stage + k], 1 + 2 * stage + k, peers[k]) for k in range(2)])
            for k in range(2):
                out_ref[quarters[k], :] = out_ref[quarters[k], :] + part[2 * stage + k]
        exchange([(out_ref.at[mine], done, 5, sib)])
        out_ref[theirs, :] = done[...]

    return pl.pallas_call(
        body, name=name, out_shape=jax.ShapeDtypeStruct((r, ccols), v.dtype),
        in_specs=[_VMEM], out_specs=_VMEM,
        scratch_shapes=[pltpu.VMEM((r, ccols), v.dtype), pltpu.VMEM((4, q, ccols), v.dtype), pltpu.VMEM((h, ccols), v.dtype),
                        pltpu.SemaphoreType.DMA((6,)), pltpu.SemaphoreType.DMA((6,))],
        compiler_params=pltpu.CompilerParams(vmem_limit_bytes=VMEM_LIMIT),
    )(v)


def _chip_peers(x, y, c):
    peers = [(_flip(x, (k >> 1) & 1), _flip(y, k & 1), c) for k in range(1, N_CHIPS)]
    return peers, [2 * p[0] + p[1] for p in peers]


class _Exchange:
    def __init__(self, ins, outs, aliases, n_sems, n_local, start, finish, mid=None, mid_at=1.0, sibling=False, chips=False):
        self.ins, self.outs, self.aliases = list(ins), list(outs), dict(aliases)
        self.n_sems, self.n_local, self.start, self.finish = n_sems, n_local, start, finish
        self.mid, self.mid_at = mid, mid_at
        self.sibling, self.chips = sibling, chips

    def collective_id(self):
        return {(True, False): 1, (False, True): 2, (True, True): 3}[(self.sibling, self.chips)]

    def handshake(self):
        x, y, c = _pos()
        peers = ([(x, y, 1 - c)] if self.sibling else []) + (_chip_peers(x, y, c)[0] if self.chips else [])
        barrier = pltpu.get_barrier_semaphore()
        for peer in peers:
            pl.semaphore_signal(barrier, inc=1, device_id=peer, device_id_type=MESH)
        pl.semaphore_wait(barrier, len(peers))

    def scratch(self):
        return [pltpu.SemaphoreType.DMA((self.n_sems,)), pltpu.SemaphoreType.DMA((self.n_sems,)),
                pltpu.SemaphoreType.DMA((max(self.n_local, 1),))]


class _Sems:
    def __init__(self, send, recv, local, base=0, lbase=0):
        self.send, self.recv, self.loc, self.base, self.lbase = send, recv, local, base, lbase

    def shifted(self, by, lby):
        return _Sems(self.send, self.recv, self.loc, self.base + by, self.lbase + lby)

    def local(self, k):
        return self.loc.at[self.lbase + k]


def _remote(src, dst, sems, k, peer):
    return pltpu.make_async_remote_copy(src_ref=src, dst_ref=dst, send_sem=sems.send.at[sems.base + k],
                                        recv_sem=sems.recv.at[sems.base + k], device_id=peer, device_id_type=MESH)


def _combine(exs):
    ins = [a for ex in exs for a in ex.ins]
    outs = [o for ex in exs for o in ex.outs]
    aliases, spans, ni, no, ns, nloc = {}, [], 0, 0, 0, 0
    for ex in exs:
        aliases.update({ni + a: no + b for a, b in ex.aliases.items()})
        spans.append((ni, no, ns, nloc))
        ni, no, ns, nloc = ni + len(ex.ins), no + len(ex.outs), ns + ex.n_sems, nloc + ex.n_local

    def each(which):
        def run(ins_, outs_, sems):
            for ex, (i0, o0, s0, l0) in zip(exs, spans):
                stage = getattr(ex, which)
                if stage is not None:
                    stage(ins_[i0:i0 + len(ex.ins)], outs_[o0:o0 + len(ex.outs)], sems.shifted(s0, l0))
        return run

    mids = [ex.mid_at for ex in exs if ex.mid is not None]
    return _Exchange(ins, outs, aliases, ns, nloc, each("start"), each("finish"),
                     mid=each("mid") if mids else None, mid_at=max(mids) if mids else 1.0,
                     sibling=any(ex.sibling for ex in exs), chips=any(ex.chips for ex in exs))


def _gather(shards, mid_at=1.0):
    n = len(shards)
    per = N_CHIPS - 1
    halves = [s.reshape(2, s.shape[0] // 2, s.shape[1]) for s in shards]

    def copies(ins, outs, sems):
        x, y, c = _pos()
        jm = 2 * x + y
        sib = (x, y, 1 - c)
        peers, pjs = _chip_peers(x, y, c)
        sends, recvs, passes, passed = [], [], [], []
        for a in range(n):
            own = _remote(ins[a], outs[a].at[jm], sems, 2 * n * per + a, sib)
            sends.append(own)
            passed.append(own)
            for k, peer in enumerate(peers):
                landed, theirs = outs[a].at[pjs[k], c], outs[a].at[pjs[k], 1 - c]
                sends.append(_remote(ins[a].at[c], outs[a].at[jm, c], sems, 2 * (a * per + k), peer))
                recvs.append(_remote(landed, landed, sems, 2 * (a * per + k), peer))
                passes.append(_remote(landed, landed, sems, 2 * (a * per + k) + 1, sib))
                passed.append(_remote(theirs, theirs, sems, 2 * (a * per + k) + 1, sib))
        return sends, recvs, passes, passed

    def start(ins, outs, sems):
        for cp in copies(ins, outs, sems)[0]:
            cp.start()

    def mid(ins, outs, sems):
        _, recvs, passes, _ = copies(ins, outs, sems)
        for got, fwd in zip(recvs, passes):
            got.wait_recv()
            fwd.start()

    def finish(ins, outs, sems):
        sends, _, passes, passed = copies(ins, outs, sems)
        for cp in passed:
            cp.wait_recv()
        for cp in sends + passes:
            cp.wait_send()

    outs = [jax.ShapeDtypeStruct((N_CHIPS,) + h.shape, h.dtype) for h in halves]
    return _Exchange(halves, outs, {}, 2 * n * per + n, 0, start, finish, mid=mid, mid_at=mid_at, sibling=True, chips=True)


def _whole(gathered):
    return [g.reshape(g.shape[0], g.shape[1] * g.shape[2], g.shape[3]) for g in gathered]


def _swap_halves(gs):
    n = len(gs)
    halves = [g.reshape(g.shape[0], 2, g.shape[1] // 2, g.shape[2]) for g in gs]

    def copies(ins, outs, sems):
        x, y, c = _pos()
        sib = (x, y, 1 - c)
        return [_remote(ins[a].at[:, 1 - c], outs[a], sems, a, sib) for a in range(n)]

    def start(ins, outs, sems):
        for cp in copies(ins, outs, sems):
            cp.start()

    def finish(ins, outs, sems):
        for cp in copies(ins, outs, sems):
            cp.wait()

    outs = [jax.ShapeDtypeStruct((g.shape[0], g.shape[1] // 2, g.shape[2]), g.dtype) for g in gs]
    return _Exchange(halves, outs, {}, n, 0, start, finish, sibling=True)


def _chip_exchange(ps):
    n = len(ps)
    per = N_CHIPS - 1

    def copies(ins, outs, sems):
        x, y, c = _pos()
        peers, pjs = _chip_peers(x, y, c)
        return [_remote(ins[a].at[pjs[k]], outs[a].at[k], sems, a * per + k, peer)
                for a in range(n) for k, peer in enumerate(peers)]

    def start(ins, outs, sems):
        for cp in copies(ins, outs, sems):
            cp.start()

    def finish(ins, outs, sems):
        for cp in copies(ins, outs, sems):
            cp.wait()

    outs = [jax.ShapeDtypeStruct((per,) + p.shape[1:], p.dtype) for p in ps]
    return _Exchange(ps, outs, {}, n * per, 0, start, finish, chips=True)


def _sibling_share(fs):
    n = len(fs)

    def copies(outs, sems):
        x, y, c = _pos()
        sib = (x, y, 1 - c)
        sends = [_remote(outs[a].at[c], outs[a].at[c], sems, a, sib) for a in range(n)]
        recvs = [_remote(outs[a].at[1 - c], outs[a].at[1 - c], sems, a, sib) for a in range(n)]
        return sends, recvs

    def start(ins, outs, sems):
        for cp in copies(outs, sems)[0]:
            cp.start()

    def finish(ins, outs, sems):
        sends, recvs = copies(outs, sems)
        for cp in recvs:
            cp.wait_recv()
        for cp in sends:
            cp.wait_send()

    outs = [jax.ShapeDtypeStruct(f.shape, f.dtype) for f in fs]
    return _Exchange(fs, outs, {a: a for a in range(n)}, n, 0, start, finish, sibling=True)


def _run_exchange(ex, name):
    ni, no = len(ex.ins), len(ex.outs)

    def body(*refs):
        ins, outs, sems = refs[:ni], refs[ni:ni + no], _Sems(*refs[ni + no:])
        ex.handshake()
        ex.start(ins, outs, sems)
        if ex.mid is not None:
            ex.mid(ins, outs, sems)
        ex.finish(ins, outs, sems)

    return pl.pallas_call(
        body, name=name, out_shape=ex.outs, in_specs=[_ANY] * ni, out_specs=[_ANY] * no,
        input_output_aliases=ex.aliases, scratch_shapes=ex.scratch(),
        compiler_params=pltpu.CompilerParams(collective_id=ex.collective_id()),
    )(*ex.ins)


def _grid_call(body, name, grid, in_specs, out_specs, out_shape, scratch_shapes, args, exchange=None):
    ni, no = len(in_specs), len(out_specs)
    params = pltpu.CompilerParams(dimension_semantics=("arbitrary",) * len(grid), vmem_limit_bytes=VMEM_LIMIT)
    if exchange is None:
        outs = pl.pallas_call(body, name=name, grid=grid, in_specs=in_specs, out_specs=out_specs, out_shape=out_shape,
                              scratch_shapes=scratch_shapes, compiler_params=params)(*args)
        return list(outs), []
    ex = exchange
    nci, nco = len(ex.ins), len(ex.outs)

    def hosted(*refs):
        cin = refs[ni:ni + nci]
        cout = refs[ni + nci + no:ni + nci + no + nco]
        sems = _Sems(*refs[len(refs) - 3:])
        main = refs[:ni] + refs[ni + nci:ni + nci + no] + refs[ni + nci + no + nco:len(refs) - 3]
        ids = [pl.program_id(a) for a in range(len(grid))]
        first = functools.reduce(lambda p, q: p & q, [i == 0 for i in ids])
        last = functools.reduce(lambda p, q: p & q, [i == g - 1 for i, g in zip(ids, grid)])

        @pl.when(first)
        def _():
            ex.handshake()
            ex.start(cin, cout, sems)

        if ex.mid is not None:
            steps = functools.reduce(lambda p, q: p * q, grid)
            flat = functools.reduce(lambda p, q: p * q[1] + q[0], zip(ids[1:], grid[1:]), ids[0])

            @pl.when(flat == min(steps - 1, int(ex.mid_at * steps)))
            def _():
                ex.mid(cin, cout, sems)

        body(*main)

        @pl.when(last)
        def _():
            ex.finish(cin, cout, sems)

    outs = pl.pallas_call(
        hosted, name=name, grid=grid, in_specs=list(in_specs) + [_ANY] * nci, out_specs=list(out_specs) + [_ANY] * nco,
        out_shape=list(out_shape) + ex.outs, scratch_shapes=list(scratch_shapes) + ex.scratch(),
        input_output_aliases={ni + a: no + b for a, b in ex.aliases.items()},
        compiler_params=pltpu.CompilerParams(dimension_semantics=("arbitrary",) * len(grid), vmem_limit_bytes=VMEM_LIMIT,
                                             collective_id=ex.collective_id()),
    )(*args, *ex.ins)
    return list(outs[:no]), list(outs[no:])


SUM_BLOCK_BYTES = 4 * 1024 * 1024
ADAM_BLOCK_BYTES = 2 * 1024 * 1024


def _row_tile(rows, cols, itemsize, budget):
    best = None
    for t in range(16, rows + 1, 16):
        if rows % t == 0 and t * cols * itemsize <= budget:
            best = t
    return best if best is not None else rows


def _turns(step_counts):
    offs, total = [], 0
    for n in step_counts:
        offs.append(total)
        total += n
    own = [lambda s, o=o, n=n: jnp.clip(s - o, 0, n - 1) for o, n in zip(offs, step_counts)]
    mine = [lambda s, o=o, n=n: (s >= o) & (s < o + n) for o, n in zip(offs, step_counts)]
    return total, own, mine


def _pair_sums(gs, r1s, cidx, name):
    n = len(gs)
    tiles = [_row_tile(g.shape[1] // 2, g.shape[2], 4, SUM_BLOCK_BYTES) for g in gs]
    nts = [g.shape[1] // 2 // tr for g, tr in zip(gs, tiles)]
    total, own, mine = _turns([g.shape[0] * nt for g, nt in zip(gs, nts)])

    def body(c_ref, *refs):
        s = pl.program_id(0)
        for a in range(n):
            g_ref, r_ref, o_ref = refs[2 * a], refs[2 * a + 1], refs[2 * n + a]

            @pl.when(mine[a](s))
            def _(g_ref=g_ref, r_ref=r_ref, o_ref=o_ref):
                o_ref[...] = (g_ref[...].astype(F32) + r_ref[...].astype(F32)).astype(o_ref.dtype)

    in_specs, out_specs = [], []
    for a, (g, tr, nt) in enumerate(zip(gs, tiles, nts)):
        blk = (None, tr, g.shape[2])
        in_specs += [pl.BlockSpec(blk, lambda s, c_ref, a=a, nt=nt: (own[a](s) // nt, c_ref[0] * nt + own[a](s) % nt, 0)),
                     pl.BlockSpec(blk, lambda s, c_ref, a=a, nt=nt: (own[a](s) // nt, own[a](s) % nt, 0))]
        out_specs.append(pl.BlockSpec(blk, lambda s, c_ref, a=a, nt=nt: (own[a](s) // nt, own[a](s) % nt, 0)))
    return pl.pallas_call(
        body, name=name, out_shape=[jax.ShapeDtypeStruct((g.shape[0], g.shape[1] // 2, g.shape[2]), g.dtype) for g in gs],
        grid_spec=pltpu.PrefetchScalarGridSpec(num_scalar_prefetch=1, grid=(total,), in_specs=in_specs, out_specs=out_specs),
        compiler_params=pltpu.CompilerParams(dimension_semantics=("arbitrary",), vmem_limit_bytes=VMEM_LIMIT),
    )(cidx, *[x for pair in zip(gs, r1s) for x in pair])


def _chip_sums(ps, r2s, idx, name):
    n = len(ps)
    tiles = [_row_tile(p.shape[1], p.shape[2], 4, SUM_BLOCK_BYTES) for p in ps]
    total, own, mine = _turns([p.shape[1] // tr for p, tr in zip(ps, tiles)])

    def body(i_ref, *refs):
        s = pl.program_id(0)
        for a in range(n):
            p_ref, r_ref, o_ref = refs[2 * a], refs[2 * a + 1], refs[2 * n + a]

            @pl.when(mine[a](s))
            def _(p_ref=p_ref, r_ref=r_ref, o_ref=o_ref):
                acc = p_ref[...].astype(F32)
                for k in range(N_CHIPS - 1):
                    acc = acc + r_ref[k].astype(F32)
                o_ref[...] = acc

    in_specs, out_specs = [], []
    for a, (p, tr) in enumerate(zip(ps, tiles)):
        ccols = p.shape[2]
        in_specs += [pl.BlockSpec((None, tr, ccols), lambda s, i_ref, a=a: (i_ref[0], own[a](s), 0)),
                     pl.BlockSpec((N_CHIPS - 1, tr, ccols), lambda s, i_ref, a=a: (0, own[a](s), 0))]
        out_specs.append(pl.BlockSpec((None, tr, ccols), lambda s, i_ref, a=a: (i_ref[1], own[a](s), 0)))
    return pl.pallas_call(
        body, name=name, out_shape=[jax.ShapeDtypeStruct((2, p.shape[1], p.shape[2]), F32) for p in ps],
        grid_spec=pltpu.PrefetchScalarGridSpec(num_scalar_prefetch=1, grid=(total,), in_specs=in_specs, out_specs=out_specs),
        compiler_params=pltpu.CompilerParams(dimension_semantics=("arbitrary",), vmem_limit_bytes=VMEM_LIMIT),
    )(idx, *[x for pair in zip(ps, r2s) for x in pair])


def _adam_math(w, g, m, v):
    m2 = ADAM_B1 * m + (1.0 - ADAM_B1) * g
    v2 = ADAM_B2 * v + (1.0 - ADAM_B2) * (g * g)
    m_hat = m2 / (1.0 - ADAM_B1 ** ADAM_STEP)
    v_hat = v2 / (1.0 - ADAM_B2 ** ADAM_STEP)
    delta = -ADAM_LR * (m_hat / (jnp.sqrt(v_hat) + ADAM_EPS) + ADAM_WD * w)
    return delta, m2, v2


def _adamw_layers(w, m, v, gs, name):
    nl, r, ccols = w.shape
    ng = len(gs)
    tr = _row_tile(r, ccols, 4, ADAM_BLOCK_BYTES)
    nt = r // tr

    def body(w_ref, m_ref, v_ref, *rest):
        g_refs, (go_ref, d_ref, mo_ref, vo_ref) = rest[:ng], rest[ng:]
        l = pl.program_id(0)
        g = g_refs[0][...]
        for k in range(1, ng):
            g = jnp.where(l == k, g_refs[k][...], g)
        delta, m2, v2 = _adam_math(w_ref[...], g, m_ref[...], v_ref[...])
        go_ref[...] = g
        d_ref[...] = delta
        mo_ref[...] = m2
        vo_ref[...] = v2

    big = pl.BlockSpec((None, tr, ccols), lambda l, i: (l, i, 0))

    def gspec(k):
        return pl.BlockSpec((tr, ccols), lambda l, i: (jnp.where(l == k, i, jnp.where(l < k, 0, nt - 1)), 0))

    assert ng == nl
    return _grid_call(body, name, (nl, nt), in_specs=[big, big, big] + [gspec(k) for k in range(ng)],
                      out_specs=[big, big, big, big], out_shape=[jax.ShapeDtypeStruct(w.shape, F32)] * 4,
                      scratch_shapes=[], args=(w, m, v, *gs))[0]


def _adamw_flat(w, g, m, v, name):
    r, ccols = w.shape

    def body(w_ref, g_ref, m_ref, v_ref, d_ref, mo_ref, vo_ref):
        delta, m2, v2 = _adam_math(w_ref[...], g_ref[...], m_ref[...], v_ref[...])
        d_ref[...] = delta
        mo_ref[...] = m2
        vo_ref[...] = v2

    return pl.pallas_call(
        body, name=name, out_shape=[jax.ShapeDtypeStruct((r, ccols), F32)] * 3,
        in_specs=[_VMEM] * 4, out_specs=[_VMEM] * 3,
        compiler_params=pltpu.CompilerParams(vmem_limit_bytes=VMEM_LIMIT),
    )(w, g, m, v)


def _ada_forward(c_all, ada_w, ada_b_cols, name, exchange=None):
    nl, d, ncols = ada_w.shape
    bg = c_all.shape[0]
    tn = 512 if ncols % 512 == 0 else ncols

    def body(c_ref, w_ref, b_ref, o_ref):
        cv = c_ref[...]
        ca = (cv * _sigmoid(cv)).astype(BF16)
        o_ref[...] = jnp.dot(ca, w_ref[...].astype(BF16), preferred_element_type=F32) + b_ref[...]

    outs, got = _grid_call(
        body, name, (nl, ncols // tn),
        in_specs=[pl.BlockSpec((bg, d), lambda l, j: (0, 0)),
                  pl.BlockSpec((None, d, tn), lambda l, j: (l, 0, j)),
                  pl.BlockSpec((None, 1, tn), lambda l, j: (l, 0, j))],
        out_specs=[pl.BlockSpec((None, bg, tn), lambda l, j: (l, 0, j))],
        out_shape=[jax.ShapeDtypeStruct((nl, bg, ncols), F32)], scratch_shapes=[], args=(c_all, ada_w, ada_b_cols),
        exchange=exchange)
    return outs[0], got


def _ada_update(c_all, dmod_cols, w, m, v, name, exchange=None):
    nl, d, ncols = w.shape
    bg = c_all.shape[0]
    tn = 512 if ncols % 512 == 0 else ncols

    def body(c_ref, dm_ref, w_ref, m_ref, v_ref, go_ref, d_ref, mo_ref, vo_ref):
        cv = c_ref[...]
        ca = (cv * _sigmoid(cv)).astype(BF16)
        g = lax.dot_general(ca, dm_ref[...].astype(BF16), (((0,), (0,)), ((), ())), preferred_element_type=F32)
        delta, m2, v2 = _adam_math(w_ref[...], g, m_ref[...], v_ref[...])
        go_ref[...] = g
        d_ref[...] = delta
        mo_ref[...] = m2
        vo_ref[...] = v2

    big = pl.BlockSpec((None, d, tn), lambda l, j: (l, 0, j))
    return _grid_call(
        body, name, (nl, ncols // tn),
        in_specs=[pl.BlockSpec((bg, d), lambda l, j: (0, 0)),
                  pl.BlockSpec((None, bg, tn), lambda l, j: (l, 0, j)), big, big, big],
        out_specs=[big, big, big, big], out_shape=[jax.ShapeDtypeStruct(w.shape, F32)] * 4,
        scratch_shapes=[], args=(c_all, dmod_cols, w, m, v), exchange=exchange)


def _load_weights(first, pairs, sems):
    @pl.when(first)
    def _():
        cps = [pltpu.make_async_copy(src, dst, sems.at[k]) for k, (src, dst) in enumerate(pairs)]
        for cp in cps:
            cp.start()
        for cp in cps:
            cp.wait()


def _ada_norm(xv, g, sc, sh):
    r = lax.rsqrt(jnp.mean(xv * xv, axis=-1, keepdims=True) + EPS)
    xn = xv * r
    return (xn * g) * (1.0 + sc) + sh, xn, r


def _ada_norm_bwd(dh, xn, r, g, sc):
    d_sh = _colsum(dh)
    d_sc = _colsum(dh * (xn * g))
    dxg = dh * (1.0 + sc)
    d_g = _colsum(dxg * xn)
    gd = dxg * g
    dx = r * (gd - xn * jnp.mean(gd * xn, axis=-1, keepdims=True))
    return dx, d_sh, d_sc, d_g


def _gated_residual_bwd(dxo, o, g_post, gt):
    r = lax.rsqrt(jnp.mean(o * o, axis=-1, keepdims=True) + EPS)
    on = o * r
    d_gt = _colsum(dxo * (on * g_post))
    dy = dxo * (1.0 + gt)
    d_gp = _colsum(dy * on)
    gd = dy * g_post
    do = r * (gd - on * jnp.mean(gd * on, axis=-1, keepdims=True))
    return do, d_gt, d_gp


def _seq_positions(i, tm, width):
    return i * tm + lax.broadcasted_iota(jnp.int32, (tm, width), 0)


def _fill_phases(ext, phases):
    rows = ext.shape[0]
    ev = ext[...]
    for r in range(1, SUBLANES):
        phases[r - 1] = pltpu.roll(ev, rows - r, axis=0)


def _shifted_rows(ext, phases, offset, n):
    q, r = divmod(offset, SUBLANES)
    if r == 0:
        return ext[pl.ds(q * SUBLANES, n), :]
    return phases[r - 1, pl.ds(q * SUBLANES, n), :]


def _rows_before(halo, cur, shift):
    e = jnp.concatenate([halo, cur], axis=0)
    return pltpu.roll(e, shift, axis=0)[halo.shape[0]:, :]


def _rows_after(cur, halo, shift):
    e = jnp.concatenate([cur, halo], axis=0)
    return pltpu.roll(e, e.shape[0] - shift, axis=0)[:cur.shape[0], :]


def _mixer_forward(x, mod, vec_d, vec_c, cw, pw, win_g, wout_g, taps, tm, name, exchange=None):
    nb, s, d = x.shape
    n = s // tm
    nj, _, dcol = win_g.shape
    din = nj * dcol
    dc = vec_c.shape[-1]
    dpool = din - 2 * dc
    dmix = dc + dpool
    ro = wout_g.shape[1]
    ngrp = dpool // LANES

    def body(x_ref, mod_ref, vd_ref, vc_ref, cw_ref, pw_ref, win_hbm, wout_hbm,
             xo_ref, h_ref, u_ref, ac_ref, dp_ref, z_ref, o_ref,
             win_v, wout_v, ext_a, ext_p, phases, sems):
        b, i = pl.program_id(0), pl.program_id(1)
        pairs = [(win_hbm.at[j], win_v.at[:, pl.ds(j * dcol, dcol)]) for j in range(nj)]
        pairs += [(wout_hbm.at[j], wout_v.at[pl.ds(j * ro, ro), :]) for j in range(nj)]
        _load_weights((b == 0) & (i == 0), pairs, sems)

        xv = x_ref[...]
        h, _, _ = _ada_norm(xv, vd_ref[0:1, :], mod_ref[1:2, :], mod_ref[0:1, :])
        hb = h.astype(BF16)
        h_ref[...] = hb
        u = jnp.dot(hb, win_v[...], preferred_element_type=F32)
        u_ref[...] = u.astype(BF16)
        ag = u[:, :dc] * _sigmoid(u[:, dc:2 * dc])
        up = u[:, 2 * dc:]

        @pl.when(i == 0)
        def _():
            ext_a[0:HALO, :] = jnp.zeros((HALO, dc), F32)
            ext_p[0:HALO, :] = jnp.zeros((HALO, dpool), F32)

        @pl.when(i > 0)
        def _():
            ext_a[0:HALO, :] = ext_a[tm:tm + HALO, :]
            ext_p[0:HALO, :] = ext_p[tm:tm + HALO, :]

        ext_a[HALO:HALO + tm, :] = ag
        ext_p[HALO:HALO + tm, :] = up

        acc = jnp.broadcast_to(vc_ref[0:1, :], (tm, dc))
        _fill_phases(ext_a, phases)
        for k in range(taps):
            acc = acc + cw_ref[k:k + 1, :] * _shifted_rows(ext_a, phases, HALO - (taps - 1) + k, tm)
        ac_ref[...] = acc.astype(BF16)
        mu = jnp.mean(acc, axis=-1, keepdims=True)
        xc = acc - mu
        var = jnp.mean(xc * xc, axis=-1, keepdims=True)
        al = (xc * lax.rsqrt(var + EPS)) * vc_ref[1:2, :] + vc_ref[2:3, :]
        a = al * _sigmoid(al)

        pos = _seq_positions(i, tm, LANES)
        parts = [a.astype(BF16)]
        for g in range(ngrp):
            w = POOL_WINDOWS[g]
            cols = slice(g * LANES, (g + 1) * LANES)
            sw = ext_p[:, cols]
            step = 1
            while step < w:
                sw = sw + pltpu.roll(sw, step, axis=0)
                step *= 2
            cnt = jnp.minimum(pos + 1, w).astype(F32)
            dg = (sw[HALO:, :] / cnt - up[:, cols]).astype(BF16)
            dp_ref[:, cols] = dg
            q = jnp.dot(dg, pw_ref[g], preferred_element_type=F32)
            parts.append((q * vc_ref[3:4, cols]).astype(BF16))
        z = jnp.concatenate(parts, axis=-1)
        z_ref[...] = z
        o = jnp.dot(z, wout_v[...], preferred_element_type=F32)
        o_ref[...] = o
        r2 = lax.rsqrt(jnp.mean(o * o, axis=-1, keepdims=True) + EPS)
        xo_ref[...] = xv + (1.0 + mod_ref[2:3, :]) * ((o * r2) * vd_ref[1:2, :])

    def tile(width):
        return pl.BlockSpec((None, tm, width), lambda b, i: (b, i, 0))

    return _grid_call(
        body, name, (nb, n),
        in_specs=[tile(d), pl.BlockSpec((None, 8, d), lambda b, i: (b, 0, 0)), _full(vec_d.shape), _full(vec_c.shape),
                  _full(cw.shape), _full(pw.shape), _ANY, _ANY],
        out_specs=[tile(d), tile(d), tile(din), tile(dc), tile(dpool), tile(dmix), tile(d)],
        out_shape=[jax.ShapeDtypeStruct((nb, s, d), F32), jax.ShapeDtypeStruct((nb, s, d), BF16),
                   jax.ShapeDtypeStruct((nb, s, din), BF16), jax.ShapeDtypeStruct((nb, s, dc), BF16),
                   jax.ShapeDtypeStruct((nb, s, dpool), BF16), jax.ShapeDtypeStruct((nb, s, dmix), BF16),
                   jax.ShapeDtypeStruct((nb, s, d), F32)],
        scratch_shapes=[pltpu.VMEM((d, din), BF16), pltpu.VMEM((dmix, d), BF16),
                        pltpu.VMEM((HALO + tm, dc), F32), pltpu.VMEM((HALO + tm, dpool), F32),
                        pltpu.VMEM((SUBLANES - 1, HALO + tm, dc), F32), pltpu.SemaphoreType.DMA((2 * nj,))],
        args=(x, mod, vec_d, vec_c, cw, pw, win_g, wout_g), exchange=exchange)


def _mixer_backward(dxo, x, o, u, ac, dpl, mod, vec_d, vec_c, cw, pw, win_g, wout_g, taps, tm, name, exchange=None):
    nb, s, d = x.shape
    n = s // tm
    nj, _, dcol = win_g.shape
    din = nj * dcol
    dc = vec_c.shape[-1]
    dpool = din - 2 * dc
    dmix = dc + dpool
    ro = wout_g.shape[1]
    ngrp = dpool // LANES
    rext = tm + HALO

    def body(dxo_ref, x_ref, o_ref, u_ref, ac_ref, dp_ref, mod_ref, vd_ref, vc_ref, cw_ref, pw_ref, win_hbm, wout_hbm,
             dx_ref, du_ref, dob_ref, rowd_ref, rowb_ref, rowc_ref, dcw_ref, dpw_ref,
             win_v, wout_v, ext_a, ext_p, phases, sems):
        b, i = pl.program_id(0), pl.program_id(1)
        first = (b == 0) & (i == 0)
        pairs = [(win_hbm.at[j], win_v.at[:, pl.ds(j * dcol, dcol)]) for j in range(nj)]
        pairs += [(wout_hbm.at[j], wout_v.at[pl.ds(j * ro, ro), :]) for j in range(nj)]
        _load_weights(first, pairs, sems)

        @pl.when(first)
        def _():
            rowd_ref[...] = jnp.zeros_like(rowd_ref)
            rowc_ref[...] = jnp.zeros_like(rowc_ref)
            dcw_ref[...] = jnp.zeros_like(dcw_ref)
            dpw_ref[...] = jnp.zeros_like(dpw_ref)

        @pl.when(i == 0)
        def _():
            rowb_ref[...] = jnp.zeros_like(rowb_ref)
            ext_a[tm:rext, :] = jnp.zeros((HALO, dc), F32)
            ext_p[tm:rext, :] = jnp.zeros((HALO, dpool), F32)

        @pl.when(i > 0)
        def _():
            ext_a[tm:rext, :] = ext_a[0:HALO, :]
            ext_p[tm:rext, :] = ext_p[0:HALO, :]

        g_pre, g_post = vd_ref[0:1, :], vd_ref[1:2, :]
        sh, sc, gt = mod_ref[0:1, :], mod_ref[1:2, :], mod_ref[2:3, :]
        do, d_gt, d_gp = _gated_residual_bwd(dxo_ref[...], o_ref[...], g_post, gt)
        dob = do.astype(BF16)
        dob_ref[...] = dob
        dz = lax.dot_general(dob, wout_v[...], (((1,), (1,)), ((), ())), preferred_element_type=F32)

        acv = ac_ref[...].astype(F32)
        mu = jnp.mean(acv, axis=-1, keepdims=True)
        xc = acv - mu
        rstd = lax.rsqrt(jnp.mean(xc * xc, axis=-1, keepdims=True) + EPS)
        an = xc * rstd
        lg = vc_ref[1:2, :]
        al = an * lg + vc_ref[2:3, :]
        sg = _sigmoid(al)
        dal = dz[:, :dc] * (sg * (1.0 + al * (1.0 - sg)))
        d_lg = _colsum(dal * an)
        d_lb = _colsum(dal)
        dan = dal * lg
        dac = rstd * (dan - jnp.mean(dan, axis=-1, keepdims=True) - an * jnp.mean(dan * an, axis=-1, keepdims=True))
        d_cb = _colsum(dac)
        ext_a[0:tm, :] = dac
        uv = u_ref[:, 0:dc].astype(F32)
        sgu = _sigmoid(u_ref[:, dc:2 * dc].astype(F32))
        ag = uv * sgu
        dag = jnp.zeros((tm, dc), F32)
        _fill_phases(ext_a, phases)
        for k in range(taps):
            sl = _shifted_rows(ext_a, phases, taps - 1 - k, tm)
            dag = dag + cw_ref[k:k + 1, :] * sl
            dcw_ref[k:k + 1, :] += _colsum(ag * sl)
        du_ref[:, 0:dc] = (dag * sgu).astype(BF16)
        du_ref[:, dc:2 * dc] = (dag * uv * (sgu * (1.0 - sgu))).astype(BF16)

        pos = _seq_positions(n - 1 - i, tm, LANES)
        d_ps = []
        for g in range(ngrp):
            w = POOL_WINDOWS[g]
            cols = slice(g * LANES, (g + 1) * LANES)
            gcols = slice(dc + g * LANES, dc + (g + 1) * LANES)
            dgb = dp_ref[:, cols]
            q = jnp.dot(dgb, pw_ref[g], preferred_element_type=F32)
            dpg = dz[:, gcols]
            d_ps.append(_colsum(dpg * q))
            dq = (dpg * vc_ref[3:4, cols]).astype(BF16)
            dpw_ref[g] += lax.dot_general(dgb, dq, (((0,), (0,)), ((), ())), preferred_element_type=F32)
            dd = lax.dot_general(dq, pw_ref[g], (((1,), (1,)), ((), ())), preferred_element_type=F32)
            cnt = jnp.minimum(pos + 1, w).astype(F32)
            ext_p[0:tm, cols] = dd / cnt
            sw = ext_p[:, cols]
            step = 1
            while step < w:
                sw = sw + pltpu.roll(sw, rext - step, axis=0)
                step *= 2
            du_ref[:, 2 * dc + g * LANES:2 * dc + (g + 1) * LANES] = (sw[0:tm, :] - dd).astype(BF16)
        rowc_ref[0:1, :] += d_cb
        rowc_ref[1:2, :] += d_lg
        rowc_ref[2:3, :] += d_lb
        rowc_ref[3:4, :] += jnp.concatenate(d_ps, axis=-1)

        dh = lax.dot_general(du_ref[...], win_v[...], (((1,), (1,)), ((), ())), preferred_element_type=F32)
        _, xn, r1 = _ada_norm(x_ref[...], g_pre, sc, sh)
        dxb, d_sh, d_sc, d_g = _ada_norm_bwd(dh, xn, r1, g_pre, sc)
        dx_ref[...] = dxo_ref[...] + dxb
        rowd_ref[0:1, :] += d_g
        rowd_ref[1:2, :] += d_gp
        rowb_ref[0:1, :] += d_sh
        rowb_ref[1:2, :] += d_sc
        rowb_ref[2:3, :] += d_gt

    def tile(width):
        return pl.BlockSpec((None, tm, width), lambda b, i: (b, n - 1 - i, 0))

    return _grid_call(
        body, name, (nb, n),
        in_specs=[tile(d), tile(d), tile(d), tile(din), tile(dc), tile(dpool),
                  pl.BlockSpec((None, 8, d), lambda b, i: (b, 0, 0)), _full(vec_d.shape), _full(vec_c.shape),
                  _full(cw.shape), _full(pw.shape), _ANY, _ANY],
        out_specs=[tile(d), tile(din), tile(d), _full((8, d)), pl.BlockSpec((None, 8, d), lambda b, i: (b, 0, 0)),
                   _full((8, dc)), _full((HALO, dc)), _full(pw.shape)],
        out_shape=[jax.ShapeDtypeStruct((nb, s, d), F32), jax.ShapeDtypeStruct((nb, s, din), BF16),
                   jax.ShapeDtypeStruct((nb, s, d), BF16), jax.ShapeDtypeStruct((8, d), F32),
                   jax.ShapeDtypeStruct((nb, 8, d), F32), jax.ShapeDtypeStruct((8, dc), F32),
                   jax.ShapeDtypeStruct((HALO, dc), F32), jax.ShapeDtypeStruct(pw.shape, F32)],
        scratch_shapes=[pltpu.VMEM((d, din), BF16), pltpu.VMEM((dmix, d), BF16),
                        pltpu.VMEM((rext, dc), F32), pltpu.VMEM((rext, dpool), F32),
                        pltpu.VMEM((SUBLANES - 1, rext, dc), F32), pltpu.SemaphoreType.DMA((2 * nj,))],
        args=(dxo, x, o, u, ac, dpl, mod, vec_d, vec_c, cw, pw, win_g, wout_g), exchange=exchange)


def _ffn_forward(x, mod, vec_d, fw, wup_g, wdn_g, tm, name, exchange=None, target=None):
    nb, s, d = x.shape
    n = s // tm
    nj, _, ucol = wup_g.shape
    f2 = nj * ucol
    dff = f2 // 2
    rd = wdn_g.shape[1]
    nq = nj // 2
    cs = dff // nq

    with_loss = target is not None

    def body(*refs):
        refs = list(refs)
        x_ref, mod_ref, vd_ref, fw_ref = refs[:4]
        t_ref = refs.pop(4) if with_loss else None
        wup_hbm, wdn_hbm, xo_ref, h_ref, u_ref, uc_ref, hid_ref, o_ref = refs[4:12]
        sq_ref = refs.pop(12) if with_loss else None
        wup_v, wdn_v, prev_u, sems = refs[12:]
        b, i = pl.program_id(0), pl.program_id(1)
        pairs = [(wup_hbm.at[j], wup_v.at[:, pl.ds(j * ucol, ucol)]) for j in range(nj)]
        pairs += [(wdn_hbm.at[j], wdn_v.at[pl.ds(j * rd, rd), :]) for j in range(nj)]
        _load_weights((b == 0) & (i == 0), pairs, sems)

        if with_loss:
            @pl.when((b == 0) & (i == 0))
            def _():
                sq_ref[...] = jnp.zeros_like(sq_ref)

        @pl.when(i == 0)
        def _():
            prev_u[...] = jnp.zeros_like(prev_u)

        xv = x_ref[...]
        h, _, _ = _ada_norm(xv, vd_ref[2:3, :], mod_ref[4:5, :], mod_ref[3:4, :])
        hb = h.astype(BF16)
        h_ref[...] = hb

        def conv(cols):
            uc = jnp.dot(hb, wup_v[:, cols], preferred_element_type=F32)
            u_ref[:, cols] = uc.astype(BF16)
            before = prev_u[:, cols]
            prev_u[:, cols] = uc[tm - FHALO:, :]
            out = (fw_ref[3:4, cols] + fw_ref[2:3, cols] * uc + fw_ref[1:2, cols] * _rows_before(before, uc, 1)
                   + fw_ref[0:1, cols] * _rows_before(before, uc, 2))
            uc_ref[:, cols] = out.astype(BF16)
            return out

        o = jnp.zeros((tm, d), F32)
        for q in range(nq):
            val = conv(pl.ds(q * cs, cs))
            gate = conv(pl.ds(dff + q * cs, cs))
            hid = ((gate * _sigmoid(gate)) * val).astype(BF16)
            hid_ref[:, pl.ds(q * cs, cs)] = hid
            o = o + jnp.dot(hid, wdn_v[pl.ds(q * cs, cs), :], preferred_element_type=F32)
        o_ref[...] = o
        r2 = lax.rsqrt(jnp.mean(o * o, axis=-1, keepdims=True) + EPS)
        y = xv + (1.0 + mod_ref[5:6, :]) * ((o * r2) * vd_ref[3:4, :])
        if with_loss:
            e = y - t_ref[...]
            xo_ref[...] = e * (1.0 / d)
            sq_ref[0:1, :] += _colsum(e * e)
        else:
            xo_ref[...] = y

    def tile(width):
        return pl.BlockSpec((None, tm, width), lambda b, i: (b, i, 0))

    loss_in = [tile(d)] if with_loss else []
    return _grid_call(
        body, name, (nb, n),
        in_specs=[tile(d), pl.BlockSpec((None, 8, d), lambda b, i: (b, 0, 0)), _full(vec_d.shape), _full(fw.shape)]
        + loss_in + [_ANY, _ANY],
        out_specs=[tile(d), tile(d), tile(f2), tile(f2), tile(dff), tile(d)] + ([_full((8, d))] if with_loss else []),
        out_shape=[jax.ShapeDtypeStruct((nb, s, d), F32), jax.ShapeDtypeStruct((nb, s, d), BF16),
                   jax.ShapeDtypeStruct((nb, s, f2), BF16), jax.ShapeDtypeStruct((nb, s, f2), BF16),
                   jax.ShapeDtypeStruct((nb, s, dff), BF16), jax.ShapeDtypeStruct((nb, s, d), F32)]
        + ([jax.ShapeDtypeStruct((8, d), F32)] if with_loss else []),
        scratch_shapes=[pltpu.VMEM((d, f2), BF16), pltpu.VMEM((dff, d), BF16),
                        pltpu.VMEM((FHALO, f2), F32), pltpu.SemaphoreType.DMA((2 * nj,))],
        args=(x, mod, vec_d, fw) + ((target,) if with_loss else ()) + (wup_g, wdn_g), exchange=exchange)


def _ffn_backward(dxo, x, o, u, uc, mod, vec_d, fw, wup_g, wdn_g, tm, name, exchange=None):
    nb, s, d = x.shape
    n = s // tm
    nj, _, ucol = wup_g.shape
    f2 = nj * ucol
    dff = f2 // 2
    rd = wdn_g.shape[1]
    nq = nj // 2
    cs = dff // nq

    def body(dxo_ref, x_ref, o_ref, u_ref, uc_ref, mod_ref, vd_ref, fw_ref, wup_hbm, wdn_hbm,
             dx_ref, du_ref, dob_ref, rowd_ref, rowb_ref, dfw_ref,
             wup_v, wdn_v, next_d, sems):
        b, i = pl.program_id(0), pl.program_id(1)
        first = (b == 0) & (i == 0)
        pairs = [(wup_hbm.at[j], wup_v.at[:, pl.ds(j * ucol, ucol)]) for j in range(nj)]
        pairs += [(wdn_hbm.at[j], wdn_v.at[pl.ds(j * rd, rd), :]) for j in range(nj)]
        _load_weights(first, pairs, sems)

        @pl.when(first)
        def _():
            rowd_ref[...] = jnp.zeros_like(rowd_ref)
            dfw_ref[...] = jnp.zeros_like(dfw_ref)

        @pl.when(i == 0)
        def _():
            rowb_ref[...] = jnp.zeros_like(rowb_ref)
            next_d[...] = jnp.zeros_like(next_d)

        g_pre, g_post = vd_ref[2:3, :], vd_ref[3:4, :]
        sh, sc, gt = mod_ref[3:4, :], mod_ref[4:5, :], mod_ref[5:6, :]
        do, d_gt, d_gp = _gated_residual_bwd(dxo_ref[...], o_ref[...], g_post, gt)
        dob = do.astype(BF16)
        dob_ref[...] = dob

        def conv_bwd(cols, duc):
            uc = u_ref[:, cols].astype(F32)
            after = next_d[:, cols]
            next_d[:, cols] = duc[0:FHALO, :]
            d1 = _rows_after(duc, after, 1)
            d2 = _rows_after(duc, after, 2)
            dfw_ref[3:4, cols] += _colsum(duc)
            dfw_ref[2:3, cols] += _colsum(uc * duc)
            dfw_ref[1:2, cols] += _colsum(uc * d1)
            dfw_ref[0:1, cols] += _colsum(uc * d2)
            ob = (fw_ref[2:3, cols] * duc + fw_ref[1:2, cols] * d1 + fw_ref[0:1, cols] * d2).astype(BF16)
            du_ref[:, cols] = ob
            return lax.dot_general(ob, wup_v[:, cols], (((1,), (1,)), ((), ())), preferred_element_type=F32)

        dh = jnp.zeros((tm, d), F32)
        for q in range(nq):
            vcols = pl.ds(q * cs, cs)
            gcols = pl.ds(dff + q * cs, cs)
            dhid = lax.dot_general(dob, wdn_v[vcols, :], (((1,), (1,)), ((), ())), preferred_element_type=F32)
            val = uc_ref[:, vcols].astype(F32)
            gate = uc_ref[:, gcols].astype(F32)
            sg = _sigmoid(gate)
            act = gate * sg
            dval = dhid * act
            dgate = (dhid * val) * (sg + act * (1.0 - sg))
            dh = dh + conv_bwd(vcols, dval)
            dh = dh + conv_bwd(gcols, dgate)

        _, xn, r1 = _ada_norm(x_ref[...], g_pre, sc, sh)
        dxb, d_sh, d_sc, d_g = _ada_norm_bwd(dh, xn, r1, g_pre, sc)
        dx_ref[...] = dxo_ref[...] + dxb
        rowd_ref[2:3, :] += d_g
        rowd_ref[3:4, :] += d_gp
        rowb_ref[3:4, :] += d_sh
        rowb_ref[4:5, :] += d_sc
        rowb_ref[5:6, :] += d_gt

    def tile(width):
        return pl.BlockSpec((None, tm, width), lambda b, i: (b, n - 1 - i, 0))

    return _grid_call(
        body, name, (nb, n),
        in_specs=[tile(d), tile(d), tile(d), tile(f2), tile(f2), pl.BlockSpec((None, 8, d), lambda b, i: (b, 0, 0)),
                  _full(vec_d.shape), _full(fw.shape), _ANY, _ANY],
        out_specs=[tile(d), tile(f2), tile(d), _full((8, d)), pl.BlockSpec((None, 8, d), lambda b, i: (b, 0, 0)),
                   _full(fw.shape)],
        out_shape=[jax.ShapeDtypeStruct((nb, s, d), F32), jax.ShapeDtypeStruct((nb, s, f2), BF16),
                   jax.ShapeDtypeStruct((nb, s, d), BF16), jax.ShapeDtypeStruct((8, d), F32),
                   jax.ShapeDtypeStruct((nb, 8, d), F32), jax.ShapeDtypeStruct(fw.shape, F32)],
        scratch_shapes=[pltpu.VMEM((d, f2), BF16), pltpu.VMEM((dff, d), BF16),
                        pltpu.VMEM((FHALO, f2), F32), pltpu.SemaphoreType.DMA((2 * nj,))],
        args=(dxo, x, o, u, uc, mod, vec_d, fw, wup_g, wdn_g), exchange=exchange)


def _weight_grad(a, b, nblk, split, tt, name, exchange=None):
    t, ka = a.shape
    nb_ = b.shape[1]
    nk = t // tt
    if split == "cols":
        wa, wb, grid = ka, nb_ // nblk, (1, nk)
        a_spec = pl.BlockSpec((tt, ka), lambda j, k: (k, 0))
        b_spec = pl.BlockSpec((tt, nb_), lambda j, k: (k, 0))
        o_spec = pl.BlockSpec((nblk, wa, wb), lambda j, k: (0, 0, 0))
        acc_shape = (ka, nb_)
    elif split == "b":
        wa, wb, grid = ka, nb_ // nblk, (nblk, nk)
        a_spec = pl.BlockSpec((tt, wa), lambda j, k: (k, 0))
        b_spec = pl.BlockSpec((tt, wb), lambda j, k: (k, j))
        o_spec = pl.BlockSpec((None, wa, wb), lambda j, k: (j, 0, 0))
        acc_shape = (wa, wb)
    else:
        wa, wb, grid = ka // nblk, nb_, (nblk, nk)
        a_spec = pl.BlockSpec((tt, wa), lambda j, k: (k, j))
        b_spec = pl.BlockSpec((tt, wb), lambda j, k: (k, 0))
        o_spec = pl.BlockSpec((None, wa, wb), lambda j, k: (j, 0, 0))
        acc_shape = (wa, wb)

    def body(a_ref, b_ref, o_ref, acc):
        k = pl.program_id(1)
        prod = lax.dot_general(a_ref[...], b_ref[...], (((0,), (0,)), ((), ())), preferred_element_type=F32)

        @pl.when(k == 0)
        def _():
            acc[...] = prod

        @pl.when(k > 0)
        def _():
            acc[...] += prod

        @pl.when(k == nk - 1)
        def _():
            if split == "cols":
                for j in range(nblk):
                    o_ref[j] = acc[:, j * wb:(j + 1) * wb].astype(o_ref.dtype)
            else:
                o_ref[...] = acc[...].astype(o_ref.dtype)

    outs, exo = _grid_call(body, name, grid, in_specs=[a_spec, b_spec], out_specs=[o_spec],
                           out_shape=[jax.ShapeDtypeStruct((nblk, wa, wb), BF16)],
                           scratch_shapes=[pltpu.VMEM(acc_shape, F32)], args=(a, b), exchange=exchange)
    return outs[0], exo


def _rows128(a):
    return a.reshape(-1, LANES)


class _ReduceScatter:
    def __init__(self, gs, cidx, idx, tag):
        self.gs, self.cidx, self.idx, self.tag = gs, cidx, idx, tag

    def swap(self):
        return _swap_halves(self.gs)

    def after_swap(self, r1):
        self.ps = _pair_sums(self.gs, r1, self.cidx, name=f"rs_pair_{self.tag}")

    def chips(self):
        return _chip_exchange(self.ps)

    def after_chips(self, r2):
        self.fh = _chip_sums(self.ps, r2, self.idx, name=f"rs_sum_{self.tag}")

    def share(self):
        return _sibling_share(self.fh)

    @staticmethod
    def result(fs):
        return [f.reshape(f.shape[0] * f.shape[1], f.shape[2]) for f in fs]


def kernel(x, c, ada_w, ada_b, pre_mix_g, post_mix_g, w_in, conv_w, conv_b, conv_ln_g, conv_ln_b, pool_w, pool_scale, w_out, pre_ffn_g, post_ffn_g, ffn_up, ffn_conv_w, ffn_conv_b, ffn_down, loss_target, m_ada_w, m_ada_b, m_pre_mix_g, m_post_mix_g, m_w_in, m_conv_w, m_conv_b, m_conv_ln_g, m_conv_ln_b, m_pool_w, m_pool_scale, m_w_out, m_pre_ffn_g, m_post_ffn_g, m_ffn_up, m_ffn_conv_w, m_ffn_conv_b, m_ffn_down, v_ada_w, v_ada_b, v_pre_mix_g, v_post_mix_g, v_w_in, v_conv_w, v_conv_b, v_conv_ln_g, v_conv_ln_b, v_pool_w, v_pool_scale, v_w_out, v_pre_ffn_g, v_post_ffn_g, v_ffn_up, v_ffn_conv_w, v_ffn_conv_b, v_ffn_down):
    nb, s, d = x.shape
    nl = w_in.shape[0]
    taps = conv_w.shape[1]
    ccol = conv_w.shape[2]
    dc = conv_b.shape[1]
    fcol = ffn_conv_w.shape[2]
    f2 = ffn_conv_b.shape[1]
    nmod = ada_b.shape[1] // d
    acol = ada_w.shape[2]
    tm = min(MLP_TILE_ROWS, s)
    tm_mix = min(MIXER_TILE_ROWS, s)
    tt = min(GRAD_CHUNK_ROWS, (nb * s) // 2)

    xi, yi, ci = _pos()
    jm = 2 * xi + yi
    cidx = jnp.reshape(ci, (1,)).astype(jnp.int32)
    idx = jnp.stack([jm, ci]).astype(jnp.int32)

    win_b, wout_b, wup_b, wdn_b = (w.astype(BF16) for w in (w_in, w_out, ffn_up, ffn_down))

    def others(l):
        return [win_b[l], wout_b[l], wdn_b[l]]

    n_cw, n_fw, n_c = nl * taps * ccol, nl * 3 * fcol, nb * d
    packed = jnp.concatenate([conv_w.reshape(-1), ffn_conv_w.reshape(-1), c.reshape(-1)])
    got, mixer_weights = _gather8(_rows128(packed), name="gather_small", exchange=_gather([win_b[0], wout_b[0]]))
    got = got.reshape(N_DEV, -1)
    chips = got[0::2]
    cw_full = chips[:, :n_cw].reshape(N_CHIPS, nl, taps, ccol).transpose(1, 2, 0, 3).reshape(nl, taps, dc)
    fw_full = chips[:, n_cw:n_cw + n_fw].reshape(N_CHIPS, nl, 3, fcol).transpose(1, 2, 0, 3).reshape(nl, 3, f2)
    c_all = got[:, n_cw + n_fw:].reshape(N_DEV * nb, d)

    ada_b_cols = lax.dynamic_slice_in_dim(ada_b, jm * acol, acol, axis=1).reshape(nl, 1, acol)
    mod_cols, first_weights = _ada_forward(c_all, ada_w, ada_b_cols, name="ada_forward",
                                           exchange=_gather([wdn_b[0]], mid_at=0.9))
    by_owner = mod_cols.reshape(nl, N_DEV, nb, acol).transpose(1, 0, 2, 3).reshape(N_DEV, -1, LANES)
    mod_own = _rows_to_owners(by_owner, name="mod_to_owners").reshape(N_CHIPS, nl, nb, acol)
    mod_own = mod_own.transpose(1, 2, 0, 3).reshape(nl, nb, nmod, d)
    mod_own = jnp.pad(mod_own, ((0, 0), (0, 0), (0, 8 - nmod), (0, 0)))

    vec_d = jnp.stack([pre_mix_g, post_mix_g, pre_ffn_g, post_ffn_g], axis=1)
    vec_c = jnp.stack([conv_b, conv_ln_g, conv_ln_b, pool_scale], axis=1)
    cw_pad = jnp.pad(cw_full, ((0, 0), (0, HALO - taps), (0, 0)))
    fw_rows = jnp.concatenate([fw_full, ffn_conv_b[:, None, :], jnp.zeros((nl, 4, f2), F32)], axis=1)
    pw_b = pool_w.astype(BF16)

    win_g, wout_g, wdn_g = _whole(list(mixer_weights) + list(first_weights))
    saved = []
    xs = x
    for l in range(nl):
        (x1, h1, u1, ac1, dp1, z1, o1), got = _mixer_forward(
            xs, mod_own[l], vec_d[l], vec_c[l], cw_pad[l], pw_b[l], win_g, wout_g, taps, tm_mix, name=f"mixer_fwd_{l}",
            exchange=_gather([wup_b[l]], mid_at=0.9))
        wup_g, = _whole(got)
        last = l + 1 == nl
        (x2, h2, u2, uc2, hid2, o2, *sq), nxt = _ffn_forward(
            x1, mod_own[l], vec_d[l], fw_rows[l], wup_g, wdn_g, tm, name=f"ffn_fwd_{l}",
            exchange=None if last else _gather(others(l + 1), mid_at=0.6), target=loss_target if last else None)
        saved.append((xs, h1, u1, ac1, dp1, z1, o1, x1, h2, u2, uc2, hid2, o2, win_g, wout_g, wup_g, wdn_g))
        if not last:
            win_g, wout_g, wdn_g = _whole(nxt)
        xs = x2

    dx = xs
    loss = lax.psum(0.5 * jnp.sum(sq[0]) / d, ("x", "y", "c"))

    flat = lambda a: a.reshape(nb * s, a.shape[-1])
    small = [None] * nl
    big_mlp, big_mix = [None] * nl, [None] * nl
    mlp = mix = None
    for l in reversed(range(nl)):
        x0, h1, u1, ac1, dp1, z1, o1, x1, h2, u2, uc2, hid2, o2, win_g, wout_g, wup_g, wdn_g = saved[l]
        (dx, du2, do2, rowd2, rowb2, dfw), got = _ffn_backward(
            dx, x1, o2, u2, uc2, mod_own[l], vec_d[l], fw_rows[l], wup_g, wdn_g, tm, name=f"ffn_bwd_{l}",
            exchange=_combine([mlp.chips(), mix.swap()]) if mlp else None)
        if mlp:
            mlp.after_chips(got[:2])
            mix.after_swap(got[2:])
        g_up, got = _weight_grad(flat(h2), flat(du2), N_CHIPS, "b", 2 * tt, name=f"grad_ffn_up_{l}",
                                 exchange=_combine([mlp.share(), mix.chips()]) if mlp else None)
        if mlp:
            big_mlp[l + 1] = mlp.result(got[:2])
            mix.after_chips(got[2:])
        g_dn, got = _weight_grad(flat(hid2), flat(do2), 2, "a", 2 * tt, name=f"grad_ffn_down_{l}",
                                 exchange=_swap_halves([g_up]) if l == 0 else None)
        g_dn = g_dn.reshape(N_CHIPS, -1, d)
        mlp_above, mlp = mlp, _ReduceScatter([g_up, g_dn], cidx, idx, f"mlp_{l}")
        if l == 0:
            mlp.after_swap(list(got) + list(_run_exchange(_swap_halves([g_dn]), name="rs_swap_down_0")))
        first = mlp.swap() if l > 0 else mlp.chips()
        (dx, du1, do1, rowd1, rowb1, rowc, dcw, dpw), got = _mixer_backward(
            dx, x0, o1, u1, ac1, dp1, mod_own[l], vec_d[l], vec_c[l], cw_pad[l], pw_b[l], win_g, wout_g, taps, tm_mix,
            name=f"mixer_bwd_{l}", exchange=_combine([first, mix.share()]) if mlp_above else first)
        if l > 0:
            mlp.after_swap(got[:2])
        else:
            mlp.after_chips(got[:2])
        if mlp_above:
            big_mix[l + 1] = mix.result(got[2:])
        g_in, got = _weight_grad(flat(h1), flat(du1), N_CHIPS, "cols", tt, name=f"grad_w_in_{l}",
                                 exchange=mlp.share() if l == 0 else None)
        if l == 0:
            big_mlp[0] = mlp.result(got)
        g_out, _ = _weight_grad(flat(z1), flat(do1), 1, "cols", tt, name=f"grad_w_out_{l}")
        mix = _ReduceScatter([g_in, g_out.reshape(N_CHIPS, -1, d)], cidx, idx, f"mix_{l}")
        small[l] = dict(rowd=rowd1 + rowd2, rowb=rowb1 + rowb2, rowc=rowc, dcw=dcw[:taps], dpw=dpw, dfw=dfw)
    mix.after_swap(_run_exchange(mix.swap(), name="rs_swap_mix_0"))

    dmod_own = jnp.stack([small[l]["rowb"][:, :nmod, :] for l in range(nl)])
    dmod_all, got = _gather8(_rows128(dmod_own), name="gather_dmod", exchange=mix.chips())
    mix.after_chips(got)
    big_mix[0] = mix.result(_run_exchange(mix.share(), name="rs_share_mix_0"))
    dmod_all = dmod_all.reshape(N_DEV, nl, nb, nmod * d)
    dmod_all = dmod_all.transpose(1, 0, 2, 3).reshape(nl, N_DEV * nb, nmod * d)
    dmod_cols = lax.dynamic_slice_in_dim(dmod_all, jm * acol, acol, axis=2)
    (g_ada_w, d_ada_w, nm_ada_w, nv_ada_w), _ = _ada_update(c_all, dmod_cols, ada_w, m_ada_w, v_ada_w, name="ada_update")

    def st(key, row=None):
        return jnp.stack([small[l][key] if row is None else small[l][key][row] for l in range(nl)])

    local = {
        "ada_b": dmod_own.sum(axis=1).reshape(nl, nmod * d),
        "pre_mix_g": st("rowd", 0), "post_mix_g": st("rowd", 1),
        "conv_b": st("rowc", 0), "conv_ln_g": st("rowc", 1), "conv_ln_b": st("rowc", 2),
        "pool_w": st("dpw"), "pool_scale": st("rowc", 3),
        "pre_ffn_g": st("rowd", 2), "post_ffn_g": st("rowd", 3),
        "ffn_conv_b": st("dfw", 3), "conv_w": st("dcw"), "ffn_conv_w": jnp.stack([small[l]["dfw"][:3] for l in range(nl)]),
    }
    names = list(local)
    sizes = [local[k].size for k in names]
    pad = -sum(sizes) % (4 * SUBLANES * LANES)
    packed = jnp.concatenate([local[k].reshape(-1) for k in names] + [jnp.zeros((pad,), F32)])
    summed = _allreduce8(_rows128(packed), name="allreduce_small").reshape(-1)
    grads, off = {}, 0
    for k, sz in zip(names, sizes):
        grads[k] = summed[off:off + sz].reshape(local[k].shape)
        off += sz
    grads["conv_w"] = lax.dynamic_slice_in_dim(grads["conv_w"], jm * ccol, ccol, axis=2)
    grads["ffn_conv_w"] = lax.dynamic_slice_in_dim(grads["ffn_conv_w"], jm * fcol, fcol, axis=2)

    params = dict(ada_b=(ada_b, m_ada_b, v_ada_b), pre_mix_g=(pre_mix_g, m_pre_mix_g, v_pre_mix_g),
                  post_mix_g=(post_mix_g, m_post_mix_g, v_post_mix_g), conv_b=(conv_b, m_conv_b, v_conv_b),
                  conv_ln_g=(conv_ln_g, m_conv_ln_g, v_conv_ln_g), conv_ln_b=(conv_ln_b, m_conv_ln_b, v_conv_ln_b),
                  pool_w=(pool_w, m_pool_w, v_pool_w), pool_scale=(pool_scale, m_pool_scale, v_pool_scale),
                  pre_ffn_g=(pre_ffn_g, m_pre_ffn_g, v_pre_ffn_g), post_ffn_g=(post_ffn_g, m_post_ffn_g, v_post_ffn_g),
                  ffn_conv_b=(ffn_conv_b, m_ffn_conv_b, v_ffn_conv_b), conv_w=(conv_w, m_conv_w, v_conv_w),
                  ffn_conv_w=(ffn_conv_w, m_ffn_conv_w, v_ffn_conv_w))
    pack = lambda i, g=None: _rows128(jnp.concatenate([(grads[k] if g else params[k][i]).reshape(-1) for k in names]))
    sd, sm, sv = _adamw_flat(pack(0), pack(0, True), pack(1), pack(2), name="adamw_small")
    outs = {}
    off = 0
    for k in names:
        shape, sz = params[k][0].shape, params[k][0].size
        outs[k] = (grads[k],) + tuple(a.reshape(-1)[off:off + sz].reshape(shape) for a in (sd, sm, sv))
        off += sz

    outs["ada_w"] = (g_ada_w, d_ada_w, nm_ada_w, nv_ada_w)
    for k, w, m, v, gs in [("w_in", w_in, m_w_in, v_w_in, [big_mix[l][0] for l in range(nl)]),
                           ("w_out", w_out, m_w_out, v_w_out, [big_mix[l][1] for l in range(nl)]),
                           ("ffn_up", ffn_up, m_ffn_up, v_ffn_up, [big_mlp[l][0] for l in range(nl)]),
                           ("ffn_down", ffn_down, m_ffn_down, v_ffn_down, [big_mlp[l][1] for l in range(nl)])]:
        outs[k] = tuple(_adamw_layers(w, m, v, gs, name=f"adamw_{k}"))

    order = ["ada_w", "ada_b", "pre_mix_g", "post_mix_g", "w_in", "conv_w", "conv_b", "conv_ln_g", "conv_ln_b", "pool_w",
             "pool_scale", "w_out", "pre_ffn_g", "post_ffn_g", "ffn_up", "ffn_conv_w", "ffn_conv_b", "ffn_down"]
    return (loss, dx) + tuple(outs[k][i] for i in range(4) for k in order)
```
